```python
import math
import jax, jax.numpy as jnp
from jax import lax
import numpy as np

D_MODEL = 1024
BATCH = 4
SEQ = 4096
DEPTH = 1
DEC_BATCH = 128
DEC_SEQ = 1
PAST_LEN = 16384
PAGE_SIZE = 128

N_HEADS = 16
HEAD_DIM = 64
N_KV_HEADS = 4
GROUP = N_HEADS // N_KV_HEADS
WINDOW = 128
BLOCK = 128
ROPE_THETA = 500000.0
ROPE_DIM = HEAD_DIM // 4
Q_WIDTH = N_HEADS * HEAD_DIM
KV_WIDTH = N_KV_HEADS * HEAD_DIM
POOL_WINDOWS = (2, 4, 8, 16)
POOL_GROUPS = 4
POOL_WIDTH = D_MODEL // 2
POOL_GROUP_DIM = POOL_WIDTH // POOL_GROUPS
POOL_STATE = max(POOL_WINDOWS) - 1
N_MEM = 256
MEM_HEADS = 4
MEM_HEAD_DIM = 64
MEM_WIDTH = MEM_HEADS * MEM_HEAD_DIM
N_BRANCHES = 3
IN_SIZES = (Q_WIDTH, KV_WIDTH, KV_WIDTH, POOL_WIDTH, MEM_WIDTH, N_BRANCHES * D_MODEL)
IN_WIDTH = sum(IN_SIZES)
N_EXPERT_GROUPS = 4
EXPERTS_PER_GROUP = 4
N_EXPERTS = N_EXPERT_GROUPS * EXPERTS_PER_GROUP
TOP_K_INNER = 2
D_EXPERT = 512
ALPHA = (2.0 * DEPTH) ** 0.25
BETA = (8.0 * DEPTH) ** -0.25
LN_EPS = 1e-5

kernel_name = 'hybrid_swa_pool_memxattn_hmoe_step'


def _layer_norm(x, g, b):
    xf = x.astype(jnp.float32)
    mu = jnp.mean(xf, -1, keepdims=True)
    var = jnp.mean(jnp.square(xf - mu), -1, keepdims=True)
    return ((xf - mu) * lax.rsqrt(var + LN_EPS) * g + b).astype(x.dtype)


def _split_in(z):
    idx = [int(i) for i in np.cumsum(IN_SIZES)[:-1]]
    q, k, v, u, cq, gz = jnp.split(z, idx, axis=-1)
    lead = z.shape[:-1]
    return (q.reshape(*lead, N_HEADS, HEAD_DIM),
            k.reshape(*lead, N_KV_HEADS, HEAD_DIM),
            v.reshape(*lead, N_KV_HEADS, HEAD_DIM),
            u,
            cq.reshape(*lead, MEM_HEADS, MEM_HEAD_DIM),
            jax.nn.sigmoid(gz.astype(jnp.float32)).reshape(*lead, N_BRANCHES, D_MODEL))


def _partial_rope(x, pos):
    half = ROPE_DIM // 2
    inv = jnp.power(ROPE_THETA, -jnp.arange(half, dtype=jnp.float32) * (2.0 / ROPE_DIM))
    ang = pos.astype(jnp.float32)[:, None] * inv[None, :]
    cos = jnp.cos(ang)[:, None, :]
    sin = jnp.sin(ang)[:, None, :]
    xr = x[..., :ROPE_DIM].astype(jnp.float32)
    x1, x2 = xr[..., :half], xr[..., half:]
    rot = jnp.concatenate([x1 * cos - x2 * sin, x2 * cos + x1 * sin], axis=-1).astype(x.dtype)
    return jnp.concatenate([rot, x[..., ROPE_DIM:]], axis=-1)


def _gqa_sink_attend(q, k, v, mask, sinks):
    s = jnp.einsum('...qkgd,...skd->...kgqs', q, k, preferred_element_type=jnp.float32) * (HEAD_DIM ** -0.5)
    s = jnp.where(mask, s, -jnp.inf)
    sink = sinks.astype(jnp.float32).reshape(N_KV_HEADS, GROUP)[:, :, None, None]
    m = jnp.maximum(jnp.max(s, axis=-1, keepdims=True), sink)
    p = jnp.exp(s - m)
    p = p / (jnp.sum(p, axis=-1, keepdims=True) + jnp.exp(sink - m))
    return jnp.einsum('...kgqs,...skd->...qkgd', p.astype(v.dtype), v)


def _swa_prompt(q, k, v, sinks):
    B, L = q.shape[0], q.shape[1]
    nb = L // BLOCK
    qb = q.reshape(B, nb, BLOCK, N_KV_HEADS, GROUP, HEAD_DIM)
    kb = k.reshape(B, nb, BLOCK, N_KV_HEADS, HEAD_DIM)
    vb = v.reshape(B, nb, BLOCK, N_KV_HEADS, HEAD_DIM)
    padb = ((0, 0), (1, 0), (0, 0), (0, 0), (0, 0))
    kk = jnp.concatenate([jnp.pad(kb, padb)[:, :-1], kb], axis=2)
    vv = jnp.concatenate([jnp.pad(vb, padb)[:, :-1], vb], axis=2)
    blk = jnp.arange(nb)[:, None, None]
    qpos = blk * BLOCK + jnp.arange(BLOCK)[None, :, None]
    kpos = (blk - 1) * BLOCK + jnp.arange(2 * BLOCK)[None, None, :]
    mask = (kpos <= qpos) & (kpos >= qpos - WINDOW) & (kpos >= 0)
    o = _gqa_sink_attend(qb, kk, vv, mask[None, :, None, None], sinks)
    return o.reshape(B, L, Q_WIDTH)


def _swa_sample(q, k_new, v_new, ck, cv, sinks):
    DB, S = q.shape[0], q.shape[1]
    W = ck.shape[1]
    kk = jnp.concatenate([ck, k_new], axis=1)
    vv = jnp.concatenate([cv, v_new], axis=1)
    kpos = jnp.concatenate([PAST_LEN - W + jnp.arange(W), PAST_LEN + jnp.arange(S)])
    qpos = PAST_LEN + jnp.arange(S)
    mask = (kpos[None, :] <= qpos[:, None]) & (kpos[None, :] >= qpos[:, None] - WINDOW)
    o = _gqa_sink_attend(q.reshape(DB, S, N_KV_HEADS, GROUP, HEAD_DIM), kk, vv, mask, sinks)
    return o.reshape(DB, S, Q_WIDTH), kk, vv


def _multiscale_pool(u_ext, pos, w_mix, scale):
    P = POOL_STATE
    B, L = u_ext.shape[0], u_ext.shape[1] - P
    uf = u_ext.astype(jnp.float32)
    c = jnp.pad(jnp.cumsum(uf, axis=1), ((0, 0), (1, 0), (0, 0)))
    cur = uf[:, P:]
    top = c[:, P + 1:]
    outs = []
    for g, w in enumerate(POOL_WINDOWS):
        sl = slice(g * POOL_GROUP_DIM, (g + 1) * POOL_GROUP_DIM)
        wsum = top[..., sl] - c[:, P + 1 - w:P + 1 - w + L, sl]
        cnt = jnp.minimum(pos + 1, w).astype(jnp.float32)[None, :, None]
        outs.append(wsum / cnt - cur[..., sl])
    pooled = jnp.stack(outs, axis=2)
    mixed = jnp.einsum('blgc,gcd->blgd', pooled, w_mix.astype(jnp.float32))
    return (mixed.reshape(B, L, POOL_WIDTH) * scale).astype(u_ext.dtype)


def _cross_attend(cq, mk, mv):
    B, L = cq.shape[0], cq.shape[1]
    s = jnp.einsum('blhd,bmhd->bhlm', cq, mk, preferred_element_type=jnp.float32) * (MEM_HEAD_DIM ** -0.5)
    p = jax.nn.softmax(s, axis=-1)
    o = jnp.einsum('bhlm,bmhd->blhd', p.astype(mv.dtype), mv)
    return o.reshape(B, L, MEM_WIDTH)


def _hier_moe(h, w_group, b_group, w_router, b_router, w_gate, w_up, w_down):
    hf = h.astype(jnp.float32)
    g_prob = jax.nn.softmax(hf @ w_group.astype(jnp.float32) + b_group, axis=-1)
    g_top_p, g_top = lax.top_k(g_prob, 1)
    e_logits = jnp.einsum('td,dge->tge', hf, w_router.astype(jnp.float32)) + b_router
    e_logits = jnp.take_along_axis(e_logits, g_top[:, :, None], axis=1)[:, 0]
    e_top_v, e_top = lax.top_k(e_logits, TOP_K_INNER)
    wts = jax.nn.softmax(e_top_v, axis=-1) * g_top_p
    eid = g_top * EXPERTS_PER_GROUP + e_top
    combine = jnp.einsum('tk,tke->te', wts, jax.nn.one_hot(eid, N_EXPERTS, dtype=jnp.float32))
    y = jnp.zeros(h.shape, jnp.float32)
    for e in range(N_EXPERTS):
        hid = jax.nn.silu(h @ w_gate[e]) * (h @ w_up[e])
        y = y + combine[:, e:e + 1] * (hid @ w_down[e]).astype(jnp.float32)
    return y.astype(h.dtype)


def _merge_and_ffn(x, o_a, o_b, o_c, gates, wa, wb, wc, wo, g1, b1, g2, b2,
                   w_group, b_group, w_router, b_router, w_gate, w_up, w_down):
    m = (gates[..., 0, :] * (o_a @ wa).astype(jnp.float32)
         + gates[..., 1, :] * (o_b @ wb).astype(jnp.float32)
         + gates[..., 2, :] * (o_c @ wc).astype(jnp.float32)).astype(x.dtype)
    h = _layer_norm(ALPHA * x + m @ wo, g1, b1)
    f = _hier_moe(h.reshape(-1, D_MODEL), w_group, b_group, w_router, b_router,
                  w_gate, w_up, w_down).reshape(h.shape)
    return _layer_norm(ALPHA * h + f, g2, b2)


def setup_inputs(seed: int = 0) -> dict:
    key = jax.random.key(seed)
    ks = jax.random.split(key, 32)

    def nrm(k, shape, s):
        return jax.random.normal(k, shape, jnp.float32) * s

    w_c = min(WINDOW, PAST_LEN)
    return {
        'x_prompt': nrm(ks[0], (BATCH, SEQ, D_MODEL), 1.0),
        'x_sample': nrm(ks[1], (DEC_BATCH, DEC_SEQ, D_MODEL), 1.0),
        'cache_win_k': nrm(ks[2], (DEPTH, DEC_BATCH, w_c, N_KV_HEADS, HEAD_DIM), 1.0),
        'cache_win_v': nrm(ks[3], (DEPTH, DEC_BATCH, w_c, N_KV_HEADS, HEAD_DIM), 1.0),
        'state_pool': nrm(ks[4], (DEPTH, DEC_BATCH, POOL_STATE, POOL_WIDTH), 1.0),
        'cache_mem_k': nrm(ks[5], (DEPTH, DEC_BATCH, N_MEM, MEM_HEADS, MEM_HEAD_DIM), 1.0),
        'cache_mem_v': nrm(ks[6], (DEPTH, DEC_BATCH, N_MEM, MEM_HEADS, MEM_HEAD_DIM), 1.0),
        'mem_prompt': nrm(ks[7], (BATCH, N_MEM, D_MODEL), 1.0),
        'w_in': nrm(ks[8], (DEPTH, D_MODEL, IN_WIDTH), D_MODEL ** -0.5),
        'sinks': nrm(ks[9], (DEPTH, N_HEADS), 0.5),
        'w_pool_mix': nrm(ks[10], (DEPTH, POOL_GROUPS, POOL_GROUP_DIM, POOL_GROUP_DIM), POOL_GROUP_DIM ** -0.5),
        'pool_scale': 1.0 + nrm(ks[11], (DEPTH, POOL_WIDTH), 0.02),
        'w_mem_k': nrm(ks[12], (DEPTH, D_MODEL, MEM_WIDTH), D_MODEL ** -0.5),
        'w_mem_v': nrm(ks[13], (DEPTH, D_MODEL, MEM_WIDTH), D_MODEL ** -0.5),
        'w_branch_a': nrm(ks[14], (DEPTH, Q_WIDTH, D_MODEL), Q_WIDTH ** -0.5),
        'w_branch_b': nrm(ks[15], (DEPTH, POOL_WIDTH, D_MODEL), POOL_WIDTH ** -0.5),
        'w_branch_c': nrm(ks[16], (DEPTH, MEM_WIDTH, D_MODEL), MEM_WIDTH ** -0.5),
        'w_out': nrm(ks[17], (DEPTH, D_MODEL, D_MODEL), BETA * D_MODEL ** -0.5),
        'ln1_g': 1.0 + nrm(ks[18], (DEPTH, D_MODEL), 0.02),
        'ln1_b': nrm(ks[19], (DEPTH, D_MODEL), 0.02),
        'w_group': nrm(ks[20], (DEPTH, D_MODEL, N_EXPERT_GROUPS), D_MODEL ** -0.5),
        'b_group': nrm(ks[21], (DEPTH, N_EXPERT_GROUPS), 0.01),
        'w_router': nrm(ks[22], (DEPTH, D_MODEL, N_EXPERT_GROUPS, EXPERTS_PER_GROUP), D_MODEL ** -0.5),
        'b_router': nrm(ks[23], (DEPTH, N_EXPERT_GROUPS, EXPERTS_PER_GROUP), 0.01),
        'w_gate': nrm(ks[24], (DEPTH, N_EXPERTS, D_MODEL, D_EXPERT), D_MODEL ** -0.5),
        'w_up': nrm(ks[25], (DEPTH, N_EXPERTS, D_MODEL, D_EXPERT), D_MODEL ** -0.5),
        'w_down': nrm(ks[26], (DEPTH, N_EXPERTS, D_EXPERT, D_MODEL), BETA * D_EXPERT ** -0.5),
        'ln2_g': 1.0 + nrm(ks[27], (DEPTH, D_MODEL), 0.02),
        'ln2_b': nrm(ks[28], (DEPTH, D_MODEL), 0.02),
    }


def reference(x_prompt, x_sample, cache_win_k, cache_win_v, state_pool, cache_mem_k, cache_mem_v, mem_prompt,
              w_in, sinks, w_pool_mix, pool_scale, w_mem_k, w_mem_v, w_branch_a, w_branch_b, w_branch_c,
              w_out, ln1_g, ln1_b, w_group, b_group, w_router, b_router, w_gate, w_up, w_down, ln2_g, ln2_b):
    pos_p = jnp.arange(x_prompt.shape[1], dtype=jnp.int32)
    pos_s = PAST_LEN + jnp.arange(x_sample.shape[1], dtype=jnp.int32)
    keep_p = min(WINDOW, x_prompt.shape[1])
    keep_s = cache_win_k.shape[2]
    bp = x_prompt.shape[0]
    xp, xs = x_prompt, x_sample
    nk_p, nv_p, npool_p, nmk_p, nmv_p, nk_s, nv_s, npool_s = ([] for _ in range(8))
    for l in range(DEPTH):
        merge_w = (w_branch_a[l], w_branch_b[l], w_branch_c[l], w_out[l], ln1_g[l], ln1_b[l], ln2_g[l], ln2_b[l],
                   w_group[l], b_group[l], w_router[l], b_router[l], w_gate[l], w_up[l], w_down[l])
        q, k, v, u, cq, gates = _split_in(xp @ w_in[l])
        q = _partial_rope(q, pos_p)
        k = _partial_rope(k, pos_p)
        o_a = _swa_prompt(q, k, v, sinks[l])
        u_ext = jnp.pad(u, ((0, 0), (POOL_STATE, 0), (0, 0)))
        o_b = _multiscale_pool(u_ext, pos_p, w_pool_mix[l], pool_scale[l])
        mk = (mem_prompt @ w_mem_k[l]).reshape(bp, N_MEM, MEM_HEADS, MEM_HEAD_DIM)
        mv = (mem_prompt @ w_mem_v[l]).reshape(bp, N_MEM, MEM_HEADS, MEM_HEAD_DIM)
        o_c = _cross_attend(cq, mk, mv)
        nk_p.append(k[:, k.shape[1] - keep_p:])
        nv_p.append(v[:, v.shape[1] - keep_p:])
        npool_p.append(u_ext[:, u_ext.shape[1] - POOL_STATE:])
        nmk_p.append(mk)
        nmv_p.append(mv)
        xp = _merge_and_ffn(xp, o_a, o_b, o_c, gates, *merge_w)
        q, k, v, u, cq, gates = _split_in(xs @ w_in[l])
        q = _partial_rope(q, pos_s)
        k = _partial_rope(k, pos_s)
        o_a, kk, vv = _swa_sample(q, k, v, cache_win_k[l], cache_win_v[l], sinks[l])
        u_ext = jnp.concatenate([state_pool[l].astype(u.dtype), u], axis=1)
        o_b = _multiscale_pool(u_ext, pos_s, w_pool_mix[l], pool_scale[l])
        o_c = _cross_attend(cq, cache_mem_k[l], cache_mem_v[l])
        nk_s.append(kk[:, kk.shape[1] - keep_s:])
        nv_s.append(vv[:, vv.shape[1] - keep_s:])
        npool_s.append(u_ext[:, u_ext.shape[1] - POOL_STATE:])
        xs = _merge_and_ffn(xs, o_a, o_b, o_c, gates, *merge_w)
    return (xp, xs, jnp.stack(nk_p), jnp.stack(nv_p), jnp.stack(npool_p), jnp.stack(nmk_p), jnp.stack(nmv_p),
            jnp.stack(nk_s), jnp.stack(nv_s), jnp.stack(npool_s))
```

```python
import functools

import jax
import jax.numpy as jnp
import numpy as np
from jax import lax
from jax.experimental import pallas as pl
from jax.experimental.pallas import tpu as pltpu

D_MODEL = 1024
N_HEADS = 16
HEAD_DIM = 64
N_KV = 4
GROUP = N_HEADS // N_KV
WINDOW = 128
ROPE_THETA = 500000.0
ROPE_DIM = HEAD_DIM // 4
Q_WIDTH = N_HEADS * HEAD_DIM
KV_WIDTH = N_KV * HEAD_DIM
POOL_WINDOWS = (2, 4, 8, 16)
POOL_WIDTH = D_MODEL // 2
POOL_GROUP_DIM = POOL_WIDTH // len(POOL_WINDOWS)
POOL_STATE = max(POOL_WINDOWS) - 1
N_MEM = 256
MEM_HEADS = 4
MEM_WIDTH = MEM_HEADS * HEAD_DIM
N_EXPERT_GROUPS = 4
EXPERTS_PER_GROUP = 4
N_EXPERTS = N_EXPERT_GROUPS * EXPERTS_PER_GROUP
D_EXPERT = 512
PAST_LEN = 16384
DEPTH = 1
ALPHA = (2.0 * DEPTH) ** 0.25
LN_EPS = 1e-5

Q0 = 0
K0 = Q0 + Q_WIDTH
V0 = K0 + KV_WIDTH
U0 = V0 + KV_WIDTH
CQ0 = U0 + POOL_WIDTH
GZ0 = CQ0 + MEM_WIDTH
IN_WIDTH = GZ0 + 3 * D_MODEL

LANES = 128
SUBLANES = 8
VMEM_LIMIT = 56 * 1024 * 1024

TM = 256
QB = WINDOW
TG = 256
TD = 256
TC = 256
SB = 8
IN_CHUNK = 768

BF16 = jnp.bfloat16
F32 = jnp.float32
NEG_INF = float("-inf")


def _const_spec(shape):
    nd = len(shape)
    return pl.BlockSpec(shape, lambda *_: (0,) * nd, pipeline_mode=pl.Buffered(1))


def _layer_norm(x, g, b):
    mu = jnp.mean(x, axis=-1, keepdims=True)
    xc = x - mu
    var = jnp.mean(xc * xc, axis=-1, keepdims=True)
    return xc * lax.rsqrt(var + LN_EPS) * g + b


def _dot(a, b):
    return jnp.dot(a, b, preferred_element_type=F32)


def _dot_nt(a, b):
    return lax.dot_general(a, b, (((1,), (1,)), ((), ())), preferred_element_type=F32)


def _lane_block_mask(shape, block, width=HEAD_DIM):
    lane = lax.broadcasted_iota(jnp.int32, shape, len(shape) - 1)
    return (lane >= block * width) & (lane < (block + 1) * width)


def _rope(x, c, s1, s2):
    half = ROPE_DIM // 2
    return x * c + pltpu.roll(x, half, 1) * s1 + pltpu.roll(x, LANES - half, 1) * s2


def _route(logits, row_tri):
    rows = logits.shape[0]
    lane = lax.broadcasted_iota(jnp.int32, (rows, LANES), 1)
    lanef = lane.astype(F32)
    big = float(LANES)
    is_g = lane < N_EXPERT_GROUPS
    glog = jnp.where(is_g, logits, NEG_INF)
    gmax = jnp.max(glog, axis=1, keepdims=True)
    gsum = jnp.sum(jnp.where(is_g, jnp.exp(glog - gmax), 0.0), axis=1, keepdims=True)
    gp = 1.0 / gsum
    gidx = jnp.min(jnp.where(glog == gmax, lanef, big), axis=1, keepdims=True).astype(jnp.int32)
    lo = N_EXPERT_GROUPS + gidx * EXPERTS_PER_GROUP
    in_grp = (lane >= lo) & (lane < lo + EXPERTS_PER_GROUP)
    el = jnp.where(in_grp, logits, NEG_INF)
    v1 = jnp.max(el, axis=1, keepdims=True)
    i1 = jnp.min(jnp.where(el == v1, lanef, big), axis=1, keepdims=True).astype(jnp.int32)
    el2 = jnp.where(lane == i1, NEG_INF, el)
    v2 = jnp.max(el2, axis=1, keepdims=True)
    i2 = jnp.min(jnp.where(el2 == v2, lanef, big), axis=1, keepdims=True).astype(jnp.int32)
    e21 = jnp.exp(v2 - v1)
    inv = 1.0 / (1.0 + e21)
    w1 = inv * gp
    w2 = e21 * inv * gp
    e1 = i1 - N_EXPERT_GROUPS
    e2 = i2 - N_EXPERT_GROUPS
    hot1 = lane == e1
    hot2 = lane == e2
    onehot = jnp.where(hot1 | hot2, 1.0, 0.0)
    counts = jnp.sum(onehot, axis=0, keepdims=True)
    if row_tri is None:
        r1 = jnp.zeros_like(w1)
        r2 = jnp.zeros_like(w1)
    else:
        rank = _dot(row_tri, onehot.astype(BF16))
        r1 = jnp.sum(jnp.where(hot1, rank, 0.0), axis=1, keepdims=True)
        r2 = jnp.sum(jnp.where(hot2, rank, 0.0), axis=1, keepdims=True)
    route = jnp.where(lane == 0, e1.astype(F32),
            jnp.where(lane == 1, e2.astype(F32),
            jnp.where(lane == 2, w1,
            jnp.where(lane == 3, w2,
            jnp.where(lane == 4, r1,
            jnp.where(lane == 5, r2, 0.0))))))
    return route, counts, (hot1, hot2, w1, w2)


def _merge_ln1(x, oa, ob, oc, gz, wa_ref, wb_ref, wc_ref, wo_ref, g1, b1):
    ya = _dot(oa, wa_ref[...])
    yb = _dot(ob.astype(BF16), wb_ref[...])
    yc = _dot(oc.astype(BF16), wc_ref[...])
    m = (jax.nn.sigmoid(gz[:, 0:D_MODEL]) * ya
         + jax.nn.sigmoid(gz[:, D_MODEL:2 * D_MODEL]) * yb
         + jax.nn.sigmoid(gz[:, 2 * D_MODEL:3 * D_MODEL]) * yc)
    hpre = ALPHA * x + _dot(m.astype(BF16), wo_ref[...])
    return _layer_norm(hpre, g1, b1)


def _mem_kernel(mem_ref, wk_ref, wv_ref, mk_ref, mv_ref):
    m = mem_ref[...].astype(BF16)
    mk_ref[...] = _dot(m, wk_ref[...])
    mv_ref[...] = _dot(m, wv_ref[...])


def _mem_project(mem, wk, wv):
    b = mem.shape[0]
    out = jax.ShapeDtypeStruct((b, N_MEM, MEM_WIDTH), F32)
    return pl.pallas_call(
        _mem_kernel,
        grid=(b,),
        in_specs=[pl.BlockSpec((None, N_MEM, D_MODEL), lambda i: (i, 0, 0)),
                  _const_spec((D_MODEL, MEM_WIDTH)), _const_spec((D_MODEL, MEM_WIDTH))],
        out_specs=[pl.BlockSpec((None, N_MEM, MEM_WIDTH), lambda i: (i, 0, 0))] * 2,
        out_shape=[out, out],
        name="mem_project",
    )(mem, wk, wv)


def _front_kernel(sinks_ref, x_ref, rope_ref, win_ref, wa_ref, wb_ref, wc_ref, wo_ref, wmix_ref,
                  pscale_ref, mk_ref, mv_ref, g1_ref, b1_ref, wr_ref, br_ref,
                  h_ref, route_ref, counts_ref, nk_ref, nv_ref, npool_ref,
                  z_ref, qb_ref, kext_ref, vext_ref, uext_ref, oa_ref, ob_ref):
    i = pl.program_id(1)
    x = x_ref[...]
    xb = x.astype(BF16)

    @pl.when(i == 0)
    def _():
        kext_ref[0:QB, :] = jnp.zeros((QB, KV_WIDTH), BF16)
        vext_ref[0:QB, :] = jnp.zeros((QB, KV_WIDTH), BF16)
        uext_ref[0:2 * SUBLANES, :] = jnp.zeros((2 * SUBLANES, POOL_WIDTH), F32)

    @pl.when(i > 0)
    def _():
        kext_ref[0:QB, :] = kext_ref[TM:TM + QB, :]
        vext_ref[0:QB, :] = vext_ref[TM:TM + QB, :]
        uext_ref[0:2 * SUBLANES, :] = uext_ref[TM:TM + 2 * SUBLANES, :]

    for c0 in range(0, IN_WIDTH, IN_CHUNK):
        z_ref[:, c0:c0 + IN_CHUNK] = _dot(xb, win_ref[:, c0:c0 + IN_CHUNK])

    c = rope_ref[0]
    s1 = rope_ref[1]
    s2 = rope_ref[2]
    for j in range(Q_WIDTH // LANES):
        sl = slice(Q0 + j * LANES, Q0 + (j + 1) * LANES)
        qb_ref[:, j * LANES:(j + 1) * LANES] = _rope(z_ref[:, sl], c, s1, s2).astype(BF16)
    for j in range(KV_WIDTH // LANES):
        sl = slice(K0 + j * LANES, K0 + (j + 1) * LANES)
        kr = _rope(z_ref[:, sl], c, s1, s2)
        z_ref[:, sl] = kr
        kext_ref[QB:QB + TM, j * LANES:(j + 1) * LANES] = kr.astype(BF16)
    vext_ref[QB:QB + TM, :] = z_ref[:, V0:V0 + KV_WIDTH].astype(BF16)
    nk_ref[...] = z_ref[TM - QB:TM, K0:K0 + KV_WIDTH]
    nv_ref[...] = z_ref[TM - QB:TM, V0:V0 + KV_WIDTH]

    rowq = lax.broadcasted_iota(jnp.int32, (GROUP * QB, 2 * QB), 0) & (QB - 1)
    colk = lax.broadcasted_iota(jnp.int32, (GROUP * QB, 2 * QB), 1)
    band = (colk >= rowq) & (colk <= rowq + WINDOW)
    rowg = lax.broadcasted_iota(jnp.int32, (GROUP * QB, 1), 0) // QB
    for sb in range(TM // QB):
        k2 = kext_ref[sb * QB:(sb + 2) * QB, :]
        v2 = vext_ref[sb * QB:(sb + 2) * QB, :]
        if sb == 0:
            mask = band & ((colk >= QB) | (i > 0))
        else:
            mask = band
        qs = jnp.concatenate(
            [qb_ref[sb * QB:(sb + 1) * QB, g * KV_WIDTH:(g + 1) * KV_WIDTH] for g in range(GROUP)], axis=0)
        acc = jnp.zeros((GROUP * QB, KV_WIDTH), F32)
        for kv in range(N_KV):
            kmask = _lane_block_mask((2 * QB, KV_WIDTH), kv)
            kkv = jnp.where(kmask, k2, jnp.zeros_like(k2))
            s = jnp.where(mask, _dot_nt(qs, kkv), NEG_INF)
            sink = jnp.zeros((GROUP * QB, 1), F32)
            for g in range(GROUP):
                sink = jnp.where(rowg == g, sinks_ref[kv * GROUP + g], sink)
            m = jnp.maximum(jnp.max(s, axis=1, keepdims=True), sink)
            p = jnp.exp(s - m)
            den = jnp.sum(p, axis=1, keepdims=True) + jnp.exp(sink - m)
            o = _dot((p * (1.0 / den)).astype(BF16), v2)
            acc = jnp.where(_lane_block_mask((GROUP * QB, KV_WIDTH), kv), o, acc)
        for g in range(GROUP):
            oa_ref[sb * QB:(sb + 1) * QB, g * KV_WIDTH:(g + 1) * KV_WIDTH] = acc[g * QB:(g + 1) * QB].astype(BF16)

    hist = 2 * SUBLANES
    uext_ref[hist:hist + TM, :] = z_ref[:, U0:U0 + POOL_WIDTH]
    npool_ref[...] = uext_ref[TM:TM + hist, :]
    pos = i * TM + lax.broadcasted_iota(jnp.int32, (TM, 1), 0)
    for g, w in enumerate(POOL_WINDOWS):
        sl = slice(g * POOL_GROUP_DIM, (g + 1) * POOL_GROUP_DIM)
        cur = uext_ref[hist:hist + TM, sl]
        ws = cur
        for j in range(1, w):
            ws = ws + uext_ref[hist - j:hist - j + TM, sl]
        cnt = jnp.minimum(pos + 1, w).astype(F32)
        pooled = ws / cnt - cur
        ob_ref[:, sl] = _dot(pooled.astype(BF16), wmix_ref[g]) * pscale_ref[:, sl]

    cq = z_ref[:, CQ0:CQ0 + MEM_WIDTH].astype(BF16)
    mk = mk_ref[...].astype(BF16)
    mv = mv_ref[...].astype(BF16)
    oc = jnp.zeros((TM, MEM_WIDTH), F32)
    for hh in range(MEM_HEADS):
        kh = jnp.where(_lane_block_mask((N_MEM, MEM_WIDTH), hh), mk, jnp.zeros_like(mk))
        s = _dot_nt(cq, kh)
        m = jnp.max(s, axis=1, keepdims=True)
        p = jnp.exp(s - m)
        den = jnp.sum(p, axis=1, keepdims=True)
        o = _dot((p * (1.0 / den)).astype(BF16), mv)
        oc = jnp.where(_lane_block_mask((TM, MEM_WIDTH), hh), o, oc)

    h = _merge_ln1(x, oa_ref[...], ob_ref[...], oc, z_ref[:, GZ0:GZ0 + 3 * D_MODEL],
                   wa_ref, wb_ref, wc_ref, wo_ref, g1_ref[...], b1_ref[...])
    h_ref[...] = h
    logits = _dot(h.astype(BF16), wr_ref[...]) + br_ref[...]
    tri = (lax.broadcasted_iota(jnp.int32, (TM, TM), 1)
           < lax.broadcasted_iota(jnp.int32, (TM, TM), 0)).astype(BF16)
    route, counts, _ = _route(logits, tri)
    route_ref[...] = route
    counts_ref[...] = jnp.broadcast_to(counts, (SUBLANES, LANES))


def _front(x, rope, sinks, win, wa, wb, wc, wo, wmix, pscale, mk, mv, g1, b1, wr, br):
    b, l, _ = x.shape
    nt = l // TM
    hist = 2 * SUBLANES
    tile = lambda w: pl.BlockSpec((None, TM, w), lambda bi, ti: (bi, ti, 0))
    per_b = lambda r, w: pl.BlockSpec((None, r, w), lambda bi, ti: (bi, 0, 0))
    return pl.pallas_call(
        _front_kernel,
        grid=(b, nt),
        in_specs=[
            pl.BlockSpec(memory_space=pltpu.SMEM),
            tile(D_MODEL),
            pl.BlockSpec((3, TM, LANES), lambda bi, ti: (0, ti, 0)),
            _const_spec((D_MODEL, IN_WIDTH)),
            _const_spec((Q_WIDTH, D_MODEL)), _const_spec((POOL_WIDTH, D_MODEL)),
            _const_spec((MEM_WIDTH, D_MODEL)), _const_spec((D_MODEL, D_MODEL)),
            _const_spec((len(POOL_WINDOWS), POOL_GROUP_DIM, POOL_GROUP_DIM)),
            _const_spec((1, POOL_WIDTH)),
            per_b(N_MEM, MEM_WIDTH), per_b(N_MEM, MEM_WIDTH),
            _const_spec((1, D_MODEL)), _const_spec((1, D_MODEL)),
            _const_spec((D_MODEL, LANES)), _const_spec((1, LANES)),
        ],
        out_specs=[
            tile(D_MODEL), tile(LANES),
            pl.BlockSpec((None, None, SUBLANES, LANES), lambda bi, ti: (bi, ti, 0, 0)),
            per_b(QB, KV_WIDTH), per_b(QB, KV_WIDTH), per_b(hist, POOL_WIDTH),
        ],
        out_shape=[
            jax.ShapeDtypeStruct((b, l, D_MODEL), F32),
            jax.ShapeDtypeStruct((b, l, LANES), F32),
            jax.ShapeDtypeStruct((b, nt, SUBLANES, LANES), F32),
            jax.ShapeDtypeStruct((b, QB, KV_WIDTH), F32),
            jax.ShapeDtypeStruct((b, QB, KV_WIDTH), F32),
            jax.ShapeDtypeStruct((b, hist, POOL_WIDTH), F32),
        ],
        scratch_shapes=[
            pltpu.VMEM((TM, IN_WIDTH), F32),
            pltpu.VMEM((TM, Q_WIDTH), BF16),
            pltpu.VMEM((QB + TM, KV_WIDTH), BF16),
            pltpu.VMEM((QB + TM, KV_WIDTH), BF16),
            pltpu.VMEM((hist + TM, POOL_WIDTH), F32),
            pltpu.VMEM((TM, Q_WIDTH), BF16),
            pltpu.VMEM((TM, POOL_WIDTH), F32),
        ],
        compiler_params=pltpu.CompilerParams(
            dimension_semantics=("arbitrary", "arbitrary"), vmem_limit_bytes=VMEM_LIMIT),
        name="front_prompt",
    )(sinks, x, rope, win, wa, wb, wc, wo, wmix, pscale, mk, mv, g1, b1, wr, br)


def _row_copy(src, dst, i_src, i_dst, sem):
    return pltpu.make_async_copy(src.at[pl.ds(i_src, 1)], dst.at[pl.ds(i_dst, 1)], sem)


def _dispatch_kernel(pos1_ref, pos2_ref, pad_lo_ref, pad_hi_ref, h_ref, hs_ref, sem):
    s = pl.program_id(0)
    n = pl.num_programs(0)

    def issue(t, carry):
        tok = s * TD + t
        _row_copy(h_ref, hs_ref, tok, pos1_ref[tok], sem.at[s % 2]).start()
        _row_copy(h_ref, hs_ref, tok, pos2_ref[tok], sem.at[s % 2]).start()
        return carry

    lax.fori_loop(0, TD, issue, 0)

    def drain(slot):
        pltpu.make_async_copy(hs_ref.at[pl.ds(0, 2 * TD)], hs_ref.at[pl.ds(0, 2 * TD)], sem.at[slot]).wait()

    @pl.when(s > 0)
    def _():
        drain((s + 1) % 2)

    @pl.when(s == n - 1)
    def _():
        drain(s % 2)
        def fill(r, carry):
            _row_copy(h_ref, hs_ref, 0, r, sem.at[0]).start()
            return carry
        for e in range(N_EXPERTS + 1):
            lax.fori_loop(pad_lo_ref[e], pad_hi_ref[e], fill, 0)
        for _ in range(N_EXPERTS * TG // (2 * TD)):
            drain(0)


def _dispatch(pos1, pos2, pad_lo, pad_hi, h, n_slots):
    t = h.shape[0]
    return pl.pallas_call(
        _dispatch_kernel,
        grid_spec=pltpu.PrefetchScalarGridSpec(
            num_scalar_prefetch=4,
            grid=(t // TD,),
            in_specs=[pl.BlockSpec(memory_space=pl.ANY)],
            out_specs=pl.BlockSpec(memory_space=pl.ANY),
            scratch_shapes=[pltpu.SemaphoreType.DMA((2,))],
        ),
        out_shape=jax.ShapeDtypeStruct((n_slots, D_MODEL), F32),
        compiler_params=pltpu.CompilerParams(dimension_semantics=("arbitrary",), has_side_effects=True),
        name="moe_dispatch",
    )(pos1, pos2, pad_lo, pad_hi, h)


def _expert_mlp(xb, wg, wu, wd):
    a = _dot(xb, wg)
    hid = (a * jax.nn.sigmoid(a)) * _dot(xb, wu)
    return _dot(hid.astype(BF16), wd)


def _gemm_kernel(te_ref, act_ref, last_ref, hs_ref, wg_ref, wu_ref, wd_ref, os_ref):
    i = pl.program_id(0)

    @pl.when(act_ref[i] > 0)
    def _():
        os_ref[...] = _expert_mlp(hs_ref[...].astype(BF16), wg_ref[...], wu_ref[...], wd_ref[...])

    @pl.when(act_ref[i] == 0)
    def _():
        os_ref[...] = jnp.zeros_like(os_ref)


def _grouped_gemm(tile_expert, tile_active, last_active, hs, wg, wu, wd):
    n_slots = hs.shape[0]
    return pl.pallas_call(
        _gemm_kernel,
        grid_spec=pltpu.PrefetchScalarGridSpec(
            num_scalar_prefetch=3,
            grid=(n_slots // TG,),
            in_specs=[
                pl.BlockSpec((TG, D_MODEL), lambda i, te, act, last: (jnp.minimum(i, last[0]), 0)),
                pl.BlockSpec((None, D_MODEL, D_EXPERT), lambda i, te, act, last: (te[i], 0, 0)),
                pl.BlockSpec((None, D_MODEL, D_EXPERT), lambda i, te, act, last: (te[i], 0, 0)),
                pl.BlockSpec((None, D_EXPERT, D_MODEL), lambda i, te, act, last: (te[i], 0, 0)),
            ],
            out_specs=pl.BlockSpec((TG, D_MODEL), lambda i, te, act, last: (i, 0)),
        ),
        out_shape=jax.ShapeDtypeStruct((n_slots, D_MODEL), F32),
        compiler_params=pltpu.CompilerParams(dimension_semantics=("arbitrary",), vmem_limit_bytes=VMEM_LIMIT),
        name="moe_grouped_gemm",
    )(tile_expert, tile_active, last_active, hs, wg, wu, wd)


def _combine_kernel(pos1_ref, pos2_ref, os_ref, h_ref, route_ref, g2_ref, b2_ref, y_ref, buf, sem):
    s = pl.program_id(0)
    n = pl.num_programs(0)

    def gather(step, slot):
        def issue(t, carry):
            tok = step * TC + t
            pltpu.make_async_copy(os_ref.at[pl.ds(pos1_ref[tok], 1)], buf.at[slot, 0, pl.ds(t, 1)], sem.at[slot]).start()
            pltpu.make_async_copy(os_ref.at[pl.ds(pos2_ref[tok], 1)], buf.at[slot, 1, pl.ds(t, 1)], sem.at[slot]).start()
            return carry
        lax.fori_loop(0, TC, issue, 0)

    @pl.when(s == 0)
    def _():
        gather(0, 0)

    @pl.when(s + 1 < n)
    def _():
        gather(s + 1, (s + 1) % 2)

    slot = s % 2
    pltpu.make_async_copy(buf.at[slot], buf.at[slot], sem.at[slot]).wait()
    route = route_ref[...]
    f = route[:, 2:3] * buf[slot, 0] + route[:, 3:4] * buf[slot, 1]
    y_ref[...] = _layer_norm(ALPHA * h_ref[...] + f, g2_ref[...], b2_ref[...])


def _combine(pos1, pos2, osort, h, route, g2, b2):
    t = h.shape[0]
    return pl.pallas_call(
        _combine_kernel,
        grid_spec=pltpu.PrefetchScalarGridSpec(
            num_scalar_prefetch=2,
            grid=(t // TC,),
            in_specs=[
                pl.BlockSpec(memory_space=pl.ANY),
                pl.BlockSpec((TC, D_MODEL), lambda i, p1, p2: (i, 0)),
                pl.BlockSpec((TC, LANES), lambda i, p1, p2: (i, 0)),
                pl.BlockSpec((1, D_MODEL), lambda i, p1, p2: (0, 0)),
                pl.BlockSpec((1, D_MODEL), lambda i, p1, p2: (0, 0)),
            ],
            out_specs=pl.BlockSpec((TC, D_MODEL), lambda i, p1, p2: (i, 0)),
            scratch_shapes=[pltpu.VMEM((2, 2, TC, D_MODEL), F32), pltpu.SemaphoreType.DMA((2,))],
        ),
        out_shape=jax.ShapeDtypeStruct((t, D_MODEL), F32),
        compiler_params=pltpu.CompilerParams(dimension_semantics=("arbitrary",)),
        name="moe_combine",
    )(pos1, pos2, osort, h, route, g2, b2)


def _routing_tables(route, counts, n_slots):
    t = route.shape[0]
    e1 = route[:, 0].astype(jnp.int32)
    e2 = route[:, 1].astype(jnp.int32)
    r1 = route[:, 4].astype(jnp.int32)
    r2 = route[:, 5].astype(jnp.int32)
    cnt = counts[:, 0, :N_EXPERTS].astype(jnp.int32)
    tot = jnp.sum(cnt, axis=0)
    padded = ((tot + TG - 1) // TG) * TG
    gend = jnp.cumsum(padded)
    gstart = gend - padded
    base = gstart[None, :] + jnp.cumsum(cnt, axis=0) - cnt
    tile_of = jnp.arange(t, dtype=jnp.int32) // TM
    pos1 = base[tile_of, e1] + r1
    pos2 = base[tile_of, e2] + r2
    starts = jnp.arange(n_slots // TG, dtype=jnp.int32) * TG
    n_ended = jnp.sum(starts[:, None] >= gend[None, :], axis=1).astype(jnp.int32)
    tile_active = (starts < gend[-1]).astype(jnp.int32)
    last_active = (gend[-1] // TG - 1).astype(jnp.int32).reshape(1)
    last_expert = jnp.minimum(n_ended[last_active[0]], N_EXPERTS - 1)
    tile_expert = jnp.where(tile_active > 0, jnp.minimum(n_ended, N_EXPERTS - 1), last_expert).astype(jnp.int32)
    pad_lo = jnp.concatenate([gstart + tot, gend[-1:]]).astype(jnp.int32)
    pad_hi = jnp.concatenate([gend, jnp.full((1,), n_slots, jnp.int32)]).astype(jnp.int32)
    return pos1, pos2, pad_lo, pad_hi, tile_expert, tile_active, last_active


def _sample_attn_kernel(x_ref, rope_ref, win_ref, sink_ref, ck_ref, cv_ref, cmk_ref, cmv_ref,
                        z_ref, oa_ref, oc_ref, nk_ref, nv_ref):
    j = pl.program_id(0)
    db = x_ref.shape[0]

    @pl.when(j == 0)
    def _():
        xb = x_ref[...].astype(BF16)
        for c0 in range(0, IN_WIDTH, IN_CHUNK):
            z_ref[:, c0:c0 + IN_CHUNK] = _dot(xb, win_ref[:, c0:c0 + IN_CHUNK])
        c = rope_ref[0]
        s1 = rope_ref[1]
        s2 = rope_ref[2]
        for jj in range((Q_WIDTH + KV_WIDTH) // LANES):
            sl = slice(jj * LANES, (jj + 1) * LANES)
            z_ref[:, sl] = _rope(z_ref[:, sl], c, s1, s2)

    r0 = pl.multiple_of(j * SB, SB)
    zq = z_ref[pl.ds(r0, SB), Q0:Q0 + Q_WIDTH]
    zk = z_ref[pl.ds(r0, SB), K0:K0 + KV_WIDTH]
    zv = z_ref[pl.ds(r0, SB), V0:V0 + KV_WIDTH]
    zc = z_ref[pl.ds(r0, SB), CQ0:CQ0 + MEM_WIDTH]
    sink = sink_ref[:, 0:1]
    row_kv = lax.broadcasted_iota(jnp.int32, (N_HEADS, KV_WIDTH), 0) & (N_KV - 1)
    lane_kv = lax.broadcasted_iota(jnp.int32, (N_HEADS, KV_WIDTH), 1) // HEAD_DIM
    own = row_kv == lane_kv
    row_c = lax.broadcasted_iota(jnp.int32, (N_HEADS, MEM_WIDTH), 0)
    lane_c = lax.broadcasted_iota(jnp.int32, (N_HEADS, MEM_WIDTH), 1) // HEAD_DIM
    own_c = row_c == lane_c
    last_row = lax.broadcasted_iota(jnp.int32, (WINDOW, KV_WIDTH), 0) == WINDOW - 1

    oa_rows = [[] for _ in range(GROUP)]
    oc_rows = []
    for b in range(SB):
        kc = ck_ref[b]
        vc = cv_ref[b]
        knew = zk[b:b + 1, :]
        vnew = zv[b:b + 1, :]
        qblk = jnp.concatenate(
            [jnp.broadcast_to(zq[b:b + 1, g * KV_WIDTH:(g + 1) * KV_WIDTH], (N_KV, KV_WIDTH)) for g in range(GROUP)],
            axis=0)
        qblk = jnp.where(own, qblk, 0.0)
        s = _dot_nt(qblk.astype(BF16), kc.astype(BF16))
        s_new = jnp.sum(qblk * knew, axis=1, keepdims=True)
        m = jnp.maximum(jnp.maximum(jnp.max(s, axis=1, keepdims=True), s_new), sink)
        p = jnp.exp(s - m)
        p_new = jnp.exp(s_new - m)
        inv = 1.0 / (jnp.sum(p, axis=1, keepdims=True) + p_new + jnp.exp(sink - m))
        o = _dot((p * inv).astype(BF16), vc.astype(BF16)) + (p_new * inv) * vnew
        o = jnp.where(own, o, 0.0)
        for g in range(GROUP):
            oa_rows[g].append(jnp.sum(o[g * N_KV:(g + 1) * N_KV], axis=0, keepdims=True))
        nk_ref[b] = jnp.where(last_row, knew, pltpu.roll(kc, WINDOW - 1, 0))
        nv_ref[b] = jnp.where(last_row, vnew, pltpu.roll(vc, WINDOW - 1, 0))
        cblk = jnp.where(own_c, jnp.broadcast_to(zc[b:b + 1, :], (N_HEADS, MEM_WIDTH)), 0.0)
        sc = _dot_nt(cblk.astype(BF16), cmk_ref[b].astype(BF16))
        mc = jnp.max(sc, axis=1, keepdims=True)
        pc = jnp.exp(sc - mc)
        pc = pc * (1.0 / jnp.sum(pc, axis=1, keepdims=True))
        ocb = jnp.where(own_c, _dot(pc.astype(BF16), cmv_ref[b].astype(BF16)), 0.0)
        oc_rows.append(jnp.sum(ocb, axis=0, keepdims=True))
    for g in range(GROUP):
        oa_ref[pl.ds(r0, SB), g * KV_WIDTH:(g + 1) * KV_WIDTH] = jnp.concatenate(oa_rows[g], axis=0)
    oc_ref[pl.ds(r0, SB), :] = jnp.concatenate(oc_rows, axis=0)


def _sample_attn(x, rope, win, sink_gk, ck, cv, cmk, cmv):
    db = x.shape[0]
    blk = lambda r: pl.BlockSpec((SB, r, KV_WIDTH), lambda j: (j, 0, 0))
    full = lambda w: pl.BlockSpec((db, w), lambda j: (0, 0))
    return pl.pallas_call(
        _sample_attn_kernel,
        grid=(db // SB,),
        in_specs=[
            full(D_MODEL), _const_spec((3, 1, LANES)), _const_spec((D_MODEL, IN_WIDTH)),
            _const_spec((N_HEADS, LANES)),
            blk(WINDOW), blk(WINDOW), blk(N_MEM), blk(N_MEM),
        ],
        out_specs=[full(IN_WIDTH), full(Q_WIDTH), full(MEM_WIDTH), blk(WINDOW), blk(WINDOW)],
        out_shape=[
            jax.ShapeDtypeStruct((db, IN_WIDTH), F32),
            jax.ShapeDtypeStruct((db, Q_WIDTH), F32),
            jax.ShapeDtypeStruct((db, MEM_WIDTH), F32),
            jax.ShapeDtypeStruct((db, WINDOW, KV_WIDTH), F32),
            jax.ShapeDtypeStruct((db, WINDOW, KV_WIDTH), F32),
        ],
        compiler_params=pltpu.CompilerParams(dimension_semantics=("arbitrary",), vmem_limit_bytes=VMEM_LIMIT),
        name="sample_attn",
    )(x, rope, win, sink_gk, ck, cv, cmk, cmv)


def _sample_tail_kernel(x_ref, z_ref, oa_ref, oc_ref, st_ref, wa_ref, wb_ref, wc_ref, wo_ref, wmix_ref,
                        pscale_ref, g1_ref, b1_ref, wr_ref, br_ref, wg_ref, wu_ref, wd_ref, g2_ref, b2_ref,
                        y_ref, npool_ref, h_sc, comb_sc, acc_sc):
    e = pl.program_id(0)

    @pl.when(e == 0)
    def _():
        u = z_ref[:, U0:U0 + POOL_WIDTH]
        npool_ref[:, 0:(POOL_STATE - 1) * POOL_WIDTH] = st_ref[:, POOL_WIDTH:POOL_STATE * POOL_WIDTH]
        npool_ref[:, (POOL_STATE - 1) * POOL_WIDTH:POOL_STATE * POOL_WIDTH] = u
        obs = []
        for g, w in enumerate(POOL_WINDOWS):
            sl = slice(g * POOL_GROUP_DIM, (g + 1) * POOL_GROUP_DIM)
            cur = u[:, sl]
            ws = cur
            for jj in range(1, w):
                base = (POOL_STATE - jj) * POOL_WIDTH
                ws = ws + st_ref[:, base + g * POOL_GROUP_DIM:base + (g + 1) * POOL_GROUP_DIM]
            cnt = float(min(PAST_LEN + 1, w))
            pooled = ws / cnt - cur
            obs.append(_dot(pooled.astype(BF16), wmix_ref[g]) * pscale_ref[:, sl])
        ob = jnp.concatenate(obs, axis=1)
        h = _merge_ln1(x_ref[...], oa_ref[...].astype(BF16), ob, oc_ref[...], z_ref[:, GZ0:GZ0 + 3 * D_MODEL],
                       wa_ref, wb_ref, wc_ref, wo_ref, g1_ref[...], b1_ref[...])
        h_sc[...] = h
        logits = _dot(h.astype(BF16), wr_ref[...]) + br_ref[...]
        _, _, (hot1, hot2, w1, w2) = _route(logits, None)
        comb_sc[...] = jnp.where(hot1, w1, 0.0) + jnp.where(hot2, w2, 0.0)
        acc_sc[...] = jnp.zeros_like(acc_sc)

    out = _expert_mlp(h_sc[...].astype(BF16), wg_ref[...], wu_ref[...], wd_ref[...])
    lane = lax.broadcasted_iota(jnp.int32, comb_sc.shape, 1)
    ce = jnp.sum(jnp.where(lane == e, comb_sc[...], 0.0), axis=1, keepdims=True)
    acc_sc[...] += ce * out

    @pl.when(e == pl.num_programs(0) - 1)
    def _():
        y_ref[...] = _layer_norm(ALPHA * h_sc[...] + acc_sc[...], g2_ref[...], b2_ref[...])


def _sample_tail(x, z, oa, oc, state, wa, wb, wc, wo, wmix, pscale, g1, b1, wr, br, wg, wu, wd, g2, b2):
    db = x.shape[0]
    full = lambda w: pl.BlockSpec((db, w), lambda e: (0, 0))
    vec = lambda w: pl.BlockSpec((1, w), lambda e: (0, 0))
    return pl.pallas_call(
        _sample_tail_kernel,
        grid=(N_EXPERTS,),
        in_specs=[
            full(D_MODEL), full(IN_WIDTH), full(Q_WIDTH), full(MEM_WIDTH), full(POOL_STATE * POOL_WIDTH),
            _const_spec((Q_WIDTH, D_MODEL)), _const_spec((POOL_WIDTH, D_MODEL)),
            _const_spec((MEM_WIDTH, D_MODEL)), _const_spec((D_MODEL, D_MODEL)),
            _const_spec((len(POOL_WINDOWS), POOL_GROUP_DIM, POOL_GROUP_DIM)),
            vec(POOL_WIDTH), vec(D_MODEL), vec(D_MODEL),
            _const_spec((D_MODEL, LANES)), vec(LANES),
            pl.BlockSpec((None, D_MODEL, D_EXPERT), lambda e: (e, 0, 0)),
            pl.BlockSpec((None, D_MODEL, D_EXPERT), lambda e: (e, 0, 0)),
            pl.BlockSpec((None, D_EXPERT, D_MODEL), lambda e: (e, 0, 0)),
            vec(D_MODEL), vec(D_MODEL),
        ],
        out_specs=[full(D_MODEL), full(POOL_STATE * POOL_WIDTH)],
        out_shape=[jax.ShapeDtypeStruct((db, D_MODEL), F32),
                   jax.ShapeDtypeStruct((db, POOL_STATE * POOL_WIDTH), F32)],
        scratch_shapes=[pltpu.VMEM((db, D_MODEL), F32), pltpu.VMEM((db, LANES), F32),
                        pltpu.VMEM((db, D_MODEL), F32)],
        compiler_params=pltpu.CompilerParams(dimension_semantics=("arbitrary",), vmem_limit_bytes=VMEM_LIMIT),
        name="sample_tail",
    )(x, z, oa, oc, state, wa, wb, wc, wo, wmix, pscale, g1, b1, wr, br, wg, wu, wd, g2, b2)


def _rope_tables(pos):
    half = ROPE_DIM // 2
    inv = jnp.power(ROPE_THETA, -jnp.arange(half, dtype=F32) * (2.0 / ROPE_DIM))
    ang = pos.astype(F32)[:, None] * inv[None, :]
    lane = np.arange(LANES)
    off = lane % HEAD_DIM
    cos = jnp.cos(ang)[:, lane % half]
    sin = jnp.sin(ang)[:, lane % half]
    c = jnp.where(off[None, :] < ROPE_DIM, cos, 1.0)
    s1 = jnp.where((off[None, :] >= half) & (off[None, :] < ROPE_DIM), sin, 0.0)
    s2 = jnp.where(off[None, :] < half, -sin, 0.0)
    return jnp.stack([c, s1, s2]).astype(F32)


def _q_heads_group_major(w, axis):
    if axis == 1:
        n = w.shape[0]
        return w.reshape(n, N_KV, GROUP, HEAD_DIM).transpose(0, 2, 1, 3).reshape(n, Q_WIDTH)
    n = w.shape[1]
    return w.reshape(N_KV, GROUP, HEAD_DIM, n).transpose(1, 0, 2, 3).reshape(Q_WIDTH, n)


def kernel(x_prompt, x_sample, cache_win_k, cache_win_v, state_pool, cache_mem_k, cache_mem_v, mem_prompt, w_in, sinks, w_pool_mix, pool_scale, w_mem_k, w_mem_v, w_branch_a, w_branch_b, w_branch_c, w_out, ln1_g, ln1_b, w_group, b_group, w_router, b_router, w_gate, w_up, w_down, ln2_g, ln2_b):
    assert w_in.shape[0] == DEPTH == 1
    b, l, _ = x_prompt.shape
    db, ds, _ = x_sample.shape
    assert ds == 1 and l % TM == 0 and db % SB == 0 and (b * l) % TD == 0 and (b * l) % TC == 0
    assert cache_win_k.shape[2] == WINDOW
    t = b * l

    win = w_in[0]
    scale = HEAD_DIM ** -0.5
    wq = _q_heads_group_major(win[:, Q0:Q0 + Q_WIDTH], 1) * scale
    wcq = win[:, CQ0:CQ0 + MEM_WIDTH] * scale
    win_b = jnp.concatenate([wq, win[:, K0:CQ0], wcq, win[:, GZ0:]], axis=1).astype(BF16)
    wa = _q_heads_group_major(w_branch_a[0], 0).astype(BF16)
    wb = w_branch_b[0].astype(BF16)
    wc = w_branch_c[0].astype(BF16)
    wo = w_out[0].astype(BF16)
    wmix = w_pool_mix[0].astype(BF16)
    pscale = pool_scale[0].reshape(1, POOL_WIDTH)
    g1 = ln1_g[0].reshape(1, D_MODEL)
    b1 = ln1_b[0].reshape(1, D_MODEL)
    g2 = ln2_g[0].reshape(1, D_MODEL)
    b2 = ln2_b[0].reshape(1, D_MODEL)
    wr = jnp.concatenate([w_group[0], w_router[0].reshape(D_MODEL, N_EXPERTS)], axis=1)
    wr = jnp.pad(wr, ((0, 0), (0, LANES - wr.shape[1]))).astype(BF16)
    br = jnp.pad(jnp.concatenate([b_group[0], b_router[0].reshape(N_EXPERTS)]), (0, LANES - N_EXPERT_GROUPS - N_EXPERTS))
    br = br.reshape(1, LANES).astype(F32)
    wg = w_gate[0].astype(BF16)
    wu = w_up[0].astype(BF16)
    wd = w_down[0].astype(BF16)
    sink = sinks[0].astype(F32)
    sink_gk = jnp.broadcast_to(sink.reshape(N_KV, GROUP).T.reshape(N_HEADS, 1), (N_HEADS, LANES))

    mk, mv = _mem_project(mem_prompt, w_mem_k[0].astype(BF16), w_mem_v[0].astype(BF16))
    rope_p = _rope_tables(jnp.arange(l, dtype=jnp.int32))
    h, route, counts, nk_p, nv_p, npool_p = _front(
        x_prompt, rope_p, sink, win_b, wa, wb, wc, wo, wmix, pscale, mk, mv, g1, b1, wr, br)
    h = h.reshape(t, D_MODEL)
    route = route.reshape(t, LANES)
    n_slots = 2 * t + N_EXPERTS * TG
    pos1, pos2, pad_lo, pad_hi, tile_expert, tile_active, last_active = _routing_tables(
        route, counts.reshape(-1, SUBLANES, LANES), n_slots)
    hs = _dispatch(pos1, pos2, pad_lo, pad_hi, h, n_slots)
    osort = _grouped_gemm(tile_expert, tile_active, last_active, hs, wg, wu, wd)
    y_p = _combine(pos1, pos2, osort, h, route, g2, b2).reshape(b, l, D_MODEL)

    rope_s = _rope_tables(jnp.full((1,), PAST_LEN, jnp.int32))
    xs = x_sample.reshape(db, D_MODEL)
    ck = cache_win_k[0].reshape(db, WINDOW, KV_WIDTH)
    cv = cache_win_v[0].reshape(db, WINDOW, KV_WIDTH)
    cmk = cache_mem_k[0].reshape(db, N_MEM, MEM_WIDTH)
    cmv = cache_mem_v[0].reshape(db, N_MEM, MEM_WIDTH)
    z_s, oa_s, oc_s, nk_s, nv_s = _sample_attn(xs, rope_s, win_b, sink_gk, ck, cv, cmk, cmv)
    state = state_pool[0].reshape(db, POOL_STATE * POOL_WIDTH)
    y_s, npool_s = _sample_tail(xs, z_s, oa_s, oc_s, state, wa, wb, wc, wo, wmix, pscale, g1, b1, wr, br,
                                wg, wu, wd, g2, b2)

    kv5 = lambda a, n, w: a.reshape(1, n, w, N_KV, HEAD_DIM)
    return (y_p, y_s.reshape(db, 1, D_MODEL),
            kv5(nk_p, b, QB), kv5(nv_p, b, QB),
            npool_p[:, 2 * SUBLANES - POOL_STATE:, :][None],
            kv5(mk, b, N_MEM), kv5(mv, b, N_MEM),
            kv5(nk_s, db, WINDOW), kv5(nv_s, db, WINDOW),
            npool_s.reshape(1, db, POOL_STATE, POOL_WIDTH))
```

```python
import functools

import jax
import jax.numpy as jnp
import numpy as np
from jax import lax
from jax.experimental import pallas as pl
from jax.experimental.pallas import tpu as pltpu

D_MODEL = 1024
N_HEADS = 16
HEAD_DIM = 64
N_KV = 4
GROUP = N_HEADS // N_KV
WINDOW = 128
ROPE_THETA = 500000.0
ROPE_DIM = HEAD_DIM // 4
Q_WIDTH = N_HEADS * HEAD_DIM
KV_WIDTH = N_KV * HEAD_DIM
POOL_WINDOWS = (2, 4, 8, 16)
POOL_WIDTH = D_MODEL // 2
POOL_GROUP_DIM = POOL_WIDTH // len(POOL_WINDOWS)
POOL_STATE = max(POOL_WINDOWS) - 1
N_MEM = 256
MEM_HEADS = 4
MEM_WIDTH = MEM_HEADS * HEAD_DIM
N_EXPERT_GROUPS = 4
EXPERTS_PER_GROUP = 4
N_EXPERTS = N_EXPERT_GROUPS * EXPERTS_PER_GROUP
D_EXPERT = 512
PAST_LEN = 16384
DEPTH = 1
ALPHA = (2.0 * DEPTH) ** 0.25
LN_EPS = 1e-5

Q0 = 0
K0 = Q0 + Q_WIDTH
V0 = K0 + KV_WIDTH
U0 = V0 + KV_WIDTH
CQ0 = U0 + POOL_WIDTH
GZ0 = CQ0 + MEM_WIDTH
IN_WIDTH = GZ0 + 3 * D_MODEL

LANES = 128
SUBLANES = 8
VMEM_LIMIT = 56 * 1024 * 1024

TM = 256
QB = WINDOW
TG = 256
TD = 512
TC = 256
SB = 8
IN_CHUNK = 768

BF16 = jnp.bfloat16
F32 = jnp.float32
NEG_INF = float("-inf")


def _const_spec(shape):
    nd = len(shape)
    return pl.BlockSpec(shape, lambda *_: (0,) * nd, pipeline_mode=pl.Buffered(1))


def _layer_norm(x, g, b):
    mu = jnp.mean(x, axis=-1, keepdims=True)
    xc = x - mu
    var = jnp.mean(xc * xc, axis=-1, keepdims=True)
    return xc * lax.rsqrt(var + LN_EPS) * g + b


def _dot(a, b):
    return jnp.dot(a, b, preferred_element_type=F32)


def _dot_nt(a, b):
    return lax.dot_general(a, b, (((1,), (1,)), ((), ())), preferred_element_type=F32)


def _lane_block_mask(shape, block, width=HEAD_DIM):
    lane = lax.broadcasted_iota(jnp.int32, shape, len(shape) - 1)
    return (lane >= block * width) & (lane < (block + 1) * width)


def _rope(x, c, s1, s2):
    half = ROPE_DIM // 2
    return x * c + pltpu.roll(x, half, 1) * s1 + pltpu.roll(x, LANES - half, 1) * s2


def _route(logits, row_tri):
    rows = logits.shape[0]
    lane = lax.broadcasted_iota(jnp.int32, (rows, LANES), 1)
    lanef = lane.astype(F32)
    big = float(LANES)
    is_g = lane < N_EXPERT_GROUPS
    glog = jnp.where(is_g, logits, NEG_INF)
    gmax = jnp.max(glog, axis=1, keepdims=True)
    gsum = jnp.sum(jnp.where(is_g, jnp.exp(glog - gmax), 0.0), axis=1, keepdims=True)
    gp = 1.0 / gsum
    gidx = jnp.min(jnp.where(glog == gmax, lanef, big), axis=1, keepdims=True).astype(jnp.int32)
    lo = N_EXPERT_GROUPS + gidx * EXPERTS_PER_GROUP
    in_grp = (lane >= lo) & (lane < lo + EXPERTS_PER_GROUP)
    el = jnp.where(in_grp, logits, NEG_INF)
    v1 = jnp.max(el, axis=1, keepdims=True)
    i1 = jnp.min(jnp.where(el == v1, lanef, big), axis=1, keepdims=True).astype(jnp.int32)
    el2 = jnp.where(lane == i1, NEG_INF, el)
    v2 = jnp.max(el2, axis=1, keepdims=True)
    i2 = jnp.min(jnp.where(el2 == v2, lanef, big), axis=1, keepdims=True).astype(jnp.int32)
    e21 = jnp.exp(v2 - v1)
    inv = 1.0 / (1.0 + e21)
    w1 = inv * gp
    w2 = e21 * inv * gp
    e1 = i1 - N_EXPERT_GROUPS
    e2 = i2 - N_EXPERT_GROUPS
    hot1 = lane == e1
    hot2 = lane == e2
    onehot = jnp.where(hot1 | hot2, 1.0, 0.0)
    counts = jnp.sum(onehot, axis=0, keepdims=True)
    if row_tri is None:
        r1 = jnp.zeros_like(w1)
        r2 = jnp.zeros_like(w1)
    else:
        rank = _dot(row_tri, onehot.astype(BF16))
        r1 = jnp.sum(jnp.where(hot1, rank, 0.0), axis=1, keepdims=True)
        r2 = jnp.sum(jnp.where(hot2, rank, 0.0), axis=1, keepdims=True)
    route = jnp.where(lane == 0, e1.astype(F32),
            jnp.where(lane == 1, e2.astype(F32),
            jnp.where(lane == 2, w1,
            jnp.where(lane == 3, w2,
            jnp.where(lane == 4, r1,
            jnp.where(lane == 5, r2, 0.0))))))
    return route, counts, (hot1, hot2, w1, w2)


def _merge_ln1(x, oa, ob, oc, gz, wa_ref, wb_ref, wc_ref, wo_ref, g1, b1):
    ya = _dot(oa, wa_ref[...])
    yb = _dot(ob.astype(BF16), wb_ref[...])
    yc = _dot(oc.astype(BF16), wc_ref[...])
    m = (jax.nn.sigmoid(gz[:, 0:D_MODEL]) * ya
         + jax.nn.sigmoid(gz[:, D_MODEL:2 * D_MODEL]) * yb
         + jax.nn.sigmoid(gz[:, 2 * D_MODEL:3 * D_MODEL]) * yc)
    hpre = ALPHA * x + _dot(m.astype(BF16), wo_ref[...])
    return _layer_norm(hpre, g1, b1)


def _mem_kernel(mem_ref, wk_ref, wv_ref, mk_ref, mv_ref):
    m = mem_ref[...].astype(BF16)
    mk_ref[...] = _dot(m, wk_ref[...])
    mv_ref[...] = _dot(m, wv_ref[...])


def _mem_project(mem, wk, wv):
    b = mem.shape[0]
    out = jax.ShapeDtypeStruct((b, N_MEM, MEM_WIDTH), F32)
    return pl.pallas_call(
        _mem_kernel,
        grid=(b,),
        in_specs=[pl.BlockSpec((None, N_MEM, D_MODEL), lambda i: (i, 0, 0)),
                  _const_spec((D_MODEL, MEM_WIDTH)), _const_spec((D_MODEL, MEM_WIDTH))],
        out_specs=[pl.BlockSpec((None, N_MEM, MEM_WIDTH), lambda i: (i, 0, 0))] * 2,
        out_shape=[out, out],
        name="mem_project",
    )(mem, wk, wv)


def _front_kernel(sinks_ref, x_ref, rope_ref, win_ref, wa_ref, wb_ref, wc_ref, wo_ref, wmix_ref,
                  pscale_ref, mk_ref, mv_ref, g1_ref, b1_ref, wr_ref, br_ref,
                  h_ref, route_ref, counts_ref, nk_ref, nv_ref, npool_ref,
                  z_ref, qb_ref, kext_ref, vext_ref, uext_ref, oa_ref, ob_ref):
    i = pl.program_id(1)
    x = x_ref[...]
    xb = x.astype(BF16)

    @pl.when(i == 0)
    def _():
        kext_ref[0:QB, :] = jnp.zeros((QB, KV_WIDTH), BF16)
        vext_ref[0:QB, :] = jnp.zeros((QB, KV_WIDTH), BF16)
        uext_ref[0:2 * SUBLANES, :] = jnp.zeros((2 * SUBLANES, POOL_WIDTH), F32)

    @pl.when(i > 0)
    def _():
        kext_ref[0:QB, :] = kext_ref[TM:TM + QB, :]
        vext_ref[0:QB, :] = vext_ref[TM:TM + QB, :]
        uext_ref[0:2 * SUBLANES, :] = uext_ref[TM:TM + 2 * SUBLANES, :]

    for c0 in range(0, IN_WIDTH, IN_CHUNK):
        z_ref[:, c0:c0 + IN_CHUNK] = _dot(xb, win_ref[:, c0:c0 + IN_CHUNK])

    c = rope_ref[0]
    s1 = rope_ref[1]
    s2 = rope_ref[2]
    for j in range(Q_WIDTH // LANES):
        sl = slice(Q0 + j * LANES, Q0 + (j + 1) * LANES)
        qb_ref[:, j * LANES:(j + 1) * LANES] = _rope(z_ref[:, sl], c, s1, s2).astype(BF16)
    for j in range(KV_WIDTH // LANES):
        sl = slice(K0 + j * LANES, K0 + (j + 1) * LANES)
        kr = _rope(z_ref[:, sl], c, s1, s2)
        z_ref[:, sl] = kr
        kext_ref[QB:QB + TM, j * LANES:(j + 1) * LANES] = kr.astype(BF16)
    vext_ref[QB:QB + TM, :] = z_ref[:, V0:V0 + KV_WIDTH].astype(BF16)
    nk_ref[...] = z_ref[TM - QB:TM, K0:K0 + KV_WIDTH]
    nv_ref[...] = z_ref[TM - QB:TM, V0:V0 + KV_WIDTH]

    rowq = lax.broadcasted_iota(jnp.int32, (GROUP * QB, 2 * QB), 0) & (QB - 1)
    colk = lax.broadcasted_iota(jnp.int32, (GROUP * QB, 2 * QB), 1)
    band = (colk >= rowq) & (colk <= rowq + WINDOW)
    rowg = lax.broadcasted_iota(jnp.int32, (GROUP * QB, 1), 0) // QB
    for sb in range(TM // QB):
        k2 = kext_ref[sb * QB:(sb + 2) * QB, :]
        v2 = vext_ref[sb * QB:(sb + 2) * QB, :]
        if sb == 0:
            mask = band & ((colk >= QB) | (i > 0))
        else:
            mask = band
        qs = jnp.concatenate(
            [qb_ref[sb * QB:(sb + 1) * QB, g * KV_WIDTH:(g + 1) * KV_WIDTH] for g in range(GROUP)], axis=0)
        acc = jnp.zeros((GROUP * QB, KV_WIDTH), F32)
        for kv in range(N_KV):
            kmask = _lane_block_mask((2 * QB, KV_WIDTH), kv)
            kkv = jnp.where(kmask, k2, jnp.zeros_like(k2))
            s = jnp.where(mask, _dot_nt(qs, kkv), NEG_INF)
            sink = jnp.zeros((GROUP * QB, 1), F32)
            for g in range(GROUP):
                sink = jnp.where(rowg == g, sinks_ref[kv * GROUP + g], sink)
            m = jnp.maximum(jnp.max(s, axis=1, keepdims=True), sink)
            p = jnp.exp(s - m)
            den = jnp.sum(p, axis=1, keepdims=True) + jnp.exp(sink - m)
            o = _dot((p * (1.0 / den)).astype(BF16), v2)
            acc = jnp.where(_lane_block_mask((GROUP * QB, KV_WIDTH), kv), o, acc)
        for g in range(GROUP):
            oa_ref[sb * QB:(sb + 1) * QB, g * KV_WIDTH:(g + 1) * KV_WIDTH] = acc[g * QB:(g + 1) * QB].astype(BF16)

    hist = 2 * SUBLANES
    uext_ref[hist:hist + TM, :] = z_ref[:, U0:U0 + POOL_WIDTH]
    npool_ref[...] = uext_ref[TM:TM + hist, :]
    pos = i * TM + lax.broadcasted_iota(jnp.int32, (TM, 1), 0)
    for g, w in enumerate(POOL_WINDOWS):
        sl = slice(g * POOL_GROUP_DIM, (g + 1) * POOL_GROUP_DIM)
        cur = uext_ref[hist:hist + TM, sl]
        ws = cur
        for j in range(1, w):
            ws = ws + uext_ref[hist - j:hist - j + TM, sl]
        cnt = jnp.minimum(pos + 1, w).astype(F32)
        pooled = ws / cnt - cur
        ob_ref[:, sl] = _dot(pooled.astype(BF16), wmix_ref[g]) * pscale_ref[:, sl]

    cq = z_ref[:, CQ0:CQ0 + MEM_WIDTH].astype(BF16)
    mk = mk_ref[...].astype(BF16)
    mv = mv_ref[...].astype(BF16)
    oc = jnp.zeros((TM, MEM_WIDTH), F32)
    for hh in range(MEM_HEADS):
        kh = jnp.where(_lane_block_mask((N_MEM, MEM_WIDTH), hh), mk, jnp.zeros_like(mk))
        s = _dot_nt(cq, kh)
        m = jnp.max(s, axis=1, keepdims=True)
        p = jnp.exp(s - m)
        den = jnp.sum(p, axis=1, keepdims=True)
        o = _dot((p * (1.0 / den)).astype(BF16), mv)
        oc = jnp.where(_lane_block_mask((TM, MEM_WIDTH), hh), o, oc)

    h = _merge_ln1(x, oa_ref[...], ob_ref[...], oc, z_ref[:, GZ0:GZ0 + 3 * D_MODEL],
                   wa_ref, wb_ref, wc_ref, wo_ref, g1_ref[...], b1_ref[...])
    h_ref[...] = h
    logits = _dot(h.astype(BF16), wr_ref[...]) + br_ref[...]
    tri = (lax.broadcasted_iota(jnp.int32, (TM, TM), 1)
           < lax.broadcasted_iota(jnp.int32, (TM, TM), 0)).astype(BF16)
    route, counts, _ = _route(logits, tri)
    route_ref[...] = route
    counts_ref[...] = jnp.broadcast_to(counts, (SUBLANES, LANES))


def _front(x, rope, sinks, win, wa, wb, wc, wo, wmix, pscale, mk, mv, g1, b1, wr, br):
    b, l, _ = x.shape
    nt = l // TM
    hist = 2 * SUBLANES
    tile = lambda w: pl.BlockSpec((None, TM, w), lambda bi, ti: (bi, ti, 0))
    per_b = lambda r, w: pl.BlockSpec((None, r, w), lambda bi, ti: (bi, 0, 0))
    return pl.pallas_call(
        _front_kernel,
        grid=(b, nt),
        in_specs=[
            pl.BlockSpec(memory_space=pltpu.SMEM),
            tile(D_MODEL),
            pl.BlockSpec((3, TM, LANES), lambda bi, ti: (0, ti, 0)),
            _const_spec((D_MODEL, IN_WIDTH)),
            _const_spec((Q_WIDTH, D_MODEL)), _const_spec((POOL_WIDTH, D_MODEL)),
            _const_spec((MEM_WIDTH, D_MODEL)), _const_spec((D_MODEL, D_MODEL)),
            _const_spec((len(POOL_WINDOWS), POOL_GROUP_DIM, POOL_GROUP_DIM)),
            _const_spec((1, POOL_WIDTH)),
            per_b(N_MEM, MEM_WIDTH), per_b(N_MEM, MEM_WIDTH),
            _const_spec((1, D_MODEL)), _const_spec((1, D_MODEL)),
            _const_spec((D_MODEL, LANES)), _const_spec((1, LANES)),
        ],
        out_specs=[
            tile(D_MODEL), tile(LANES),
            pl.BlockSpec((None, None, SUBLANES, LANES), lambda bi, ti: (bi, ti, 0, 0)),
            per_b(QB, KV_WIDTH), per_b(QB, KV_WIDTH), per_b(hist, POOL_WIDTH),
        ],
        out_shape=[
            jax.ShapeDtypeStruct((b, l, D_MODEL), F32),
            jax.ShapeDtypeStruct((b, l, LANES), F32),
            jax.ShapeDtypeStruct((b, nt, SUBLANES, LANES), F32),
            jax.ShapeDtypeStruct((b, QB, KV_WIDTH), F32),
            jax.ShapeDtypeStruct((b, QB, KV_WIDTH), F32),
            jax.ShapeDtypeStruct((b, hist, POOL_WIDTH), F32),
        ],
        scratch_shapes=[
            pltpu.VMEM((TM, IN_WIDTH), F32),
            pltpu.VMEM((TM, Q_WIDTH), BF16),
            pltpu.VMEM((QB + TM, KV_WIDTH), BF16),
            pltpu.VMEM((QB + TM, KV_WIDTH), BF16),
            pltpu.VMEM((hist + TM, POOL_WIDTH), F32),
            pltpu.VMEM((TM, Q_WIDTH), BF16),
            pltpu.VMEM((TM, POOL_WIDTH), F32),
        ],
        compiler_params=pltpu.CompilerParams(
            dimension_semantics=("arbitrary", "arbitrary"), vmem_limit_bytes=VMEM_LIMIT),
        name="front_prompt",
    )(sinks, x, rope, win, wa, wb, wc, wo, wmix, pscale, mk, mv, g1, b1, wr, br)


def _row_copy(src, dst, i_src, i_dst, sem):
    return pltpu.make_async_copy(src.at[pl.ds(i_src, 1)], dst.at[pl.ds(i_dst, 1)], sem)


def _dispatch_kernel(pos1_ref, pos2_ref, pad_lo_ref, pad_hi_ref, h_ref, hs_ref, sem):
    s = pl.program_id(0)
    n = pl.num_programs(0)

    def issue(t, carry):
        tok = s * TD + t
        _row_copy(h_ref, hs_ref, t, pos1_ref[tok], sem).start()
        _row_copy(h_ref, hs_ref, t, pos2_ref[tok], sem).start()
        return carry

    lax.fori_loop(0, TD, issue, 0)

    def drain(rows):
        pltpu.make_async_copy(hs_ref.at[pl.ds(0, rows)], hs_ref.at[pl.ds(0, rows)], sem).wait()

    drain(2 * TD)

    @pl.when(s == n - 1)
    def _():
        def fill(r, carry):
            _row_copy(h_ref, hs_ref, 0, r, sem).start()
            return carry
        for e in range(N_EXPERTS + 1):
            lax.fori_loop(pad_lo_ref[e], pad_hi_ref[e], fill, 0)
        drain(N_EXPERTS * TG)


def _dispatch(pos1, pos2, pad_lo, pad_hi, h, n_slots):
    t = h.shape[0]
    return pl.pallas_call(
        _dispatch_kernel,
        grid_spec=pltpu.PrefetchScalarGridSpec(
            num_scalar_prefetch=4,
            grid=(t // TD,),
            in_specs=[pl.BlockSpec((TD, D_MODEL), lambda i, *_: (i, 0))],
            out_specs=pl.BlockSpec(memory_space=pl.ANY),
            scratch_shapes=[pltpu.SemaphoreType.DMA(())],
        ),
        out_shape=jax.ShapeDtypeStruct((n_slots, D_MODEL), F32),
        compiler_params=pltpu.CompilerParams(dimension_semantics=("arbitrary",), has_side_effects=True),
        name="moe_dispatch",
    )(pos1, pos2, pad_lo, pad_hi, h)


def _expert_mlp(xb, wg, wu, wd):
    a = _dot(xb, wg)
    hid = (a * jax.nn.sigmoid(a)) * _dot(xb, wu)
    return _dot(hid.astype(BF16), wd)


def _gemm_kernel(te_ref, act_ref, last_ref, hs_ref, wg_ref, wu_ref, wd_ref, os_ref):
    i = pl.program_id(0)

    @pl.when(act_ref[i] > 0)
    def _():
        os_ref[...] = _expert_mlp(hs_ref[...].astype(BF16), wg_ref[...], wu_ref[...], wd_ref[...])

    @pl.when(act_ref[i] == 0)
    def _():
        os_ref[...] = jnp.zeros_like(os_ref)


def _grouped_gemm(tile_expert, tile_active, last_active, hs, wg, wu, wd):
    n_slots = hs.shape[0]
    return pl.pallas_call(
        _gemm_kernel,
        grid_spec=pltpu.PrefetchScalarGridSpec(
            num_scalar_prefetch=3,
            grid=(n_slots // TG,),
            in_specs=[
                pl.BlockSpec((TG, D_MODEL), lambda i, te, act, last: (jnp.minimum(i, last[0]), 0)),
                pl.BlockSpec((None, D_MODEL, D_EXPERT), lambda i, te, act, last: (te[i], 0, 0)),
                pl.BlockSpec((None, D_MODEL, D_EXPERT), lambda i, te, act, last: (te[i], 0, 0)),
                pl.BlockSpec((None, D_EXPERT, D_MODEL), lambda i, te, act, last: (te[i], 0, 0)),
            ],
            out_specs=pl.BlockSpec((TG, D_MODEL), lambda i, te, act, last: (i, 0)),
        ),
        out_shape=jax.ShapeDtypeStruct((n_slots, D_MODEL), F32),
        compiler_params=pltpu.CompilerParams(dimension_semantics=("arbitrary",), vmem_limit_bytes=VMEM_LIMIT),
        name="moe_grouped_gemm",
    )(tile_expert, tile_active, last_active, hs, wg, wu, wd)


def _combine_kernel(pos1_ref, pos2_ref, os_ref, h_ref, route_ref, g2_ref, b2_ref, y_ref, buf, sem):
    s = pl.program_id(0)
    n = pl.num_programs(0)

    def gather(step, slot):
        def issue(t, carry):
            tok = step * TC + t
            pltpu.make_async_copy(os_ref.at[pl.ds(pos1_ref[tok], 1)], buf.at[slot, 0, pl.ds(t, 1)], sem.at[slot]).start()
            pltpu.make_async_copy(os_ref.at[pl.ds(pos2_ref[tok], 1)], buf.at[slot, 1, pl.ds(t, 1)], sem.at[slot]).start()
            return carry
        lax.fori_loop(0, TC, issue, 0)

    @pl.when(s == 0)
    def _():
        gather(0, 0)

    @pl.when(s + 1 < n)
    def _():
        gather(s + 1, (s + 1) % 2)

    slot = s % 2
    pltpu.make_async_copy(buf.at[slot], buf.at[slot], sem.at[slot]).wait()
    route = route_ref[...]
    f = route[:, 2:3] * buf[slot, 0] + route[:, 3:4] * buf[slot, 1]
    y_ref[...] = _layer_norm(ALPHA * h_ref[...] + f, g2_ref[...], b2_ref[...])


def _combine(pos1, pos2, osort, h, route, g2, b2):
    t = h.shape[0]
    return pl.pallas_call(
        _combine_kernel,
        grid_spec=pltpu.PrefetchScalarGridSpec(
            num_scalar_prefetch=2,
            grid=(t // TC,),
            in_specs=[
                pl.BlockSpec(memory_space=pl.ANY),
                pl.BlockSpec((TC, D_MODEL), lambda i, p1, p2: (i, 0)),
                pl.BlockSpec((TC, LANES), lambda i, p1, p2: (i, 0)),
                pl.BlockSpec((1, D_MODEL), lambda i, p1, p2: (0, 0)),
                pl.BlockSpec((1, D_MODEL), lambda i, p1, p2: (0, 0)),
            ],
            out_specs=pl.BlockSpec((TC, D_MODEL), lambda i, p1, p2: (i, 0)),
            scratch_shapes=[pltpu.VMEM((2, 2, TC, D_MODEL), F32), pltpu.SemaphoreType.DMA((2,))],
        ),
        out_shape=jax.ShapeDtypeStruct((t, D_MODEL), F32),
        compiler_params=pltpu.CompilerParams(dimension_semantics=("arbitrary",)),
        name="moe_combine",
    )(pos1, pos2, osort, h, route, g2, b2)


def _routing_tables(route, counts, n_slots):
    t = route.shape[0]
    e1 = route[:, 0].astype(jnp.int32)
    e2 = route[:, 1].astype(jnp.int32)
    r1 = route[:, 4].astype(jnp.int32)
    r2 = route[:, 5].astype(jnp.int32)
    cnt = counts[:, 0, :N_EXPERTS].astype(jnp.int32)
    tot = jnp.sum(cnt, axis=0)
    padded = ((tot + TG - 1) // TG) * TG
    gend = jnp.cumsum(padded)
    gstart = gend - padded
    base = gstart[None, :] + jnp.cumsum(cnt, axis=0) - cnt
    base_tok = jnp.broadcast_to(base[:, None, :], (t // TM, TM, N_EXPERTS)).reshape(t, N_EXPERTS)
    experts = jnp.arange(N_EXPERTS, dtype=jnp.int32)[None, :]
    pos1 = jnp.sum(jnp.where(e1[:, None] == experts, base_tok, 0), axis=1) + r1
    pos2 = jnp.sum(jnp.where(e2[:, None] == experts, base_tok, 0), axis=1) + r2
    starts = jnp.arange(n_slots // TG, dtype=jnp.int32) * TG
    n_ended = jnp.sum(starts[:, None] >= gend[None, :], axis=1).astype(jnp.int32)
    tile_active = (starts < gend[-1]).astype(jnp.int32)
    last_active = (gend[-1] // TG - 1).astype(jnp.int32).reshape(1)
    last_expert = jnp.minimum(n_ended[last_active[0]], N_EXPERTS - 1)
    tile_expert = jnp.where(tile_active > 0, jnp.minimum(n_ended, N_EXPERTS - 1), last_expert).astype(jnp.int32)
    pad_lo = jnp.concatenate([gstart + tot, gend[-1:]]).astype(jnp.int32)
    pad_hi = jnp.concatenate([gend, jnp.full((1,), n_slots, jnp.int32)]).astype(jnp.int32)
    return pos1, pos2, pad_lo, pad_hi, tile_expert, tile_active, last_active


def _sample_attn_kernel(x_ref, rope_ref, win_ref, sink_ref, ck_ref, cv_ref, cmk_ref, cmv_ref,
                        z_ref, oa_ref, oc_ref, nk_ref, nv_ref):
    j = pl.program_id(0)
    db = x_ref.shape[0]

    @pl.when(j == 0)
    def _():
        xb = x_ref[...].astype(BF16)
        for c0 in range(0, IN_WIDTH, IN_CHUNK):
            z_ref[:, c0:c0 + IN_CHUNK] = _dot(xb, win_ref[:, c0:c0 + IN_CHUNK])
        c = rope_ref[0]
        s1 = rope_ref[1]
        s2 = rope_ref[2]
        for jj in range((Q_WIDTH + KV_WIDTH) // LANES):
            sl = slice(jj * LANES, (jj + 1) * LANES)
            z_ref[:, sl] = _rope(z_ref[:, sl], c, s1, s2)

    r0 = pl.multiple_of(j * SB, SB)
    zq = z_ref[pl.ds(r0, SB), Q0:Q0 + Q_WIDTH]
    zk = z_ref[pl.ds(r0, SB), K0:K0 + KV_WIDTH]
    zv = z_ref[pl.ds(r0, SB), V0:V0 + KV_WIDTH]
    zc = z_ref[pl.ds(r0, SB), CQ0:CQ0 + MEM_WIDTH]
    sink = sink_ref[:, 0:1]
    row_kv = lax.broadcasted_iota(jnp.int32, (N_HEADS, KV_WIDTH), 0) & (N_KV - 1)
    lane_kv = lax.broadcasted_iota(jnp.int32, (N_HEADS, KV_WIDTH), 1) // HEAD_DIM
    own = row_kv == lane_kv
    row_c = lax.broadcasted_iota(jnp.int32, (N_HEADS, MEM_WIDTH), 0)
    lane_c = lax.broadcasted_iota(jnp.int32, (N_HEADS, MEM_WIDTH), 1) // HEAD_DIM
    own_c = row_c == lane_c
    last_row = lax.broadcasted_iota(jnp.int32, (WINDOW, KV_WIDTH), 0) == WINDOW - 1

    oa_rows = [[] for _ in range(GROUP)]
    oc_rows = []
    for b in range(SB):
        kc = ck_ref[b]
        vc = cv_ref[b]
        knew = zk[b:b + 1, :]
        vnew = zv[b:b + 1, :]
        qblk = jnp.concatenate(
            [jnp.broadcast_to(zq[b:b + 1, g * KV_WIDTH:(g + 1) * KV_WIDTH], (N_KV, KV_WIDTH)) for g in range(GROUP)],
            axis=0)
        qblk = jnp.where(own, qblk, 0.0).astype(BF16)
        s = _dot_nt(qblk, kc.astype(BF16))
        s_new = jnp.sum(qblk.astype(F32) * knew.astype(BF16).astype(F32), axis=1, keepdims=True)
        m = jnp.maximum(jnp.maximum(jnp.max(s, axis=1, keepdims=True), s_new), sink)
        p = jnp.exp(s - m)
        p_new = jnp.exp(s_new - m)
        inv = 1.0 / (jnp.sum(p, axis=1, keepdims=True) + p_new + jnp.exp(sink - m))
        o = (_dot((p * inv).astype(BF16), vc.astype(BF16))
             + (p_new * inv).astype(BF16).astype(F32) * vnew.astype(BF16).astype(F32))
        o = jnp.where(own, o, 0.0)
        for g in range(GROUP):
            oa_rows[g].append(jnp.sum(o[g * N_KV:(g + 1) * N_KV], axis=0, keepdims=True))
        nk_ref[b] = jnp.where(last_row, knew, pltpu.roll(kc, WINDOW - 1, 0))
        nv_ref[b] = jnp.where(last_row, vnew, pltpu.roll(vc, WINDOW - 1, 0))
        cblk = jnp.where(own_c, jnp.broadcast_to(zc[b:b + 1, :], (N_HEADS, MEM_WIDTH)), 0.0)
        sc = _dot_nt(cblk.astype(BF16), cmk_ref[b].astype(BF16))
        mc = jnp.max(sc, axis=1, keepdims=True)
        pc = jnp.exp(sc - mc)
        pc = pc * (1.0 / jnp.sum(pc, axis=1, keepdims=True))
        ocb = jnp.where(own_c, _dot(pc.astype(BF16), cmv_ref[b].astype(BF16)), 0.0)
        oc_rows.append(jnp.sum(ocb, axis=0, keepdims=True))
    for g in range(GROUP):
        oa_ref[pl.ds(r0, SB), g * KV_WIDTH:(g + 1) * KV_WIDTH] = jnp.concatenate(oa_rows[g], axis=0)
    oc_ref[pl.ds(r0, SB), :] = jnp.concatenate(oc_rows, axis=0)


def _sample_attn(x, rope, win, sink_gk, ck, cv, cmk, cmv):
    db = x.shape[0]
    blk = lambda r: pl.BlockSpec((SB, r, KV_WIDTH), lambda j: (j, 0, 0))
    full = lambda w: pl.BlockSpec((db, w), lambda j: (0, 0))
    return pl.pallas_call(
        _sample_attn_kernel,
        grid=(db // SB,),
        in_specs=[
            full(D_MODEL), _const_spec((3, 1, LANES)), _const_spec((D_MODEL, IN_WIDTH)),
            _const_spec((N_HEADS, LANES)),
            blk(WINDOW), blk(WINDOW), blk(N_MEM), blk(N_MEM),
        ],
        out_specs=[full(IN_WIDTH), full(Q_WIDTH), full(MEM_WIDTH), blk(WINDOW), blk(WINDOW)],
        out_shape=[
            jax.ShapeDtypeStruct((db, IN_WIDTH), F32),
            jax.ShapeDtypeStruct((db, Q_WIDTH), F32),
            jax.ShapeDtypeStruct((db, MEM_WIDTH), F32),
            jax.ShapeDtypeStruct((db, WINDOW, KV_WIDTH), F32),
            jax.ShapeDtypeStruct((db, WINDOW, KV_WIDTH), F32),
        ],
        compiler_params=pltpu.CompilerParams(dimension_semantics=("arbitrary",), vmem_limit_bytes=VMEM_LIMIT),
        name="sample_attn",
    )(x, rope, win, sink_gk, ck, cv, cmk, cmv)


def _sample_tail_kernel(x_ref, z_ref, oa_ref, oc_ref, st_ref, wa_ref, wb_ref, wc_ref, wo_ref, wmix_ref,
                        pscale_ref, g1_ref, b1_ref, wr_ref, br_ref, wg_ref, wu_ref, wd_ref, g2_ref, b2_ref,
                        y_ref, npool_ref, h_sc, comb_sc, acc_sc):
    e = pl.program_id(0)

    @pl.when(e == 0)
    def _():
        u = z_ref[:, U0:U0 + POOL_WIDTH]
        npool_ref[:, 0:(POOL_STATE - 1) * POOL_WIDTH] = st_ref[:, POOL_WIDTH:POOL_STATE * POOL_WIDTH]
        npool_ref[:, (POOL_STATE - 1) * POOL_WIDTH:POOL_STATE * POOL_WIDTH] = u
        obs = []
        for g, w in enumerate(POOL_WINDOWS):
            sl = slice(g * POOL_GROUP_DIM, (g + 1) * POOL_GROUP_DIM)
            cur = u[:, sl]
            ws = cur
            for jj in range(1, w):
                base = (POOL_STATE - jj) * POOL_WIDTH
                ws = ws + st_ref[:, base + g * POOL_GROUP_DIM:base + (g + 1) * POOL_GROUP_DIM]
            cnt = float(min(PAST_LEN + 1, w))
            pooled = ws / cnt - cur
            obs.append(_dot(pooled.astype(BF16), wmix_ref[g]) * pscale_ref[:, sl])
        ob = jnp.concatenate(obs, axis=1)
        h = _merge_ln1(x_ref[...], oa_ref[...].astype(BF16), ob, oc_ref[...], z_ref[:, GZ0:GZ0 + 3 * D_MODEL],
                       wa_ref, wb_ref, wc_ref, wo_ref, g1_ref[...], b1_ref[...])
        h_sc[...] = h
        logits = _dot(h.astype(BF16), wr_ref[...]) + br_ref[...]
        _, _, (hot1, hot2, w1, w2) = _route(logits, None)
        comb_sc[...] = jnp.where(hot1, w1, 0.0) + jnp.where(hot2, w2, 0.0)
        acc_sc[...] = jnp.zeros_like(acc_sc)

    out = _expert_mlp(h_sc[...].astype(BF16), wg_ref[...], wu_ref[...], wd_ref[...])
    lane = lax.broadcasted_iota(jnp.int32, comb_sc.shape, 1)
    ce = jnp.sum(jnp.where(lane == e, comb_sc[...], 0.0), axis=1, keepdims=True)
    acc_sc[...] += ce * out

    @pl.when(e == pl.num_programs(0) - 1)
    def _():
        y_ref[...] = _layer_norm(ALPHA * h_sc[...] + acc_sc[...], g2_ref[...], b2_ref[...])


def _sample_tail(x, z, oa, oc, state, wa, wb, wc, wo, wmix, pscale, g1, b1, wr, br, wg, wu, wd, g2, b2):
    db = x.shape[0]
    full = lambda w: pl.BlockSpec((db, w), lambda e: (0, 0))
    vec = lambda w: pl.BlockSpec((1, w), lambda e: (0, 0))
    return pl.pallas_call(
        _sample_tail_kernel,
        grid=(N_EXPERTS,),
        in_specs=[
            full(D_MODEL), full(IN_WIDTH), full(Q_WIDTH), full(MEM_WIDTH), full(POOL_STATE * POOL_WIDTH),
            _const_spec((Q_WIDTH, D_MODEL)), _const_spec((POOL_WIDTH, D_MODEL)),
            _const_spec((MEM_WIDTH, D_MODEL)), _const_spec((D_MODEL, D_MODEL)),
            _const_spec((len(POOL_WINDOWS), POOL_GROUP_DIM, POOL_GROUP_DIM)),
            vec(POOL_WIDTH), vec(D_MODEL), vec(D_MODEL),
            _const_spec((D_MODEL, LANES)), vec(LANES),
            pl.BlockSpec((None, D_MODEL, D_EXPERT), lambda e: (e, 0, 0)),
            pl.BlockSpec((None, D_MODEL, D_EXPERT), lambda e: (e, 0, 0)),
            pl.BlockSpec((None, D_EXPERT, D_MODEL), lambda e: (e, 0, 0)),
            vec(D_MODEL), vec(D_MODEL),
        ],
        out_specs=[full(D_MODEL), full(POOL_STATE * POOL_WIDTH)],
        out_shape=[jax.ShapeDtypeStruct((db, D_MODEL), F32),
                   jax.ShapeDtypeStruct((db, POOL_STATE * POOL_WIDTH), F32)],
        scratch_shapes=[pltpu.VMEM((db, D_MODEL), F32), pltpu.VMEM((db, LANES), F32),
                        pltpu.VMEM((db, D_MODEL), F32)],
        compiler_params=pltpu.CompilerParams(dimension_semantics=("arbitrary",), vmem_limit_bytes=VMEM_LIMIT),
        name="sample_tail",
    )(x, z, oa, oc, state, wa, wb, wc, wo, wmix, pscale, g1, b1, wr, br, wg, wu, wd, g2, b2)


def _rope_tables(pos):
    half = ROPE_DIM // 2
    inv = jnp.power(ROPE_THETA, -jnp.arange(half, dtype=F32) * (2.0 / ROPE_DIM))
    ang = pos.astype(F32)[:, None] * inv[None, :]
    lane = np.arange(LANES)
    off = lane % HEAD_DIM
    cos = jnp.cos(ang)[:, lane % half]
    sin = jnp.sin(ang)[:, lane % half]
    c = jnp.where(off[None, :] < ROPE_DIM, cos, 1.0)
    s1 = jnp.where((off[None, :] >= half) & (off[None, :] < ROPE_DIM), sin, 0.0)
    s2 = jnp.where(off[None, :] < half, -sin, 0.0)
    return jnp.stack([c, s1, s2]).astype(F32)


def _q_heads_group_major(w, axis):
    if axis == 1:
        n = w.shape[0]
        return w.reshape(n, N_KV, GROUP, HEAD_DIM).transpose(0, 2, 1, 3).reshape(n, Q_WIDTH)
    n = w.shape[1]
    return w.reshape(N_KV, GROUP, HEAD_DIM, n).transpose(1, 0, 2, 3).reshape(Q_WIDTH, n)


def kernel(x_prompt, x_sample, cache_win_k, cache_win_v, state_pool, cache_mem_k, cache_mem_v, mem_prompt, w_in, sinks, w_pool_mix, pool_scale, w_mem_k, w_mem_v, w_branch_a, w_branch_b, w_branch_c, w_out, ln1_g, ln1_b, w_group, b_group, w_router, b_router, w_gate, w_up, w_down, ln2_g, ln2_b):
    assert w_in.shape[0] == DEPTH == 1
    b, l, _ = x_prompt.shape
    db, ds, _ = x_sample.shape
    assert ds == 1 and l % TM == 0 and db % SB == 0 and (b * l) % TD == 0 and (b * l) % TC == 0
    assert cache_win_k.shape[2] == WINDOW
    t = b * l

    win = w_in[0]
    scale = HEAD_DIM ** -0.5
    wq = _q_heads_group_major(win[:, Q0:Q0 + Q_WIDTH], 1) * scale
    wcq = win[:, CQ0:CQ0 + MEM_WIDTH] * scale
    win_b = jnp.concatenate([wq, win[:, K0:CQ0], wcq, win[:, GZ0:]], axis=1).astype(BF16)
    wa = _q_heads_group_major(w_branch_a[0], 0).astype(BF16)
    wb = w_branch_b[0].astype(BF16)
    wc = w_branch_c[0].astype(BF16)
    wo = w_out[0].astype(BF16)
    wmix = w_pool_mix[0].astype(BF16)
    pscale = pool_scale[0].reshape(1, POOL_WIDTH)
    g1 = ln1_g[0].reshape(1, D_MODEL)
    b1 = ln1_b[0].reshape(1, D_MODEL)
    g2 = ln2_g[0].reshape(1, D_MODEL)
    b2 = ln2_b[0].reshape(1, D_MODEL)
    wr = jnp.concatenate([w_group[0], w_router[0].reshape(D_MODEL, N_EXPERTS)], axis=1)
    wr = jnp.pad(wr, ((0, 0), (0, LANES - wr.shape[1]))).astype(BF16)
    br = jnp.pad(jnp.concatenate([b_group[0], b_router[0].reshape(N_EXPERTS)]), (0, LANES - N_EXPERT_GROUPS - N_EXPERTS))
    br = br.reshape(1, LANES).astype(F32)
    wg = w_gate[0].astype(BF16)
    wu = w_up[0].astype(BF16)
    wd = w_down[0].astype(BF16)
    sink = sinks[0].astype(F32)
    sink_gk = jnp.broadcast_to(sink.reshape(N_KV, GROUP).T.reshape(N_HEADS, 1), (N_HEADS, LANES))

    mk, mv = _mem_project(mem_prompt, w_mem_k[0].astype(BF16), w_mem_v[0].astype(BF16))
    rope_p = _rope_tables(jnp.arange(l, dtype=jnp.int32))
    h, route, counts, nk_p, nv_p, npool_p = _front(
        x_prompt, rope_p, sink, win_b, wa, wb, wc, wo, wmix, pscale, mk, mv, g1, b1, wr, br)
    h = h.reshape(t, D_MODEL)
    route = route.reshape(t, LANES)
    n_slots = 2 * t + N_EXPERTS * TG
    pos1, pos2, pad_lo, pad_hi, tile_expert, tile_active, last_active = _routing_tables(
        route, counts.reshape(-1, SUBLANES, LANES), n_slots)
    hs = _dispatch(pos1, pos2, pad_lo, pad_hi, h, n_slots)
    osort = _grouped_gemm(tile_expert, tile_active, last_active, hs, wg, wu, wd)
    y_p = _combine(pos1, pos2, osort, h, route, g2, b2).reshape(b, l, D_MODEL)

    rope_s = _rope_tables(jnp.full((1,), PAST_LEN, jnp.int32))
    xs = x_sample.reshape(db, D_MODEL)
    ck = cache_win_k[0].reshape(db, WINDOW, KV_WIDTH)
    cv = cache_win_v[0].reshape(db, WINDOW, KV_WIDTH)
    cmk = cache_mem_k[0].reshape(db, N_MEM, MEM_WIDTH)
    cmv = cache_mem_v[0].reshape(db, N_MEM, MEM_WIDTH)
    z_s, oa_s, oc_s, nk_s, nv_s = _sample_attn(xs, rope_s, win_b, sink_gk, ck, cv, cmk, cmv)
    state = state_pool[0].reshape(db, POOL_STATE * POOL_WIDTH)
    y_s, npool_s = _sample_tail(xs, z_s, oa_s, oc_s, state, wa, wb, wc, wo, wmix, pscale, g1, b1, wr, br,
                                wg, wu, wd, g2, b2)

    kv5 = lambda a, n, w: a.reshape(1, n, w, N_KV, HEAD_DIM)
    return (y_p, y_s.reshape(db, 1, D_MODEL),
            kv5(nk_p, b, QB), kv5(nv_p, b, QB),
            npool_p[:, 2 * SUBLANES - POOL_STATE:, :][None],
            kv5(mk, b, N_MEM), kv5(mv, b, N_MEM),
            kv5(nk_s, db, WINDOW), kv5(nv_s, db, WINDOW),
            npool_s.reshape(1, db, POOL_STATE, POOL_WIDTH))
```

```python
import functools

import jax
import jax.numpy as jnp
import numpy as np
from jax import lax
from jax.experimental import pallas as pl
from jax.experimental.pallas import tpu as pltpu

D_MODEL = 1024
N_HEADS = 16
HEAD_DIM = 64
N_KV = 4
GROUP = N_HEADS // N_KV
WINDOW = 128
ROPE_THETA = 500000.0
ROPE_DIM = HEAD_DIM // 4
Q_WIDTH = N_HEADS * HEAD_DIM
KV_WIDTH = N_KV * HEAD_DIM
POOL_WINDOWS = (2, 4, 8, 16)
POOL_WIDTH = D_MODEL // 2
POOL_GROUP_DIM = POOL_WIDTH // len(POOL_WINDOWS)
POOL_STATE = max(POOL_WINDOWS) - 1
N_MEM = 256
MEM_HEADS = 4
MEM_WIDTH = MEM_HEADS * HEAD_DIM
N_EXPERT_GROUPS = 4
EXPERTS_PER_GROUP = 4
N_EXPERTS = N_EXPERT_GROUPS * EXPERTS_PER_GROUP
D_EXPERT = 512
PAST_LEN = 16384
DEPTH = 1
ALPHA = (2.0 * DEPTH) ** 0.25
LN_EPS = 1e-5

Q0 = 0
K0 = Q0 + Q_WIDTH
V0 = K0 + KV_WIDTH
U0 = V0 + KV_WIDTH
CQ0 = U0 + POOL_WIDTH
GZ0 = CQ0 + MEM_WIDTH
IN_WIDTH = GZ0 + 3 * D_MODEL

LANES = 128
SUBLANES = 8
VMEM_LIMIT = 56 * 1024 * 1024

TM = 256
QB = WINDOW
PIECE = 16
PIECES_PER_TILE = 16
TG = PIECE * PIECES_PER_TILE
MAX_CHUNK_PIECES = 2 * TM // PIECE + N_EXPERTS - 1
CAP = -(-MAX_CHUNK_PIECES * PIECE // LANES) * LANES
SB = 8
IN_CHUNK = 768

BF16 = jnp.bfloat16
F32 = jnp.float32
NEG_INF = float("-inf")


def _const_spec(shape):
    nd = len(shape)
    return pl.BlockSpec(shape, lambda *_: (0,) * nd, pipeline_mode=pl.Buffered(1))


def _layer_norm(x, g, b):
    mu = jnp.mean(x, axis=-1, keepdims=True)
    xc = x - mu
    var = jnp.mean(xc * xc, axis=-1, keepdims=True)
    return xc * lax.rsqrt(var + LN_EPS) * g + b


def _dot(a, b):
    return jnp.dot(a, b, preferred_element_type=F32)


def _dot_nt(a, b):
    return lax.dot_general(a, b, (((1,), (1,)), ((), ())), preferred_element_type=F32)


def _lane_block_mask(shape, block, width=HEAD_DIM):
    lane = lax.broadcasted_iota(jnp.int32, shape, len(shape) - 1)
    return (lane >= block * width) & (lane < (block + 1) * width)


def _rope(x, c, s1, s2):
    half = ROPE_DIM // 2
    return x * c + pltpu.roll(x, half, 1) * s1 + pltpu.roll(x, LANES - half, 1) * s2


def _route(logits):
    rows = logits.shape[0]
    lane = lax.broadcasted_iota(jnp.int32, (rows, LANES), 1)
    lanef = lane.astype(F32)
    big = float(LANES)
    is_g = lane < N_EXPERT_GROUPS
    glog = jnp.where(is_g, logits, NEG_INF)
    gmax = jnp.max(glog, axis=1, keepdims=True)
    gsum = jnp.sum(jnp.where(is_g, jnp.exp(glog - gmax), 0.0), axis=1, keepdims=True)
    gp = 1.0 / gsum
    gidx = jnp.min(jnp.where(glog == gmax, lanef, big), axis=1, keepdims=True).astype(jnp.int32)
    lo = N_EXPERT_GROUPS + gidx * EXPERTS_PER_GROUP
    in_grp = (lane >= lo) & (lane < lo + EXPERTS_PER_GROUP)
    el = jnp.where(in_grp, logits, NEG_INF)
    v1 = jnp.max(el, axis=1, keepdims=True)
    i1 = jnp.min(jnp.where(el == v1, lanef, big), axis=1, keepdims=True).astype(jnp.int32)
    el2 = jnp.where(lane == i1, NEG_INF, el)
    v2 = jnp.max(el2, axis=1, keepdims=True)
    i2 = jnp.min(jnp.where(el2 == v2, lanef, big), axis=1, keepdims=True).astype(jnp.int32)
    e21 = jnp.exp(v2 - v1)
    inv = 1.0 / (1.0 + e21)
    w1 = inv * gp
    w2 = e21 * inv * gp
    e1 = i1 - N_EXPERT_GROUPS
    e2 = i2 - N_EXPERT_GROUPS
    return lane == e1, lane == e2, w1, w2


def _local_sort(hot1, hot2, w1, w2, hb):
    rows = hb.shape[0]
    lane = lax.broadcasted_iota(jnp.int32, (rows, LANES), 1)
    onehot = jnp.where(hot1 | hot2, 1.0, 0.0)
    counts = jnp.sum(onehot, axis=0, keepdims=True)
    before = (lax.broadcasted_iota(jnp.int32, (rows, rows), 1)
              < lax.broadcasted_iota(jnp.int32, (rows, rows), 0)).astype(BF16)
    rank = _dot(before, onehot.astype(BF16))
    cap = (((counts.astype(jnp.int32) + (PIECE - 1)) // PIECE) * PIECE).astype(F32)
    lower = (lax.broadcasted_iota(jnp.int32, (LANES, LANES), 0)
             < lax.broadcasted_iota(jnp.int32, (LANES, LANES), 1)).astype(BF16)
    start = _dot(jnp.broadcast_to(cap, (SUBLANES, LANES)).astype(BF16), lower)[0:1]
    slot = start + rank
    s1 = jnp.sum(jnp.where(hot1, slot, 0.0), axis=1, keepdims=True)
    s2 = jnp.sum(jnp.where(hot2, slot, 0.0), axis=1, keepdims=True)
    route = jnp.where(lane == 0, w1, jnp.where(lane == 1, w2, jnp.where(lane == 2, s1, jnp.where(lane == 3, s2, 0.0))))
    route_t = route.T
    srow = lax.broadcasted_iota(jnp.int32, (CAP, rows), 0).astype(F32)
    perm = jnp.where((srow == route_t[2:3, :]) | (srow == route_t[3:4, :]), 1.0, 0.0).astype(BF16)
    return _dot(perm, hb).astype(BF16), route, counts


def _merge_ln1(x, oa, ob, oc, gz, wa_ref, wb_ref, wc_ref, wo_ref, g1, b1):
    ya = _dot(oa, wa_ref[...])
    yb = _dot(ob.astype(BF16), wb_ref[...])
    yc = _dot(oc.astype(BF16), wc_ref[...])
    m = (jax.nn.sigmoid(gz[:, 0:D_MODEL]) * ya
         + jax.nn.sigmoid(gz[:, D_MODEL:2 * D_MODEL]) * yb
         + jax.nn.sigmoid(gz[:, 2 * D_MODEL:3 * D_MODEL]) * yc)
    hpre = ALPHA * x + _dot(m.astype(BF16), wo_ref[...])
    return _layer_norm(hpre, g1, b1)


def _mem_kernel(mem_ref, wk_ref, wv_ref, mk_ref, mv_ref):
    m = mem_ref[...].astype(BF16)
    mk_ref[...] = _dot(m, wk_ref[...])
    mv_ref[...] = _dot(m, wv_ref[...])


def _mem_project(mem, wk, wv):
    b = mem.shape[0]
    out = jax.ShapeDtypeStruct((b, N_MEM, MEM_WIDTH), F32)
    return pl.pallas_call(
        _mem_kernel,
        grid=(b,),
        in_specs=[pl.BlockSpec((None, N_MEM, D_MODEL), lambda i: (i, 0, 0)),
                  _const_spec((D_MODEL, MEM_WIDTH)), _const_spec((D_MODEL, MEM_WIDTH))],
        out_specs=[pl.BlockSpec((None, N_MEM, MEM_WIDTH), lambda i: (i, 0, 0))] * 2,
        out_shape=[out, out],
        name="mem_project",
    )(mem, wk, wv)


def _front_kernel(sinks_ref, x_ref, rope_ref, win_ref, wa_ref, wb_ref, wc_ref, wo_ref, wmix_ref,
                  pscale_ref, mk_ref, mv_ref, g1_ref, b1_ref, wr_ref, br_ref,
                  h_ref, xs_ref, route_ref, counts_ref, nk_ref, nv_ref, npool_ref,
                  z_ref, qb_ref, kext_ref, vext_ref, uext_ref, oa_ref, ob_ref):
    i = pl.program_id(1)
    x = x_ref[...]
    xb = x.astype(BF16)

    @pl.when(i == 0)
    def _():
        kext_ref[0:QB, :] = jnp.zeros((QB, KV_WIDTH), BF16)
        vext_ref[0:QB, :] = jnp.zeros((QB, KV_WIDTH), BF16)
        uext_ref[0:2 * SUBLANES, :] = jnp.zeros((2 * SUBLANES, POOL_WIDTH), F32)

    @pl.when(i > 0)
    def _():
        kext_ref[0:QB, :] = kext_ref[TM:TM + QB, :]
        vext_ref[0:QB, :] = vext_ref[TM:TM + QB, :]
        uext_ref[0:2 * SUBLANES, :] = uext_ref[TM:TM + 2 * SUBLANES, :]

    for c0 in range(0, IN_WIDTH, IN_CHUNK):
        z_ref[:, c0:c0 + IN_CHUNK] = _dot(xb, win_ref[:, c0:c0 + IN_CHUNK])

    c = rope_ref[0]
    s1 = rope_ref[1]
    s2 = rope_ref[2]
    for j in range(Q_WIDTH // LANES):
        sl = slice(Q0 + j * LANES, Q0 + (j + 1) * LANES)
        qb_ref[:, j * LANES:(j + 1) * LANES] = _rope(z_ref[:, sl], c, s1, s2).astype(BF16)
    for j in range(KV_WIDTH // LANES):
        sl = slice(K0 + j * LANES, K0 + (j + 1) * LANES)
        kr = _rope(z_ref[:, sl], c, s1, s2)
        z_ref[:, sl] = kr
        kext_ref[QB:QB + TM, j * LANES:(j + 1) * LANES] = kr.astype(BF16)
    vext_ref[QB:QB + TM, :] = z_ref[:, V0:V0 + KV_WIDTH].astype(BF16)
    nk_ref[...] = z_ref[TM - QB:TM, K0:K0 + KV_WIDTH]
    nv_ref[...] = z_ref[TM - QB:TM, V0:V0 + KV_WIDTH]

    rowq = lax.broadcasted_iota(jnp.int32, (GROUP * QB, 2 * QB), 0) & (QB - 1)
    colk = lax.broadcasted_iota(jnp.int32, (GROUP * QB, 2 * QB), 1)
    band = (colk >= rowq) & (colk <= rowq + WINDOW)
    rowg = lax.broadcasted_iota(jnp.int32, (GROUP * QB, 1), 0) // QB
    for sb in range(TM // QB):
        k2 = kext_ref[sb * QB:(sb + 2) * QB, :]
        v2 = vext_ref[sb * QB:(sb + 2) * QB, :]
        if sb == 0:
            mask = band & ((colk >= QB) | (i > 0))
        else:
            mask = band
        qs = jnp.concatenate(
            [qb_ref[sb * QB:(sb + 1) * QB, g * KV_WIDTH:(g + 1) * KV_WIDTH] for g in range(GROUP)], axis=0)
        acc = jnp.zeros((GROUP * QB, KV_WIDTH), F32)
        for kv in range(N_KV):
            kmask = _lane_block_mask((2 * QB, KV_WIDTH), kv)
            kkv = jnp.where(kmask, k2, jnp.zeros_like(k2))
            s = jnp.where(mask, _dot_nt(qs, kkv), NEG_INF)
            sink = jnp.zeros((GROUP * QB, 1), F32)
            for g in range(GROUP):
                sink = jnp.where(rowg == g, sinks_ref[kv * GROUP + g], sink)
            m = jnp.maximum(jnp.max(s, axis=1, keepdims=True), sink)
            p = jnp.exp(s - m)
            den = jnp.sum(p, axis=1, keepdims=True) + jnp.exp(sink - m)
            o = _dot((p * (1.0 / den)).astype(BF16), v2)
            acc = jnp.where(_lane_block_mask((GROUP * QB, KV_WIDTH), kv), o, acc)
        for g in range(GROUP):
            oa_ref[sb * QB:(sb + 1) * QB, g * KV_WIDTH:(g + 1) * KV_WIDTH] = acc[g * QB:(g + 1) * QB].astype(BF16)

    hist = 2 * SUBLANES
    uext_ref[hist:hist + TM, :] = z_ref[:, U0:U0 + POOL_WIDTH]
    npool_ref[...] = uext_ref[TM:TM + hist, :]
    pos = i * TM + lax.broadcasted_iota(jnp.int32, (TM, 1), 0)
    for g, w in enumerate(POOL_WINDOWS):
        sl = slice(g * POOL_GROUP_DIM, (g + 1) * POOL_GROUP_DIM)
        cur = uext_ref[hist:hist + TM, sl]
        ws = cur
        for j in range(1, w):
            ws = ws + uext_ref[hist - j:hist - j + TM, sl]
        cnt = jnp.minimum(pos + 1, w).astype(F32)
        pooled = ws / cnt - cur
        ob_ref[:, sl] = _dot(pooled.astype(BF16), wmix_ref[g]) * pscale_ref[:, sl]

    cq = z_ref[:, CQ0:CQ0 + MEM_WIDTH].astype(BF16)
    mk = mk_ref[...].astype(BF16)
    mv = mv_ref[...].astype(BF16)
    oc = jnp.zeros((TM, MEM_WIDTH), F32)
    for hh in range(MEM_HEADS):
        kh = jnp.where(_lane_block_mask((N_MEM, MEM_WIDTH), hh), mk, jnp.zeros_like(mk))
        s = _dot_nt(cq, kh)
        m = jnp.max(s, axis=1, keepdims=True)
        p = jnp.exp(s - m)
        den = jnp.sum(p, axis=1, keepdims=True)
        o = _dot((p * (1.0 / den)).astype(BF16), mv)
        oc = jnp.where(_lane_block_mask((TM, MEM_WIDTH), hh), o, oc)

    h = _merge_ln1(x, oa_ref[...], ob_ref[...], oc, z_ref[:, GZ0:GZ0 + 3 * D_MODEL],
                   wa_ref, wb_ref, wc_ref, wo_ref, g1_ref[...], b1_ref[...])
    h_ref[...] = h
    hb = h.astype(BF16)
    logits = _dot(hb, wr_ref[...]) + br_ref[...]
    xs, route, counts = _local_sort(*_route(logits), hb)
    xs_ref[...] = xs
    route_ref[...] = route
    counts_ref[...] = jnp.broadcast_to(counts, (SUBLANES, LANES))


def _front(x, rope, sinks, win, wa, wb, wc, wo, wmix, pscale, mk, mv, g1, b1, wr, br):
    b, l, _ = x.shape
    nt = l // TM
    hist = 2 * SUBLANES
    tile = lambda w: pl.BlockSpec((None, TM, w), lambda bi, ti: (bi, ti, 0))
    per_b = lambda r, w: pl.BlockSpec((None, r, w), lambda bi, ti: (bi, 0, 0))
    return pl.pallas_call(
        _front_kernel,
        grid=(b, nt),
        in_specs=[
            pl.BlockSpec(memory_space=pltpu.SMEM),
            tile(D_MODEL),
            pl.BlockSpec((3, TM, LANES), lambda bi, ti: (0, ti, 0)),
            _const_spec((D_MODEL, IN_WIDTH)),
            _const_spec((Q_WIDTH, D_MODEL)), _const_spec((POOL_WIDTH, D_MODEL)),
            _const_spec((MEM_WIDTH, D_MODEL)), _const_spec((D_MODEL, D_MODEL)),
            _const_spec((len(POOL_WINDOWS), POOL_GROUP_DIM, POOL_GROUP_DIM)),
            _const_spec((1, POOL_WIDTH)),
            per_b(N_MEM, MEM_WIDTH), per_b(N_MEM, MEM_WIDTH),
            _const_spec((1, D_MODEL)), _const_spec((1, D_MODEL)),
            _const_spec((D_MODEL, LANES)), _const_spec((1, LANES)),
        ],
        out_specs=[
            tile(D_MODEL),
            pl.BlockSpec((CAP, D_MODEL), lambda bi, ti: (bi * nt + ti, 0)),
            tile(LANES),
            pl.BlockSpec((None, None, SUBLANES, LANES), lambda bi, ti: (bi, ti, 0, 0)),
            per_b(QB, KV_WIDTH), per_b(QB, KV_WIDTH), per_b(hist, POOL_WIDTH),
        ],
        out_shape=[
            jax.ShapeDtypeStruct((b, l, D_MODEL), F32),
            jax.ShapeDtypeStruct((b * nt * CAP, D_MODEL), BF16),
            jax.ShapeDtypeStruct((b, l, LANES), F32),
            jax.ShapeDtypeStruct((b, nt, SUBLANES, LANES), F32),
            jax.ShapeDtypeStruct((b, QB, KV_WIDTH), F32),
            jax.ShapeDtypeStruct((b, QB, KV_WIDTH), F32),
            jax.ShapeDtypeStruct((b, hist, POOL_WIDTH), F32),
        ],
        scratch_shapes=[
            pltpu.VMEM((TM, IN_WIDTH), F32),
            pltpu.VMEM((TM, Q_WIDTH), BF16),
            pltpu.VMEM((QB + TM, KV_WIDTH), BF16),
            pltpu.VMEM((QB + TM, KV_WIDTH), BF16),
            pltpu.VMEM((hist + TM, POOL_WIDTH), F32),
            pltpu.VMEM((TM, Q_WIDTH), BF16),
            pltpu.VMEM((TM, POOL_WIDTH), F32),
        ],
        compiler_params=pltpu.CompilerParams(
            dimension_semantics=("arbitrary", "arbitrary"), vmem_limit_bytes=VMEM_LIMIT),
        name="front_prompt",
    )(sinks, x, rope, win, wa, wb, wc, wo, wmix, pscale, mk, mv, g1, b1, wr, br)


def _expert_mlp(xb, wg, wu, wd):
    a = _dot(xb, wg)
    hid = (a * jax.nn.sigmoid(a)) * _dot(xb, wu)
    return _dot(hid.astype(BF16), wd)


def _gemm_kernel(rows_ref, te_ref, act_ref, xs_ref, wg_ref, wu_ref, wd_ref, ys_ref, xbuf, obuf, sem_in, sem_out):
    i = pl.program_id(0)
    n = pl.num_programs(0)

    def for_pieces(tile, slot, fn):
        for j in range(PIECES_PER_TILE):
            r = rows_ref[tile * PIECES_PER_TILE + j]

            @pl.when(r >= 0)
            def _():
                row0 = pl.multiple_of(r, PIECE)
                fn(row0, j, slot)

    def copy_in(row0, j, slot):
        return pltpu.make_async_copy(xs_ref.at[pl.ds(row0, PIECE)], xbuf.at[slot, pl.ds(j * PIECE, PIECE)], sem_in.at[slot])

    def copy_out(row0, j, slot):
        return pltpu.make_async_copy(obuf.at[slot, pl.ds(j * PIECE, PIECE)], ys_ref.at[pl.ds(row0, PIECE)], sem_out.at[slot])

    start_in = lambda tile, slot: for_pieces(tile, slot, lambda *a: copy_in(*a).start())
    wait_in = lambda tile, slot: for_pieces(tile, slot, lambda *a: copy_in(*a).wait())
    start_out = lambda tile, slot: for_pieces(tile, slot, lambda *a: copy_out(*a).start())
    wait_out = lambda tile, slot: for_pieces(tile, slot, lambda *a: copy_out(*a).wait())

    slot = i % 2

    @pl.when(i == 0)
    def _():
        xbuf[...] = jnp.zeros_like(xbuf)
        start_in(0, 0)

    @pl.when(i + 1 < n)
    def _():
        start_in(i + 1, 1 - slot)

    wait_in(i, slot)

    @pl.when(i >= 2)
    def _():
        wait_out(i - 2, slot)

    @pl.when(act_ref[i] > 0)
    def _():
        obuf[slot] = _expert_mlp(xbuf[slot], wg_ref[...], wu_ref[...], wd_ref[...]).astype(BF16)
        start_out(i, slot)

    @pl.when(i == n - 1)
    def _():
        @pl.when(i >= 1)
        def _():
            wait_out(i - 1, 1 - slot)
        wait_out(i, slot)


def _grouped_gemm(piece_rows, tile_expert, tile_active, xs, wg, wu, wd):
    n_tiles = tile_expert.shape[0]
    wspec = lambda r, c: pl.BlockSpec((None, r, c), lambda i, rows, te, act: (te[i], 0, 0))
    return pl.pallas_call(
        _gemm_kernel,
        grid_spec=pltpu.PrefetchScalarGridSpec(
            num_scalar_prefetch=3,
            grid=(n_tiles,),
            in_specs=[pl.BlockSpec(memory_space=pl.ANY),
                      wspec(D_MODEL, D_EXPERT), wspec(D_MODEL, D_EXPERT), wspec(D_EXPERT, D_MODEL)],
            out_specs=pl.BlockSpec(memory_space=pl.ANY),
            scratch_shapes=[pltpu.VMEM((2, TG, D_MODEL), BF16), pltpu.VMEM((2, TG, D_MODEL), BF16),
                            pltpu.SemaphoreType.DMA((2,)), pltpu.SemaphoreType.DMA((2,))],
        ),
        out_shape=jax.ShapeDtypeStruct(xs.shape, xs.dtype),
        input_output_aliases={3: 0},
        compiler_params=pltpu.CompilerParams(dimension_semantics=("arbitrary",), vmem_limit_bytes=VMEM_LIMIT),
        name="moe_grouped_gemm",
    )(piece_rows, tile_expert, tile_active, xs, wg, wu, wd)


def _combine_kernel(ys_ref, h_ref, route_ref, g2_ref, b2_ref, y_ref):
    route = route_ref[...]
    slot = lax.broadcasted_iota(jnp.int32, (TM, CAP), 1).astype(F32)
    ys = ys_ref[...]
    f = (route[:, 0:1] * _dot(jnp.where(slot == route[:, 2:3], 1.0, 0.0).astype(BF16), ys)
         + route[:, 1:2] * _dot(jnp.where(slot == route[:, 3:4], 1.0, 0.0).astype(BF16), ys))
    y_ref[...] = _layer_norm(ALPHA * h_ref[...] + f, g2_ref[...], b2_ref[...])


def _combine(ys, h, route, g2, b2):
    t = h.shape[0]
    return pl.pallas_call(
        _combine_kernel,
        grid=(t // TM,),
        in_specs=[
            pl.BlockSpec((CAP, D_MODEL), lambda i: (i, 0)),
            pl.BlockSpec((TM, D_MODEL), lambda i: (i, 0)),
            pl.BlockSpec((TM, LANES), lambda i: (i, 0)),
            pl.BlockSpec((1, D_MODEL), lambda i: (0, 0)),
            pl.BlockSpec((1, D_MODEL), lambda i: (0, 0)),
        ],
        out_specs=pl.BlockSpec((TM, D_MODEL), lambda i: (i, 0)),
        out_shape=jax.ShapeDtypeStruct((t, D_MODEL), F32),
        compiler_params=pltpu.CompilerParams(dimension_semantics=("arbitrary",)),
        name="moe_combine",
    )(ys, h, route, g2, b2)


def _select(table, idx):
    hot = idx[:, None] == jnp.arange(table.shape[0], dtype=jnp.int32)[None, :]
    return jnp.sum(jnp.where(hot[:, :, None], table[None, :, :], 0), axis=1)


def _piece_tables(counts):
    n_chunks = counts.shape[0]
    n_tiles = -(-(n_chunks * MAX_CHUNK_PIECES + N_EXPERTS * (PIECES_PER_TILE - 1)) // PIECES_PER_TILE)
    npc = (counts + (PIECE - 1)) // PIECE
    first = (jnp.cumsum(npc, axis=1) - npc).T
    npc_t = npc.T
    cum = jnp.cumsum(npc_t, axis=1)
    per_expert = cum[:, -1]
    tiles_e = (per_expert + (PIECES_PER_TILE - 1)) // PIECES_PER_TILE
    tile_end = jnp.cumsum(tiles_e)
    tile_idx = jnp.arange(n_tiles, dtype=jnp.int32)
    expert_of = lambda i: jnp.minimum(jnp.sum(i[:, None] >= tile_end[None, :], axis=1), N_EXPERTS - 1).astype(jnp.int32)
    active = tile_idx < tile_end[-1]
    tile_expert = jnp.where(active, expert_of(tile_idx), expert_of(tile_end[-1:] - 1))
    meta = jnp.stack([tile_end - tiles_e, per_expert], axis=1)
    meta_t = _select(meta, tile_expert)
    k = (tile_idx - meta_t[:, 0])[:, None] * PIECES_PER_TILE + jnp.arange(PIECES_PER_TILE, dtype=jnp.int32)[None, :]
    valid = active[:, None] & (k < meta_t[:, 1:2])
    cum_t = _select(cum, tile_expert)
    chunk = jnp.minimum(jnp.sum(k[:, :, None] >= cum_t[:, None, :], axis=2), n_chunks - 1).astype(jnp.int32)
    at_chunk = chunk[:, :, None] == jnp.arange(n_chunks, dtype=jnp.int32)[None, None, :]
    pick = lambda tab: jnp.sum(jnp.where(at_chunk, _select(tab, tile_expert)[:, None, :], 0), axis=2)
    piece = pick(first) + k - pick(cum - npc_t)
    rows = jnp.where(valid, chunk * CAP + piece * PIECE, -1).astype(jnp.int32)
    return rows.reshape(-1), tile_expert.astype(jnp.int32), active.astype(jnp.int32)


def _sample_attn_kernel(x_ref, rope_ref, win_ref, sink_ref, ck_ref, cv_ref, cmk_ref, cmv_ref,
                        z_ref, oa_ref, oc_ref, nk_ref, nv_ref):
    j = pl.program_id(0)
    db = x_ref.shape[0]

    @pl.when(j == 0)
    def _():
        xb = x_ref[...].astype(BF16)
        for c0 in range(0, IN_WIDTH, IN_CHUNK):
            z_ref[:, c0:c0 + IN_CHUNK] = _dot(xb, win_ref[:, c0:c0 + IN_CHUNK])
        c = rope_ref[0]
        s1 = rope_ref[1]
        s2 = rope_ref[2]
        for jj in range((Q_WIDTH + KV_WIDTH) // LANES):
            sl = slice(jj * LANES, (jj + 1) * LANES)
            z_ref[:, sl] = _rope(z_ref[:, sl], c, s1, s2)

    r0 = pl.multiple_of(j * SB, SB)
    zq = z_ref[pl.ds(r0, SB), Q0:Q0 + Q_WIDTH]
    zk = z_ref[pl.ds(r0, SB), K0:K0 + KV_WIDTH]
    zv = z_ref[pl.ds(r0, SB), V0:V0 + KV_WIDTH]
    zc = z_ref[pl.ds(r0, SB), CQ0:CQ0 + MEM_WIDTH]
    sink = sink_ref[:, 0:1]
    row_kv = lax.broadcasted_iota(jnp.int32, (N_HEADS, KV_WIDTH), 0) & (N_KV - 1)
    lane_kv = lax.broadcasted_iota(jnp.int32, (N_HEADS, KV_WIDTH), 1) // HEAD_DIM
    own = row_kv == lane_kv
    row_c = lax.broadcasted_iota(jnp.int32, (N_HEADS, MEM_WIDTH), 0)
    lane_c = lax.broadcasted_iota(jnp.int32, (N_HEADS, MEM_WIDTH), 1) // HEAD_DIM
    own_c = row_c == lane_c
    last_row = lax.broadcasted_iota(jnp.int32, (WINDOW, KV_WIDTH), 0) == WINDOW - 1

    oa_rows = [[] for _ in range(GROUP)]
    oc_rows = []
    for b in range(SB):
        kc = ck_ref[b]
        vc = cv_ref[b]
        knew = zk[b:b + 1, :]
        vnew = zv[b:b + 1, :]
        qblk = jnp.concatenate(
            [jnp.broadcast_to(zq[b:b + 1, g * KV_WIDTH:(g + 1) * KV_WIDTH], (N_KV, KV_WIDTH)) for g in range(GROUP)],
            axis=0)
        qblk = jnp.where(own, qblk, 0.0).astype(BF16)
        s = _dot_nt(qblk, kc.astype(BF16))
        s_new = jnp.sum(qblk.astype(F32) * knew.astype(BF16).astype(F32), axis=1, keepdims=True)
        m = jnp.maximum(jnp.maximum(jnp.max(s, axis=1, keepdims=True), s_new), sink)
        p = jnp.exp(s - m)
        p_new = jnp.exp(s_new - m)
        inv = 1.0 / (jnp.sum(p, axis=1, keepdims=True) + p_new + jnp.exp(sink - m))
        o = (_dot((p * inv).astype(BF16), vc.astype(BF16))
             + (p_new * inv).astype(BF16).astype(F32) * vnew.astype(BF16).astype(F32))
        o = jnp.where(own, o, 0.0)
        for g in range(GROUP):
            oa_rows[g].append(jnp.sum(o[g * N_KV:(g + 1) * N_KV], axis=0, keepdims=True))
        nk_ref[b] = jnp.where(last_row, knew, pltpu.roll(kc, WINDOW - 1, 0))
        nv_ref[b] = jnp.where(last_row, vnew, pltpu.roll(vc, WINDOW - 1, 0))
        cblk = jnp.where(own_c, jnp.broadcast_to(zc[b:b + 1, :], (N_HEADS, MEM_WIDTH)), 0.0)
        sc = _dot_nt(cblk.astype(BF16), cmk_ref[b].astype(BF16))
        mc = jnp.max(sc, axis=1, keepdims=True)
        pc = jnp.exp(sc - mc)
        pc = pc * (1.0 / jnp.sum(pc, axis=1, keepdims=True))
        ocb = jnp.where(own_c, _dot(pc.astype(BF16), cmv_ref[b].astype(BF16)), 0.0)
        oc_rows.append(jnp.sum(ocb, axis=0, keepdims=True))
    for g in range(GROUP):
        oa_ref[pl.ds(r0, SB), g * KV_WIDTH:(g + 1) * KV_WIDTH] = jnp.concatenate(oa_rows[g], axis=0)
    oc_ref[pl.ds(r0, SB), :] = jnp.concatenate(oc_rows, axis=0)


def _sample_attn(x, rope, win, sink_gk, ck, cv, cmk, cmv):
    db = x.shape[0]
    blk = lambda r: pl.BlockSpec((SB, r, KV_WIDTH), lambda j: (j, 0, 0))
    full = lambda w: pl.BlockSpec((db, w), lambda j: (0, 0))
    return pl.pallas_call(
        _sample_attn_kernel,
        grid=(db // SB,),
        in_specs=[
            full(D_MODEL), _const_spec((3, 1, LANES)), _const_spec((D_MODEL, IN_WIDTH)),
            _const_spec((N_HEADS, LANES)),
            blk(WINDOW), blk(WINDOW), blk(N_MEM), blk(N_MEM),
        ],
        out_specs=[full(IN_WIDTH), full(Q_WIDTH), full(MEM_WIDTH), blk(WINDOW), blk(WINDOW)],
        out_shape=[
            jax.ShapeDtypeStruct((db, IN_WIDTH), F32),
            jax.ShapeDtypeStruct((db, Q_WIDTH), F32),
            jax.ShapeDtypeStruct((db, MEM_WIDTH), F32),
            jax.ShapeDtypeStruct((db, WINDOW, KV_WIDTH), F32),
            jax.ShapeDtypeStruct((db, WINDOW, KV_WIDTH), F32),
        ],
        compiler_params=pltpu.CompilerParams(dimension_semantics=("arbitrary",), vmem_limit_bytes=VMEM_LIMIT),
        name="sample_attn",
    )(x, rope, win, sink_gk, ck, cv, cmk, cmv)


def _sample_tail_kernel(x_ref, z_ref, oa_ref, oc_ref, st_ref, wa_ref, wb_ref, wc_ref, wo_ref, wmix_ref,
                        pscale_ref, g1_ref, b1_ref, wr_ref, br_ref, wg_ref, wu_ref, wd_ref, g2_ref, b2_ref,
                        y_ref, npool_ref, h_sc, comb_sc, acc_sc):
    e = pl.program_id(0)

    @pl.when(e == 0)
    def _():
        u = z_ref[:, U0:U0 + POOL_WIDTH]
        npool_ref[:, 0:(POOL_STATE - 1) * POOL_WIDTH] = st_ref[:, POOL_WIDTH:POOL_STATE * POOL_WIDTH]
        npool_ref[:, (POOL_STATE - 1) * POOL_WIDTH:POOL_STATE * POOL_WIDTH] = u
        obs = []
        for g, w in enumerate(POOL_WINDOWS):
            sl = slice(g * POOL_GROUP_DIM, (g + 1) * POOL_GROUP_DIM)
            cur = u[:, sl]
            ws = cur
            for jj in range(1, w):
                base = (POOL_STATE - jj) * POOL_WIDTH
                ws = ws + st_ref[:, base + g * POOL_GROUP_DIM:base + (g + 1) * POOL_GROUP_DIM]
            cnt = float(min(PAST_LEN + 1, w))
            pooled = ws / cnt - cur
            obs.append(_dot(pooled.astype(BF16), wmix_ref[g]) * pscale_ref[:, sl])
        ob = jnp.concatenate(obs, axis=1)
        h = _merge_ln1(x_ref[...], oa_ref[...].astype(BF16), ob, oc_ref[...], z_ref[:, GZ0:GZ0 + 3 * D_MODEL],
                       wa_ref, wb_ref, wc_ref, wo_ref, g1_ref[...], b1_ref[...])
        h_sc[...] = h
        logits = _dot(h.astype(BF16), wr_ref[...]) + br_ref[...]
        hot1, hot2, w1, w2 = _route(logits)
        comb_sc[...] = jnp.where(hot1, w1, 0.0) + jnp.where(hot2, w2, 0.0)
        acc_sc[...] = jnp.zeros_like(acc_sc)

    out = _expert_mlp(h_sc[...].astype(BF16), wg_ref[...], wu_ref[...], wd_ref[...])
    lane = lax.broadcasted_iota(jnp.int32, comb_sc.shape, 1)
    ce = jnp.sum(jnp.where(lane == e, comb_sc[...], 0.0), axis=1, keepdims=True)
    acc_sc[...] += ce * out

    @pl.when(e == pl.num_programs(0) - 1)
    def _():
        y_ref[...] = _layer_norm(ALPHA * h_sc[...] + acc_sc[...], g2_ref[...], b2_ref[...])


def _sample_tail(x, z, oa, oc, state, wa, wb, wc, wo, wmix, pscale, g1, b1, wr, br, wg, wu, wd, g2, b2):
    db = x.shape[0]
    full = lambda w: pl.BlockSpec((db, w), lambda e: (0, 0))
    vec = lambda w: pl.BlockSpec((1, w), lambda e: (0, 0))
    return pl.pallas_call(
        _sample_tail_kernel,
        grid=(N_EXPERTS,),
        in_specs=[
            full(D_MODEL), full(IN_WIDTH), full(Q_WIDTH), full(MEM_WIDTH), full(POOL_STATE * POOL_WIDTH),
            _const_spec((Q_WIDTH, D_MODEL)), _const_spec((POOL_WIDTH, D_MODEL)),
            _const_spec((MEM_WIDTH, D_MODEL)), _const_spec((D_MODEL, D_MODEL)),
            _const_spec((len(POOL_WINDOWS), POOL_GROUP_DIM, POOL_GROUP_DIM)),
            vec(POOL_WIDTH), vec(D_MODEL), vec(D_MODEL),
            _const_spec((D_MODEL, LANES)), vec(LANES),
            pl.BlockSpec((None, D_MODEL, D_EXPERT), lambda e: (e, 0, 0)),
            pl.BlockSpec((None, D_MODEL, D_EXPERT), lambda e: (e, 0, 0)),
            pl.BlockSpec((None, D_EXPERT, D_MODEL), lambda e: (e, 0, 0)),
            vec(D_MODEL), vec(D_MODEL),
        ],
        out_specs=[full(D_MODEL), full(POOL_STATE * POOL_WIDTH)],
        out_shape=[jax.ShapeDtypeStruct((db, D_MODEL), F32),
                   jax.ShapeDtypeStruct((db, POOL_STATE * POOL_WIDTH), F32)],
        scratch_shapes=[pltpu.VMEM((db, D_MODEL), F32), pltpu.VMEM((db, LANES), F32),
                        pltpu.VMEM((db, D_MODEL), F32)],
        compiler_params=pltpu.CompilerParams(dimension_semantics=("arbitrary",), vmem_limit_bytes=VMEM_LIMIT),
        name="sample_tail",
    )(x, z, oa, oc, state, wa, wb, wc, wo, wmix, pscale, g1, b1, wr, br, wg, wu, wd, g2, b2)


def _rope_tables(pos):
    half = ROPE_DIM // 2
    inv = jnp.power(ROPE_THETA, -jnp.arange(half, dtype=F32) * (2.0 / ROPE_DIM))
    ang = pos.astype(F32)[:, None] * inv[None, :]
    lane = np.arange(LANES)
    off = lane % HEAD_DIM
    cos = jnp.cos(ang)[:, lane % half]
    sin = jnp.sin(ang)[:, lane % half]
    c = jnp.where(off[None, :] < ROPE_DIM, cos, 1.0)
    s1 = jnp.where((off[None, :] >= half) & (off[None, :] < ROPE_DIM), sin, 0.0)
    s2 = jnp.where(off[None, :] < half, -sin, 0.0)
    return jnp.stack([c, s1, s2]).astype(F32)


def _q_heads_group_major(w, axis):
    if axis == 1:
        n = w.shape[0]
        return w.reshape(n, N_KV, GROUP, HEAD_DIM).transpose(0, 2, 1, 3).reshape(n, Q_WIDTH)
    n = w.shape[1]
    return w.reshape(N_KV, GROUP, HEAD_DIM, n).transpose(1, 0, 2, 3).reshape(Q_WIDTH, n)


def kernel(x_prompt, x_sample, cache_win_k, cache_win_v, state_pool, cache_mem_k, cache_mem_v, mem_prompt, w_in, sinks, w_pool_mix, pool_scale, w_mem_k, w_mem_v, w_branch_a, w_branch_b, w_branch_c, w_out, ln1_g, ln1_b, w_group, b_group, w_router, b_router, w_gate, w_up, w_down, ln2_g, ln2_b):
    assert w_in.shape[0] == DEPTH == 1
    b, l, _ = x_prompt.shape
    db, ds, _ = x_sample.shape
    assert ds == 1 and l % TM == 0 and db % SB == 0
    assert cache_win_k.shape[2] == WINDOW
    t = b * l

    win = w_in[0]
    scale = HEAD_DIM ** -0.5
    wq = _q_heads_group_major(win[:, Q0:Q0 + Q_WIDTH], 1) * scale
    wcq = win[:, CQ0:CQ0 + MEM_WIDTH] * scale
    win_b = jnp.concatenate([wq, win[:, K0:CQ0], wcq, win[:, GZ0:]], axis=1).astype(BF16)
    wa = _q_heads_group_major(w_branch_a[0], 0).astype(BF16)
    wb = w_branch_b[0].astype(BF16)
    wc = w_branch_c[0].astype(BF16)
    wo = w_out[0].astype(BF16)
    wmix = w_pool_mix[0].astype(BF16)
    pscale = pool_scale[0].reshape(1, POOL_WIDTH)
    g1 = ln1_g[0].reshape(1, D_MODEL)
    b1 = ln1_b[0].reshape(1, D_MODEL)
    g2 = ln2_g[0].reshape(1, D_MODEL)
    b2 = ln2_b[0].reshape(1, D_MODEL)
    wr = jnp.concatenate([w_group[0], w_router[0].reshape(D_MODEL, N_EXPERTS)], axis=1)
    wr = jnp.pad(wr, ((0, 0), (0, LANES - wr.shape[1]))).astype(BF16)
    br = jnp.pad(jnp.concatenate([b_group[0], b_router[0].reshape(N_EXPERTS)]), (0, LANES - N_EXPERT_GROUPS - N_EXPERTS))
    br = br.reshape(1, LANES).astype(F32)
    wg = w_gate[0].astype(BF16)
    wu = w_up[0].astype(BF16)
    wd = w_down[0].astype(BF16)
    sink = sinks[0].astype(F32)
    sink_gk = jnp.broadcast_to(sink.reshape(N_KV, GROUP).T.reshape(N_HEADS, 1), (N_HEADS, LANES))

    mk, mv = _mem_project(mem_prompt, w_mem_k[0].astype(BF16), w_mem_v[0].astype(BF16))
    rope_p = _rope_tables(jnp.arange(l, dtype=jnp.int32))
    h, xs, route, counts, nk_p, nv_p, npool_p = _front(
        x_prompt, rope_p, sink, win_b, wa, wb, wc, wo, wmix, pscale, mk, mv, g1, b1, wr, br)
    piece_rows, tile_expert, tile_active = _piece_tables(
        counts.reshape(-1, SUBLANES, LANES)[:, 0, :N_EXPERTS].astype(jnp.int32))
    ys = _grouped_gemm(piece_rows, tile_expert, tile_active, xs, wg, wu, wd)
    y_p = _combine(ys, h.reshape(t, D_MODEL), route.reshape(t, LANES), g2, b2).reshape(b, l, D_MODEL)

    rope_s = _rope_tables(jnp.full((1,), PAST_LEN, jnp.int32))
    xs = x_sample.reshape(db, D_MODEL)
    ck = cache_win_k[0].reshape(db, WINDOW, KV_WIDTH)
    cv = cache_win_v[0].reshape(db, WINDOW, KV_WIDTH)
    cmk = cache_mem_k[0].reshape(db, N_MEM, MEM_WIDTH)
    cmv = cache_mem_v[0].reshape(db, N_MEM, MEM_WIDTH)
    z_s, oa_s, oc_s, nk_s, nv_s = _sample_attn(xs, rope_s, win_b, sink_gk, ck, cv, cmk, cmv)
    state = state_pool[0].reshape(db, POOL_STATE * POOL_WIDTH)
    y_s, npool_s = _sample_tail(xs, z_s, oa_s, oc_s, state, wa, wb, wc, wo, wmix, pscale, g1, b1, wr, br,
                                wg, wu, wd, g2, b2)

    kv5 = lambda a, n, w: a.reshape(1, n, w, N_KV, HEAD_DIM)
    return (y_p, y_s.reshape(db, 1, D_MODEL),
            kv5(nk_p, b, QB), kv5(nv_p, b, QB),
            npool_p[:, 2 * SUBLANES - POOL_STATE:, :][None],
            kv5(mk, b, N_MEM), kv5(mv, b, N_MEM),
            kv5(nk_s, db, WINDOW), kv5(nv_s, db, WINDOW),
            npool_s.reshape(1, db, POOL_STATE, POOL_WIDTH))
```

```python
import functools

import jax
import jax.numpy as jnp
import numpy as np
from jax import lax
from jax.experimental import pallas as pl
from jax.experimental.pallas import tpu as pltpu

D_MODEL = 1024
N_HEADS = 16
HEAD_DIM = 64
N_KV = 4
GROUP = N_HEADS // N_KV
WINDOW = 128
ROPE_THETA = 500000.0
ROPE_DIM = HEAD_DIM // 4
Q_WIDTH = N_HEADS * HEAD_DIM
KV_WIDTH = N_KV * HEAD_DIM
POOL_WINDOWS = (2, 4, 8, 16)
POOL_WIDTH = D_MODEL // 2
POOL_GROUP_DIM = POOL_WIDTH // len(POOL_WINDOWS)
POOL_STATE = max(POOL_WINDOWS) - 1
N_MEM = 256
MEM_HEADS = 4
MEM_WIDTH = MEM_HEADS * HEAD_DIM
N_EXPERT_GROUPS = 4
EXPERTS_PER_GROUP = 4
N_EXPERTS = N_EXPERT_GROUPS * EXPERTS_PER_GROUP
D_EXPERT = 512
PAST_LEN = 16384
DEPTH = 1
ALPHA = (2.0 * DEPTH) ** 0.25
LN_EPS = 1e-5

Q0 = 0
K0 = Q0 + Q_WIDTH
V0 = K0 + KV_WIDTH
U0 = V0 + KV_WIDTH
CQ0 = U0 + POOL_WIDTH
GZ0 = CQ0 + MEM_WIDTH
IN_WIDTH = GZ0 + 3 * D_MODEL

LANES = 128
SUBLANES = 8
VMEM_LIMIT = 56 * 1024 * 1024

TM = 256
QB = WINDOW
PIECE = 16
PIECES_PER_TILE = 16
TG = PIECE * PIECES_PER_TILE
MAX_CHUNK_PIECES = 2 * TM // PIECE + N_EXPERTS - 1
N_SPARE = 2 * PIECES_PER_TILE + 1


def _chunk_rows(n_chunks):
    spare = -(-N_SPARE // n_chunks)
    return -(-(MAX_CHUNK_PIECES + spare) * PIECE // LANES) * LANES
SB = 8
IN_CHUNK = 768
SOFTMAX_ROWS = 64
assert N_MEM == 2 * QB and MEM_HEADS == N_KV and MEM_WIDTH == KV_WIDTH and TM <= GROUP * QB

BF16 = jnp.bfloat16
F32 = jnp.float32
NEG_INF = float("-inf")


def _const_spec(shape):
    nd = len(shape)
    return pl.BlockSpec(shape, lambda *_: (0,) * nd, pipeline_mode=pl.Buffered(1))


def _layer_norm(x, g, b):
    mu = jnp.mean(x, axis=-1, keepdims=True)
    xc = x - mu
    var = jnp.mean(xc * xc, axis=-1, keepdims=True)
    return xc * lax.rsqrt(var + LN_EPS) * g + b


def _dot(a, b):
    return jnp.dot(a, b, preferred_element_type=F32)


def _dot_nt(a, b):
    return lax.dot_general(a, b, (((1,), (1,)), ((), ())), preferred_element_type=F32)


def _lane_block_mask(shape, block, width=HEAD_DIM):
    lane = lax.broadcasted_iota(jnp.int32, shape, len(shape) - 1)
    return (lane >= block * width) & (lane < (block + 1) * width)


def _rope(x, c, s1, s2):
    half = ROPE_DIM // 2
    return x * c + pltpu.roll(x, half, 1) * s1 + pltpu.roll(x, LANES - half, 1) * s2


def _route(logits):
    rows = logits.shape[0]
    lane = lax.broadcasted_iota(jnp.int32, (rows, LANES), 1)
    lanef = lane.astype(F32)
    big = float(LANES)
    is_g = lane < N_EXPERT_GROUPS
    glog = jnp.where(is_g, logits, NEG_INF)
    gmax = jnp.max(glog, axis=1, keepdims=True)
    gsum = jnp.sum(jnp.where(is_g, jnp.exp(glog - gmax), 0.0), axis=1, keepdims=True)
    gp = 1.0 / gsum
    gidx = jnp.min(jnp.where(glog == gmax, lanef, big), axis=1, keepdims=True).astype(jnp.int32)
    lo = N_EXPERT_GROUPS + gidx * EXPERTS_PER_GROUP
    in_grp = (lane >= lo) & (lane < lo + EXPERTS_PER_GROUP)
    el = jnp.where(in_grp, logits, NEG_INF)
    v1 = jnp.max(el, axis=1, keepdims=True)
    i1 = jnp.min(jnp.where(el == v1, lanef, big), axis=1, keepdims=True).astype(jnp.int32)
    el2 = jnp.where(lane == i1, NEG_INF, el)
    v2 = jnp.max(el2, axis=1, keepdims=True)
    i2 = jnp.min(jnp.where(el2 == v2, lanef, big), axis=1, keepdims=True).astype(jnp.int32)
    e21 = jnp.exp(v2 - v1)
    inv = 1.0 / (1.0 + e21)
    w1 = inv * gp
    w2 = e21 * inv * gp
    e1 = i1 - N_EXPERT_GROUPS
    e2 = i2 - N_EXPERT_GROUPS
    return lane == e1, lane == e2, w1, w2


def _local_sort(hot1, hot2, w1, w2, hb, cap):
    rows = hb.shape[0]
    lane = lax.broadcasted_iota(jnp.int32, (rows, LANES), 1)
    onehot = jnp.where(hot1 | hot2, 1.0, 0.0)
    counts = jnp.sum(onehot, axis=0, keepdims=True)
    before = (lax.broadcasted_iota(jnp.int32, (rows, rows), 1)
              < lax.broadcasted_iota(jnp.int32, (rows, rows), 0)).astype(BF16)
    rank = _dot(before, onehot.astype(BF16))
    run = (((counts.astype(jnp.int32) + (PIECE - 1)) // PIECE) * PIECE).astype(F32)
    lower = (lax.broadcasted_iota(jnp.int32, (LANES, LANES), 0)
             < lax.broadcasted_iota(jnp.int32, (LANES, LANES), 1)).astype(BF16)
    start = _dot(jnp.broadcast_to(run, (SUBLANES, LANES)).astype(BF16), lower)[0:1]
    slot = start + rank
    s1 = jnp.sum(jnp.where(hot1, slot, 0.0), axis=1, keepdims=True)
    s2 = jnp.sum(jnp.where(hot2, slot, 0.0), axis=1, keepdims=True)
    route = jnp.where(lane == 0, w1, jnp.where(lane == 1, w2, jnp.where(lane == 2, s1, jnp.where(lane == 3, s2, 0.0))))
    route_t = route.T
    srow = lax.broadcasted_iota(jnp.int32, (cap, rows), 0).astype(F32)
    perm = jnp.where((srow == route_t[2:3, :]) | (srow == route_t[3:4, :]), 1.0, 0.0).astype(BF16)
    return _dot(perm, hb).astype(BF16), route, counts


def _sigmoid(x):
    return 0.5 * jnp.tanh(0.5 * x) + 0.5


def _merge_ln1(x, oa, ob, oc, gz, wa_ref, wb_ref, wc_ref, wo_ref, g1, b1):
    ya = _dot(oa, wa_ref[...])
    yb = _dot(ob.astype(BF16), wb_ref[...])
    yc = _dot(oc.astype(BF16), wc_ref[...])
    m = (_sigmoid(gz[:, 0:D_MODEL]) * ya
         + _sigmoid(gz[:, D_MODEL:2 * D_MODEL]) * yb
         + _sigmoid(gz[:, 2 * D_MODEL:3 * D_MODEL]) * yc)
    hpre = ALPHA * x + _dot(m.astype(BF16), wo_ref[...])
    return _layer_norm(hpre, g1, b1)


def _mem_kernel(mem_ref, wk_ref, wv_ref, mk_ref, mv_ref):
    m = mem_ref[...].astype(BF16)
    mk_ref[...] = _dot(m, wk_ref[...])
    mv_ref[...] = _dot(m, wv_ref[...])


def _mem_project(mem, wk, wv):
    b = mem.shape[0]
    out = jax.ShapeDtypeStruct((b, N_MEM, MEM_WIDTH), F32)
    return pl.pallas_call(
        _mem_kernel,
        grid=(b,),
        in_specs=[pl.BlockSpec((None, N_MEM, D_MODEL), lambda i: (i, 0, 0)),
                  _const_spec((D_MODEL, MEM_WIDTH)), _const_spec((D_MODEL, MEM_WIDTH))],
        out_specs=[pl.BlockSpec((None, N_MEM, MEM_WIDTH), lambda i: (i, 0, 0))] * 2,
        out_shape=[out, out],
        name="mem_project",
    )(mem, wk, wv)


def _front_kernel(sinks_ref, x_ref, rope_ref, win_ref, wa_ref, wb_ref, wc_ref, wo_ref, wmix_ref,
                  pscale_ref, mk_ref, mv_ref, g1_ref, b1_ref, wr_ref, br_ref,
                  h_ref, xs_ref, route_ref, counts_ref, nk_ref, nv_ref, npool_ref,
                  z_ref, qb_ref, kext_ref, vext_ref, uext_ref, oa_ref, ob_ref, bias_ref, s_ref, p_ref, vblk_ref):
    i = pl.program_id(1)
    x = x_ref[...]
    xb = x.astype(BF16)

    @pl.when(i == 0)
    def _():
        kext_ref[0:QB, :] = jnp.zeros((QB, KV_WIDTH), BF16)
        vext_ref[0:QB, :] = jnp.zeros((QB, KV_WIDTH), BF16)
        uext_ref[0:2 * SUBLANES, :] = jnp.zeros((2 * SUBLANES, POOL_WIDTH), F32)

    @pl.when(i > 0)
    def _():
        kext_ref[0:QB, :] = kext_ref[TM:TM + QB, :]
        vext_ref[0:QB, :] = vext_ref[TM:TM + QB, :]
        uext_ref[0:2 * SUBLANES, :] = uext_ref[TM:TM + 2 * SUBLANES, :]

    for c0 in range(0, IN_WIDTH, IN_CHUNK):
        z_ref[:, c0:c0 + IN_CHUNK] = _dot(xb, win_ref[:, c0:c0 + IN_CHUNK])

    c = rope_ref[0]
    s1 = rope_ref[1]
    s2 = rope_ref[2]
    for j in range(Q_WIDTH // LANES):
        sl = slice(Q0 + j * LANES, Q0 + (j + 1) * LANES)
        qb_ref[:, j * LANES:(j + 1) * LANES] = _rope(z_ref[:, sl], c, s1, s2).astype(BF16)
    for j in range(KV_WIDTH // LANES):
        sl = slice(K0 + j * LANES, K0 + (j + 1) * LANES)
        kr = _rope(z_ref[:, sl], c, s1, s2)
        z_ref[:, sl] = kr
        kext_ref[QB:QB + TM, j * LANES:(j + 1) * LANES] = kr.astype(BF16)
    vext_ref[QB:QB + TM, :] = z_ref[:, V0:V0 + KV_WIDTH].astype(BF16)
    nk_ref[...] = z_ref[TM - QB:TM, K0:K0 + KV_WIDTH]
    nv_ref[...] = z_ref[TM - QB:TM, V0:V0 + KV_WIDTH]

    rowq = lax.broadcasted_iota(jnp.int32, (QB, 2 * QB), 0)
    colk = lax.broadcasted_iota(jnp.int32, (QB, 2 * QB), 1)
    band = (colk >= rowq) & (colk <= rowq + WINDOW)
    bias_ref[1] = jnp.where(band, 0.0, NEG_INF)
    bias_ref[0] = jnp.where(band & ((colk >= QB) | (i > 0)), 0.0, NEG_INF)
    for sb in range(TM // QB):
        k2 = kext_ref[sb * QB:(sb + 2) * QB, :]
        v2 = vext_ref[sb * QB:(sb + 2) * QB, :]
        qs = jnp.concatenate(
            [qb_ref[sb * QB:(sb + 1) * QB, g * KV_WIDTH:(g + 1) * KV_WIDTH] for g in range(GROUP)], axis=0)
        for kv in range(N_KV):
            kmask = _lane_block_mask((2 * QB, KV_WIDTH), kv)
            s_ref[...] = _dot_nt(qs, jnp.where(kmask, k2, jnp.zeros_like(k2)))
            vblk_ref[kv * 2 * QB:(kv + 1) * 2 * QB, :] = jnp.where(kmask, v2, jnp.zeros_like(v2))
            for c0 in range(0, GROUP * QB, SOFTMAX_ROWS):
                rq = c0 % QB
                sink = sinks_ref[kv * GROUP + c0 // QB]
                s = s_ref[c0:c0 + SOFTMAX_ROWS, :] + bias_ref[min(sb, 1), rq:rq + SOFTMAX_ROWS, :]
                m = jnp.maximum(jnp.max(s, axis=1, keepdims=True), sink)
                p = jnp.exp(s - m)
                den = jnp.sum(p, axis=1, keepdims=True) + jnp.exp(sink - m)
                p_ref[c0:c0 + SOFTMAX_ROWS, kv * 2 * QB:(kv + 1) * 2 * QB] = (p * (1.0 / den)).astype(BF16)
        o = _dot(p_ref[...], vblk_ref[...])
        for g in range(GROUP):
            oa_ref[sb * QB:(sb + 1) * QB, g * KV_WIDTH:(g + 1) * KV_WIDTH] = o[g * QB:(g + 1) * QB].astype(BF16)

    hist = 2 * SUBLANES
    uext_ref[hist:hist + TM, :] = z_ref[:, U0:U0 + POOL_WIDTH]
    npool_ref[...] = uext_ref[TM:TM + hist, :]
    pos = i * TM + lax.broadcasted_iota(jnp.int32, (TM, 1), 0)
    for g, w in enumerate(POOL_WINDOWS):
        sl = slice(g * POOL_GROUP_DIM, (g + 1) * POOL_GROUP_DIM)
        cur = uext_ref[hist:hist + TM, sl]
        ws = cur
        for j in range(1, w):
            ws = ws + uext_ref[hist - j:hist - j + TM, sl]
        cnt = jnp.minimum(pos + 1, w).astype(F32)
        pooled = ws / cnt - cur
        ob_ref[:, sl] = _dot(pooled.astype(BF16), wmix_ref[g]) * pscale_ref[:, sl]

    cq = z_ref[:, CQ0:CQ0 + MEM_WIDTH].astype(BF16)
    mk = mk_ref[...].astype(BF16)
    mv = mv_ref[...].astype(BF16)
    for hh in range(MEM_HEADS):
        hmask = _lane_block_mask((N_MEM, MEM_WIDTH), hh)
        s_ref[0:TM, :] = _dot_nt(cq, jnp.where(hmask, mk, jnp.zeros_like(mk)))
        vblk_ref[hh * N_MEM:(hh + 1) * N_MEM, :] = jnp.where(hmask, mv, jnp.zeros_like(mv))
        for c0 in range(0, TM, SOFTMAX_ROWS):
            s = s_ref[c0:c0 + SOFTMAX_ROWS, :]
            p = jnp.exp(s - jnp.max(s, axis=1, keepdims=True))
            den = jnp.sum(p, axis=1, keepdims=True)
            p_ref[c0:c0 + SOFTMAX_ROWS, hh * N_MEM:(hh + 1) * N_MEM] = (p * (1.0 / den)).astype(BF16)
    oc = _dot(p_ref[0:TM, :], vblk_ref[...])

    h = _merge_ln1(x, oa_ref[...], ob_ref[...], oc, z_ref[:, GZ0:GZ0 + 3 * D_MODEL],
                   wa_ref, wb_ref, wc_ref, wo_ref, g1_ref[...], b1_ref[...])
    h_ref[...] = h
    hb = h.astype(BF16)
    logits = _dot(hb, wr_ref[...]) + br_ref[...]
    xs, route, counts = _local_sort(*_route(logits), hb, xs_ref.shape[0])
    xs_ref[...] = xs
    route_ref[...] = route
    counts_ref[...] = jnp.broadcast_to(counts, (SUBLANES, LANES))


def _front(x, rope, sinks, win, wa, wb, wc, wo, wmix, pscale, mk, mv, g1, b1, wr, br):
    b, l, _ = x.shape
    nt = l // TM
    cap = _chunk_rows(b * nt)
    hist = 2 * SUBLANES
    tile = lambda w: pl.BlockSpec((None, TM, w), lambda bi, ti: (bi, ti, 0))
    per_b = lambda r, w: pl.BlockSpec((None, r, w), lambda bi, ti: (bi, 0, 0))
    return pl.pallas_call(
        _front_kernel,
        grid=(b, nt),
        in_specs=[
            pl.BlockSpec(memory_space=pltpu.SMEM),
            tile(D_MODEL),
            pl.BlockSpec((3, TM, LANES), lambda bi, ti: (0, ti, 0)),
            _const_spec((D_MODEL, IN_WIDTH)),
            _const_spec((Q_WIDTH, D_MODEL)), _const_spec((POOL_WIDTH, D_MODEL)),
            _const_spec((MEM_WIDTH, D_MODEL)), _const_spec((D_MODEL, D_MODEL)),
            _const_spec((len(POOL_WINDOWS), POOL_GROUP_DIM, POOL_GROUP_DIM)),
            _const_spec((1, POOL_WIDTH)),
            per_b(N_MEM, MEM_WIDTH), per_b(N_MEM, MEM_WIDTH),
            _const_spec((1, D_MODEL)), _const_spec((1, D_MODEL)),
            _const_spec((D_MODEL, LANES)), _const_spec((1, LANES)),
        ],
        out_specs=[
            tile(D_MODEL),
            pl.BlockSpec((cap, D_MODEL), lambda bi, ti: (bi * nt + ti, 0)),
            tile(LANES),
            pl.BlockSpec((None, None, SUBLANES, LANES), lambda bi, ti: (bi, ti, 0, 0)),
            per_b(QB, KV_WIDTH), per_b(QB, KV_WIDTH), per_b(hist, POOL_WIDTH),
        ],
        out_shape=[
            jax.ShapeDtypeStruct((b, l, D_MODEL), F32),
            jax.ShapeDtypeStruct((b * nt * cap, D_MODEL), BF16),
            jax.ShapeDtypeStruct((b, l, LANES), F32),
            jax.ShapeDtypeStruct((b, nt, SUBLANES, LANES), F32),
            jax.ShapeDtypeStruct((b, QB, KV_WIDTH), F32),
            jax.ShapeDtypeStruct((b, QB, KV_WIDTH), F32),
            jax.ShapeDtypeStruct((b, hist, POOL_WIDTH), F32),
        ],
        scratch_shapes=[
            pltpu.VMEM((TM, IN_WIDTH), F32),
            pltpu.VMEM((TM, Q_WIDTH), BF16),
            pltpu.VMEM((QB + TM, KV_WIDTH), BF16),
            pltpu.VMEM((QB + TM, KV_WIDTH), BF16),
            pltpu.VMEM((hist + TM, POOL_WIDTH), F32),
            pltpu.VMEM((TM, Q_WIDTH), BF16),
            pltpu.VMEM((TM, POOL_WIDTH), F32),
            pltpu.VMEM((2, QB, 2 * QB), F32),
            pltpu.VMEM((GROUP * QB, 2 * QB), F32),
            pltpu.VMEM((GROUP * QB, N_KV * 2 * QB), BF16),
            pltpu.VMEM((N_KV * 2 * QB, KV_WIDTH), BF16),
        ],
        compiler_params=pltpu.CompilerParams(
            dimension_semantics=("arbitrary", "arbitrary"), vmem_limit_bytes=VMEM_LIMIT),
        name="front_prompt",
    )(sinks, x, rope, win, wa, wb, wc, wo, wmix, pscale, mk, mv, g1, b1, wr, br)


def _expert_mlp(xb, wg, wu, wd):
    a = _dot(xb, wg)
    hid = (a * jax.nn.sigmoid(a)) * _dot(xb, wu)
    return _dot(hid.astype(BF16), wd)


def _gemm_kernel(src_ref, dst_ref, te_ref, act_ref, xs_ref, wg_ref, wu_ref, wd_ref, ys_ref,
                 xbuf, obuf, wgb, wub, wdb, sem_in, sem_out):
    i = pl.program_id(0)
    n = pl.num_programs(0)
    slot = i % 2

    def copy_in(tile, j, slot):
        row0 = pl.multiple_of(src_ref[tile * PIECES_PER_TILE + j], PIECE)
        return pltpu.make_async_copy(xs_ref.at[pl.ds(row0, PIECE)], xbuf.at[slot, pl.ds(j * PIECE, PIECE)], sem_in.at[slot])

    def copy_out(tile, j, slot):
        row0 = pl.multiple_of(dst_ref[tile * PIECES_PER_TILE + j], PIECE)
        return pltpu.make_async_copy(obuf.at[slot, pl.ds(j * PIECE, PIECE)], ys_ref.at[pl.ds(row0, PIECE)], sem_out.at[slot])

    def start_in(tile, slot):
        for j in range(PIECES_PER_TILE):
            copy_in(tile, j, slot).start()

    def wait_out(tile, slot):
        for j in range(PIECES_PER_TILE):
            copy_out(tile, j, slot).wait()

    @pl.when((i == 0) & (act_ref[0] > 0))
    def _():
        start_in(0, 0)

    @pl.when((i + 1 < n) & (act_ref[jnp.minimum(i + 1, n - 1)] > 0))
    def _():
        start_in(i + 1, 1 - slot)

    @pl.when((i >= 2) & (act_ref[jnp.maximum(i - 2, 0)] > 0))
    def _():
        wait_out(i - 2, slot)

    @pl.when(act_ref[i] > 0)
    def _():
        @pl.when((i == 0) | (te_ref[i] != te_ref[jnp.maximum(i - 1, 0)]))
        def _():
            wgb[...] = wg_ref[...].astype(BF16)
            wub[...] = wu_ref[...].astype(BF16)
            wdb[...] = wd_ref[...].astype(BF16)

        for j in range(PIECES_PER_TILE):
            copy_in(i, j, slot).wait()
        obuf[slot] = _expert_mlp(xbuf[slot], wgb[...], wub[...], wdb[...]).astype(BF16)
        for j in range(PIECES_PER_TILE):
            copy_out(i, j, slot).start()

    @pl.when(i == n - 1)
    def _():
        @pl.when((i >= 1) & (act_ref[jnp.maximum(i - 1, 0)] > 0))
        def _():
            wait_out(i - 1, 1 - slot)

        @pl.when(act_ref[i] > 0)
        def _():
            wait_out(i, slot)


def _grouped_gemm(piece_src, piece_dst, tile_expert, tile_active, xs, wg, wu, wd):
    n_tiles = tile_expert.shape[0]
    wspec = lambda r, c: pl.BlockSpec((None, r, c), lambda i, src, dst, te, act: (te[i], 0, 0))
    return pl.pallas_call(
        _gemm_kernel,
        grid_spec=pltpu.PrefetchScalarGridSpec(
            num_scalar_prefetch=4,
            grid=(n_tiles,),
            in_specs=[pl.BlockSpec(memory_space=pl.ANY),
                      wspec(D_MODEL, D_EXPERT), wspec(D_MODEL, D_EXPERT), wspec(D_EXPERT, D_MODEL)],
            out_specs=pl.BlockSpec(memory_space=pl.ANY),
            scratch_shapes=[pltpu.VMEM((2, TG, D_MODEL), BF16), pltpu.VMEM((2, TG, D_MODEL), BF16),
                            pltpu.VMEM((D_MODEL, D_EXPERT), BF16), pltpu.VMEM((D_MODEL, D_EXPERT), BF16),
                            pltpu.VMEM((D_EXPERT, D_MODEL), BF16),
                            pltpu.SemaphoreType.DMA((2,)), pltpu.SemaphoreType.DMA((2,))],
        ),
        out_shape=jax.ShapeDtypeStruct(xs.shape, xs.dtype),
        input_output_aliases={4: 0},
        compiler_params=pltpu.CompilerParams(dimension_semantics=("arbitrary",), vmem_limit_bytes=VMEM_LIMIT),
        name="moe_grouped_gemm",
    )(piece_src, piece_dst, tile_expert, tile_active, xs, wg, wu, wd)


def _combine_kernel(ys_ref, h_ref, route_ref, g2_ref, b2_ref, y_ref):
    route = route_ref[...]
    slot = lax.broadcasted_iota(jnp.int32, (TM, ys_ref.shape[0]), 1).astype(F32)
    ys = ys_ref[...]
    f = (route[:, 0:1] * _dot(jnp.where(slot == route[:, 2:3], 1.0, 0.0).astype(BF16), ys)
         + route[:, 1:2] * _dot(jnp.where(slot == route[:, 3:4], 1.0, 0.0).astype(BF16), ys))
    y_ref[...] = _layer_norm(ALPHA * h_ref[...] + f, g2_ref[...], b2_ref[...])


def _combine(ys, h, route, g2, b2):
    t = h.shape[0]
    return pl.pallas_call(
        _combine_kernel,
        grid=(t // TM,),
        in_specs=[
            pl.BlockSpec((ys.shape[0] // (t // TM), D_MODEL), lambda i: (i, 0)),
            pl.BlockSpec((TM, D_MODEL), lambda i: (i, 0)),
            pl.BlockSpec((TM, LANES), lambda i: (i, 0)),
            pl.BlockSpec((1, D_MODEL), lambda i: (0, 0)),
            pl.BlockSpec((1, D_MODEL), lambda i: (0, 0)),
        ],
        out_specs=pl.BlockSpec((TM, D_MODEL), lambda i: (i, 0)),
        out_shape=jax.ShapeDtypeStruct((t, D_MODEL), F32),
        compiler_params=pltpu.CompilerParams(dimension_semantics=("arbitrary",)),
        name="moe_combine",
    )(ys, h, route, g2, b2)


def _select(table, idx):
    hot = idx[:, None] == jnp.arange(table.shape[0], dtype=jnp.int32)[None, :]
    return jnp.sum(jnp.where(hot[:, :, None], table[None, :, :], 0), axis=1)


def _piece_tables(counts):
    n_chunks = counts.shape[0]
    n_tiles = -(-(n_chunks * MAX_CHUNK_PIECES + N_EXPERTS * (PIECES_PER_TILE - 1)) // PIECES_PER_TILE)
    npc = (counts + (PIECE - 1)) // PIECE
    first = (jnp.cumsum(npc, axis=1) - npc).T
    npc_t = npc.T
    cum = jnp.cumsum(npc_t, axis=1)
    per_expert = cum[:, -1]
    tiles_e = (per_expert + (PIECES_PER_TILE - 1)) // PIECES_PER_TILE
    tile_end = jnp.cumsum(tiles_e)
    tile_idx = jnp.arange(n_tiles, dtype=jnp.int32)
    expert_of = lambda i: jnp.minimum(jnp.sum(i[:, None] >= tile_end[None, :], axis=1), N_EXPERTS - 1).astype(jnp.int32)
    active = tile_idx < tile_end[-1]
    tile_expert = jnp.where(active, expert_of(tile_idx), expert_of(tile_end[-1:] - 1))
    meta = jnp.stack([tile_end - tiles_e, per_expert], axis=1)
    meta_t = _select(meta, tile_expert)
    k = (tile_idx - meta_t[:, 0])[:, None] * PIECES_PER_TILE + jnp.arange(PIECES_PER_TILE, dtype=jnp.int32)[None, :]
    valid = active[:, None] & (k < meta_t[:, 1:2])
    cum_t = _select(cum, tile_expert)
    chunk = jnp.minimum(jnp.sum(k[:, :, None] >= cum_t[:, None, :], axis=2), n_chunks - 1).astype(jnp.int32)
    at_chunk = chunk[:, :, None] == jnp.arange(n_chunks, dtype=jnp.int32)[None, None, :]
    pick = lambda tab: jnp.sum(jnp.where(at_chunk, _select(tab, tile_expert)[:, None, :], 0), axis=2)
    piece = pick(first) + k - pick(cum - npc_t)
    cap = _chunk_rows(n_chunks)
    rows = chunk * cap + piece * PIECE
    d = (tile_idx % 2)[:, None] * PIECES_PER_TILE + jnp.arange(PIECES_PER_TILE, dtype=jnp.int32)[None, :]
    spare_row = lambda d: (d % n_chunks) * cap + (MAX_CHUNK_PIECES + d // n_chunks) * PIECE
    spare = spare_row(d)
    src = jnp.where(valid, rows, spare_row(N_SPARE - 1)).astype(jnp.int32).reshape(-1)
    dst = jnp.where(valid, rows, spare).astype(jnp.int32).reshape(-1)
    return src, dst, tile_expert.astype(jnp.int32), active.astype(jnp.int32)


def _sample_attn_kernel(x_ref, rope_ref, win_ref, sink_ref, ck_ref, cv_ref, cmk_ref, cmv_ref,
                        z_ref, oa_ref, oc_ref, nk_ref, nv_ref):
    j = pl.program_id(0)
    db = x_ref.shape[0]

    @pl.when(j == 0)
    def _():
        xb = x_ref[...].astype(BF16)
        for c0 in range(0, IN_WIDTH, IN_CHUNK):
            z_ref[:, c0:c0 + IN_CHUNK] = _dot(xb, win_ref[:, c0:c0 + IN_CHUNK])
        c = rope_ref[0]
        s1 = rope_ref[1]
        s2 = rope_ref[2]
        for jj in range((Q_WIDTH + KV_WIDTH) // LANES):
            sl = slice(jj * LANES, (jj + 1) * LANES)
            z_ref[:, sl] = _rope(z_ref[:, sl], c, s1, s2)

    r0 = pl.multiple_of(j * SB, SB)
    zq = z_ref[pl.ds(r0, SB), Q0:Q0 + Q_WIDTH]
    zk = z_ref[pl.ds(r0, SB), K0:K0 + KV_WIDTH]
    zv = z_ref[pl.ds(r0, SB), V0:V0 + KV_WIDTH]
    zc = z_ref[pl.ds(r0, SB), CQ0:CQ0 + MEM_WIDTH]
    sink = sink_ref[:, 0:1]
    row_kv = lax.broadcasted_iota(jnp.int32, (N_HEADS, KV_WIDTH), 0) & (N_KV - 1)
    lane_kv = lax.broadcasted_iota(jnp.int32, (N_HEADS, KV_WIDTH), 1) // HEAD_DIM
    own = row_kv == lane_kv
    row_c = lax.broadcasted_iota(jnp.int32, (N_HEADS, MEM_WIDTH), 0)
    lane_c = lax.broadcasted_iota(jnp.int32, (N_HEADS, MEM_WIDTH), 1) // HEAD_DIM
    own_c = row_c == lane_c
    last_row = lax.broadcasted_iota(jnp.int32, (WINDOW, KV_WIDTH), 0) == WINDOW - 1

    oa_rows = [[] for _ in range(GROUP)]
    oc_rows = []
    for b in range(SB):
        kc = ck_ref[b]
        vc = cv_ref[b]
        knew = zk[b:b + 1, :]
        vnew = zv[b:b + 1, :]
        qblk = jnp.concatenate(
            [jnp.broadcast_to(zq[b:b + 1, g * KV_WIDTH:(g + 1) * KV_WIDTH], (N_KV, KV_WIDTH)) for g in range(GROUP)],
            axis=0)
        qblk = jnp.where(own, qblk, 0.0).astype(BF16)
        s = _dot_nt(qblk, kc.astype(BF16))
        s_new = jnp.sum(qblk.astype(F32) * knew.astype(BF16).astype(F32), axis=1, keepdims=True)
        m = jnp.maximum(jnp.maximum(jnp.max(s, axis=1, keepdims=True), s_new), sink)
        p = jnp.exp(s - m)
        p_new = jnp.exp(s_new - m)
        inv = 1.0 / (jnp.sum(p, axis=1, keepdims=True) + p_new + jnp.exp(sink - m))
        o = (_dot((p * inv).astype(BF16), vc.astype(BF16))
             + (p_new * inv).astype(BF16).astype(F32) * vnew.astype(BF16).astype(F32))
        o = jnp.where(own, o, 0.0)
        for g in range(GROUP):
            oa_rows[g].append(jnp.sum(o[g * N_KV:(g + 1) * N_KV], axis=0, keepdims=True))
        nk_ref[b] = jnp.where(last_row, knew, pltpu.roll(kc, WINDOW - 1, 0))
        nv_ref[b] = jnp.where(last_row, vnew, pltpu.roll(vc, WINDOW - 1, 0))
        cblk = jnp.where(own_c, jnp.broadcast_to(zc[b:b + 1, :], (N_HEADS, MEM_WIDTH)), 0.0)
        sc = _dot_nt(cblk.astype(BF16), cmk_ref[b].astype(BF16))
        mc = jnp.max(sc, axis=1, keepdims=True)
        pc = jnp.exp(sc - mc)
        pc = pc * (1.0 / jnp.sum(pc, axis=1, keepdims=True))
        ocb = jnp.where(own_c, _dot(pc.astype(BF16), cmv_ref[b].astype(BF16)), 0.0)
        oc_rows.append(jnp.sum(ocb, axis=0, keepdims=True))
    for g in range(GROUP):
        oa_ref[pl.ds(r0, SB), g * KV_WIDTH:(g + 1) * KV_WIDTH] = jnp.concatenate(oa_rows[g], axis=0)
    oc_ref[pl.ds(r0, SB), :] = jnp.concatenate(oc_rows, axis=0)


def _sample_attn(x, rope, win, sink_gk, ck, cv, cmk, cmv):
    db = x.shape[0]
    blk = lambda r: pl.BlockSpec((SB, r, KV_WIDTH), lambda j: (j, 0, 0))
    full = lambda w: pl.BlockSpec((db, w), lambda j: (0, 0))
    return pl.pallas_call(
        _sample_attn_kernel,
        grid=(db // SB,),
        in_specs=[
            full(D_MODEL), _const_spec((3, 1, LANES)), _const_spec((D_MODEL, IN_WIDTH)),
            _const_spec((N_HEADS, LANES)),
            blk(WINDOW), blk(WINDOW), blk(N_MEM), blk(N_MEM),
        ],
        out_specs=[full(IN_WIDTH), full(Q_WIDTH), full(MEM_WIDTH), blk(WINDOW), blk(WINDOW)],
        out_shape=[
            jax.ShapeDtypeStruct((db, IN_WIDTH), F32),
            jax.ShapeDtypeStruct((db, Q_WIDTH), F32),
            jax.ShapeDtypeStruct((db, MEM_WIDTH), F32),
            jax.ShapeDtypeStruct((db, WINDOW, KV_WIDTH), F32),
            jax.ShapeDtypeStruct((db, WINDOW, KV_WIDTH), F32),
        ],
        compiler_params=pltpu.CompilerParams(dimension_semantics=("arbitrary",), vmem_limit_bytes=VMEM_LIMIT),
        name="sample_attn",
    )(x, rope, win, sink_gk, ck, cv, cmk, cmv)


def _sample_tail_kernel(x_ref, z_ref, oa_ref, oc_ref, st_ref, wa_ref, wb_ref, wc_ref, wo_ref, wmix_ref,
                        pscale_ref, g1_ref, b1_ref, wr_ref, br_ref, wg_ref, wu_ref, wd_ref, g2_ref, b2_ref,
                        y_ref, npool_ref, h_sc, comb_sc, acc_sc):
    e = pl.program_id(0)

    @pl.when(e == 0)
    def _():
        u = z_ref[:, U0:U0 + POOL_WIDTH]
        npool_ref[:, 0:(POOL_STATE - 1) * POOL_WIDTH] = st_ref[:, POOL_WIDTH:POOL_STATE * POOL_WIDTH]
        npool_ref[:, (POOL_STATE - 1) * POOL_WIDTH:POOL_STATE * POOL_WIDTH] = u
        obs = []
        for g, w in enumerate(POOL_WINDOWS):
            sl = slice(g * POOL_GROUP_DIM, (g + 1) * POOL_GROUP_DIM)
            cur = u[:, sl]
            ws = cur
            for jj in range(1, w):
                base = (POOL_STATE - jj) * POOL_WIDTH
                ws = ws + st_ref[:, base + g * POOL_GROUP_DIM:base + (g + 1) * POOL_GROUP_DIM]
            cnt = float(min(PAST_LEN + 1, w))
            pooled = ws / cnt - cur
            obs.append(_dot(pooled.astype(BF16), wmix_ref[g]) * pscale_ref[:, sl])
        ob = jnp.concatenate(obs, axis=1)
        h = _merge_ln1(x_ref[...], oa_ref[...].astype(BF16), ob, oc_ref[...], z_ref[:, GZ0:GZ0 + 3 * D_MODEL],
                       wa_ref, wb_ref, wc_ref, wo_ref, g1_ref[...], b1_ref[...])
        h_sc[...] = h
        logits = _dot(h.astype(BF16), wr_ref[...]) + br_ref[...]
        hot1, hot2, w1, w2 = _route(logits)
        comb_sc[...] = jnp.where(hot1, w1, 0.0) + jnp.where(hot2, w2, 0.0)
        acc_sc[...] = jnp.zeros_like(acc_sc)

    out = _expert_mlp(h_sc[...].astype(BF16), wg_ref[...].astype(BF16), wu_ref[...].astype(BF16),
                      wd_ref[...].astype(BF16))
    lane = lax.broadcasted_iota(jnp.int32, comb_sc.shape, 1)
    ce = jnp.sum(jnp.where(lane == e, comb_sc[...], 0.0), axis=1, keepdims=True)
    acc_sc[...] += ce * out

    @pl.when(e == pl.num_programs(0) - 1)
    def _():
        y_ref[...] = _layer_norm(ALPHA * h_sc[...] + acc_sc[...], g2_ref[...], b2_ref[...])


def _sample_tail(x, z, oa, oc, state, wa, wb, wc, wo, wmix, pscale, g1, b1, wr, br, wg, wu, wd, g2, b2):
    db = x.shape[0]
    full = lambda w: pl.BlockSpec((db, w), lambda e: (0, 0))
    vec = lambda w: pl.BlockSpec((1, w), lambda e: (0, 0))
    return pl.pallas_call(
        _sample_tail_kernel,
        grid=(N_EXPERTS,),
        in_specs=[
            full(D_MODEL), full(IN_WIDTH), full(Q_WIDTH), full(MEM_WIDTH), full(POOL_STATE * POOL_WIDTH),
            _const_spec((Q_WIDTH, D_MODEL)), _const_spec((POOL_WIDTH, D_MODEL)),
            _const_spec((MEM_WIDTH, D_MODEL)), _const_spec((D_MODEL, D_MODEL)),
            _const_spec((len(POOL_WINDOWS), POOL_GROUP_DIM, POOL_GROUP_DIM)),
            vec(POOL_WIDTH), vec(D_MODEL), vec(D_MODEL),
            _const_spec((D_MODEL, LANES)), vec(LANES),
            pl.BlockSpec((None, D_MODEL, D_EXPERT), lambda e: (e, 0, 0)),
            pl.BlockSpec((None, D_MODEL, D_EXPERT), lambda e: (e, 0, 0)),
            pl.BlockSpec((None, D_EXPERT, D_MODEL), lambda e: (e, 0, 0)),
            vec(D_MODEL), vec(D_MODEL),
        ],
        out_specs=[full(D_MODEL), full(POOL_STATE * POOL_WIDTH)],
        out_shape=[jax.ShapeDtypeStruct((db, D_MODEL), F32),
                   jax.ShapeDtypeStruct((db, POOL_STATE * POOL_WIDTH), F32)],
        scratch_shapes=[pltpu.VMEM((db, D_MODEL), F32), pltpu.VMEM((db, LANES), F32),
                        pltpu.VMEM((db, D_MODEL), F32)],
        compiler_params=pltpu.CompilerParams(dimension_semantics=("arbitrary",), vmem_limit_bytes=VMEM_LIMIT),
        name="sample_tail",
    )(x, z, oa, oc, state, wa, wb, wc, wo, wmix, pscale, g1, b1, wr, br, wg, wu, wd, g2, b2)


def _rope_tables(pos):
    half = ROPE_DIM // 2
    inv = jnp.power(ROPE_THETA, -jnp.arange(half, dtype=F32) * (2.0 / ROPE_DIM))
    ang = pos.astype(F32)[:, None] * inv[None, :]
    lane = np.arange(LANES)
    off = lane % HEAD_DIM
    cos = jnp.cos(ang)[:, lane % half]
    sin = jnp.sin(ang)[:, lane % half]
    c = jnp.where(off[None, :] < ROPE_DIM, cos, 1.0)
    s1 = jnp.where((off[None, :] >= half) & (off[None, :] < ROPE_DIM), sin, 0.0)
    s2 = jnp.where(off[None, :] < half, -sin, 0.0)
    return jnp.stack([c, s1, s2]).astype(F32)


def _q_heads_group_major(w, axis):
    if axis == 1:
        n = w.shape[0]
        return w.reshape(n, N_KV, GROUP, HEAD_DIM).transpose(0, 2, 1, 3).reshape(n, Q_WIDTH)
    n = w.shape[1]
    return w.reshape(N_KV, GROUP, HEAD_DIM, n).transpose(1, 0, 2, 3).reshape(Q_WIDTH, n)


def kernel(x_prompt, x_sample, cache_win_k, cache_win_v, state_pool, cache_mem_k, cache_mem_v, mem_prompt, w_in, sinks, w_pool_mix, pool_scale, w_mem_k, w_mem_v, w_branch_a, w_branch_b, w_branch_c, w_out, ln1_g, ln1_b, w_group, b_group, w_router, b_router, w_gate, w_up, w_down, ln2_g, ln2_b):
    assert w_in.shape[0] == DEPTH == 1
    b, l, _ = x_prompt.shape
    db, ds, _ = x_sample.shape
    assert ds == 1 and l % TM == 0 and db % SB == 0
    assert cache_win_k.shape[2] == WINDOW
    t = b * l

    win = w_in[0]
    scale = HEAD_DIM ** -0.5
    wq = _q_heads_group_major(win[:, Q0:Q0 + Q_WIDTH], 1) * scale
    wcq = win[:, CQ0:CQ0 + MEM_WIDTH] * scale
    win_b = jnp.concatenate([wq, win[:, K0:CQ0], wcq, win[:, GZ0:]], axis=1).astype(BF16)
    wa = _q_heads_group_major(w_branch_a[0], 0).astype(BF16)
    wb = w_branch_b[0].astype(BF16)
    wc = w_branch_c[0].astype(BF16)
    wo = w_out[0].astype(BF16)
    wmix = w_pool_mix[0].astype(BF16)
    pscale = pool_scale[0].reshape(1, POOL_WIDTH)
    g1 = ln1_g[0].reshape(1, D_MODEL)
    b1 = ln1_b[0].reshape(1, D_MODEL)
    g2 = ln2_g[0].reshape(1, D_MODEL)
    b2 = ln2_b[0].reshape(1, D_MODEL)
    wr = jnp.concatenate([w_group[0], w_router[0].reshape(D_MODEL, N_EXPERTS)], axis=1)
    wr = jnp.pad(wr, ((0, 0), (0, LANES - wr.shape[1]))).astype(BF16)
    br = jnp.pad(jnp.concatenate([b_group[0], b_router[0].reshape(N_EXPERTS)]), (0, LANES - N_EXPERT_GROUPS - N_EXPERTS))
    br = br.reshape(1, LANES).astype(F32)
    wg = w_gate[0]
    wu = w_up[0]
    wd = w_down[0]
    sink = sinks[0].astype(F32)
    sink_gk = jnp.broadcast_to(sink.reshape(N_KV, GROUP).T.reshape(N_HEADS, 1), (N_HEADS, LANES))

    mk, mv = _mem_project(mem_prompt, w_mem_k[0].astype(BF16), w_mem_v[0].astype(BF16))
    rope_p = _rope_tables(jnp.arange(l, dtype=jnp.int32))
    h, xs, route, counts, nk_p, nv_p, npool_p = _front(
        x_prompt, rope_p, sink, win_b, wa, wb, wc, wo, wmix, pscale, mk, mv, g1, b1, wr, br)
    piece_src, piece_dst, tile_expert, tile_active = _piece_tables(
        counts.reshape(-1, SUBLANES, LANES)[:, 0, :N_EXPERTS].astype(jnp.int32))
    ys = _grouped_gemm(piece_src, piece_dst, tile_expert, tile_active, xs, wg, wu, wd)
    y_p = _combine(ys, h.reshape(t, D_MODEL), route.reshape(t, LANES), g2, b2).reshape(b, l, D_MODEL)

    rope_s = _rope_tables(jnp.full((1,), PAST_LEN, jnp.int32))
    xs = x_sample.reshape(db, D_MODEL)
    ck = cache_win_k[0].reshape(db, WINDOW, KV_WIDTH)
    cv = cache_win_v[0].reshape(db, WINDOW, KV_WIDTH)
    cmk = cache_mem_k[0].reshape(db, N_MEM, MEM_WIDTH)
    cmv = cache_mem_v[0].reshape(db, N_MEM, MEM_WIDTH)
    z_s, oa_s, oc_s, nk_s, nv_s = _sample_attn(xs, rope_s, win_b, sink_gk, ck, cv, cmk, cmv)
    state = state_pool[0].reshape(db, POOL_STATE * POOL_WIDTH)
    y_s, npool_s = _sample_tail(xs, z_s, oa_s, oc_s, state, wa, wb, wc, wo, wmix, pscale, g1, b1, wr, br,
                                wg, wu, wd, g2, b2)

    kv5 = lambda a, n, w: a.reshape(1, n, w, N_KV, HEAD_DIM)
    return (y_p, y_s.reshape(db, 1, D_MODEL),
            kv5(nk_p, b, QB), kv5(nv_p, b, QB),
            npool_p[:, 2 * SUBLANES - POOL_STATE:, :][None],
            kv5(mk, b, N_MEM), kv5(mv, b, N_MEM),
            kv5(nk_s, db, WINDOW), kv5(nv_s, db, WINDOW),
            npool_s.reshape(1, db, POOL_STATE, POOL_WIDTH))
```

```python
import functools

import jax
import jax.numpy as jnp
import numpy as np
from jax import lax
from jax.experimental import pallas as pl
from jax.experimental.pallas import tpu as pltpu

D_MODEL = 1024
N_HEADS = 16
HEAD_DIM = 64
N_KV = 4
GROUP = N_HEADS // N_KV
WINDOW = 128
ROPE_THETA = 500000.0
ROPE_DIM = HEAD_DIM // 4
Q_WIDTH = N_HEADS * HEAD_DIM
KV_WIDTH = N_KV * HEAD_DIM
POOL_WINDOWS = (2, 4, 8, 16)
POOL_WIDTH = D_MODEL // 2
POOL_GROUP_DIM = POOL_WIDTH // len(POOL_WINDOWS)
POOL_STATE = max(POOL_WINDOWS) - 1
N_MEM = 256
MEM_HEADS = 4
MEM_WIDTH = MEM_HEADS * HEAD_DIM
N_EXPERT_GROUPS = 4
EXPERTS_PER_GROUP = 4
N_EXPERTS = N_EXPERT_GROUPS * EXPERTS_PER_GROUP
D_EXPERT = 512
PAST_LEN = 16384
DEPTH = 1
ALPHA = (2.0 * DEPTH) ** 0.25
LN_EPS = 1e-5

Q0 = 0
K0 = Q0 + Q_WIDTH
V0 = K0 + KV_WIDTH
U0 = V0 + KV_WIDTH
CQ0 = U0 + POOL_WIDTH
GZ0 = CQ0 + MEM_WIDTH
IN_WIDTH = GZ0 + 3 * D_MODEL

LANES = 128
SUBLANES = 8
VMEM_LIMIT = 56 * 1024 * 1024

TM = 256
QB = WINDOW
PIECE = 16
PIECES_PER_TILE = 16
TG = PIECE * PIECES_PER_TILE
MAX_CHUNK_PIECES = 2 * TM // PIECE + N_EXPERTS - 1
N_SPARE = 2 * PIECES_PER_TILE + 1


def _chunk_rows(n_chunks):
    spare = -(-N_SPARE // n_chunks)
    return -(-(MAX_CHUNK_PIECES + spare) * PIECE // LANES) * LANES
SB = 8
IN_CHUNK = 768
SOFTMAX_ROWS = 64
assert N_MEM == 2 * QB and MEM_HEADS == N_KV and MEM_WIDTH == KV_WIDTH and TM <= GROUP * QB

BF16 = jnp.bfloat16
F32 = jnp.float32
NEG_INF = float("-inf")


def _const_spec(shape):
    nd = len(shape)
    return pl.BlockSpec(shape, lambda *_: (0,) * nd, pipeline_mode=pl.Buffered(1))


def _layer_norm(x, g, b):
    mu = jnp.mean(x, axis=-1, keepdims=True)
    xc = x - mu
    var = jnp.mean(xc * xc, axis=-1, keepdims=True)
    return xc * lax.rsqrt(var + LN_EPS) * g + b


def _dot(a, b):
    return jnp.dot(a, b, preferred_element_type=F32)


def _dot_nt(a, b):
    return lax.dot_general(a, b, (((1,), (1,)), ((), ())), preferred_element_type=F32)


def _lane_block_mask(shape, block, width=HEAD_DIM):
    lane = lax.broadcasted_iota(jnp.int32, shape, len(shape) - 1)
    return (lane >= block * width) & (lane < (block + 1) * width)


def _rope(x, c, s1, s2):
    half = ROPE_DIM // 2
    return x * c + pltpu.roll(x, half, 1) * s1 + pltpu.roll(x, LANES - half, 1) * s2


def _route(logits):
    rows = logits.shape[0]
    lane = lax.broadcasted_iota(jnp.int32, (rows, LANES), 1)
    lanef = lane.astype(F32)
    big = float(LANES)
    is_g = lane < N_EXPERT_GROUPS
    glog = jnp.where(is_g, logits, NEG_INF)
    gmax = jnp.max(glog, axis=1, keepdims=True)
    gsum = jnp.sum(jnp.where(is_g, jnp.exp(glog - gmax), 0.0), axis=1, keepdims=True)
    gp = 1.0 / gsum
    gidx = jnp.min(jnp.where(glog == gmax, lanef, big), axis=1, keepdims=True).astype(jnp.int32)
    lo = N_EXPERT_GROUPS + gidx * EXPERTS_PER_GROUP
    in_grp = (lane >= lo) & (lane < lo + EXPERTS_PER_GROUP)
    el = jnp.where(in_grp, logits, NEG_INF)
    v1 = jnp.max(el, axis=1, keepdims=True)
    i1 = jnp.min(jnp.where(el == v1, lanef, big), axis=1, keepdims=True).astype(jnp.int32)
    el2 = jnp.where(lane == i1, NEG_INF, el)
    v2 = jnp.max(el2, axis=1, keepdims=True)
    i2 = jnp.min(jnp.where(el2 == v2, lanef, big), axis=1, keepdims=True).astype(jnp.int32)
    e21 = jnp.exp(v2 - v1)
    inv = 1.0 / (1.0 + e21)
    w1 = inv * gp
    w2 = e21 * inv * gp
    e1 = i1 - N_EXPERT_GROUPS
    e2 = i2 - N_EXPERT_GROUPS
    return lane == e1, lane == e2, w1, w2


def _local_sort(hot1, hot2, w1, w2, hb, cap):
    rows = hb.shape[0]
    lane = lax.broadcasted_iota(jnp.int32, (rows, LANES), 1)
    onehot = jnp.where(hot1 | hot2, 1.0, 0.0)
    counts = jnp.sum(onehot, axis=0, keepdims=True)
    before = (lax.broadcasted_iota(jnp.int32, (rows, rows), 1)
              < lax.broadcasted_iota(jnp.int32, (rows, rows), 0)).astype(BF16)
    rank = _dot(before, onehot.astype(BF16))
    run = (((counts.astype(jnp.int32) + (PIECE - 1)) // PIECE) * PIECE).astype(F32)
    lower = (lax.broadcasted_iota(jnp.int32, (LANES, LANES), 0)
             < lax.broadcasted_iota(jnp.int32, (LANES, LANES), 1)).astype(BF16)
    start = _dot(jnp.broadcast_to(run, (SUBLANES, LANES)).astype(BF16), lower)[0:1]
    slot = start + rank
    s1 = jnp.sum(jnp.where(hot1, slot, 0.0), axis=1, keepdims=True)
    s2 = jnp.sum(jnp.where(hot2, slot, 0.0), axis=1, keepdims=True)
    route = jnp.where(lane == 0, w1, jnp.where(lane == 1, w2, jnp.where(lane == 2, s1, jnp.where(lane == 3, s2, 0.0))))
    route_t = route.T
    srow = lax.broadcasted_iota(jnp.int32, (cap, rows), 0).astype(F32)
    perm = jnp.where((srow == route_t[2:3, :]) | (srow == route_t[3:4, :]), 1.0, 0.0).astype(BF16)
    return _dot(perm, hb).astype(BF16), route, counts


def _sigmoid(x):
    return 0.5 * jnp.tanh(0.5 * x) + 0.5


def _merge_ln1(x, oa, ob, oc, gz, wa_ref, wb_ref, wc_ref, wo_ref, g1, b1):
    ya = _dot(oa, wa_ref[...])
    yb = _dot(ob.astype(BF16), wb_ref[...])
    yc = _dot(oc.astype(BF16), wc_ref[...])
    m = (_sigmoid(gz[:, 0:D_MODEL]) * ya
         + _sigmoid(gz[:, D_MODEL:2 * D_MODEL]) * yb
         + _sigmoid(gz[:, 2 * D_MODEL:3 * D_MODEL]) * yc)
    hpre = ALPHA * x + _dot(m.astype(BF16), wo_ref[...])
    return _layer_norm(hpre, g1, b1)


def _mem_kernel(mem_ref, wk_ref, wv_ref, mk_ref, mv_ref):
    m = mem_ref[...].astype(BF16)
    mk_ref[...] = _dot(m, wk_ref[...])
    mv_ref[...] = _dot(m, wv_ref[...])


def _mem_project(mem, wk, wv):
    b = mem.shape[0]
    out = jax.ShapeDtypeStruct((b, N_MEM, MEM_WIDTH), F32)
    return pl.pallas_call(
        _mem_kernel,
        grid=(b,),
        in_specs=[pl.BlockSpec((None, N_MEM, D_MODEL), lambda i: (i, 0, 0)),
                  _const_spec((D_MODEL, MEM_WIDTH)), _const_spec((D_MODEL, MEM_WIDTH))],
        out_specs=[pl.BlockSpec((None, N_MEM, MEM_WIDTH), lambda i: (i, 0, 0))] * 2,
        out_shape=[out, out],
        name="mem_project",
    )(mem, wk, wv)


def _front_kernel(sinks_ref, x_ref, rope_ref, win_ref, wa_ref, wb_ref, wc_ref, wo_ref, wmix_ref,
                  pscale_ref, mk_ref, mv_ref, g1_ref, b1_ref, wr_ref, br_ref,
                  h_ref, xs_ref, route_ref, counts_ref, nk_ref, nv_ref, npool_ref,
                  z_ref, qb_ref, kext_ref, vext_ref, uext_ref, oa_ref, ob_ref, bias_ref, s_ref, p_ref, vblk_ref):
    i = pl.program_id(1)
    x = x_ref[...]
    xb = x.astype(BF16)

    @pl.when(i == 0)
    def _():
        kext_ref[0:QB, :] = jnp.zeros((QB, KV_WIDTH), BF16)
        vext_ref[0:QB, :] = jnp.zeros((QB, KV_WIDTH), BF16)
        uext_ref[0:2 * SUBLANES, :] = jnp.zeros((2 * SUBLANES, POOL_WIDTH), F32)

    @pl.when(i > 0)
    def _():
        kext_ref[0:QB, :] = kext_ref[TM:TM + QB, :]
        vext_ref[0:QB, :] = vext_ref[TM:TM + QB, :]
        uext_ref[0:2 * SUBLANES, :] = uext_ref[TM:TM + 2 * SUBLANES, :]

    for c0 in range(0, IN_WIDTH, IN_CHUNK):
        z_ref[:, c0:c0 + IN_CHUNK] = _dot(xb, win_ref[:, c0:c0 + IN_CHUNK])

    c = rope_ref[0]
    s1 = rope_ref[1]
    s2 = rope_ref[2]
    for j in range(Q_WIDTH // LANES):
        sl = slice(Q0 + j * LANES, Q0 + (j + 1) * LANES)
        qb_ref[:, j * LANES:(j + 1) * LANES] = _rope(z_ref[:, sl], c, s1, s2).astype(BF16)
    for j in range(KV_WIDTH // LANES):
        sl = slice(K0 + j * LANES, K0 + (j + 1) * LANES)
        kr = _rope(z_ref[:, sl], c, s1, s2)
        z_ref[:, sl] = kr
        kext_ref[QB:QB + TM, j * LANES:(j + 1) * LANES] = kr.astype(BF16)
    vext_ref[QB:QB + TM, :] = z_ref[:, V0:V0 + KV_WIDTH].astype(BF16)
    nk_ref[...] = z_ref[TM - QB:TM, K0:K0 + KV_WIDTH]
    nv_ref[...] = z_ref[TM - QB:TM, V0:V0 + KV_WIDTH]

    rowq = lax.broadcasted_iota(jnp.int32, (QB, 2 * QB), 0)
    colk = lax.broadcasted_iota(jnp.int32, (QB, 2 * QB), 1)
    band = (colk >= rowq) & (colk <= rowq + WINDOW)
    bias_ref[1] = jnp.where(band, 0.0, NEG_INF)
    bias_ref[0] = jnp.where(band & ((colk >= QB) | (i > 0)), 0.0, NEG_INF)
    for sb in range(TM // QB):
        k2 = kext_ref[sb * QB:(sb + 2) * QB, :]
        v2 = vext_ref[sb * QB:(sb + 2) * QB, :]
        qs = jnp.concatenate(
            [qb_ref[sb * QB:(sb + 1) * QB, g * KV_WIDTH:(g + 1) * KV_WIDTH] for g in range(GROUP)], axis=0)
        for kv in range(N_KV):
            kmask = _lane_block_mask((2 * QB, KV_WIDTH), kv)
            s_ref[...] = _dot_nt(qs, jnp.where(kmask, k2, jnp.zeros_like(k2)))
            vblk_ref[kv * 2 * QB:(kv + 1) * 2 * QB, :] = jnp.where(kmask, v2, jnp.zeros_like(v2))
            for c0 in range(0, GROUP * QB, SOFTMAX_ROWS):
                rq = c0 % QB
                sink = sinks_ref[kv * GROUP + c0 // QB]
                s = s_ref[c0:c0 + SOFTMAX_ROWS, :] + bias_ref[min(sb, 1), rq:rq + SOFTMAX_ROWS, :]
                m = jnp.maximum(jnp.max(s, axis=1, keepdims=True), sink)
                p = jnp.exp(s - m)
                den = jnp.sum(p, axis=1, keepdims=True) + jnp.exp(sink - m)
                p_ref[c0:c0 + SOFTMAX_ROWS, kv * 2 * QB:(kv + 1) * 2 * QB] = (p * (1.0 / den)).astype(BF16)
        o = _dot(p_ref[...], vblk_ref[...])
        for g in range(GROUP):
            oa_ref[sb * QB:(sb + 1) * QB, g * KV_WIDTH:(g + 1) * KV_WIDTH] = o[g * QB:(g + 1) * QB].astype(BF16)

    hist = 2 * SUBLANES
    uext_ref[hist:hist + TM, :] = z_ref[:, U0:U0 + POOL_WIDTH]
    npool_ref[...] = uext_ref[TM:TM + hist, :]
    pos = i * TM + lax.broadcasted_iota(jnp.int32, (TM, 1), 0)
    for g, w in enumerate(POOL_WINDOWS):
        sl = slice(g * POOL_GROUP_DIM, (g + 1) * POOL_GROUP_DIM)
        cur = uext_ref[hist:hist + TM, sl]
        ws = cur
        for j in range(1, w):
            ws = ws + uext_ref[hist - j:hist - j + TM, sl]
        cnt = jnp.minimum(pos + 1, w).astype(F32)
        pooled = ws / cnt - cur
        ob_ref[:, sl] = _dot(pooled.astype(BF16), wmix_ref[g]) * pscale_ref[:, sl]

    cq = z_ref[:, CQ0:CQ0 + MEM_WIDTH].astype(BF16)
    mk = mk_ref[...].astype(BF16)
    mv = mv_ref[...].astype(BF16)
    for hh in range(MEM_HEADS):
        hmask = _lane_block_mask((N_MEM, MEM_WIDTH), hh)
        s_ref[0:TM, :] = _dot_nt(cq, jnp.where(hmask, mk, jnp.zeros_like(mk)))
        vblk_ref[hh * N_MEM:(hh + 1) * N_MEM, :] = jnp.where(hmask, mv, jnp.zeros_like(mv))
        for c0 in range(0, TM, SOFTMAX_ROWS):
            s = s_ref[c0:c0 + SOFTMAX_ROWS, :]
            p = jnp.exp(s - jnp.max(s, axis=1, keepdims=True))
            den = jnp.sum(p, axis=1, keepdims=True)
            p_ref[c0:c0 + SOFTMAX_ROWS, hh * N_MEM:(hh + 1) * N_MEM] = (p * (1.0 / den)).astype(BF16)
    oc = _dot(p_ref[0:TM, :], vblk_ref[...])

    h = _merge_ln1(x, oa_ref[...], ob_ref[...], oc, z_ref[:, GZ0:GZ0 + 3 * D_MODEL],
                   wa_ref, wb_ref, wc_ref, wo_ref, g1_ref[...], b1_ref[...])
    h_ref[...] = h
    hb = h.astype(BF16)
    logits = _dot(hb, wr_ref[...]) + br_ref[...]
    xs, route, counts = _local_sort(*_route(logits), hb, xs_ref.shape[0])
    xs_ref[...] = xs
    route_ref[...] = route
    counts_ref[...] = jnp.broadcast_to(counts, (SUBLANES, LANES))


def _front(x, rope, sinks, win, wa, wb, wc, wo, wmix, pscale, mk, mv, g1, b1, wr, br):
    b, l, _ = x.shape
    nt = l // TM
    cap = _chunk_rows(b * nt)
    hist = 2 * SUBLANES
    tile = lambda w: pl.BlockSpec((None, TM, w), lambda bi, ti: (bi, ti, 0))
    per_b = lambda r, w: pl.BlockSpec((None, r, w), lambda bi, ti: (bi, 0, 0))
    return pl.pallas_call(
        _front_kernel,
        grid=(b, nt),
        in_specs=[
            pl.BlockSpec(memory_space=pltpu.SMEM),
            tile(D_MODEL),
            pl.BlockSpec((3, TM, LANES), lambda bi, ti: (0, ti, 0)),
            _const_spec((D_MODEL, IN_WIDTH)),
            _const_spec((Q_WIDTH, D_MODEL)), _const_spec((POOL_WIDTH, D_MODEL)),
            _const_spec((MEM_WIDTH, D_MODEL)), _const_spec((D_MODEL, D_MODEL)),
            _const_spec((len(POOL_WINDOWS), POOL_GROUP_DIM, POOL_GROUP_DIM)),
            _const_spec((1, POOL_WIDTH)),
            per_b(N_MEM, MEM_WIDTH), per_b(N_MEM, MEM_WIDTH),
            _const_spec((1, D_MODEL)), _const_spec((1, D_MODEL)),
            _const_spec((D_MODEL, LANES)), _const_spec((1, LANES)),
        ],
        out_specs=[
            tile(D_MODEL),
            pl.BlockSpec((cap, D_MODEL), lambda bi, ti: (bi * nt + ti, 0)),
            tile(LANES),
            pl.BlockSpec((None, None, SUBLANES, LANES), lambda bi, ti: (bi, ti, 0, 0)),
            per_b(QB, KV_WIDTH), per_b(QB, KV_WIDTH), per_b(hist, POOL_WIDTH),
        ],
        out_shape=[
            jax.ShapeDtypeStruct((b, l, D_MODEL), F32),
            jax.ShapeDtypeStruct((b * nt * cap, D_MODEL), BF16),
            jax.ShapeDtypeStruct((b, l, LANES), F32),
            jax.ShapeDtypeStruct((b, nt, SUBLANES, LANES), F32),
            jax.ShapeDtypeStruct((b, QB, KV_WIDTH), F32),
            jax.ShapeDtypeStruct((b, QB, KV_WIDTH), F32),
            jax.ShapeDtypeStruct((b, hist, POOL_WIDTH), F32),
        ],
        scratch_shapes=[
            pltpu.VMEM((TM, IN_WIDTH), F32),
            pltpu.VMEM((TM, Q_WIDTH), BF16),
            pltpu.VMEM((QB + TM, KV_WIDTH), BF16),
            pltpu.VMEM((QB + TM, KV_WIDTH), BF16),
            pltpu.VMEM((hist + TM, POOL_WIDTH), F32),
            pltpu.VMEM((TM, Q_WIDTH), BF16),
            pltpu.VMEM((TM, POOL_WIDTH), F32),
            pltpu.VMEM((2, QB, 2 * QB), F32),
            pltpu.VMEM((GROUP * QB, 2 * QB), F32),
            pltpu.VMEM((GROUP * QB, N_KV * 2 * QB), BF16),
            pltpu.VMEM((N_KV * 2 * QB, KV_WIDTH), BF16),
        ],
        compiler_params=pltpu.CompilerParams(
            dimension_semantics=("arbitrary", "arbitrary"), vmem_limit_bytes=VMEM_LIMIT),
        name="front_prompt",
    )(sinks, x, rope, win, wa, wb, wc, wo, wmix, pscale, mk, mv, g1, b1, wr, br)


def _expert_mlp(xb, wg, wu, wd):
    a = _dot(xb, wg)
    hid = (a * jax.nn.sigmoid(a)) * _dot(xb, wu)
    return _dot(hid.astype(BF16), wd)


def _gemm_kernel(src_ref, dst_ref, te_ref, act_ref, xs_ref, wg_ref, wu_ref, wd_ref, ys_ref,
                 xbuf, obuf, wgb, wub, wdb, prime, sem_in, sem_out):
    i = pl.program_id(0)
    n = pl.num_programs(0)
    slot = i % 2

    def start_in(tile, slot):
        for j in range(PIECES_PER_TILE):
            row0 = pl.multiple_of(src_ref[tile * PIECES_PER_TILE + j], PIECE)
            pltpu.make_async_copy(xs_ref.at[pl.ds(row0, PIECE)], xbuf.at[slot, pl.ds(j * PIECE, PIECE)],
                                  sem_in.at[slot]).start()

    def start_out(tile, slot):
        for j in range(PIECES_PER_TILE):
            row0 = pl.multiple_of(dst_ref[tile * PIECES_PER_TILE + j], PIECE)
            pltpu.make_async_copy(obuf.at[slot, pl.ds(j * PIECE, PIECE)], ys_ref.at[pl.ds(row0, PIECE)],
                                  sem_out.at[slot]).start()

    def wait_in(slot):
        for j in range(PIECES_PER_TILE):
            pltpu.make_async_copy(xs_ref.at[pl.ds(0, PIECE)], xbuf.at[slot, pl.ds(j * PIECE, PIECE)],
                                  sem_in.at[slot]).wait()

    def wait_out(slot):
        for j in range(PIECES_PER_TILE):
            pltpu.make_async_copy(obuf.at[slot, pl.ds(j * PIECE, PIECE)], ys_ref.at[pl.ds(0, PIECE)],
                                  sem_out.at[slot]).wait()

    active = act_ref[i] > 0
    prefetched = (i == 0) | (act_ref[jnp.maximum(i - 1, 0)] > 0)
    out_pending = (i < 2) | (act_ref[jnp.maximum(i - 2, 0)] > 0)

    @pl.when(i == 0)
    def _():
        prime[0] = jnp.zeros((PIECE, D_MODEL), BF16)
        for s in range(2):
            for j in range(PIECES_PER_TILE):
                pltpu.make_async_copy(prime.at[0], prime.at[1 + s * PIECES_PER_TILE + j], sem_out.at[s]).start()
        start_in(0, 0)

    @pl.when(active)
    def _():
        @pl.when((i == 0) | (te_ref[i] != te_ref[jnp.maximum(i - 1, 0)]))
        def _():
            wgb[...] = wg_ref[...].astype(BF16)
            wub[...] = wu_ref[...].astype(BF16)
            wdb[...] = wd_ref[...].astype(BF16)

        wait_in(slot)
        half = TG // 2
        xs_ = [xbuf[slot, 0:half], xbuf[slot, half:TG]]
        a = [_dot(x, wgb[...]) for x in xs_]
        start_in(i + 1, 1 - slot)
        b = [_dot(x, wub[...]) for x in xs_]
        hid = [((a_ * jax.nn.sigmoid(a_)) * b_).astype(BF16) for a_, b_ in zip(a, b)]
        wait_out(slot)
        obuf[slot, 0:half] = _dot(hid[0], wdb[...]).astype(BF16)
        obuf[slot, half:TG] = _dot(hid[1], wdb[...]).astype(BF16)
        start_out(i, slot)

    @pl.when(jnp.logical_not(active))
    def _():
        @pl.when(prefetched)
        def _():
            wait_in(slot)

        @pl.when(out_pending)
        def _():
            wait_out(slot)

    @pl.when(i == n - 1)
    def _():
        @pl.when(active)
        def _():
            wait_in(1 - slot)
            wait_out(slot)

        @pl.when((i >= 1) & (act_ref[jnp.maximum(i - 1, 0)] > 0))
        def _():
            wait_out(1 - slot)


def _grouped_gemm(piece_src, piece_dst, tile_expert, tile_active, xs, wg, wu, wd):
    n_tiles = tile_expert.shape[0]
    assert n_tiles >= 2
    wspec = lambda r, c: pl.BlockSpec((None, r, c), lambda i, src, dst, te, act: (te[i], 0, 0))
    return pl.pallas_call(
        _gemm_kernel,
        grid_spec=pltpu.PrefetchScalarGridSpec(
            num_scalar_prefetch=4,
            grid=(n_tiles,),
            in_specs=[pl.BlockSpec(memory_space=pl.ANY),
                      wspec(D_MODEL, D_EXPERT), wspec(D_MODEL, D_EXPERT), wspec(D_EXPERT, D_MODEL)],
            out_specs=pl.BlockSpec(memory_space=pl.ANY),
            scratch_shapes=[pltpu.VMEM((2, TG, D_MODEL), BF16), pltpu.VMEM((2, TG, D_MODEL), BF16),
                            pltpu.VMEM((D_MODEL, D_EXPERT), BF16), pltpu.VMEM((D_MODEL, D_EXPERT), BF16),
                            pltpu.VMEM((D_EXPERT, D_MODEL), BF16),
                            pltpu.VMEM((1 + 2 * PIECES_PER_TILE, PIECE, D_MODEL), BF16),
                            pltpu.SemaphoreType.DMA((2,)), pltpu.SemaphoreType.DMA((2,))],
        ),
        out_shape=jax.ShapeDtypeStruct(xs.shape, xs.dtype),
        input_output_aliases={4: 0},
        compiler_params=pltpu.CompilerParams(dimension_semantics=("arbitrary",), vmem_limit_bytes=VMEM_LIMIT),
        name="moe_grouped_gemm",
    )(piece_src, piece_dst, tile_expert, tile_active, xs, wg, wu, wd)


def _combine_kernel(ys_ref, h_ref, route_ref, g2_ref, b2_ref, y_ref):
    route = route_ref[...]
    slot = lax.broadcasted_iota(jnp.int32, (TM, ys_ref.shape[0]), 1).astype(F32)
    sel = jnp.concatenate([jnp.where(slot == route[:, 2:3], 1.0, 0.0).astype(BF16),
                           jnp.where(slot == route[:, 3:4], 1.0, 0.0).astype(BF16)], axis=0)
    picked = _dot(sel, ys_ref[...])
    f = route[:, 0:1] * picked[0:TM] + route[:, 1:2] * picked[TM:2 * TM]
    y_ref[...] = _layer_norm(ALPHA * h_ref[...] + f, g2_ref[...], b2_ref[...])


def _combine(ys, h, route, g2, b2):
    t = h.shape[0]
    return pl.pallas_call(
        _combine_kernel,
        grid=(t // TM,),
        in_specs=[
            pl.BlockSpec((ys.shape[0] // (t // TM), D_MODEL), lambda i: (i, 0)),
            pl.BlockSpec((TM, D_MODEL), lambda i: (i, 0)),
            pl.BlockSpec((TM, LANES), lambda i: (i, 0)),
            pl.BlockSpec((1, D_MODEL), lambda i: (0, 0)),
            pl.BlockSpec((1, D_MODEL), lambda i: (0, 0)),
        ],
        out_specs=pl.BlockSpec((TM, D_MODEL), lambda i: (i, 0)),
        out_shape=jax.ShapeDtypeStruct((t, D_MODEL), F32),
        compiler_params=pltpu.CompilerParams(dimension_semantics=("arbitrary",)),
        name="moe_combine",
    )(ys, h, route, g2, b2)


def _select(table, idx):
    hot = idx[:, None] == jnp.arange(table.shape[0], dtype=jnp.int32)[None, :]
    return jnp.sum(jnp.where(hot[:, :, None], table[None, :, :], 0), axis=1)


def _piece_tables(counts):
    n_chunks = counts.shape[0]
    n_tiles = -(-(n_chunks * MAX_CHUNK_PIECES + N_EXPERTS * (PIECES_PER_TILE - 1)) // PIECES_PER_TILE)
    npc = (counts + (PIECE - 1)) // PIECE
    first = (jnp.cumsum(npc, axis=1) - npc).T
    npc_t = npc.T
    cum = jnp.cumsum(npc_t, axis=1)
    per_expert = cum[:, -1]
    tiles_e = (per_expert + (PIECES_PER_TILE - 1)) // PIECES_PER_TILE
    tile_end = jnp.cumsum(tiles_e)
    tile_idx = jnp.arange(n_tiles, dtype=jnp.int32)
    expert_of = lambda i: jnp.minimum(jnp.sum(i[:, None] >= tile_end[None, :], axis=1), N_EXPERTS - 1).astype(jnp.int32)
    active = tile_idx < tile_end[-1]
    tile_expert = jnp.where(active, expert_of(tile_idx), expert_of(tile_end[-1:] - 1))
    meta = jnp.stack([tile_end - tiles_e, per_expert], axis=1)
    meta_t = _select(meta, tile_expert)
    k = (tile_idx - meta_t[:, 0])[:, None] * PIECES_PER_TILE + jnp.arange(PIECES_PER_TILE, dtype=jnp.int32)[None, :]
    valid = active[:, None] & (k < meta_t[:, 1:2])
    cum_t = _select(cum, tile_expert)
    chunk = jnp.minimum(jnp.sum(k[:, :, None] >= cum_t[:, None, :], axis=2), n_chunks - 1).astype(jnp.int32)
    at_chunk = chunk[:, :, None] == jnp.arange(n_chunks, dtype=jnp.int32)[None, None, :]
    pick = lambda tab: jnp.sum(jnp.where(at_chunk, _select(tab, tile_expert)[:, None, :], 0), axis=2)
    piece = pick(first) + k - pick(cum - npc_t)
    cap = _chunk_rows(n_chunks)
    rows = chunk * cap + piece * PIECE
    d = (tile_idx % 2)[:, None] * PIECES_PER_TILE + jnp.arange(PIECES_PER_TILE, dtype=jnp.int32)[None, :]
    spare_row = lambda d: (d % n_chunks) * cap + (MAX_CHUNK_PIECES + d // n_chunks) * PIECE
    spare = spare_row(d)
    zero_piece = spare_row(N_SPARE - 1)
    extra = jnp.full((PIECES_PER_TILE,), zero_piece, jnp.int32)
    src = jnp.concatenate([jnp.where(valid, rows, zero_piece).astype(jnp.int32).reshape(-1), extra])
    dst = jnp.concatenate([jnp.where(valid, rows, spare).astype(jnp.int32).reshape(-1), extra])
    return src, dst, tile_expert.astype(jnp.int32), active.astype(jnp.int32)


def _sample_attn_kernel(x_ref, rope_ref, win_ref, sink_ref, ck_ref, cv_ref, cmk_ref, cmv_ref,
                        z_ref, oa_ref, oc_ref, nk_ref, nv_ref):
    j = pl.program_id(0)
    db = x_ref.shape[0]

    @pl.when(j == 0)
    def _():
        xb = x_ref[...].astype(BF16)
        for c0 in range(0, IN_WIDTH, IN_CHUNK):
            z_ref[:, c0:c0 + IN_CHUNK] = _dot(xb, win_ref[:, c0:c0 + IN_CHUNK])
        c = rope_ref[0]
        s1 = rope_ref[1]
        s2 = rope_ref[2]
        for jj in range((Q_WIDTH + KV_WIDTH) // LANES):
            sl = slice(jj * LANES, (jj + 1) * LANES)
            z_ref[:, sl] = _rope(z_ref[:, sl], c, s1, s2)

    r0 = pl.multiple_of(j * SB, SB)
    zq = z_ref[pl.ds(r0, SB), Q0:Q0 + Q_WIDTH]
    zk = z_ref[pl.ds(r0, SB), K0:K0 + KV_WIDTH]
    zv = z_ref[pl.ds(r0, SB), V0:V0 + KV_WIDTH]
    zc = z_ref[pl.ds(r0, SB), CQ0:CQ0 + MEM_WIDTH]
    sink = sink_ref[:, 0:1]
    row_kv = lax.broadcasted_iota(jnp.int32, (N_HEADS, KV_WIDTH), 0) & (N_KV - 1)
    lane_kv = lax.broadcasted_iota(jnp.int32, (N_HEADS, KV_WIDTH), 1) // HEAD_DIM
    own = row_kv == lane_kv
    row_c = lax.broadcasted_iota(jnp.int32, (N_HEADS, MEM_WIDTH), 0)
    lane_c = lax.broadcasted_iota(jnp.int32, (N_HEADS, MEM_WIDTH), 1) // HEAD_DIM
    own_c = row_c == lane_c
    last_row = lax.broadcasted_iota(jnp.int32, (WINDOW, KV_WIDTH), 0) == WINDOW - 1

    oa_rows = [[] for _ in range(GROUP)]
    oc_rows = []
    for b in range(SB):
        kc = ck_ref[b]
        vc = cv_ref[b]
        knew = zk[b:b + 1, :]
        vnew = zv[b:b + 1, :]
        qblk = jnp.concatenate(
            [jnp.broadcast_to(zq[b:b + 1, g * KV_WIDTH:(g + 1) * KV_WIDTH], (N_KV, KV_WIDTH)) for g in range(GROUP)],
            axis=0)
        qblk = jnp.where(own, qblk, 0.0).astype(BF16)
        s = _dot_nt(qblk, kc.astype(BF16))
        s_new = jnp.sum(qblk.astype(F32) * knew.astype(BF16).astype(F32), axis=1, keepdims=True)
        m = jnp.maximum(jnp.maximum(jnp.max(s, axis=1, keepdims=True), s_new), sink)
        p = jnp.exp(s - m)
        p_new = jnp.exp(s_new - m)
        inv = 1.0 / (jnp.sum(p, axis=1, keepdims=True) + p_new + jnp.exp(sink - m))
        o = (_dot((p * inv).astype(BF16), vc.astype(BF16))
             + (p_new * inv).astype(BF16).astype(F32) * vnew.astype(BF16).astype(F32))
        o = jnp.where(own, o, 0.0)
        for g in range(GROUP):
            oa_rows[g].append(jnp.sum(o[g * N_KV:(g + 1) * N_KV], axis=0, keepdims=True))
        nk_ref[b] = jnp.where(last_row, knew, pltpu.roll(kc, WINDOW - 1, 0))
        nv_ref[b] = jnp.where(last_row, vnew, pltpu.roll(vc, WINDOW - 1, 0))
        cblk = jnp.where(own_c, jnp.broadcast_to(zc[b:b + 1, :], (N_HEADS, MEM_WIDTH)), 0.0)
        sc = _dot_nt(cblk.astype(BF16), cmk_ref[b].astype(BF16))
        mc = jnp.max(sc, axis=1, keepdims=True)
        pc = jnp.exp(sc - mc)
        pc = pc * (1.0 / jnp.sum(pc, axis=1, keepdims=True))
        ocb = jnp.where(own_c, _dot(pc.astype(BF16), cmv_ref[b].astype(BF16)), 0.0)
        oc_rows.append(jnp.sum(ocb, axis=0, keepdims=True))
    for g in range(GROUP):
        oa_ref[pl.ds(r0, SB), g * KV_WIDTH:(g + 1) * KV_WIDTH] = jnp.concatenate(oa_rows[g], axis=0)
    oc_ref[pl.ds(r0, SB), :] = jnp.concatenate(oc_rows, axis=0)


def _sample_attn(x, rope, win, sink_gk, ck, cv, cmk, cmv):
    db = x.shape[0]
    blk = lambda r: pl.BlockSpec((SB, r, KV_WIDTH), lambda j: (j, 0, 0))
    full = lambda w: pl.BlockSpec((db, w), lambda j: (0, 0))
    return pl.pallas_call(
        _sample_attn_kernel,
        grid=(db // SB,),
        in_specs=[
            full(D_MODEL), _const_spec((3, 1, LANES)), _const_spec((D_MODEL, IN_WIDTH)),
            _const_spec((N_HEADS, LANES)),
            blk(WINDOW), blk(WINDOW), blk(N_MEM), blk(N_MEM),
        ],
        out_specs=[full(IN_WIDTH), full(Q_WIDTH), full(MEM_WIDTH), blk(WINDOW), blk(WINDOW)],
        out_shape=[
            jax.ShapeDtypeStruct((db, IN_WIDTH), F32),
            jax.ShapeDtypeStruct((db, Q_WIDTH), F32),
            jax.ShapeDtypeStruct((db, MEM_WIDTH), F32),
            jax.ShapeDtypeStruct((db, WINDOW, KV_WIDTH), F32),
            jax.ShapeDtypeStruct((db, WINDOW, KV_WIDTH), F32),
        ],
        compiler_params=pltpu.CompilerParams(dimension_semantics=("arbitrary",), vmem_limit_bytes=VMEM_LIMIT),
        name="sample_attn",
    )(x, rope, win, sink_gk, ck, cv, cmk, cmv)


def _sample_tail_kernel(x_ref, z_ref, oa_ref, oc_ref, st_ref, wa_ref, wb_ref, wc_ref, wo_ref, wmix_ref,
                        pscale_ref, g1_ref, b1_ref, wr_ref, br_ref, wg_ref, wu_ref, wd_ref, g2_ref, b2_ref,
                        y_ref, npool_ref, h_sc, comb_sc, acc_sc):
    e = pl.program_id(0)

    @pl.when(e == 0)
    def _():
        u = z_ref[:, U0:U0 + POOL_WIDTH]
        npool_ref[:, 0:(POOL_STATE - 1) * POOL_WIDTH] = st_ref[:, POOL_WIDTH:POOL_STATE * POOL_WIDTH]
        npool_ref[:, (POOL_STATE - 1) * POOL_WIDTH:POOL_STATE * POOL_WIDTH] = u
        obs = []
        for g, w in enumerate(POOL_WINDOWS):
            sl = slice(g * POOL_GROUP_DIM, (g + 1) * POOL_GROUP_DIM)
            cur = u[:, sl]
            ws = cur
            for jj in range(1, w):
                base = (POOL_STATE - jj) * POOL_WIDTH
                ws = ws + st_ref[:, base + g * POOL_GROUP_DIM:base + (g + 1) * POOL_GROUP_DIM]
            cnt = float(min(PAST_LEN + 1, w))
            pooled = ws / cnt - cur
            obs.append(_dot(pooled.astype(BF16), wmix_ref[g]) * pscale_ref[:, sl])
        ob = jnp.concatenate(obs, axis=1)
        h = _merge_ln1(x_ref[...], oa_ref[...].astype(BF16), ob, oc_ref[...], z_ref[:, GZ0:GZ0 + 3 * D_MODEL],
                       wa_ref, wb_ref, wc_ref, wo_ref, g1_ref[...], b1_ref[...])
        h_sc[...] = h
        logits = _dot(h.astype(BF16), wr_ref[...]) + br_ref[...]
        hot1, hot2, w1, w2 = _route(logits)
        comb_sc[...] = jnp.where(hot1, w1, 0.0) + jnp.where(hot2, w2, 0.0)
        acc_sc[...] = jnp.zeros_like(acc_sc)

    out = _expert_mlp(h_sc[...].astype(BF16), wg_ref[...].astype(BF16), wu_ref[...].astype(BF16),
                      wd_ref[...].astype(BF16))
    lane = lax.broadcasted_iota(jnp.int32, comb_sc.shape, 1)
    ce = jnp.sum(jnp.where(lane == e, comb_sc[...], 0.0), axis=1, keepdims=True)
    acc_sc[...] += ce * out

    @pl.when(e == pl.num_programs(0) - 1)
    def _():
        y_ref[...] = _layer_norm(ALPHA * h_sc[...] + acc_sc[...], g2_ref[...], b2_ref[...])


def _sample_tail(x, z, oa, oc, state, wa, wb, wc, wo, wmix, pscale, g1, b1, wr, br, wg, wu, wd, g2, b2):
    db = x.shape[0]
    full = lambda w: pl.BlockSpec((db, w), lambda e: (0, 0))
    vec = lambda w: pl.BlockSpec((1, w), lambda e: (0, 0))
    return pl.pallas_call(
        _sample_tail_kernel,
        grid=(N_EXPERTS,),
        in_specs=[
            full(D_MODEL), full(IN_WIDTH), full(Q_WIDTH), full(MEM_WIDTH), full(POOL_STATE * POOL_WIDTH),
            _const_spec((Q_WIDTH, D_MODEL)), _const_spec((POOL_WIDTH, D_MODEL)),
            _const_spec((MEM_WIDTH, D_MODEL)), _const_spec((D_MODEL, D_MODEL)),
            _const_spec((len(POOL_WINDOWS), POOL_GROUP_DIM, POOL_GROUP_DIM)),
            vec(POOL_WIDTH), vec(D_MODEL), vec(D_MODEL),
            _const_spec((D_MODEL, LANES)), vec(LANES),
            pl.BlockSpec((None, D_MODEL, D_EXPERT), lambda e: (e, 0, 0)),
            pl.BlockSpec((None, D_MODEL, D_EXPERT), lambda e: (e, 0, 0)),
            pl.BlockSpec((None, D_EXPERT, D_MODEL), lambda e: (e, 0, 0)),
            vec(D_MODEL), vec(D_MODEL),
        ],
        out_specs=[full(D_MODEL), full(POOL_STATE * POOL_WIDTH)],
        out_shape=[jax.ShapeDtypeStruct((db, D_MODEL), F32),
                   jax.ShapeDtypeStruct((db, POOL_STATE * POOL_WIDTH), F32)],
        scratch_shapes=[pltpu.VMEM((db, D_MODEL), F32), pltpu.VMEM((db, LANES), F32),
                        pltpu.VMEM((db, D_MODEL), F32)],
        compiler_params=pltpu.CompilerParams(dimension_semantics=("arbitrary",), vmem_limit_bytes=VMEM_LIMIT),
        name="sample_tail",
    )(x, z, oa, oc, state, wa, wb, wc, wo, wmix, pscale, g1, b1, wr, br, wg, wu, wd, g2, b2)


def _rope_tables(pos):
    half = ROPE_DIM // 2
    inv = jnp.power(ROPE_THETA, -jnp.arange(half, dtype=F32) * (2.0 / ROPE_DIM))
    ang = pos.astype(F32)[:, None] * inv[None, :]
    lane = np.arange(LANES)
    off = lane % HEAD_DIM
    cos = jnp.cos(ang)[:, lane % half]
    sin = jnp.sin(ang)[:, lane % half]
    c = jnp.where(off[None, :] < ROPE_DIM, cos, 1.0)
    s1 = jnp.where((off[None, :] >= half) & (off[None, :] < ROPE_DIM), sin, 0.0)
    s2 = jnp.where(off[None, :] < half, -sin, 0.0)
    return jnp.stack([c, s1, s2]).astype(F32)


def _q_heads_group_major(w, axis):
    if axis == 1:
        n = w.shape[0]
        return w.reshape(n, N_KV, GROUP, HEAD_DIM).transpose(0, 2, 1, 3).reshape(n, Q_WIDTH)
    n = w.shape[1]
    return w.reshape(N_KV, GROUP, HEAD_DIM, n).transpose(1, 0, 2, 3).reshape(Q_WIDTH, n)


def kernel(x_prompt, x_sample, cache_win_k, cache_win_v, state_pool, cache_mem_k, cache_mem_v, mem_prompt, w_in, sinks, w_pool_mix, pool_scale, w_mem_k, w_mem_v, w_branch_a, w_branch_b, w_branch_c, w_out, ln1_g, ln1_b, w_group, b_group, w_router, b_router, w_gate, w_up, w_down, ln2_g, ln2_b):
    assert w_in.shape[0] == DEPTH == 1
    b, l, _ = x_prompt.shape
    db, ds, _ = x_sample.shape
    assert ds == 1 and l % TM == 0 and db % SB == 0
    assert cache_win_k.shape[2] == WINDOW
    t = b * l

    win = w_in[0]
    scale = HEAD_DIM ** -0.5
    wq = _q_heads_group_major(win[:, Q0:Q0 + Q_WIDTH], 1) * scale
    wcq = win[:, CQ0:CQ0 + MEM_WIDTH] * scale
    win_b = jnp.concatenate([wq, win[:, K0:CQ0], wcq, win[:, GZ0:]], axis=1).astype(BF16)
    wa = _q_heads_group_major(w_branch_a[0], 0).astype(BF16)
    wb = w_branch_b[0].astype(BF16)
    wc = w_branch_c[0].astype(BF16)
    wo = w_out[0].astype(BF16)
    wmix = w_pool_mix[0].astype(BF16)
    pscale = pool_scale[0].reshape(1, POOL_WIDTH)
    g1 = ln1_g[0].reshape(1, D_MODEL)
    b1 = ln1_b[0].reshape(1, D_MODEL)
    g2 = ln2_g[0].reshape(1, D_MODEL)
    b2 = ln2_b[0].reshape(1, D_MODEL)
    wr = jnp.concatenate([w_group[0], w_router[0].reshape(D_MODEL, N_EXPERTS)], axis=1)
    wr = jnp.pad(wr, ((0, 0), (0, LANES - wr.shape[1]))).astype(BF16)
    br = jnp.pad(jnp.concatenate([b_group[0], b_router[0].reshape(N_EXPERTS)]), (0, LANES - N_EXPERT_GROUPS - N_EXPERTS))
    br = br.reshape(1, LANES).astype(F32)
    wg = w_gate[0]
    wu = w_up[0]
    wd = w_down[0]
    sink = sinks[0].astype(F32)
    sink_gk = jnp.broadcast_to(sink.reshape(N_KV, GROUP).T.reshape(N_HEADS, 1), (N_HEADS, LANES))

    mk, mv = _mem_project(mem_prompt, w_mem_k[0].astype(BF16), w_mem_v[0].astype(BF16))
    rope_p = _rope_tables(jnp.arange(l, dtype=jnp.int32))
    h, xs, route, counts, nk_p, nv_p, npool_p = _front(
        x_prompt, rope_p, sink, win_b, wa, wb, wc, wo, wmix, pscale, mk, mv, g1, b1, wr, br)
    piece_src, piece_dst, tile_expert, tile_active = _piece_tables(
        counts.reshape(-1, SUBLANES, LANES)[:, 0, :N_EXPERTS].astype(jnp.int32))
    ys = _grouped_gemm(piece_src, piece_dst, tile_expert, tile_active, xs, wg, wu, wd)
    y_p = _combine(ys, h.reshape(t, D_MODEL), route.reshape(t, LANES), g2, b2).reshape(b, l, D_MODEL)

    rope_s = _rope_tables(jnp.full((1,), PAST_LEN, jnp.int32))
    xs = x_sample.reshape(db, D_MODEL)
    ck = cache_win_k[0].reshape(db, WINDOW, KV_WIDTH)
    cv = cache_win_v[0].reshape(db, WINDOW, KV_WIDTH)
    cmk = cache_mem_k[0].reshape(db, N_MEM, MEM_WIDTH)
    cmv = cache_mem_v[0].reshape(db, N_MEM, MEM_WIDTH)
    z_s, oa_s, oc_s, nk_s, nv_s = _sample_attn(xs, rope_s, win_b, sink_gk, ck, cv, cmk, cmv)
    state = state_pool[0].reshape(db, POOL_STATE * POOL_WIDTH)
    y_s, npool_s = _sample_tail(xs, z_s, oa_s, oc_s, state, wa, wb, wc, wo, wmix, pscale, g1, b1, wr, br,
                                wg, wu, wd, g2, b2)

    kv5 = lambda a, n, w: a.reshape(1, n, w, N_KV, HEAD_DIM)
    return (y_p, y_s.reshape(db, 1, D_MODEL),
            kv5(nk_p, b, QB), kv5(nv_p, b, QB),
            npool_p[:, 2 * SUBLANES - POOL_STATE:, :][None],
            kv5(mk, b, N_MEM), kv5(mv, b, N_MEM),
            kv5(nk_s, db, WINDOW), kv5(nv_s, db, WINDOW),
            npool_s.reshape(1, db, POOL_STATE, POOL_WIDTH))
```

```python
import functools

import jax
import jax.numpy as jnp
import numpy as np
from jax import lax
from jax.experimental import pallas as pl
from jax.experimental.pallas import tpu as pltpu

D_MODEL = 1024
N_HEADS = 16
HEAD_DIM = 64
N_KV = 4
GROUP = N_HEADS // N_KV
WINDOW = 128
ROPE_THETA = 500000.0
ROPE_DIM = HEAD_DIM // 4
Q_WIDTH = N_HEADS * HEAD_DIM
KV_WIDTH = N_KV * HEAD_DIM
POOL_WINDOWS = (2, 4, 8, 16)
POOL_WIDTH = D_MODEL // 2
POOL_GROUP_DIM = POOL_WIDTH // len(POOL_WINDOWS)
POOL_STATE = max(POOL_WINDOWS) - 1
N_MEM = 256
MEM_HEADS = 4
MEM_WIDTH = MEM_HEADS * HEAD_DIM
N_EXPERT_GROUPS = 4
EXPERTS_PER_GROUP = 4
N_EXPERTS = N_EXPERT_GROUPS * EXPERTS_PER_GROUP
D_EXPERT = 512
PAST_LEN = 16384
DEPTH = 1
ALPHA = (2.0 * DEPTH) ** 0.25
LN_EPS = 1e-5

Q0 = 0
K0 = Q0 + Q_WIDTH
V0 = K0 + KV_WIDTH
U0 = V0 + KV_WIDTH
CQ0 = U0 + POOL_WIDTH
GZ0 = CQ0 + MEM_WIDTH
IN_WIDTH = GZ0 + 3 * D_MODEL

LANES = 128
SUBLANES = 8
VMEM_LIMIT = 56 * 1024 * 1024

TM = 256
QB = WINDOW
PIECE = 16
PIECES_PER_TILE = 16
TG = PIECE * PIECES_PER_TILE
MAX_CHUNK_PIECES = 2 * TM // PIECE + N_EXPERTS - 1
N_SPARE = 2 * PIECES_PER_TILE + 1


def _chunk_rows(n_chunks):
    spare = -(-N_SPARE // n_chunks)
    return -(-(MAX_CHUNK_PIECES + spare) * PIECE // LANES) * LANES
SB = 8
IN_CHUNK = 768
SOFTMAX_ROWS = 64
assert N_MEM == 2 * QB and MEM_HEADS == N_KV and MEM_WIDTH == KV_WIDTH and TM <= GROUP * QB

BF16 = jnp.bfloat16
F32 = jnp.float32
NEG_INF = float("-inf")


def _const_spec(shape):
    nd = len(shape)
    return pl.BlockSpec(shape, lambda *_: (0,) * nd, pipeline_mode=pl.Buffered(1))


def _layer_norm(x, g, b):
    mu = jnp.mean(x, axis=-1, keepdims=True)
    xc = x - mu
    var = jnp.mean(xc * xc, axis=-1, keepdims=True)
    return xc * lax.rsqrt(var + LN_EPS) * g + b


def _dot(a, b):
    return jnp.dot(a, b, preferred_element_type=F32)


def _dot_nt(a, b):
    return lax.dot_general(a, b, (((1,), (1,)), ((), ())), preferred_element_type=F32)


def _lane_block_mask(shape, block, width=HEAD_DIM):
    lane = lax.broadcasted_iota(jnp.int32, shape, len(shape) - 1)
    return (lane >= block * width) & (lane < (block + 1) * width)


def _rope(x, c, s1, s2):
    half = ROPE_DIM // 2
    return x * c + pltpu.roll(x, half, 1) * s1 + pltpu.roll(x, LANES - half, 1) * s2


def _route(logits):
    rows = logits.shape[0]
    lane = lax.broadcasted_iota(jnp.int32, (rows, LANES), 1)
    lanef = lane.astype(F32)
    big = float(LANES)
    is_g = lane < N_EXPERT_GROUPS
    glog = jnp.where(is_g, logits, NEG_INF)
    gmax = jnp.max(glog, axis=1, keepdims=True)
    gsum = jnp.sum(jnp.where(is_g, jnp.exp(glog - gmax), 0.0), axis=1, keepdims=True)
    gp = 1.0 / gsum
    gidx = jnp.min(jnp.where(glog == gmax, lanef, big), axis=1, keepdims=True).astype(jnp.int32)
    lo = N_EXPERT_GROUPS + gidx * EXPERTS_PER_GROUP
    in_grp = (lane >= lo) & (lane < lo + EXPERTS_PER_GROUP)
    el = jnp.where(in_grp, logits, NEG_INF)
    v1 = jnp.max(el, axis=1, keepdims=True)
    i1 = jnp.min(jnp.where(el == v1, lanef, big), axis=1, keepdims=True).astype(jnp.int32)
    el2 = jnp.where(lane == i1, NEG_INF, el)
    v2 = jnp.max(el2, axis=1, keepdims=True)
    i2 = jnp.min(jnp.where(el2 == v2, lanef, big), axis=1, keepdims=True).astype(jnp.int32)
    e21 = jnp.exp(v2 - v1)
    inv = 1.0 / (1.0 + e21)
    w1 = inv * gp
    w2 = e21 * inv * gp
    e1 = i1 - N_EXPERT_GROUPS
    e2 = i2 - N_EXPERT_GROUPS
    return lane == e1, lane == e2, w1, w2


def _local_sort(hot1, hot2, w1, w2, hb, cap):
    rows = hb.shape[0]
    lane = lax.broadcasted_iota(jnp.int32, (rows, LANES), 1)
    onehot = jnp.where(hot1 | hot2, 1.0, 0.0)
    counts = jnp.sum(onehot, axis=0, keepdims=True)
    before = (lax.broadcasted_iota(jnp.int32, (rows, rows), 1)
              < lax.broadcasted_iota(jnp.int32, (rows, rows), 0)).astype(BF16)
    rank = _dot(before, onehot.astype(BF16))
    run = (((counts.astype(jnp.int32) + (PIECE - 1)) // PIECE) * PIECE).astype(F32)
    lower = (lax.broadcasted_iota(jnp.int32, (LANES, LANES), 0)
             < lax.broadcasted_iota(jnp.int32, (LANES, LANES), 1)).astype(BF16)
    start = _dot(jnp.broadcast_to(run, (SUBLANES, LANES)).astype(BF16), lower)[0:1]
    slot = start + rank
    s1 = jnp.sum(jnp.where(hot1, slot, 0.0), axis=1, keepdims=True)
    s2 = jnp.sum(jnp.where(hot2, slot, 0.0), axis=1, keepdims=True)
    route = jnp.where(lane == 0, w1, jnp.where(lane == 1, w2, jnp.where(lane == 2, s1, jnp.where(lane == 3, s2, 0.0))))
    route_t = route.T
    srow = lax.broadcasted_iota(jnp.int32, (cap, rows), 0).astype(F32)
    perm = jnp.where((srow == route_t[2:3, :]) | (srow == route_t[3:4, :]), 1.0, 0.0).astype(BF16)
    return _dot(perm, hb).astype(BF16), route, counts


def _sigmoid(x):
    return 0.5 * jnp.tanh(0.5 * x) + 0.5


def _merge_ln1(x, oa, ob, oc, gz, wa_ref, wb_ref, wc_ref, wo_ref, g1, b1):
    ya = _dot(oa, wa_ref[...])
    yb = _dot(ob.astype(BF16), wb_ref[...])
    yc = _dot(oc.astype(BF16), wc_ref[...])
    m = (_sigmoid(gz[:, 0:D_MODEL]) * ya
         + _sigmoid(gz[:, D_MODEL:2 * D_MODEL]) * yb
         + _sigmoid(gz[:, 2 * D_MODEL:3 * D_MODEL]) * yc)
    hpre = ALPHA * x + _dot(m.astype(BF16), wo_ref[...])
    return _layer_norm(hpre, g1, b1)


def _mem_kernel(mem_ref, wk_ref, wv_ref, mk_ref, mv_ref):
    m = mem_ref[...].astype(BF16)
    mk_ref[...] = _dot(m, wk_ref[...])
    mv_ref[...] = _dot(m, wv_ref[...])


def _mem_project(mem, wk, wv):
    b = mem.shape[0]
    out = jax.ShapeDtypeStruct((b, N_MEM, MEM_WIDTH), F32)
    return pl.pallas_call(
        _mem_kernel,
        grid=(b,),
        in_specs=[pl.BlockSpec((None, N_MEM, D_MODEL), lambda i: (i, 0, 0)),
                  _const_spec((D_MODEL, MEM_WIDTH)), _const_spec((D_MODEL, MEM_WIDTH))],
        out_specs=[pl.BlockSpec((None, N_MEM, MEM_WIDTH), lambda i: (i, 0, 0))] * 2,
        out_shape=[out, out],
        name="mem_project",
    )(mem, wk, wv)


def _front_kernel(sinks_ref, x_ref, rope_ref, win_ref, wa_ref, wb_ref, wc_ref, wo_ref, wmix_ref,
                  pscale_ref, mk_ref, mv_ref, g1_ref, b1_ref, wr_ref, br_ref,
                  h_ref, xs_ref, route_ref, counts_ref, nk_ref, nv_ref, npool_ref,
                  z_ref, qb_ref, kext_ref, vext_ref, uext_ref, oa_ref, ob_ref, bias_ref, s_ref, p_ref, vblk_ref):
    i = pl.program_id(1)
    x = x_ref[...]
    xb = x.astype(BF16)

    @pl.when(i == 0)
    def _():
        kext_ref[0:QB, :] = jnp.zeros((QB, KV_WIDTH), BF16)
        vext_ref[0:QB, :] = jnp.zeros((QB, KV_WIDTH), BF16)
        uext_ref[0:2 * SUBLANES, :] = jnp.zeros((2 * SUBLANES, POOL_WIDTH), F32)

    @pl.when(i > 0)
    def _():
        kext_ref[0:QB, :] = kext_ref[TM:TM + QB, :]
        vext_ref[0:QB, :] = vext_ref[TM:TM + QB, :]
        uext_ref[0:2 * SUBLANES, :] = uext_ref[TM:TM + 2 * SUBLANES, :]

    for c0 in range(0, IN_WIDTH, IN_CHUNK):
        z_ref[:, c0:c0 + IN_CHUNK] = _dot(xb, win_ref[:, c0:c0 + IN_CHUNK])

    c = rope_ref[0]
    s1 = rope_ref[1]
    s2 = rope_ref[2]
    for j in range(Q_WIDTH // LANES):
        sl = slice(Q0 + j * LANES, Q0 + (j + 1) * LANES)
        qb_ref[:, j * LANES:(j + 1) * LANES] = _rope(z_ref[:, sl], c, s1, s2).astype(BF16)
    for j in range(KV_WIDTH // LANES):
        sl = slice(K0 + j * LANES, K0 + (j + 1) * LANES)
        kr = _rope(z_ref[:, sl], c, s1, s2)
        z_ref[:, sl] = kr
        kext_ref[QB:QB + TM, j * LANES:(j + 1) * LANES] = kr.astype(BF16)
    vext_ref[QB:QB + TM, :] = z_ref[:, V0:V0 + KV_WIDTH].astype(BF16)
    nk_ref[...] = z_ref[TM - QB:TM, K0:K0 + KV_WIDTH]
    nv_ref[...] = z_ref[TM - QB:TM, V0:V0 + KV_WIDTH]

    rowq = lax.broadcasted_iota(jnp.int32, (QB, 2 * QB), 0)
    colk = lax.broadcasted_iota(jnp.int32, (QB, 2 * QB), 1)
    band = (colk >= rowq) & (colk <= rowq + WINDOW)
    bias_ref[1] = jnp.where(band, 0.0, NEG_INF)
    bias_ref[0] = jnp.where(band & ((colk >= QB) | (i > 0)), 0.0, NEG_INF)
    for sb in range(TM // QB):
        k2 = kext_ref[sb * QB:(sb + 2) * QB, :]
        v2 = vext_ref[sb * QB:(sb + 2) * QB, :]
        qs = jnp.concatenate(
            [qb_ref[sb * QB:(sb + 1) * QB, g * KV_WIDTH:(g + 1) * KV_WIDTH] for g in range(GROUP)], axis=0)
        for kv in range(N_KV):
            kmask = _lane_block_mask((2 * QB, KV_WIDTH), kv)
            s_ref[...] = _dot_nt(qs, jnp.where(kmask, k2, jnp.zeros_like(k2)))
            vblk_ref[kv * 2 * QB:(kv + 1) * 2 * QB, :] = jnp.where(kmask, v2, jnp.zeros_like(v2))
            for c0 in range(0, GROUP * QB, SOFTMAX_ROWS):
                rq = c0 % QB
                sink = sinks_ref[kv * GROUP + c0 // QB]
                s = s_ref[c0:c0 + SOFTMAX_ROWS, :] + bias_ref[min(sb, 1), rq:rq + SOFTMAX_ROWS, :]
                m = jnp.maximum(jnp.max(s, axis=1, keepdims=True), sink)
                p = jnp.exp(s - m)
                den = jnp.sum(p, axis=1, keepdims=True) + jnp.exp(sink - m)
                p_ref[c0:c0 + SOFTMAX_ROWS, kv * 2 * QB:(kv + 1) * 2 * QB] = (p * (1.0 / den)).astype(BF16)
        o = _dot(p_ref[...], vblk_ref[...])
        for g in range(GROUP):
            oa_ref[sb * QB:(sb + 1) * QB, g * KV_WIDTH:(g + 1) * KV_WIDTH] = o[g * QB:(g + 1) * QB].astype(BF16)

    hist = 2 * SUBLANES
    uext_ref[hist:hist + TM, :] = z_ref[:, U0:U0 + POOL_WIDTH]
    npool_ref[...] = uext_ref[TM:TM + hist, :]
    pos = i * TM + lax.broadcasted_iota(jnp.int32, (TM, 1), 0)
    for g, w in enumerate(POOL_WINDOWS):
        sl = slice(g * POOL_GROUP_DIM, (g + 1) * POOL_GROUP_DIM)
        cur = uext_ref[hist:hist + TM, sl]
        ws = cur
        for j in range(1, w):
            ws = ws + uext_ref[hist - j:hist - j + TM, sl]
        cnt = jnp.minimum(pos + 1, w).astype(F32)
        pooled = ws / cnt - cur
        ob_ref[:, sl] = _dot(pooled.astype(BF16), wmix_ref[g]) * pscale_ref[:, sl]

    cq = z_ref[:, CQ0:CQ0 + MEM_WIDTH].astype(BF16)
    mk = mk_ref[...].astype(BF16)
    mv = mv_ref[...].astype(BF16)
    for hh in range(MEM_HEADS):
        hmask = _lane_block_mask((N_MEM, MEM_WIDTH), hh)
        s_ref[0:TM, :] = _dot_nt(cq, jnp.where(hmask, mk, jnp.zeros_like(mk)))
        vblk_ref[hh * N_MEM:(hh + 1) * N_MEM, :] = jnp.where(hmask, mv, jnp.zeros_like(mv))
        for c0 in range(0, TM, SOFTMAX_ROWS):
            s = s_ref[c0:c0 + SOFTMAX_ROWS, :]
            p = jnp.exp(s - jnp.max(s, axis=1, keepdims=True))
            den = jnp.sum(p, axis=1, keepdims=True)
            p_ref[c0:c0 + SOFTMAX_ROWS, hh * N_MEM:(hh + 1) * N_MEM] = (p * (1.0 / den)).astype(BF16)
    oc = _dot(p_ref[0:TM, :], vblk_ref[...])

    h = _merge_ln1(x, oa_ref[...], ob_ref[...], oc, z_ref[:, GZ0:GZ0 + 3 * D_MODEL],
                   wa_ref, wb_ref, wc_ref, wo_ref, g1_ref[...], b1_ref[...])
    h_ref[...] = h
    hb = h.astype(BF16)
    logits = _dot(hb, wr_ref[...]) + br_ref[...]
    xs, route, counts = _local_sort(*_route(logits), hb, xs_ref.shape[0])
    xs_ref[...] = xs
    route_ref[...] = route
    counts_ref[...] = jnp.broadcast_to(counts, (SUBLANES, LANES))


def _front(x, rope, sinks, win, wa, wb, wc, wo, wmix, pscale, mk, mv, g1, b1, wr, br):
    b, l, _ = x.shape
    nt = l // TM
    cap = _chunk_rows(b * nt)
    hist = 2 * SUBLANES
    tile = lambda w: pl.BlockSpec((None, TM, w), lambda bi, ti: (bi, ti, 0))
    per_b = lambda r, w: pl.BlockSpec((None, r, w), lambda bi, ti: (bi, 0, 0))
    return pl.pallas_call(
        _front_kernel,
        grid=(b, nt),
        in_specs=[
            pl.BlockSpec(memory_space=pltpu.SMEM),
            tile(D_MODEL),
            pl.BlockSpec((3, TM, LANES), lambda bi, ti: (0, ti, 0)),
            _const_spec((D_MODEL, IN_WIDTH)),
            _const_spec((Q_WIDTH, D_MODEL)), _const_spec((POOL_WIDTH, D_MODEL)),
            _const_spec((MEM_WIDTH, D_MODEL)), _const_spec((D_MODEL, D_MODEL)),
            _const_spec((len(POOL_WINDOWS), POOL_GROUP_DIM, POOL_GROUP_DIM)),
            _const_spec((1, POOL_WIDTH)),
            per_b(N_MEM, MEM_WIDTH), per_b(N_MEM, MEM_WIDTH),
            _const_spec((1, D_MODEL)), _const_spec((1, D_MODEL)),
            _const_spec((D_MODEL, LANES)), _const_spec((1, LANES)),
        ],
        out_specs=[
            tile(D_MODEL),
            pl.BlockSpec((cap, D_MODEL), lambda bi, ti: (bi * nt + ti, 0)),
            tile(LANES),
            pl.BlockSpec((None, None, SUBLANES, LANES), lambda bi, ti: (bi, ti, 0, 0)),
            per_b(QB, KV_WIDTH), per_b(QB, KV_WIDTH), per_b(hist, POOL_WIDTH),
        ],
        out_shape=[
            jax.ShapeDtypeStruct((b, l, D_MODEL), F32),
            jax.ShapeDtypeStruct((b * nt * cap, D_MODEL), BF16),
            jax.ShapeDtypeStruct((b, l, LANES), F32),
            jax.ShapeDtypeStruct((b, nt, SUBLANES, LANES), F32),
            jax.ShapeDtypeStruct((b, QB, KV_WIDTH), F32),
            jax.ShapeDtypeStruct((b, QB, KV_WIDTH), F32),
            jax.ShapeDtypeStruct((b, hist, POOL_WIDTH), F32),
        ],
        scratch_shapes=[
            pltpu.VMEM((TM, IN_WIDTH), F32),
            pltpu.VMEM((TM, Q_WIDTH), BF16),
            pltpu.VMEM((QB + TM, KV_WIDTH), BF16),
            pltpu.VMEM((QB + TM, KV_WIDTH), BF16),
            pltpu.VMEM((hist + TM, POOL_WIDTH), F32),
            pltpu.VMEM((TM, Q_WIDTH), BF16),
            pltpu.VMEM((TM, POOL_WIDTH), F32),
            pltpu.VMEM((2, QB, 2 * QB), F32),
            pltpu.VMEM((GROUP * QB, 2 * QB), F32),
            pltpu.VMEM((GROUP * QB, N_KV * 2 * QB), BF16),
            pltpu.VMEM((N_KV * 2 * QB, KV_WIDTH), BF16),
        ],
        compiler_params=pltpu.CompilerParams(
            dimension_semantics=("arbitrary", "arbitrary"), vmem_limit_bytes=VMEM_LIMIT),
        name="front_prompt",
    )(sinks, x, rope, win, wa, wb, wc, wo, wmix, pscale, mk, mv, g1, b1, wr, br)


def _expert_mlp(xb, wg, wu, wd):
    a = _dot(xb, wg)
    hid = (a * jax.nn.sigmoid(a)) * _dot(xb, wu)
    return _dot(hid.astype(BF16), wd)


def _gemm_kernel(src_ref, dst_ref, te_ref, act_ref, xs_ref, wg_ref, wu_ref, wd_ref, ys_ref,
                 xbuf, obuf, wgb, wub, wdb, prime, sem_in, sem_out):
    i = pl.program_id(0)
    n = pl.num_programs(0)
    slot = i % 2

    def start_in(tile, slot):
        for j in range(PIECES_PER_TILE):
            row0 = pl.multiple_of(src_ref[tile * PIECES_PER_TILE + j], PIECE)
            pltpu.make_async_copy(xs_ref.at[pl.ds(row0, PIECE)], xbuf.at[slot, pl.ds(j * PIECE, PIECE)],
                                  sem_in.at[slot]).start()

    def start_out(tile, slot):
        for j in range(PIECES_PER_TILE):
            row0 = pl.multiple_of(dst_ref[tile * PIECES_PER_TILE + j], PIECE)
            pltpu.make_async_copy(obuf.at[slot, pl.ds(j * PIECE, PIECE)], ys_ref.at[pl.ds(row0, PIECE)],
                                  sem_out.at[slot]).start()

    def wait_in(slot):
        for j in range(PIECES_PER_TILE):
            pltpu.make_async_copy(xs_ref.at[pl.ds(0, PIECE)], xbuf.at[slot, pl.ds(j * PIECE, PIECE)],
                                  sem_in.at[slot]).wait()

    def wait_out(slot):
        for j in range(PIECES_PER_TILE):
            pltpu.make_async_copy(obuf.at[slot, pl.ds(j * PIECE, PIECE)], ys_ref.at[pl.ds(0, PIECE)],
                                  sem_out.at[slot]).wait()

    active = act_ref[i] > 0
    prefetched = (i == 0) | (act_ref[jnp.maximum(i - 1, 0)] > 0)
    out_pending = (i < 2) | (act_ref[jnp.maximum(i - 2, 0)] > 0)

    @pl.when(i == 0)
    def _():
        prime[0] = jnp.zeros((PIECE, D_MODEL), BF16)
        for s in range(2):
            for j in range(PIECES_PER_TILE):
                pltpu.make_async_copy(prime.at[0], prime.at[1 + s * PIECES_PER_TILE + j], sem_out.at[s]).start()
        start_in(0, 0)

    @pl.when(active)
    def _():
        start_in(i + 1, 1 - slot)

        @pl.when((i == 0) | (te_ref[i] != te_ref[jnp.maximum(i - 1, 0)]))
        def _():
            wgb[...] = wg_ref[...].astype(BF16)
            wub[...] = wu_ref[...].astype(BF16)
            wdb[...] = wd_ref[...].astype(BF16)

        wait_in(slot)
        xb = xbuf[slot]
        a = _dot(xb, wgb[...])
        hid = (a * jax.nn.sigmoid(a)) * _dot(xb, wub[...])
        wait_out(slot)
        obuf[slot] = _dot(hid.astype(BF16), wdb[...]).astype(BF16)
        start_out(i, slot)

    @pl.when(jnp.logical_not(active))
    def _():
        @pl.when(prefetched)
        def _():
            wait_in(slot)

        @pl.when(out_pending)
        def _():
            wait_out(slot)

    @pl.when(i == n - 1)
    def _():
        @pl.when(active)
        def _():
            wait_in(1 - slot)
            wait_out(slot)

        @pl.when((i >= 1) & (act_ref[jnp.maximum(i - 1, 0)] > 0))
        def _():
            wait_out(1 - slot)


def _grouped_gemm(piece_src, piece_dst, tile_expert, tile_active, xs, wg, wu, wd):
    n_tiles = tile_expert.shape[0]
    assert n_tiles >= 2
    wspec = lambda r, c: pl.BlockSpec((None, r, c), lambda i, src, dst, te, act: (te[i], 0, 0))
    return pl.pallas_call(
        _gemm_kernel,
        grid_spec=pltpu.PrefetchScalarGridSpec(
            num_scalar_prefetch=4,
            grid=(n_tiles,),
            in_specs=[pl.BlockSpec(memory_space=pl.ANY),
                      wspec(D_MODEL, D_EXPERT), wspec(D_MODEL, D_EXPERT), wspec(D_EXPERT, D_MODEL)],
            out_specs=pl.BlockSpec(memory_space=pl.ANY),
            scratch_shapes=[pltpu.VMEM((2, TG, D_MODEL), BF16), pltpu.VMEM((2, TG, D_MODEL), BF16),
                            pltpu.VMEM((D_MODEL, D_EXPERT), BF16), pltpu.VMEM((D_MODEL, D_EXPERT), BF16),
                            pltpu.VMEM((D_EXPERT, D_MODEL), BF16),
                            pltpu.VMEM((1 + 2 * PIECES_PER_TILE, PIECE, D_MODEL), BF16),
                            pltpu.SemaphoreType.DMA((2,)), pltpu.SemaphoreType.DMA((2,))],
        ),
        out_shape=jax.ShapeDtypeStruct(xs.shape, xs.dtype),
        input_output_aliases={4: 0},
        compiler_params=pltpu.CompilerParams(dimension_semantics=("arbitrary",), vmem_limit_bytes=VMEM_LIMIT),
        name="moe_grouped_gemm",
    )(piece_src, piece_dst, tile_expert, tile_active, xs, wg, wu, wd)


def _combine_kernel(ys_ref, h_ref, route_ref, g2_ref, b2_ref, y_ref):
    route = route_ref[...]
    slot = lax.broadcasted_iota(jnp.int32, (TM, ys_ref.shape[0]), 1).astype(F32)
    sel = jnp.concatenate([jnp.where(slot == route[:, 2:3], 1.0, 0.0).astype(BF16),
                           jnp.where(slot == route[:, 3:4], 1.0, 0.0).astype(BF16)], axis=0)
    picked = _dot(sel, ys_ref[...])
    f = route[:, 0:1] * picked[0:TM] + route[:, 1:2] * picked[TM:2 * TM]
    y_ref[...] = _layer_norm(ALPHA * h_ref[...] + f, g2_ref[...], b2_ref[...])


def _combine(ys, h, route, g2, b2):
    t = h.shape[0]
    return pl.pallas_call(
        _combine_kernel,
        grid=(t // TM,),
        in_specs=[
            pl.BlockSpec((ys.shape[0] // (t // TM), D_MODEL), lambda i: (i, 0)),
            pl.BlockSpec((TM, D_MODEL), lambda i: (i, 0)),
            pl.BlockSpec((TM, LANES), lambda i: (i, 0)),
            pl.BlockSpec((1, D_MODEL), lambda i: (0, 0)),
            pl.BlockSpec((1, D_MODEL), lambda i: (0, 0)),
        ],
        out_specs=pl.BlockSpec((TM, D_MODEL), lambda i: (i, 0)),
        out_shape=jax.ShapeDtypeStruct((t, D_MODEL), F32),
        compiler_params=pltpu.CompilerParams(dimension_semantics=("arbitrary",)),
        name="moe_combine",
    )(ys, h, route, g2, b2)


def _select(table, idx):
    hot = idx[:, None] == jnp.arange(table.shape[0], dtype=jnp.int32)[None, :]
    return jnp.sum(jnp.where(hot[:, :, None], table[None, :, :], 0), axis=1)


def _piece_tables(counts):
    n_chunks = counts.shape[0]
    n_tiles = -(-(n_chunks * MAX_CHUNK_PIECES + N_EXPERTS * (PIECES_PER_TILE - 1)) // PIECES_PER_TILE)
    npc = (counts + (PIECE - 1)) // PIECE
    first = (jnp.cumsum(npc, axis=1) - npc).T
    npc_t = npc.T
    cum = jnp.cumsum(npc_t, axis=1)
    per_expert = cum[:, -1]
    tiles_e = (per_expert + (PIECES_PER_TILE - 1)) // PIECES_PER_TILE
    tile_end = jnp.cumsum(tiles_e)
    tile_idx = jnp.arange(n_tiles, dtype=jnp.int32)
    expert_of = lambda i: jnp.minimum(jnp.sum(i[:, None] >= tile_end[None, :], axis=1), N_EXPERTS - 1).astype(jnp.int32)
    active = tile_idx < tile_end[-1]
    tile_expert = jnp.where(active, expert_of(tile_idx), expert_of(tile_end[-1:] - 1))
    meta = jnp.stack([tile_end - tiles_e, per_expert], axis=1)
    meta_t = _select(meta, tile_expert)
    k = (tile_idx - meta_t[:, 0])[:, None] * PIECES_PER_TILE + jnp.arange(PIECES_PER_TILE, dtype=jnp.int32)[None, :]
    valid = active[:, None] & (k < meta_t[:, 1:2])
    cum_t = _select(cum, tile_expert)
    chunk = jnp.minimum(jnp.sum(k[:, :, None] >= cum_t[:, None, :], axis=2), n_chunks - 1).astype(jnp.int32)
    at_chunk = chunk[:, :, None] == jnp.arange(n_chunks, dtype=jnp.int32)[None, None, :]
    pick = lambda tab: jnp.sum(jnp.where(at_chunk, _select(tab, tile_expert)[:, None, :], 0), axis=2)
    piece = pick(first) + k - pick(cum - npc_t)
    cap = _chunk_rows(n_chunks)
    rows = chunk * cap + piece * PIECE
    d = (tile_idx % 2)[:, None] * PIECES_PER_TILE + jnp.arange(PIECES_PER_TILE, dtype=jnp.int32)[None, :]
    spare_row = lambda d: (d % n_chunks) * cap + (MAX_CHUNK_PIECES + d // n_chunks) * PIECE
    spare = spare_row(d)
    zero_piece = spare_row(N_SPARE - 1)
    extra = jnp.full((PIECES_PER_TILE,), zero_piece, jnp.int32)
    src = jnp.concatenate([jnp.where(valid, rows, zero_piece).astype(jnp.int32).reshape(-1), extra])
    dst = jnp.concatenate([jnp.where(valid, rows, spare).astype(jnp.int32).reshape(-1), extra])
    return src, dst, tile_expert.astype(jnp.int32), active.astype(jnp.int32)


def _sample_attn_kernel(x_ref, rope_ref, win_ref, sink_ref, ck_ref, cv_ref, cmk_ref, cmv_ref,
                        z_ref, oa_ref, oc_ref, nk_ref, nv_ref, knew_t, vnew_t):
    j = pl.program_id(0)
    db = x_ref.shape[0]

    @pl.when(j == 0)
    def _():
        xb = x_ref[...].astype(BF16)
        for c0 in range(0, IN_WIDTH, IN_CHUNK):
            z_ref[:, c0:c0 + IN_CHUNK] = _dot(xb, win_ref[:, c0:c0 + IN_CHUNK])
        c = rope_ref[0]
        s1 = rope_ref[1]
        s2 = rope_ref[2]
        for jj in range((Q_WIDTH + KV_WIDTH) // LANES):
            sl = slice(jj * LANES, (jj + 1) * LANES)
            z_ref[:, sl] = _rope(z_ref[:, sl], c, s1, s2)
        knew_t[...] = z_ref[:, K0:K0 + KV_WIDTH].T
        vnew_t[...] = z_ref[:, V0:V0 + KV_WIDTH].T

    r0 = pl.multiple_of(j * SB, SB)
    zq = z_ref[pl.ds(r0, SB), Q0:Q0 + Q_WIDTH]
    zk = z_ref[pl.ds(r0, SB), K0:K0 + KV_WIDTH]
    zv = z_ref[pl.ds(r0, SB), V0:V0 + KV_WIDTH]
    zc = z_ref[pl.ds(r0, SB), CQ0:CQ0 + MEM_WIDTH]
    sink = sink_ref[:, 0:1]
    row_kv = lax.broadcasted_iota(jnp.int32, (N_HEADS, KV_WIDTH), 0) & (N_KV - 1)
    lane_kv = lax.broadcasted_iota(jnp.int32, (N_HEADS, KV_WIDTH), 1) // HEAD_DIM
    own = row_kv == lane_kv
    row_c = lax.broadcasted_iota(jnp.int32, (N_HEADS, MEM_WIDTH), 0)
    lane_c = lax.broadcasted_iota(jnp.int32, (N_HEADS, MEM_WIDTH), 1) // HEAD_DIM
    own_c = row_c == lane_c
    last_pos = lax.broadcasted_iota(jnp.int32, (KV_WIDTH, WINDOW), 1) == WINDOW - 1
    seq_lane = lax.broadcasted_iota(jnp.int32, (KV_WIDTH, db), 1)

    oa_rows = [[] for _ in range(GROUP)]
    oc_rows = []
    for b in range(SB):
        kc = ck_ref[b]
        vc = cv_ref[b]
        knew = zk[b:b + 1, :]
        vnew = zv[b:b + 1, :]
        qblk = jnp.concatenate(
            [jnp.broadcast_to(zq[b:b + 1, g * KV_WIDTH:(g + 1) * KV_WIDTH], (N_KV, KV_WIDTH)) for g in range(GROUP)],
            axis=0)
        qblk = jnp.where(own, qblk, 0.0).astype(BF16)
        s = _dot(qblk, kc.astype(BF16))
        s_new = jnp.sum(qblk.astype(F32) * knew.astype(BF16).astype(F32), axis=1, keepdims=True)
        m = jnp.maximum(jnp.maximum(jnp.max(s, axis=1, keepdims=True), s_new), sink)
        p = jnp.exp(s - m)
        p_new = jnp.exp(s_new - m)
        inv = 1.0 / (jnp.sum(p, axis=1, keepdims=True) + p_new + jnp.exp(sink - m))
        o = (_dot_nt((p * inv).astype(BF16), vc.astype(BF16))
             + (p_new * inv).astype(BF16).astype(F32) * vnew.astype(BF16).astype(F32))
        o = jnp.where(own, o, 0.0)
        for g in range(GROUP):
            oa_rows[g].append(jnp.sum(o[g * N_KV:(g + 1) * N_KV], axis=0, keepdims=True))
        here = seq_lane == j * SB + b
        knew_col = jnp.sum(jnp.where(here, knew_t[...], 0.0), axis=1, keepdims=True)
        vnew_col = jnp.sum(jnp.where(here, vnew_t[...], 0.0), axis=1, keepdims=True)
        nk_ref[b] = jnp.where(last_pos, knew_col, pltpu.roll(kc, WINDOW - 1, 1))
        nv_ref[b] = jnp.where(last_pos, vnew_col, pltpu.roll(vc, WINDOW - 1, 1))
        cblk = jnp.where(own_c, jnp.broadcast_to(zc[b:b + 1, :], (N_HEADS, MEM_WIDTH)), 0.0)
        sc = _dot(cblk.astype(BF16), cmk_ref[b].astype(BF16))
        mc = jnp.max(sc, axis=1, keepdims=True)
        pc = jnp.exp(sc - mc)
        pc = pc * (1.0 / jnp.sum(pc, axis=1, keepdims=True))
        ocb = jnp.where(own_c, _dot_nt(pc.astype(BF16), cmv_ref[b].astype(BF16)), 0.0)
        oc_rows.append(jnp.sum(ocb, axis=0, keepdims=True))
    for g in range(GROUP):
        oa_ref[pl.ds(r0, SB), g * KV_WIDTH:(g + 1) * KV_WIDTH] = jnp.concatenate(oa_rows[g], axis=0)
    oc_ref[pl.ds(r0, SB), :] = jnp.concatenate(oc_rows, axis=0)


def _sample_attn(x, rope, win, sink_gk, ck, cv, cmk, cmv):
    db = x.shape[0]
    blk = lambda r: pl.BlockSpec((SB, KV_WIDTH, r), lambda j: (j, 0, 0))
    full = lambda w: pl.BlockSpec((db, w), lambda j: (0, 0))
    return pl.pallas_call(
        _sample_attn_kernel,
        grid=(db // SB,),
        in_specs=[
            full(D_MODEL), _const_spec((3, 1, LANES)), _const_spec((D_MODEL, IN_WIDTH)),
            _const_spec((N_HEADS, LANES)),
            blk(WINDOW), blk(WINDOW), blk(N_MEM), blk(N_MEM),
        ],
        out_specs=[full(IN_WIDTH), full(Q_WIDTH), full(MEM_WIDTH), blk(WINDOW), blk(WINDOW)],
        out_shape=[
            jax.ShapeDtypeStruct((db, IN_WIDTH), F32),
            jax.ShapeDtypeStruct((db, Q_WIDTH), F32),
            jax.ShapeDtypeStruct((db, MEM_WIDTH), F32),
            jax.ShapeDtypeStruct((db, KV_WIDTH, WINDOW), F32),
            jax.ShapeDtypeStruct((db, KV_WIDTH, WINDOW), F32),
        ],
        scratch_shapes=[pltpu.VMEM((KV_WIDTH, db), F32), pltpu.VMEM((KV_WIDTH, db), F32)],
        compiler_params=pltpu.CompilerParams(dimension_semantics=("arbitrary",), vmem_limit_bytes=VMEM_LIMIT),
        name="sample_attn",
    )(x, rope, win, sink_gk, ck, cv, cmk, cmv)


def _sample_tail_kernel(x_ref, z_ref, oa_ref, oc_ref, st_ref, wa_ref, wb_ref, wc_ref, wo_ref, wmix_ref,
                        pscale_ref, g1_ref, b1_ref, wr_ref, br_ref, wg_ref, wu_ref, wd_ref, g2_ref, b2_ref,
                        y_ref, npool_ref, h_sc, comb_sc, acc_sc):
    e = pl.program_id(0)

    @pl.when(e == 0)
    def _():
        u = z_ref[:, U0:U0 + POOL_WIDTH]
        npool_ref[0:POOL_STATE - 1] = st_ref[1:POOL_STATE]
        npool_ref[POOL_STATE - 1] = u
        obs = []
        for g, w in enumerate(POOL_WINDOWS):
            sl = slice(g * POOL_GROUP_DIM, (g + 1) * POOL_GROUP_DIM)
            cur = u[:, sl]
            ws = cur
            for jj in range(1, w):
                ws = ws + st_ref[POOL_STATE - jj, :, sl]
            cnt = float(min(PAST_LEN + 1, w))
            pooled = ws / cnt - cur
            obs.append(_dot(pooled.astype(BF16), wmix_ref[g]) * pscale_ref[:, sl])
        ob = jnp.concatenate(obs, axis=1)
        h = _merge_ln1(x_ref[...], oa_ref[...].astype(BF16), ob, oc_ref[...], z_ref[:, GZ0:GZ0 + 3 * D_MODEL],
                       wa_ref, wb_ref, wc_ref, wo_ref, g1_ref[...], b1_ref[...])
        h_sc[...] = h
        logits = _dot(h.astype(BF16), wr_ref[...]) + br_ref[...]
        hot1, hot2, w1, w2 = _route(logits)
        comb_sc[...] = jnp.where(hot1, w1, 0.0) + jnp.where(hot2, w2, 0.0)
        acc_sc[...] = jnp.zeros_like(acc_sc)

    out = _expert_mlp(h_sc[...].astype(BF16), wg_ref[...].astype(BF16), wu_ref[...].astype(BF16),
                      wd_ref[...].astype(BF16))
    lane = lax.broadcasted_iota(jnp.int32, comb_sc.shape, 1)
    ce = jnp.sum(jnp.where(lane == e, comb_sc[...], 0.0), axis=1, keepdims=True)
    acc_sc[...] += ce * out

    @pl.when(e == pl.num_programs(0) - 1)
    def _():
        y_ref[...] = _layer_norm(ALPHA * h_sc[...] + acc_sc[...], g2_ref[...], b2_ref[...])


def _sample_tail(x, z, oa, oc, state, wa, wb, wc, wo, wmix, pscale, g1, b1, wr, br, wg, wu, wd, g2, b2):
    db = x.shape[0]
    full = lambda w: pl.BlockSpec((db, w), lambda e: (0, 0))
    vec = lambda w: pl.BlockSpec((1, w), lambda e: (0, 0))
    hist = pl.BlockSpec((POOL_STATE, db, POOL_WIDTH), lambda e: (0, 0, 0))
    return pl.pallas_call(
        _sample_tail_kernel,
        grid=(N_EXPERTS,),
        in_specs=[
            full(D_MODEL), full(IN_WIDTH), full(Q_WIDTH), full(MEM_WIDTH), hist,
            _const_spec((Q_WIDTH, D_MODEL)), _const_spec((POOL_WIDTH, D_MODEL)),
            _const_spec((MEM_WIDTH, D_MODEL)), _const_spec((D_MODEL, D_MODEL)),
            _const_spec((len(POOL_WINDOWS), POOL_GROUP_DIM, POOL_GROUP_DIM)),
            vec(POOL_WIDTH), vec(D_MODEL), vec(D_MODEL),
            _const_spec((D_MODEL, LANES)), vec(LANES),
            pl.BlockSpec((None, D_MODEL, D_EXPERT), lambda e: (e, 0, 0)),
            pl.BlockSpec((None, D_MODEL, D_EXPERT), lambda e: (e, 0, 0)),
            pl.BlockSpec((None, D_EXPERT, D_MODEL), lambda e: (e, 0, 0)),
            vec(D_MODEL), vec(D_MODEL),
        ],
        out_specs=[full(D_MODEL), hist],
        out_shape=[jax.ShapeDtypeStruct((db, D_MODEL), F32),
                   jax.ShapeDtypeStruct((POOL_STATE, db, POOL_WIDTH), F32)],
        scratch_shapes=[pltpu.VMEM((db, D_MODEL), F32), pltpu.VMEM((db, LANES), F32),
                        pltpu.VMEM((db, D_MODEL), F32)],
        compiler_params=pltpu.CompilerParams(dimension_semantics=("arbitrary",), vmem_limit_bytes=VMEM_LIMIT),
        name="sample_tail",
    )(x, z, oa, oc, state, wa, wb, wc, wo, wmix, pscale, g1, b1, wr, br, wg, wu, wd, g2, b2)


def _rope_tables(pos):
    half = ROPE_DIM // 2
    inv = jnp.power(ROPE_THETA, -jnp.arange(half, dtype=F32) * (2.0 / ROPE_DIM))
    ang = pos.astype(F32)[:, None] * inv[None, :]
    lane = np.arange(LANES)
    off = lane % HEAD_DIM
    cos = jnp.cos(ang)[:, lane % half]
    sin = jnp.sin(ang)[:, lane % half]
    c = jnp.where(off[None, :] < ROPE_DIM, cos, 1.0)
    s1 = jnp.where((off[None, :] >= half) & (off[None, :] < ROPE_DIM), sin, 0.0)
    s2 = jnp.where(off[None, :] < half, -sin, 0.0)
    return jnp.stack([c, s1, s2]).astype(F32)


def _q_heads_group_major(w, axis):
    if axis == 1:
        n = w.shape[0]
        return w.reshape(n, N_KV, GROUP, HEAD_DIM).transpose(0, 2, 1, 3).reshape(n, Q_WIDTH)
    n = w.shape[1]
    return w.reshape(N_KV, GROUP, HEAD_DIM, n).transpose(1, 0, 2, 3).reshape(Q_WIDTH, n)


def kernel(x_prompt, x_sample, cache_win_k, cache_win_v, state_pool, cache_mem_k, cache_mem_v, mem_prompt, w_in, sinks, w_pool_mix, pool_scale, w_mem_k, w_mem_v, w_branch_a, w_branch_b, w_branch_c, w_out, ln1_g, ln1_b, w_group, b_group, w_router, b_router, w_gate, w_up, w_down, ln2_g, ln2_b):
    assert w_in.shape[0] == DEPTH == 1
    b, l, _ = x_prompt.shape
    db, ds, _ = x_sample.shape
    assert ds == 1 and l % TM == 0 and db % SB == 0
    assert cache_win_k.shape[2] == WINDOW
    t = b * l

    win = w_in[0]
    scale = HEAD_DIM ** -0.5
    wq = _q_heads_group_major(win[:, Q0:Q0 + Q_WIDTH], 1) * scale
    wcq = win[:, CQ0:CQ0 + MEM_WIDTH] * scale
    win_b = jnp.concatenate([wq, win[:, K0:CQ0], wcq, win[:, GZ0:]], axis=1).astype(BF16)
    wa = _q_heads_group_major(w_branch_a[0], 0).astype(BF16)
    wb = w_branch_b[0].astype(BF16)
    wc = w_branch_c[0].astype(BF16)
    wo = w_out[0].astype(BF16)
    wmix = w_pool_mix[0].astype(BF16)
    pscale = pool_scale[0].reshape(1, POOL_WIDTH)
    g1 = ln1_g[0].reshape(1, D_MODEL)
    b1 = ln1_b[0].reshape(1, D_MODEL)
    g2 = ln2_g[0].reshape(1, D_MODEL)
    b2 = ln2_b[0].reshape(1, D_MODEL)
    wr = jnp.concatenate([w_group[0], w_router[0].reshape(D_MODEL, N_EXPERTS)], axis=1)
    wr = jnp.pad(wr, ((0, 0), (0, LANES - wr.shape[1]))).astype(BF16)
    br = jnp.pad(jnp.concatenate([b_group[0], b_router[0].reshape(N_EXPERTS)]), (0, LANES - N_EXPERT_GROUPS - N_EXPERTS))
    br = br.reshape(1, LANES).astype(F32)
    wg = w_gate[0]
    wu = w_up[0]
    wd = w_down[0]
    sink = sinks[0].astype(F32)
    sink_gk = jnp.broadcast_to(sink.reshape(N_KV, GROUP).T.reshape(N_HEADS, 1), (N_HEADS, LANES))

    mk, mv = _mem_project(mem_prompt, w_mem_k[0].astype(BF16), w_mem_v[0].astype(BF16))
    rope_p = _rope_tables(jnp.arange(l, dtype=jnp.int32))
    h, xs, route, counts, nk_p, nv_p, npool_p = _front(
        x_prompt, rope_p, sink, win_b, wa, wb, wc, wo, wmix, pscale, mk, mv, g1, b1, wr, br)
    piece_src, piece_dst, tile_expert, tile_active = _piece_tables(
        counts.reshape(-1, SUBLANES, LANES)[:, 0, :N_EXPERTS].astype(jnp.int32))
    ys = _grouped_gemm(piece_src, piece_dst, tile_expert, tile_active, xs, wg, wu, wd)
    y_p = _combine(ys, h.reshape(t, D_MODEL), route.reshape(t, LANES), g2, b2).reshape(b, l, D_MODEL)

    rope_s = _rope_tables(jnp.full((1,), PAST_LEN, jnp.int32))
    xs = x_sample.reshape(db, D_MODEL)
    feat_major = lambda c: jnp.transpose(c[0], (0, 2, 3, 1)).reshape(db, KV_WIDTH, c.shape[2])
    from_feat_major = lambda a: jnp.transpose(a.reshape(db, N_KV, HEAD_DIM, a.shape[2]), (0, 3, 1, 2))[None]
    z_s, oa_s, oc_s, nk_s, nv_s = _sample_attn(xs, rope_s, win_b, sink_gk, feat_major(cache_win_k),
                                               feat_major(cache_win_v), feat_major(cache_mem_k), feat_major(cache_mem_v))
    state = jnp.transpose(state_pool[0], (1, 0, 2))
    y_s, npool_s = _sample_tail(xs, z_s, oa_s, oc_s, state, wa, wb, wc, wo, wmix, pscale, g1, b1, wr, br,
                                wg, wu, wd, g2, b2)

    kv5 = lambda a, n, w: a.reshape(1, n, w, N_KV, HEAD_DIM)
    return (y_p, y_s.reshape(db, 1, D_MODEL),
            kv5(nk_p, b, QB), kv5(nv_p, b, QB),
            npool_p[:, 2 * SUBLANES - POOL_STATE:, :][None],
            kv5(mk, b, N_MEM), kv5(mv, b, N_MEM),
            from_feat_major(nk_s), from_feat_major(nv_s),
            jnp.transpose(npool_s, (1, 0, 2))[None])
```

```python
import functools

import jax
import jax.numpy as jnp
import numpy as np
from jax import lax
from jax.experimental import pallas as pl
from jax.experimental.pallas import tpu as pltpu

D_MODEL = 1024
N_HEADS = 16
HEAD_DIM = 64
N_KV = 4
GROUP = N_HEADS // N_KV
WINDOW = 128
ROPE_THETA = 500000.0
ROPE_DIM = HEAD_DIM // 4
Q_WIDTH = N_HEADS * HEAD_DIM
KV_WIDTH = N_KV * HEAD_DIM
POOL_WINDOWS = (2, 4, 8, 16)
POOL_WIDTH = D_MODEL // 2
POOL_GROUP_DIM = POOL_WIDTH // len(POOL_WINDOWS)
POOL_STATE = max(POOL_WINDOWS) - 1
N_MEM = 256
MEM_HEADS = 4
MEM_WIDTH = MEM_HEADS * HEAD_DIM
N_EXPERT_GROUPS = 4
EXPERTS_PER_GROUP = 4
N_EXPERTS = N_EXPERT_GROUPS * EXPERTS_PER_GROUP
D_EXPERT = 512
PAST_LEN = 16384
DEPTH = 1
ALPHA = (2.0 * DEPTH) ** 0.25
LN_EPS = 1e-5

Q0 = 0
K0 = Q0 + Q_WIDTH
V0 = K0 + KV_WIDTH
U0 = V0 + KV_WIDTH
CQ0 = U0 + POOL_WIDTH
GZ0 = CQ0 + MEM_WIDTH
IN_WIDTH = GZ0 + 3 * D_MODEL

LANES = 128
SUBLANES = 8
VMEM_LIMIT = 56 * 1024 * 1024

TM = 256
QB = WINDOW
PIECE = 16
PIECES_PER_TILE = 16
TG = PIECE * PIECES_PER_TILE
MAX_CHUNK_PIECES = 2 * TM // PIECE + N_EXPERTS - 1
N_SPARE = 2 * PIECES_PER_TILE + 1


def _chunk_rows(n_chunks):
    spare = -(-N_SPARE // n_chunks)
    return -(-(MAX_CHUNK_PIECES + spare) * PIECE // LANES) * LANES
SB = 8
IN_CHUNK = 768
SOFTMAX_ROWS = 64
assert N_MEM == 2 * QB and MEM_HEADS == N_KV and MEM_WIDTH == KV_WIDTH and TM <= GROUP * QB

BF16 = jnp.bfloat16
F32 = jnp.float32
NEG_INF = float("-inf")


def _const_spec(shape):
    nd = len(shape)
    return pl.BlockSpec(shape, lambda *_: (0,) * nd, pipeline_mode=pl.Buffered(1))


def _layer_norm(x, g, b):
    mu = jnp.mean(x, axis=-1, keepdims=True)
    xc = x - mu
    var = jnp.mean(xc * xc, axis=-1, keepdims=True)
    return xc * lax.rsqrt(var + LN_EPS) * g + b


def _dot(a, b):
    return jnp.dot(a, b, preferred_element_type=F32)


def _dot_nt(a, b):
    return lax.dot_general(a, b, (((1,), (1,)), ((), ())), preferred_element_type=F32)


def _lane_block_mask(shape, block, width=HEAD_DIM):
    lane = lax.broadcasted_iota(jnp.int32, shape, len(shape) - 1)
    return (lane >= block * width) & (lane < (block + 1) * width)


def _rope(x, c, s1, s2):
    half = ROPE_DIM // 2
    return x * c + pltpu.roll(x, half, 1) * s1 + pltpu.roll(x, LANES - half, 1) * s2


def _route(logits):
    rows = logits.shape[0]
    lane = lax.broadcasted_iota(jnp.int32, (rows, LANES), 1)
    lanef = lane.astype(F32)
    big = float(LANES)
    is_g = lane < N_EXPERT_GROUPS
    glog = jnp.where(is_g, logits, NEG_INF)
    gmax = jnp.max(glog, axis=1, keepdims=True)
    gsum = jnp.sum(jnp.where(is_g, jnp.exp(glog - gmax), 0.0), axis=1, keepdims=True)
    gp = 1.0 / gsum
    gidx = jnp.min(jnp.where(glog == gmax, lanef, big), axis=1, keepdims=True).astype(jnp.int32)
    lo = N_EXPERT_GROUPS + gidx * EXPERTS_PER_GROUP
    in_grp = (lane >= lo) & (lane < lo + EXPERTS_PER_GROUP)
    el = jnp.where(in_grp, logits, NEG_INF)
    v1 = jnp.max(el, axis=1, keepdims=True)
    i1 = jnp.min(jnp.where(el == v1, lanef, big), axis=1, keepdims=True).astype(jnp.int32)
    el2 = jnp.where(lane == i1, NEG_INF, el)
    v2 = jnp.max(el2, axis=1, keepdims=True)
    i2 = jnp.min(jnp.where(el2 == v2, lanef, big), axis=1, keepdims=True).astype(jnp.int32)
    e21 = jnp.exp(v2 - v1)
    inv = 1.0 / (1.0 + e21)
    w1 = inv * gp
    w2 = e21 * inv * gp
    e1 = i1 - N_EXPERT_GROUPS
    e2 = i2 - N_EXPERT_GROUPS
    return lane == e1, lane == e2, w1, w2


def _local_sort(hot1, hot2, w1, w2, hb, cap):
    rows = hb.shape[0]
    lane = lax.broadcasted_iota(jnp.int32, (rows, LANES), 1)
    onehot = jnp.where(hot1 | hot2, 1.0, 0.0)
    counts = jnp.sum(onehot, axis=0, keepdims=True)
    before = (lax.broadcasted_iota(jnp.int32, (rows, rows), 1)
              < lax.broadcasted_iota(jnp.int32, (rows, rows), 0)).astype(BF16)
    rank = _dot(before, onehot.astype(BF16))
    run = (((counts.astype(jnp.int32) + (PIECE - 1)) // PIECE) * PIECE).astype(F32)
    lower = (lax.broadcasted_iota(jnp.int32, (LANES, LANES), 0)
             < lax.broadcasted_iota(jnp.int32, (LANES, LANES), 1)).astype(BF16)
    start = _dot(jnp.broadcast_to(run, (SUBLANES, LANES)).astype(BF16), lower)[0:1]
    slot = start + rank
    s1 = jnp.sum(jnp.where(hot1, slot, 0.0), axis=1, keepdims=True)
    s2 = jnp.sum(jnp.where(hot2, slot, 0.0), axis=1, keepdims=True)
    route = jnp.where(lane == 0, w1, jnp.where(lane == 1, w2, jnp.where(lane == 2, s1, jnp.where(lane == 3, s2, 0.0))))
    route_t = route.T
    srow = lax.broadcasted_iota(jnp.int32, (cap, rows), 0).astype(F32)
    perm = jnp.where((srow == route_t[2:3, :]) | (srow == route_t[3:4, :]), 1.0, 0.0).astype(BF16)
    return _dot(perm, hb).astype(BF16), route, counts


def _sigmoid(x):
    return 0.5 * jnp.tanh(0.5 * x) + 0.5


def _merge_ln1(x, oa, ob, oc, gz, wa_ref, wb_ref, wc_ref, wo_ref, g1, b1):
    ya = _dot(oa, wa_ref[...])
    yb = _dot(ob.astype(BF16), wb_ref[...])
    yc = _dot(oc.astype(BF16), wc_ref[...])
    m = (_sigmoid(gz[:, 0:D_MODEL]) * ya
         + _sigmoid(gz[:, D_MODEL:2 * D_MODEL]) * yb
         + _sigmoid(gz[:, 2 * D_MODEL:3 * D_MODEL]) * yc)
    hpre = ALPHA * x + _dot(m.astype(BF16), wo_ref[...])
    return _layer_norm(hpre, g1, b1)


def _mem_kernel(mem_ref, wk_ref, wv_ref, mk_ref, mv_ref):
    m = mem_ref[...].astype(BF16)
    mk_ref[...] = _dot(m, wk_ref[...])
    mv_ref[...] = _dot(m, wv_ref[...])


def _mem_project(mem, wk, wv):
    b = mem.shape[0]
    out = jax.ShapeDtypeStruct((b, N_MEM, MEM_WIDTH), F32)
    return pl.pallas_call(
        _mem_kernel,
        grid=(b,),
        in_specs=[pl.BlockSpec((None, N_MEM, D_MODEL), lambda i: (i, 0, 0)),
                  _const_spec((D_MODEL, MEM_WIDTH)), _const_spec((D_MODEL, MEM_WIDTH))],
        out_specs=[pl.BlockSpec((None, N_MEM, MEM_WIDTH), lambda i: (i, 0, 0))] * 2,
        out_shape=[out, out],
        name="mem_project",
    )(mem, wk, wv)


def _front_kernel(sinks_ref, x_ref, rope_ref, win_ref, wa_ref, wb_ref, wc_ref, wo_ref, wmix_ref,
                  pscale_ref, mk_ref, mv_ref, g1_ref, b1_ref, wr_ref, br_ref,
                  h_ref, xs_ref, route_ref, counts_ref, nk_ref, nv_ref, npool_ref,
                  z_ref, qb_ref, kext_ref, vext_ref, uext_ref, oa_ref, ob_ref, bias_ref, s_ref, p_ref, vblk_ref):
    i = pl.program_id(1)
    x = x_ref[...]
    xb = x.astype(BF16)
    hist = 2 * SUBLANES

    @pl.when(i == 0)
    def _():
        kext_ref[0:QB, :] = jnp.zeros((QB, KV_WIDTH), BF16)
        vext_ref[0:QB, :] = jnp.zeros((QB, KV_WIDTH), BF16)
        uext_ref[0:hist, :] = jnp.zeros((hist, POOL_WIDTH), F32)

    @pl.when(i > 0)
    def _():
        kext_ref[0:QB, :] = kext_ref[TM:TM + QB, :]
        vext_ref[0:QB, :] = vext_ref[TM:TM + QB, :]
        uext_ref[0:hist, :] = uext_ref[TM:TM + hist, :]

    for c0 in range(0, IN_WIDTH, IN_CHUNK):
        z_ref[:, c0:c0 + IN_CHUNK] = _dot(xb, win_ref[:, c0:c0 + IN_CHUNK])

    c = rope_ref[0]
    s1 = rope_ref[1]
    s2 = rope_ref[2]
    for j in range(Q_WIDTH // LANES):
        sl = slice(Q0 + j * LANES, Q0 + (j + 1) * LANES)
        qb_ref[:, j * LANES:(j + 1) * LANES] = _rope(z_ref[:, sl], c, s1, s2).astype(BF16)
    for j in range(KV_WIDTH // LANES):
        sl = slice(K0 + j * LANES, K0 + (j + 1) * LANES)
        kr = _rope(z_ref[:, sl], c, s1, s2)
        z_ref[:, sl] = kr
        kext_ref[QB:QB + TM, j * LANES:(j + 1) * LANES] = kr.astype(BF16)
    vext_ref[QB:QB + TM, :] = z_ref[:, V0:V0 + KV_WIDTH].astype(BF16)
    uext_ref[hist:hist + TM, :] = z_ref[:, U0:U0 + POOL_WIDTH]
    nk_ref[...] = z_ref[TM - QB:TM, K0:K0 + KV_WIDTH]
    nv_ref[...] = z_ref[TM - QB:TM, V0:V0 + KV_WIDTH]

    rowq = lax.broadcasted_iota(jnp.int32, (QB, 2 * QB), 0)
    colk = lax.broadcasted_iota(jnp.int32, (QB, 2 * QB), 1)
    band = (colk >= rowq) & (colk <= rowq + WINDOW)
    bias_ref[1] = jnp.where(band, 0.0, NEG_INF)
    bias_ref[0] = jnp.where(band & ((colk >= QB) | (i > 0)), 0.0, NEG_INF)
    for sb in range(TM // QB):
        k2 = kext_ref[sb * QB:(sb + 2) * QB, :]
        v2 = vext_ref[sb * QB:(sb + 2) * QB, :]
        qs = jnp.concatenate(
            [qb_ref[sb * QB:(sb + 1) * QB, g * KV_WIDTH:(g + 1) * KV_WIDTH] for g in range(GROUP)], axis=0)
        for kv in range(N_KV):
            kmask = _lane_block_mask((2 * QB, KV_WIDTH), kv)
            s_ref[...] = _dot_nt(qs, jnp.where(kmask, k2, jnp.zeros_like(k2)))
            vblk_ref[kv * 2 * QB:(kv + 1) * 2 * QB, :] = jnp.where(kmask, v2, jnp.zeros_like(v2))
            for c0 in range(0, GROUP * QB, SOFTMAX_ROWS):
                rq = c0 % QB
                sink = sinks_ref[kv * GROUP + c0 // QB]
                s = s_ref[c0:c0 + SOFTMAX_ROWS, :] + bias_ref[min(sb, 1), rq:rq + SOFTMAX_ROWS, :]
                m = jnp.maximum(jnp.max(s, axis=1, keepdims=True), sink)
                p = jnp.exp(s - m)
                den = jnp.sum(p, axis=1, keepdims=True) + jnp.exp(sink - m)
                p_ref[c0:c0 + SOFTMAX_ROWS, kv * 2 * QB:(kv + 1) * 2 * QB] = (p * (1.0 / den)).astype(BF16)
        o = _dot(p_ref[...], vblk_ref[...])
        for g in range(GROUP):
            oa_ref[sb * QB:(sb + 1) * QB, g * KV_WIDTH:(g + 1) * KV_WIDTH] = o[g * QB:(g + 1) * QB].astype(BF16)

    npool_ref[...] = uext_ref[TM:TM + hist, :]
    pos = i * TM + lax.broadcasted_iota(jnp.int32, (TM, 1), 0)
    for g, w in enumerate(POOL_WINDOWS):
        sl = slice(g * POOL_GROUP_DIM, (g + 1) * POOL_GROUP_DIM)
        cur = uext_ref[hist:hist + TM, sl]
        ws = cur
        for j in range(1, w):
            ws = ws + uext_ref[hist - j:hist - j + TM, sl]
        cnt = jnp.minimum(pos + 1, w).astype(F32)
        pooled = ws / cnt - cur
        ob_ref[:, sl] = _dot(pooled.astype(BF16), wmix_ref[g]) * pscale_ref[:, sl]

    cq = z_ref[:, CQ0:CQ0 + MEM_WIDTH].astype(BF16)
    mk = mk_ref[...].astype(BF16)
    mv = mv_ref[...].astype(BF16)
    for hh in range(MEM_HEADS):
        hmask = _lane_block_mask((N_MEM, MEM_WIDTH), hh)
        s_ref[0:TM, :] = _dot_nt(cq, jnp.where(hmask, mk, jnp.zeros_like(mk)))
        vblk_ref[hh * N_MEM:(hh + 1) * N_MEM, :] = jnp.where(hmask, mv, jnp.zeros_like(mv))
        for c0 in range(0, TM, SOFTMAX_ROWS):
            s = s_ref[c0:c0 + SOFTMAX_ROWS, :]
            p = jnp.exp(s - jnp.max(s, axis=1, keepdims=True))
            den = jnp.sum(p, axis=1, keepdims=True)
            p_ref[c0:c0 + SOFTMAX_ROWS, hh * N_MEM:(hh + 1) * N_MEM] = (p * (1.0 / den)).astype(BF16)
    oc = _dot(p_ref[0:TM, :], vblk_ref[...])

    h = _merge_ln1(x, oa_ref[...], ob_ref[...], oc, z_ref[:, GZ0:GZ0 + 3 * D_MODEL],
                   wa_ref, wb_ref, wc_ref, wo_ref, g1_ref[...], b1_ref[...])
    h_ref[...] = h
    hb = h.astype(BF16)
    logits = _dot(hb, wr_ref[...]) + br_ref[...]
    xs, route, counts = _local_sort(*_route(logits), hb, xs_ref.shape[0])
    xs_ref[...] = xs
    route_ref[...] = route
    counts_ref[...] = jnp.broadcast_to(counts, (SUBLANES, LANES))


def _front(x, rope, sinks, win, wa, wb, wc, wo, wmix, pscale, mk, mv, g1, b1, wr, br):
    b, l, _ = x.shape
    nt = l // TM
    cap = _chunk_rows(b * nt)
    hist = 2 * SUBLANES
    tile = lambda w: pl.BlockSpec((None, TM, w), lambda bi, ti: (bi, ti, 0))
    per_b = lambda r, w: pl.BlockSpec((None, r, w), lambda bi, ti: (bi, 0, 0))
    return pl.pallas_call(
        _front_kernel,
        grid=(b, nt),
        in_specs=[
            pl.BlockSpec(memory_space=pltpu.SMEM),
            tile(D_MODEL),
            pl.BlockSpec((3, TM, LANES), lambda bi, ti: (0, ti, 0)),
            _const_spec((D_MODEL, IN_WIDTH)),
            _const_spec((Q_WIDTH, D_MODEL)), _const_spec((POOL_WIDTH, D_MODEL)),
            _const_spec((MEM_WIDTH, D_MODEL)), _const_spec((D_MODEL, D_MODEL)),
            _const_spec((len(POOL_WINDOWS), POOL_GROUP_DIM, POOL_GROUP_DIM)),
            _const_spec((1, POOL_WIDTH)),
            per_b(N_MEM, MEM_WIDTH), per_b(N_MEM, MEM_WIDTH),
            _const_spec((1, D_MODEL)), _const_spec((1, D_MODEL)),
            _const_spec((D_MODEL, LANES)), _const_spec((1, LANES)),
        ],
        out_specs=[
            tile(D_MODEL),
            pl.BlockSpec((cap, D_MODEL), lambda bi, ti: (bi * nt + ti, 0)),
            tile(LANES),
            pl.BlockSpec((None, None, SUBLANES, LANES), lambda bi, ti: (bi, ti, 0, 0)),
            per_b(QB, KV_WIDTH), per_b(QB, KV_WIDTH), per_b(hist, POOL_WIDTH),
        ],
        out_shape=[
            jax.ShapeDtypeStruct((b, l, D_MODEL), F32),
            jax.ShapeDtypeStruct((b * nt * cap, D_MODEL), BF16),
            jax.ShapeDtypeStruct((b, l, LANES), F32),
            jax.ShapeDtypeStruct((b, nt, SUBLANES, LANES), F32),
            jax.ShapeDtypeStruct((b, QB, KV_WIDTH), F32),
            jax.ShapeDtypeStruct((b, QB, KV_WIDTH), F32),
            jax.ShapeDtypeStruct((b, hist, POOL_WIDTH), F32),
        ],
        scratch_shapes=[
            pltpu.VMEM((TM, IN_WIDTH), F32),
            pltpu.VMEM((TM, Q_WIDTH), BF16),
            pltpu.VMEM((QB + TM, KV_WIDTH), BF16),
            pltpu.VMEM((QB + TM, KV_WIDTH), BF16),
            pltpu.VMEM((hist + TM, POOL_WIDTH), F32),
            pltpu.VMEM((TM, Q_WIDTH), BF16),
            pltpu.VMEM((TM, POOL_WIDTH), F32),
            pltpu.VMEM((2, QB, 2 * QB), F32),
            pltpu.VMEM((GROUP * QB, 2 * QB), F32),
            pltpu.VMEM((GROUP * QB, N_KV * 2 * QB), BF16),
            pltpu.VMEM((N_KV * 2 * QB, KV_WIDTH), BF16),
        ],
        compiler_params=pltpu.CompilerParams(
            dimension_semantics=("arbitrary", "arbitrary"), vmem_limit_bytes=VMEM_LIMIT),
        name="front_prompt",
    )(sinks, x, rope, win, wa, wb, wc, wo, wmix, pscale, mk, mv, g1, b1, wr, br)


def _expert_mlp(xb, wg, wu, wd):
    a = _dot(xb, wg)
    hid = (a * jax.nn.sigmoid(a)) * _dot(xb, wu)
    return _dot(hid.astype(BF16), wd)


def _gemm_kernel(src_ref, dst_ref, te_ref, act_ref, xs_ref, wg_ref, wu_ref, wd_ref, ys_ref,
                 xbuf, obuf, wgb, wub, wdb, prime, sem_in, sem_out):
    i = pl.program_id(0)
    n = pl.num_programs(0)
    slot = i % 2

    def start_in(tile, slot):
        for j in range(PIECES_PER_TILE):
            row0 = pl.multiple_of(src_ref[tile * PIECES_PER_TILE + j], PIECE)
            pltpu.make_async_copy(xs_ref.at[pl.ds(row0, PIECE)], xbuf.at[slot, pl.ds(j * PIECE, PIECE)],
                                  sem_in.at[slot]).start()

    def start_out(tile, slot):
        for j in range(PIECES_PER_TILE):
            row0 = pl.multiple_of(dst_ref[tile * PIECES_PER_TILE + j], PIECE)
            pltpu.make_async_copy(obuf.at[slot, pl.ds(j * PIECE, PIECE)], ys_ref.at[pl.ds(row0, PIECE)],
                                  sem_out.at[slot]).start()

    def wait_in(slot):
        for j in range(PIECES_PER_TILE):
            pltpu.make_async_copy(xs_ref.at[pl.ds(0, PIECE)], xbuf.at[slot, pl.ds(j * PIECE, PIECE)],
                                  sem_in.at[slot]).wait()

    def wait_out(slot):
        for j in range(PIECES_PER_TILE):
            pltpu.make_async_copy(obuf.at[slot, pl.ds(j * PIECE, PIECE)], ys_ref.at[pl.ds(0, PIECE)],
                                  sem_out.at[slot]).wait()

    active = act_ref[i] > 0
    prefetched = (i == 0) | (act_ref[jnp.maximum(i - 1, 0)] > 0)
    out_pending = (i < 2) | (act_ref[jnp.maximum(i - 2, 0)] > 0)

    @pl.when(i == 0)
    def _():
        prime[0] = jnp.zeros((PIECE, D_MODEL), BF16)
        for s in range(2):
            for j in range(PIECES_PER_TILE):
                pltpu.make_async_copy(prime.at[0], prime.at[1 + s * PIECES_PER_TILE + j], sem_out.at[s]).start()
        start_in(0, 0)

    @pl.when(active)
    def _():
        start_in(i + 1, 1 - slot)

        @pl.when((i == 0) | (te_ref[i] != te_ref[jnp.maximum(i - 1, 0)]))
        def _():
            wgb[...] = wg_ref[...].astype(BF16)
            wub[...] = wu_ref[...].astype(BF16)
            wdb[...] = wd_ref[...].astype(BF16)

        wait_in(slot)
        wait_out(slot)
        obuf[slot] = _expert_mlp(xbuf[slot], wgb[...], wub[...], wdb[...]).astype(BF16)
        start_out(i, slot)

    @pl.when(jnp.logical_not(active))
    def _():
        @pl.when(prefetched)
        def _():
            wait_in(slot)

        @pl.when(out_pending)
        def _():
            wait_out(slot)

    @pl.when(i == n - 1)
    def _():
        @pl.when(active)
        def _():
            wait_in(1 - slot)
            wait_out(slot)

        @pl.when((i >= 1) & (act_ref[jnp.maximum(i - 1, 0)] > 0))
        def _():
            wait_out(1 - slot)


def _grouped_gemm(piece_src, piece_dst, tile_expert, tile_active, xs, wg, wu, wd):
    n_tiles = tile_expert.shape[0]
    assert n_tiles >= 2
    wspec = lambda r, c: pl.BlockSpec((None, r, c), lambda i, src, dst, te, act: (te[i], 0, 0))
    return pl.pallas_call(
        _gemm_kernel,
        grid_spec=pltpu.PrefetchScalarGridSpec(
            num_scalar_prefetch=4,
            grid=(n_tiles,),
            in_specs=[pl.BlockSpec(memory_space=pl.ANY),
                      wspec(D_MODEL, D_EXPERT), wspec(D_MODEL, D_EXPERT), wspec(D_EXPERT, D_MODEL)],
            out_specs=pl.BlockSpec(memory_space=pl.ANY),
            scratch_shapes=[pltpu.VMEM((2, TG, D_MODEL), BF16), pltpu.VMEM((2, TG, D_MODEL), BF16),
                            pltpu.VMEM((D_MODEL, D_EXPERT), BF16), pltpu.VMEM((D_MODEL, D_EXPERT), BF16),
                            pltpu.VMEM((D_EXPERT, D_MODEL), BF16),
                            pltpu.VMEM((1 + 2 * PIECES_PER_TILE, PIECE, D_MODEL), BF16),
                            pltpu.SemaphoreType.DMA((2,)), pltpu.SemaphoreType.DMA((2,))],
        ),
        out_shape=jax.ShapeDtypeStruct(xs.shape, xs.dtype),
        input_output_aliases={4: 0},
        compiler_params=pltpu.CompilerParams(dimension_semantics=("arbitrary",), vmem_limit_bytes=VMEM_LIMIT),
        name="moe_grouped_gemm",
    )(piece_src, piece_dst, tile_expert, tile_active, xs, wg, wu, wd)


def _combine_kernel(ys_ref, h_ref, route_ref, g2_ref, b2_ref, y_ref):
    route = route_ref[...]
    slot = lax.broadcasted_iota(jnp.int32, (TM, ys_ref.shape[0]), 1).astype(F32)
    sel = jnp.concatenate([jnp.where(slot == route[:, 2:3], 1.0, 0.0).astype(BF16),
                           jnp.where(slot == route[:, 3:4], 1.0, 0.0).astype(BF16)], axis=0)
    picked = _dot(sel, ys_ref[...])
    f = route[:, 0:1] * picked[0:TM] + route[:, 1:2] * picked[TM:2 * TM]
    y_ref[...] = _layer_norm(ALPHA * h_ref[...] + f, g2_ref[...], b2_ref[...])


def _combine(ys, h, route, g2, b2):
    t = h.shape[0]
    return pl.pallas_call(
        _combine_kernel,
        grid=(t // TM,),
        in_specs=[
            pl.BlockSpec((ys.shape[0] // (t // TM), D_MODEL), lambda i: (i, 0)),
            pl.BlockSpec((TM, D_MODEL), lambda i: (i, 0)),
            pl.BlockSpec((TM, LANES), lambda i: (i, 0)),
            pl.BlockSpec((1, D_MODEL), lambda i: (0, 0)),
            pl.BlockSpec((1, D_MODEL), lambda i: (0, 0)),
        ],
        out_specs=pl.BlockSpec((TM, D_MODEL), lambda i: (i, 0)),
        out_shape=jax.ShapeDtypeStruct((t, D_MODEL), F32),
        compiler_params=pltpu.CompilerParams(dimension_semantics=("arbitrary",)),
        name="moe_combine",
    )(ys, h, route, g2, b2)


def _select(table, idx):
    hot = idx[:, None] == jnp.arange(table.shape[0], dtype=jnp.int32)[None, :]
    return jnp.sum(jnp.where(hot[:, :, None], table[None, :, :], 0), axis=1)


def _piece_tables(counts):
    n_chunks = counts.shape[0]
    n_tiles = -(-(n_chunks * MAX_CHUNK_PIECES + N_EXPERTS * (PIECES_PER_TILE - 1)) // PIECES_PER_TILE)
    npc = (counts + (PIECE - 1)) // PIECE
    first = (jnp.cumsum(npc, axis=1) - npc).T
    npc_t = npc.T
    cum = jnp.cumsum(npc_t, axis=1)
    per_expert = cum[:, -1]
    tiles_e = (per_expert + (PIECES_PER_TILE - 1)) // PIECES_PER_TILE
    tile_end = jnp.cumsum(tiles_e)
    tile_idx = jnp.arange(n_tiles, dtype=jnp.int32)
    expert_of = lambda i: jnp.minimum(jnp.sum(i[:, None] >= tile_end[None, :], axis=1), N_EXPERTS - 1).astype(jnp.int32)
    active = tile_idx < tile_end[-1]
    tile_expert = jnp.where(active, expert_of(tile_idx), expert_of(tile_end[-1:] - 1))
    meta = jnp.stack([tile_end - tiles_e, per_expert], axis=1)
    meta_t = _select(meta, tile_expert)
    k = (tile_idx - meta_t[:, 0])[:, None] * PIECES_PER_TILE + jnp.arange(PIECES_PER_TILE, dtype=jnp.int32)[None, :]
    valid = active[:, None] & (k < meta_t[:, 1:2])
    cum_t = _select(cum, tile_expert)
    chunk = jnp.minimum(jnp.sum(k[:, :, None] >= cum_t[:, None, :], axis=2), n_chunks - 1).astype(jnp.int32)
    at_chunk = chunk[:, :, None] == jnp.arange(n_chunks, dtype=jnp.int32)[None, None, :]
    pick = lambda tab: jnp.sum(jnp.where(at_chunk, _select(tab, tile_expert)[:, None, :], 0), axis=2)
    piece = pick(first) + k - pick(cum - npc_t)
    cap = _chunk_rows(n_chunks)
    rows = chunk * cap + piece * PIECE
    d = (tile_idx % 2)[:, None] * PIECES_PER_TILE + jnp.arange(PIECES_PER_TILE, dtype=jnp.int32)[None, :]
    spare_row = lambda d: (d % n_chunks) * cap + (MAX_CHUNK_PIECES + d // n_chunks) * PIECE
    spare = spare_row(d)
    zero_piece = spare_row(N_SPARE - 1)
    extra = jnp.full((PIECES_PER_TILE,), zero_piece, jnp.int32)
    src = jnp.concatenate([jnp.where(valid, rows, zero_piece).astype(jnp.int32).reshape(-1), extra])
    dst = jnp.concatenate([jnp.where(valid, rows, spare).astype(jnp.int32).reshape(-1), extra])
    return src, dst, tile_expert.astype(jnp.int32), active.astype(jnp.int32)


def _sample_attn_kernel(x_ref, rope_ref, win_ref, sink_ref, ck_ref, cv_ref, cmk_ref, cmv_ref,
                        z_ref, oa_ref, oc_ref, nk_ref, nv_ref, knew_t, vnew_t):
    j = pl.program_id(0)
    db = x_ref.shape[0]

    @pl.when(j == 0)
    def _():
        xb = x_ref[...].astype(BF16)
        for c0 in range(0, IN_WIDTH, IN_CHUNK):
            z_ref[:, c0:c0 + IN_CHUNK] = _dot(xb, win_ref[:, c0:c0 + IN_CHUNK])
        c = rope_ref[0]
        s1 = rope_ref[1]
        s2 = rope_ref[2]
        for jj in range((Q_WIDTH + KV_WIDTH) // LANES):
            sl = slice(jj * LANES, (jj + 1) * LANES)
            z_ref[:, sl] = _rope(z_ref[:, sl], c, s1, s2)
        knew_t[...] = z_ref[:, K0:K0 + KV_WIDTH].T
        vnew_t[...] = z_ref[:, V0:V0 + KV_WIDTH].T

    r0 = pl.multiple_of(j * SB, SB)
    zq = z_ref[pl.ds(r0, SB), Q0:Q0 + Q_WIDTH]
    zk = z_ref[pl.ds(r0, SB), K0:K0 + KV_WIDTH]
    zv = z_ref[pl.ds(r0, SB), V0:V0 + KV_WIDTH]
    zc = z_ref[pl.ds(r0, SB), CQ0:CQ0 + MEM_WIDTH]
    sink = sink_ref[:, 0:1]
    row_kv = lax.broadcasted_iota(jnp.int32, (N_HEADS, KV_WIDTH), 0) & (N_KV - 1)
    lane_kv = lax.broadcasted_iota(jnp.int32, (N_HEADS, KV_WIDTH), 1) // HEAD_DIM
    own = row_kv == lane_kv
    row_c = lax.broadcasted_iota(jnp.int32, (N_HEADS, MEM_WIDTH), 0)
    lane_c = lax.broadcasted_iota(jnp.int32, (N_HEADS, MEM_WIDTH), 1) // HEAD_DIM
    own_c = row_c == lane_c
    last_pos = lax.broadcasted_iota(jnp.int32, (KV_WIDTH, WINDOW), 1) == WINDOW - 1
    seq_lane = lax.broadcasted_iota(jnp.int32, (KV_WIDTH, db), 1)

    oa_rows = [[] for _ in range(GROUP)]
    oc_rows = []
    for b in range(SB):
        kc = ck_ref[b]
        vc = cv_ref[b]
        knew = zk[b:b + 1, :]
        vnew = zv[b:b + 1, :]
        qblk = jnp.concatenate(
            [jnp.broadcast_to(zq[b:b + 1, g * KV_WIDTH:(g + 1) * KV_WIDTH], (N_KV, KV_WIDTH)) for g in range(GROUP)],
            axis=0)
        qblk = jnp.where(own, qblk, 0.0).astype(BF16)
        s = _dot(qblk, kc.astype(BF16))
        s_new = jnp.sum(qblk.astype(F32) * knew.astype(BF16).astype(F32), axis=1, keepdims=True)
        m = jnp.maximum(jnp.maximum(jnp.max(s, axis=1, keepdims=True), s_new), sink)
        p = jnp.exp(s - m)
        p_new = jnp.exp(s_new - m)
        inv = 1.0 / (jnp.sum(p, axis=1, keepdims=True) + p_new + jnp.exp(sink - m))
        o = (_dot_nt((p * inv).astype(BF16), vc.astype(BF16))
             + (p_new * inv).astype(BF16).astype(F32) * vnew.astype(BF16).astype(F32))
        o = jnp.where(own, o, 0.0)
        for g in range(GROUP):
            oa_rows[g].append(jnp.sum(o[g * N_KV:(g + 1) * N_KV], axis=0, keepdims=True))
        here = seq_lane == j * SB + b
        knew_col = jnp.sum(jnp.where(here, knew_t[...], 0.0), axis=1, keepdims=True)
        vnew_col = jnp.sum(jnp.where(here, vnew_t[...], 0.0), axis=1, keepdims=True)
        nk_ref[b] = jnp.where(last_pos, knew_col, pltpu.roll(kc, WINDOW - 1, 1))
        nv_ref[b] = jnp.where(last_pos, vnew_col, pltpu.roll(vc, WINDOW - 1, 1))
        cblk = jnp.where(own_c, jnp.broadcast_to(zc[b:b + 1, :], (N_HEADS, MEM_WIDTH)), 0.0)
        sc = _dot(cblk.astype(BF16), cmk_ref[b].astype(BF16))
        mc = jnp.max(sc, axis=1, keepdims=True)
        pc = jnp.exp(sc - mc)
        pc = pc * (1.0 / jnp.sum(pc, axis=1, keepdims=True))
        ocb = jnp.where(own_c, _dot_nt(pc.astype(BF16), cmv_ref[b].astype(BF16)), 0.0)
        oc_rows.append(jnp.sum(ocb, axis=0, keepdims=True))
    for g in range(GROUP):
        oa_ref[pl.ds(r0, SB), g * KV_WIDTH:(g + 1) * KV_WIDTH] = jnp.concatenate(oa_rows[g], axis=0)
    oc_ref[pl.ds(r0, SB), :] = jnp.concatenate(oc_rows, axis=0)


def _sample_attn(x, rope, win, sink_gk, ck, cv, cmk, cmv):
    db = x.shape[0]
    blk = lambda r: pl.BlockSpec((SB, KV_WIDTH, r), lambda j: (j, 0, 0))
    full = lambda w: pl.BlockSpec((db, w), lambda j: (0, 0))
    return pl.pallas_call(
        _sample_attn_kernel,
        grid=(db // SB,),
        in_specs=[
            full(D_MODEL), _const_spec((3, 1, LANES)), _const_spec((D_MODEL, IN_WIDTH)),
            _const_spec((N_HEADS, LANES)),
            blk(WINDOW), blk(WINDOW), blk(N_MEM), blk(N_MEM),
        ],
        out_specs=[full(IN_WIDTH), full(Q_WIDTH), full(MEM_WIDTH), blk(WINDOW), blk(WINDOW)],
        out_shape=[
            jax.ShapeDtypeStruct((db, IN_WIDTH), F32),
            jax.ShapeDtypeStruct((db, Q_WIDTH), F32),
            jax.ShapeDtypeStruct((db, MEM_WIDTH), F32),
            jax.ShapeDtypeStruct((db, KV_WIDTH, WINDOW), F32),
            jax.ShapeDtypeStruct((db, KV_WIDTH, WINDOW), F32),
        ],
        scratch_shapes=[pltpu.VMEM((KV_WIDTH, db), F32), pltpu.VMEM((KV_WIDTH, db), F32)],
        compiler_params=pltpu.CompilerParams(dimension_semantics=("arbitrary",), vmem_limit_bytes=VMEM_LIMIT),
        name="sample_attn",
    )(x, rope, win, sink_gk, ck, cv, cmk, cmv)


def _sample_tail_kernel(x_ref, z_ref, oa_ref, oc_ref, st_ref, wa_ref, wb_ref, wc_ref, wo_ref, wmix_ref,
                        pscale_ref, g1_ref, b1_ref, wr_ref, br_ref, wg_ref, wu_ref, wd_ref, g2_ref, b2_ref,
                        y_ref, npool_ref, h_sc, comb_sc, acc_sc):
    e = pl.program_id(0)

    @pl.when(e == 0)
    def _():
        u = z_ref[:, U0:U0 + POOL_WIDTH]
        npool_ref[0:POOL_STATE - 1] = st_ref[1:POOL_STATE]
        npool_ref[POOL_STATE - 1] = u
        obs = []
        for g, w in enumerate(POOL_WINDOWS):
            sl = slice(g * POOL_GROUP_DIM, (g + 1) * POOL_GROUP_DIM)
            cur = u[:, sl]
            ws = cur
            for jj in range(1, w):
                ws = ws + st_ref[POOL_STATE - jj, :, sl]
            cnt = float(min(PAST_LEN + 1, w))
            pooled = ws / cnt - cur
            obs.append(_dot(pooled.astype(BF16), wmix_ref[g]) * pscale_ref[:, sl])
        ob = jnp.concatenate(obs, axis=1)
        h = _merge_ln1(x_ref[...], oa_ref[...].astype(BF16), ob, oc_ref[...], z_ref[:, GZ0:GZ0 + 3 * D_MODEL],
                       wa_ref, wb_ref, wc_ref, wo_ref, g1_ref[...], b1_ref[...])
        h_sc[...] = h
        logits = _dot(h.astype(BF16), wr_ref[...]) + br_ref[...]
        hot1, hot2, w1, w2 = _route(logits)
        comb_sc[...] = jnp.where(hot1, w1, 0.0) + jnp.where(hot2, w2, 0.0)
        acc_sc[...] = jnp.zeros_like(acc_sc)

    out = _expert_mlp(h_sc[...].astype(BF16), wg_ref[...].astype(BF16), wu_ref[...].astype(BF16),
                      wd_ref[...].astype(BF16))
    lane = lax.broadcasted_iota(jnp.int32, comb_sc.shape, 1)
    ce = jnp.sum(jnp.where(lane == e, comb_sc[...], 0.0), axis=1, keepdims=True)
    acc_sc[...] += ce * out

    @pl.when(e == pl.num_programs(0) - 1)
    def _():
        y_ref[...] = _layer_norm(ALPHA * h_sc[...] + acc_sc[...], g2_ref[...], b2_ref[...])


def _sample_tail(x, z, oa, oc, state, wa, wb, wc, wo, wmix, pscale, g1, b1, wr, br, wg, wu, wd, g2, b2):
    db = x.shape[0]
    full = lambda w: pl.BlockSpec((db, w), lambda e: (0, 0))
    vec = lambda w: pl.BlockSpec((1, w), lambda e: (0, 0))
    hist = pl.BlockSpec((POOL_STATE, db, POOL_WIDTH), lambda e: (0, 0, 0))
    return pl.pallas_call(
        _sample_tail_kernel,
        grid=(N_EXPERTS,),
        in_specs=[
            full(D_MODEL), full(IN_WIDTH), full(Q_WIDTH), full(MEM_WIDTH), hist,
            _const_spec((Q_WIDTH, D_MODEL)), _const_spec((POOL_WIDTH, D_MODEL)),
            _const_spec((MEM_WIDTH, D_MODEL)), _const_spec((D_MODEL, D_MODEL)),
            _const_spec((len(POOL_WINDOWS), POOL_GROUP_DIM, POOL_GROUP_DIM)),
            vec(POOL_WIDTH), vec(D_MODEL), vec(D_MODEL),
            _const_spec((D_MODEL, LANES)), vec(LANES),
            pl.BlockSpec((None, D_MODEL, D_EXPERT), lambda e: (e, 0, 0)),
            pl.BlockSpec((None, D_MODEL, D_EXPERT), lambda e: (e, 0, 0)),
            pl.BlockSpec((None, D_EXPERT, D_MODEL), lambda e: (e, 0, 0)),
            vec(D_MODEL), vec(D_MODEL),
        ],
        out_specs=[full(D_MODEL), hist],
        out_shape=[jax.ShapeDtypeStruct((db, D_MODEL), F32),
                   jax.ShapeDtypeStruct((POOL_STATE, db, POOL_WIDTH), F32)],
        scratch_shapes=[pltpu.VMEM((db, D_MODEL), F32), pltpu.VMEM((db, LANES), F32),
                        pltpu.VMEM((db, D_MODEL), F32)],
        compiler_params=pltpu.CompilerParams(dimension_semantics=("arbitrary",), vmem_limit_bytes=VMEM_LIMIT),
        name="sample_tail",
    )(x, z, oa, oc, state, wa, wb, wc, wo, wmix, pscale, g1, b1, wr, br, wg, wu, wd, g2, b2)


def _rope_tables(pos):
    half = ROPE_DIM // 2
    inv = jnp.power(ROPE_THETA, -jnp.arange(half, dtype=F32) * (2.0 / ROPE_DIM))
    ang = pos.astype(F32)[:, None] * inv[None, :]
    lane = np.arange(LANES)
    off = lane % HEAD_DIM
    cos = jnp.cos(ang)[:, lane % half]
    sin = jnp.sin(ang)[:, lane % half]
    c = jnp.where(off[None, :] < ROPE_DIM, cos, 1.0)
    s1 = jnp.where((off[None, :] >= half) & (off[None, :] < ROPE_DIM), sin, 0.0)
    s2 = jnp.where(off[None, :] < half, -sin, 0.0)
    return jnp.stack([c, s1, s2]).astype(F32)


def _q_heads_group_major(w, axis):
    if axis == 1:
        n = w.shape[0]
        return w.reshape(n, N_KV, GROUP, HEAD_DIM).transpose(0, 2, 1, 3).reshape(n, Q_WIDTH)
    n = w.shape[1]
    return w.reshape(N_KV, GROUP, HEAD_DIM, n).transpose(1, 0, 2, 3).reshape(Q_WIDTH, n)


def kernel(x_prompt, x_sample, cache_win_k, cache_win_v, state_pool, cache_mem_k, cache_mem_v, mem_prompt, w_in, sinks, w_pool_mix, pool_scale, w_mem_k, w_mem_v, w_branch_a, w_branch_b, w_branch_c, w_out, ln1_g, ln1_b, w_group, b_group, w_router, b_router, w_gate, w_up, w_down, ln2_g, ln2_b):
    assert w_in.shape[0] == DEPTH == 1
    b, l, _ = x_prompt.shape
    db, ds, _ = x_sample.shape
    assert ds == 1 and l % TM == 0 and db % SB == 0
    assert cache_win_k.shape[2] == WINDOW
    t = b * l

    win = w_in[0]
    scale = HEAD_DIM ** -0.5
    wq = _q_heads_group_major(win[:, Q0:Q0 + Q_WIDTH], 1) * scale
    wcq = win[:, CQ0:CQ0 + MEM_WIDTH] * scale
    win_b = jnp.concatenate([wq, win[:, K0:CQ0], wcq, win[:, GZ0:]], axis=1).astype(BF16)
    wa = _q_heads_group_major(w_branch_a[0], 0).astype(BF16)
    wb = w_branch_b[0].astype(BF16)
    wc = w_branch_c[0].astype(BF16)
    wo = w_out[0].astype(BF16)
    wmix = w_pool_mix[0].astype(BF16)
    pscale = pool_scale[0].reshape(1, POOL_WIDTH)
    g1 = ln1_g[0].reshape(1, D_MODEL)
    b1 = ln1_b[0].reshape(1, D_MODEL)
    g2 = ln2_g[0].reshape(1, D_MODEL)
    b2 = ln2_b[0].reshape(1, D_MODEL)
    wr = jnp.concatenate([w_group[0], w_router[0].reshape(D_MODEL, N_EXPERTS)], axis=1)
    wr = jnp.pad(wr, ((0, 0), (0, LANES - wr.shape[1]))).astype(BF16)
    br = jnp.pad(jnp.concatenate([b_group[0], b_router[0].reshape(N_EXPERTS)]), (0, LANES - N_EXPERT_GROUPS - N_EXPERTS))
    br = br.reshape(1, LANES).astype(F32)
    wg = w_gate[0]
    wu = w_up[0]
    wd = w_down[0]
    sink = sinks[0].astype(F32)
    sink_gk = jnp.broadcast_to(sink.reshape(N_KV, GROUP).T.reshape(N_HEADS, 1), (N_HEADS, LANES))

    mk, mv = _mem_project(mem_prompt, w_mem_k[0].astype(BF16), w_mem_v[0].astype(BF16))
    rope_p = _rope_tables(jnp.arange(l, dtype=jnp.int32))
    h, xs, route, counts, nk_p, nv_p, npool_p = _front(
        x_prompt, rope_p, sink, win_b, wa, wb, wc, wo, wmix, pscale, mk, mv, g1, b1, wr, br)
    piece_src, piece_dst, tile_expert, tile_active = _piece_tables(
        counts.reshape(-1, SUBLANES, LANES)[:, 0, :N_EXPERTS].astype(jnp.int32))
    ys = _grouped_gemm(piece_src, piece_dst, tile_expert, tile_active, xs, wg, wu, wd)
    y_p = _combine(ys, h.reshape(t, D_MODEL), route.reshape(t, LANES), g2, b2).reshape(b, l, D_MODEL)

    rope_s = _rope_tables(jnp.full((1,), PAST_LEN, jnp.int32))
    xs = x_sample.reshape(db, D_MODEL)
    feat_major = lambda c: jnp.transpose(c[0], (0, 2, 3, 1)).reshape(db, KV_WIDTH, c.shape[2])
    from_feat_major = lambda a: jnp.transpose(a.reshape(db, N_KV, HEAD_DIM, a.shape[2]), (0, 3, 1, 2))[None]
    z_s, oa_s, oc_s, nk_s, nv_s = _sample_attn(xs, rope_s, win_b, sink_gk, feat_major(cache_win_k),
                                               feat_major(cache_win_v), feat_major(cache_mem_k), feat_major(cache_mem_v))
    state = jnp.transpose(state_pool[0], (1, 0, 2))
    y_s, npool_s = _sample_tail(xs, z_s, oa_s, oc_s, state, wa, wb, wc, wo, wmix, pscale, g1, b1, wr, br,
                                wg, wu, wd, g2, b2)

    kv5 = lambda a, n, w: a.reshape(1, n, w, N_KV, HEAD_DIM)
    return (y_p, y_s.reshape(db, 1, D_MODEL),
            kv5(nk_p, b, QB), kv5(nv_p, b, QB),
            npool_p[:, 2 * SUBLANES - POOL_STATE:, :][None],
            kv5(mk, b, N_MEM), kv5(mv, b, N_MEM),
            from_feat_major(nk_s), from_feat_major(nv_s),
            jnp.transpose(npool_s, (1, 0, 2))[None])
```

```python
import functools

import jax
import jax.numpy as jnp
import numpy as np
from jax import lax
from jax.experimental import pallas as pl
from jax.experimental.pallas import tpu as pltpu

D_MODEL = 1024
N_HEADS = 16
HEAD_DIM = 64
N_KV = 4
GROUP = N_HEADS // N_KV
WINDOW = 128
ROPE_THETA = 500000.0
ROPE_DIM = HEAD_DIM // 4
Q_WIDTH = N_HEADS * HEAD_DIM
KV_WIDTH = N_KV * HEAD_DIM
POOL_WINDOWS = (2, 4, 8, 16)
POOL_WIDTH = D_MODEL // 2
POOL_GROUP_DIM = POOL_WIDTH // len(POOL_WINDOWS)
POOL_STATE = max(POOL_WINDOWS) - 1
N_MEM = 256
MEM_HEADS = 4
MEM_WIDTH = MEM_HEADS * HEAD_DIM
N_EXPERT_GROUPS = 4
EXPERTS_PER_GROUP = 4
N_EXPERTS = N_EXPERT_GROUPS * EXPERTS_PER_GROUP
D_EXPERT = 512
PAST_LEN = 16384
DEPTH = 1
ALPHA = (2.0 * DEPTH) ** 0.25
LN_EPS = 1e-5

Q0 = 0
K0 = Q0 + Q_WIDTH
V0 = K0 + KV_WIDTH
U0 = V0 + KV_WIDTH
CQ0 = U0 + POOL_WIDTH
GZ0 = CQ0 + MEM_WIDTH
IN_WIDTH = GZ0 + 3 * D_MODEL

LANES = 128
SUBLANES = 8
VMEM_LIMIT = 56 * 1024 * 1024

TM = 256
QB = WINDOW
PIECE = 16
PIECES_PER_TILE = 16
TG = PIECE * PIECES_PER_TILE
MAX_CHUNK_PIECES = 2 * TM // PIECE + N_EXPERTS - 1
N_SPARE = 2 * PIECES_PER_TILE + 1
IN_SLOTS = 3


def _chunk_rows(n_chunks):
    spare = -(-N_SPARE // n_chunks)
    return -(-(MAX_CHUNK_PIECES + spare) * PIECE // LANES) * LANES
SB = 8
IN_CHUNK = 768
SOFTMAX_ROWS = 64
assert N_MEM == 2 * QB and MEM_HEADS == N_KV and MEM_WIDTH == KV_WIDTH and TM <= GROUP * QB

BF16 = jnp.bfloat16
F32 = jnp.float32
NEG_INF = float("-inf")


def _const_spec(shape):
    nd = len(shape)
    return pl.BlockSpec(shape, lambda *_: (0,) * nd, pipeline_mode=pl.Buffered(1))


def _layer_norm(x, g, b):
    mu = jnp.mean(x, axis=-1, keepdims=True)
    xc = x - mu
    var = jnp.mean(xc * xc, axis=-1, keepdims=True)
    return xc * lax.rsqrt(var + LN_EPS) * g + b


def _dot(a, b):
    return jnp.dot(a, b, preferred_element_type=F32)


def _dot_nt(a, b):
    return lax.dot_general(a, b, (((1,), (1,)), ((), ())), preferred_element_type=F32)


def _lane_block_mask(shape, block, width=HEAD_DIM):
    lane = lax.broadcasted_iota(jnp.int32, shape, len(shape) - 1)
    return (lane >= block * width) & (lane < (block + 1) * width)


def _rope(x, c, s1, s2):
    half = ROPE_DIM // 2
    return x * c + pltpu.roll(x, half, 1) * s1 + pltpu.roll(x, LANES - half, 1) * s2


def _route(logits):
    rows = logits.shape[0]
    lane = lax.broadcasted_iota(jnp.int32, (rows, LANES), 1)
    lanef = lane.astype(F32)
    big = float(LANES)
    is_g = lane < N_EXPERT_GROUPS
    glog = jnp.where(is_g, logits, NEG_INF)
    gmax = jnp.max(glog, axis=1, keepdims=True)
    gsum = jnp.sum(jnp.where(is_g, jnp.exp(glog - gmax), 0.0), axis=1, keepdims=True)
    gp = 1.0 / gsum
    gidx = jnp.min(jnp.where(glog == gmax, lanef, big), axis=1, keepdims=True).astype(jnp.int32)
    lo = N_EXPERT_GROUPS + gidx * EXPERTS_PER_GROUP
    in_grp = (lane >= lo) & (lane < lo + EXPERTS_PER_GROUP)
    el = jnp.where(in_grp, logits, NEG_INF)
    v1 = jnp.max(el, axis=1, keepdims=True)
    i1 = jnp.min(jnp.where(el == v1, lanef, big), axis=1, keepdims=True).astype(jnp.int32)
    el2 = jnp.where(lane == i1, NEG_INF, el)
    v2 = jnp.max(el2, axis=1, keepdims=True)
    i2 = jnp.min(jnp.where(el2 == v2, lanef, big), axis=1, keepdims=True).astype(jnp.int32)
    e21 = jnp.exp(v2 - v1)
    inv = 1.0 / (1.0 + e21)
    w1 = inv * gp
    w2 = e21 * inv * gp
    e1 = i1 - N_EXPERT_GROUPS
    e2 = i2 - N_EXPERT_GROUPS
    return lane == e1, lane == e2, w1, w2


def _local_sort(hot1, hot2, w1, w2, hb, cap):
    rows = hb.shape[0]
    lane = lax.broadcasted_iota(jnp.int32, (rows, LANES), 1)
    onehot = jnp.where(hot1 | hot2, 1.0, 0.0)
    counts = jnp.sum(onehot, axis=0, keepdims=True)
    before = (lax.broadcasted_iota(jnp.int32, (rows, rows), 1)
              < lax.broadcasted_iota(jnp.int32, (rows, rows), 0)).astype(BF16)
    rank = _dot(before, onehot.astype(BF16))
    run = (((counts.astype(jnp.int32) + (PIECE - 1)) // PIECE) * PIECE).astype(F32)
    lower = (lax.broadcasted_iota(jnp.int32, (LANES, LANES), 0)
             < lax.broadcasted_iota(jnp.int32, (LANES, LANES), 1)).astype(BF16)
    start = _dot(jnp.broadcast_to(run, (SUBLANES, LANES)).astype(BF16), lower)[0:1]
    slot = start + rank
    s1 = jnp.sum(jnp.where(hot1, slot, 0.0), axis=1, keepdims=True)
    s2 = jnp.sum(jnp.where(hot2, slot, 0.0), axis=1, keepdims=True)
    route = jnp.where(lane == 0, w1, jnp.where(lane == 1, w2, jnp.where(lane == 2, s1, jnp.where(lane == 3, s2, 0.0))))
    route_t = route.T
    srow = lax.broadcasted_iota(jnp.int32, (cap, rows), 0).astype(F32)
    perm = jnp.where((srow == route_t[2:3, :]) | (srow == route_t[3:4, :]), 1.0, 0.0).astype(BF16)
    return _dot(perm, hb).astype(BF16), route, counts


def _sigmoid(x):
    return 0.5 * jnp.tanh(0.5 * x) + 0.5


def _merge_ln1(x, oa, ob, oc, gz, wa_ref, wb_ref, wc_ref, wo_ref, g1, b1):
    ya = _dot(oa, wa_ref[...])
    yb = _dot(ob.astype(BF16), wb_ref[...])
    yc = _dot(oc.astype(BF16), wc_ref[...])
    m = (_sigmoid(gz[:, 0:D_MODEL]) * ya
         + _sigmoid(gz[:, D_MODEL:2 * D_MODEL]) * yb
         + _sigmoid(gz[:, 2 * D_MODEL:3 * D_MODEL]) * yc)
    hpre = ALPHA * x + _dot(m.astype(BF16), wo_ref[...])
    return _layer_norm(hpre, g1, b1)


def _mem_kernel(mem_ref, wk_ref, wv_ref, mk_ref, mv_ref):
    m = mem_ref[...].astype(BF16)
    mk_ref[...] = _dot(m, wk_ref[...])
    mv_ref[...] = _dot(m, wv_ref[...])


def _mem_project(mem, wk, wv):
    b = mem.shape[0]
    out = jax.ShapeDtypeStruct((b, N_MEM, MEM_WIDTH), F32)
    return pl.pallas_call(
        _mem_kernel,
        grid=(b,),
        in_specs=[pl.BlockSpec((None, N_MEM, D_MODEL), lambda i: (i, 0, 0)),
                  _const_spec((D_MODEL, MEM_WIDTH)), _const_spec((D_MODEL, MEM_WIDTH))],
        out_specs=[pl.BlockSpec((None, N_MEM, MEM_WIDTH), lambda i: (i, 0, 0))] * 2,
        out_shape=[out, out],
        name="mem_project",
    )(mem, wk, wv)


def _front_kernel(sinks_ref, x_ref, rope_ref, win_ref, wa_ref, wb_ref, wc_ref, wo_ref, wmix_ref,
                  pscale_ref, mk_ref, mv_ref, g1_ref, b1_ref, wr_ref, br_ref,
                  h_ref, xs_ref, route_ref, counts_ref, nk_ref, nv_ref, npool_ref,
                  z_ref, qb_ref, kext_ref, vext_ref, uext_ref, oa_ref, ob_ref, bias_ref, s_ref, p_ref, vblk_ref):
    i = pl.program_id(1)
    x = x_ref[...]
    xb = x.astype(BF16)
    hist = 2 * SUBLANES

    @pl.when(i == 0)
    def _():
        kext_ref[0:QB, :] = jnp.zeros((QB, KV_WIDTH), BF16)
        vext_ref[0:QB, :] = jnp.zeros((QB, KV_WIDTH), BF16)
        uext_ref[0:hist, :] = jnp.zeros((hist, POOL_WIDTH), F32)

    @pl.when(i > 0)
    def _():
        kext_ref[0:QB, :] = kext_ref[TM:TM + QB, :]
        vext_ref[0:QB, :] = vext_ref[TM:TM + QB, :]
        uext_ref[0:hist, :] = uext_ref[TM:TM + hist, :]

    for c0 in range(0, IN_WIDTH, IN_CHUNK):
        z_ref[:, c0:c0 + IN_CHUNK] = _dot(xb, win_ref[:, c0:c0 + IN_CHUNK])

    c = rope_ref[0]
    s1 = rope_ref[1]
    s2 = rope_ref[2]
    for j in range(Q_WIDTH // LANES):
        sl = slice(Q0 + j * LANES, Q0 + (j + 1) * LANES)
        qb_ref[:, j * LANES:(j + 1) * LANES] = _rope(z_ref[:, sl], c, s1, s2).astype(BF16)
    for j in range(KV_WIDTH // LANES):
        sl = slice(K0 + j * LANES, K0 + (j + 1) * LANES)
        kr = _rope(z_ref[:, sl], c, s1, s2)
        z_ref[:, sl] = kr
        kext_ref[QB:QB + TM, j * LANES:(j + 1) * LANES] = kr.astype(BF16)
    vext_ref[QB:QB + TM, :] = z_ref[:, V0:V0 + KV_WIDTH].astype(BF16)
    uext_ref[hist:hist + TM, :] = z_ref[:, U0:U0 + POOL_WIDTH]
    nk_ref[...] = z_ref[TM - QB:TM, K0:K0 + KV_WIDTH]
    nv_ref[...] = z_ref[TM - QB:TM, V0:V0 + KV_WIDTH]

    rowq = lax.broadcasted_iota(jnp.int32, (QB, 2 * QB), 0)
    colk = lax.broadcasted_iota(jnp.int32, (QB, 2 * QB), 1)
    band = (colk >= rowq) & (colk <= rowq + WINDOW)
    bias_ref[1] = jnp.where(band, 0.0, NEG_INF)
    bias_ref[0] = jnp.where(band & ((colk >= QB) | (i > 0)), 0.0, NEG_INF)
    for sb in range(TM // QB):
        k2 = kext_ref[sb * QB:(sb + 2) * QB, :]
        v2 = vext_ref[sb * QB:(sb + 2) * QB, :]
        qs = jnp.concatenate(
            [qb_ref[sb * QB:(sb + 1) * QB, g * KV_WIDTH:(g + 1) * KV_WIDTH] for g in range(GROUP)], axis=0)
        for kv in range(N_KV):
            kmask = _lane_block_mask((2 * QB, KV_WIDTH), kv)
            s_ref[...] = _dot_nt(qs, jnp.where(kmask, k2, jnp.zeros_like(k2)))
            vblk_ref[kv * 2 * QB:(kv + 1) * 2 * QB, :] = jnp.where(kmask, v2, jnp.zeros_like(v2))
            for c0 in range(0, GROUP * QB, SOFTMAX_ROWS):
                rq = c0 % QB
                sink = sinks_ref[kv * GROUP + c0 // QB]
                s = s_ref[c0:c0 + SOFTMAX_ROWS, :] + bias_ref[min(sb, 1), rq:rq + SOFTMAX_ROWS, :]
                m = jnp.maximum(jnp.max(s, axis=1, keepdims=True), sink)
                p = jnp.exp(s - m)
                den = jnp.sum(p, axis=1, keepdims=True) + jnp.exp(sink - m)
                p_ref[c0:c0 + SOFTMAX_ROWS, kv * 2 * QB:(kv + 1) * 2 * QB] = (p * (1.0 / den)).astype(BF16)
        o = _dot(p_ref[...], vblk_ref[...])
        for g in range(GROUP):
            oa_ref[sb * QB:(sb + 1) * QB, g * KV_WIDTH:(g + 1) * KV_WIDTH] = o[g * QB:(g + 1) * QB].astype(BF16)

    npool_ref[...] = uext_ref[TM:TM + hist, :]
    pos = i * TM + lax.broadcasted_iota(jnp.int32, (TM, 1), 0)
    for g, w in enumerate(POOL_WINDOWS):
        sl = slice(g * POOL_GROUP_DIM, (g + 1) * POOL_GROUP_DIM)
        cur = uext_ref[hist:hist + TM, sl]
        ws = cur
        for j in range(1, w):
            ws = ws + uext_ref[hist - j:hist - j + TM, sl]
        cnt = jnp.minimum(pos + 1, w).astype(F32)
        pooled = ws / cnt - cur
        ob_ref[:, sl] = _dot(pooled.astype(BF16), wmix_ref[g]) * pscale_ref[:, sl]

    cq = z_ref[:, CQ0:CQ0 + MEM_WIDTH].astype(BF16)
    mk = mk_ref[...].astype(BF16)
    mv = mv_ref[...].astype(BF16)
    for hh in range(MEM_HEADS):
        hmask = _lane_block_mask((N_MEM, MEM_WIDTH), hh)
        s_ref[0:TM, :] = _dot_nt(cq, jnp.where(hmask, mk, jnp.zeros_like(mk)))
        vblk_ref[hh * N_MEM:(hh + 1) * N_MEM, :] = jnp.where(hmask, mv, jnp.zeros_like(mv))
        for c0 in range(0, TM, SOFTMAX_ROWS):
            s = s_ref[c0:c0 + SOFTMAX_ROWS, :]
            p = jnp.exp(s - jnp.max(s, axis=1, keepdims=True))
            den = jnp.sum(p, axis=1, keepdims=True)
            p_ref[c0:c0 + SOFTMAX_ROWS, hh * N_MEM:(hh + 1) * N_MEM] = (p * (1.0 / den)).astype(BF16)
    oc = _dot(p_ref[0:TM, :], vblk_ref[...])

    h = _merge_ln1(x, oa_ref[...], ob_ref[...], oc, z_ref[:, GZ0:GZ0 + 3 * D_MODEL],
                   wa_ref, wb_ref, wc_ref, wo_ref, g1_ref[...], b1_ref[...])
    h_ref[...] = h
    hb = h.astype(BF16)
    logits = _dot(hb, wr_ref[...]) + br_ref[...]
    xs, route, counts = _local_sort(*_route(logits), hb, xs_ref.shape[0])
    xs_ref[...] = xs
    route_ref[...] = route
    counts_ref[...] = jnp.broadcast_to(counts, (SUBLANES, LANES))


def _front(x, rope, sinks, win, wa, wb, wc, wo, wmix, pscale, mk, mv, g1, b1, wr, br):
    b, l, _ = x.shape
    nt = l // TM
    cap = _chunk_rows(b * nt)
    hist = 2 * SUBLANES
    tile = lambda w: pl.BlockSpec((None, TM, w), lambda bi, ti: (bi, ti, 0))
    per_b = lambda r, w: pl.BlockSpec((None, r, w), lambda bi, ti: (bi, 0, 0))
    return pl.pallas_call(
        _front_kernel,
        grid=(b, nt),
        in_specs=[
            pl.BlockSpec(memory_space=pltpu.SMEM),
            tile(D_MODEL),
            pl.BlockSpec((3, TM, LANES), lambda bi, ti: (0, ti, 0)),
            _const_spec((D_MODEL, IN_WIDTH)),
            _const_spec((Q_WIDTH, D_MODEL)), _const_spec((POOL_WIDTH, D_MODEL)),
            _const_spec((MEM_WIDTH, D_MODEL)), _const_spec((D_MODEL, D_MODEL)),
            _const_spec((len(POOL_WINDOWS), POOL_GROUP_DIM, POOL_GROUP_DIM)),
            _const_spec((1, POOL_WIDTH)),
            per_b(N_MEM, MEM_WIDTH), per_b(N_MEM, MEM_WIDTH),
            _const_spec((1, D_MODEL)), _const_spec((1, D_MODEL)),
            _const_spec((D_MODEL, LANES)), _const_spec((1, LANES)),
        ],
        out_specs=[
            tile(D_MODEL),
            pl.BlockSpec((cap, D_MODEL), lambda bi, ti: (bi * nt + ti, 0)),
            tile(LANES),
            pl.BlockSpec((None, None, SUBLANES, LANES), lambda bi, ti: (bi, ti, 0, 0)),
            per_b(QB, KV_WIDTH), per_b(QB, KV_WIDTH), per_b(hist, POOL_WIDTH),
        ],
        out_shape=[
            jax.ShapeDtypeStruct((b, l, D_MODEL), F32),
            jax.ShapeDtypeStruct((b * nt * cap, D_MODEL), BF16),
            jax.ShapeDtypeStruct((b, l, LANES), F32),
            jax.ShapeDtypeStruct((b, nt, SUBLANES, LANES), F32),
            jax.ShapeDtypeStruct((b, QB, KV_WIDTH), F32),
            jax.ShapeDtypeStruct((b, QB, KV_WIDTH), F32),
            jax.ShapeDtypeStruct((b, hist, POOL_WIDTH), F32),
        ],
        scratch_shapes=[
            pltpu.VMEM((TM, IN_WIDTH), F32),
            pltpu.VMEM((TM, Q_WIDTH), BF16),
            pltpu.VMEM((QB + TM, KV_WIDTH), BF16),
            pltpu.VMEM((QB + TM, KV_WIDTH), BF16),
            pltpu.VMEM((hist + TM, POOL_WIDTH), F32),
            pltpu.VMEM((TM, Q_WIDTH), BF16),
            pltpu.VMEM((TM, POOL_WIDTH), F32),
            pltpu.VMEM((2, QB, 2 * QB), F32),
            pltpu.VMEM((GROUP * QB, 2 * QB), F32),
            pltpu.VMEM((GROUP * QB, N_KV * 2 * QB), BF16),
            pltpu.VMEM((N_KV * 2 * QB, KV_WIDTH), BF16),
        ],
        compiler_params=pltpu.CompilerParams(
            dimension_semantics=("arbitrary", "arbitrary"), vmem_limit_bytes=VMEM_LIMIT),
        name="front_prompt",
    )(sinks, x, rope, win, wa, wb, wc, wo, wmix, pscale, mk, mv, g1, b1, wr, br)


def _expert_mlp(xb, wg, wu, wd):
    a = _dot(xb, wg)
    hid = (a * jax.nn.sigmoid(a)) * _dot(xb, wu)
    return _dot(hid.astype(BF16), wd)


def _gemm_kernel(src_ref, dst_ref, te_ref, act_ref, xs_ref, wg_ref, wu_ref, wd_ref, ys_ref,
                 xbuf, obuf, wgb, wub, wdb, prime, sem_in, sem_out):
    i = pl.program_id(0)
    n = pl.num_programs(0)
    slot = i % 2
    in_slot = i % IN_SLOTS

    def start_in(tile, slot):
        for j in range(PIECES_PER_TILE):
            row0 = pl.multiple_of(src_ref[tile * PIECES_PER_TILE + j], PIECE)
            pltpu.make_async_copy(xs_ref.at[pl.ds(row0, PIECE)], xbuf.at[slot, pl.ds(j * PIECE, PIECE)],
                                  sem_in.at[slot]).start()

    def start_out(tile, slot):
        for j in range(PIECES_PER_TILE):
            row0 = pl.multiple_of(dst_ref[tile * PIECES_PER_TILE + j], PIECE)
            pltpu.make_async_copy(obuf.at[slot, pl.ds(j * PIECE, PIECE)], ys_ref.at[pl.ds(row0, PIECE)],
                                  sem_out.at[slot]).start()

    def wait_in(slot):
        for j in range(PIECES_PER_TILE):
            pltpu.make_async_copy(xs_ref.at[pl.ds(0, PIECE)], xbuf.at[slot, pl.ds(j * PIECE, PIECE)],
                                  sem_in.at[slot]).wait()

    def wait_out(slot):
        for j in range(PIECES_PER_TILE):
            pltpu.make_async_copy(obuf.at[slot, pl.ds(j * PIECE, PIECE)], ys_ref.at[pl.ds(0, PIECE)],
                                  sem_out.at[slot]).wait()

    active = act_ref[i] > 0
    ahead = IN_SLOTS - 1
    prefetched = (i < ahead) | (act_ref[jnp.maximum(i - ahead, 0)] > 0)
    out_pending = (i < 2) | (act_ref[jnp.maximum(i - 2, 0)] > 0)

    @pl.when(i == 0)
    def _():
        prime[0] = jnp.zeros((PIECE, D_MODEL), BF16)
        for s in range(2):
            for j in range(PIECES_PER_TILE):
                pltpu.make_async_copy(prime.at[0], prime.at[1 + s * PIECES_PER_TILE + j], sem_out.at[s]).start()
        for t in range(ahead):
            start_in(t, t)

    @pl.when(active)
    def _():
        start_in(i + ahead, (i + ahead) % IN_SLOTS)

        @pl.when((i == 0) | (te_ref[i] != te_ref[jnp.maximum(i - 1, 0)]))
        def _():
            wgb[...] = wg_ref[...].astype(BF16)
            wub[...] = wu_ref[...].astype(BF16)
            wdb[...] = wd_ref[...].astype(BF16)

        wait_in(in_slot)
        wait_out(slot)
        obuf[slot] = _expert_mlp(xbuf[in_slot], wgb[...], wub[...], wdb[...]).astype(BF16)
        start_out(i, slot)

    @pl.when(jnp.logical_not(active))
    def _():
        @pl.when(prefetched)
        def _():
            wait_in(in_slot)

        @pl.when(out_pending)
        def _():
            wait_out(slot)

    @pl.when(i == n - 1)
    def _():
        for t in range(ahead):
            @pl.when(act_ref[jnp.maximum(i - t, 0)] > 0)
            def _(t=t):
                wait_in((i - t + ahead) % IN_SLOTS)

        @pl.when(active)
        def _():
            wait_out(slot)

        @pl.when((i >= 1) & (act_ref[jnp.maximum(i - 1, 0)] > 0))
        def _():
            wait_out(1 - slot)


def _grouped_gemm(piece_src, piece_dst, tile_expert, tile_active, xs, wg, wu, wd):
    n_tiles = tile_expert.shape[0]
    assert n_tiles >= 2
    wspec = lambda r, c: pl.BlockSpec((None, r, c), lambda i, src, dst, te, act: (te[i], 0, 0))
    return pl.pallas_call(
        _gemm_kernel,
        grid_spec=pltpu.PrefetchScalarGridSpec(
            num_scalar_prefetch=4,
            grid=(n_tiles,),
            in_specs=[pl.BlockSpec(memory_space=pl.ANY),
                      wspec(D_MODEL, D_EXPERT), wspec(D_MODEL, D_EXPERT), wspec(D_EXPERT, D_MODEL)],
            out_specs=pl.BlockSpec(memory_space=pl.ANY),
            scratch_shapes=[pltpu.VMEM((IN_SLOTS, TG, D_MODEL), BF16), pltpu.VMEM((2, TG, D_MODEL), BF16),
                            pltpu.VMEM((D_MODEL, D_EXPERT), BF16), pltpu.VMEM((D_MODEL, D_EXPERT), BF16),
                            pltpu.VMEM((D_EXPERT, D_MODEL), BF16),
                            pltpu.VMEM((1 + 2 * PIECES_PER_TILE, PIECE, D_MODEL), BF16),
                            pltpu.SemaphoreType.DMA((IN_SLOTS,)), pltpu.SemaphoreType.DMA((2,))],
        ),
        out_shape=jax.ShapeDtypeStruct(xs.shape, xs.dtype),
        input_output_aliases={4: 0},
        compiler_params=pltpu.CompilerParams(dimension_semantics=("arbitrary",), vmem_limit_bytes=VMEM_LIMIT),
        name="moe_grouped_gemm",
    )(piece_src, piece_dst, tile_expert, tile_active, xs, wg, wu, wd)


def _combine_kernel(ys_ref, h_ref, route_ref, g2_ref, b2_ref, y_ref):
    route = route_ref[...]
    slot = lax.broadcasted_iota(jnp.int32, (TM, ys_ref.shape[0]), 1).astype(F32)
    sel = jnp.concatenate([jnp.where(slot == route[:, 2:3], 1.0, 0.0).astype(BF16),
                           jnp.where(slot == route[:, 3:4], 1.0, 0.0).astype(BF16)], axis=0)
    picked = _dot(sel, ys_ref[...])
    f = route[:, 0:1] * picked[0:TM] + route[:, 1:2] * picked[TM:2 * TM]
    y_ref[...] = _layer_norm(ALPHA * h_ref[...] + f, g2_ref[...], b2_ref[...])


def _combine(ys, h, route, g2, b2):
    t = h.shape[0]
    return pl.pallas_call(
        _combine_kernel,
        grid=(t // TM,),
        in_specs=[
            pl.BlockSpec((ys.shape[0] // (t // TM), D_MODEL), lambda i: (i, 0)),
            pl.BlockSpec((TM, D_MODEL), lambda i: (i, 0)),
            pl.BlockSpec((TM, LANES), lambda i: (i, 0)),
            pl.BlockSpec((1, D_MODEL), lambda i: (0, 0)),
            pl.BlockSpec((1, D_MODEL), lambda i: (0, 0)),
        ],
        out_specs=pl.BlockSpec((TM, D_MODEL), lambda i: (i, 0)),
        out_shape=jax.ShapeDtypeStruct((t, D_MODEL), F32),
        compiler_params=pltpu.CompilerParams(dimension_semantics=("arbitrary",)),
        name="moe_combine",
    )(ys, h, route, g2, b2)


def _select(table, idx):
    hot = idx[:, None] == jnp.arange(table.shape[0], dtype=jnp.int32)[None, :]
    return jnp.sum(jnp.where(hot[:, :, None], table[None, :, :], 0), axis=1)


def _piece_tables(counts):
    n_chunks = counts.shape[0]
    n_tiles = -(-(n_chunks * MAX_CHUNK_PIECES + N_EXPERTS * (PIECES_PER_TILE - 1)) // PIECES_PER_TILE)
    npc = (counts + (PIECE - 1)) // PIECE
    first = (jnp.cumsum(npc, axis=1) - npc).T
    npc_t = npc.T
    cum = jnp.cumsum(npc_t, axis=1)
    per_expert = cum[:, -1]
    tiles_e = (per_expert + (PIECES_PER_TILE - 1)) // PIECES_PER_TILE
    tile_end = jnp.cumsum(tiles_e)
    tile_idx = jnp.arange(n_tiles, dtype=jnp.int32)
    expert_of = lambda i: jnp.minimum(jnp.sum(i[:, None] >= tile_end[None, :], axis=1), N_EXPERTS - 1).astype(jnp.int32)
    active = tile_idx < tile_end[-1]
    tile_expert = jnp.where(active, expert_of(tile_idx), expert_of(tile_end[-1:] - 1))
    meta = jnp.stack([tile_end - tiles_e, per_expert], axis=1)
    meta_t = _select(meta, tile_expert)
    k = (tile_idx - meta_t[:, 0])[:, None] * PIECES_PER_TILE + jnp.arange(PIECES_PER_TILE, dtype=jnp.int32)[None, :]
    valid = active[:, None] & (k < meta_t[:, 1:2])
    cum_t = _select(cum, tile_expert)
    chunk = jnp.minimum(jnp.sum(k[:, :, None] >= cum_t[:, None, :], axis=2), n_chunks - 1).astype(jnp.int32)
    at_chunk = chunk[:, :, None] == jnp.arange(n_chunks, dtype=jnp.int32)[None, None, :]
    pick = lambda tab: jnp.sum(jnp.where(at_chunk, _select(tab, tile_expert)[:, None, :], 0), axis=2)
    piece = pick(first) + k - pick(cum - npc_t)
    cap = _chunk_rows(n_chunks)
    rows = chunk * cap + piece * PIECE
    d = (tile_idx % 2)[:, None] * PIECES_PER_TILE + jnp.arange(PIECES_PER_TILE, dtype=jnp.int32)[None, :]
    spare_row = lambda d: (d % n_chunks) * cap + (MAX_CHUNK_PIECES + d // n_chunks) * PIECE
    spare = spare_row(d)
    zero_piece = spare_row(N_SPARE - 1)
    extra = jnp.full(((IN_SLOTS - 1) * PIECES_PER_TILE,), zero_piece, jnp.int32)
    src = jnp.concatenate([jnp.where(valid, rows, zero_piece).astype(jnp.int32).reshape(-1), extra])
    dst = jnp.concatenate([jnp.where(valid, rows, spare).astype(jnp.int32).reshape(-1), extra])
    return src, dst, tile_expert.astype(jnp.int32), active.astype(jnp.int32)


def _sample_attn_kernel(x_ref, rope_ref, win_ref, sink_ref, ck_ref, cv_ref, cmk_ref, cmv_ref,
                        z_ref, oa_ref, oc_ref, nk_ref, nv_ref, knew_t, vnew_t):
    j = pl.program_id(0)
    db = x_ref.shape[0]

    @pl.when(j == 0)
    def _():
        xb = x_ref[...].astype(BF16)
        for c0 in range(0, IN_WIDTH, IN_CHUNK):
            z_ref[:, c0:c0 + IN_CHUNK] = _dot(xb, win_ref[:, c0:c0 + IN_CHUNK])
        c = rope_ref[0]
        s1 = rope_ref[1]
        s2 = rope_ref[2]
        for jj in range((Q_WIDTH + KV_WIDTH) // LANES):
            sl = slice(jj * LANES, (jj + 1) * LANES)
            z_ref[:, sl] = _rope(z_ref[:, sl], c, s1, s2)
        knew_t[...] = z_ref[:, K0:K0 + KV_WIDTH].T
        vnew_t[...] = z_ref[:, V0:V0 + KV_WIDTH].T

    r0 = pl.multiple_of(j * SB, SB)
    zq = z_ref[pl.ds(r0, SB), Q0:Q0 + Q_WIDTH]
    zk = z_ref[pl.ds(r0, SB), K0:K0 + KV_WIDTH]
    zv = z_ref[pl.ds(r0, SB), V0:V0 + KV_WIDTH]
    zc = z_ref[pl.ds(r0, SB), CQ0:CQ0 + MEM_WIDTH]
    sink = sink_ref[:, 0:1]
    row_kv = lax.broadcasted_iota(jnp.int32, (N_HEADS, KV_WIDTH), 0) & (N_KV - 1)
    lane_kv = lax.broadcasted_iota(jnp.int32, (N_HEADS, KV_WIDTH), 1) // HEAD_DIM
    own = row_kv == lane_kv
    row_c = lax.broadcasted_iota(jnp.int32, (N_HEADS, MEM_WIDTH), 0)
    lane_c = lax.broadcasted_iota(jnp.int32, (N_HEADS, MEM_WIDTH), 1) // HEAD_DIM
    own_c = row_c == lane_c
    last_pos = lax.broadcasted_iota(jnp.int32, (KV_WIDTH, WINDOW), 1) == WINDOW - 1
    seq_lane = lax.broadcasted_iota(jnp.int32, (KV_WIDTH, db), 1)

    oa_rows = [[] for _ in range(GROUP)]
    oc_rows = []
    for b in range(SB):
        kc = ck_ref[b]
        vc = cv_ref[b]
        knew = zk[b:b + 1, :]
        vnew = zv[b:b + 1, :]
        qblk = jnp.concatenate(
            [jnp.broadcast_to(zq[b:b + 1, g * KV_WIDTH:(g + 1) * KV_WIDTH], (N_KV, KV_WIDTH)) for g in range(GROUP)],
            axis=0)
        qblk = jnp.where(own, qblk, 0.0).astype(BF16)
        s = _dot(qblk, kc.astype(BF16))
        s_new = jnp.sum(qblk.astype(F32) * knew.astype(BF16).astype(F32), axis=1, keepdims=True)
        m = jnp.maximum(jnp.maximum(jnp.max(s, axis=1, keepdims=True), s_new), sink)
        p = jnp.exp(s - m)
        p_new = jnp.exp(s_new - m)
        inv = 1.0 / (jnp.sum(p, axis=1, keepdims=True) + p_new + jnp.exp(sink - m))
        o = (_dot_nt((p * inv).astype(BF16), vc.astype(BF16))
             + (p_new * inv).astype(BF16).astype(F32) * vnew.astype(BF16).astype(F32))
        o = jnp.where(own, o, 0.0)
        for g in range(GROUP):
            oa_rows[g].append(jnp.sum(o[g * N_KV:(g + 1) * N_KV], axis=0, keepdims=True))
        here = seq_lane == j * SB + b
        knew_col = jnp.sum(jnp.where(here, knew_t[...], 0.0), axis=1, keepdims=True)
        vnew_col = jnp.sum(jnp.where(here, vnew_t[...], 0.0), axis=1, keepdims=True)
        nk_ref[b] = jnp.where(last_pos, knew_col, pltpu.roll(kc, WINDOW - 1, 1))
        nv_ref[b] = jnp.where(last_pos, vnew_col, pltpu.roll(vc, WINDOW - 1, 1))
        cblk = jnp.where(own_c, jnp.broadcast_to(zc[b:b + 1, :], (N_HEADS, MEM_WIDTH)), 0.0)
        sc = _dot(cblk.astype(BF16), cmk_ref[b].astype(BF16))
        mc = jnp.max(sc, axis=1, keepdims=True)
        pc = jnp.exp(sc - mc)
        pc = pc * (1.0 / jnp.sum(pc, axis=1, keepdims=True))
        ocb = jnp.where(own_c, _dot_nt(pc.astype(BF16), cmv_ref[b].astype(BF16)), 0.0)
        oc_rows.append(jnp.sum(ocb, axis=0, keepdims=True))
    for g in range(GROUP):
        oa_ref[pl.ds(r0, SB), g * KV_WIDTH:(g + 1) * KV_WIDTH] = jnp.concatenate(oa_rows[g], axis=0)
    oc_ref[pl.ds(r0, SB), :] = jnp.concatenate(oc_rows, axis=0)


def _sample_attn(x, rope, win, sink_gk, ck, cv, cmk, cmv):
    db = x.shape[0]
    blk = lambda r: pl.BlockSpec((SB, KV_WIDTH, r), lambda j: (j, 0, 0))
    full = lambda w: pl.BlockSpec((db, w), lambda j: (0, 0))
    return pl.pallas_call(
        _sample_attn_kernel,
        grid=(db // SB,),
        in_specs=[
            full(D_MODEL), _const_spec((3, 1, LANES)), _const_spec((D_MODEL, IN_WIDTH)),
            _const_spec((N_HEADS, LANES)),
            blk(WINDOW), blk(WINDOW), blk(N_MEM), blk(N_MEM),
        ],
        out_specs=[full(IN_WIDTH), full(Q_WIDTH), full(MEM_WIDTH), blk(WINDOW), blk(WINDOW)],
        out_shape=[
            jax.ShapeDtypeStruct((db, IN_WIDTH), F32),
            jax.ShapeDtypeStruct((db, Q_WIDTH), F32),
            jax.ShapeDtypeStruct((db, MEM_WIDTH), F32),
            jax.ShapeDtypeStruct((db, KV_WIDTH, WINDOW), F32),
            jax.ShapeDtypeStruct((db, KV_WIDTH, WINDOW), F32),
        ],
        scratch_shapes=[pltpu.VMEM((KV_WIDTH, db), F32), pltpu.VMEM((KV_WIDTH, db), F32)],
        compiler_params=pltpu.CompilerParams(dimension_semantics=("arbitrary",), vmem_limit_bytes=VMEM_LIMIT),
        name="sample_attn",
    )(x, rope, win, sink_gk, ck, cv, cmk, cmv)


def _sample_tail_kernel(x_ref, z_ref, oa_ref, oc_ref, st_ref, wa_ref, wb_ref, wc_ref, wo_ref, wmix_ref,
                        pscale_ref, g1_ref, b1_ref, wr_ref, br_ref, wg_ref, wu_ref, wd_ref, g2_ref, b2_ref,
                        y_ref, npool_ref, h_sc, comb_sc, acc_sc):
    e = pl.program_id(0)

    @pl.when(e == 0)
    def _():
        u = z_ref[:, U0:U0 + POOL_WIDTH]
        npool_ref[0:POOL_STATE - 1] = st_ref[1:POOL_STATE]
        npool_ref[POOL_STATE - 1] = u
        obs = []
        for g, w in enumerate(POOL_WINDOWS):
            sl = slice(g * POOL_GROUP_DIM, (g + 1) * POOL_GROUP_DIM)
            cur = u[:, sl]
            ws = cur
            for jj in range(1, w):
                ws = ws + st_ref[POOL_STATE - jj, :, sl]
            cnt = float(min(PAST_LEN + 1, w))
            pooled = ws / cnt - cur
            obs.append(_dot(pooled.astype(BF16), wmix_ref[g]) * pscale_ref[:, sl])
        ob = jnp.concatenate(obs, axis=1)
        h = _merge_ln1(x_ref[...], oa_ref[...].astype(BF16), ob, oc_ref[...], z_ref[:, GZ0:GZ0 + 3 * D_MODEL],
                       wa_ref, wb_ref, wc_ref, wo_ref, g1_ref[...], b1_ref[...])
        h_sc[...] = h
        logits = _dot(h.astype(BF16), wr_ref[...]) + br_ref[...]
        hot1, hot2, w1, w2 = _route(logits)
        comb_sc[...] = jnp.where(hot1, w1, 0.0) + jnp.where(hot2, w2, 0.0)
        acc_sc[...] = jnp.zeros_like(acc_sc)

    out = _expert_mlp(h_sc[...].astype(BF16), wg_ref[...].astype(BF16), wu_ref[...].astype(BF16),
                      wd_ref[...].astype(BF16))
    lane = lax.broadcasted_iota(jnp.int32, comb_sc.shape, 1)
    ce = jnp.sum(jnp.where(lane == e, comb_sc[...], 0.0), axis=1, keepdims=True)
    acc_sc[...] += ce * out

    @pl.when(e == pl.num_programs(0) - 1)
    def _():
        y_ref[...] = _layer_norm(ALPHA * h_sc[...] + acc_sc[...], g2_ref[...], b2_ref[...])


def _sample_tail(x, z, oa, oc, state, wa, wb, wc, wo, wmix, pscale, g1, b1, wr, br, wg, wu, wd, g2, b2):
    db = x.shape[0]
    full = lambda w: pl.BlockSpec((db, w), lambda e: (0, 0))
    vec = lambda w: pl.BlockSpec((1, w), lambda e: (0, 0))
    hist = pl.BlockSpec((POOL_STATE, db, POOL_WIDTH), lambda e: (0, 0, 0))
    return pl.pallas_call(
        _sample_tail_kernel,
        grid=(N_EXPERTS,),
        in_specs=[
            full(D_MODEL), full(IN_WIDTH), full(Q_WIDTH), full(MEM_WIDTH), hist,
            _const_spec((Q_WIDTH, D_MODEL)), _const_spec((POOL_WIDTH, D_MODEL)),
            _const_spec((MEM_WIDTH, D_MODEL)), _const_spec((D_MODEL, D_MODEL)),
            _const_spec((len(POOL_WINDOWS), POOL_GROUP_DIM, POOL_GROUP_DIM)),
            vec(POOL_WIDTH), vec(D_MODEL), vec(D_MODEL),
            _const_spec((D_MODEL, LANES)), vec(LANES),
            pl.BlockSpec((None, D_MODEL, D_EXPERT), lambda e: (e, 0, 0)),
            pl.BlockSpec((None, D_MODEL, D_EXPERT), lambda e: (e, 0, 0)),
            pl.BlockSpec((None, D_EXPERT, D_MODEL), lambda e: (e, 0, 0)),
            vec(D_MODEL), vec(D_MODEL),
        ],
        out_specs=[full(D_MODEL), hist],
        out_shape=[jax.ShapeDtypeStruct((db, D_MODEL), F32),
                   jax.ShapeDtypeStruct((POOL_STATE, db, POOL_WIDTH), F32)],
        scratch_shapes=[pltpu.VMEM((db, D_MODEL), F32), pltpu.VMEM((db, LANES), F32),
                        pltpu.VMEM((db, D_MODEL), F32)],
        compiler_params=pltpu.CompilerParams(dimension_semantics=("arbitrary",), vmem_limit_bytes=VMEM_LIMIT),
        name="sample_tail",
    )(x, z, oa, oc, state, wa, wb, wc, wo, wmix, pscale, g1, b1, wr, br, wg, wu, wd, g2, b2)


def _rope_tables(pos):
    half = ROPE_DIM // 2
    inv = jnp.power(ROPE_THETA, -jnp.arange(half, dtype=F32) * (2.0 / ROPE_DIM))
    ang = pos.astype(F32)[:, None] * inv[None, :]
    lane = np.arange(LANES)
    off = lane % HEAD_DIM
    cos = jnp.cos(ang)[:, lane % half]
    sin = jnp.sin(ang)[:, lane % half]
    c = jnp.where(off[None, :] < ROPE_DIM, cos, 1.0)
    s1 = jnp.where((off[None, :] >= half) & (off[None, :] < ROPE_DIM), sin, 0.0)
    s2 = jnp.where(off[None, :] < half, -sin, 0.0)
    return jnp.stack([c, s1, s2]).astype(F32)


def _q_heads_group_major(w, axis):
    if axis == 1:
        n = w.shape[0]
        return w.reshape(n, N_KV, GROUP, HEAD_DIM).transpose(0, 2, 1, 3).reshape(n, Q_WIDTH)
    n = w.shape[1]
    return w.reshape(N_KV, GROUP, HEAD_DIM, n).transpose(1, 0, 2, 3).reshape(Q_WIDTH, n)


def kernel(x_prompt, x_sample, cache_win_k, cache_win_v, state_pool, cache_mem_k, cache_mem_v, mem_prompt, w_in, sinks, w_pool_mix, pool_scale, w_mem_k, w_mem_v, w_branch_a, w_branch_b, w_branch_c, w_out, ln1_g, ln1_b, w_group, b_group, w_router, b_router, w_gate, w_up, w_down, ln2_g, ln2_b):
    assert w_in.shape[0] == DEPTH == 1
    b, l, _ = x_prompt.shape
    db, ds, _ = x_sample.shape
    assert ds == 1 and l % TM == 0 and db % SB == 0
    assert cache_win_k.shape[2] == WINDOW
    t = b * l

    win = w_in[0]
    scale = HEAD_DIM ** -0.5
    wq = _q_heads_group_major(win[:, Q0:Q0 + Q_WIDTH], 1) * scale
    wcq = win[:, CQ0:CQ0 + MEM_WIDTH] * scale
    win_b = jnp.concatenate([wq, win[:, K0:CQ0], wcq, win[:, GZ0:]], axis=1).astype(BF16)
    wa = _q_heads_group_major(w_branch_a[0], 0).astype(BF16)
    wb = w_branch_b[0].astype(BF16)
    wc = w_branch_c[0].astype(BF16)
    wo = w_out[0].astype(BF16)
    wmix = w_pool_mix[0].astype(BF16)
    pscale = pool_scale[0].reshape(1, POOL_WIDTH)
    g1 = ln1_g[0].reshape(1, D_MODEL)
    b1 = ln1_b[0].reshape(1, D_MODEL)
    g2 = ln2_g[0].reshape(1, D_MODEL)
    b2 = ln2_b[0].reshape(1, D_MODEL)
    wr = jnp.concatenate([w_group[0], w_router[0].reshape(D_MODEL, N_EXPERTS)], axis=1)
    wr = jnp.pad(wr, ((0, 0), (0, LANES - wr.shape[1]))).astype(BF16)
    br = jnp.pad(jnp.concatenate([b_group[0], b_router[0].reshape(N_EXPERTS)]), (0, LANES - N_EXPERT_GROUPS - N_EXPERTS))
    br = br.reshape(1, LANES).astype(F32)
    wg = w_gate[0]
    wu = w_up[0]
    wd = w_down[0]
    sink = sinks[0].astype(F32)
    sink_gk = jnp.broadcast_to(sink.reshape(N_KV, GROUP).T.reshape(N_HEADS, 1), (N_HEADS, LANES))

    mk, mv = _mem_project(mem_prompt, w_mem_k[0].astype(BF16), w_mem_v[0].astype(BF16))
    rope_p = _rope_tables(jnp.arange(l, dtype=jnp.int32))
    h, xs, route, counts, nk_p, nv_p, npool_p = _front(
        x_prompt, rope_p, sink, win_b, wa, wb, wc, wo, wmix, pscale, mk, mv, g1, b1, wr, br)
    piece_src, piece_dst, tile_expert, tile_active = _piece_tables(
        counts.reshape(-1, SUBLANES, LANES)[:, 0, :N_EXPERTS].astype(jnp.int32))
    ys = _grouped_gemm(piece_src, piece_dst, tile_expert, tile_active, xs, wg, wu, wd)
    y_p = _combine(ys, h.reshape(t, D_MODEL), route.reshape(t, LANES), g2, b2).reshape(b, l, D_MODEL)

    rope_s = _rope_tables(jnp.full((1,), PAST_LEN, jnp.int32))
    xs = x_sample.reshape(db, D_MODEL)
    feat_major = lambda c: jnp.transpose(c[0], (0, 2, 3, 1)).reshape(db, KV_WIDTH, c.shape[2])
    from_feat_major = lambda a: jnp.transpose(a.reshape(db, N_KV, HEAD_DIM, a.shape[2]), (0, 3, 1, 2))[None]
    z_s, oa_s, oc_s, nk_s, nv_s = _sample_attn(xs, rope_s, win_b, sink_gk, feat_major(cache_win_k),
                                               feat_major(cache_win_v), feat_major(cache_mem_k), feat_major(cache_mem_v))
    state = jnp.transpose(state_pool[0], (1, 0, 2))
    y_s, npool_s = _sample_tail(xs, z_s, oa_s, oc_s, state, wa, wb, wc, wo, wmix, pscale, g1, b1, wr, br,
                                wg, wu, wd, g2, b2)

    kv5 = lambda a, n, w: a.reshape(1, n, w, N_KV, HEAD_DIM)
    return (y_p, y_s.reshape(db, 1, D_MODEL),
            kv5(nk_p, b, QB), kv5(nv_p, b, QB),
            npool_p[:, 2 * SUBLANES - POOL_STATE:, :][None],
            kv5(mk, b, N_MEM), kv5(mv, b, N_MEM),
            from_feat_major(nk_s), from_feat_major(nv_s),
            jnp.transpose(npool_s, (1, 0, 2))[None])
```

```python
import functools

import jax
import jax.numpy as jnp
import numpy as np
from jax import lax
from jax.experimental import pallas as pl
from jax.experimental.pallas import tpu as pltpu

D_MODEL = 1024
N_HEADS = 16
HEAD_DIM = 64
N_KV = 4
GROUP = N_HEADS // N_KV
WINDOW = 128
ROPE_THETA = 500000.0
ROPE_DIM = HEAD_DIM // 4
Q_WIDTH = N_HEADS * HEAD_DIM
KV_WIDTH = N_KV * HEAD_DIM
POOL_WINDOWS = (2, 4, 8, 16)
POOL_WIDTH = D_MODEL // 2
POOL_GROUP_DIM = POOL_WIDTH // len(POOL_WINDOWS)
POOL_STATE = max(POOL_WINDOWS) - 1
N_MEM = 256
MEM_HEADS = 4
MEM_WIDTH = MEM_HEADS * HEAD_DIM
N_EXPERT_GROUPS = 4
EXPERTS_PER_GROUP = 4
N_EXPERTS = N_EXPERT_GROUPS * EXPERTS_PER_GROUP
D_EXPERT = 512
PAST_LEN = 16384
DEPTH = 1
ALPHA = (2.0 * DEPTH) ** 0.25
LN_EPS = 1e-5

Q0 = 0
K0 = Q0 + Q_WIDTH
V0 = K0 + KV_WIDTH
U0 = V0 + KV_WIDTH
CQ0 = U0 + POOL_WIDTH
GZ0 = CQ0 + MEM_WIDTH
IN_WIDTH = GZ0 + 3 * D_MODEL

LANES = 128
SUBLANES = 8
VMEM_LIMIT = 56 * 1024 * 1024

TM = 256
QB = WINDOW
PIECE = 16
PIECES_PER_TILE = 16
TG = PIECE * PIECES_PER_TILE
MAX_CHUNK_PIECES = 2 * TM // PIECE + N_EXPERTS - 1
N_SPARE = 2 * PIECES_PER_TILE + 1
IN_SLOTS = 3


def _chunk_rows(n_chunks):
    spare = -(-N_SPARE // n_chunks)
    return -(-(MAX_CHUNK_PIECES + spare) * PIECE // LANES) * LANES
SB = 8
IN_CHUNK = 768
SOFTMAX_ROWS = 64
assert N_MEM == 2 * QB and MEM_HEADS == N_KV and MEM_WIDTH == KV_WIDTH and TM <= GROUP * QB

BF16 = jnp.bfloat16
F32 = jnp.float32
NEG_INF = float("-inf")


def _const_spec(shape):
    nd = len(shape)
    return pl.BlockSpec(shape, lambda *_: (0,) * nd, pipeline_mode=pl.Buffered(1))


def _layer_norm(x, g, b):
    mu = jnp.mean(x, axis=-1, keepdims=True)
    xc = x - mu
    var = jnp.mean(xc * xc, axis=-1, keepdims=True)
    return xc * lax.rsqrt(var + LN_EPS) * g + b


def _dot(a, b):
    return jnp.dot(a, b, preferred_element_type=F32)


def _dot_nt(a, b):
    return lax.dot_general(a, b, (((1,), (1,)), ((), ())), preferred_element_type=F32)


def _lane_block_mask(shape, block, width=HEAD_DIM):
    lane = lax.broadcasted_iota(jnp.int32, shape, len(shape) - 1)
    return (lane >= block * width) & (lane < (block + 1) * width)


def _rope(x, c, s1, s2):
    half = ROPE_DIM // 2
    return x * c + pltpu.roll(x, half, 1) * s1 + pltpu.roll(x, LANES - half, 1) * s2


def _route(logits):
    rows = logits.shape[0]
    lane = lax.broadcasted_iota(jnp.int32, (rows, LANES), 1)
    lanef = lane.astype(F32)
    big = float(LANES)
    is_g = lane < N_EXPERT_GROUPS
    glog = jnp.where(is_g, logits, NEG_INF)
    gmax = jnp.max(glog, axis=1, keepdims=True)
    gsum = jnp.sum(jnp.where(is_g, jnp.exp(glog - gmax), 0.0), axis=1, keepdims=True)
    gp = 1.0 / gsum
    gidx = jnp.min(jnp.where(glog == gmax, lanef, big), axis=1, keepdims=True).astype(jnp.int32)
    lo = N_EXPERT_GROUPS + gidx * EXPERTS_PER_GROUP
    in_grp = (lane >= lo) & (lane < lo + EXPERTS_PER_GROUP)
    el = jnp.where(in_grp, logits, NEG_INF)
    v1 = jnp.max(el, axis=1, keepdims=True)
    i1 = jnp.min(jnp.where(el == v1, lanef, big), axis=1, keepdims=True).astype(jnp.int32)
    el2 = jnp.where(lane == i1, NEG_INF, el)
    v2 = jnp.max(el2, axis=1, keepdims=True)
    i2 = jnp.min(jnp.where(el2 == v2, lanef, big), axis=1, keepdims=True).astype(jnp.int32)
    e21 = jnp.exp(v2 - v1)
    inv = 1.0 / (1.0 + e21)
    w1 = inv * gp
    w2 = e21 * inv * gp
    e1 = i1 - N_EXPERT_GROUPS
    e2 = i2 - N_EXPERT_GROUPS
    return lane == e1, lane == e2, w1, w2


def _local_sort(hot1, hot2, w1, w2, hb, cap):
    rows = hb.shape[0]
    lane = lax.broadcasted_iota(jnp.int32, (rows, LANES), 1)
    onehot = jnp.where(hot1 | hot2, 1.0, 0.0)
    counts = jnp.sum(onehot, axis=0, keepdims=True)
    before = (lax.broadcasted_iota(jnp.int32, (rows, rows), 1)
              < lax.broadcasted_iota(jnp.int32, (rows, rows), 0)).astype(BF16)
    rank = _dot(before, onehot.astype(BF16))
    run = (((counts.astype(jnp.int32) + (PIECE - 1)) // PIECE) * PIECE).astype(F32)
    lower = (lax.broadcasted_iota(jnp.int32, (LANES, LANES), 0)
             < lax.broadcasted_iota(jnp.int32, (LANES, LANES), 1)).astype(BF16)
    start = _dot(jnp.broadcast_to(run, (SUBLANES, LANES)).astype(BF16), lower)[0:1]
    slot = start + rank
    s1 = jnp.sum(jnp.where(hot1, slot, 0.0), axis=1, keepdims=True)
    s2 = jnp.sum(jnp.where(hot2, slot, 0.0), axis=1, keepdims=True)
    route = jnp.where(lane == 0, w1, jnp.where(lane == 1, w2, jnp.where(lane == 2, s1, jnp.where(lane == 3, s2, 0.0))))
    route_t = route.T
    srow = lax.broadcasted_iota(jnp.int32, (cap, rows), 0).astype(F32)
    perm = jnp.where((srow == route_t[2:3, :]) | (srow == route_t[3:4, :]), 1.0, 0.0).astype(BF16)
    return _dot(perm, hb).astype(BF16), route, counts


def _sigmoid(x):
    return 0.5 * jnp.tanh(0.5 * x) + 0.5


def _merge_ln1(x, oa, ob, oc, gz, wa_ref, wb_ref, wc_ref, wo_ref, g1, b1):
    ya = _dot(oa, wa_ref[...])
    yb = _dot(ob.astype(BF16), wb_ref[...])
    yc = _dot(oc.astype(BF16), wc_ref[...])
    m = (_sigmoid(gz[:, 0:D_MODEL]) * ya
         + _sigmoid(gz[:, D_MODEL:2 * D_MODEL]) * yb
         + _sigmoid(gz[:, 2 * D_MODEL:3 * D_MODEL]) * yc)
    hpre = ALPHA * x + _dot(m.astype(BF16), wo_ref[...])
    return _layer_norm(hpre, g1, b1)


def _mem_kernel(mem_ref, wk_ref, wv_ref, mk_ref, mv_ref):
    m = mem_ref[...].astype(BF16)
    mk_ref[...] = _dot(m, wk_ref[...])
    mv_ref[...] = _dot(m, wv_ref[...])


def _mem_project(mem, wk, wv):
    b = mem.shape[0]
    out = jax.ShapeDtypeStruct((b, N_MEM, MEM_WIDTH), F32)
    return pl.pallas_call(
        _mem_kernel,
        grid=(b,),
        in_specs=[pl.BlockSpec((None, N_MEM, D_MODEL), lambda i: (i, 0, 0)),
                  _const_spec((D_MODEL, MEM_WIDTH)), _const_spec((D_MODEL, MEM_WIDTH))],
        out_specs=[pl.BlockSpec((None, N_MEM, MEM_WIDTH), lambda i: (i, 0, 0))] * 2,
        out_shape=[out, out],
        name="mem_project",
    )(mem, wk, wv)


def _front_kernel(sinks_ref, x_ref, rope_ref, win_ref, wa_ref, wb_ref, wc_ref, wo_ref, wmix_ref,
                  pscale_ref, mk_ref, mv_ref, g1_ref, b1_ref, wr_ref, br_ref,
                  h_ref, xs_ref, route_ref, counts_ref, nk_ref, nv_ref, npool_ref,
                  z_ref, qb_ref, kext_ref, vext_ref, uext_ref, oa_ref, ob_ref, bias_ref, s_ref, p_ref, vblk_ref):
    i = pl.program_id(1)
    x = x_ref[...]
    xb = x.astype(BF16)
    hist = 2 * SUBLANES

    @pl.when(i == 0)
    def _():
        kext_ref[0:QB, :] = jnp.zeros((QB, KV_WIDTH), BF16)
        vext_ref[0:QB, :] = jnp.zeros((QB, KV_WIDTH), BF16)
        uext_ref[0:hist, :] = jnp.zeros((hist, POOL_WIDTH), F32)

    @pl.when(i > 0)
    def _():
        kext_ref[0:QB, :] = kext_ref[TM:TM + QB, :]
        vext_ref[0:QB, :] = vext_ref[TM:TM + QB, :]
        uext_ref[0:hist, :] = uext_ref[TM:TM + hist, :]

    for c0 in range(0, IN_WIDTH, IN_CHUNK):
        z_ref[:, c0:c0 + IN_CHUNK] = _dot(xb, win_ref[:, c0:c0 + IN_CHUNK])

    c = rope_ref[0]
    s1 = rope_ref[1]
    s2 = rope_ref[2]
    for j in range(Q_WIDTH // LANES):
        sl = slice(Q0 + j * LANES, Q0 + (j + 1) * LANES)
        qb_ref[:, j * LANES:(j + 1) * LANES] = _rope(z_ref[:, sl], c, s1, s2).astype(BF16)
    for j in range(KV_WIDTH // LANES):
        sl = slice(K0 + j * LANES, K0 + (j + 1) * LANES)
        kr = _rope(z_ref[:, sl], c, s1, s2)
        z_ref[:, sl] = kr
        kext_ref[QB:QB + TM, j * LANES:(j + 1) * LANES] = kr.astype(BF16)
    vext_ref[QB:QB + TM, :] = z_ref[:, V0:V0 + KV_WIDTH].astype(BF16)
    uext_ref[hist:hist + TM, :] = z_ref[:, U0:U0 + POOL_WIDTH]
    nk_ref[...] = z_ref[TM - QB:TM, K0:K0 + KV_WIDTH]
    nv_ref[...] = z_ref[TM - QB:TM, V0:V0 + KV_WIDTH]

    rowq = lax.broadcasted_iota(jnp.int32, (QB, 2 * QB), 0)
    colk = lax.broadcasted_iota(jnp.int32, (QB, 2 * QB), 1)
    band = (colk >= rowq) & (colk <= rowq + WINDOW)
    bias_ref[1] = jnp.where(band, 0.0, NEG_INF)
    bias_ref[0] = jnp.where(band & ((colk >= QB) | (i > 0)), 0.0, NEG_INF)
    for sb in range(TM // QB):
        k2 = kext_ref[sb * QB:(sb + 2) * QB, :]
        v2 = vext_ref[sb * QB:(sb + 2) * QB, :]
        qs = jnp.concatenate(
            [qb_ref[sb * QB:(sb + 1) * QB, g * KV_WIDTH:(g + 1) * KV_WIDTH] for g in range(GROUP)], axis=0)
        for kv in range(N_KV):
            kmask = _lane_block_mask((2 * QB, KV_WIDTH), kv)
            s_ref[...] = _dot_nt(qs, jnp.where(kmask, k2, jnp.zeros_like(k2)))
            vblk_ref[kv * 2 * QB:(kv + 1) * 2 * QB, :] = jnp.where(kmask, v2, jnp.zeros_like(v2))
            for c0 in range(0, GROUP * QB, SOFTMAX_ROWS):
                rq = c0 % QB
                sink = sinks_ref[kv * GROUP + c0 // QB]
                s = s_ref[c0:c0 + SOFTMAX_ROWS, :] + bias_ref[min(sb, 1), rq:rq + SOFTMAX_ROWS, :]
                m = jnp.maximum(jnp.max(s, axis=1, keepdims=True), sink)
                p = jnp.exp(s - m)
                den = jnp.sum(p, axis=1, keepdims=True) + jnp.exp(sink - m)
                p_ref[c0:c0 + SOFTMAX_ROWS, kv * 2 * QB:(kv + 1) * 2 * QB] = (p * (1.0 / den)).astype(BF16)
        o = _dot(p_ref[...], vblk_ref[...])
        for g in range(GROUP):
            oa_ref[sb * QB:(sb + 1) * QB, g * KV_WIDTH:(g + 1) * KV_WIDTH] = o[g * QB:(g + 1) * QB].astype(BF16)

    npool_ref[...] = uext_ref[TM:TM + hist, :]
    pos = i * TM + lax.broadcasted_iota(jnp.int32, (TM, 1), 0)
    for g, w in enumerate(POOL_WINDOWS):
        sl = slice(g * POOL_GROUP_DIM, (g + 1) * POOL_GROUP_DIM)
        cur = uext_ref[hist:hist + TM, sl]
        ws = cur
        for j in range(1, w):
            ws = ws + uext_ref[hist - j:hist - j + TM, sl]
        cnt = jnp.minimum(pos + 1, w).astype(F32)
        pooled = ws / cnt - cur
        ob_ref[:, sl] = _dot(pooled.astype(BF16), wmix_ref[g]) * pscale_ref[:, sl]

    cq = z_ref[:, CQ0:CQ0 + MEM_WIDTH].astype(BF16)
    mk = mk_ref[...].astype(BF16)
    mv = mv_ref[...].astype(BF16)
    for hh in range(MEM_HEADS):
        hmask = _lane_block_mask((N_MEM, MEM_WIDTH), hh)
        s_ref[0:TM, :] = _dot_nt(cq, jnp.where(hmask, mk, jnp.zeros_like(mk)))
        vblk_ref[hh * N_MEM:(hh + 1) * N_MEM, :] = jnp.where(hmask, mv, jnp.zeros_like(mv))
        for c0 in range(0, TM, SOFTMAX_ROWS):
            s = s_ref[c0:c0 + SOFTMAX_ROWS, :]
            p = jnp.exp(s - jnp.max(s, axis=1, keepdims=True))
            den = jnp.sum(p, axis=1, keepdims=True)
            p_ref[c0:c0 + SOFTMAX_ROWS, hh * N_MEM:(hh + 1) * N_MEM] = (p * (1.0 / den)).astype(BF16)
    oc = _dot(p_ref[0:TM, :], vblk_ref[...])

    h = _merge_ln1(x, oa_ref[...], ob_ref[...], oc, z_ref[:, GZ0:GZ0 + 3 * D_MODEL],
                   wa_ref, wb_ref, wc_ref, wo_ref, g1_ref[...], b1_ref[...])
    h_ref[...] = h
    hb = h.astype(BF16)
    logits = _dot(hb, wr_ref[...]) + br_ref[...]
    xs, route, counts = _local_sort(*_route(logits), hb, xs_ref.shape[0])
    xs_ref[...] = xs
    route_ref[...] = route
    counts_ref[...] = jnp.broadcast_to(counts, (SUBLANES, LANES))


def _front(x, rope, sinks, win, wa, wb, wc, wo, wmix, pscale, mk, mv, g1, b1, wr, br):
    b, l, _ = x.shape
    nt = l // TM
    cap = _chunk_rows(b * nt)
    hist = 2 * SUBLANES
    tile = lambda w: pl.BlockSpec((None, TM, w), lambda bi, ti: (bi, ti, 0))
    per_b = lambda r, w: pl.BlockSpec((None, r, w), lambda bi, ti: (bi, 0, 0))
    return pl.pallas_call(
        _front_kernel,
        grid=(b, nt),
        in_specs=[
            pl.BlockSpec(memory_space=pltpu.SMEM),
            tile(D_MODEL),
            pl.BlockSpec((3, TM, LANES), lambda bi, ti: (0, ti, 0)),
            _const_spec((D_MODEL, IN_WIDTH)),
            _const_spec((Q_WIDTH, D_MODEL)), _const_spec((POOL_WIDTH, D_MODEL)),
            _const_spec((MEM_WIDTH, D_MODEL)), _const_spec((D_MODEL, D_MODEL)),
            _const_spec((len(POOL_WINDOWS), POOL_GROUP_DIM, POOL_GROUP_DIM)),
            _const_spec((1, POOL_WIDTH)),
            per_b(N_MEM, MEM_WIDTH), per_b(N_MEM, MEM_WIDTH),
            _const_spec((1, D_MODEL)), _const_spec((1, D_MODEL)),
            _const_spec((D_MODEL, LANES)), _const_spec((1, LANES)),
        ],
        out_specs=[
            tile(D_MODEL),
            pl.BlockSpec((cap, D_MODEL), lambda bi, ti: (bi * nt + ti, 0)),
            tile(LANES),
            pl.BlockSpec((None, None, SUBLANES, LANES), lambda bi, ti: (bi, ti, 0, 0)),
            per_b(QB, KV_WIDTH), per_b(QB, KV_WIDTH), per_b(hist, POOL_WIDTH),
        ],
        out_shape=[
            jax.ShapeDtypeStruct((b, l, D_MODEL), F32),
            jax.ShapeDtypeStruct((b * nt * cap, D_MODEL), BF16),
            jax.ShapeDtypeStruct((b, l, LANES), F32),
            jax.ShapeDtypeStruct((b, nt, SUBLANES, LANES), F32),
            jax.ShapeDtypeStruct((b, QB, KV_WIDTH), F32),
            jax.ShapeDtypeStruct((b, QB, KV_WIDTH), F32),
            jax.ShapeDtypeStruct((b, hist, POOL_WIDTH), F32),
        ],
        scratch_shapes=[
            pltpu.VMEM((TM, IN_WIDTH), F32),
            pltpu.VMEM((TM, Q_WIDTH), BF16),
            pltpu.VMEM((QB + TM, KV_WIDTH), BF16),
            pltpu.VMEM((QB + TM, KV_WIDTH), BF16),
            pltpu.VMEM((hist + TM, POOL_WIDTH), F32),
            pltpu.VMEM((TM, Q_WIDTH), BF16),
            pltpu.VMEM((TM, POOL_WIDTH), F32),
            pltpu.VMEM((2, QB, 2 * QB), F32),
            pltpu.VMEM((GROUP * QB, 2 * QB), F32),
            pltpu.VMEM((GROUP * QB, N_KV * 2 * QB), BF16),
            pltpu.VMEM((N_KV * 2 * QB, KV_WIDTH), BF16),
        ],
        compiler_params=pltpu.CompilerParams(
            dimension_semantics=("arbitrary", "arbitrary"), vmem_limit_bytes=VMEM_LIMIT),
        name="front_prompt",
    )(sinks, x, rope, win, wa, wb, wc, wo, wmix, pscale, mk, mv, g1, b1, wr, br)


def _expert_mlp(xb, wg, wu, wd):
    a = _dot(xb, wg)
    hid = (a * jax.nn.sigmoid(a)) * _dot(xb, wu)
    return _dot(hid.astype(BF16), wd)


def _gemm_kernel(src_ref, dst_ref, te_ref, act_ref, xs_ref, wg_ref, wu_ref, wd_ref, ys_ref,
                 xbuf, obuf, wgb, wub, wdb, prime, sem_in, sem_out):
    i = pl.program_id(0)
    n = pl.num_programs(0)
    slot = i % 2
    in_slot = i % IN_SLOTS

    def start_in(tile, slot):
        for j in range(PIECES_PER_TILE):
            row0 = pl.multiple_of(src_ref[tile * PIECES_PER_TILE + j], PIECE)
            pltpu.make_async_copy(xs_ref.at[pl.ds(row0, PIECE)], xbuf.at[slot, pl.ds(j * PIECE, PIECE)],
                                  sem_in.at[slot]).start()

    def start_out(tile, slot):
        for j in range(PIECES_PER_TILE):
            row0 = pl.multiple_of(dst_ref[tile * PIECES_PER_TILE + j], PIECE)
            pltpu.make_async_copy(obuf.at[slot, pl.ds(j * PIECE, PIECE)], ys_ref.at[pl.ds(row0, PIECE)],
                                  sem_out.at[slot]).start()

    def wait_in(slot):
        for j in range(PIECES_PER_TILE):
            pltpu.make_async_copy(xs_ref.at[pl.ds(0, PIECE)], xbuf.at[slot, pl.ds(j * PIECE, PIECE)],
                                  sem_in.at[slot]).wait()

    def wait_out(slot):
        for j in range(PIECES_PER_TILE):
            pltpu.make_async_copy(obuf.at[slot, pl.ds(j * PIECE, PIECE)], ys_ref.at[pl.ds(0, PIECE)],
                                  sem_out.at[slot]).wait()

    active = act_ref[i] > 0
    ahead = IN_SLOTS - 1
    prefetched = (i < ahead) | (act_ref[jnp.maximum(i - ahead, 0)] > 0)
    out_pending = (i < 2) | (act_ref[jnp.maximum(i - 2, 0)] > 0)

    @pl.when(i == 0)
    def _():
        prime[0] = jnp.zeros((PIECE, D_MODEL), BF16)
        for s in range(2):
            for j in range(PIECES_PER_TILE):
                pltpu.make_async_copy(prime.at[0], prime.at[1 + s * PIECES_PER_TILE + j], sem_out.at[s]).start()
        for t in range(ahead):
            start_in(t, t)

    @pl.when(active)
    def _():
        start_in(i + ahead, (i + ahead) % IN_SLOTS)

        @pl.when((i == 0) | (te_ref[i] != te_ref[jnp.maximum(i - 1, 0)]))
        def _():
            wgb[...] = wg_ref[...].astype(BF16)
            wub[...] = wu_ref[...].astype(BF16)
            wdb[...] = wd_ref[...].astype(BF16)

        wait_in(in_slot)
        wait_out(slot)
        obuf[slot] = _expert_mlp(xbuf[in_slot], wgb[...], wub[...], wdb[...]).astype(BF16)
        start_out(i, slot)

    @pl.when(jnp.logical_not(active))
    def _():
        @pl.when(prefetched)
        def _():
            wait_in(in_slot)

        @pl.when(out_pending)
        def _():
            wait_out(slot)

    @pl.when(i == n - 1)
    def _():
        for t in range(ahead):
            @pl.when(act_ref[jnp.maximum(i - t, 0)] > 0)
            def _(t=t):
                wait_in((i - t + ahead) % IN_SLOTS)

        @pl.when(active)
        def _():
            wait_out(slot)

        @pl.when((i >= 1) & (act_ref[jnp.maximum(i - 1, 0)] > 0))
        def _():
            wait_out(1 - slot)


def _grouped_gemm(piece_src, piece_dst, tile_expert, tile_active, xs, wg, wu, wd):
    n_tiles = tile_expert.shape[0]
    assert n_tiles >= 2
    wspec = lambda r, c: pl.BlockSpec((None, r, c), lambda i, src, dst, te, act: (te[i], 0, 0))
    return pl.pallas_call(
        _gemm_kernel,
        grid_spec=pltpu.PrefetchScalarGridSpec(
            num_scalar_prefetch=4,
            grid=(n_tiles,),
            in_specs=[pl.BlockSpec(memory_space=pl.ANY),
                      wspec(D_MODEL, D_EXPERT), wspec(D_MODEL, D_EXPERT), wspec(D_EXPERT, D_MODEL)],
            out_specs=pl.BlockSpec(memory_space=pl.ANY),
            scratch_shapes=[pltpu.VMEM((IN_SLOTS, TG, D_MODEL), BF16), pltpu.VMEM((2, TG, D_MODEL), BF16),
                            pltpu.VMEM((D_MODEL, D_EXPERT), BF16), pltpu.VMEM((D_MODEL, D_EXPERT), BF16),
                            pltpu.VMEM((D_EXPERT, D_MODEL), BF16),
                            pltpu.VMEM((1 + 2 * PIECES_PER_TILE, PIECE, D_MODEL), BF16),
                            pltpu.SemaphoreType.DMA((IN_SLOTS,)), pltpu.SemaphoreType.DMA((2,))],
        ),
        out_shape=jax.ShapeDtypeStruct(xs.shape, xs.dtype),
        input_output_aliases={4: 0},
        compiler_params=pltpu.CompilerParams(dimension_semantics=("arbitrary",), vmem_limit_bytes=VMEM_LIMIT),
        name="moe_grouped_gemm",
    )(piece_src, piece_dst, tile_expert, tile_active, xs, wg, wu, wd)


def _combine_kernel(ys_ref, h_ref, route_ref, g2_ref, b2_ref, y_ref):
    chunks = h_ref.shape[0] // TM
    cap = ys_ref.shape[0] // chunks
    slot = lax.broadcasted_iota(jnp.int32, (TM, cap), 1).astype(F32)
    for c in range(chunks):
        rows = slice(c * TM, (c + 1) * TM)
        route = route_ref[rows, :]
        sel = jnp.concatenate([jnp.where(slot == route[:, 2:3], 1.0, 0.0).astype(BF16),
                               jnp.where(slot == route[:, 3:4], 1.0, 0.0).astype(BF16)], axis=0)
        picked = _dot(sel, ys_ref[c * cap:(c + 1) * cap, :])
        f = route[:, 0:1] * picked[0:TM] + route[:, 1:2] * picked[TM:2 * TM]
        y_ref[rows, :] = _layer_norm(ALPHA * h_ref[rows, :] + f, g2_ref[...], b2_ref[...])


def _combine(ys, h, route, g2, b2):
    t = h.shape[0]
    cap = ys.shape[0] // (t // TM)
    per_step = 2 if (t // TM) % 2 == 0 else 1
    rows = per_step * TM
    return pl.pallas_call(
        _combine_kernel,
        grid=(t // rows,),
        in_specs=[
            pl.BlockSpec((per_step * cap, D_MODEL), lambda i: (i, 0)),
            pl.BlockSpec((rows, D_MODEL), lambda i: (i, 0)),
            pl.BlockSpec((rows, LANES), lambda i: (i, 0)),
            pl.BlockSpec((1, D_MODEL), lambda i: (0, 0)),
            pl.BlockSpec((1, D_MODEL), lambda i: (0, 0)),
        ],
        out_specs=pl.BlockSpec((rows, D_MODEL), lambda i: (i, 0)),
        out_shape=jax.ShapeDtypeStruct((t, D_MODEL), F32),
        compiler_params=pltpu.CompilerParams(dimension_semantics=("arbitrary",)),
        name="moe_combine",
    )(ys, h, route, g2, b2)


def _select(table, idx):
    hot = idx[:, None] == jnp.arange(table.shape[0], dtype=jnp.int32)[None, :]
    return jnp.sum(jnp.where(hot[:, :, None], table[None, :, :], 0), axis=1)


def _piece_tables(counts):
    n_chunks = counts.shape[0]
    n_tiles = -(-(n_chunks * MAX_CHUNK_PIECES + N_EXPERTS * (PIECES_PER_TILE - 1)) // PIECES_PER_TILE)
    npc = (counts + (PIECE - 1)) // PIECE
    first = (jnp.cumsum(npc, axis=1) - npc).T
    npc_t = npc.T
    cum = jnp.cumsum(npc_t, axis=1)
    per_expert = cum[:, -1]
    tiles_e = (per_expert + (PIECES_PER_TILE - 1)) // PIECES_PER_TILE
    tile_end = jnp.cumsum(tiles_e)
    tile_idx = jnp.arange(n_tiles, dtype=jnp.int32)
    expert_of = lambda i: jnp.minimum(jnp.sum(i[:, None] >= tile_end[None, :], axis=1), N_EXPERTS - 1).astype(jnp.int32)
    active = tile_idx < tile_end[-1]
    tile_expert = jnp.where(active, expert_of(tile_idx), expert_of(tile_end[-1:] - 1))
    meta = jnp.stack([tile_end - tiles_e, per_expert], axis=1)
    meta_t = _select(meta, tile_expert)
    k = (tile_idx - meta_t[:, 0])[:, None] * PIECES_PER_TILE + jnp.arange(PIECES_PER_TILE, dtype=jnp.int32)[None, :]
    valid = active[:, None] & (k < meta_t[:, 1:2])
    cum_t = _select(cum, tile_expert)
    chunk = jnp.minimum(jnp.sum(k[:, :, None] >= cum_t[:, None, :], axis=2), n_chunks - 1).astype(jnp.int32)
    at_chunk = chunk[:, :, None] == jnp.arange(n_chunks, dtype=jnp.int32)[None, None, :]
    pick = lambda tab: jnp.sum(jnp.where(at_chunk, _select(tab, tile_expert)[:, None, :], 0), axis=2)
    piece = pick(first) + k - pick(cum - npc_t)
    cap = _chunk_rows(n_chunks)
    rows = chunk * cap + piece * PIECE
    d = (tile_idx % 2)[:, None] * PIECES_PER_TILE + jnp.arange(PIECES_PER_TILE, dtype=jnp.int32)[None, :]
    spare_row = lambda d: (d % n_chunks) * cap + (MAX_CHUNK_PIECES + d // n_chunks) * PIECE
    spare = spare_row(d)
    zero_piece = spare_row(N_SPARE - 1)
    extra = jnp.full(((IN_SLOTS - 1) * PIECES_PER_TILE,), zero_piece, jnp.int32)
    src = jnp.concatenate([jnp.where(valid, rows, zero_piece).astype(jnp.int32).reshape(-1), extra])
    dst = jnp.concatenate([jnp.where(valid, rows, spare).astype(jnp.int32).reshape(-1), extra])
    return src, dst, tile_expert.astype(jnp.int32), active.astype(jnp.int32)


def _sample_attn_kernel(x_ref, rope_ref, win_ref, sink_ref, ck_ref, cv_ref, cmk_ref, cmv_ref,
                        z_ref, oa_ref, oc_ref, nk_ref, nv_ref, knew_t, vnew_t):
    j = pl.program_id(0)
    db = x_ref.shape[0]

    @pl.when(j == 0)
    def _():
        xb = x_ref[...].astype(BF16)
        for c0 in range(0, IN_WIDTH, IN_CHUNK):
            z_ref[:, c0:c0 + IN_CHUNK] = _dot(xb, win_ref[:, c0:c0 + IN_CHUNK])
        c = rope_ref[0]
        s1 = rope_ref[1]
        s2 = rope_ref[2]
        for jj in range((Q_WIDTH + KV_WIDTH) // LANES):
            sl = slice(jj * LANES, (jj + 1) * LANES)
            z_ref[:, sl] = _rope(z_ref[:, sl], c, s1, s2)
        knew_t[...] = z_ref[:, K0:K0 + KV_WIDTH].T
        vnew_t[...] = z_ref[:, V0:V0 + KV_WIDTH].T

    r0 = pl.multiple_of(j * SB, SB)
    zq = z_ref[pl.ds(r0, SB), Q0:Q0 + Q_WIDTH]
    zk = z_ref[pl.ds(r0, SB), K0:K0 + KV_WIDTH]
    zv = z_ref[pl.ds(r0, SB), V0:V0 + KV_WIDTH]
    zc = z_ref[pl.ds(r0, SB), CQ0:CQ0 + MEM_WIDTH]
    sink = sink_ref[:, 0:1]
    row_kv = lax.broadcasted_iota(jnp.int32, (N_HEADS, KV_WIDTH), 0) & (N_KV - 1)
    lane_kv = lax.broadcasted_iota(jnp.int32, (N_HEADS, KV_WIDTH), 1) // HEAD_DIM
    own = row_kv == lane_kv
    row_c = lax.broadcasted_iota(jnp.int32, (N_HEADS, MEM_WIDTH), 0)
    lane_c = lax.broadcasted_iota(jnp.int32, (N_HEADS, MEM_WIDTH), 1) // HEAD_DIM
    own_c = row_c == lane_c
    last_pos = lax.broadcasted_iota(jnp.int32, (KV_WIDTH, WINDOW), 1) == WINDOW - 1
    seq_lane = lax.broadcasted_iota(jnp.int32, (KV_WIDTH, db), 1)

    seqs = range(SB)
    qblk, cblk, s, sc = [], [], [], []
    for b in seqs:
        q4 = jnp.concatenate(
            [jnp.broadcast_to(zq[b:b + 1, g * KV_WIDTH:(g + 1) * KV_WIDTH], (N_KV, KV_WIDTH)) for g in range(GROUP)],
            axis=0)
        qblk.append(jnp.where(own, q4, 0.0).astype(BF16))
        cblk.append(jnp.where(own_c, jnp.broadcast_to(zc[b:b + 1, :], (N_HEADS, MEM_WIDTH)), 0.0).astype(BF16))
        s.append(_dot(qblk[b], ck_ref[b].astype(BF16)))
        sc.append(_dot(cblk[b], cmk_ref[b].astype(BF16)))

    pn, p_new, pc = [], [], []
    for b in seqs:
        s_new = jnp.sum(qblk[b].astype(F32) * zk[b:b + 1, :].astype(BF16).astype(F32), axis=1, keepdims=True)
        m = jnp.maximum(jnp.maximum(jnp.max(s[b], axis=1, keepdims=True), s_new), sink)
        p = jnp.exp(s[b] - m)
        e_new = jnp.exp(s_new - m)
        inv = 1.0 / (jnp.sum(p, axis=1, keepdims=True) + e_new + jnp.exp(sink - m))
        pn.append((p * inv).astype(BF16))
        p_new.append((e_new * inv).astype(BF16).astype(F32))
        e = jnp.exp(sc[b] - jnp.max(sc[b], axis=1, keepdims=True))
        pc.append((e * (1.0 / jnp.sum(e, axis=1, keepdims=True))).astype(BF16))

    oa_rows = [[] for _ in range(GROUP)]
    oc_rows = []
    for b in seqs:
        vc = cv_ref[b]
        o = _dot_nt(pn[b], vc.astype(BF16)) + p_new[b] * zv[b:b + 1, :].astype(BF16).astype(F32)
        o = jnp.where(own, o, 0.0)
        for g in range(GROUP):
            oa_rows[g].append(jnp.sum(o[g * N_KV:(g + 1) * N_KV], axis=0, keepdims=True))
        ocb = jnp.where(own_c, _dot_nt(pc[b], cmv_ref[b].astype(BF16)), 0.0)
        oc_rows.append(jnp.sum(ocb, axis=0, keepdims=True))
        here = seq_lane == j * SB + b
        knew_col = jnp.sum(jnp.where(here, knew_t[...], 0.0), axis=1, keepdims=True)
        vnew_col = jnp.sum(jnp.where(here, vnew_t[...], 0.0), axis=1, keepdims=True)
        nk_ref[b] = jnp.where(last_pos, knew_col, pltpu.roll(ck_ref[b], WINDOW - 1, 1))
        nv_ref[b] = jnp.where(last_pos, vnew_col, pltpu.roll(vc, WINDOW - 1, 1))
    for g in range(GROUP):
        oa_ref[pl.ds(r0, SB), g * KV_WIDTH:(g + 1) * KV_WIDTH] = jnp.concatenate(oa_rows[g], axis=0)
    oc_ref[pl.ds(r0, SB), :] = jnp.concatenate(oc_rows, axis=0)


def _sample_attn(x, rope, win, sink_gk, ck, cv, cmk, cmv):
    db = x.shape[0]
    blk = lambda r: pl.BlockSpec((SB, KV_WIDTH, r), lambda j: (j, 0, 0))
    full = lambda w: pl.BlockSpec((db, w), lambda j: (0, 0))
    return pl.pallas_call(
        _sample_attn_kernel,
        grid=(db // SB,),
        in_specs=[
            full(D_MODEL), _const_spec((3, 1, LANES)), _const_spec((D_MODEL, IN_WIDTH)),
            _const_spec((N_HEADS, LANES)),
            blk(WINDOW), blk(WINDOW), blk(N_MEM), blk(N_MEM),
        ],
        out_specs=[full(IN_WIDTH), full(Q_WIDTH), full(MEM_WIDTH), blk(WINDOW), blk(WINDOW)],
        out_shape=[
            jax.ShapeDtypeStruct((db, IN_WIDTH), F32),
            jax.ShapeDtypeStruct((db, Q_WIDTH), F32),
            jax.ShapeDtypeStruct((db, MEM_WIDTH), F32),
            jax.ShapeDtypeStruct((db, KV_WIDTH, WINDOW), F32),
            jax.ShapeDtypeStruct((db, KV_WIDTH, WINDOW), F32),
        ],
        scratch_shapes=[pltpu.VMEM((KV_WIDTH, db), F32), pltpu.VMEM((KV_WIDTH, db), F32)],
        compiler_params=pltpu.CompilerParams(dimension_semantics=("arbitrary",), vmem_limit_bytes=VMEM_LIMIT),
        name="sample_attn",
    )(x, rope, win, sink_gk, ck, cv, cmk, cmv)


def _sample_tail_kernel(x_ref, z_ref, oa_ref, oc_ref, st_ref, wa_ref, wb_ref, wc_ref, wo_ref, wmix_ref,
                        pscale_ref, g1_ref, b1_ref, wr_ref, br_ref, wg_ref, wu_ref, wd_ref, g2_ref, b2_ref,
                        y_ref, npool_ref, h_sc, comb_sc, acc_sc):
    e = pl.program_id(0)

    @pl.when(e == 0)
    def _():
        u = z_ref[:, U0:U0 + POOL_WIDTH]
        npool_ref[0:POOL_STATE - 1] = st_ref[1:POOL_STATE]
        npool_ref[POOL_STATE - 1] = u
        obs = []
        for g, w in enumerate(POOL_WINDOWS):
            sl = slice(g * POOL_GROUP_DIM, (g + 1) * POOL_GROUP_DIM)
            cur = u[:, sl]
            ws = cur
            for jj in range(1, w):
                ws = ws + st_ref[POOL_STATE - jj, :, sl]
            cnt = float(min(PAST_LEN + 1, w))
            pooled = ws / cnt - cur
            obs.append(_dot(pooled.astype(BF16), wmix_ref[g]) * pscale_ref[:, sl])
        ob = jnp.concatenate(obs, axis=1)
        h = _merge_ln1(x_ref[...], oa_ref[...].astype(BF16), ob, oc_ref[...], z_ref[:, GZ0:GZ0 + 3 * D_MODEL],
                       wa_ref, wb_ref, wc_ref, wo_ref, g1_ref[...], b1_ref[...])
        h_sc[...] = h
        logits = _dot(h.astype(BF16), wr_ref[...]) + br_ref[...]
        hot1, hot2, w1, w2 = _route(logits)
        comb_sc[...] = jnp.where(hot1, w1, 0.0) + jnp.where(hot2, w2, 0.0)
        acc_sc[...] = jnp.zeros_like(acc_sc)

    out = _expert_mlp(h_sc[...].astype(BF16), wg_ref[...].astype(BF16), wu_ref[...].astype(BF16),
                      wd_ref[...].astype(BF16))
    lane = lax.broadcasted_iota(jnp.int32, comb_sc.shape, 1)
    ce = jnp.sum(jnp.where(lane == e, comb_sc[...], 0.0), axis=1, keepdims=True)
    acc_sc[...] += ce * out

    @pl.when(e == pl.num_programs(0) - 1)
    def _():
        y_ref[...] = _layer_norm(ALPHA * h_sc[...] + acc_sc[...], g2_ref[...], b2_ref[...])


def _sample_tail(x, z, oa, oc, state, wa, wb, wc, wo, wmix, pscale, g1, b1, wr, br, wg, wu, wd, g2, b2):
    db = x.shape[0]
    full = lambda w: pl.BlockSpec((db, w), lambda e: (0, 0))
    vec = lambda w: pl.BlockSpec((1, w), lambda e: (0, 0))
    hist = pl.BlockSpec((POOL_STATE, db, POOL_WIDTH), lambda e: (0, 0, 0))
    return pl.pallas_call(
        _sample_tail_kernel,
        grid=(N_EXPERTS,),
        in_specs=[
            full(D_MODEL), full(IN_WIDTH), full(Q_WIDTH), full(MEM_WIDTH), hist,
            _const_spec((Q_WIDTH, D_MODEL)), _const_spec((POOL_WIDTH, D_MODEL)),
            _const_spec((MEM_WIDTH, D_MODEL)), _const_spec((D_MODEL, D_MODEL)),
            _const_spec((len(POOL_WINDOWS), POOL_GROUP_DIM, POOL_GROUP_DIM)),
            vec(POOL_WIDTH), vec(D_MODEL), vec(D_MODEL),
            _const_spec((D_MODEL, LANES)), vec(LANES),
            pl.BlockSpec((None, D_MODEL, D_EXPERT), lambda e: (e, 0, 0)),
            pl.BlockSpec((None, D_MODEL, D_EXPERT), lambda e: (e, 0, 0)),
            pl.BlockSpec((None, D_EXPERT, D_MODEL), lambda e: (e, 0, 0)),
            vec(D_MODEL), vec(D_MODEL),
        ],
        out_specs=[full(D_MODEL), hist],
        out_shape=[jax.ShapeDtypeStruct((db, D_MODEL), F32),
                   jax.ShapeDtypeStruct((POOL_STATE, db, POOL_WIDTH), F32)],
        scratch_shapes=[pltpu.VMEM((db, D_MODEL), F32), pltpu.VMEM((db, LANES), F32),
                        pltpu.VMEM((db, D_MODEL), F32)],
        compiler_params=pltpu.CompilerParams(dimension_semantics=("arbitrary",), vmem_limit_bytes=VMEM_LIMIT),
        name="sample_tail",
    )(x, z, oa, oc, state, wa, wb, wc, wo, wmix, pscale, g1, b1, wr, br, wg, wu, wd, g2, b2)


def _rope_tables(pos):
    half = ROPE_DIM // 2
    inv = jnp.power(ROPE_THETA, -jnp.arange(half, dtype=F32) * (2.0 / ROPE_DIM))
    ang = pos.astype(F32)[:, None] * inv[None, :]
    lane = np.arange(LANES)
    off = lane % HEAD_DIM
    cos = jnp.cos(ang)[:, lane % half]
    sin = jnp.sin(ang)[:, lane % half]
    c = jnp.where(off[None, :] < ROPE_DIM, cos, 1.0)
    s1 = jnp.where((off[None, :] >= half) & (off[None, :] < ROPE_DIM), sin, 0.0)
    s2 = jnp.where(off[None, :] < half, -sin, 0.0)
    return jnp.stack([c, s1, s2]).astype(F32)


def _q_heads_group_major(w, axis):
    if axis == 1:
        n = w.shape[0]
        return w.reshape(n, N_KV, GROUP, HEAD_DIM).transpose(0, 2, 1, 3).reshape(n, Q_WIDTH)
    n = w.shape[1]
    return w.reshape(N_KV, GROUP, HEAD_DIM, n).transpose(1, 0, 2, 3).reshape(Q_WIDTH, n)


def kernel(x_prompt, x_sample, cache_win_k, cache_win_v, state_pool, cache_mem_k, cache_mem_v, mem_prompt, w_in, sinks, w_pool_mix, pool_scale, w_mem_k, w_mem_v, w_branch_a, w_branch_b, w_branch_c, w_out, ln1_g, ln1_b, w_group, b_group, w_router, b_router, w_gate, w_up, w_down, ln2_g, ln2_b):
    assert w_in.shape[0] == DEPTH == 1
    b, l, _ = x_prompt.shape
    db, ds, _ = x_sample.shape
    assert ds == 1 and l % TM == 0 and db % SB == 0
    assert cache_win_k.shape[2] == WINDOW
    t = b * l

    win = w_in[0]
    scale = HEAD_DIM ** -0.5
    wq = _q_heads_group_major(win[:, Q0:Q0 + Q_WIDTH], 1) * scale
    wcq = win[:, CQ0:CQ0 + MEM_WIDTH] * scale
    win_b = jnp.concatenate([wq, win[:, K0:CQ0], wcq, win[:, GZ0:]], axis=1).astype(BF16)
    wa = _q_heads_group_major(w_branch_a[0], 0).astype(BF16)
    wb = w_branch_b[0].astype(BF16)
    wc = w_branch_c[0].astype(BF16)
    wo = w_out[0].astype(BF16)
    wmix = w_pool_mix[0].astype(BF16)
    pscale = pool_scale[0].reshape(1, POOL_WIDTH)
    g1 = ln1_g[0].reshape(1, D_MODEL)
    b1 = ln1_b[0].reshape(1, D_MODEL)
    g2 = ln2_g[0].reshape(1, D_MODEL)
    b2 = ln2_b[0].reshape(1, D_MODEL)
    wr = jnp.concatenate([w_group[0], w_router[0].reshape(D_MODEL, N_EXPERTS)], axis=1)
    wr = jnp.pad(wr, ((0, 0), (0, LANES - wr.shape[1]))).astype(BF16)
    br = jnp.pad(jnp.concatenate([b_group[0], b_router[0].reshape(N_EXPERTS)]), (0, LANES - N_EXPERT_GROUPS - N_EXPERTS))
    br = br.reshape(1, LANES).astype(F32)
    wg = w_gate[0]
    wu = w_up[0]
    wd = w_down[0]
    sink = sinks[0].astype(F32)
    sink_gk = jnp.broadcast_to(sink.reshape(N_KV, GROUP).T.reshape(N_HEADS, 1), (N_HEADS, LANES))

    mk, mv = _mem_project(mem_prompt, w_mem_k[0].astype(BF16), w_mem_v[0].astype(BF16))
    rope_p = _rope_tables(jnp.arange(l, dtype=jnp.int32))
    h, xs, route, counts, nk_p, nv_p, npool_p = _front(
        x_prompt, rope_p, sink, win_b, wa, wb, wc, wo, wmix, pscale, mk, mv, g1, b1, wr, br)
    piece_src, piece_dst, tile_expert, tile_active = _piece_tables(
        counts.reshape(-1, SUBLANES, LANES)[:, 0, :N_EXPERTS].astype(jnp.int32))
    ys = _grouped_gemm(piece_src, piece_dst, tile_expert, tile_active, xs, wg, wu, wd)
    y_p = _combine(ys, h.reshape(t, D_MODEL), route.reshape(t, LANES), g2, b2).reshape(b, l, D_MODEL)

    rope_s = _rope_tables(jnp.full((1,), PAST_LEN, jnp.int32))
    xs = x_sample.reshape(db, D_MODEL)
    feat_major = lambda c: jnp.transpose(c[0], (0, 2, 3, 1)).reshape(db, KV_WIDTH, c.shape[2])
    from_feat_major = lambda a: jnp.transpose(a.reshape(db, N_KV, HEAD_DIM, a.shape[2]), (0, 3, 1, 2))[None]
    z_s, oa_s, oc_s, nk_s, nv_s = _sample_attn(xs, rope_s, win_b, sink_gk, feat_major(cache_win_k),
                                               feat_major(cache_win_v), feat_major(cache_mem_k), feat_major(cache_mem_v))
    state = jnp.transpose(state_pool[0], (1, 0, 2))
    y_s, npool_s = _sample_tail(xs, z_s, oa_s, oc_s, state, wa, wb, wc, wo, wmix, pscale, g1, b1, wr, br,
                                wg, wu, wd, g2, b2)

    kv5 = lambda a, n, w: a.reshape(1, n, w, N_KV, HEAD_DIM)
    return (y_p, y_s.reshape(db, 1, D_MODEL),
            kv5(nk_p, b, QB), kv5(nv_p, b, QB),
            npool_p[:, 2 * SUBLANES - POOL_STATE:, :][None],
            kv5(mk, b, N_MEM), kv5(mv, b, N_MEM),
            from_feat_major(nk_s), from_feat_major(nv_s),
            jnp.transpose(npool_s, (1, 0, 2))[None])
```

```python
import functools

import jax
import jax.numpy as jnp
import numpy as np
from jax import lax
from jax.experimental import pallas as pl
from jax.experimental.pallas import tpu as pltpu

D_MODEL = 1024
N_HEADS = 16
HEAD_DIM = 64
N_KV = 4
GROUP = N_HEADS // N_KV
WINDOW = 128
ROPE_THETA = 500000.0
ROPE_DIM = HEAD_DIM // 4
Q_WIDTH = N_HEADS * HEAD_DIM
KV_WIDTH = N_KV * HEAD_DIM
POOL_WINDOWS = (2, 4, 8, 16)
POOL_WIDTH = D_MODEL // 2
POOL_GROUP_DIM = POOL_WIDTH // len(POOL_WINDOWS)
POOL_STATE = max(POOL_WINDOWS) - 1
N_MEM = 256
MEM_HEADS = 4
MEM_WIDTH = MEM_HEADS * HEAD_DIM
N_EXPERT_GROUPS = 4
EXPERTS_PER_GROUP = 4
N_EXPERTS = N_EXPERT_GROUPS * EXPERTS_PER_GROUP
D_EXPERT = 512
PAST_LEN = 16384
DEPTH = 1
ALPHA = (2.0 * DEPTH) ** 0.25
LN_EPS = 1e-5

Q0 = 0
K0 = Q0 + Q_WIDTH
V0 = K0 + KV_WIDTH
U0 = V0 + KV_WIDTH
CQ0 = U0 + POOL_WIDTH
GZ0 = CQ0 + MEM_WIDTH
IN_WIDTH = GZ0 + 3 * D_MODEL

LANES = 128
SUBLANES = 8
VMEM_LIMIT = 56 * 1024 * 1024

TM = 256
QB = WINDOW
PIECE = 16
PIECES_PER_TILE = 32
TG = PIECE * PIECES_PER_TILE
MAX_CHUNK_PIECES = 2 * TM // PIECE + N_EXPERTS - 1
N_SPARE = 2 * (PIECES_PER_TILE - 1) + 1
IN_SLOTS = 3


def _chunk_rows(n_chunks):
    spare = -(-N_SPARE // n_chunks)
    return -(-(MAX_CHUNK_PIECES + spare) * PIECE // LANES) * LANES
SB = 8
IN_CHUNK = 768
SOFTMAX_ROWS = 64
assert N_MEM == 2 * QB and MEM_HEADS == N_KV and MEM_WIDTH == KV_WIDTH and TM <= GROUP * QB

BF16 = jnp.bfloat16
F32 = jnp.float32
NEG_INF = float("-inf")


def _const_spec(shape):
    nd = len(shape)
    return pl.BlockSpec(shape, lambda *_: (0,) * nd, pipeline_mode=pl.Buffered(1))


def _layer_norm(x, g, b):
    mu = jnp.mean(x, axis=-1, keepdims=True)
    xc = x - mu
    var = jnp.mean(xc * xc, axis=-1, keepdims=True)
    return xc * lax.rsqrt(var + LN_EPS) * g + b


def _dot(a, b):
    return jnp.dot(a, b, preferred_element_type=F32)


def _dot_nt(a, b):
    return lax.dot_general(a, b, (((1,), (1,)), ((), ())), preferred_element_type=F32)


def _lane_block_mask(shape, block, width=HEAD_DIM):
    lane = lax.broadcasted_iota(jnp.int32, shape, len(shape) - 1)
    return (lane >= block * width) & (lane < (block + 1) * width)


def _rope(x, c, s1, s2):
    half = ROPE_DIM // 2
    return x * c + pltpu.roll(x, half, 1) * s1 + pltpu.roll(x, LANES - half, 1) * s2


def _route(logits):
    rows = logits.shape[0]
    lane = lax.broadcasted_iota(jnp.int32, (rows, LANES), 1)
    lanef = lane.astype(F32)
    big = float(LANES)
    is_g = lane < N_EXPERT_GROUPS
    glog = jnp.where(is_g, logits, NEG_INF)
    gmax = jnp.max(glog, axis=1, keepdims=True)
    gsum = jnp.sum(jnp.where(is_g, jnp.exp(glog - gmax), 0.0), axis=1, keepdims=True)
    gp = 1.0 / gsum
    gidx = jnp.min(jnp.where(glog == gmax, lanef, big), axis=1, keepdims=True).astype(jnp.int32)
    lo = N_EXPERT_GROUPS + gidx * EXPERTS_PER_GROUP
    in_grp = (lane >= lo) & (lane < lo + EXPERTS_PER_GROUP)
    el = jnp.where(in_grp, logits, NEG_INF)
    v1 = jnp.max(el, axis=1, keepdims=True)
    i1 = jnp.min(jnp.where(el == v1, lanef, big), axis=1, keepdims=True).astype(jnp.int32)
    el2 = jnp.where(lane == i1, NEG_INF, el)
    v2 = jnp.max(el2, axis=1, keepdims=True)
    i2 = jnp.min(jnp.where(el2 == v2, lanef, big), axis=1, keepdims=True).astype(jnp.int32)
    e21 = jnp.exp(v2 - v1)
    inv = 1.0 / (1.0 + e21)
    w1 = inv * gp
    w2 = e21 * inv * gp
    e1 = i1 - N_EXPERT_GROUPS
    e2 = i2 - N_EXPERT_GROUPS
    return lane == e1, lane == e2, w1, w2


def _local_sort(hot1, hot2, w1, w2, hb, cap):
    rows = hb.shape[0]
    lane = lax.broadcasted_iota(jnp.int32, (rows, LANES), 1)
    onehot = jnp.where(hot1 | hot2, 1.0, 0.0)
    counts = jnp.sum(onehot, axis=0, keepdims=True)
    before = (lax.broadcasted_iota(jnp.int32, (rows, rows), 1)
              < lax.broadcasted_iota(jnp.int32, (rows, rows), 0)).astype(BF16)
    rank = _dot(before, onehot.astype(BF16))
    run = (((counts.astype(jnp.int32) + (PIECE - 1)) // PIECE) * PIECE).astype(F32)
    lower = (lax.broadcasted_iota(jnp.int32, (LANES, LANES), 0)
             < lax.broadcasted_iota(jnp.int32, (LANES, LANES), 1)).astype(BF16)
    start = _dot(jnp.broadcast_to(run, (SUBLANES, LANES)).astype(BF16), lower)[0:1]
    slot = start + rank
    s1 = jnp.sum(jnp.where(hot1, slot, 0.0), axis=1, keepdims=True)
    s2 = jnp.sum(jnp.where(hot2, slot, 0.0), axis=1, keepdims=True)
    route = jnp.where(lane == 0, w1, jnp.where(lane == 1, w2, jnp.where(lane == 2, s1, jnp.where(lane == 3, s2, 0.0))))
    route_t = route.T
    srow = lax.broadcasted_iota(jnp.int32, (cap, rows), 0).astype(F32)
    perm = jnp.where((srow == route_t[2:3, :]) | (srow == route_t[3:4, :]), 1.0, 0.0).astype(BF16)
    return _dot(perm, hb).astype(BF16), route, counts


def _sigmoid(x):
    return 0.5 * jnp.tanh(0.5 * x) + 0.5


def _merge_ln1(x, oa, ob, oc, gz, wa_ref, wb_ref, wc_ref, wo_ref, g1, b1):
    ya = _dot(oa, wa_ref[...])
    yb = _dot(ob.astype(BF16), wb_ref[...])
    yc = _dot(oc.astype(BF16), wc_ref[...])
    m = (_sigmoid(gz[:, 0:D_MODEL]) * ya
         + _sigmoid(gz[:, D_MODEL:2 * D_MODEL]) * yb
         + _sigmoid(gz[:, 2 * D_MODEL:3 * D_MODEL]) * yc)
    hpre = ALPHA * x + _dot(m.astype(BF16), wo_ref[...])
    return _layer_norm(hpre, g1, b1)


def _mem_kernel(mem_ref, wk_ref, wv_ref, mk_ref, mv_ref):
    m = mem_ref[...].astype(BF16)
    mk_ref[...] = _dot(m, wk_ref[...])
    mv_ref[...] = _dot(m, wv_ref[...])


def _mem_project(mem, wk, wv):
    b = mem.shape[0]
    out = jax.ShapeDtypeStruct((b, N_MEM, MEM_WIDTH), F32)
    return pl.pallas_call(
        _mem_kernel,
        grid=(b,),
        in_specs=[pl.BlockSpec((None, N_MEM, D_MODEL), lambda i: (i, 0, 0)),
                  _const_spec((D_MODEL, MEM_WIDTH)), _const_spec((D_MODEL, MEM_WIDTH))],
        out_specs=[pl.BlockSpec((None, N_MEM, MEM_WIDTH), lambda i: (i, 0, 0))] * 2,
        out_shape=[out, out],
        name="mem_project",
    )(mem, wk, wv)


def _front_kernel(sinks_ref, x_ref, rope_ref, win_ref, wa_ref, wb_ref, wc_ref, wo_ref, wmix_ref,
                  pscale_ref, mk_ref, mv_ref, g1_ref, b1_ref, wr_ref, br_ref,
                  h_ref, xs_ref, route_ref, counts_ref, nk_ref, nv_ref, npool_ref,
                  z_ref, qb_ref, kext_ref, vext_ref, uext_ref, oa_ref, ob_ref, bias_ref, s_ref, p_ref, vblk_ref):
    i = pl.program_id(1)
    x = x_ref[...]
    xb = x.astype(BF16)
    hist = 2 * SUBLANES

    @pl.when(i == 0)
    def _():
        kext_ref[0:QB, :] = jnp.zeros((QB, KV_WIDTH), BF16)
        vext_ref[0:QB, :] = jnp.zeros((QB, KV_WIDTH), BF16)
        uext_ref[0:hist, :] = jnp.zeros((hist, POOL_WIDTH), F32)

    @pl.when(i > 0)
    def _():
        kext_ref[0:QB, :] = kext_ref[TM:TM + QB, :]
        vext_ref[0:QB, :] = vext_ref[TM:TM + QB, :]
        uext_ref[0:hist, :] = uext_ref[TM:TM + hist, :]

    for c0 in range(0, IN_WIDTH, IN_CHUNK):
        z_ref[:, c0:c0 + IN_CHUNK] = _dot(xb, win_ref[:, c0:c0 + IN_CHUNK])

    c = rope_ref[0]
    s1 = rope_ref[1]
    s2 = rope_ref[2]
    for j in range(Q_WIDTH // LANES):
        sl = slice(Q0 + j * LANES, Q0 + (j + 1) * LANES)
        qb_ref[:, j * LANES:(j + 1) * LANES] = _rope(z_ref[:, sl], c, s1, s2).astype(BF16)
    for j in range(KV_WIDTH // LANES):
        sl = slice(K0 + j * LANES, K0 + (j + 1) * LANES)
        kr = _rope(z_ref[:, sl], c, s1, s2)
        z_ref[:, sl] = kr
        kext_ref[QB:QB + TM, j * LANES:(j + 1) * LANES] = kr.astype(BF16)
    vext_ref[QB:QB + TM, :] = z_ref[:, V0:V0 + KV_WIDTH].astype(BF16)
    uext_ref[hist:hist + TM, :] = z_ref[:, U0:U0 + POOL_WIDTH]
    nk_ref[...] = z_ref[TM - QB:TM, K0:K0 + KV_WIDTH]
    nv_ref[...] = z_ref[TM - QB:TM, V0:V0 + KV_WIDTH]

    rowq = lax.broadcasted_iota(jnp.int32, (QB, 2 * QB), 0)
    colk = lax.broadcasted_iota(jnp.int32, (QB, 2 * QB), 1)
    band = (colk >= rowq) & (colk <= rowq + WINDOW)
    bias_ref[1] = jnp.where(band, 0.0, NEG_INF)
    bias_ref[0] = jnp.where(band & ((colk >= QB) | (i > 0)), 0.0, NEG_INF)
    for sb in range(TM // QB):
        k2 = kext_ref[sb * QB:(sb + 2) * QB, :]
        v2 = vext_ref[sb * QB:(sb + 2) * QB, :]
        qs = jnp.concatenate(
            [qb_ref[sb * QB:(sb + 1) * QB, g * KV_WIDTH:(g + 1) * KV_WIDTH] for g in range(GROUP)], axis=0)
        for kv in range(N_KV):
            kmask = _lane_block_mask((2 * QB, KV_WIDTH), kv)
            s_ref[...] = _dot_nt(qs, jnp.where(kmask, k2, jnp.zeros_like(k2)))
            vblk_ref[kv * 2 * QB:(kv + 1) * 2 * QB, :] = jnp.where(kmask, v2, jnp.zeros_like(v2))
            for c0 in range(0, GROUP * QB, SOFTMAX_ROWS):
                rq = c0 % QB
                sink = sinks_ref[kv * GROUP + c0 // QB]
                s = s_ref[c0:c0 + SOFTMAX_ROWS, :] + bias_ref[min(sb, 1), rq:rq + SOFTMAX_ROWS, :]
                m = jnp.maximum(jnp.max(s, axis=1, keepdims=True), sink)
                p = jnp.exp(s - m)
                den = jnp.sum(p, axis=1, keepdims=True) + jnp.exp(sink - m)
                p_ref[c0:c0 + SOFTMAX_ROWS, kv * 2 * QB:(kv + 1) * 2 * QB] = (p * (1.0 / den)).astype(BF16)
        o = _dot(p_ref[...], vblk_ref[...])
        for g in range(GROUP):
            oa_ref[sb * QB:(sb + 1) * QB, g * KV_WIDTH:(g + 1) * KV_WIDTH] = o[g * QB:(g + 1) * QB].astype(BF16)

    npool_ref[...] = uext_ref[TM:TM + hist, :]
    pos = i * TM + lax.broadcasted_iota(jnp.int32, (TM, 1), 0)
    for g, w in enumerate(POOL_WINDOWS):
        sl = slice(g * POOL_GROUP_DIM, (g + 1) * POOL_GROUP_DIM)
        cur = uext_ref[hist:hist + TM, sl]
        ws = cur
        for j in range(1, w):
            ws = ws + uext_ref[hist - j:hist - j + TM, sl]
        cnt = jnp.minimum(pos + 1, w).astype(F32)
        pooled = ws / cnt - cur
        ob_ref[:, sl] = _dot(pooled.astype(BF16), wmix_ref[g]) * pscale_ref[:, sl]

    cq = z_ref[:, CQ0:CQ0 + MEM_WIDTH].astype(BF16)
    mk = mk_ref[...].astype(BF16)
    mv = mv_ref[...].astype(BF16)
    for hh in range(MEM_HEADS):
        hmask = _lane_block_mask((N_MEM, MEM_WIDTH), hh)
        s_ref[0:TM, :] = _dot_nt(cq, jnp.where(hmask, mk, jnp.zeros_like(mk)))
        vblk_ref[hh * N_MEM:(hh + 1) * N_MEM, :] = jnp.where(hmask, mv, jnp.zeros_like(mv))
        for c0 in range(0, TM, SOFTMAX_ROWS):
            s = s_ref[c0:c0 + SOFTMAX_ROWS, :]
            p = jnp.exp(s - jnp.max(s, axis=1, keepdims=True))
            den = jnp.sum(p, axis=1, keepdims=True)
            p_ref[c0:c0 + SOFTMAX_ROWS, hh * N_MEM:(hh + 1) * N_MEM] = (p * (1.0 / den)).astype(BF16)
    oc = _dot(p_ref[0:TM, :], vblk_ref[...])

    h = _merge_ln1(x, oa_ref[...], ob_ref[...], oc, z_ref[:, GZ0:GZ0 + 3 * D_MODEL],
                   wa_ref, wb_ref, wc_ref, wo_ref, g1_ref[...], b1_ref[...])
    h_ref[...] = h
    hb = h.astype(BF16)
    logits = _dot(hb, wr_ref[...]) + br_ref[...]
    xs, route, counts = _local_sort(*_route(logits), hb, xs_ref.shape[0])
    xs_ref[...] = xs
    route_ref[...] = route
    counts_ref[...] = jnp.broadcast_to(counts, (SUBLANES, LANES))


def _front(x, rope, sinks, win, wa, wb, wc, wo, wmix, pscale, mk, mv, g1, b1, wr, br):
    b, l, _ = x.shape
    nt = l // TM
    cap = _chunk_rows(b * nt)
    hist = 2 * SUBLANES
    tile = lambda w: pl.BlockSpec((None, TM, w), lambda bi, ti: (bi, ti, 0))
    per_b = lambda r, w: pl.BlockSpec((None, r, w), lambda bi, ti: (bi, 0, 0))
    return pl.pallas_call(
        _front_kernel,
        grid=(b, nt),
        in_specs=[
            pl.BlockSpec(memory_space=pltpu.SMEM),
            tile(D_MODEL),
            pl.BlockSpec((3, TM, LANES), lambda bi, ti: (0, ti, 0)),
            _const_spec((D_MODEL, IN_WIDTH)),
            _const_spec((Q_WIDTH, D_MODEL)), _const_spec((POOL_WIDTH, D_MODEL)),
            _const_spec((MEM_WIDTH, D_MODEL)), _const_spec((D_MODEL, D_MODEL)),
            _const_spec((len(POOL_WINDOWS), POOL_GROUP_DIM, POOL_GROUP_DIM)),
            _const_spec((1, POOL_WIDTH)),
            per_b(N_MEM, MEM_WIDTH), per_b(N_MEM, MEM_WIDTH),
            _const_spec((1, D_MODEL)), _const_spec((1, D_MODEL)),
            _const_spec((D_MODEL, LANES)), _const_spec((1, LANES)),
        ],
        out_specs=[
            tile(D_MODEL),
            pl.BlockSpec((cap, D_MODEL), lambda bi, ti: (bi * nt + ti, 0)),
            tile(LANES),
            pl.BlockSpec((None, None, SUBLANES, LANES), lambda bi, ti: (bi, ti, 0, 0)),
            per_b(QB, KV_WIDTH), per_b(QB, KV_WIDTH), per_b(hist, POOL_WIDTH),
        ],
        out_shape=[
            jax.ShapeDtypeStruct((b, l, D_MODEL), F32),
            jax.ShapeDtypeStruct((b * nt * cap, D_MODEL), BF16),
            jax.ShapeDtypeStruct((b, l, LANES), F32),
            jax.ShapeDtypeStruct((b, nt, SUBLANES, LANES), F32),
            jax.ShapeDtypeStruct((b, QB, KV_WIDTH), F32),
            jax.ShapeDtypeStruct((b, QB, KV_WIDTH), F32),
            jax.ShapeDtypeStruct((b, hist, POOL_WIDTH), F32),
        ],
        scratch_shapes=[
            pltpu.VMEM((TM, IN_WIDTH), F32),
            pltpu.VMEM((TM, Q_WIDTH), BF16),
            pltpu.VMEM((QB + TM, KV_WIDTH), BF16),
            pltpu.VMEM((QB + TM, KV_WIDTH), BF16),
            pltpu.VMEM((hist + TM, POOL_WIDTH), F32),
            pltpu.VMEM((TM, Q_WIDTH), BF16),
            pltpu.VMEM((TM, POOL_WIDTH), F32),
            pltpu.VMEM((2, QB, 2 * QB), F32),
            pltpu.VMEM((GROUP * QB, 2 * QB), F32),
            pltpu.VMEM((GROUP * QB, N_KV * 2 * QB), BF16),
            pltpu.VMEM((N_KV * 2 * QB, KV_WIDTH), BF16),
        ],
        compiler_params=pltpu.CompilerParams(
            dimension_semantics=("arbitrary", "arbitrary"), vmem_limit_bytes=VMEM_LIMIT),
        name="front_prompt",
    )(sinks, x, rope, win, wa, wb, wc, wo, wmix, pscale, mk, mv, g1, b1, wr, br)


def _expert_mlp(xb, wg, wu, wd):
    a = _dot(xb, wg)
    hid = (a * jax.nn.sigmoid(a)) * _dot(xb, wu)
    return _dot(hid.astype(BF16), wd)


def _gemm_kernel(src_ref, dst_ref, te_ref, act_ref, xs_ref, wg_ref, wu_ref, wd_ref, ys_ref,
                 xbuf, obuf, wgb, wub, wdb, prime, sem_in, sem_out):
    i = pl.program_id(0)
    n = pl.num_programs(0)
    slot = i % 2
    in_slot = i % IN_SLOTS

    def start_in(tile, slot):
        for j in range(PIECES_PER_TILE):
            row0 = pl.multiple_of(src_ref[tile * PIECES_PER_TILE + j], PIECE)
            pltpu.make_async_copy(xs_ref.at[pl.ds(row0, PIECE)], xbuf.at[slot, pl.ds(j * PIECE, PIECE)],
                                  sem_in.at[slot]).start()

    def start_out(tile, slot):
        for j in range(PIECES_PER_TILE):
            row0 = pl.multiple_of(dst_ref[tile * PIECES_PER_TILE + j], PIECE)
            pltpu.make_async_copy(obuf.at[slot, pl.ds(j * PIECE, PIECE)], ys_ref.at[pl.ds(row0, PIECE)],
                                  sem_out.at[slot]).start()

    def wait_in(slot):
        for j in range(PIECES_PER_TILE):
            pltpu.make_async_copy(xs_ref.at[pl.ds(0, PIECE)], xbuf.at[slot, pl.ds(j * PIECE, PIECE)],
                                  sem_in.at[slot]).wait()

    def wait_out(slot):
        for j in range(PIECES_PER_TILE):
            pltpu.make_async_copy(obuf.at[slot, pl.ds(j * PIECE, PIECE)], ys_ref.at[pl.ds(0, PIECE)],
                                  sem_out.at[slot]).wait()

    active = act_ref[i] > 0
    ahead = IN_SLOTS - 1
    prefetched = (i < ahead) | (act_ref[jnp.maximum(i - ahead, 0)] > 0)
    out_pending = (i < 2) | (act_ref[jnp.maximum(i - 2, 0)] > 0)

    @pl.when(i == 0)
    def _():
        prime[0] = jnp.zeros((PIECE, D_MODEL), BF16)
        for s in range(2):
            for j in range(PIECES_PER_TILE):
                pltpu.make_async_copy(prime.at[0], prime.at[1 + s * PIECES_PER_TILE + j], sem_out.at[s]).start()
        for t in range(ahead):
            start_in(t, t)

    @pl.when(active)
    def _():
        start_in(i + ahead, (i + ahead) % IN_SLOTS)

        @pl.when((i == 0) | (te_ref[i] != te_ref[jnp.maximum(i - 1, 0)]))
        def _():
            wgb[...] = wg_ref[...].astype(BF16)
            wub[...] = wu_ref[...].astype(BF16)
            wdb[...] = wd_ref[...].astype(BF16)

        wait_in(in_slot)
        wait_out(slot)
        obuf[slot] = _expert_mlp(xbuf[in_slot], wgb[...], wub[...], wdb[...]).astype(BF16)
        start_out(i, slot)

    @pl.when(jnp.logical_not(active))
    def _():
        @pl.when(prefetched)
        def _():
            wait_in(in_slot)

        @pl.when(out_pending)
        def _():
            wait_out(slot)

    @pl.when(i == n - 1)
    def _():
        for t in range(ahead):
            @pl.when(act_ref[jnp.maximum(i - t, 0)] > 0)
            def _(t=t):
                wait_in((i - t + ahead) % IN_SLOTS)

        @pl.when(active)
        def _():
            wait_out(slot)

        @pl.when((i >= 1) & (act_ref[jnp.maximum(i - 1, 0)] > 0))
        def _():
            wait_out(1 - slot)


def _grouped_gemm(piece_src, piece_dst, tile_expert, tile_active, xs, wg, wu, wd):
    n_tiles = tile_expert.shape[0]
    assert n_tiles >= 2
    wspec = lambda r, c: pl.BlockSpec((None, r, c), lambda i, src, dst, te, act: (te[i], 0, 0))
    return pl.pallas_call(
        _gemm_kernel,
        grid_spec=pltpu.PrefetchScalarGridSpec(
            num_scalar_prefetch=4,
            grid=(n_tiles,),
            in_specs=[pl.BlockSpec(memory_space=pl.ANY),
                      wspec(D_MODEL, D_EXPERT), wspec(D_MODEL, D_EXPERT), wspec(D_EXPERT, D_MODEL)],
            out_specs=pl.BlockSpec(memory_space=pl.ANY),
            scratch_shapes=[pltpu.VMEM((IN_SLOTS, TG, D_MODEL), BF16), pltpu.VMEM((2, TG, D_MODEL), BF16),
                            pltpu.VMEM((D_MODEL, D_EXPERT), BF16), pltpu.VMEM((D_MODEL, D_EXPERT), BF16),
                            pltpu.VMEM((D_EXPERT, D_MODEL), BF16),
                            pltpu.VMEM((1 + 2 * PIECES_PER_TILE, PIECE, D_MODEL), BF16),
                            pltpu.SemaphoreType.DMA((IN_SLOTS,)), pltpu.SemaphoreType.DMA((2,))],
        ),
        out_shape=jax.ShapeDtypeStruct(xs.shape, xs.dtype),
        input_output_aliases={4: 0},
        compiler_params=pltpu.CompilerParams(dimension_semantics=("arbitrary",), vmem_limit_bytes=VMEM_LIMIT),
        name="moe_grouped_gemm",
    )(piece_src, piece_dst, tile_expert, tile_active, xs, wg, wu, wd)


def _combine_kernel(ys_ref, h_ref, route_ref, g2_ref, b2_ref, y_ref):
    chunks = h_ref.shape[0] // TM
    cap = ys_ref.shape[0] // chunks
    slot = lax.broadcasted_iota(jnp.int32, (TM, cap), 1).astype(F32)
    for c in range(chunks):
        rows = slice(c * TM, (c + 1) * TM)
        route = route_ref[rows, :]
        sel = jnp.concatenate([jnp.where(slot == route[:, 2:3], 1.0, 0.0).astype(BF16),
                               jnp.where(slot == route[:, 3:4], 1.0, 0.0).astype(BF16)], axis=0)
        picked = _dot(sel, ys_ref[c * cap:(c + 1) * cap, :])
        f = route[:, 0:1] * picked[0:TM] + route[:, 1:2] * picked[TM:2 * TM]
        y_ref[rows, :] = _layer_norm(ALPHA * h_ref[rows, :] + f, g2_ref[...], b2_ref[...])


def _combine(ys, h, route, g2, b2):
    t = h.shape[0]
    cap = ys.shape[0] // (t // TM)
    per_step = next(k for k in (4, 2, 1) if (t // TM) % k == 0)
    rows = per_step * TM
    return pl.pallas_call(
        _combine_kernel,
        grid=(t // rows,),
        in_specs=[
            pl.BlockSpec((per_step * cap, D_MODEL), lambda i: (i, 0)),
            pl.BlockSpec((rows, D_MODEL), lambda i: (i, 0)),
            pl.BlockSpec((rows, LANES), lambda i: (i, 0)),
            pl.BlockSpec((1, D_MODEL), lambda i: (0, 0)),
            pl.BlockSpec((1, D_MODEL), lambda i: (0, 0)),
        ],
        out_specs=pl.BlockSpec((rows, D_MODEL), lambda i: (i, 0)),
        out_shape=jax.ShapeDtypeStruct((t, D_MODEL), F32),
        compiler_params=pltpu.CompilerParams(dimension_semantics=("arbitrary",)),
        name="moe_combine",
    )(ys, h, route, g2, b2)


def _select(table, idx):
    hot = idx[:, None] == jnp.arange(table.shape[0], dtype=jnp.int32)[None, :]
    return jnp.sum(jnp.where(hot[:, :, None], table[None, :, :], 0), axis=1)


def _piece_tables(counts):
    n_chunks = counts.shape[0]
    n_tiles = -(-(n_chunks * MAX_CHUNK_PIECES + N_EXPERTS * (PIECES_PER_TILE - 1)) // PIECES_PER_TILE)
    npc = (counts + (PIECE - 1)) // PIECE
    first = (jnp.cumsum(npc, axis=1) - npc).T
    npc_t = npc.T
    cum = jnp.cumsum(npc_t, axis=1)
    per_expert = cum[:, -1]
    tiles_e = (per_expert + (PIECES_PER_TILE - 1)) // PIECES_PER_TILE
    tile_end = jnp.cumsum(tiles_e)
    tile_idx = jnp.arange(n_tiles, dtype=jnp.int32)
    expert_of = lambda i: jnp.minimum(jnp.sum(i[:, None] >= tile_end[None, :], axis=1), N_EXPERTS - 1).astype(jnp.int32)
    active = tile_idx < tile_end[-1]
    tile_expert = jnp.where(active, expert_of(tile_idx), expert_of(tile_end[-1:] - 1))
    meta = jnp.stack([tile_end - tiles_e, per_expert], axis=1)
    meta_t = _select(meta, tile_expert)
    k = (tile_idx - meta_t[:, 0])[:, None] * PIECES_PER_TILE + jnp.arange(PIECES_PER_TILE, dtype=jnp.int32)[None, :]
    valid = active[:, None] & (k < meta_t[:, 1:2])
    cum_t = _select(cum, tile_expert)
    chunk = jnp.minimum(jnp.sum(k[:, :, None] >= cum_t[:, None, :], axis=2), n_chunks - 1).astype(jnp.int32)
    at_chunk = chunk[:, :, None] == jnp.arange(n_chunks, dtype=jnp.int32)[None, None, :]
    pick = lambda tab: jnp.sum(jnp.where(at_chunk, _select(tab, tile_expert)[:, None, :], 0), axis=2)
    piece = pick(first) + k - pick(cum - npc_t)
    cap = _chunk_rows(n_chunks)
    rows = chunk * cap + piece * PIECE
    d = ((tile_idx % 2)[:, None] * (PIECES_PER_TILE - 1)
         + jnp.maximum(jnp.arange(PIECES_PER_TILE, dtype=jnp.int32)[None, :] - 1, 0))
    spare_row = lambda d: (d % n_chunks) * cap + (MAX_CHUNK_PIECES + d // n_chunks) * PIECE
    spare = spare_row(d)
    zero_piece = spare_row(N_SPARE - 1)
    extra = jnp.full(((IN_SLOTS - 1) * PIECES_PER_TILE,), zero_piece, jnp.int32)
    src = jnp.concatenate([jnp.where(valid, rows, zero_piece).astype(jnp.int32).reshape(-1), extra])
    dst = jnp.concatenate([jnp.where(valid, rows, spare).astype(jnp.int32).reshape(-1), extra])
    return src, dst, tile_expert.astype(jnp.int32), active.astype(jnp.int32)


def _sample_attn_kernel(x_ref, rope_ref, win_ref, sink_ref, ck_ref, cv_ref, cmk_ref, cmv_ref,
                        z_ref, oa_ref, oc_ref, nk_ref, nv_ref, knew_t, vnew_t):
    j = pl.program_id(0)
    db = x_ref.shape[0]

    @pl.when(j == 0)
    def _():
        xb = x_ref[...].astype(BF16)
        for c0 in range(0, IN_WIDTH, IN_CHUNK):
            z_ref[:, c0:c0 + IN_CHUNK] = _dot(xb, win_ref[:, c0:c0 + IN_CHUNK])
        c = rope_ref[0]
        s1 = rope_ref[1]
        s2 = rope_ref[2]
        for jj in range((Q_WIDTH + KV_WIDTH) // LANES):
            sl = slice(jj * LANES, (jj + 1) * LANES)
            z_ref[:, sl] = _rope(z_ref[:, sl], c, s1, s2)
        knew_t[...] = z_ref[:, K0:K0 + KV_WIDTH].T
        vnew_t[...] = z_ref[:, V0:V0 + KV_WIDTH].T

    r0 = pl.multiple_of(j * SB, SB)
    zq = z_ref[pl.ds(r0, SB), Q0:Q0 + Q_WIDTH]
    zk = z_ref[pl.ds(r0, SB), K0:K0 + KV_WIDTH]
    zv = z_ref[pl.ds(r0, SB), V0:V0 + KV_WIDTH]
    zc = z_ref[pl.ds(r0, SB), CQ0:CQ0 + MEM_WIDTH]
    sink = sink_ref[:, 0:1]
    row_kv = lax.broadcasted_iota(jnp.int32, (N_HEADS, KV_WIDTH), 0) & (N_KV - 1)
    lane_kv = lax.broadcasted_iota(jnp.int32, (N_HEADS, KV_WIDTH), 1) // HEAD_DIM
    own = row_kv == lane_kv
    row_c = lax.broadcasted_iota(jnp.int32, (N_HEADS, MEM_WIDTH), 0)
    lane_c = lax.broadcasted_iota(jnp.int32, (N_HEADS, MEM_WIDTH), 1) // HEAD_DIM
    own_c = row_c == lane_c
    last_pos = lax.broadcasted_iota(jnp.int32, (KV_WIDTH, WINDOW), 1) == WINDOW - 1
    seq_lane = lax.broadcasted_iota(jnp.int32, (KV_WIDTH, db), 1)

    seqs = range(SB)
    qblk, cblk, s, sc = [], [], [], []
    for b in seqs:
        q4 = jnp.concatenate(
            [jnp.broadcast_to(zq[b:b + 1, g * KV_WIDTH:(g + 1) * KV_WIDTH], (N_KV, KV_WIDTH)) for g in range(GROUP)],
            axis=0)
        qblk.append(jnp.where(own, q4, 0.0).astype(BF16))
        cblk.append(jnp.where(own_c, jnp.broadcast_to(zc[b:b + 1, :], (N_HEADS, MEM_WIDTH)), 0.0).astype(BF16))
        s.append(_dot(qblk[b], ck_ref[b].astype(BF16)))
        sc.append(_dot(cblk[b], cmk_ref[b].astype(BF16)))

    pn, p_new, pc = [], [], []
    for b in seqs:
        s_new = jnp.sum(qblk[b].astype(F32) * zk[b:b + 1, :].astype(BF16).astype(F32), axis=1, keepdims=True)
        m = jnp.maximum(jnp.maximum(jnp.max(s[b], axis=1, keepdims=True), s_new), sink)
        p = jnp.exp(s[b] - m)
        e_new = jnp.exp(s_new - m)
        inv = 1.0 / (jnp.sum(p, axis=1, keepdims=True) + e_new + jnp.exp(sink - m))
        pn.append((p * inv).astype(BF16))
        p_new.append((e_new * inv).astype(BF16).astype(F32))
        e = jnp.exp(sc[b] - jnp.max(sc[b], axis=1, keepdims=True))
        pc.append((e * (1.0 / jnp.sum(e, axis=1, keepdims=True))).astype(BF16))

    oa_rows = [[] for _ in range(GROUP)]
    oc_rows = []
    for b in seqs:
        vc = cv_ref[b]
        o = _dot_nt(pn[b], vc.astype(BF16)) + p_new[b] * zv[b:b + 1, :].astype(BF16).astype(F32)
        o = jnp.where(own, o, 0.0)
        for g in range(GROUP):
            oa_rows[g].append(jnp.sum(o[g * N_KV:(g + 1) * N_KV], axis=0, keepdims=True))
        ocb = jnp.where(own_c, _dot_nt(pc[b], cmv_ref[b].astype(BF16)), 0.0)
        oc_rows.append(jnp.sum(ocb, axis=0, keepdims=True))
        here = seq_lane == j * SB + b
        knew_col = jnp.sum(jnp.where(here, knew_t[...], 0.0), axis=1, keepdims=True)
        vnew_col = jnp.sum(jnp.where(here, vnew_t[...], 0.0), axis=1, keepdims=True)
        nk_ref[b] = jnp.where(last_pos, knew_col, pltpu.roll(ck_ref[b], WINDOW - 1, 1))
        nv_ref[b] = jnp.where(last_pos, vnew_col, pltpu.roll(vc, WINDOW - 1, 1))
    for g in range(GROUP):
        oa_ref[pl.ds(r0, SB), g * KV_WIDTH:(g + 1) * KV_WIDTH] = jnp.concatenate(oa_rows[g], axis=0)
    oc_ref[pl.ds(r0, SB), :] = jnp.concatenate(oc_rows, axis=0)


def _sample_attn(x, rope, win, sink_gk, ck, cv, cmk, cmv):
    db = x.shape[0]
    blk = lambda r: pl.BlockSpec((SB, KV_WIDTH, r), lambda j: (j, 0, 0))
    full = lambda w: pl.BlockSpec((db, w), lambda j: (0, 0))
    return pl.pallas_call(
        _sample_attn_kernel,
        grid=(db // SB,),
        in_specs=[
            full(D_MODEL), _const_spec((3, 1, LANES)), _const_spec((D_MODEL, IN_WIDTH)),
            _const_spec((N_HEADS, LANES)),
            blk(WINDOW), blk(WINDOW), blk(N_MEM), blk(N_MEM),
        ],
        out_specs=[full(IN_WIDTH), full(Q_WIDTH), full(MEM_WIDTH), blk(WINDOW), blk(WINDOW)],
        out_shape=[
            jax.ShapeDtypeStruct((db, IN_WIDTH), F32),
            jax.ShapeDtypeStruct((db, Q_WIDTH), F32),
            jax.ShapeDtypeStruct((db, MEM_WIDTH), F32),
            jax.ShapeDtypeStruct((db, KV_WIDTH, WINDOW), F32),
            jax.ShapeDtypeStruct((db, KV_WIDTH, WINDOW), F32),
        ],
        scratch_shapes=[pltpu.VMEM((KV_WIDTH, db), F32), pltpu.VMEM((KV_WIDTH, db), F32)],
        compiler_params=pltpu.CompilerParams(dimension_semantics=("arbitrary",), vmem_limit_bytes=VMEM_LIMIT),
        name="sample_attn",
    )(x, rope, win, sink_gk, ck, cv, cmk, cmv)


def _sample_tail_kernel(x_ref, z_ref, oa_ref, oc_ref, st_ref, wa_ref, wb_ref, wc_ref, wo_ref, wmix_ref,
                        pscale_ref, g1_ref, b1_ref, wr_ref, br_ref, wg_ref, wu_ref, wd_ref, g2_ref, b2_ref,
                        y_ref, npool_ref, h_sc, comb_sc, acc_sc):
    e = pl.program_id(0)

    @pl.when(e == 0)
    def _():
        u = z_ref[:, U0:U0 + POOL_WIDTH]
        npool_ref[0:POOL_STATE - 1] = st_ref[1:POOL_STATE]
        npool_ref[POOL_STATE - 1] = u
        obs = []
        for g, w in enumerate(POOL_WINDOWS):
            sl = slice(g * POOL_GROUP_DIM, (g + 1) * POOL_GROUP_DIM)
            cur = u[:, sl]
            ws = cur
            for jj in range(1, w):
                ws = ws + st_ref[POOL_STATE - jj, :, sl]
            cnt = float(min(PAST_LEN + 1, w))
            pooled = ws / cnt - cur
            obs.append(_dot(pooled.astype(BF16), wmix_ref[g]) * pscale_ref[:, sl])
        ob = jnp.concatenate(obs, axis=1)
        h = _merge_ln1(x_ref[...], oa_ref[...].astype(BF16), ob, oc_ref[...], z_ref[:, GZ0:GZ0 + 3 * D_MODEL],
                       wa_ref, wb_ref, wc_ref, wo_ref, g1_ref[...], b1_ref[...])
        h_sc[...] = h
        logits = _dot(h.astype(BF16), wr_ref[...]) + br_ref[...]
        hot1, hot2, w1, w2 = _route(logits)
        comb_sc[...] = jnp.where(hot1, w1, 0.0) + jnp.where(hot2, w2, 0.0)
        acc_sc[...] = jnp.zeros_like(acc_sc)

    out = _expert_mlp(h_sc[...].astype(BF16), wg_ref[...].astype(BF16), wu_ref[...].astype(BF16),
                      wd_ref[...].astype(BF16))
    lane = lax.broadcasted_iota(jnp.int32, comb_sc.shape, 1)
    ce = jnp.sum(jnp.where(lane == e, comb_sc[...], 0.0), axis=1, keepdims=True)
    acc_sc[...] += ce * out

    @pl.when(e == pl.num_programs(0) - 1)
    def _():
        y_ref[...] = _layer_norm(ALPHA * h_sc[...] + acc_sc[...], g2_ref[...], b2_ref[...])


def _sample_tail(x, z, oa, oc, state, wa, wb, wc, wo, wmix, pscale, g1, b1, wr, br, wg, wu, wd, g2, b2):
    db = x.shape[0]
    full = lambda w: pl.BlockSpec((db, w), lambda e: (0, 0))
    vec = lambda w: pl.BlockSpec((1, w), lambda e: (0, 0))
    hist = pl.BlockSpec((POOL_STATE, db, POOL_WIDTH), lambda e: (0, 0, 0))
    return pl.pallas_call(
        _sample_tail_kernel,
        grid=(N_EXPERTS,),
        in_specs=[
            full(D_MODEL), full(IN_WIDTH), full(Q_WIDTH), full(MEM_WIDTH), hist,
            _const_spec((Q_WIDTH, D_MODEL)), _const_spec((POOL_WIDTH, D_MODEL)),
            _const_spec((MEM_WIDTH, D_MODEL)), _const_spec((D_MODEL, D_MODEL)),
            _const_spec((len(POOL_WINDOWS), POOL_GROUP_DIM, POOL_GROUP_DIM)),
            vec(POOL_WIDTH), vec(D_MODEL), vec(D_MODEL),
            _const_spec((D_MODEL, LANES)), vec(LANES),
            pl.BlockSpec((None, D_MODEL, D_EXPERT), lambda e: (e, 0, 0)),
            pl.BlockSpec((None, D_MODEL, D_EXPERT), lambda e: (e, 0, 0)),
            pl.BlockSpec((None, D_EXPERT, D_MODEL), lambda e: (e, 0, 0)),
            vec(D_MODEL), vec(D_MODEL),
        ],
        out_specs=[full(D_MODEL), hist],
        out_shape=[jax.ShapeDtypeStruct((db, D_MODEL), F32),
                   jax.ShapeDtypeStruct((POOL_STATE, db, POOL_WIDTH), F32)],
        scratch_shapes=[pltpu.VMEM((db, D_MODEL), F32), pltpu.VMEM((db, LANES), F32),
                        pltpu.VMEM((db, D_MODEL), F32)],
        compiler_params=pltpu.CompilerParams(dimension_semantics=("arbitrary",), vmem_limit_bytes=VMEM_LIMIT),
        name="sample_tail",
    )(x, z, oa, oc, state, wa, wb, wc, wo, wmix, pscale, g1, b1, wr, br, wg, wu, wd, g2, b2)


def _rope_tables(pos):
    half = ROPE_DIM // 2
    inv = jnp.power(ROPE_THETA, -jnp.arange(half, dtype=F32) * (2.0 / ROPE_DIM))
    ang = pos.astype(F32)[:, None] * inv[None, :]
    lane = np.arange(LANES)
    off = lane % HEAD_DIM
    cos = jnp.cos(ang)[:, lane % half]
    sin = jnp.sin(ang)[:, lane % half]
    c = jnp.where(off[None, :] < ROPE_DIM, cos, 1.0)
    s1 = jnp.where((off[None, :] >= half) & (off[None, :] < ROPE_DIM), sin, 0.0)
    s2 = jnp.where(off[None, :] < half, -sin, 0.0)
    return jnp.stack([c, s1, s2]).astype(F32)


def _q_heads_group_major(w, axis):
    if axis == 1:
        n = w.shape[0]
        return w.reshape(n, N_KV, GROUP, HEAD_DIM).transpose(0, 2, 1, 3).reshape(n, Q_WIDTH)
    n = w.shape[1]
    return w.reshape(N_KV, GROUP, HEAD_DIM, n).transpose(1, 0, 2, 3).reshape(Q_WIDTH, n)


def kernel(x_prompt, x_sample, cache_win_k, cache_win_v, state_pool, cache_mem_k, cache_mem_v, mem_prompt, w_in, sinks, w_pool_mix, pool_scale, w_mem_k, w_mem_v, w_branch_a, w_branch_b, w_branch_c, w_out, ln1_g, ln1_b, w_group, b_group, w_router, b_router, w_gate, w_up, w_down, ln2_g, ln2_b):
    assert w_in.shape[0] == DEPTH == 1
    b, l, _ = x_prompt.shape
    db, ds, _ = x_sample.shape
    assert ds == 1 and l % TM == 0 and db % SB == 0
    assert cache_win_k.shape[2] == WINDOW
    t = b * l

    win = w_in[0]
    scale = HEAD_DIM ** -0.5
    wq = _q_heads_group_major(win[:, Q0:Q0 + Q_WIDTH], 1) * scale
    wcq = win[:, CQ0:CQ0 + MEM_WIDTH] * scale
    win_b = jnp.concatenate([wq, win[:, K0:CQ0], wcq, win[:, GZ0:]], axis=1).astype(BF16)
    wa = _q_heads_group_major(w_branch_a[0], 0).astype(BF16)
    wb = w_branch_b[0].astype(BF16)
    wc = w_branch_c[0].astype(BF16)
    wo = w_out[0].astype(BF16)
    wmix = w_pool_mix[0].astype(BF16)
    pscale = pool_scale[0].reshape(1, POOL_WIDTH)
    g1 = ln1_g[0].reshape(1, D_MODEL)
    b1 = ln1_b[0].reshape(1, D_MODEL)
    g2 = ln2_g[0].reshape(1, D_MODEL)
    b2 = ln2_b[0].reshape(1, D_MODEL)
    wr = jnp.concatenate([w_group[0], w_router[0].reshape(D_MODEL, N_EXPERTS)], axis=1)
    wr = jnp.pad(wr, ((0, 0), (0, LANES - wr.shape[1]))).astype(BF16)
    br = jnp.pad(jnp.concatenate([b_group[0], b_router[0].reshape(N_EXPERTS)]), (0, LANES - N_EXPERT_GROUPS - N_EXPERTS))
    br = br.reshape(1, LANES).astype(F32)
    wg = w_gate[0]
    wu = w_up[0]
    wd = w_down[0]
    sink = sinks[0].astype(F32)
    sink_gk = jnp.broadcast_to(sink.reshape(N_KV, GROUP).T.reshape(N_HEADS, 1), (N_HEADS, LANES))

    mk, mv = _mem_project(mem_prompt, w_mem_k[0].astype(BF16), w_mem_v[0].astype(BF16))
    rope_p = _rope_tables(jnp.arange(l, dtype=jnp.int32))
    h, xs, route, counts, nk_p, nv_p, npool_p = _front(
        x_prompt, rope_p, sink, win_b, wa, wb, wc, wo, wmix, pscale, mk, mv, g1, b1, wr, br)
    piece_src, piece_dst, tile_expert, tile_active = _piece_tables(
        counts.reshape(-1, SUBLANES, LANES)[:, 0, :N_EXPERTS].astype(jnp.int32))
    ys = _grouped_gemm(piece_src, piece_dst, tile_expert, tile_active, xs, wg, wu, wd)
    y_p = _combine(ys, h.reshape(t, D_MODEL), route.reshape(t, LANES), g2, b2).reshape(b, l, D_MODEL)

    rope_s = _rope_tables(jnp.full((1,), PAST_LEN, jnp.int32))
    xs = x_sample.reshape(db, D_MODEL)
    feat_major = lambda c: jnp.transpose(c[0], (0, 2, 3, 1)).reshape(db, KV_WIDTH, c.shape[2])
    from_feat_major = lambda a: jnp.transpose(a.reshape(db, N_KV, HEAD_DIM, a.shape[2]), (0, 3, 1, 2))[None]
    z_s, oa_s, oc_s, nk_s, nv_s = _sample_attn(xs, rope_s, win_b, sink_gk, feat_major(cache_win_k),
                                               feat_major(cache_win_v), feat_major(cache_mem_k), feat_major(cache_mem_v))
    state = jnp.transpose(state_pool[0], (1, 0, 2))
    y_s, npool_s = _sample_tail(xs, z_s, oa_s, oc_s, state, wa, wb, wc, wo, wmix, pscale, g1, b1, wr, br,
                                wg, wu, wd, g2, b2)

    kv5 = lambda a, n, w: a.reshape(1, n, w, N_KV, HEAD_DIM)
    return (y_p, y_s.reshape(db, 1, D_MODEL),
            kv5(nk_p, b, QB), kv5(nv_p, b, QB),
            npool_p[:, 2 * SUBLANES - POOL_STATE:, :][None],
            kv5(mk, b, N_MEM), kv5(mv, b, N_MEM),
            from_feat_major(nk_s), from_feat_major(nv_s),
            jnp.transpose(npool_s, (1, 0, 2))[None])
```

```python
import functools

import jax
import jax.numpy as jnp
import numpy as np
from jax import lax
from jax.experimental import pallas as pl
from jax.experimental.pallas import tpu as pltpu

D_MODEL = 1024
N_HEADS = 16
HEAD_DIM = 64
N_KV = 4
GROUP = N_HEADS // N_KV
WINDOW = 128
ROPE_THETA = 500000.0
ROPE_DIM = HEAD_DIM // 4
Q_WIDTH = N_HEADS * HEAD_DIM
KV_WIDTH = N_KV * HEAD_DIM
POOL_WINDOWS = (2, 4, 8, 16)
POOL_WIDTH = D_MODEL // 2
POOL_GROUP_DIM = POOL_WIDTH // len(POOL_WINDOWS)
POOL_STATE = max(POOL_WINDOWS) - 1
N_MEM = 256
MEM_HEADS = 4
MEM_WIDTH = MEM_HEADS * HEAD_DIM
N_EXPERT_GROUPS = 4
EXPERTS_PER_GROUP = 4
N_EXPERTS = N_EXPERT_GROUPS * EXPERTS_PER_GROUP
D_EXPERT = 512
PAST_LEN = 16384
DEPTH = 1
ALPHA = (2.0 * DEPTH) ** 0.25
LN_EPS = 1e-5

Q0 = 0
K0 = Q0 + Q_WIDTH
V0 = K0 + KV_WIDTH
U0 = V0 + KV_WIDTH
CQ0 = U0 + POOL_WIDTH
GZ0 = CQ0 + MEM_WIDTH
IN_WIDTH = GZ0 + 3 * D_MODEL

LANES = 128
SUBLANES = 8
VMEM_LIMIT = 56 * 1024 * 1024

TM = 256
QB = WINDOW
PIECE = 16
PIECES_PER_TILE = 32
TG = PIECE * PIECES_PER_TILE
MAX_CHUNK_PIECES = 2 * TM // PIECE + N_EXPERTS - 1
N_SPARE = 2 * (PIECES_PER_TILE - 1) + 1
IN_SLOTS = 3


def _chunk_rows(n_chunks):
    spare = -(-N_SPARE // n_chunks)
    return -(-(MAX_CHUNK_PIECES + spare) * PIECE // LANES) * LANES
SB = 8
IN_CHUNK = 768
Q_CHUNK = 512
ATT_SCALE = HEAD_DIM ** -0.5
SOFTMAX_ROWS = 64
assert N_MEM == 2 * QB and MEM_HEADS == N_KV and MEM_WIDTH == KV_WIDTH and TM <= GROUP * QB

BF16 = jnp.bfloat16
F32 = jnp.float32
NEG_INF = float("-inf")


def _const_spec(shape):
    nd = len(shape)
    return pl.BlockSpec(shape, lambda *_: (0,) * nd, pipeline_mode=pl.Buffered(1))


def _layer_norm(x, g, b):
    mu = jnp.mean(x, axis=-1, keepdims=True)
    xc = x - mu
    var = jnp.mean(xc * xc, axis=-1, keepdims=True)
    return xc * lax.rsqrt(var + LN_EPS) * g + b


def _dot(a, b):
    return jnp.dot(a, b, preferred_element_type=F32)


def _dot_nt(a, b):
    return lax.dot_general(a, b, (((1,), (1,)), ((), ())), preferred_element_type=F32)


def _lane_block_mask(shape, block, width=HEAD_DIM):
    lane = lax.broadcasted_iota(jnp.int32, shape, len(shape) - 1)
    return (lane >= block * width) & (lane < (block + 1) * width)


def _rope(x, c, s1, s2):
    half = ROPE_DIM // 2
    return x * c + pltpu.roll(x, half, 1) * s1 + pltpu.roll(x, LANES - half, 1) * s2


def _route(logits):
    rows = logits.shape[0]
    lane = lax.broadcasted_iota(jnp.int32, (rows, LANES), 1)
    lanef = lane.astype(F32)
    big = float(LANES)
    is_g = lane < N_EXPERT_GROUPS
    glog = jnp.where(is_g, logits, NEG_INF)
    gmax = jnp.max(glog, axis=1, keepdims=True)
    gsum = jnp.sum(jnp.where(is_g, jnp.exp(glog - gmax), 0.0), axis=1, keepdims=True)
    gp = 1.0 / gsum
    gidx = jnp.min(jnp.where(glog == gmax, lanef, big), axis=1, keepdims=True).astype(jnp.int32)
    lo = N_EXPERT_GROUPS + gidx * EXPERTS_PER_GROUP
    in_grp = (lane >= lo) & (lane < lo + EXPERTS_PER_GROUP)
    el = jnp.where(in_grp, logits, NEG_INF)
    v1 = jnp.max(el, axis=1, keepdims=True)
    i1 = jnp.min(jnp.where(el == v1, lanef, big), axis=1, keepdims=True).astype(jnp.int32)
    el2 = jnp.where(lane == i1, NEG_INF, el)
    v2 = jnp.max(el2, axis=1, keepdims=True)
    i2 = jnp.min(jnp.where(el2 == v2, lanef, big), axis=1, keepdims=True).astype(jnp.int32)
    e21 = jnp.exp(v2 - v1)
    inv = 1.0 / (1.0 + e21)
    w1 = inv * gp
    w2 = e21 * inv * gp
    e1 = i1 - N_EXPERT_GROUPS
    e2 = i2 - N_EXPERT_GROUPS
    return lane == e1, lane == e2, w1, w2


def _local_sort(hot1, hot2, w1, w2, hb, cap):
    rows = hb.shape[0]
    lane = lax.broadcasted_iota(jnp.int32, (rows, LANES), 1)
    onehot = jnp.where(hot1 | hot2, 1.0, 0.0)
    counts = jnp.sum(onehot, axis=0, keepdims=True)
    before = (lax.broadcasted_iota(jnp.int32, (rows, rows), 1)
              < lax.broadcasted_iota(jnp.int32, (rows, rows), 0)).astype(BF16)
    rank = _dot(before, onehot.astype(BF16))
    run = (((counts.astype(jnp.int32) + (PIECE - 1)) // PIECE) * PIECE).astype(F32)
    lower = (lax.broadcasted_iota(jnp.int32, (LANES, LANES), 0)
             < lax.broadcasted_iota(jnp.int32, (LANES, LANES), 1)).astype(BF16)
    start = _dot(jnp.broadcast_to(run, (SUBLANES, LANES)).astype(BF16), lower)[0:1]
    slot = start + rank
    s1 = jnp.sum(jnp.where(hot1, slot, 0.0), axis=1, keepdims=True)
    s2 = jnp.sum(jnp.where(hot2, slot, 0.0), axis=1, keepdims=True)
    route = jnp.where(lane == 0, w1, jnp.where(lane == 1, w2, jnp.where(lane == 2, s1, jnp.where(lane == 3, s2, 0.0))))
    route_t = route.T
    srow = lax.broadcasted_iota(jnp.int32, (cap, rows), 0).astype(F32)
    perm = jnp.where((srow == route_t[2:3, :]) | (srow == route_t[3:4, :]), 1.0, 0.0).astype(BF16)
    return _dot(perm, hb).astype(BF16), route, counts


def _sigmoid(x):
    return 0.5 * jnp.tanh(0.5 * x) + 0.5


def _merge_ln1(x, oa, ob, oc, gz, wa_ref, wb_ref, wc_ref, wo_ref, g1, b1):
    ya = _dot(oa, wa_ref[...])
    yb = _dot(ob.astype(BF16), wb_ref[...])
    yc = _dot(oc.astype(BF16), wc_ref[...])
    m = (_sigmoid(gz[:, 0:D_MODEL]) * ya
         + _sigmoid(gz[:, D_MODEL:2 * D_MODEL]) * yb
         + _sigmoid(gz[:, 2 * D_MODEL:3 * D_MODEL]) * yc)
    hpre = ALPHA * x + _dot(m.astype(BF16), wo_ref[...])
    return _layer_norm(hpre, g1, b1)


def _mem_kernel(mem_ref, wk_ref, wv_ref, mk_ref, mv_ref):
    m = mem_ref[...].astype(BF16)
    mk_ref[...] = _dot(m, wk_ref[...])
    mv_ref[...] = _dot(m, wv_ref[...])


def _mem_project(mem, wk, wv):
    b = mem.shape[0]
    out = jax.ShapeDtypeStruct((b, N_MEM, MEM_WIDTH), F32)
    return pl.pallas_call(
        _mem_kernel,
        grid=(b,),
        in_specs=[pl.BlockSpec((None, N_MEM, D_MODEL), lambda i: (i, 0, 0)),
                  _const_spec((D_MODEL, MEM_WIDTH)), _const_spec((D_MODEL, MEM_WIDTH))],
        out_specs=[pl.BlockSpec((None, N_MEM, MEM_WIDTH), lambda i: (i, 0, 0))] * 2,
        out_shape=[out, out],
        name="mem_project",
    )(mem, wk, wv)


def _project(xb, wq_ref, wrest_ref, z_ref):
    for c0 in range(0, Q_WIDTH, Q_CHUNK):
        z_ref[:, Q0 + c0:Q0 + c0 + Q_CHUNK] = _dot(xb, wq_ref[:, c0:c0 + Q_CHUNK])
    rest = IN_WIDTH - K0
    for c0 in range(0, rest, IN_CHUNK):
        c1 = min(c0 + IN_CHUNK, rest)
        z_ref[:, K0 + c0:K0 + c1] = _dot(xb, wrest_ref[:, c0:c1])


def _front_kernel(sinks_ref, x_ref, rope_ref, wq_ref, wrest_ref, wa_ref, wb_ref, wc_ref, wo_ref, wmix_ref,
                  pscale_ref, mk_ref, mv_ref, g1_ref, b1_ref, wr_ref, br_ref,
                  h_ref, xs_ref, route_ref, counts_ref, nk_ref, nv_ref, npool_ref,
                  z_ref, qb_ref, kext_ref, vext_ref, uext_ref, oa_ref, ob_ref, bias_ref, s_ref, p_ref, vblk_ref):
    i = pl.program_id(1)
    x = x_ref[...]
    xb = x.astype(BF16)
    hist = 2 * SUBLANES

    @pl.when(i == 0)
    def _():
        kext_ref[0:QB, :] = jnp.zeros((QB, KV_WIDTH), BF16)
        vext_ref[0:QB, :] = jnp.zeros((QB, KV_WIDTH), BF16)
        uext_ref[0:hist, :] = jnp.zeros((hist, POOL_WIDTH), F32)

    @pl.when(i > 0)
    def _():
        kext_ref[0:QB, :] = kext_ref[TM:TM + QB, :]
        vext_ref[0:QB, :] = vext_ref[TM:TM + QB, :]
        uext_ref[0:hist, :] = uext_ref[TM:TM + hist, :]

    _project(xb, wq_ref, wrest_ref, z_ref)

    c = rope_ref[0]
    s1 = rope_ref[1]
    s2 = rope_ref[2]
    for j in range(Q_WIDTH // LANES):
        sl = slice(Q0 + j * LANES, Q0 + (j + 1) * LANES)
        qb_ref[:, j * LANES:(j + 1) * LANES] = _rope(z_ref[:, sl], c, s1, s2).astype(BF16)
    for j in range(KV_WIDTH // LANES):
        sl = slice(K0 + j * LANES, K0 + (j + 1) * LANES)
        kr = _rope(z_ref[:, sl], c, s1, s2)
        z_ref[:, sl] = kr
        kext_ref[QB:QB + TM, j * LANES:(j + 1) * LANES] = kr.astype(BF16)
    vext_ref[QB:QB + TM, :] = z_ref[:, V0:V0 + KV_WIDTH].astype(BF16)
    uext_ref[hist:hist + TM, :] = z_ref[:, U0:U0 + POOL_WIDTH]
    nk_ref[...] = z_ref[TM - QB:TM, K0:K0 + KV_WIDTH]
    nv_ref[...] = z_ref[TM - QB:TM, V0:V0 + KV_WIDTH]

    rowq = lax.broadcasted_iota(jnp.int32, (QB, 2 * QB), 0)
    colk = lax.broadcasted_iota(jnp.int32, (QB, 2 * QB), 1)
    band = (colk >= rowq) & (colk <= rowq + WINDOW)
    bias_ref[1] = jnp.where(band, 0.0, NEG_INF)
    bias_ref[0] = jnp.where(band & ((colk >= QB) | (i > 0)), 0.0, NEG_INF)
    for sb in range(TM // QB):
        k2 = kext_ref[sb * QB:(sb + 2) * QB, :]
        v2 = vext_ref[sb * QB:(sb + 2) * QB, :]
        qs = jnp.concatenate(
            [qb_ref[sb * QB:(sb + 1) * QB, g * KV_WIDTH:(g + 1) * KV_WIDTH] for g in range(GROUP)], axis=0)
        for kv in range(N_KV):
            kmask = _lane_block_mask((2 * QB, KV_WIDTH), kv)
            s_ref[...] = _dot_nt(qs, jnp.where(kmask, k2, jnp.zeros_like(k2)))
            vblk_ref[kv * 2 * QB:(kv + 1) * 2 * QB, :] = jnp.where(kmask, v2, jnp.zeros_like(v2))
            for c0 in range(0, GROUP * QB, SOFTMAX_ROWS):
                rq = c0 % QB
                sink = sinks_ref[kv * GROUP + c0 // QB]
                s = s_ref[c0:c0 + SOFTMAX_ROWS, :] + bias_ref[min(sb, 1), rq:rq + SOFTMAX_ROWS, :]
                m = jnp.maximum(jnp.max(s, axis=1, keepdims=True), sink)
                p = jnp.exp(s - m)
                den = jnp.sum(p, axis=1, keepdims=True) + jnp.exp(sink - m)
                p_ref[c0:c0 + SOFTMAX_ROWS, kv * 2 * QB:(kv + 1) * 2 * QB] = (p * (1.0 / den)).astype(BF16)
        o = _dot(p_ref[...], vblk_ref[...])
        for g in range(GROUP):
            oa_ref[sb * QB:(sb + 1) * QB, g * KV_WIDTH:(g + 1) * KV_WIDTH] = o[g * QB:(g + 1) * QB].astype(BF16)

    npool_ref[...] = uext_ref[TM:TM + hist, :]
    pos = i * TM + lax.broadcasted_iota(jnp.int32, (TM, 1), 0)
    for g, w in enumerate(POOL_WINDOWS):
        sl = slice(g * POOL_GROUP_DIM, (g + 1) * POOL_GROUP_DIM)
        cur = uext_ref[hist:hist + TM, sl]
        ws = cur
        for j in range(1, w):
            ws = ws + uext_ref[hist - j:hist - j + TM, sl]
        cnt = jnp.minimum(pos + 1, w).astype(F32)
        pooled = ws / cnt - cur
        ob_ref[:, sl] = _dot(pooled.astype(BF16), wmix_ref[g]) * pscale_ref[:, sl]

    cq = (z_ref[:, CQ0:CQ0 + MEM_WIDTH] * ATT_SCALE).astype(BF16)
    mk = mk_ref[...].astype(BF16)
    mv = mv_ref[...].astype(BF16)
    for hh in range(MEM_HEADS):
        hmask = _lane_block_mask((N_MEM, MEM_WIDTH), hh)
        s_ref[0:TM, :] = _dot_nt(cq, jnp.where(hmask, mk, jnp.zeros_like(mk)))
        vblk_ref[hh * N_MEM:(hh + 1) * N_MEM, :] = jnp.where(hmask, mv, jnp.zeros_like(mv))
        for c0 in range(0, TM, SOFTMAX_ROWS):
            s = s_ref[c0:c0 + SOFTMAX_ROWS, :]
            p = jnp.exp(s - jnp.max(s, axis=1, keepdims=True))
            den = jnp.sum(p, axis=1, keepdims=True)
            p_ref[c0:c0 + SOFTMAX_ROWS, hh * N_MEM:(hh + 1) * N_MEM] = (p * (1.0 / den)).astype(BF16)
    oc = _dot(p_ref[0:TM, :], vblk_ref[...])

    h = _merge_ln1(x, oa_ref[...], ob_ref[...], oc, z_ref[:, GZ0:GZ0 + 3 * D_MODEL],
                   wa_ref, wb_ref, wc_ref, wo_ref, g1_ref[...], b1_ref[...])
    h_ref[...] = h
    hb = h.astype(BF16)
    logits = _dot(hb, wr_ref[...]) + br_ref[...]
    xs, route, counts = _local_sort(*_route(logits), hb, xs_ref.shape[0])
    xs_ref[...] = xs
    route_ref[...] = route
    counts_ref[...] = jnp.broadcast_to(counts, (SUBLANES, LANES))


def _front(x, rope, sinks, wq, wrest, wa, wb, wc, wo, wmix, pscale, mk, mv, g1, b1, wr, br):
    b, l, _ = x.shape
    nt = l // TM
    cap = _chunk_rows(b * nt)
    hist = 2 * SUBLANES
    tile = lambda w: pl.BlockSpec((None, TM, w), lambda bi, ti: (bi, ti, 0))
    per_b = lambda r, w: pl.BlockSpec((None, r, w), lambda bi, ti: (bi, 0, 0))
    return pl.pallas_call(
        _front_kernel,
        grid=(b, nt),
        in_specs=[
            pl.BlockSpec(memory_space=pltpu.SMEM),
            tile(D_MODEL),
            pl.BlockSpec((3, TM, LANES), lambda bi, ti: (0, ti, 0)),
            _const_spec((D_MODEL, Q_WIDTH)), _const_spec((D_MODEL, IN_WIDTH - K0)),
            _const_spec((Q_WIDTH, D_MODEL)), _const_spec((POOL_WIDTH, D_MODEL)),
            _const_spec((MEM_WIDTH, D_MODEL)), _const_spec((D_MODEL, D_MODEL)),
            _const_spec((len(POOL_WINDOWS), POOL_GROUP_DIM, POOL_GROUP_DIM)),
            _const_spec((1, POOL_WIDTH)),
            per_b(N_MEM, MEM_WIDTH), per_b(N_MEM, MEM_WIDTH),
            _const_spec((1, D_MODEL)), _const_spec((1, D_MODEL)),
            _const_spec((D_MODEL, LANES)), _const_spec((1, LANES)),
        ],
        out_specs=[
            tile(D_MODEL),
            pl.BlockSpec((cap, D_MODEL), lambda bi, ti: (bi * nt + ti, 0)),
            tile(LANES),
            pl.BlockSpec((None, None, SUBLANES, LANES), lambda bi, ti: (bi, ti, 0, 0)),
            per_b(QB, KV_WIDTH), per_b(QB, KV_WIDTH), per_b(hist, POOL_WIDTH),
        ],
        out_shape=[
            jax.ShapeDtypeStruct((b, l, D_MODEL), F32),
            jax.ShapeDtypeStruct((b * nt * cap, D_MODEL), BF16),
            jax.ShapeDtypeStruct((b, l, LANES), F32),
            jax.ShapeDtypeStruct((b, nt, SUBLANES, LANES), F32),
            jax.ShapeDtypeStruct((b, QB, KV_WIDTH), F32),
            jax.ShapeDtypeStruct((b, QB, KV_WIDTH), F32),
            jax.ShapeDtypeStruct((b, hist, POOL_WIDTH), F32),
        ],
        scratch_shapes=[
            pltpu.VMEM((TM, IN_WIDTH), F32),
            pltpu.VMEM((TM, Q_WIDTH), BF16),
            pltpu.VMEM((QB + TM, KV_WIDTH), BF16),
            pltpu.VMEM((QB + TM, KV_WIDTH), BF16),
            pltpu.VMEM((hist + TM, POOL_WIDTH), F32),
            pltpu.VMEM((TM, Q_WIDTH), BF16),
            pltpu.VMEM((TM, POOL_WIDTH), F32),
            pltpu.VMEM((2, QB, 2 * QB), F32),
            pltpu.VMEM((GROUP * QB, 2 * QB), F32),
            pltpu.VMEM((GROUP * QB, N_KV * 2 * QB), BF16),
            pltpu.VMEM((N_KV * 2 * QB, KV_WIDTH), BF16),
        ],
        compiler_params=pltpu.CompilerParams(
            dimension_semantics=("arbitrary", "arbitrary"), vmem_limit_bytes=VMEM_LIMIT),
        name="front_prompt",
    )(sinks, x, rope, wq, wrest, wa, wb, wc, wo, wmix, pscale, mk, mv, g1, b1, wr, br)


def _expert_mlp(xb, wg, wu, wd):
    a = _dot(xb, wg)
    hid = (a * jax.nn.sigmoid(a)) * _dot(xb, wu)
    return _dot(hid.astype(BF16), wd)


def _gemm_kernel(src_ref, dst_ref, te_ref, act_ref, xs_ref, wg_ref, wu_ref, wd_ref, ys_ref,
                 xbuf, obuf, wgb, wub, wdb, prime, sem_in, sem_out):
    i = pl.program_id(0)
    n = pl.num_programs(0)
    slot = i % 2
    in_slot = i % IN_SLOTS

    def start_in(tile, slot):
        for j in range(PIECES_PER_TILE):
            row0 = pl.multiple_of(src_ref[tile * PIECES_PER_TILE + j], PIECE)
            pltpu.make_async_copy(xs_ref.at[pl.ds(row0, PIECE)], xbuf.at[slot, pl.ds(j * PIECE, PIECE)],
                                  sem_in.at[slot]).start()

    def start_out(tile, slot):
        for j in range(PIECES_PER_TILE):
            row0 = pl.multiple_of(dst_ref[tile * PIECES_PER_TILE + j], PIECE)
            pltpu.make_async_copy(obuf.at[slot, pl.ds(j * PIECE, PIECE)], ys_ref.at[pl.ds(row0, PIECE)],
                                  sem_out.at[slot]).start()

    def wait_in(slot):
        for j in range(PIECES_PER_TILE):
            pltpu.make_async_copy(xs_ref.at[pl.ds(0, PIECE)], xbuf.at[slot, pl.ds(j * PIECE, PIECE)],
                                  sem_in.at[slot]).wait()

    def wait_out(slot):
        for j in range(PIECES_PER_TILE):
            pltpu.make_async_copy(obuf.at[slot, pl.ds(j * PIECE, PIECE)], ys_ref.at[pl.ds(0, PIECE)],
                                  sem_out.at[slot]).wait()

    active = act_ref[i] > 0
    ahead = IN_SLOTS - 1
    prefetched = (i < ahead) | (act_ref[jnp.maximum(i - ahead, 0)] > 0)
    out_pending = (i < 2) | (act_ref[jnp.maximum(i - 2, 0)] > 0)

    @pl.when(i == 0)
    def _():
        prime[0] = jnp.zeros((PIECE, D_MODEL), BF16)
        for s in range(2):
            for j in range(PIECES_PER_TILE):
                pltpu.make_async_copy(prime.at[0], prime.at[1 + s * PIECES_PER_TILE + j], sem_out.at[s]).start()
        for t in range(ahead):
            start_in(t, t)

    @pl.when(active)
    def _():
        start_in(i + ahead, (i + ahead) % IN_SLOTS)

        @pl.when((i == 0) | (te_ref[i] != te_ref[jnp.maximum(i - 1, 0)]))
        def _():
            wgb[...] = wg_ref[...].astype(BF16)
            wub[...] = wu_ref[...].astype(BF16)
            wdb[...] = wd_ref[...].astype(BF16)

        wait_in(in_slot)
        wait_out(slot)
        obuf[slot] = _expert_mlp(xbuf[in_slot], wgb[...], wub[...], wdb[...]).astype(BF16)
        start_out(i, slot)

    @pl.when(jnp.logical_not(active))
    def _():
        @pl.when(prefetched)
        def _():
            wait_in(in_slot)

        @pl.when(out_pending)
        def _():
            wait_out(slot)

    @pl.when(i == n - 1)
    def _():
        for t in range(ahead):
            @pl.when(act_ref[jnp.maximum(i - t, 0)] > 0)
            def _(t=t):
                wait_in((i - t + ahead) % IN_SLOTS)

        @pl.when(active)
        def _():
            wait_out(slot)

        @pl.when((i >= 1) & (act_ref[jnp.maximum(i - 1, 0)] > 0))
        def _():
            wait_out(1 - slot)


def _grouped_gemm(piece_src, piece_dst, tile_expert, tile_active, xs, wg, wu, wd):
    n_tiles = tile_expert.shape[0]
    assert n_tiles >= 2
    wspec = lambda r, c: pl.BlockSpec((None, r, c), lambda i, src, dst, te, act: (te[i], 0, 0))
    return pl.pallas_call(
        _gemm_kernel,
        grid_spec=pltpu.PrefetchScalarGridSpec(
            num_scalar_prefetch=4,
            grid=(n_tiles,),
            in_specs=[pl.BlockSpec(memory_space=pl.ANY),
                      wspec(D_MODEL, D_EXPERT), wspec(D_MODEL, D_EXPERT), wspec(D_EXPERT, D_MODEL)],
            out_specs=pl.BlockSpec(memory_space=pl.ANY),
            scratch_shapes=[pltpu.VMEM((IN_SLOTS, TG, D_MODEL), BF16), pltpu.VMEM((2, TG, D_MODEL), BF16),
                            pltpu.VMEM((D_MODEL, D_EXPERT), BF16), pltpu.VMEM((D_MODEL, D_EXPERT), BF16),
                            pltpu.VMEM((D_EXPERT, D_MODEL), BF16),
                            pltpu.VMEM((1 + 2 * PIECES_PER_TILE, PIECE, D_MODEL), BF16),
                            pltpu.SemaphoreType.DMA((IN_SLOTS,)), pltpu.SemaphoreType.DMA((2,))],
        ),
        out_shape=jax.ShapeDtypeStruct(xs.shape, xs.dtype),
        input_output_aliases={4: 0},
        compiler_params=pltpu.CompilerParams(dimension_semantics=("arbitrary",), vmem_limit_bytes=VMEM_LIMIT),
        name="moe_grouped_gemm",
    )(piece_src, piece_dst, tile_expert, tile_active, xs, wg, wu, wd)


def _combine_kernel(ys_ref, h_ref, route_ref, g2_ref, b2_ref, y_ref):
    chunks = h_ref.shape[0] // TM
    cap = ys_ref.shape[0] // chunks
    slot = lax.broadcasted_iota(jnp.int32, (TM, cap), 1).astype(F32)
    for c in range(chunks):
        rows = slice(c * TM, (c + 1) * TM)
        route = route_ref[rows, :]
        sel = jnp.concatenate([jnp.where(slot == route[:, 2:3], 1.0, 0.0).astype(BF16),
                               jnp.where(slot == route[:, 3:4], 1.0, 0.0).astype(BF16)], axis=0)
        picked = _dot(sel, ys_ref[c * cap:(c + 1) * cap, :])
        f = route[:, 0:1] * picked[0:TM] + route[:, 1:2] * picked[TM:2 * TM]
        y_ref[rows, :] = _layer_norm(ALPHA * h_ref[rows, :] + f, g2_ref[...], b2_ref[...])


def _combine(ys, h, route, g2, b2):
    t = h.shape[0]
    cap = ys.shape[0] // (t // TM)
    per_step = next(k for k in (4, 2, 1) if (t // TM) % k == 0)
    rows = per_step * TM
    return pl.pallas_call(
        _combine_kernel,
        grid=(t // rows,),
        in_specs=[
            pl.BlockSpec((per_step * cap, D_MODEL), lambda i: (i, 0)),
            pl.BlockSpec((rows, D_MODEL), lambda i: (i, 0)),
            pl.BlockSpec((rows, LANES), lambda i: (i, 0)),
            pl.BlockSpec((1, D_MODEL), lambda i: (0, 0)),
            pl.BlockSpec((1, D_MODEL), lambda i: (0, 0)),
        ],
        out_specs=pl.BlockSpec((rows, D_MODEL), lambda i: (i, 0)),
        out_shape=jax.ShapeDtypeStruct((t, D_MODEL), F32),
        compiler_params=pltpu.CompilerParams(dimension_semantics=("arbitrary",)),
        name="moe_combine",
    )(ys, h, route, g2, b2)


def _select(table, idx):
    hot = idx[:, None] == jnp.arange(table.shape[0], dtype=jnp.int32)[None, :]
    return jnp.sum(jnp.where(hot[:, :, None], table[None, :, :], 0), axis=1)


def _piece_tables(counts):
    n_chunks = counts.shape[0]
    n_tiles = -(-(n_chunks * MAX_CHUNK_PIECES + N_EXPERTS * (PIECES_PER_TILE - 1)) // PIECES_PER_TILE)
    npc = (counts + (PIECE - 1)) // PIECE
    first = (jnp.cumsum(npc, axis=1) - npc).T
    npc_t = npc.T
    cum = jnp.cumsum(npc_t, axis=1)
    per_expert = cum[:, -1]
    tiles_e = (per_expert + (PIECES_PER_TILE - 1)) // PIECES_PER_TILE
    tile_end = jnp.cumsum(tiles_e)
    tile_idx = jnp.arange(n_tiles, dtype=jnp.int32)
    expert_of = lambda i: jnp.minimum(jnp.sum(i[:, None] >= tile_end[None, :], axis=1), N_EXPERTS - 1).astype(jnp.int32)
    active = tile_idx < tile_end[-1]
    tile_expert = jnp.where(active, expert_of(tile_idx), expert_of(tile_end[-1:] - 1))
    meta = jnp.stack([tile_end - tiles_e, per_expert], axis=1)
    meta_t = _select(meta, tile_expert)
    k = (tile_idx - meta_t[:, 0])[:, None] * PIECES_PER_TILE + jnp.arange(PIECES_PER_TILE, dtype=jnp.int32)[None, :]
    valid = active[:, None] & (k < meta_t[:, 1:2])
    cum_t = _select(cum, tile_expert)
    chunk = jnp.minimum(jnp.sum(k[:, :, None] >= cum_t[:, None, :], axis=2), n_chunks - 1).astype(jnp.int32)
    at_chunk = chunk[:, :, None] == jnp.arange(n_chunks, dtype=jnp.int32)[None, None, :]
    pick = lambda tab: jnp.sum(jnp.where(at_chunk, _select(tab, tile_expert)[:, None, :], 0), axis=2)
    piece = pick(first) + k - pick(cum - npc_t)
    cap = _chunk_rows(n_chunks)
    rows = chunk * cap + piece * PIECE
    d = ((tile_idx % 2)[:, None] * (PIECES_PER_TILE - 1)
         + jnp.maximum(jnp.arange(PIECES_PER_TILE, dtype=jnp.int32)[None, :] - 1, 0))
    spare_row = lambda d: (d % n_chunks) * cap + (MAX_CHUNK_PIECES + d // n_chunks) * PIECE
    spare = spare_row(d)
    zero_piece = spare_row(N_SPARE - 1)
    extra = jnp.full(((IN_SLOTS - 1) * PIECES_PER_TILE,), zero_piece, jnp.int32)
    src = jnp.concatenate([jnp.where(valid, rows, zero_piece).astype(jnp.int32).reshape(-1), extra])
    dst = jnp.concatenate([jnp.where(valid, rows, spare).astype(jnp.int32).reshape(-1), extra])
    return src, dst, tile_expert.astype(jnp.int32), active.astype(jnp.int32)


def _sample_attn_kernel(x_ref, rope_ref, wq_ref, wrest_ref, sink_ref, ck_ref, cv_ref, cmk_ref, cmv_ref,
                        z_ref, oa_ref, oc_ref, nk_ref, nv_ref, knew_t, vnew_t):
    j = pl.program_id(0)
    db = x_ref.shape[0]

    @pl.when(j == 0)
    def _():
        _project(x_ref[...].astype(BF16), wq_ref, wrest_ref, z_ref)
        c = rope_ref[0]
        s1 = rope_ref[1]
        s2 = rope_ref[2]
        for jj in range((Q_WIDTH + KV_WIDTH) // LANES):
            sl = slice(jj * LANES, (jj + 1) * LANES)
            z_ref[:, sl] = _rope(z_ref[:, sl], c, s1, s2)
        knew_t[...] = z_ref[:, K0:K0 + KV_WIDTH].T
        vnew_t[...] = z_ref[:, V0:V0 + KV_WIDTH].T

    r0 = pl.multiple_of(j * SB, SB)
    zq = z_ref[pl.ds(r0, SB), Q0:Q0 + Q_WIDTH]
    zk = z_ref[pl.ds(r0, SB), K0:K0 + KV_WIDTH]
    zv = z_ref[pl.ds(r0, SB), V0:V0 + KV_WIDTH]
    zc = z_ref[pl.ds(r0, SB), CQ0:CQ0 + MEM_WIDTH] * ATT_SCALE
    sink = sink_ref[:, 0:1]
    row_kv = lax.broadcasted_iota(jnp.int32, (N_HEADS, KV_WIDTH), 0) & (N_KV - 1)
    lane_kv = lax.broadcasted_iota(jnp.int32, (N_HEADS, KV_WIDTH), 1) // HEAD_DIM
    own = row_kv == lane_kv
    row_c = lax.broadcasted_iota(jnp.int32, (N_HEADS, MEM_WIDTH), 0)
    lane_c = lax.broadcasted_iota(jnp.int32, (N_HEADS, MEM_WIDTH), 1) // HEAD_DIM
    own_c = row_c == lane_c
    last_pos = lax.broadcasted_iota(jnp.int32, (KV_WIDTH, WINDOW), 1) == WINDOW - 1
    seq_lane = lax.broadcasted_iota(jnp.int32, (KV_WIDTH, db), 1)

    seqs = range(SB)
    qblk, cblk, s, sc = [], [], [], []
    for b in seqs:
        q4 = jnp.concatenate(
            [jnp.broadcast_to(zq[b:b + 1, g * KV_WIDTH:(g + 1) * KV_WIDTH], (N_KV, KV_WIDTH)) for g in range(GROUP)],
            axis=0)
        qblk.append(jnp.where(own, q4, 0.0).astype(BF16))
        cblk.append(jnp.where(own_c, jnp.broadcast_to(zc[b:b + 1, :], (N_HEADS, MEM_WIDTH)), 0.0).astype(BF16))
        s.append(_dot(qblk[b], ck_ref[b].astype(BF16)))
        sc.append(_dot(cblk[b], cmk_ref[b].astype(BF16)))

    pn, p_new, pc = [], [], []
    for b in seqs:
        s_new = jnp.sum(qblk[b].astype(F32) * zk[b:b + 1, :].astype(BF16).astype(F32), axis=1, keepdims=True)
        m = jnp.maximum(jnp.maximum(jnp.max(s[b], axis=1, keepdims=True), s_new), sink)
        p = jnp.exp(s[b] - m)
        e_new = jnp.exp(s_new - m)
        inv = 1.0 / (jnp.sum(p, axis=1, keepdims=True) + e_new + jnp.exp(sink - m))
        pn.append((p * inv).astype(BF16))
        p_new.append((e_new * inv).astype(BF16).astype(F32))
        e = jnp.exp(sc[b] - jnp.max(sc[b], axis=1, keepdims=True))
        pc.append((e * (1.0 / jnp.sum(e, axis=1, keepdims=True))).astype(BF16))

    oa_rows = [[] for _ in range(GROUP)]
    oc_rows = []
    for b in seqs:
        vc = cv_ref[b]
        o = _dot_nt(pn[b], vc.astype(BF16)) + p_new[b] * zv[b:b + 1, :].astype(BF16).astype(F32)
        o = jnp.where(own, o, 0.0)
        for g in range(GROUP):
            oa_rows[g].append(jnp.sum(o[g * N_KV:(g + 1) * N_KV], axis=0, keepdims=True))
        ocb = jnp.where(own_c, _dot_nt(pc[b], cmv_ref[b].astype(BF16)), 0.0)
        oc_rows.append(jnp.sum(ocb, axis=0, keepdims=True))
        here = seq_lane == j * SB + b
        knew_col = jnp.sum(jnp.where(here, knew_t[...], 0.0), axis=1, keepdims=True)
        vnew_col = jnp.sum(jnp.where(here, vnew_t[...], 0.0), axis=1, keepdims=True)
        nk_ref[b] = jnp.where(last_pos, knew_col, pltpu.roll(ck_ref[b], WINDOW - 1, 1))
        nv_ref[b] = jnp.where(last_pos, vnew_col, pltpu.roll(vc, WINDOW - 1, 1))
    for g in range(GROUP):
        oa_ref[pl.ds(r0, SB), g * KV_WIDTH:(g + 1) * KV_WIDTH] = jnp.concatenate(oa_rows[g], axis=0)
    oc_ref[pl.ds(r0, SB), :] = jnp.concatenate(oc_rows, axis=0)


def _sample_attn(x, rope, wq, wrest, sink_gk, ck, cv, cmk, cmv):
    db = x.shape[0]
    blk = lambda r: pl.BlockSpec((SB, KV_WIDTH, r), lambda j: (j, 0, 0))
    full = lambda w: pl.BlockSpec((db, w), lambda j: (0, 0))
    return pl.pallas_call(
        _sample_attn_kernel,
        grid=(db // SB,),
        in_specs=[
            full(D_MODEL), _const_spec((3, 1, LANES)),
            _const_spec((D_MODEL, Q_WIDTH)), _const_spec((D_MODEL, IN_WIDTH - K0)),
            _const_spec((N_HEADS, LANES)),
            blk(WINDOW), blk(WINDOW), blk(N_MEM), blk(N_MEM),
        ],
        out_specs=[full(IN_WIDTH), full(Q_WIDTH), full(MEM_WIDTH), blk(WINDOW), blk(WINDOW)],
        out_shape=[
            jax.ShapeDtypeStruct((db, IN_WIDTH), F32),
            jax.ShapeDtypeStruct((db, Q_WIDTH), F32),
            jax.ShapeDtypeStruct((db, MEM_WIDTH), F32),
            jax.ShapeDtypeStruct((db, KV_WIDTH, WINDOW), F32),
            jax.ShapeDtypeStruct((db, KV_WIDTH, WINDOW), F32),
        ],
        scratch_shapes=[pltpu.VMEM((KV_WIDTH, db), F32), pltpu.VMEM((KV_WIDTH, db), F32)],
        compiler_params=pltpu.CompilerParams(dimension_semantics=("arbitrary",), vmem_limit_bytes=VMEM_LIMIT),
        name="sample_attn",
    )(x, rope, wq, wrest, sink_gk, ck, cv, cmk, cmv)


def _sample_tail_kernel(x_ref, z_ref, oa_ref, oc_ref, st_ref, wa_ref, wb_ref, wc_ref, wo_ref, wmix_ref,
                        pscale_ref, g1_ref, b1_ref, wr_ref, br_ref, wg_ref, wu_ref, wd_ref, g2_ref, b2_ref,
                        y_ref, npool_ref, h_sc, comb_sc, acc_sc):
    e = pl.program_id(0)

    @pl.when(e == 0)
    def _():
        u = z_ref[:, U0:U0 + POOL_WIDTH]
        npool_ref[0:POOL_STATE - 1] = st_ref[1:POOL_STATE]
        npool_ref[POOL_STATE - 1] = u
        obs = []
        for g, w in enumerate(POOL_WINDOWS):
            sl = slice(g * POOL_GROUP_DIM, (g + 1) * POOL_GROUP_DIM)
            cur = u[:, sl]
            ws = cur
            for jj in range(1, w):
                ws = ws + st_ref[POOL_STATE - jj, :, sl]
            cnt = float(min(PAST_LEN + 1, w))
            pooled = ws / cnt - cur
            obs.append(_dot(pooled.astype(BF16), wmix_ref[g]) * pscale_ref[:, sl])
        ob = jnp.concatenate(obs, axis=1)
        h = _merge_ln1(x_ref[...], oa_ref[...].astype(BF16), ob, oc_ref[...], z_ref[:, GZ0:GZ0 + 3 * D_MODEL],
                       wa_ref, wb_ref, wc_ref, wo_ref, g1_ref[...], b1_ref[...])
        h_sc[...] = h
        logits = _dot(h.astype(BF16), wr_ref[...]) + br_ref[...]
        hot1, hot2, w1, w2 = _route(logits)
        comb_sc[...] = jnp.where(hot1, w1, 0.0) + jnp.where(hot2, w2, 0.0)
        acc_sc[...] = jnp.zeros_like(acc_sc)

    out = _expert_mlp(h_sc[...].astype(BF16), wg_ref[...].astype(BF16), wu_ref[...].astype(BF16),
                      wd_ref[...].astype(BF16))
    lane = lax.broadcasted_iota(jnp.int32, comb_sc.shape, 1)
    ce = jnp.sum(jnp.where(lane == e, comb_sc[...], 0.0), axis=1, keepdims=True)
    acc_sc[...] += ce * out

    @pl.when(e == pl.num_programs(0) - 1)
    def _():
        y_ref[...] = _layer_norm(ALPHA * h_sc[...] + acc_sc[...], g2_ref[...], b2_ref[...])


def _sample_tail(x, z, oa, oc, state, wa, wb, wc, wo, wmix, pscale, g1, b1, wr, br, wg, wu, wd, g2, b2):
    db = x.shape[0]
    full = lambda w: pl.BlockSpec((db, w), lambda e: (0, 0))
    vec = lambda w: pl.BlockSpec((1, w), lambda e: (0, 0))
    hist = pl.BlockSpec((POOL_STATE, db, POOL_WIDTH), lambda e: (0, 0, 0))
    return pl.pallas_call(
        _sample_tail_kernel,
        grid=(N_EXPERTS,),
        in_specs=[
            full(D_MODEL), full(IN_WIDTH), full(Q_WIDTH), full(MEM_WIDTH), hist,
            _const_spec((Q_WIDTH, D_MODEL)), _const_spec((POOL_WIDTH, D_MODEL)),
            _const_spec((MEM_WIDTH, D_MODEL)), _const_spec((D_MODEL, D_MODEL)),
            _const_spec((len(POOL_WINDOWS), POOL_GROUP_DIM, POOL_GROUP_DIM)),
            vec(POOL_WIDTH), vec(D_MODEL), vec(D_MODEL),
            _const_spec((D_MODEL, LANES)), vec(LANES),
            pl.BlockSpec((None, D_MODEL, D_EXPERT), lambda e: (e, 0, 0)),
            pl.BlockSpec((None, D_MODEL, D_EXPERT), lambda e: (e, 0, 0)),
            pl.BlockSpec((None, D_EXPERT, D_MODEL), lambda e: (e, 0, 0)),
            vec(D_MODEL), vec(D_MODEL),
        ],
        out_specs=[full(D_MODEL), hist],
        out_shape=[jax.ShapeDtypeStruct((db, D_MODEL), F32),
                   jax.ShapeDtypeStruct((POOL_STATE, db, POOL_WIDTH), F32)],
        scratch_shapes=[pltpu.VMEM((db, D_MODEL), F32), pltpu.VMEM((db, LANES), F32),
                        pltpu.VMEM((db, D_MODEL), F32)],
        compiler_params=pltpu.CompilerParams(dimension_semantics=("arbitrary",), vmem_limit_bytes=VMEM_LIMIT),
        name="sample_tail",
    )(x, z, oa, oc, state, wa, wb, wc, wo, wmix, pscale, g1, b1, wr, br, wg, wu, wd, g2, b2)


def _rope_tables(pos):
    half = ROPE_DIM // 2
    inv = jnp.power(ROPE_THETA, -jnp.arange(half, dtype=F32) * (2.0 / ROPE_DIM))
    ang = pos.astype(F32)[:, None] * inv[None, :]
    lane = np.arange(LANES)
    off = lane % HEAD_DIM
    cos = jnp.cos(ang)[:, lane % half]
    sin = jnp.sin(ang)[:, lane % half]
    c = jnp.where(off[None, :] < ROPE_DIM, cos, 1.0)
    s1 = jnp.where((off[None, :] >= half) & (off[None, :] < ROPE_DIM), sin, 0.0)
    s2 = jnp.where(off[None, :] < half, -sin, 0.0)
    return jnp.stack([c, s1, s2]).astype(F32)


def _q_heads_group_major(w, axis):
    if axis == 1:
        n = w.shape[0]
        return w.reshape(n, N_KV, GROUP, HEAD_DIM).transpose(0, 2, 1, 3).reshape(n, Q_WIDTH)
    n = w.shape[1]
    return w.reshape(N_KV, GROUP, HEAD_DIM, n).transpose(1, 0, 2, 3).reshape(Q_WIDTH, n)


def kernel(x_prompt, x_sample, cache_win_k, cache_win_v, state_pool, cache_mem_k, cache_mem_v, mem_prompt, w_in, sinks, w_pool_mix, pool_scale, w_mem_k, w_mem_v, w_branch_a, w_branch_b, w_branch_c, w_out, ln1_g, ln1_b, w_group, b_group, w_router, b_router, w_gate, w_up, w_down, ln2_g, ln2_b):
    assert w_in.shape[0] == DEPTH == 1
    b, l, _ = x_prompt.shape
    db, ds, _ = x_sample.shape
    assert ds == 1 and l % TM == 0 and db % SB == 0
    assert cache_win_k.shape[2] == WINDOW
    t = b * l

    win = w_in[0]
    wq = (_q_heads_group_major(win[:, Q0:Q0 + Q_WIDTH], 1) * ATT_SCALE).astype(BF16)
    wrest = win[:, K0:].astype(BF16)
    wa = _q_heads_group_major(w_branch_a[0], 0).astype(BF16)
    wb = w_branch_b[0].astype(BF16)
    wc = w_branch_c[0].astype(BF16)
    wo = w_out[0].astype(BF16)
    wmix = w_pool_mix[0].astype(BF16)
    pscale = pool_scale[0].reshape(1, POOL_WIDTH)
    g1 = ln1_g[0].reshape(1, D_MODEL)
    b1 = ln1_b[0].reshape(1, D_MODEL)
    g2 = ln2_g[0].reshape(1, D_MODEL)
    b2 = ln2_b[0].reshape(1, D_MODEL)
    wr = jnp.concatenate([w_group[0], w_router[0].reshape(D_MODEL, N_EXPERTS)], axis=1)
    wr = jnp.pad(wr, ((0, 0), (0, LANES - wr.shape[1]))).astype(BF16)
    br = jnp.pad(jnp.concatenate([b_group[0], b_router[0].reshape(N_EXPERTS)]), (0, LANES - N_EXPERT_GROUPS - N_EXPERTS))
    br = br.reshape(1, LANES).astype(F32)
    wg = w_gate[0]
    wu = w_up[0]
    wd = w_down[0]
    sink = sinks[0].astype(F32)
    sink_gk = jnp.broadcast_to(sink.reshape(N_KV, GROUP).T.reshape(N_HEADS, 1), (N_HEADS, LANES))

    mk, mv = _mem_project(mem_prompt, w_mem_k[0].astype(BF16), w_mem_v[0].astype(BF16))
    rope_p = _rope_tables(jnp.arange(l, dtype=jnp.int32))
    h, xs, route, counts, nk_p, nv_p, npool_p = _front(
        x_prompt, rope_p, sink, wq, wrest, wa, wb, wc, wo, wmix, pscale, mk, mv, g1, b1, wr, br)
    piece_src, piece_dst, tile_expert, tile_active = _piece_tables(
        counts.reshape(-1, SUBLANES, LANES)[:, 0, :N_EXPERTS].astype(jnp.int32))
    ys = _grouped_gemm(piece_src, piece_dst, tile_expert, tile_active, xs, wg, wu, wd)
    y_p = _combine(ys, h.reshape(t, D_MODEL), route.reshape(t, LANES), g2, b2).reshape(b, l, D_MODEL)

    rope_s = _rope_tables(jnp.full((1,), PAST_LEN, jnp.int32))
    xs = x_sample.reshape(db, D_MODEL)
    feat_major = lambda c: jnp.transpose(c[0], (0, 2, 3, 1)).reshape(db, KV_WIDTH, c.shape[2])
    from_feat_major = lambda a: jnp.transpose(a.reshape(db, N_KV, HEAD_DIM, a.shape[2]), (0, 3, 1, 2))[None]
    z_s, oa_s, oc_s, nk_s, nv_s = _sample_attn(xs, rope_s, wq, wrest, sink_gk, feat_major(cache_win_k),
                                               feat_major(cache_win_v), feat_major(cache_mem_k), feat_major(cache_mem_v))
    state = jnp.transpose(state_pool[0], (1, 0, 2))
    y_s, npool_s = _sample_tail(xs, z_s, oa_s, oc_s, state, wa, wb, wc, wo, wmix, pscale, g1, b1, wr, br,
                                wg, wu, wd, g2, b2)

    kv5 = lambda a, n, w: a.reshape(1, n, w, N_KV, HEAD_DIM)
    return (y_p, y_s.reshape(db, 1, D_MODEL),
            kv5(nk_p, b, QB), kv5(nv_p, b, QB),
            npool_p[:, 2 * SUBLANES - POOL_STATE:, :][None],
            kv5(mk, b, N_MEM), kv5(mv, b, N_MEM),
            from_feat_major(nk_s), from_feat_major(nv_s),
            jnp.transpose(npool_s, (1, 0, 2))[None])
```

```python
import functools

import jax
import jax.numpy as jnp
import numpy as np
from jax import lax
from jax.experimental import pallas as pl
from jax.experimental.pallas import tpu as pltpu

D_MODEL = 1024
N_HEADS = 16
HEAD_DIM = 64
N_KV = 4
GROUP = N_HEADS // N_KV
WINDOW = 128
ROPE_THETA = 500000.0
ROPE_DIM = HEAD_DIM // 4
Q_WIDTH = N_HEADS * HEAD_DIM
KV_WIDTH = N_KV * HEAD_DIM
POOL_WINDOWS = (2, 4, 8, 16)
POOL_WIDTH = D_MODEL // 2
POOL_GROUP_DIM = POOL_WIDTH // len(POOL_WINDOWS)
POOL_STATE = max(POOL_WINDOWS) - 1
N_MEM = 256
MEM_HEADS = 4
MEM_WIDTH = MEM_HEADS * HEAD_DIM
N_EXPERT_GROUPS = 4
EXPERTS_PER_GROUP = 4
N_EXPERTS = N_EXPERT_GROUPS * EXPERTS_PER_GROUP
D_EXPERT = 512
PAST_LEN = 16384
DEPTH = 1
ALPHA = (2.0 * DEPTH) ** 0.25
LN_EPS = 1e-5

Q0 = 0
K0 = Q0 + Q_WIDTH
V0 = K0 + KV_WIDTH
U0 = V0 + KV_WIDTH
CQ0 = U0 + POOL_WIDTH
GZ0 = CQ0 + MEM_WIDTH
IN_WIDTH = GZ0 + 3 * D_MODEL

LANES = 128
SUBLANES = 8
VMEM_LIMIT = 56 * 1024 * 1024

TM = 256
QB = WINDOW
PIECE = 16
PIECES_PER_TILE = 32
TG = PIECE * PIECES_PER_TILE
MAX_CHUNK_PIECES = 2 * TM // PIECE + N_EXPERTS - 1
N_SPARE = 2 * (PIECES_PER_TILE - 1) + 1
IN_SLOTS = 3


def _chunk_rows(n_chunks):
    spare = -(-N_SPARE // n_chunks)
    return -(-(MAX_CHUNK_PIECES + spare) * PIECE // LANES) * LANES
SB = 8
IN_CHUNK = 768
Q_CHUNK = 512
ATT_SCALE = HEAD_DIM ** -0.5
SOFTMAX_ROWS = 64
assert N_MEM == 2 * QB and MEM_HEADS == N_KV and MEM_WIDTH == KV_WIDTH and TM <= GROUP * QB

BF16 = jnp.bfloat16
F32 = jnp.float32
NEG_INF = float("-inf")


def _const_spec(shape):
    nd = len(shape)
    return pl.BlockSpec(shape, lambda *_: (0,) * nd, pipeline_mode=pl.Buffered(1))


def _layer_norm(x, g, b):
    mu = jnp.mean(x, axis=-1, keepdims=True)
    xc = x - mu
    var = jnp.mean(xc * xc, axis=-1, keepdims=True)
    return xc * lax.rsqrt(var + LN_EPS) * g + b


def _dot(a, b):
    return jnp.dot(a, b, preferred_element_type=F32)


def _dot_nt(a, b):
    return lax.dot_general(a, b, (((1,), (1,)), ((), ())), preferred_element_type=F32)


def _lane_block_mask(shape, block, width=HEAD_DIM):
    lane = lax.broadcasted_iota(jnp.int32, shape, len(shape) - 1)
    return (lane >= block * width) & (lane < (block + 1) * width)


def _rope(x, c, s1, s2):
    half = ROPE_DIM // 2
    return x * c + pltpu.roll(x, half, 1) * s1 + pltpu.roll(x, LANES - half, 1) * s2


def _route(logits):
    rows = logits.shape[0]
    lane = lax.broadcasted_iota(jnp.int32, (rows, LANES), 1)
    lanef = lane.astype(F32)
    big = float(LANES)
    is_g = lane < N_EXPERT_GROUPS
    glog = jnp.where(is_g, logits, NEG_INF)
    gmax = jnp.max(glog, axis=1, keepdims=True)
    gsum = jnp.sum(jnp.where(is_g, jnp.exp(glog - gmax), 0.0), axis=1, keepdims=True)
    gp = 1.0 / gsum
    gidx = jnp.min(jnp.where(glog == gmax, lanef, big), axis=1, keepdims=True).astype(jnp.int32)
    lo = N_EXPERT_GROUPS + gidx * EXPERTS_PER_GROUP
    in_grp = (lane >= lo) & (lane < lo + EXPERTS_PER_GROUP)
    el = jnp.where(in_grp, logits, NEG_INF)
    v1 = jnp.max(el, axis=1, keepdims=True)
    i1 = jnp.min(jnp.where(el == v1, lanef, big), axis=1, keepdims=True).astype(jnp.int32)
    el2 = jnp.where(lane == i1, NEG_INF, el)
    v2 = jnp.max(el2, axis=1, keepdims=True)
    i2 = jnp.min(jnp.where(el2 == v2, lanef, big), axis=1, keepdims=True).astype(jnp.int32)
    e21 = jnp.exp(v2 - v1)
    inv = 1.0 / (1.0 + e21)
    w1 = inv * gp
    w2 = e21 * inv * gp
    e1 = i1 - N_EXPERT_GROUPS
    e2 = i2 - N_EXPERT_GROUPS
    return lane == e1, lane == e2, w1, w2


def _local_sort(hot1, hot2, w1, w2, hb, cap):
    rows = hb.shape[0]
    lane = lax.broadcasted_iota(jnp.int32, (rows, LANES), 1)
    onehot = jnp.where(hot1 | hot2, 1.0, 0.0)
    counts = jnp.sum(onehot, axis=0, keepdims=True)
    before = (lax.broadcasted_iota(jnp.int32, (rows, rows), 1)
              < lax.broadcasted_iota(jnp.int32, (rows, rows), 0)).astype(BF16)
    rank = _dot(before, onehot.astype(BF16))
    run = (((counts.astype(jnp.int32) + (PIECE - 1)) // PIECE) * PIECE).astype(F32)
    lower = (lax.broadcasted_iota(jnp.int32, (LANES, LANES), 0)
             < lax.broadcasted_iota(jnp.int32, (LANES, LANES), 1)).astype(BF16)
    start = _dot(jnp.broadcast_to(run, (SUBLANES, LANES)).astype(BF16), lower)[0:1]
    slot = start + rank
    s1 = jnp.sum(jnp.where(hot1, slot, 0.0), axis=1, keepdims=True)
    s2 = jnp.sum(jnp.where(hot2, slot, 0.0), axis=1, keepdims=True)
    route = jnp.where(lane == 0, w1, jnp.where(lane == 1, w2, jnp.where(lane == 2, s1, jnp.where(lane == 3, s2, 0.0))))
    route_t = route.T
    srow = lax.broadcasted_iota(jnp.int32, (cap, rows), 0).astype(F32)
    perm = jnp.where((srow == route_t[2:3, :]) | (srow == route_t[3:4, :]), 1.0, 0.0).astype(BF16)
    return _dot(perm, hb).astype(BF16), route, counts


def _sigmoid(x):
    return 0.5 * jnp.tanh(0.5 * x) + 0.5


def _merge_ln1(x, oa, ob, oc, gz, wa_ref, wb_ref, wc_ref, wo_ref, g1, b1):
    ya = _dot(oa, wa_ref[...])
    yb = _dot(ob.astype(BF16), wb_ref[...])
    yc = _dot(oc.astype(BF16), wc_ref[...])
    m = (_sigmoid(gz[:, 0:D_MODEL]) * ya
         + _sigmoid(gz[:, D_MODEL:2 * D_MODEL]) * yb
         + _sigmoid(gz[:, 2 * D_MODEL:3 * D_MODEL]) * yc)
    hpre = ALPHA * x + _dot(m.astype(BF16), wo_ref[...])
    return _layer_norm(hpre, g1, b1)


def _mem_kernel(mem_ref, wk_ref, wv_ref, mk_ref, mv_ref):
    m = mem_ref[...].astype(BF16)
    mk_ref[...] = _dot(m, wk_ref[...])
    mv_ref[...] = _dot(m, wv_ref[...])


def _mem_project(mem, wk, wv):
    b = mem.shape[0]
    out = jax.ShapeDtypeStruct((b, N_MEM, MEM_WIDTH), F32)
    return pl.pallas_call(
        _mem_kernel,
        grid=(b,),
        in_specs=[pl.BlockSpec((None, N_MEM, D_MODEL), lambda i: (i, 0, 0)),
                  _const_spec((D_MODEL, MEM_WIDTH)), _const_spec((D_MODEL, MEM_WIDTH))],
        out_specs=[pl.BlockSpec((None, N_MEM, MEM_WIDTH), lambda i: (i, 0, 0))] * 2,
        out_shape=[out, out],
        name="mem_project",
    )(mem, wk, wv)


def _project(xb, wq_ref, wrest_ref, z_ref):
    for c0 in range(0, Q_WIDTH, Q_CHUNK):
        z_ref[:, Q0 + c0:Q0 + c0 + Q_CHUNK] = _dot(xb, wq_ref[:, c0:c0 + Q_CHUNK])
    rest = IN_WIDTH - K0
    for c0 in range(0, rest, IN_CHUNK):
        c1 = min(c0 + IN_CHUNK, rest)
        z_ref[:, K0 + c0:K0 + c1] = _dot(xb, wrest_ref[:, c0:c1])


def _front_kernel(sinks_ref, x_ref, rope_ref, wq_ref, wrest_ref, wa_ref, wb_ref, wc_ref, wo_ref, wmix_ref,
                  pscale_ref, mk_ref, mv_ref, g1_ref, b1_ref, wr_ref, br_ref,
                  h_ref, xs_ref, route_ref, counts_ref, nk_ref, nv_ref, npool_ref,
                  z_ref, qb_ref, kext_ref, vext_ref, uext_ref, oa_ref, ob_ref, bias_ref, s_ref, p_ref, vblk_ref):
    i = pl.program_id(1)
    x = x_ref[...]
    xb = x.astype(BF16)
    hist = 2 * SUBLANES

    @pl.when(i == 0)
    def _():
        kext_ref[0:QB, :] = jnp.zeros((QB, KV_WIDTH), BF16)
        vext_ref[0:QB, :] = jnp.zeros((QB, KV_WIDTH), BF16)
        uext_ref[0:hist, :] = jnp.zeros((hist, POOL_WIDTH), F32)

    @pl.when(i > 0)
    def _():
        kext_ref[0:QB, :] = kext_ref[TM:TM + QB, :]
        vext_ref[0:QB, :] = vext_ref[TM:TM + QB, :]
        uext_ref[0:hist, :] = uext_ref[TM:TM + hist, :]

    _project(xb, wq_ref, wrest_ref, z_ref)

    c = rope_ref[0]
    s1 = rope_ref[1]
    s2 = rope_ref[2]
    for j in range(Q_WIDTH // LANES):
        sl = slice(Q0 + j * LANES, Q0 + (j + 1) * LANES)
        qb_ref[:, j * LANES:(j + 1) * LANES] = _rope(z_ref[:, sl], c, s1, s2).astype(BF16)
    for j in range(KV_WIDTH // LANES):
        sl = slice(K0 + j * LANES, K0 + (j + 1) * LANES)
        kr = _rope(z_ref[:, sl], c, s1, s2)
        z_ref[:, sl] = kr
        kext_ref[QB:QB + TM, j * LANES:(j + 1) * LANES] = kr.astype(BF16)
    vext_ref[QB:QB + TM, :] = z_ref[:, V0:V0 + KV_WIDTH].astype(BF16)
    uext_ref[hist:hist + TM, :] = z_ref[:, U0:U0 + POOL_WIDTH]
    nk_ref[...] = z_ref[TM - QB:TM, K0:K0 + KV_WIDTH]
    nv_ref[...] = z_ref[TM - QB:TM, V0:V0 + KV_WIDTH]

    rowq = lax.broadcasted_iota(jnp.int32, (QB, 2 * QB), 0)
    colk = lax.broadcasted_iota(jnp.int32, (QB, 2 * QB), 1)
    band = (colk >= rowq) & (colk <= rowq + WINDOW)
    bias_ref[1] = jnp.where(band, 0.0, NEG_INF)
    bias_ref[0] = jnp.where(band & ((colk >= QB) | (i > 0)), 0.0, NEG_INF)
    for sb in range(TM // QB):
        k2 = kext_ref[sb * QB:(sb + 2) * QB, :]
        v2 = vext_ref[sb * QB:(sb + 2) * QB, :]
        qs = jnp.concatenate(
            [qb_ref[sb * QB:(sb + 1) * QB, g * KV_WIDTH:(g + 1) * KV_WIDTH] for g in range(GROUP)], axis=0)
        for kv in range(N_KV):
            kmask = _lane_block_mask((2 * QB, KV_WIDTH), kv)
            s_ref[...] = _dot_nt(qs, jnp.where(kmask, k2, jnp.zeros_like(k2)))
            vblk_ref[kv * 2 * QB:(kv + 1) * 2 * QB, :] = jnp.where(kmask, v2, jnp.zeros_like(v2))
            for c0 in range(0, GROUP * QB, SOFTMAX_ROWS):
                rq = c0 % QB
                sink = sinks_ref[kv * GROUP + c0 // QB]
                s = s_ref[c0:c0 + SOFTMAX_ROWS, :] + bias_ref[min(sb, 1), rq:rq + SOFTMAX_ROWS, :]
                m = jnp.maximum(jnp.max(s, axis=1, keepdims=True), sink)
                p = jnp.exp(s - m)
                den = jnp.sum(p, axis=1, keepdims=True) + jnp.exp(sink - m)
                p_ref[c0:c0 + SOFTMAX_ROWS, kv * 2 * QB:(kv + 1) * 2 * QB] = (p * (1.0 / den)).astype(BF16)
        o = _dot(p_ref[...], vblk_ref[...])
        for g in range(GROUP):
            oa_ref[sb * QB:(sb + 1) * QB, g * KV_WIDTH:(g + 1) * KV_WIDTH] = o[g * QB:(g + 1) * QB].astype(BF16)

    npool_ref[...] = uext_ref[TM:TM + hist, :]
    pos = i * TM + lax.broadcasted_iota(jnp.int32, (TM, 1), 0)
    for g, w in enumerate(POOL_WINDOWS):
        sl = slice(g * POOL_GROUP_DIM, (g + 1) * POOL_GROUP_DIM)
        cur = uext_ref[hist:hist + TM, sl]
        ws = cur
        for j in range(1, w):
            ws = ws + uext_ref[hist - j:hist - j + TM, sl]
        cnt = jnp.minimum(pos + 1, w).astype(F32)
        pooled = ws / cnt - cur
        ob_ref[:, sl] = _dot(pooled.astype(BF16), wmix_ref[g]) * pscale_ref[:, sl]

    cq = (z_ref[:, CQ0:CQ0 + MEM_WIDTH] * ATT_SCALE).astype(BF16)
    mk = mk_ref[...].astype(BF16)
    mv = mv_ref[...].astype(BF16)
    for hh in range(MEM_HEADS):
        hmask = _lane_block_mask((N_MEM, MEM_WIDTH), hh)
        s_ref[0:TM, :] = _dot_nt(cq, jnp.where(hmask, mk, jnp.zeros_like(mk)))
        vblk_ref[hh * N_MEM:(hh + 1) * N_MEM, :] = jnp.where(hmask, mv, jnp.zeros_like(mv))
        for c0 in range(0, TM, SOFTMAX_ROWS):
            s = s_ref[c0:c0 + SOFTMAX_ROWS, :]
            p = jnp.exp(s - jnp.max(s, axis=1, keepdims=True))
            den = jnp.sum(p, axis=1, keepdims=True)
            p_ref[c0:c0 + SOFTMAX_ROWS, hh * N_MEM:(hh + 1) * N_MEM] = (p * (1.0 / den)).astype(BF16)
    oc = _dot(p_ref[0:TM, :], vblk_ref[...])

    h = _merge_ln1(x, oa_ref[...], ob_ref[...], oc, z_ref[:, GZ0:GZ0 + 3 * D_MODEL],
                   wa_ref, wb_ref, wc_ref, wo_ref, g1_ref[...], b1_ref[...])
    h_ref[...] = h
    hb = h.astype(BF16)
    logits = _dot(hb, wr_ref[...]) + br_ref[...]
    xs, route, counts = _local_sort(*_route(logits), hb, xs_ref.shape[0])
    xs_ref[...] = xs
    route_ref[...] = route
    counts_ref[...] = jnp.broadcast_to(counts, (SUBLANES, LANES))


def _front(x, rope, sinks, wq, wrest, wa, wb, wc, wo, wmix, pscale, mk, mv, g1, b1, wr, br):
    b, l, _ = x.shape
    nt = l // TM
    cap = _chunk_rows(b * nt)
    hist = 2 * SUBLANES
    tile = lambda w: pl.BlockSpec((None, TM, w), lambda bi, ti: (bi, ti, 0))
    per_b = lambda r, w: pl.BlockSpec((None, r, w), lambda bi, ti: (bi, 0, 0))
    return pl.pallas_call(
        _front_kernel,
        grid=(b, nt),
        in_specs=[
            pl.BlockSpec(memory_space=pltpu.SMEM),
            tile(D_MODEL),
            pl.BlockSpec((3, TM, LANES), lambda bi, ti: (0, ti, 0)),
            _const_spec((D_MODEL, Q_WIDTH)), _const_spec((D_MODEL, IN_WIDTH - K0)),
            _const_spec((Q_WIDTH, D_MODEL)), _const_spec((POOL_WIDTH, D_MODEL)),
            _const_spec((MEM_WIDTH, D_MODEL)), _const_spec((D_MODEL, D_MODEL)),
            _const_spec((len(POOL_WINDOWS), POOL_GROUP_DIM, POOL_GROUP_DIM)),
            _const_spec((1, POOL_WIDTH)),
            per_b(N_MEM, MEM_WIDTH), per_b(N_MEM, MEM_WIDTH),
            _const_spec((1, D_MODEL)), _const_spec((1, D_MODEL)),
            _const_spec((D_MODEL, LANES)), _const_spec((1, LANES)),
        ],
        out_specs=[
            tile(D_MODEL),
            pl.BlockSpec((cap, D_MODEL), lambda bi, ti: (bi * nt + ti, 0)),
            tile(LANES),
            pl.BlockSpec((None, None, SUBLANES, LANES), lambda bi, ti: (bi, ti, 0, 0)),
            per_b(QB, KV_WIDTH), per_b(QB, KV_WIDTH), per_b(hist, POOL_WIDTH),
        ],
        out_shape=[
            jax.ShapeDtypeStruct((b, l, D_MODEL), F32),
            jax.ShapeDtypeStruct((b * nt * cap, D_MODEL), BF16),
            jax.ShapeDtypeStruct((b, l, LANES), F32),
            jax.ShapeDtypeStruct((b, nt, SUBLANES, LANES), F32),
            jax.ShapeDtypeStruct((b, QB, KV_WIDTH), F32),
            jax.ShapeDtypeStruct((b, QB, KV_WIDTH), F32),
            jax.ShapeDtypeStruct((b, hist, POOL_WIDTH), F32),
        ],
        scratch_shapes=[
            pltpu.VMEM((TM, IN_WIDTH), F32),
            pltpu.VMEM((TM, Q_WIDTH), BF16),
            pltpu.VMEM((QB + TM, KV_WIDTH), BF16),
            pltpu.VMEM((QB + TM, KV_WIDTH), BF16),
            pltpu.VMEM((hist + TM, POOL_WIDTH), F32),
            pltpu.VMEM((TM, Q_WIDTH), BF16),
            pltpu.VMEM((TM, POOL_WIDTH), F32),
            pltpu.VMEM((2, QB, 2 * QB), F32),
            pltpu.VMEM((GROUP * QB, 2 * QB), F32),
            pltpu.VMEM((GROUP * QB, N_KV * 2 * QB), BF16),
            pltpu.VMEM((N_KV * 2 * QB, KV_WIDTH), BF16),
        ],
        compiler_params=pltpu.CompilerParams(
            dimension_semantics=("arbitrary", "arbitrary"), vmem_limit_bytes=VMEM_LIMIT),
        name="front_prompt",
    )(sinks, x, rope, wq, wrest, wa, wb, wc, wo, wmix, pscale, mk, mv, g1, b1, wr, br)


def _expert_mlp(xb, wg, wu, wd):
    a = _dot(xb, wg)
    hid = (a * jax.nn.sigmoid(a)) * _dot(xb, wu)
    return _dot(hid.astype(BF16), wd)


def _gemm_kernel(src_ref, dst_ref, te_ref, act_ref, xs_ref, wg_ref, wu_ref, wd_ref, ys_ref,
                 xbuf, obuf, wgb, wub, wdb, prime, sem_in, sem_out):
    i = pl.program_id(0)
    n = pl.num_programs(0)
    slot = i % 2
    in_slot = i % IN_SLOTS

    def start_in(tile, slot):
        for j in range(PIECES_PER_TILE):
            row0 = pl.multiple_of(src_ref[tile * PIECES_PER_TILE + j], PIECE)
            pltpu.make_async_copy(xs_ref.at[pl.ds(row0, PIECE)], xbuf.at[slot, pl.ds(j * PIECE, PIECE)],
                                  sem_in.at[slot]).start()

    def start_out(tile, slot):
        for j in range(PIECES_PER_TILE):
            row0 = pl.multiple_of(dst_ref[tile * PIECES_PER_TILE + j], PIECE)
            pltpu.make_async_copy(obuf.at[slot, pl.ds(j * PIECE, PIECE)], ys_ref.at[pl.ds(row0, PIECE)],
                                  sem_out.at[slot]).start()

    def wait_in(slot):
        for j in range(PIECES_PER_TILE):
            pltpu.make_async_copy(xs_ref.at[pl.ds(0, PIECE)], xbuf.at[slot, pl.ds(j * PIECE, PIECE)],
                                  sem_in.at[slot]).wait()

    def wait_out(slot):
        for j in range(PIECES_PER_TILE):
            pltpu.make_async_copy(obuf.at[slot, pl.ds(j * PIECE, PIECE)], ys_ref.at[pl.ds(0, PIECE)],
                                  sem_out.at[slot]).wait()

    active = act_ref[i] > 0
    ahead = IN_SLOTS - 1
    prefetched = (i < ahead) | (act_ref[jnp.maximum(i - ahead, 0)] > 0)
    out_pending = (i < 2) | (act_ref[jnp.maximum(i - 2, 0)] > 0)

    @pl.when(i == 0)
    def _():
        prime[0] = jnp.zeros((PIECE, D_MODEL), BF16)
        for s in range(2):
            for j in range(PIECES_PER_TILE):
                pltpu.make_async_copy(prime.at[0], prime.at[1 + s * PIECES_PER_TILE + j], sem_out.at[s]).start()
        for t in range(ahead):
            start_in(t, t)

    prev_active = (i >= 1) & (act_ref[jnp.maximum(i - 1, 0)] > 0)

    def active_step(after_first):
        @pl.when((i == 0) | (te_ref[i] != te_ref[jnp.maximum(i - 1, 0)]))
        def _():
            wgb[...] = wg_ref[...].astype(BF16)
            wub[...] = wu_ref[...].astype(BF16)
            wdb[...] = wd_ref[...].astype(BF16)

        wait_in(in_slot)
        wait_out(slot)
        out = _expert_mlp(xbuf[in_slot], wgb[...], wub[...], wdb[...])
        start_in(i + ahead, (i + ahead) % IN_SLOTS)
        if after_first:
            start_out(i - 1, 1 - slot)
        obuf[slot] = out.astype(BF16)

    @pl.when(active & (i == 0))
    def _():
        active_step(False)

    @pl.when(active & (i > 0))
    def _():
        active_step(True)

    @pl.when(jnp.logical_not(active))
    def _():
        @pl.when(prev_active)
        def _():
            start_out(i - 1, 1 - slot)

        @pl.when(prefetched)
        def _():
            wait_in(in_slot)

        @pl.when(out_pending)
        def _():
            wait_out(slot)

    @pl.when(i == n - 1)
    def _():
        for t in range(ahead):
            @pl.when(act_ref[jnp.maximum(i - t, 0)] > 0)
            def _(t=t):
                wait_in((i - t + ahead) % IN_SLOTS)

        @pl.when(active)
        def _():
            start_out(i, slot)
            wait_out(slot)

        @pl.when(prev_active)
        def _():
            wait_out(1 - slot)


def _grouped_gemm(piece_src, piece_dst, tile_expert, tile_active, xs, wg, wu, wd):
    n_tiles = tile_expert.shape[0]
    assert n_tiles >= 2
    wspec = lambda r, c: pl.BlockSpec((None, r, c), lambda i, src, dst, te, act: (te[i], 0, 0))
    return pl.pallas_call(
        _gemm_kernel,
        grid_spec=pltpu.PrefetchScalarGridSpec(
            num_scalar_prefetch=4,
            grid=(n_tiles,),
            in_specs=[pl.BlockSpec(memory_space=pl.ANY),
                      wspec(D_MODEL, D_EXPERT), wspec(D_MODEL, D_EXPERT), wspec(D_EXPERT, D_MODEL)],
            out_specs=pl.BlockSpec(memory_space=pl.ANY),
            scratch_shapes=[pltpu.VMEM((IN_SLOTS, TG, D_MODEL), BF16), pltpu.VMEM((2, TG, D_MODEL), BF16),
                            pltpu.VMEM((D_MODEL, D_EXPERT), BF16), pltpu.VMEM((D_MODEL, D_EXPERT), BF16),
                            pltpu.VMEM((D_EXPERT, D_MODEL), BF16),
                            pltpu.VMEM((1 + 2 * PIECES_PER_TILE, PIECE, D_MODEL), BF16),
                            pltpu.SemaphoreType.DMA((IN_SLOTS,)), pltpu.SemaphoreType.DMA((2,))],
        ),
        out_shape=jax.ShapeDtypeStruct(xs.shape, xs.dtype),
        input_output_aliases={4: 0},
        compiler_params=pltpu.CompilerParams(dimension_semantics=("arbitrary",), vmem_limit_bytes=VMEM_LIMIT),
        name="moe_grouped_gemm",
    )(piece_src, piece_dst, tile_expert, tile_active, xs, wg, wu, wd)


def _combine_kernel(ys_ref, h_ref, route_ref, g2_ref, b2_ref, y_ref):
    chunks = h_ref.shape[0] // TM
    cap = ys_ref.shape[0] // chunks
    slot = lax.broadcasted_iota(jnp.int32, (TM, cap), 1).astype(F32)
    for c in range(chunks):
        rows = slice(c * TM, (c + 1) * TM)
        route = route_ref[rows, :]
        sel = jnp.concatenate([jnp.where(slot == route[:, 2:3], 1.0, 0.0).astype(BF16),
                               jnp.where(slot == route[:, 3:4], 1.0, 0.0).astype(BF16)], axis=0)
        picked = _dot(sel, ys_ref[c * cap:(c + 1) * cap, :])
        f = route[:, 0:1] * picked[0:TM] + route[:, 1:2] * picked[TM:2 * TM]
        y_ref[rows, :] = _layer_norm(ALPHA * h_ref[rows, :] + f, g2_ref[...], b2_ref[...])


def _combine(ys, h, route, g2, b2):
    t = h.shape[0]
    cap = ys.shape[0] // (t // TM)
    per_step = next(k for k in (4, 2, 1) if (t // TM) % k == 0)
    rows = per_step * TM
    return pl.pallas_call(
        _combine_kernel,
        grid=(t // rows,),
        in_specs=[
            pl.BlockSpec((per_step * cap, D_MODEL), lambda i: (i, 0)),
            pl.BlockSpec((rows, D_MODEL), lambda i: (i, 0)),
            pl.BlockSpec((rows, LANES), lambda i: (i, 0)),
            pl.BlockSpec((1, D_MODEL), lambda i: (0, 0)),
            pl.BlockSpec((1, D_MODEL), lambda i: (0, 0)),
        ],
        out_specs=pl.BlockSpec((rows, D_MODEL), lambda i: (i, 0)),
        out_shape=jax.ShapeDtypeStruct((t, D_MODEL), F32),
        compiler_params=pltpu.CompilerParams(dimension_semantics=("arbitrary",)),
        name="moe_combine",
    )(ys, h, route, g2, b2)


def _select(table, idx):
    hot = idx[:, None] == jnp.arange(table.shape[0], dtype=jnp.int32)[None, :]
    return jnp.sum(jnp.where(hot[:, :, None], table[None, :, :], 0), axis=1)


def _piece_tables(counts):
    n_chunks = counts.shape[0]
    n_tiles = -(-(n_chunks * MAX_CHUNK_PIECES + N_EXPERTS * (PIECES_PER_TILE - 1)) // PIECES_PER_TILE)
    npc = (counts + (PIECE - 1)) // PIECE
    first = (jnp.cumsum(npc, axis=1) - npc).T
    npc_t = npc.T
    cum = jnp.cumsum(npc_t, axis=1)
    per_expert = cum[:, -1]
    tiles_e = (per_expert + (PIECES_PER_TILE - 1)) // PIECES_PER_TILE
    tile_end = jnp.cumsum(tiles_e)
    tile_idx = jnp.arange(n_tiles, dtype=jnp.int32)
    expert_of = lambda i: jnp.minimum(jnp.sum(i[:, None] >= tile_end[None, :], axis=1), N_EXPERTS - 1).astype(jnp.int32)
    active = tile_idx < tile_end[-1]
    tile_expert = jnp.where(active, expert_of(tile_idx), expert_of(tile_end[-1:] - 1))
    meta = jnp.stack([tile_end - tiles_e, per_expert], axis=1)
    meta_t = _select(meta, tile_expert)
    k = (tile_idx - meta_t[:, 0])[:, None] * PIECES_PER_TILE + jnp.arange(PIECES_PER_TILE, dtype=jnp.int32)[None, :]
    valid = active[:, None] & (k < meta_t[:, 1:2])
    cum_t = _select(cum, tile_expert)
    chunk = jnp.minimum(jnp.sum(k[:, :, None] >= cum_t[:, None, :], axis=2), n_chunks - 1).astype(jnp.int32)
    at_chunk = chunk[:, :, None] == jnp.arange(n_chunks, dtype=jnp.int32)[None, None, :]
    pick = lambda tab: jnp.sum(jnp.where(at_chunk, _select(tab, tile_expert)[:, None, :], 0), axis=2)
    piece = pick(first) + k - pick(cum - npc_t)
    cap = _chunk_rows(n_chunks)
    rows = chunk * cap + piece * PIECE
    d = ((tile_idx % 2)[:, None] * (PIECES_PER_TILE - 1)
         + jnp.maximum(jnp.arange(PIECES_PER_TILE, dtype=jnp.int32)[None, :] - 1, 0))
    spare_row = lambda d: (d % n_chunks) * cap + (MAX_CHUNK_PIECES + d // n_chunks) * PIECE
    spare = spare_row(d)
    zero_piece = spare_row(N_SPARE - 1)
    extra = jnp.full(((IN_SLOTS - 1) * PIECES_PER_TILE,), zero_piece, jnp.int32)
    src = jnp.concatenate([jnp.where(valid, rows, zero_piece).astype(jnp.int32).reshape(-1), extra])
    dst = jnp.concatenate([jnp.where(valid, rows, spare).astype(jnp.int32).reshape(-1), extra])
    return src, dst, tile_expert.astype(jnp.int32), active.astype(jnp.int32)


def _sample_attn_kernel(x_ref, rope_ref, wq_ref, wrest_ref, sink_ref, ck_ref, cv_ref, cmk_ref, cmv_ref,
                        z_ref, oa_ref, oc_ref, nk_ref, nv_ref, knew_t, vnew_t):
    j = pl.program_id(0)
    db = x_ref.shape[0]

    @pl.when(j == 0)
    def _():
        _project(x_ref[...].astype(BF16), wq_ref, wrest_ref, z_ref)
        c = rope_ref[0]
        s1 = rope_ref[1]
        s2 = rope_ref[2]
        for jj in range((Q_WIDTH + KV_WIDTH) // LANES):
            sl = slice(jj * LANES, (jj + 1) * LANES)
            z_ref[:, sl] = _rope(z_ref[:, sl], c, s1, s2)
        knew_t[...] = z_ref[:, K0:K0 + KV_WIDTH].T
        vnew_t[...] = z_ref[:, V0:V0 + KV_WIDTH].T

    r0 = pl.multiple_of(j * SB, SB)
    zq = z_ref[pl.ds(r0, SB), Q0:Q0 + Q_WIDTH]
    zk = z_ref[pl.ds(r0, SB), K0:K0 + KV_WIDTH]
    zv = z_ref[pl.ds(r0, SB), V0:V0 + KV_WIDTH]
    zc = z_ref[pl.ds(r0, SB), CQ0:CQ0 + MEM_WIDTH] * ATT_SCALE
    sink = sink_ref[:, 0:1]
    row_kv = lax.broadcasted_iota(jnp.int32, (N_HEADS, KV_WIDTH), 0) & (N_KV - 1)
    lane_kv = lax.broadcasted_iota(jnp.int32, (N_HEADS, KV_WIDTH), 1) // HEAD_DIM
    own = row_kv == lane_kv
    row_c = lax.broadcasted_iota(jnp.int32, (N_HEADS, MEM_WIDTH), 0)
    lane_c = lax.broadcasted_iota(jnp.int32, (N_HEADS, MEM_WIDTH), 1) // HEAD_DIM
    own_c = row_c == lane_c
    last_pos = lax.broadcasted_iota(jnp.int32, (KV_WIDTH, WINDOW), 1) == WINDOW - 1
    seq_lane = lax.broadcasted_iota(jnp.int32, (KV_WIDTH, db), 1)

    seqs = range(SB)
    qblk, cblk, s, sc = [], [], [], []
    for b in seqs:
        q4 = jnp.concatenate(
            [jnp.broadcast_to(zq[b:b + 1, g * KV_WIDTH:(g + 1) * KV_WIDTH], (N_KV, KV_WIDTH)) for g in range(GROUP)],
            axis=0)
        qblk.append(jnp.where(own, q4, 0.0).astype(BF16))
        cblk.append(jnp.where(own_c, jnp.broadcast_to(zc[b:b + 1, :], (N_HEADS, MEM_WIDTH)), 0.0).astype(BF16))
        s.append(_dot(qblk[b], ck_ref[b].astype(BF16)))
        sc.append(_dot(cblk[b], cmk_ref[b].astype(BF16)))

    pn, p_new, pc = [], [], []
    for b in seqs:
        s_new = jnp.sum(qblk[b].astype(F32) * zk[b:b + 1, :].astype(BF16).astype(F32), axis=1, keepdims=True)
        m = jnp.maximum(jnp.maximum(jnp.max(s[b], axis=1, keepdims=True), s_new), sink)
        p = jnp.exp(s[b] - m)
        e_new = jnp.exp(s_new - m)
        inv = 1.0 / (jnp.sum(p, axis=1, keepdims=True) + e_new + jnp.exp(sink - m))
        pn.append((p * inv).astype(BF16))
        p_new.append((e_new * inv).astype(BF16).astype(F32))
        e = jnp.exp(sc[b] - jnp.max(sc[b], axis=1, keepdims=True))
        pc.append((e * (1.0 / jnp.sum(e, axis=1, keepdims=True))).astype(BF16))

    oa_rows = [[] for _ in range(GROUP)]
    oc_rows = []
    for b in seqs:
        vc = cv_ref[b]
        o = _dot_nt(pn[b], vc.astype(BF16)) + p_new[b] * zv[b:b + 1, :].astype(BF16).astype(F32)
        o = jnp.where(own, o, 0.0)
        for g in range(GROUP):
            oa_rows[g].append(jnp.sum(o[g * N_KV:(g + 1) * N_KV], axis=0, keepdims=True))
        ocb = jnp.where(own_c, _dot_nt(pc[b], cmv_ref[b].astype(BF16)), 0.0)
        oc_rows.append(jnp.sum(ocb, axis=0, keepdims=True))
        here = seq_lane == j * SB + b
        knew_col = jnp.sum(jnp.where(here, knew_t[...], 0.0), axis=1, keepdims=True)
        vnew_col = jnp.sum(jnp.where(here, vnew_t[...], 0.0), axis=1, keepdims=True)
        nk_ref[b] = jnp.where(last_pos, knew_col, pltpu.roll(ck_ref[b], WINDOW - 1, 1))
        nv_ref[b] = jnp.where(last_pos, vnew_col, pltpu.roll(vc, WINDOW - 1, 1))
    for g in range(GROUP):
        oa_ref[pl.ds(r0, SB), g * KV_WIDTH:(g + 1) * KV_WIDTH] = jnp.concatenate(oa_rows[g], axis=0)
    oc_ref[pl.ds(r0, SB), :] = jnp.concatenate(oc_rows, axis=0)


def _sample_attn(x, rope, wq, wrest, sink_gk, ck, cv, cmk, cmv):
    db = x.shape[0]
    blk = lambda r: pl.BlockSpec((SB, KV_WIDTH, r), lambda j: (j, 0, 0))
    full = lambda w: pl.BlockSpec((db, w), lambda j: (0, 0))
    return pl.pallas_call(
        _sample_attn_kernel,
        grid=(db // SB,),
        in_specs=[
            full(D_MODEL), _const_spec((3, 1, LANES)),
            _const_spec((D_MODEL, Q_WIDTH)), _const_spec((D_MODEL, IN_WIDTH - K0)),
            _const_spec((N_HEADS, LANES)),
            blk(WINDOW), blk(WINDOW), blk(N_MEM), blk(N_MEM),
        ],
        out_specs=[full(IN_WIDTH), full(Q_WIDTH), full(MEM_WIDTH), blk(WINDOW), blk(WINDOW)],
        out_shape=[
            jax.ShapeDtypeStruct((db, IN_WIDTH), F32),
            jax.ShapeDtypeStruct((db, Q_WIDTH), F32),
            jax.ShapeDtypeStruct((db, MEM_WIDTH), F32),
            jax.ShapeDtypeStruct((db, KV_WIDTH, WINDOW), F32),
            jax.ShapeDtypeStruct((db, KV_WIDTH, WINDOW), F32),
        ],
        scratch_shapes=[pltpu.VMEM((KV_WIDTH, db), F32), pltpu.VMEM((KV_WIDTH, db), F32)],
        compiler_params=pltpu.CompilerParams(dimension_semantics=("arbitrary",), vmem_limit_bytes=VMEM_LIMIT),
        name="sample_attn",
    )(x, rope, wq, wrest, sink_gk, ck, cv, cmk, cmv)


def _sample_tail_kernel(x_ref, z_ref, oa_ref, oc_ref, st_ref, wa_ref, wb_ref, wc_ref, wo_ref, wmix_ref,
                        pscale_ref, g1_ref, b1_ref, wr_ref, br_ref, wg_ref, wu_ref, wd_ref, g2_ref, b2_ref,
                        y_ref, npool_ref, h_sc, comb_sc, acc_sc):
    e = pl.program_id(0)

    @pl.when(e == 0)
    def _():
        u = z_ref[:, U0:U0 + POOL_WIDTH]
        npool_ref[0:POOL_STATE - 1] = st_ref[1:POOL_STATE]
        npool_ref[POOL_STATE - 1] = u
        obs = []
        for g, w in enumerate(POOL_WINDOWS):
            sl = slice(g * POOL_GROUP_DIM, (g + 1) * POOL_GROUP_DIM)
            cur = u[:, sl]
            ws = cur
            for jj in range(1, w):
                ws = ws + st_ref[POOL_STATE - jj, :, sl]
            cnt = float(min(PAST_LEN + 1, w))
            pooled = ws / cnt - cur
            obs.append(_dot(pooled.astype(BF16), wmix_ref[g]) * pscale_ref[:, sl])
        ob = jnp.concatenate(obs, axis=1)
        h = _merge_ln1(x_ref[...], oa_ref[...].astype(BF16), ob, oc_ref[...], z_ref[:, GZ0:GZ0 + 3 * D_MODEL],
                       wa_ref, wb_ref, wc_ref, wo_ref, g1_ref[...], b1_ref[...])
        h_sc[...] = h
        logits = _dot(h.astype(BF16), wr_ref[...]) + br_ref[...]
        hot1, hot2, w1, w2 = _route(logits)
        comb_sc[...] = jnp.where(hot1, w1, 0.0) + jnp.where(hot2, w2, 0.0)
        acc_sc[...] = jnp.zeros_like(acc_sc)

    out = _expert_mlp(h_sc[...].astype(BF16), wg_ref[...].astype(BF16), wu_ref[...].astype(BF16),
                      wd_ref[...].astype(BF16))
    lane = lax.broadcasted_iota(jnp.int32, comb_sc.shape, 1)
    ce = jnp.sum(jnp.where(lane == e, comb_sc[...], 0.0), axis=1, keepdims=True)
    acc_sc[...] += ce * out

    @pl.when(e == pl.num_programs(0) - 1)
    def _():
        y_ref[...] = _layer_norm(ALPHA * h_sc[...] + acc_sc[...], g2_ref[...], b2_ref[...])


def _sample_tail(x, z, oa, oc, state, wa, wb, wc, wo, wmix, pscale, g1, b1, wr, br, wg, wu, wd, g2, b2):
    db = x.shape[0]
    full = lambda w: pl.BlockSpec((db, w), lambda e: (0, 0))
    vec = lambda w: pl.BlockSpec((1, w), lambda e: (0, 0))
    hist = pl.BlockSpec((POOL_STATE, db, POOL_WIDTH), lambda e: (0, 0, 0))
    return pl.pallas_call(
        _sample_tail_kernel,
        grid=(N_EXPERTS,),
        in_specs=[
            full(D_MODEL), full(IN_WIDTH), full(Q_WIDTH), full(MEM_WIDTH), hist,
            _const_spec((Q_WIDTH, D_MODEL)), _const_spec((POOL_WIDTH, D_MODEL)),
            _const_spec((MEM_WIDTH, D_MODEL)), _const_spec((D_MODEL, D_MODEL)),
            _const_spec((len(POOL_WINDOWS), POOL_GROUP_DIM, POOL_GROUP_DIM)),
            vec(POOL_WIDTH), vec(D_MODEL), vec(D_MODEL),
            _const_spec((D_MODEL, LANES)), vec(LANES),
            pl.BlockSpec((None, D_MODEL, D_EXPERT), lambda e: (e, 0, 0)),
            pl.BlockSpec((None, D_MODEL, D_EXPERT), lambda e: (e, 0, 0)),
            pl.BlockSpec((None, D_EXPERT, D_MODEL), lambda e: (e, 0, 0)),
            vec(D_MODEL), vec(D_MODEL),
        ],
        out_specs=[full(D_MODEL), hist],
        out_shape=[jax.ShapeDtypeStruct((db, D_MODEL), F32),
                   jax.ShapeDtypeStruct((POOL_STATE, db, POOL_WIDTH), F32)],
        scratch_shapes=[pltpu.VMEM((db, D_MODEL), F32), pltpu.VMEM((db, LANES), F32),
                        pltpu.VMEM((db, D_MODEL), F32)],
        compiler_params=pltpu.CompilerParams(dimension_semantics=("arbitrary",), vmem_limit_bytes=VMEM_LIMIT),
        name="sample_tail",
    )(x, z, oa, oc, state, wa, wb, wc, wo, wmix, pscale, g1, b1, wr, br, wg, wu, wd, g2, b2)


def _rope_tables(pos):
    half = ROPE_DIM // 2
    inv = jnp.power(ROPE_THETA, -jnp.arange(half, dtype=F32) * (2.0 / ROPE_DIM))
    ang = pos.astype(F32)[:, None] * inv[None, :]
    lane = np.arange(LANES)
    off = lane % HEAD_DIM
    cos = jnp.cos(ang)[:, lane % half]
    sin = jnp.sin(ang)[:, lane % half]
    c = jnp.where(off[None, :] < ROPE_DIM, cos, 1.0)
    s1 = jnp.where((off[None, :] >= half) & (off[None, :] < ROPE_DIM), sin, 0.0)
    s2 = jnp.where(off[None, :] < half, -sin, 0.0)
    return jnp.stack([c, s1, s2]).astype(F32)


def _q_heads_group_major(w, axis):
    if axis == 1:
        n = w.shape[0]
        return w.reshape(n, N_KV, GROUP, HEAD_DIM).transpose(0, 2, 1, 3).reshape(n, Q_WIDTH)
    n = w.shape[1]
    return w.reshape(N_KV, GROUP, HEAD_DIM, n).transpose(1, 0, 2, 3).reshape(Q_WIDTH, n)


def kernel(x_prompt, x_sample, cache_win_k, cache_win_v, state_pool, cache_mem_k, cache_mem_v, mem_prompt, w_in, sinks, w_pool_mix, pool_scale, w_mem_k, w_mem_v, w_branch_a, w_branch_b, w_branch_c, w_out, ln1_g, ln1_b, w_group, b_group, w_router, b_router, w_gate, w_up, w_down, ln2_g, ln2_b):
    assert w_in.shape[0] == DEPTH == 1
    b, l, _ = x_prompt.shape
    db, ds, _ = x_sample.shape
    assert ds == 1 and l % TM == 0 and db % SB == 0
    assert cache_win_k.shape[2] == WINDOW
    t = b * l

    win = w_in[0]
    wq = (_q_heads_group_major(win[:, Q0:Q0 + Q_WIDTH], 1) * ATT_SCALE).astype(BF16)
    wrest = win[:, K0:].astype(BF16)
    wa = _q_heads_group_major(w_branch_a[0], 0).astype(BF16)
    wb = w_branch_b[0].astype(BF16)
    wc = w_branch_c[0].astype(BF16)
    wo = w_out[0].astype(BF16)
    wmix = w_pool_mix[0].astype(BF16)
    pscale = pool_scale[0].reshape(1, POOL_WIDTH)
    g1 = ln1_g[0].reshape(1, D_MODEL)
    b1 = ln1_b[0].reshape(1, D_MODEL)
    g2 = ln2_g[0].reshape(1, D_MODEL)
    b2 = ln2_b[0].reshape(1, D_MODEL)
    wr = jnp.concatenate([w_group[0], w_router[0].reshape(D_MODEL, N_EXPERTS)], axis=1)
    wr = jnp.pad(wr, ((0, 0), (0, LANES - wr.shape[1]))).astype(BF16)
    br = jnp.pad(jnp.concatenate([b_group[0], b_router[0].reshape(N_EXPERTS)]), (0, LANES - N_EXPERT_GROUPS - N_EXPERTS))
    br = br.reshape(1, LANES).astype(F32)
    wg = w_gate[0]
    wu = w_up[0]
    wd = w_down[0]
    sink = sinks[0].astype(F32)
    sink_gk = jnp.broadcast_to(sink.reshape(N_KV, GROUP).T.reshape(N_HEADS, 1), (N_HEADS, LANES))

    mk, mv = _mem_project(mem_prompt, w_mem_k[0].astype(BF16), w_mem_v[0].astype(BF16))
    rope_p = _rope_tables(jnp.arange(l, dtype=jnp.int32))
    h, xs, route, counts, nk_p, nv_p, npool_p = _front(
        x_prompt, rope_p, sink, wq, wrest, wa, wb, wc, wo, wmix, pscale, mk, mv, g1, b1, wr, br)
    piece_src, piece_dst, tile_expert, tile_active = _piece_tables(
        counts.reshape(-1, SUBLANES, LANES)[:, 0, :N_EXPERTS].astype(jnp.int32))
    ys = _grouped_gemm(piece_src, piece_dst, tile_expert, tile_active, xs, wg, wu, wd)
    y_p = _combine(ys, h.reshape(t, D_MODEL), route.reshape(t, LANES), g2, b2).reshape(b, l, D_MODEL)

    rope_s = _rope_tables(jnp.full((1,), PAST_LEN, jnp.int32))
    xs = x_sample.reshape(db, D_MODEL)
    feat_major = lambda c: jnp.transpose(c[0], (0, 2, 3, 1)).reshape(db, KV_WIDTH, c.shape[2])
    from_feat_major = lambda a: jnp.transpose(a.reshape(db, N_KV, HEAD_DIM, a.shape[2]), (0, 3, 1, 2))[None]
    z_s, oa_s, oc_s, nk_s, nv_s = _sample_attn(xs, rope_s, wq, wrest, sink_gk, feat_major(cache_win_k),
                                               feat_major(cache_win_v), feat_major(cache_mem_k), feat_major(cache_mem_v))
    state = jnp.transpose(state_pool[0], (1, 0, 2))
    y_s, npool_s = _sample_tail(xs, z_s, oa_s, oc_s, state, wa, wb, wc, wo, wmix, pscale, g1, b1, wr, br,
                                wg, wu, wd, g2, b2)

    kv5 = lambda a, n, w: a.reshape(1, n, w, N_KV, HEAD_DIM)
    return (y_p, y_s.reshape(db, 1, D_MODEL),
            kv5(nk_p, b, QB), kv5(nv_p, b, QB),
            npool_p[:, 2 * SUBLANES - POOL_STATE:, :][None],
            kv5(mk, b, N_MEM), kv5(mv, b, N_MEM),
            from_feat_major(nk_s), from_feat_major(nv_s),
            jnp.transpose(npool_s, (1, 0, 2))[None])
```

```python
import functools

import jax
import jax.numpy as jnp
import numpy as np
from jax import lax
from jax.experimental import pallas as pl
from jax.experimental.pallas import tpu as pltpu

D_MODEL = 1024
N_HEADS = 16
HEAD_DIM = 64
N_KV = 4
GROUP = N_HEADS // N_KV
WINDOW = 128
ROPE_THETA = 500000.0
ROPE_DIM = HEAD_DIM // 4
Q_WIDTH = N_HEADS * HEAD_DIM
KV_WIDTH = N_KV * HEAD_DIM
POOL_WINDOWS = (2, 4, 8, 16)
POOL_WIDTH = D_MODEL // 2
POOL_GROUP_DIM = POOL_WIDTH // len(POOL_WINDOWS)
POOL_STATE = max(POOL_WINDOWS) - 1
N_MEM = 256
MEM_HEADS = 4
MEM_WIDTH = MEM_HEADS * HEAD_DIM
N_EXPERT_GROUPS = 4
EXPERTS_PER_GROUP = 4
N_EXPERTS = N_EXPERT_GROUPS * EXPERTS_PER_GROUP
D_EXPERT = 512
PAST_LEN = 16384
DEPTH = 1
ALPHA = (2.0 * DEPTH) ** 0.25
LN_EPS = 1e-5

Q0 = 0
K0 = Q0 + Q_WIDTH
V0 = K0 + KV_WIDTH
U0 = V0 + KV_WIDTH
CQ0 = U0 + POOL_WIDTH
GZ0 = CQ0 + MEM_WIDTH
IN_WIDTH = GZ0 + 3 * D_MODEL

LANES = 128
SUBLANES = 8
VMEM_LIMIT = 56 * 1024 * 1024

TM = 256
QB = WINDOW
PIECE = 16
PIECES_PER_TILE = 32
TG = PIECE * PIECES_PER_TILE
MAX_CHUNK_PIECES = 2 * TM // PIECE + N_EXPERTS - 1
N_SPARE = 2 * (PIECES_PER_TILE - 1) + 1
IN_SLOTS = 3


def _chunk_rows(n_chunks):
    spare = -(-N_SPARE // n_chunks)
    return -(-(MAX_CHUNK_PIECES + spare) * PIECE // LANES) * LANES
SB = 8
IN_CHUNK = 768
Q_CHUNK = 512
ATT_SCALE = HEAD_DIM ** -0.5
SOFTMAX_ROWS = 64
assert N_MEM == 2 * QB and MEM_HEADS == N_KV and MEM_WIDTH == KV_WIDTH and TM <= GROUP * QB

BF16 = jnp.bfloat16
F32 = jnp.float32
NEG_INF = float("-inf")


def _const_spec(shape):
    nd = len(shape)
    return pl.BlockSpec(shape, lambda *_: (0,) * nd, pipeline_mode=pl.Buffered(1))


def _layer_norm(x, g, b):
    mu = jnp.mean(x, axis=-1, keepdims=True)
    xc = x - mu
    var = jnp.mean(xc * xc, axis=-1, keepdims=True)
    return xc * lax.rsqrt(var + LN_EPS) * g + b


def _dot(a, b):
    return jnp.dot(a, b, preferred_element_type=F32)


def _dot_nt(a, b):
    return lax.dot_general(a, b, (((1,), (1,)), ((), ())), preferred_element_type=F32)


def _lane_block_mask(shape, block, width=HEAD_DIM):
    lane = lax.broadcasted_iota(jnp.int32, shape, len(shape) - 1)
    return (lane >= block * width) & (lane < (block + 1) * width)


def _rope(x, c, s1, s2):
    half = ROPE_DIM // 2
    return x * c + pltpu.roll(x, half, 1) * s1 + pltpu.roll(x, LANES - half, 1) * s2


def _route(logits):
    rows = logits.shape[0]
    lane = lax.broadcasted_iota(jnp.int32, (rows, LANES), 1)
    lanef = lane.astype(F32)
    big = float(LANES)
    is_g = lane < N_EXPERT_GROUPS
    glog = jnp.where(is_g, logits, NEG_INF)
    gmax = jnp.max(glog, axis=1, keepdims=True)
    gsum = jnp.sum(jnp.where(is_g, jnp.exp(glog - gmax), 0.0), axis=1, keepdims=True)
    gp = 1.0 / gsum
    gidx = jnp.min(jnp.where(glog == gmax, lanef, big), axis=1, keepdims=True).astype(jnp.int32)
    lo = N_EXPERT_GROUPS + gidx * EXPERTS_PER_GROUP
    in_grp = (lane >= lo) & (lane < lo + EXPERTS_PER_GROUP)
    el = jnp.where(in_grp, logits, NEG_INF)
    v1 = jnp.max(el, axis=1, keepdims=True)
    i1 = jnp.min(jnp.where(el == v1, lanef, big), axis=1, keepdims=True).astype(jnp.int32)
    el2 = jnp.where(lane == i1, NEG_INF, el)
    v2 = jnp.max(el2, axis=1, keepdims=True)
    i2 = jnp.min(jnp.where(el2 == v2, lanef, big), axis=1, keepdims=True).astype(jnp.int32)
    e21 = jnp.exp(v2 - v1)
    inv = 1.0 / (1.0 + e21)
    w1 = inv * gp
    w2 = e21 * inv * gp
    e1 = i1 - N_EXPERT_GROUPS
    e2 = i2 - N_EXPERT_GROUPS
    return lane == e1, lane == e2, w1, w2


def _route_and_sort(logits, hb, cap):
    rows = hb.shape[0]
    lt = logits.T
    row = lax.broadcasted_iota(jnp.int32, (LANES, rows), 0)
    rowf = row.astype(F32)
    big = float(LANES)
    is_g = row < N_EXPERT_GROUPS
    glog = jnp.where(is_g, lt, NEG_INF)
    gmax = jnp.max(glog, axis=0, keepdims=True)
    gp = 1.0 / jnp.sum(jnp.where(is_g, jnp.exp(glog - gmax), 0.0), axis=0, keepdims=True)
    gidx = jnp.min(jnp.where(glog == gmax, rowf, big), axis=0, keepdims=True).astype(jnp.int32)
    lo = N_EXPERT_GROUPS + gidx * EXPERTS_PER_GROUP
    el = jnp.where((row >= lo) & (row < lo + EXPERTS_PER_GROUP), lt, NEG_INF)
    v1 = jnp.max(el, axis=0, keepdims=True)
    i1 = jnp.min(jnp.where(el == v1, rowf, big), axis=0, keepdims=True).astype(jnp.int32)
    el2 = jnp.where(row == i1, NEG_INF, el)
    v2 = jnp.max(el2, axis=0, keepdims=True)
    i2 = jnp.min(jnp.where(el2 == v2, rowf, big), axis=0, keepdims=True).astype(jnp.int32)
    e21 = jnp.exp(v2 - v1)
    inv = 1.0 / (1.0 + e21)
    w1 = inv * gp
    w2 = e21 * inv * gp
    hot1 = row == i1 - N_EXPERT_GROUPS
    hot2 = row == i2 - N_EXPERT_GROUPS
    onehot = jnp.where(hot1 | hot2, 1.0, 0.0)
    counts = jnp.broadcast_to(jnp.sum(onehot, axis=1, keepdims=True), (LANES, LANES))
    earlier = (lax.broadcasted_iota(jnp.int32, (rows, rows), 0)
               < lax.broadcasted_iota(jnp.int32, (rows, rows), 1)).astype(BF16)
    rank = _dot(onehot.astype(BF16), earlier)
    run = (((counts.astype(jnp.int32) + (PIECE - 1)) // PIECE) * PIECE).astype(BF16)
    below = (lax.broadcasted_iota(jnp.int32, (LANES, LANES), 1)
             < lax.broadcasted_iota(jnp.int32, (LANES, LANES), 0)).astype(BF16)
    start = _dot(below, run)
    slot = jnp.concatenate([start] * (rows // LANES), axis=1) + rank
    s1 = jnp.sum(jnp.where(hot1, slot, 0.0), axis=0, keepdims=True)
    s2 = jnp.sum(jnp.where(hot2, slot, 0.0), axis=0, keepdims=True)
    srow = lax.broadcasted_iota(jnp.int32, (cap, rows), 0).astype(F32)
    perm = jnp.where((srow == s1) | (srow == s2), 1.0, 0.0).astype(BF16)
    route_t = jnp.where(row == 0, w1, jnp.where(row == 1, w2, jnp.where(row == 2, s1, jnp.where(row == 3, s2, 0.0))))
    return _dot(perm, hb).astype(BF16), route_t.T, counts


def _sigmoid(x):
    return 0.5 * jnp.tanh(0.5 * x) + 0.5


def _merge_ln1(x, oa, ob, oc, gz, wa_ref, wb_ref, wc_ref, wo_ref, g1, b1):
    ya = _dot(oa, wa_ref[...])
    yb = _dot(ob.astype(BF16), wb_ref[...])
    yc = _dot(oc.astype(BF16), wc_ref[...])
    m = (_sigmoid(gz[:, 0:D_MODEL]) * ya
         + _sigmoid(gz[:, D_MODEL:2 * D_MODEL]) * yb
         + _sigmoid(gz[:, 2 * D_MODEL:3 * D_MODEL]) * yc)
    hpre = ALPHA * x + _dot(m.astype(BF16), wo_ref[...])
    return _layer_norm(hpre, g1, b1)


def _mem_kernel(mem_ref, wk_ref, wv_ref, mk_ref, mv_ref):
    m = mem_ref[...].astype(BF16)
    mk_ref[...] = _dot(m, wk_ref[...])
    mv_ref[...] = _dot(m, wv_ref[...])


def _mem_project(mem, wk, wv):
    b = mem.shape[0]
    out = jax.ShapeDtypeStruct((b, N_MEM, MEM_WIDTH), F32)
    return pl.pallas_call(
        _mem_kernel,
        grid=(b,),
        in_specs=[pl.BlockSpec((None, N_MEM, D_MODEL), lambda i: (i, 0, 0)),
                  _const_spec((D_MODEL, MEM_WIDTH)), _const_spec((D_MODEL, MEM_WIDTH))],
        out_specs=[pl.BlockSpec((None, N_MEM, MEM_WIDTH), lambda i: (i, 0, 0))] * 2,
        out_shape=[out, out],
        name="mem_project",
    )(mem, wk, wv)


def _project(xb, wq_ref, wrest_ref, z_ref):
    for c0 in range(0, Q_WIDTH, Q_CHUNK):
        z_ref[:, Q0 + c0:Q0 + c0 + Q_CHUNK] = _dot(xb, wq_ref[:, c0:c0 + Q_CHUNK])
    rest = IN_WIDTH - K0
    for c0 in range(0, rest, IN_CHUNK):
        c1 = min(c0 + IN_CHUNK, rest)
        z_ref[:, K0 + c0:K0 + c1] = _dot(xb, wrest_ref[:, c0:c1])


def _front_kernel(sinks_ref, x_ref, rope_ref, wq_ref, wrest_ref, wa_ref, wb_ref, wc_ref, wo_ref, wmix_ref,
                  pscale_ref, mk_ref, mv_ref, g1_ref, b1_ref, wr_ref, br_ref,
                  h_ref, xs_ref, route_ref, counts_ref, nk_ref, nv_ref, npool_ref,
                  z_ref, qb_ref, kext_ref, vext_ref, uext_ref, oa_ref, ob_ref, bias_ref, s_ref, p_ref, vblk_ref):
    i = pl.program_id(1)
    x = x_ref[...]
    xb = x.astype(BF16)
    hist = 2 * SUBLANES

    @pl.when(i == 0)
    def _():
        kext_ref[0:QB, :] = jnp.zeros((QB, KV_WIDTH), BF16)
        vext_ref[0:QB, :] = jnp.zeros((QB, KV_WIDTH), BF16)
        uext_ref[0:hist, :] = jnp.zeros((hist, POOL_WIDTH), F32)

    @pl.when(i > 0)
    def _():
        kext_ref[0:QB, :] = kext_ref[TM:TM + QB, :]
        vext_ref[0:QB, :] = vext_ref[TM:TM + QB, :]
        uext_ref[0:hist, :] = uext_ref[TM:TM + hist, :]

    _project(xb, wq_ref, wrest_ref, z_ref)

    c = rope_ref[0]
    s1 = rope_ref[1]
    s2 = rope_ref[2]
    for j in range(Q_WIDTH // LANES):
        sl = slice(Q0 + j * LANES, Q0 + (j + 1) * LANES)
        qb_ref[:, j * LANES:(j + 1) * LANES] = _rope(z_ref[:, sl], c, s1, s2).astype(BF16)
    for j in range(KV_WIDTH // LANES):
        sl = slice(K0 + j * LANES, K0 + (j + 1) * LANES)
        kr = _rope(z_ref[:, sl], c, s1, s2)
        z_ref[:, sl] = kr
        kext_ref[QB:QB + TM, j * LANES:(j + 1) * LANES] = kr.astype(BF16)
    vext_ref[QB:QB + TM, :] = z_ref[:, V0:V0 + KV_WIDTH].astype(BF16)
    uext_ref[hist:hist + TM, :] = z_ref[:, U0:U0 + POOL_WIDTH]
    nk_ref[...] = z_ref[TM - QB:TM, K0:K0 + KV_WIDTH]
    nv_ref[...] = z_ref[TM - QB:TM, V0:V0 + KV_WIDTH]

    rowq = lax.broadcasted_iota(jnp.int32, (QB, 2 * QB), 0)
    colk = lax.broadcasted_iota(jnp.int32, (QB, 2 * QB), 1)
    band = (colk >= rowq) & (colk <= rowq + WINDOW)
    bias_ref[1] = jnp.where(band, 0.0, NEG_INF)
    bias_ref[0] = jnp.where(band & ((colk >= QB) | (i > 0)), 0.0, NEG_INF)
    for sb in range(TM // QB):
        k2 = kext_ref[sb * QB:(sb + 2) * QB, :]
        v2 = vext_ref[sb * QB:(sb + 2) * QB, :]
        qs = jnp.concatenate(
            [qb_ref[sb * QB:(sb + 1) * QB, g * KV_WIDTH:(g + 1) * KV_WIDTH] for g in range(GROUP)], axis=0)
        for kv in range(N_KV):
            kmask = _lane_block_mask((2 * QB, KV_WIDTH), kv)
            s_ref[...] = _dot_nt(qs, jnp.where(kmask, k2, jnp.zeros_like(k2)))
            vblk_ref[kv * 2 * QB:(kv + 1) * 2 * QB, :] = jnp.where(kmask, v2, jnp.zeros_like(v2))
            for c0 in range(0, GROUP * QB, SOFTMAX_ROWS):
                rq = c0 % QB
                sink = sinks_ref[kv * GROUP + c0 // QB]
                s = s_ref[c0:c0 + SOFTMAX_ROWS, :] + bias_ref[min(sb, 1), rq:rq + SOFTMAX_ROWS, :]
                m = jnp.maximum(jnp.max(s, axis=1, keepdims=True), sink)
                p = jnp.exp(s - m)
                den = jnp.sum(p, axis=1, keepdims=True) + jnp.exp(sink - m)
                p_ref[c0:c0 + SOFTMAX_ROWS, kv * 2 * QB:(kv + 1) * 2 * QB] = (p * (1.0 / den)).astype(BF16)
        o = _dot(p_ref[...], vblk_ref[...])
        for g in range(GROUP):
            oa_ref[sb * QB:(sb + 1) * QB, g * KV_WIDTH:(g + 1) * KV_WIDTH] = o[g * QB:(g + 1) * QB].astype(BF16)

    npool_ref[...] = uext_ref[TM:TM + hist, :]
    pos = i * TM + lax.broadcasted_iota(jnp.int32, (TM, 1), 0)
    for g, w in enumerate(POOL_WINDOWS):
        sl = slice(g * POOL_GROUP_DIM, (g + 1) * POOL_GROUP_DIM)
        cur = uext_ref[hist:hist + TM, sl]
        ws = cur
        for j in range(1, w):
            ws = ws + uext_ref[hist - j:hist - j + TM, sl]
        cnt = jnp.minimum(pos + 1, w).astype(F32)
        pooled = ws / cnt - cur
        ob_ref[:, sl] = _dot(pooled.astype(BF16), wmix_ref[g]) * pscale_ref[:, sl]

    cq = (z_ref[:, CQ0:CQ0 + MEM_WIDTH] * ATT_SCALE).astype(BF16)
    mk = mk_ref[...].astype(BF16)
    mv = mv_ref[...].astype(BF16)
    for hh in range(MEM_HEADS):
        hmask = _lane_block_mask((N_MEM, MEM_WIDTH), hh)
        s_ref[0:TM, :] = _dot_nt(cq, jnp.where(hmask, mk, jnp.zeros_like(mk)))
        vblk_ref[hh * N_MEM:(hh + 1) * N_MEM, :] = jnp.where(hmask, mv, jnp.zeros_like(mv))
        for c0 in range(0, TM, SOFTMAX_ROWS):
            s = s_ref[c0:c0 + SOFTMAX_ROWS, :]
            p = jnp.exp(s - jnp.max(s, axis=1, keepdims=True))
            den = jnp.sum(p, axis=1, keepdims=True)
            p_ref[c0:c0 + SOFTMAX_ROWS, hh * N_MEM:(hh + 1) * N_MEM] = (p * (1.0 / den)).astype(BF16)
    oc = _dot(p_ref[0:TM, :], vblk_ref[...])

    h = _merge_ln1(x, oa_ref[...], ob_ref[...], oc, z_ref[:, GZ0:GZ0 + 3 * D_MODEL],
                   wa_ref, wb_ref, wc_ref, wo_ref, g1_ref[...], b1_ref[...])
    h_ref[...] = h
    hb = h.astype(BF16)
    logits = _dot(hb, wr_ref[...]) + br_ref[...]
    xs_ref[...], route_ref[...], counts_ref[...] = _route_and_sort(logits, hb, xs_ref.shape[0])


def _front(x, rope, sinks, wq, wrest, wa, wb, wc, wo, wmix, pscale, mk, mv, g1, b1, wr, br):
    b, l, _ = x.shape
    nt = l // TM
    cap = _chunk_rows(b * nt)
    hist = 2 * SUBLANES
    tile = lambda w: pl.BlockSpec((None, TM, w), lambda bi, ti: (bi, ti, 0))
    per_b = lambda r, w: pl.BlockSpec((None, r, w), lambda bi, ti: (bi, 0, 0))
    return pl.pallas_call(
        _front_kernel,
        grid=(b, nt),
        in_specs=[
            pl.BlockSpec(memory_space=pltpu.SMEM),
            tile(D_MODEL),
            pl.BlockSpec((3, TM, LANES), lambda bi, ti: (0, ti, 0)),
            _const_spec((D_MODEL, Q_WIDTH)), _const_spec((D_MODEL, IN_WIDTH - K0)),
            _const_spec((Q_WIDTH, D_MODEL)), _const_spec((POOL_WIDTH, D_MODEL)),
            _const_spec((MEM_WIDTH, D_MODEL)), _const_spec((D_MODEL, D_MODEL)),
            _const_spec((len(POOL_WINDOWS), POOL_GROUP_DIM, POOL_GROUP_DIM)),
            _const_spec((1, POOL_WIDTH)),
            per_b(N_MEM, MEM_WIDTH), per_b(N_MEM, MEM_WIDTH),
            _const_spec((1, D_MODEL)), _const_spec((1, D_MODEL)),
            _const_spec((D_MODEL, LANES)), _const_spec((1, LANES)),
        ],
        out_specs=[
            tile(D_MODEL),
            pl.BlockSpec((cap, D_MODEL), lambda bi, ti: (bi * nt + ti, 0)),
            tile(LANES),
            pl.BlockSpec((None, None, LANES, LANES), lambda bi, ti: (bi, ti, 0, 0)),
            per_b(QB, KV_WIDTH), per_b(QB, KV_WIDTH), per_b(hist, POOL_WIDTH),
        ],
        out_shape=[
            jax.ShapeDtypeStruct((b, l, D_MODEL), F32),
            jax.ShapeDtypeStruct((b * nt * cap, D_MODEL), BF16),
            jax.ShapeDtypeStruct((b, l, LANES), F32),
            jax.ShapeDtypeStruct((b, nt, LANES, LANES), F32),
            jax.ShapeDtypeStruct((b, QB, KV_WIDTH), F32),
            jax.ShapeDtypeStruct((b, QB, KV_WIDTH), F32),
            jax.ShapeDtypeStruct((b, hist, POOL_WIDTH), F32),
        ],
        scratch_shapes=[
            pltpu.VMEM((TM, IN_WIDTH), F32),
            pltpu.VMEM((TM, Q_WIDTH), BF16),
            pltpu.VMEM((QB + TM, KV_WIDTH), BF16),
            pltpu.VMEM((QB + TM, KV_WIDTH), BF16),
            pltpu.VMEM((hist + TM, POOL_WIDTH), F32),
            pltpu.VMEM((TM, Q_WIDTH), BF16),
            pltpu.VMEM((TM, POOL_WIDTH), F32),
            pltpu.VMEM((2, QB, 2 * QB), F32),
            pltpu.VMEM((GROUP * QB, 2 * QB), F32),
            pltpu.VMEM((GROUP * QB, N_KV * 2 * QB), BF16),
            pltpu.VMEM((N_KV * 2 * QB, KV_WIDTH), BF16),
        ],
        compiler_params=pltpu.CompilerParams(
            dimension_semantics=("arbitrary", "arbitrary"), vmem_limit_bytes=VMEM_LIMIT),
        name="front_prompt",
    )(sinks, x, rope, wq, wrest, wa, wb, wc, wo, wmix, pscale, mk, mv, g1, b1, wr, br)


def _expert_mlp(xb, wg, wu, wd):
    a = _dot(xb, wg)
    hid = (a * jax.nn.sigmoid(a)) * _dot(xb, wu)
    return _dot(hid.astype(BF16), wd)


def _gemm_kernel(src_ref, dst_ref, te_ref, act_ref, xs_ref, wg_ref, wu_ref, wd_ref, ys_ref,
                 xbuf, obuf, wgb, wub, wdb, prime, sem_in, sem_out):
    i = pl.program_id(0)
    n = pl.num_programs(0)
    slot = i % 2
    in_slot = i % IN_SLOTS

    def start_in(tile, slot):
        for j in range(PIECES_PER_TILE):
            row0 = pl.multiple_of(src_ref[tile * PIECES_PER_TILE + j], PIECE)
            pltpu.make_async_copy(xs_ref.at[pl.ds(row0, PIECE)], xbuf.at[slot, pl.ds(j * PIECE, PIECE)],
                                  sem_in.at[slot]).start()

    def start_out(tile, slot):
        for j in range(PIECES_PER_TILE):
            row0 = pl.multiple_of(dst_ref[tile * PIECES_PER_TILE + j], PIECE)
            pltpu.make_async_copy(obuf.at[slot, pl.ds(j * PIECE, PIECE)], ys_ref.at[pl.ds(row0, PIECE)],
                                  sem_out.at[slot]).start()

    def wait_in(slot):
        for j in range(PIECES_PER_TILE):
            pltpu.make_async_copy(xs_ref.at[pl.ds(0, PIECE)], xbuf.at[slot, pl.ds(j * PIECE, PIECE)],
                                  sem_in.at[slot]).wait()

    def wait_out(slot):
        for j in range(PIECES_PER_TILE):
            pltpu.make_async_copy(obuf.at[slot, pl.ds(j * PIECE, PIECE)], ys_ref.at[pl.ds(0, PIECE)],
                                  sem_out.at[slot]).wait()

    active = act_ref[i] > 0
    ahead = IN_SLOTS - 1
    prefetched = (i < ahead) | (act_ref[jnp.maximum(i - ahead, 0)] > 0)
    out_pending = (i < 2) | (act_ref[jnp.maximum(i - 2, 0)] > 0)

    @pl.when(i == 0)
    def _():
        prime[0] = jnp.zeros((PIECE, D_MODEL), BF16)
        for s in range(2):
            for j in range(PIECES_PER_TILE):
                pltpu.make_async_copy(prime.at[0], prime.at[1 + s * PIECES_PER_TILE + j], sem_out.at[s]).start()
        for t in range(ahead):
            start_in(t, t)

    prev_active = (i >= 1) & (act_ref[jnp.maximum(i - 1, 0)] > 0)

    def active_step(after_first):
        @pl.when((i == 0) | (te_ref[i] != te_ref[jnp.maximum(i - 1, 0)]))
        def _():
            wgb[...] = wg_ref[...].astype(BF16)
            wub[...] = wu_ref[...].astype(BF16)
            wdb[...] = wd_ref[...].astype(BF16)

        wait_in(in_slot)
        wait_out(slot)
        out = _expert_mlp(xbuf[in_slot], wgb[...], wub[...], wdb[...])
        start_in(i + ahead, (i + ahead) % IN_SLOTS)
        if after_first:
            start_out(i - 1, 1 - slot)
        obuf[slot] = out.astype(BF16)

    @pl.when(active & (i == 0))
    def _():
        active_step(False)

    @pl.when(active & (i > 0))
    def _():
        active_step(True)

    @pl.when(jnp.logical_not(active))
    def _():
        @pl.when(prev_active)
        def _():
            start_out(i - 1, 1 - slot)

        @pl.when(prefetched)
        def _():
            wait_in(in_slot)

        @pl.when(out_pending)
        def _():
            wait_out(slot)

    @pl.when(i == n - 1)
    def _():
        for t in range(ahead):
            @pl.when(act_ref[jnp.maximum(i - t, 0)] > 0)
            def _(t=t):
                wait_in((i - t + ahead) % IN_SLOTS)

        @pl.when(active)
        def _():
            start_out(i, slot)
            wait_out(slot)

        @pl.when(prev_active)
        def _():
            wait_out(1 - slot)


def _grouped_gemm(piece_src, piece_dst, tile_expert, tile_active, xs, wg, wu, wd):
    n_tiles = tile_expert.shape[0]
    assert n_tiles >= 2
    wspec = lambda r, c: pl.BlockSpec((None, r, c), lambda i, src, dst, te, act: (te[i], 0, 0))
    return pl.pallas_call(
        _gemm_kernel,
        grid_spec=pltpu.PrefetchScalarGridSpec(
            num_scalar_prefetch=4,
            grid=(n_tiles,),
            in_specs=[pl.BlockSpec(memory_space=pl.ANY),
                      wspec(D_MODEL, D_EXPERT), wspec(D_MODEL, D_EXPERT), wspec(D_EXPERT, D_MODEL)],
            out_specs=pl.BlockSpec(memory_space=pl.ANY),
            scratch_shapes=[pltpu.VMEM((IN_SLOTS, TG, D_MODEL), BF16), pltpu.VMEM((2, TG, D_MODEL), BF16),
                            pltpu.VMEM((D_MODEL, D_EXPERT), BF16), pltpu.VMEM((D_MODEL, D_EXPERT), BF16),
                            pltpu.VMEM((D_EXPERT, D_MODEL), BF16),
                            pltpu.VMEM((1 + 2 * PIECES_PER_TILE, PIECE, D_MODEL), BF16),
                            pltpu.SemaphoreType.DMA((IN_SLOTS,)), pltpu.SemaphoreType.DMA((2,))],
        ),
        out_shape=jax.ShapeDtypeStruct(xs.shape, xs.dtype),
        input_output_aliases={4: 0},
        compiler_params=pltpu.CompilerParams(dimension_semantics=("arbitrary",), vmem_limit_bytes=VMEM_LIMIT),
        name="moe_grouped_gemm",
    )(piece_src, piece_dst, tile_expert, tile_active, xs, wg, wu, wd)


def _combine_kernel(ys_ref, h_ref, route_ref, g2_ref, b2_ref, y_ref):
    chunks = h_ref.shape[0] // TM
    cap = ys_ref.shape[0] // chunks
    slot = lax.broadcasted_iota(jnp.int32, (TM, cap), 1).astype(F32)
    for c in range(chunks):
        rows = slice(c * TM, (c + 1) * TM)
        route = route_ref[rows, :]
        sel = jnp.concatenate([jnp.where(slot == route[:, 2:3], 1.0, 0.0).astype(BF16),
                               jnp.where(slot == route[:, 3:4], 1.0, 0.0).astype(BF16)], axis=0)
        picked = _dot(sel, ys_ref[c * cap:(c + 1) * cap, :])
        f = route[:, 0:1] * picked[0:TM] + route[:, 1:2] * picked[TM:2 * TM]
        y_ref[rows, :] = _layer_norm(ALPHA * h_ref[rows, :] + f, g2_ref[...], b2_ref[...])


def _combine(ys, h, route, g2, b2):
    t = h.shape[0]
    cap = ys.shape[0] // (t // TM)
    per_step = next(k for k in (4, 2, 1) if (t // TM) % k == 0)
    rows = per_step * TM
    return pl.pallas_call(
        _combine_kernel,
        grid=(t // rows,),
        in_specs=[
            pl.BlockSpec((per_step * cap, D_MODEL), lambda i: (i, 0)),
            pl.BlockSpec((rows, D_MODEL), lambda i: (i, 0)),
            pl.BlockSpec((rows, LANES), lambda i: (i, 0)),
            pl.BlockSpec((1, D_MODEL), lambda i: (0, 0)),
            pl.BlockSpec((1, D_MODEL), lambda i: (0, 0)),
        ],
        out_specs=pl.BlockSpec((rows, D_MODEL), lambda i: (i, 0)),
        out_shape=jax.ShapeDtypeStruct((t, D_MODEL), F32),
        compiler_params=pltpu.CompilerParams(dimension_semantics=("arbitrary",)),
        name="moe_combine",
    )(ys, h, route, g2, b2)


def _select(table, idx):
    hot = idx[:, None] == jnp.arange(table.shape[0], dtype=jnp.int32)[None, :]
    return jnp.sum(jnp.where(hot[:, :, None], table[None, :, :], 0), axis=1)


def _piece_tables(counts):
    n_chunks = counts.shape[0]
    n_tiles = -(-(n_chunks * MAX_CHUNK_PIECES + N_EXPERTS * (PIECES_PER_TILE - 1)) // PIECES_PER_TILE)
    npc = (counts + (PIECE - 1)) // PIECE
    first = (jnp.cumsum(npc, axis=1) - npc).T
    npc_t = npc.T
    cum = jnp.cumsum(npc_t, axis=1)
    per_expert = cum[:, -1]
    tiles_e = (per_expert + (PIECES_PER_TILE - 1)) // PIECES_PER_TILE
    tile_end = jnp.cumsum(tiles_e)
    tile_idx = jnp.arange(n_tiles, dtype=jnp.int32)
    expert_of = lambda i: jnp.minimum(jnp.sum(i[:, None] >= tile_end[None, :], axis=1), N_EXPERTS - 1).astype(jnp.int32)
    active = tile_idx < tile_end[-1]
    tile_expert = jnp.where(active, expert_of(tile_idx), expert_of(tile_end[-1:] - 1))
    meta = jnp.stack([tile_end - tiles_e, per_expert], axis=1)
    meta_t = _select(meta, tile_expert)
    k = (tile_idx - meta_t[:, 0])[:, None] * PIECES_PER_TILE + jnp.arange(PIECES_PER_TILE, dtype=jnp.int32)[None, :]
    valid = active[:, None] & (k < meta_t[:, 1:2])
    cum_t = _select(cum, tile_expert)
    chunk = jnp.minimum(jnp.sum(k[:, :, None] >= cum_t[:, None, :], axis=2), n_chunks - 1).astype(jnp.int32)
    at_chunk = chunk[:, :, None] == jnp.arange(n_chunks, dtype=jnp.int32)[None, None, :]
    pick = lambda tab: jnp.sum(jnp.where(at_chunk, _select(tab, tile_expert)[:, None, :], 0), axis=2)
    piece = pick(first) + k - pick(cum - npc_t)
    cap = _chunk_rows(n_chunks)
    rows = chunk * cap + piece * PIECE
    d = ((tile_idx % 2)[:, None] * (PIECES_PER_TILE - 1)
         + jnp.maximum(jnp.arange(PIECES_PER_TILE, dtype=jnp.int32)[None, :] - 1, 0))
    spare_row = lambda d: (d % n_chunks) * cap + (MAX_CHUNK_PIECES + d // n_chunks) * PIECE
    spare = spare_row(d)
    zero_piece = spare_row(N_SPARE - 1)
    extra = jnp.full(((IN_SLOTS - 1) * PIECES_PER_TILE,), zero_piece, jnp.int32)
    src = jnp.concatenate([jnp.where(valid, rows, zero_piece).astype(jnp.int32).reshape(-1), extra])
    dst = jnp.concatenate([jnp.where(valid, rows, spare).astype(jnp.int32).reshape(-1), extra])
    return src, dst, tile_expert.astype(jnp.int32), active.astype(jnp.int32)


def _sample_attn_kernel(x_ref, rope_ref, wq_ref, wrest_ref, sink_ref, ck_ref, cv_ref, cmk_ref, cmv_ref,
                        z_ref, oa_ref, oc_ref, nk_ref, nv_ref, knew_t, vnew_t):
    j = pl.program_id(0)
    db = x_ref.shape[0]

    @pl.when(j == 0)
    def _():
        _project(x_ref[...].astype(BF16), wq_ref, wrest_ref, z_ref)
        c = rope_ref[0]
        s1 = rope_ref[1]
        s2 = rope_ref[2]
        for jj in range((Q_WIDTH + KV_WIDTH) // LANES):
            sl = slice(jj * LANES, (jj + 1) * LANES)
            z_ref[:, sl] = _rope(z_ref[:, sl], c, s1, s2)
        knew_t[...] = z_ref[:, K0:K0 + KV_WIDTH].T
        vnew_t[...] = z_ref[:, V0:V0 + KV_WIDTH].T

    r0 = pl.multiple_of(j * SB, SB)
    zq = z_ref[pl.ds(r0, SB), Q0:Q0 + Q_WIDTH]
    zk = z_ref[pl.ds(r0, SB), K0:K0 + KV_WIDTH]
    zv = z_ref[pl.ds(r0, SB), V0:V0 + KV_WIDTH]
    zc = z_ref[pl.ds(r0, SB), CQ0:CQ0 + MEM_WIDTH] * ATT_SCALE
    sink = sink_ref[:, 0:1]
    row_kv = lax.broadcasted_iota(jnp.int32, (N_HEADS, KV_WIDTH), 0) & (N_KV - 1)
    lane_kv = lax.broadcasted_iota(jnp.int32, (N_HEADS, KV_WIDTH), 1) // HEAD_DIM
    own = row_kv == lane_kv
    row_c = lax.broadcasted_iota(jnp.int32, (N_HEADS, MEM_WIDTH), 0)
    lane_c = lax.broadcasted_iota(jnp.int32, (N_HEADS, MEM_WIDTH), 1) // HEAD_DIM
    own_c = row_c == lane_c
    last_pos = lax.broadcasted_iota(jnp.int32, (KV_WIDTH, WINDOW), 1) == WINDOW - 1
    seq_lane = lax.broadcasted_iota(jnp.int32, (KV_WIDTH, db), 1)

    seqs = range(SB)
    qblk, cblk, s, sc = [], [], [], []
    for b in seqs:
        q4 = jnp.concatenate(
            [jnp.broadcast_to(zq[b:b + 1, g * KV_WIDTH:(g + 1) * KV_WIDTH], (N_KV, KV_WIDTH)) for g in range(GROUP)],
            axis=0)
        qblk.append(jnp.where(own, q4, 0.0).astype(BF16))
        cblk.append(jnp.where(own_c, jnp.broadcast_to(zc[b:b + 1, :], (N_HEADS, MEM_WIDTH)), 0.0).astype(BF16))
        s.append(_dot(qblk[b], ck_ref[b].astype(BF16)))
        sc.append(_dot(cblk[b], cmk_ref[b].astype(BF16)))

    pn, p_new, pc = [], [], []
    for b in seqs:
        s_new = jnp.sum(qblk[b].astype(F32) * zk[b:b + 1, :].astype(BF16).astype(F32), axis=1, keepdims=True)
        m = jnp.maximum(jnp.maximum(jnp.max(s[b], axis=1, keepdims=True), s_new), sink)
        p = jnp.exp(s[b] - m)
        e_new = jnp.exp(s_new - m)
        inv = 1.0 / (jnp.sum(p, axis=1, keepdims=True) + e_new + jnp.exp(sink - m))
        pn.append((p * inv).astype(BF16))
        p_new.append((e_new * inv).astype(BF16).astype(F32))
        e = jnp.exp(sc[b] - jnp.max(sc[b], axis=1, keepdims=True))
        pc.append((e * (1.0 / jnp.sum(e, axis=1, keepdims=True))).astype(BF16))

    oa_rows = [[] for _ in range(GROUP)]
    oc_rows = []
    for b in seqs:
        vc = cv_ref[b]
        o = _dot_nt(pn[b], vc.astype(BF16)) + p_new[b] * zv[b:b + 1, :].astype(BF16).astype(F32)
        o = jnp.where(own, o, 0.0)
        for g in range(GROUP):
            oa_rows[g].append(jnp.sum(o[g * N_KV:(g + 1) * N_KV], axis=0, keepdims=True))
        ocb = jnp.where(own_c, _dot_nt(pc[b], cmv_ref[b].astype(BF16)), 0.0)
        oc_rows.append(jnp.sum(ocb, axis=0, keepdims=True))
        here = seq_lane == j * SB + b
        knew_col = jnp.sum(jnp.where(here, knew_t[...], 0.0), axis=1, keepdims=True)
        vnew_col = jnp.sum(jnp.where(here, vnew_t[...], 0.0), axis=1, keepdims=True)
        nk_ref[b] = jnp.where(last_pos, knew_col, pltpu.roll(ck_ref[b], WINDOW - 1, 1))
        nv_ref[b] = jnp.where(last_pos, vnew_col, pltpu.roll(vc, WINDOW - 1, 1))
    for g in range(GROUP):
        oa_ref[pl.ds(r0, SB), g * KV_WIDTH:(g + 1) * KV_WIDTH] = jnp.concatenate(oa_rows[g], axis=0)
    oc_ref[pl.ds(r0, SB), :] = jnp.concatenate(oc_rows, axis=0)


def _sample_attn(x, rope, wq, wrest, sink_gk, ck, cv, cmk, cmv):
    db = x.shape[0]
    blk = lambda r: pl.BlockSpec((SB, KV_WIDTH, r), lambda j: (j, 0, 0))
    full = lambda w: pl.BlockSpec((db, w), lambda j: (0, 0))
    return pl.pallas_call(
        _sample_attn_kernel,
        grid=(db // SB,),
        in_specs=[
            full(D_MODEL), _const_spec((3, 1, LANES)),
            _const_spec((D_MODEL, Q_WIDTH)), _const_spec((D_MODEL, IN_WIDTH - K0)),
            _const_spec((N_HEADS, LANES)),
            blk(WINDOW), blk(WINDOW), blk(N_MEM), blk(N_MEM),
        ],
        out_specs=[full(IN_WIDTH), full(Q_WIDTH), full(MEM_WIDTH), blk(WINDOW), blk(WINDOW)],
        out_shape=[
            jax.ShapeDtypeStruct((db, IN_WIDTH), F32),
            jax.ShapeDtypeStruct((db, Q_WIDTH), F32),
            jax.ShapeDtypeStruct((db, MEM_WIDTH), F32),
            jax.ShapeDtypeStruct((db, KV_WIDTH, WINDOW), F32),
            jax.ShapeDtypeStruct((db, KV_WIDTH, WINDOW), F32),
        ],
        scratch_shapes=[pltpu.VMEM((KV_WIDTH, db), F32), pltpu.VMEM((KV_WIDTH, db), F32)],
        compiler_params=pltpu.CompilerParams(dimension_semantics=("arbitrary",), vmem_limit_bytes=VMEM_LIMIT),
        name="sample_attn",
    )(x, rope, wq, wrest, sink_gk, ck, cv, cmk, cmv)


def _sample_tail_kernel(x_ref, z_ref, oa_ref, oc_ref, st_ref, wa_ref, wb_ref, wc_ref, wo_ref, wmix_ref,
                        pscale_ref, g1_ref, b1_ref, wr_ref, br_ref, wg_ref, wu_ref, wd_ref, g2_ref, b2_ref,
                        y_ref, npool_ref, h_sc, comb_sc, acc_sc):
    e = pl.program_id(0)

    @pl.when(e == 0)
    def _():
        u = z_ref[:, U0:U0 + POOL_WIDTH]
        npool_ref[0:POOL_STATE - 1] = st_ref[1:POOL_STATE]
        npool_ref[POOL_STATE - 1] = u
        obs = []
        for g, w in enumerate(POOL_WINDOWS):
            sl = slice(g * POOL_GROUP_DIM, (g + 1) * POOL_GROUP_DIM)
            cur = u[:, sl]
            ws = cur
            for jj in range(1, w):
                ws = ws + st_ref[POOL_STATE - jj, :, sl]
            cnt = float(min(PAST_LEN + 1, w))
            pooled = ws / cnt - cur
            obs.append(_dot(pooled.astype(BF16), wmix_ref[g]) * pscale_ref[:, sl])
        ob = jnp.concatenate(obs, axis=1)
        h = _merge_ln1(x_ref[...], oa_ref[...].astype(BF16), ob, oc_ref[...], z_ref[:, GZ0:GZ0 + 3 * D_MODEL],
                       wa_ref, wb_ref, wc_ref, wo_ref, g1_ref[...], b1_ref[...])
        h_sc[...] = h
        logits = _dot(h.astype(BF16), wr_ref[...]) + br_ref[...]
        hot1, hot2, w1, w2 = _route(logits)
        comb_sc[...] = jnp.where(hot1, w1, 0.0) + jnp.where(hot2, w2, 0.0)
        acc_sc[...] = jnp.zeros_like(acc_sc)

    out = _expert_mlp(h_sc[...].astype(BF16), wg_ref[...].astype(BF16), wu_ref[...].astype(BF16),
                      wd_ref[...].astype(BF16))
    lane = lax.broadcasted_iota(jnp.int32, comb_sc.shape, 1)
    ce = jnp.sum(jnp.where(lane == e, comb_sc[...], 0.0), axis=1, keepdims=True)
    acc_sc[...] += ce * out

    @pl.when(e == pl.num_programs(0) - 1)
    def _():
        y_ref[...] = _layer_norm(ALPHA * h_sc[...] + acc_sc[...], g2_ref[...], b2_ref[...])


def _sample_tail(x, z, oa, oc, state, wa, wb, wc, wo, wmix, pscale, g1, b1, wr, br, wg, wu, wd, g2, b2):
    db = x.shape[0]
    full = lambda w: pl.BlockSpec((db, w), lambda e: (0, 0))
    vec = lambda w: pl.BlockSpec((1, w), lambda e: (0, 0))
    hist = pl.BlockSpec((POOL_STATE, db, POOL_WIDTH), lambda e: (0, 0, 0))
    return pl.pallas_call(
        _sample_tail_kernel,
        grid=(N_EXPERTS,),
        in_specs=[
            full(D_MODEL), full(IN_WIDTH), full(Q_WIDTH), full(MEM_WIDTH), hist,
            _const_spec((Q_WIDTH, D_MODEL)), _const_spec((POOL_WIDTH, D_MODEL)),
            _const_spec((MEM_WIDTH, D_MODEL)), _const_spec((D_MODEL, D_MODEL)),
            _const_spec((len(POOL_WINDOWS), POOL_GROUP_DIM, POOL_GROUP_DIM)),
            vec(POOL_WIDTH), vec(D_MODEL), vec(D_MODEL),
            _const_spec((D_MODEL, LANES)), vec(LANES),
            pl.BlockSpec((None, D_MODEL, D_EXPERT), lambda e: (e, 0, 0)),
            pl.BlockSpec((None, D_MODEL, D_EXPERT), lambda e: (e, 0, 0)),
            pl.BlockSpec((None, D_EXPERT, D_MODEL), lambda e: (e, 0, 0)),
            vec(D_MODEL), vec(D_MODEL),
        ],
        out_specs=[full(D_MODEL), hist],
        out_shape=[jax.ShapeDtypeStruct((db, D_MODEL), F32),
                   jax.ShapeDtypeStruct((POOL_STATE, db, POOL_WIDTH), F32)],
        scratch_shapes=[pltpu.VMEM((db, D_MODEL), F32), pltpu.VMEM((db, LANES), F32),
                        pltpu.VMEM((db, D_MODEL), F32)],
        compiler_params=pltpu.CompilerParams(dimension_semantics=("arbitrary",), vmem_limit_bytes=VMEM_LIMIT),
        name="sample_tail",
    )(x, z, oa, oc, state, wa, wb, wc, wo, wmix, pscale, g1, b1, wr, br, wg, wu, wd, g2, b2)


def _rope_tables(pos):
    half = ROPE_DIM // 2
    inv = jnp.power(ROPE_THETA, -jnp.arange(half, dtype=F32) * (2.0 / ROPE_DIM))
    ang = pos.astype(F32)[:, None] * inv[None, :]
    lane = np.arange(LANES)
    off = lane % HEAD_DIM
    cos = jnp.cos(ang)[:, lane % half]
    sin = jnp.sin(ang)[:, lane % half]
    c = jnp.where(off[None, :] < ROPE_DIM, cos, 1.0)
    s1 = jnp.where((off[None, :] >= half) & (off[None, :] < ROPE_DIM), sin, 0.0)
    s2 = jnp.where(off[None, :] < half, -sin, 0.0)
    return jnp.stack([c, s1, s2]).astype(F32)


def _q_heads_group_major(w, axis):
    if axis == 1:
        n = w.shape[0]
        return w.reshape(n, N_KV, GROUP, HEAD_DIM).transpose(0, 2, 1, 3).reshape(n, Q_WIDTH)
    n = w.shape[1]
    return w.reshape(N_KV, GROUP, HEAD_DIM, n).transpose(1, 0, 2, 3).reshape(Q_WIDTH, n)


def kernel(x_prompt, x_sample, cache_win_k, cache_win_v, state_pool, cache_mem_k, cache_mem_v, mem_prompt, w_in, sinks, w_pool_mix, pool_scale, w_mem_k, w_mem_v, w_branch_a, w_branch_b, w_branch_c, w_out, ln1_g, ln1_b, w_group, b_group, w_router, b_router, w_gate, w_up, w_down, ln2_g, ln2_b):
    assert w_in.shape[0] == DEPTH == 1
    b, l, _ = x_prompt.shape
    db, ds, _ = x_sample.shape
    assert ds == 1 and l % TM == 0 and db % SB == 0
    assert cache_win_k.shape[2] == WINDOW
    t = b * l

    win = w_in[0]
    wq = (_q_heads_group_major(win[:, Q0:Q0 + Q_WIDTH], 1) * ATT_SCALE).astype(BF16)
    wrest = win[:, K0:].astype(BF16)
    wa = _q_heads_group_major(w_branch_a[0], 0).astype(BF16)
    wb = w_branch_b[0].astype(BF16)
    wc = w_branch_c[0].astype(BF16)
    wo = w_out[0].astype(BF16)
    wmix = w_pool_mix[0].astype(BF16)
    pscale = pool_scale[0].reshape(1, POOL_WIDTH)
    g1 = ln1_g[0].reshape(1, D_MODEL)
    b1 = ln1_b[0].reshape(1, D_MODEL)
    g2 = ln2_g[0].reshape(1, D_MODEL)
    b2 = ln2_b[0].reshape(1, D_MODEL)
    wr = jnp.concatenate([w_group[0], w_router[0].reshape(D_MODEL, N_EXPERTS)], axis=1)
    wr = jnp.pad(wr, ((0, 0), (0, LANES - wr.shape[1]))).astype(BF16)
    br = jnp.pad(jnp.concatenate([b_group[0], b_router[0].reshape(N_EXPERTS)]), (0, LANES - N_EXPERT_GROUPS - N_EXPERTS))
    br = br.reshape(1, LANES).astype(F32)
    wg = w_gate[0]
    wu = w_up[0]
    wd = w_down[0]
    sink = sinks[0].astype(F32)
    sink_gk = jnp.broadcast_to(sink.reshape(N_KV, GROUP).T.reshape(N_HEADS, 1), (N_HEADS, LANES))

    mk, mv = _mem_project(mem_prompt, w_mem_k[0].astype(BF16), w_mem_v[0].astype(BF16))
    rope_p = _rope_tables(jnp.arange(l, dtype=jnp.int32))
    h, xs, route, counts, nk_p, nv_p, npool_p = _front(
        x_prompt, rope_p, sink, wq, wrest, wa, wb, wc, wo, wmix, pscale, mk, mv, g1, b1, wr, br)
    piece_src, piece_dst, tile_expert, tile_active = _piece_tables(
        counts.reshape(-1, LANES, LANES)[:, :N_EXPERTS, 0].astype(jnp.int32))
    ys = _grouped_gemm(piece_src, piece_dst, tile_expert, tile_active, xs, wg, wu, wd)
    y_p = _combine(ys, h.reshape(t, D_MODEL), route.reshape(t, LANES), g2, b2).reshape(b, l, D_MODEL)

    rope_s = _rope_tables(jnp.full((1,), PAST_LEN, jnp.int32))
    xs = x_sample.reshape(db, D_MODEL)
    feat_major = lambda c: jnp.transpose(c[0], (0, 2, 3, 1)).reshape(db, KV_WIDTH, c.shape[2])
    from_feat_major = lambda a: jnp.transpose(a.reshape(db, N_KV, HEAD_DIM, a.shape[2]), (0, 3, 1, 2))[None]
    z_s, oa_s, oc_s, nk_s, nv_s = _sample_attn(xs, rope_s, wq, wrest, sink_gk, feat_major(cache_win_k),
                                               feat_major(cache_win_v), feat_major(cache_mem_k), feat_major(cache_mem_v))
    state = jnp.transpose(state_pool[0], (1, 0, 2))
    y_s, npool_s = _sample_tail(xs, z_s, oa_s, oc_s, state, wa, wb, wc, wo, wmix, pscale, g1, b1, wr, br,
                                wg, wu, wd, g2, b2)

    kv5 = lambda a, n, w: a.reshape(1, n, w, N_KV, HEAD_DIM)
    return (y_p, y_s.reshape(db, 1, D_MODEL),
            kv5(nk_p, b, QB), kv5(nv_p, b, QB),
            npool_p[:, 2 * SUBLANES - POOL_STATE:, :][None],
            kv5(mk, b, N_MEM), kv5(mv, b, N_MEM),
            from_feat_major(nk_s), from_feat_major(nv_s),
            jnp.transpose(npool_s, (1, 0, 2))[None])
```

```python
import functools

import jax
import jax.numpy as jnp
import numpy as np
from jax import lax
from jax.experimental import pallas as pl
from jax.experimental.pallas import tpu as pltpu

D_MODEL = 1024
N_HEADS = 16
HEAD_DIM = 64
N_KV = 4
GROUP = N_HEADS // N_KV
WINDOW = 128
ROPE_THETA = 500000.0
ROPE_DIM = HEAD_DIM // 4
Q_WIDTH = N_HEADS * HEAD_DIM
KV_WIDTH = N_KV * HEAD_DIM
POOL_WINDOWS = (2, 4, 8, 16)
POOL_WIDTH = D_MODEL // 2
POOL_GROUP_DIM = POOL_WIDTH // len(POOL_WINDOWS)
POOL_STATE = max(POOL_WINDOWS) - 1
N_MEM = 256
MEM_HEADS = 4
MEM_WIDTH = MEM_HEADS * HEAD_DIM
N_EXPERT_GROUPS = 4
EXPERTS_PER_GROUP = 4
N_EXPERTS = N_EXPERT_GROUPS * EXPERTS_PER_GROUP
D_EXPERT = 512
PAST_LEN = 16384
DEPTH = 1
ALPHA = (2.0 * DEPTH) ** 0.25
LN_EPS = 1e-5

Q0 = 0
K0 = Q0 + Q_WIDTH
V0 = K0 + KV_WIDTH
U0 = V0 + KV_WIDTH
CQ0 = U0 + POOL_WIDTH
GZ0 = CQ0 + MEM_WIDTH
IN_WIDTH = GZ0 + 3 * D_MODEL

LANES = 128
SUBLANES = 8
VMEM_LIMIT = 56 * 1024 * 1024

TM = 256
QB = WINDOW
PIECE = 16
PIECES_PER_TILE = 32
TG = PIECE * PIECES_PER_TILE
MAX_CHUNK_PIECES = 2 * TM // PIECE + N_EXPERTS - 1
N_SPARE = 2 * (PIECES_PER_TILE - 1) + 1
IN_SLOTS = 3


def _chunk_rows(n_chunks):
    spare = -(-N_SPARE // n_chunks)
    return -(-(MAX_CHUNK_PIECES + spare) * PIECE // LANES) * LANES
SB = 8
IN_CHUNK = 768
Q_CHUNK = 512
ATT_SCALE = HEAD_DIM ** -0.5
SOFTMAX_ROWS = 64
assert N_MEM == 2 * QB and MEM_HEADS == N_KV and MEM_WIDTH == KV_WIDTH and TM <= GROUP * QB

BF16 = jnp.bfloat16
F32 = jnp.float32
NEG_INF = float("-inf")


def _const_spec(shape):
    nd = len(shape)
    return pl.BlockSpec(shape, lambda *_: (0,) * nd, pipeline_mode=pl.Buffered(1))


def _layer_norm(x, g, b):
    mu = jnp.mean(x, axis=-1, keepdims=True)
    xc = x - mu
    var = jnp.mean(xc * xc, axis=-1, keepdims=True)
    return xc * lax.rsqrt(var + LN_EPS) * g + b


def _dot(a, b):
    return jnp.dot(a, b, preferred_element_type=F32)


def _dot_nt(a, b):
    return lax.dot_general(a, b, (((1,), (1,)), ((), ())), preferred_element_type=F32)


def _lane_block_mask(shape, block, width=HEAD_DIM):
    lane = lax.broadcasted_iota(jnp.int32, shape, len(shape) - 1)
    return (lane >= block * width) & (lane < (block + 1) * width)


def _rope(x, c, s1, s2):
    half = ROPE_DIM // 2
    return x * c + pltpu.roll(x, half, 1) * s1 + pltpu.roll(x, LANES - half, 1) * s2


def _route(logits):
    rows = logits.shape[0]
    lane = lax.broadcasted_iota(jnp.int32, (rows, LANES), 1)
    lanef = lane.astype(F32)
    big = float(LANES)
    is_g = lane < N_EXPERT_GROUPS
    glog = jnp.where(is_g, logits, NEG_INF)
    gmax = jnp.max(glog, axis=1, keepdims=True)
    gsum = jnp.sum(jnp.where(is_g, jnp.exp(glog - gmax), 0.0), axis=1, keepdims=True)
    gp = 1.0 / gsum
    gidx = jnp.min(jnp.where(glog == gmax, lanef, big), axis=1, keepdims=True).astype(jnp.int32)
    lo = N_EXPERT_GROUPS + gidx * EXPERTS_PER_GROUP
    in_grp = (lane >= lo) & (lane < lo + EXPERTS_PER_GROUP)
    el = jnp.where(in_grp, logits, NEG_INF)
    v1 = jnp.max(el, axis=1, keepdims=True)
    i1 = jnp.min(jnp.where(el == v1, lanef, big), axis=1, keepdims=True).astype(jnp.int32)
    el2 = jnp.where(lane == i1, NEG_INF, el)
    v2 = jnp.max(el2, axis=1, keepdims=True)
    i2 = jnp.min(jnp.where(el2 == v2, lanef, big), axis=1, keepdims=True).astype(jnp.int32)
    e21 = jnp.exp(v2 - v1)
    inv = 1.0 / (1.0 + e21)
    w1 = inv * gp
    w2 = e21 * inv * gp
    e1 = i1 - N_EXPERT_GROUPS
    e2 = i2 - N_EXPERT_GROUPS
    return lane == e1, lane == e2, w1, w2


def _route_and_sort(logits, hb, cap):
    rows = hb.shape[0]
    lt = logits.T
    row = lax.broadcasted_iota(jnp.int32, (LANES, rows), 0)
    rowf = row.astype(F32)
    big = float(LANES)
    is_g = row < N_EXPERT_GROUPS
    glog = jnp.where(is_g, lt, NEG_INF)
    gmax = jnp.max(glog, axis=0, keepdims=True)
    gp = 1.0 / jnp.sum(jnp.where(is_g, jnp.exp(glog - gmax), 0.0), axis=0, keepdims=True)
    gidx = jnp.min(jnp.where(glog == gmax, rowf, big), axis=0, keepdims=True).astype(jnp.int32)
    lo = N_EXPERT_GROUPS + gidx * EXPERTS_PER_GROUP
    el = jnp.where((row >= lo) & (row < lo + EXPERTS_PER_GROUP), lt, NEG_INF)
    v1 = jnp.max(el, axis=0, keepdims=True)
    i1 = jnp.min(jnp.where(el == v1, rowf, big), axis=0, keepdims=True).astype(jnp.int32)
    el2 = jnp.where(row == i1, NEG_INF, el)
    v2 = jnp.max(el2, axis=0, keepdims=True)
    i2 = jnp.min(jnp.where(el2 == v2, rowf, big), axis=0, keepdims=True).astype(jnp.int32)
    e21 = jnp.exp(v2 - v1)
    inv = 1.0 / (1.0 + e21)
    w1 = inv * gp
    w2 = e21 * inv * gp
    hot1 = row == i1 - N_EXPERT_GROUPS
    hot2 = row == i2 - N_EXPERT_GROUPS
    onehot = jnp.where(hot1 | hot2, 1.0, 0.0)
    counts = jnp.broadcast_to(jnp.sum(onehot, axis=1, keepdims=True), (LANES, LANES))
    earlier = (lax.broadcasted_iota(jnp.int32, (rows, rows), 0)
               < lax.broadcasted_iota(jnp.int32, (rows, rows), 1)).astype(BF16)
    rank = _dot(onehot.astype(BF16), earlier)
    run = (((counts.astype(jnp.int32) + (PIECE - 1)) // PIECE) * PIECE).astype(BF16)
    below = (lax.broadcasted_iota(jnp.int32, (LANES, LANES), 1)
             < lax.broadcasted_iota(jnp.int32, (LANES, LANES), 0)).astype(BF16)
    start = _dot(below, run)
    slot = jnp.concatenate([start] * (rows // LANES), axis=1) + rank
    s1 = jnp.sum(jnp.where(hot1, slot, 0.0), axis=0, keepdims=True)
    s2 = jnp.sum(jnp.where(hot2, slot, 0.0), axis=0, keepdims=True)
    srow = lax.broadcasted_iota(jnp.int32, (cap, rows), 0).astype(F32)
    perm = jnp.where((srow == s1) | (srow == s2), 1.0, 0.0).astype(BF16)
    route_t = jnp.where(row == 0, w1, jnp.where(row == 1, w2, jnp.where(row == 2, s1, jnp.where(row == 3, s2, 0.0))))
    return _dot(perm, hb).astype(BF16), route_t.T, counts


def _sigmoid(x):
    return 0.5 * jnp.tanh(0.5 * x) + 0.5


def _merge_ln1(x, oa, ob, oc, gz, wa_ref, wb_ref, wc_ref, wo_ref, g1, b1):
    ya = _dot(oa, wa_ref[...])
    yb = _dot(ob.astype(BF16), wb_ref[...])
    yc = _dot(oc.astype(BF16), wc_ref[...])
    m = (_sigmoid(gz[:, 0:D_MODEL]) * ya
         + _sigmoid(gz[:, D_MODEL:2 * D_MODEL]) * yb
         + _sigmoid(gz[:, 2 * D_MODEL:3 * D_MODEL]) * yc)
    hpre = ALPHA * x + _dot(m.astype(BF16), wo_ref[...])
    return _layer_norm(hpre, g1, b1)


def _mem_kernel(mem_ref, wk_ref, wv_ref, mk_ref, mv_ref):
    m = mem_ref[...].astype(BF16)
    mk_ref[...] = _dot(m, wk_ref[...])
    mv_ref[...] = _dot(m, wv_ref[...])


def _mem_project(mem, wk, wv):
    b = mem.shape[0]
    out = jax.ShapeDtypeStruct((b, N_MEM, MEM_WIDTH), F32)
    return pl.pallas_call(
        _mem_kernel,
        grid=(b,),
        in_specs=[pl.BlockSpec((None, N_MEM, D_MODEL), lambda i: (i, 0, 0)),
                  _const_spec((D_MODEL, MEM_WIDTH)), _const_spec((D_MODEL, MEM_WIDTH))],
        out_specs=[pl.BlockSpec((None, N_MEM, MEM_WIDTH), lambda i: (i, 0, 0))] * 2,
        out_shape=[out, out],
        name="mem_project",
    )(mem, wk, wv)


def _project_q(xb, wq_ref, zq_ref):
    for c0 in range(0, Q_WIDTH, Q_CHUNK):
        zq_ref[:, c0:c0 + Q_CHUNK] = _dot(xb, wq_ref[:, c0:c0 + Q_CHUNK])


def _project_rest(xb, wrest_ref, z_ref):
    rest = IN_WIDTH - K0
    for c0 in range(0, rest, IN_CHUNK):
        c1 = min(c0 + IN_CHUNK, rest)
        z_ref[:, K0 + c0:K0 + c1] = _dot(xb, wrest_ref[:, c0:c1])


def _front_kernel(sinks_ref, x_ref, xn_ref, rope_ref, wq_ref, wrest_ref, wa_ref, wb_ref, wc_ref, wo_ref, wmix_ref,
                  pscale_ref, mk_ref, mv_ref, g1_ref, b1_ref, wr_ref, br_ref,
                  h_ref, xs_ref, route_ref, counts_ref, nk_ref, nv_ref, npool_ref,
                  z_ref, qb_ref, kext_ref, vext_ref, uext_ref, oa_ref, ob_ref, bias_ref, s_ref, p_ref, vblk_ref, zq_ref):
    i = pl.program_id(1)
    x = x_ref[...]
    xb = x.astype(BF16)
    hist = 2 * SUBLANES

    @pl.when((pl.program_id(0) == 0) & (i == 0))
    def _():
        _project_q(xb, wq_ref, zq_ref)

    @pl.when(i == 0)
    def _():
        kext_ref[0:QB, :] = jnp.zeros((QB, KV_WIDTH), BF16)
        vext_ref[0:QB, :] = jnp.zeros((QB, KV_WIDTH), BF16)
        uext_ref[0:hist, :] = jnp.zeros((hist, POOL_WIDTH), F32)

    @pl.when(i > 0)
    def _():
        kext_ref[0:QB, :] = kext_ref[TM:TM + QB, :]
        vext_ref[0:QB, :] = vext_ref[TM:TM + QB, :]
        uext_ref[0:hist, :] = uext_ref[TM:TM + hist, :]

    _project_rest(xb, wrest_ref, z_ref)

    c = rope_ref[0]
    s1 = rope_ref[1]
    s2 = rope_ref[2]
    for j in range(Q_WIDTH // LANES):
        sl = slice(j * LANES, (j + 1) * LANES)
        qb_ref[:, sl] = _rope(zq_ref[:, sl], c, s1, s2).astype(BF16)
    for j in range(KV_WIDTH // LANES):
        sl = slice(K0 + j * LANES, K0 + (j + 1) * LANES)
        kr = _rope(z_ref[:, sl], c, s1, s2)
        z_ref[:, sl] = kr
        kext_ref[QB:QB + TM, j * LANES:(j + 1) * LANES] = kr.astype(BF16)
    vext_ref[QB:QB + TM, :] = z_ref[:, V0:V0 + KV_WIDTH].astype(BF16)
    uext_ref[hist:hist + TM, :] = z_ref[:, U0:U0 + POOL_WIDTH]
    nk_ref[...] = z_ref[TM - QB:TM, K0:K0 + KV_WIDTH]
    nv_ref[...] = z_ref[TM - QB:TM, V0:V0 + KV_WIDTH]

    rowq = lax.broadcasted_iota(jnp.int32, (QB, 2 * QB), 0)
    colk = lax.broadcasted_iota(jnp.int32, (QB, 2 * QB), 1)
    band = (colk >= rowq) & (colk <= rowq + WINDOW)
    bias_ref[1] = jnp.where(band, 0.0, NEG_INF)
    bias_ref[0] = jnp.where(band & ((colk >= QB) | (i > 0)), 0.0, NEG_INF)
    for sb in range(TM // QB):
        k2 = kext_ref[sb * QB:(sb + 2) * QB, :]
        v2 = vext_ref[sb * QB:(sb + 2) * QB, :]
        qs = jnp.concatenate(
            [qb_ref[sb * QB:(sb + 1) * QB, g * KV_WIDTH:(g + 1) * KV_WIDTH] for g in range(GROUP)], axis=0)
        for kv in range(N_KV):
            kmask = _lane_block_mask((2 * QB, KV_WIDTH), kv)
            s_ref[...] = _dot_nt(qs, jnp.where(kmask, k2, jnp.zeros_like(k2)))
            vblk_ref[kv * 2 * QB:(kv + 1) * 2 * QB, :] = jnp.where(kmask, v2, jnp.zeros_like(v2))
            for c0 in range(0, GROUP * QB, SOFTMAX_ROWS):
                rq = c0 % QB
                sink = sinks_ref[kv * GROUP + c0 // QB]
                s = s_ref[c0:c0 + SOFTMAX_ROWS, :] + bias_ref[min(sb, 1), rq:rq + SOFTMAX_ROWS, :]
                m = jnp.maximum(jnp.max(s, axis=1, keepdims=True), sink)
                p = jnp.exp(s - m)
                den = jnp.sum(p, axis=1, keepdims=True) + jnp.exp(sink - m)
                p_ref[c0:c0 + SOFTMAX_ROWS, kv * 2 * QB:(kv + 1) * 2 * QB] = (p * (1.0 / den)).astype(BF16)
        o = _dot(p_ref[...], vblk_ref[...])
        for g in range(GROUP):
            oa_ref[sb * QB:(sb + 1) * QB, g * KV_WIDTH:(g + 1) * KV_WIDTH] = o[g * QB:(g + 1) * QB].astype(BF16)

    npool_ref[...] = uext_ref[TM:TM + hist, :]
    pos = i * TM + lax.broadcasted_iota(jnp.int32, (TM, 1), 0)
    for g, w in enumerate(POOL_WINDOWS):
        sl = slice(g * POOL_GROUP_DIM, (g + 1) * POOL_GROUP_DIM)
        cur = uext_ref[hist:hist + TM, sl]
        ws = cur
        for j in range(1, w):
            ws = ws + uext_ref[hist - j:hist - j + TM, sl]
        cnt = jnp.minimum(pos + 1, w).astype(F32)
        pooled = ws / cnt - cur
        ob_ref[:, sl] = _dot(pooled.astype(BF16), wmix_ref[g]) * pscale_ref[:, sl]

    cq = (z_ref[:, CQ0:CQ0 + MEM_WIDTH] * ATT_SCALE).astype(BF16)
    mk = mk_ref[...].astype(BF16)
    mv = mv_ref[...].astype(BF16)
    for hh in range(MEM_HEADS):
        hmask = _lane_block_mask((N_MEM, MEM_WIDTH), hh)
        s_ref[0:TM, :] = _dot_nt(cq, jnp.where(hmask, mk, jnp.zeros_like(mk)))
        vblk_ref[hh * N_MEM:(hh + 1) * N_MEM, :] = jnp.where(hmask, mv, jnp.zeros_like(mv))
        for c0 in range(0, TM, SOFTMAX_ROWS):
            s = s_ref[c0:c0 + SOFTMAX_ROWS, :]
            p = jnp.exp(s - jnp.max(s, axis=1, keepdims=True))
            den = jnp.sum(p, axis=1, keepdims=True)
            p_ref[c0:c0 + SOFTMAX_ROWS, hh * N_MEM:(hh + 1) * N_MEM] = (p * (1.0 / den)).astype(BF16)
    oc = _dot(p_ref[0:TM, :], vblk_ref[...])

    h = _merge_ln1(x, oa_ref[...], ob_ref[...], oc, z_ref[:, GZ0:GZ0 + 3 * D_MODEL],
                   wa_ref, wb_ref, wc_ref, wo_ref, g1_ref[...], b1_ref[...])
    h_ref[...] = h
    hb = h.astype(BF16)
    logits = _dot(hb, wr_ref[...]) + br_ref[...]
    _project_q(xn_ref[...].astype(BF16), wq_ref, zq_ref)
    xs_ref[...], route_ref[...], counts_ref[...] = _route_and_sort(logits, hb, xs_ref.shape[0])


def _front(x, rope, sinks, wq, wrest, wa, wb, wc, wo, wmix, pscale, mk, mv, g1, b1, wr, br):
    b, l, _ = x.shape
    nt = l // TM
    cap = _chunk_rows(b * nt)
    hist = 2 * SUBLANES
    tile = lambda w: pl.BlockSpec((None, TM, w), lambda bi, ti: (bi, ti, 0))
    per_b = lambda r, w: pl.BlockSpec((None, r, w), lambda bi, ti: (bi, 0, 0))
    nxt = lambda bi, ti: jnp.minimum(bi * nt + ti + 1, b * nt - 1)
    return pl.pallas_call(
        _front_kernel,
        grid=(b, nt),
        in_specs=[
            pl.BlockSpec(memory_space=pltpu.SMEM),
            tile(D_MODEL),
            pl.BlockSpec((None, TM, D_MODEL), lambda bi, ti: (nxt(bi, ti) // nt, nxt(bi, ti) % nt, 0)),
            pl.BlockSpec((3, TM, LANES), lambda bi, ti: (0, ti, 0)),
            _const_spec((D_MODEL, Q_WIDTH)), _const_spec((D_MODEL, IN_WIDTH - K0)),
            _const_spec((Q_WIDTH, D_MODEL)), _const_spec((POOL_WIDTH, D_MODEL)),
            _const_spec((MEM_WIDTH, D_MODEL)), _const_spec((D_MODEL, D_MODEL)),
            _const_spec((len(POOL_WINDOWS), POOL_GROUP_DIM, POOL_GROUP_DIM)),
            _const_spec((1, POOL_WIDTH)),
            per_b(N_MEM, MEM_WIDTH), per_b(N_MEM, MEM_WIDTH),
            _const_spec((1, D_MODEL)), _const_spec((1, D_MODEL)),
            _const_spec((D_MODEL, LANES)), _const_spec((1, LANES)),
        ],
        out_specs=[
            tile(D_MODEL),
            pl.BlockSpec((cap, D_MODEL), lambda bi, ti: (bi * nt + ti, 0)),
            tile(LANES),
            pl.BlockSpec((None, None, LANES, LANES), lambda bi, ti: (bi, ti, 0, 0)),
            per_b(QB, KV_WIDTH), per_b(QB, KV_WIDTH), per_b(hist, POOL_WIDTH),
        ],
        out_shape=[
            jax.ShapeDtypeStruct((b, l, D_MODEL), F32),
            jax.ShapeDtypeStruct((b * nt * cap, D_MODEL), BF16),
            jax.ShapeDtypeStruct((b, l, LANES), F32),
            jax.ShapeDtypeStruct((b, nt, LANES, LANES), F32),
            jax.ShapeDtypeStruct((b, QB, KV_WIDTH), F32),
            jax.ShapeDtypeStruct((b, QB, KV_WIDTH), F32),
            jax.ShapeDtypeStruct((b, hist, POOL_WIDTH), F32),
        ],
        scratch_shapes=[
            pltpu.VMEM((TM, IN_WIDTH), F32),
            pltpu.VMEM((TM, Q_WIDTH), BF16),
            pltpu.VMEM((QB + TM, KV_WIDTH), BF16),
            pltpu.VMEM((QB + TM, KV_WIDTH), BF16),
            pltpu.VMEM((hist + TM, POOL_WIDTH), F32),
            pltpu.VMEM((TM, Q_WIDTH), BF16),
            pltpu.VMEM((TM, POOL_WIDTH), F32),
            pltpu.VMEM((2, QB, 2 * QB), F32),
            pltpu.VMEM((GROUP * QB, 2 * QB), F32),
            pltpu.VMEM((GROUP * QB, N_KV * 2 * QB), BF16),
            pltpu.VMEM((N_KV * 2 * QB, KV_WIDTH), BF16),
            pltpu.VMEM((TM, Q_WIDTH), F32),
        ],
        compiler_params=pltpu.CompilerParams(
            dimension_semantics=("arbitrary", "arbitrary"), vmem_limit_bytes=VMEM_LIMIT),
        name="front_prompt",
    )(sinks, x, x, rope, wq, wrest, wa, wb, wc, wo, wmix, pscale, mk, mv, g1, b1, wr, br)


def _expert_mlp(xb, wg, wu, wd):
    a = _dot(xb, wg)
    hid = (a * jax.nn.sigmoid(a)) * _dot(xb, wu)
    return _dot(hid.astype(BF16), wd)


def _gemm_kernel(src_ref, dst_ref, te_ref, act_ref, xs_ref, wg_ref, wu_ref, wd_ref, ys_ref,
                 xbuf, obuf, wgb, wub, wdb, prime, sem_in, sem_out):
    i = pl.program_id(0)
    n = pl.num_programs(0)
    slot = i % 2
    in_slot = i % IN_SLOTS

    def start_in(tile, slot):
        for j in range(PIECES_PER_TILE):
            row0 = pl.multiple_of(src_ref[tile * PIECES_PER_TILE + j], PIECE)
            pltpu.make_async_copy(xs_ref.at[pl.ds(row0, PIECE)], xbuf.at[slot, pl.ds(j * PIECE, PIECE)],
                                  sem_in.at[slot]).start()

    def start_out(tile, slot):
        for j in range(PIECES_PER_TILE):
            row0 = pl.multiple_of(dst_ref[tile * PIECES_PER_TILE + j], PIECE)
            pltpu.make_async_copy(obuf.at[slot, pl.ds(j * PIECE, PIECE)], ys_ref.at[pl.ds(row0, PIECE)],
                                  sem_out.at[slot]).start()

    def wait_in(slot):
        for j in range(PIECES_PER_TILE):
            pltpu.make_async_copy(xs_ref.at[pl.ds(0, PIECE)], xbuf.at[slot, pl.ds(j * PIECE, PIECE)],
                                  sem_in.at[slot]).wait()

    def wait_out(slot):
        for j in range(PIECES_PER_TILE):
            pltpu.make_async_copy(obuf.at[slot, pl.ds(j * PIECE, PIECE)], ys_ref.at[pl.ds(0, PIECE)],
                                  sem_out.at[slot]).wait()

    active = act_ref[i] > 0
    ahead = IN_SLOTS - 1
    prefetched = (i < ahead) | (act_ref[jnp.maximum(i - ahead, 0)] > 0)
    out_pending = (i < 2) | (act_ref[jnp.maximum(i - 2, 0)] > 0)

    @pl.when(i == 0)
    def _():
        prime[0] = jnp.zeros((PIECE, D_MODEL), BF16)
        for s in range(2):
            for j in range(PIECES_PER_TILE):
                pltpu.make_async_copy(prime.at[0], prime.at[1 + s * PIECES_PER_TILE + j], sem_out.at[s]).start()
        for t in range(ahead):
            start_in(t, t)

    prev_active = (i >= 1) & (act_ref[jnp.maximum(i - 1, 0)] > 0)

    def active_step(after_first):
        @pl.when((i == 0) | (te_ref[i] != te_ref[jnp.maximum(i - 1, 0)]))
        def _():
            wgb[...] = wg_ref[...].astype(BF16)
            wub[...] = wu_ref[...].astype(BF16)
            wdb[...] = wd_ref[...].astype(BF16)

        wait_in(in_slot)
        wait_out(slot)
        out = _expert_mlp(xbuf[in_slot], wgb[...], wub[...], wdb[...])
        start_in(i + ahead, (i + ahead) % IN_SLOTS)
        if after_first:
            start_out(i - 1, 1 - slot)
        obuf[slot] = out.astype(BF16)

    @pl.when(active & (i == 0))
    def _():
        active_step(False)

    @pl.when(active & (i > 0))
    def _():
        active_step(True)

    @pl.when(jnp.logical_not(active))
    def _():
        @pl.when(prev_active)
        def _():
            start_out(i - 1, 1 - slot)

        @pl.when(prefetched)
        def _():
            wait_in(in_slot)

        @pl.when(out_pending)
        def _():
            wait_out(slot)

    @pl.when(i == n - 1)
    def _():
        for t in range(ahead):
            @pl.when(act_ref[jnp.maximum(i - t, 0)] > 0)
            def _(t=t):
                wait_in((i - t + ahead) % IN_SLOTS)

        @pl.when(active)
        def _():
            start_out(i, slot)
            wait_out(slot)

        @pl.when(prev_active)
        def _():
            wait_out(1 - slot)


def _grouped_gemm(piece_src, piece_dst, tile_expert, tile_active, xs, wg, wu, wd):
    n_tiles = tile_expert.shape[0]
    assert n_tiles >= 2
    wspec = lambda r, c: pl.BlockSpec((None, r, c), lambda i, src, dst, te, act: (te[i], 0, 0))
    return pl.pallas_call(
        _gemm_kernel,
        grid_spec=pltpu.PrefetchScalarGridSpec(
            num_scalar_prefetch=4,
            grid=(n_tiles,),
            in_specs=[pl.BlockSpec(memory_space=pl.ANY),
                      wspec(D_MODEL, D_EXPERT), wspec(D_MODEL, D_EXPERT), wspec(D_EXPERT, D_MODEL)],
            out_specs=pl.BlockSpec(memory_space=pl.ANY),
            scratch_shapes=[pltpu.VMEM((IN_SLOTS, TG, D_MODEL), BF16), pltpu.VMEM((2, TG, D_MODEL), BF16),
                            pltpu.VMEM((D_MODEL, D_EXPERT), BF16), pltpu.VMEM((D_MODEL, D_EXPERT), BF16),
                            pltpu.VMEM((D_EXPERT, D_MODEL), BF16),
                            pltpu.VMEM((1 + 2 * PIECES_PER_TILE, PIECE, D_MODEL), BF16),
                            pltpu.SemaphoreType.DMA((IN_SLOTS,)), pltpu.SemaphoreType.DMA((2,))],
        ),
        out_shape=jax.ShapeDtypeStruct(xs.shape, xs.dtype),
        input_output_aliases={4: 0},
        compiler_params=pltpu.CompilerParams(dimension_semantics=("arbitrary",), vmem_limit_bytes=VMEM_LIMIT),
        name="moe_grouped_gemm",
    )(piece_src, piece_dst, tile_expert, tile_active, xs, wg, wu, wd)


def _combine_kernel(ys_ref, h_ref, route_ref, g2_ref, b2_ref, y_ref):
    chunks = h_ref.shape[0] // TM
    cap = ys_ref.shape[0] // chunks
    slot = lax.broadcasted_iota(jnp.int32, (TM, cap), 1).astype(F32)
    for c in range(chunks):
        rows = slice(c * TM, (c + 1) * TM)
        route = route_ref[rows, :]
        sel = jnp.concatenate([jnp.where(slot == route[:, 2:3], 1.0, 0.0).astype(BF16),
                               jnp.where(slot == route[:, 3:4], 1.0, 0.0).astype(BF16)], axis=0)
        picked = _dot(sel, ys_ref[c * cap:(c + 1) * cap, :])
        f = route[:, 0:1] * picked[0:TM] + route[:, 1:2] * picked[TM:2 * TM]
        y_ref[rows, :] = _layer_norm(ALPHA * h_ref[rows, :] + f, g2_ref[...], b2_ref[...])


def _combine(ys, h, route, g2, b2):
    t = h.shape[0]
    cap = ys.shape[0] // (t // TM)
    per_step = next(k for k in (4, 2, 1) if (t // TM) % k == 0)
    rows = per_step * TM
    return pl.pallas_call(
        _combine_kernel,
        grid=(t // rows,),
        in_specs=[
            pl.BlockSpec((per_step * cap, D_MODEL), lambda i: (i, 0)),
            pl.BlockSpec((rows, D_MODEL), lambda i: (i, 0)),
            pl.BlockSpec((rows, LANES), lambda i: (i, 0)),
            pl.BlockSpec((1, D_MODEL), lambda i: (0, 0)),
            pl.BlockSpec((1, D_MODEL), lambda i: (0, 0)),
        ],
        out_specs=pl.BlockSpec((rows, D_MODEL), lambda i: (i, 0)),
        out_shape=jax.ShapeDtypeStruct((t, D_MODEL), F32),
        compiler_params=pltpu.CompilerParams(dimension_semantics=("arbitrary",)),
        name="moe_combine",
    )(ys, h, route, g2, b2)


def _select(table, idx):
    hot = idx[:, None] == jnp.arange(table.shape[0], dtype=jnp.int32)[None, :]
    return jnp.sum(jnp.where(hot[:, :, None], table[None, :, :], 0), axis=1)


def _piece_tables(counts):
    n_chunks = counts.shape[0]
    n_tiles = -(-(n_chunks * MAX_CHUNK_PIECES + N_EXPERTS * (PIECES_PER_TILE - 1)) // PIECES_PER_TILE)
    npc = (counts + (PIECE - 1)) // PIECE
    first = (jnp.cumsum(npc, axis=1) - npc).T
    npc_t = npc.T
    cum = jnp.cumsum(npc_t, axis=1)
    per_expert = cum[:, -1]
    tiles_e = (per_expert + (PIECES_PER_TILE - 1)) // PIECES_PER_TILE
    tile_end = jnp.cumsum(tiles_e)
    tile_idx = jnp.arange(n_tiles, dtype=jnp.int32)
    expert_of = lambda i: jnp.minimum(jnp.sum(i[:, None] >= tile_end[None, :], axis=1), N_EXPERTS - 1).astype(jnp.int32)
    active = tile_idx < tile_end[-1]
    tile_expert = jnp.where(active, expert_of(tile_idx), expert_of(tile_end[-1:] - 1))
    meta = jnp.stack([tile_end - tiles_e, per_expert], axis=1)
    meta_t = _select(meta, tile_expert)
    k = (tile_idx - meta_t[:, 0])[:, None] * PIECES_PER_TILE + jnp.arange(PIECES_PER_TILE, dtype=jnp.int32)[None, :]
    valid = active[:, None] & (k < meta_t[:, 1:2])
    cum_t = _select(cum, tile_expert)
    chunk = jnp.minimum(jnp.sum(k[:, :, None] >= cum_t[:, None, :], axis=2), n_chunks - 1).astype(jnp.int32)
    at_chunk = chunk[:, :, None] == jnp.arange(n_chunks, dtype=jnp.int32)[None, None, :]
    pick = lambda tab: jnp.sum(jnp.where(at_chunk, _select(tab, tile_expert)[:, None, :], 0), axis=2)
    piece = pick(first) + k - pick(cum - npc_t)
    cap = _chunk_rows(n_chunks)
    rows = chunk * cap + piece * PIECE
    d = ((tile_idx % 2)[:, None] * (PIECES_PER_TILE - 1)
         + jnp.maximum(jnp.arange(PIECES_PER_TILE, dtype=jnp.int32)[None, :] - 1, 0))
    spare_row = lambda d: (d % n_chunks) * cap + (MAX_CHUNK_PIECES + d // n_chunks) * PIECE
    spare = spare_row(d)
    zero_piece = spare_row(N_SPARE - 1)
    extra = jnp.full(((IN_SLOTS - 1) * PIECES_PER_TILE,), zero_piece, jnp.int32)
    src = jnp.concatenate([jnp.where(valid, rows, zero_piece).astype(jnp.int32).reshape(-1), extra])
    dst = jnp.concatenate([jnp.where(valid, rows, spare).astype(jnp.int32).reshape(-1), extra])
    return src, dst, tile_expert.astype(jnp.int32), active.astype(jnp.int32)


def _sample_attn_kernel(x_ref, rope_ref, wq_ref, wrest_ref, sink_ref, ck_ref, cv_ref, cmk_ref, cmv_ref,
                        z_ref, oa_ref, oc_ref, nk_ref, nv_ref, knew_t, vnew_t):
    j = pl.program_id(0)
    db = x_ref.shape[0]

    @pl.when(j == 0)
    def _():
        xb = x_ref[...].astype(BF16)
        _project_q(xb, wq_ref, z_ref)
        _project_rest(xb, wrest_ref, z_ref)
        c = rope_ref[0]
        s1 = rope_ref[1]
        s2 = rope_ref[2]
        for jj in range((Q_WIDTH + KV_WIDTH) // LANES):
            sl = slice(jj * LANES, (jj + 1) * LANES)
            z_ref[:, sl] = _rope(z_ref[:, sl], c, s1, s2)
        knew_t[...] = z_ref[:, K0:K0 + KV_WIDTH].T
        vnew_t[...] = z_ref[:, V0:V0 + KV_WIDTH].T

    r0 = pl.multiple_of(j * SB, SB)
    zq = z_ref[pl.ds(r0, SB), Q0:Q0 + Q_WIDTH]
    zk = z_ref[pl.ds(r0, SB), K0:K0 + KV_WIDTH]
    zv = z_ref[pl.ds(r0, SB), V0:V0 + KV_WIDTH]
    zc = z_ref[pl.ds(r0, SB), CQ0:CQ0 + MEM_WIDTH] * ATT_SCALE
    sink = sink_ref[:, 0:1]
    row_kv = lax.broadcasted_iota(jnp.int32, (N_HEADS, KV_WIDTH), 0) & (N_KV - 1)
    lane_kv = lax.broadcasted_iota(jnp.int32, (N_HEADS, KV_WIDTH), 1) // HEAD_DIM
    own = row_kv == lane_kv
    row_c = lax.broadcasted_iota(jnp.int32, (N_HEADS, MEM_WIDTH), 0)
    lane_c = lax.broadcasted_iota(jnp.int32, (N_HEADS, MEM_WIDTH), 1) // HEAD_DIM
    own_c = row_c == lane_c
    last_pos = lax.broadcasted_iota(jnp.int32, (KV_WIDTH, WINDOW), 1) == WINDOW - 1
    seq_lane = lax.broadcasted_iota(jnp.int32, (KV_WIDTH, db), 1)

    seqs = range(SB)
    qblk, cblk, s, sc = [], [], [], []
    for b in seqs:
        q4 = jnp.concatenate(
            [jnp.broadcast_to(zq[b:b + 1, g * KV_WIDTH:(g + 1) * KV_WIDTH], (N_KV, KV_WIDTH)) for g in range(GROUP)],
            axis=0)
        qblk.append(jnp.where(own, q4, 0.0).astype(BF16))
        cblk.append(jnp.where(own_c, jnp.broadcast_to(zc[b:b + 1, :], (N_HEADS, MEM_WIDTH)), 0.0).astype(BF16))
        s.append(_dot(qblk[b], ck_ref[b].astype(BF16)))
        sc.append(_dot(cblk[b], cmk_ref[b].astype(BF16)))

    pn, p_new, pc = [], [], []
    for b in seqs:
        s_new = jnp.sum(qblk[b].astype(F32) * zk[b:b + 1, :].astype(BF16).astype(F32), axis=1, keepdims=True)
        m = jnp.maximum(jnp.maximum(jnp.max(s[b], axis=1, keepdims=True), s_new), sink)
        p = jnp.exp(s[b] - m)
        e_new = jnp.exp(s_new - m)
        inv = 1.0 / (jnp.sum(p, axis=1, keepdims=True) + e_new + jnp.exp(sink - m))
        pn.append((p * inv).astype(BF16))
        p_new.append((e_new * inv).astype(BF16).astype(F32))
        e = jnp.exp(sc[b] - jnp.max(sc[b], axis=1, keepdims=True))
        pc.append((e * (1.0 / jnp.sum(e, axis=1, keepdims=True))).astype(BF16))

    oa_rows = [[] for _ in range(GROUP)]
    oc_rows = []
    for b in seqs:
        vc = cv_ref[b]
        o = _dot_nt(pn[b], vc.astype(BF16)) + p_new[b] * zv[b:b + 1, :].astype(BF16).astype(F32)
        o = jnp.where(own, o, 0.0)
        for g in range(GROUP):
            oa_rows[g].append(jnp.sum(o[g * N_KV:(g + 1) * N_KV], axis=0, keepdims=True))
        ocb = jnp.where(own_c, _dot_nt(pc[b], cmv_ref[b].astype(BF16)), 0.0)
        oc_rows.append(jnp.sum(ocb, axis=0, keepdims=True))
        here = seq_lane == j * SB + b
        knew_col = jnp.sum(jnp.where(here, knew_t[...], 0.0), axis=1, keepdims=True)
        vnew_col = jnp.sum(jnp.where(here, vnew_t[...], 0.0), axis=1, keepdims=True)
        nk_ref[b] = jnp.where(last_pos, knew_col, pltpu.roll(ck_ref[b], WINDOW - 1, 1))
        nv_ref[b] = jnp.where(last_pos, vnew_col, pltpu.roll(vc, WINDOW - 1, 1))
    for g in range(GROUP):
        oa_ref[pl.ds(r0, SB), g * KV_WIDTH:(g + 1) * KV_WIDTH] = jnp.concatenate(oa_rows[g], axis=0)
    oc_ref[pl.ds(r0, SB), :] = jnp.concatenate(oc_rows, axis=0)


def _sample_attn(x, rope, wq, wrest, sink_gk, ck, cv, cmk, cmv):
    db = x.shape[0]
    blk = lambda r: pl.BlockSpec((SB, KV_WIDTH, r), lambda j: (j, 0, 0))
    full = lambda w: pl.BlockSpec((db, w), lambda j: (0, 0))
    return pl.pallas_call(
        _sample_attn_kernel,
        grid=(db // SB,),
        in_specs=[
            full(D_MODEL), _const_spec((3, 1, LANES)),
            _const_spec((D_MODEL, Q_WIDTH)), _const_spec((D_MODEL, IN_WIDTH - K0)),
            _const_spec((N_HEADS, LANES)),
            blk(WINDOW), blk(WINDOW), blk(N_MEM), blk(N_MEM),
        ],
        out_specs=[full(IN_WIDTH), full(Q_WIDTH), full(MEM_WIDTH), blk(WINDOW), blk(WINDOW)],
        out_shape=[
            jax.ShapeDtypeStruct((db, IN_WIDTH), F32),
            jax.ShapeDtypeStruct((db, Q_WIDTH), F32),
            jax.ShapeDtypeStruct((db, MEM_WIDTH), F32),
            jax.ShapeDtypeStruct((db, KV_WIDTH, WINDOW), F32),
            jax.ShapeDtypeStruct((db, KV_WIDTH, WINDOW), F32),
        ],
        scratch_shapes=[pltpu.VMEM((KV_WIDTH, db), F32), pltpu.VMEM((KV_WIDTH, db), F32)],
        compiler_params=pltpu.CompilerParams(dimension_semantics=("arbitrary",), vmem_limit_bytes=VMEM_LIMIT),
        name="sample_attn",
    )(x, rope, wq, wrest, sink_gk, ck, cv, cmk, cmv)


def _sample_tail_kernel(x_ref, z_ref, oa_ref, oc_ref, st_ref, wa_ref, wb_ref, wc_ref, wo_ref, wmix_ref,
                        pscale_ref, g1_ref, b1_ref, wr_ref, br_ref, wg_ref, wu_ref, wd_ref, g2_ref, b2_ref,
                        y_ref, npool_ref, h_sc, comb_sc, acc_sc):
    e = pl.program_id(0)

    @pl.when(e == 0)
    def _():
        u = z_ref[:, U0:U0 + POOL_WIDTH]
        npool_ref[0:POOL_STATE - 1] = st_ref[1:POOL_STATE]
        npool_ref[POOL_STATE - 1] = u
        obs = []
        for g, w in enumerate(POOL_WINDOWS):
            sl = slice(g * POOL_GROUP_DIM, (g + 1) * POOL_GROUP_DIM)
            cur = u[:, sl]
            ws = cur
            for jj in range(1, w):
                ws = ws + st_ref[POOL_STATE - jj, :, sl]
            cnt = float(min(PAST_LEN + 1, w))
            pooled = ws / cnt - cur
            obs.append(_dot(pooled.astype(BF16), wmix_ref[g]) * pscale_ref[:, sl])
        ob = jnp.concatenate(obs, axis=1)
        h = _merge_ln1(x_ref[...], oa_ref[...].astype(BF16), ob, oc_ref[...], z_ref[:, GZ0:GZ0 + 3 * D_MODEL],
                       wa_ref, wb_ref, wc_ref, wo_ref, g1_ref[...], b1_ref[...])
        h_sc[...] = h
        logits = _dot(h.astype(BF16), wr_ref[...]) + br_ref[...]
        hot1, hot2, w1, w2 = _route(logits)
        comb_sc[...] = jnp.where(hot1, w1, 0.0) + jnp.where(hot2, w2, 0.0)
        acc_sc[...] = jnp.zeros_like(acc_sc)

    out = _expert_mlp(h_sc[...].astype(BF16), wg_ref[...].astype(BF16), wu_ref[...].astype(BF16),
                      wd_ref[...].astype(BF16))
    lane = lax.broadcasted_iota(jnp.int32, comb_sc.shape, 1)
    ce = jnp.sum(jnp.where(lane == e, comb_sc[...], 0.0), axis=1, keepdims=True)
    acc_sc[...] += ce * out

    @pl.when(e == pl.num_programs(0) - 1)
    def _():
        y_ref[...] = _layer_norm(ALPHA * h_sc[...] + acc_sc[...], g2_ref[...], b2_ref[...])


def _sample_tail(x, z, oa, oc, state, wa, wb, wc, wo, wmix, pscale, g1, b1, wr, br, wg, wu, wd, g2, b2):
    db = x.shape[0]
    full = lambda w: pl.BlockSpec((db, w), lambda e: (0, 0))
    vec = lambda w: pl.BlockSpec((1, w), lambda e: (0, 0))
    hist = pl.BlockSpec((POOL_STATE, db, POOL_WIDTH), lambda e: (0, 0, 0))
    return pl.pallas_call(
        _sample_tail_kernel,
        grid=(N_EXPERTS,),
        in_specs=[
            full(D_MODEL), full(IN_WIDTH), full(Q_WIDTH), full(MEM_WIDTH), hist,
            _const_spec((Q_WIDTH, D_MODEL)), _const_spec((POOL_WIDTH, D_MODEL)),
            _const_spec((MEM_WIDTH, D_MODEL)), _const_spec((D_MODEL, D_MODEL)),
            _const_spec((len(POOL_WINDOWS), POOL_GROUP_DIM, POOL_GROUP_DIM)),
            vec(POOL_WIDTH), vec(D_MODEL), vec(D_MODEL),
            _const_spec((D_MODEL, LANES)), vec(LANES),
            pl.BlockSpec((None, D_MODEL, D_EXPERT), lambda e: (e, 0, 0)),
            pl.BlockSpec((None, D_MODEL, D_EXPERT), lambda e: (e, 0, 0)),
            pl.BlockSpec((None, D_EXPERT, D_MODEL), lambda e: (e, 0, 0)),
            vec(D_MODEL), vec(D_MODEL),
        ],
        out_specs=[full(D_MODEL), hist],
        out_shape=[jax.ShapeDtypeStruct((db, D_MODEL), F32),
                   jax.ShapeDtypeStruct((POOL_STATE, db, POOL_WIDTH), F32)],
        scratch_shapes=[pltpu.VMEM((db, D_MODEL), F32), pltpu.VMEM((db, LANES), F32),
                        pltpu.VMEM((db, D_MODEL), F32)],
        compiler_params=pltpu.CompilerParams(dimension_semantics=("arbitrary",), vmem_limit_bytes=VMEM_LIMIT),
        name="sample_tail",
    )(x, z, oa, oc, state, wa, wb, wc, wo, wmix, pscale, g1, b1, wr, br, wg, wu, wd, g2, b2)


def _rope_tables(pos):
    half = ROPE_DIM // 2
    inv = jnp.power(ROPE_THETA, -jnp.arange(half, dtype=F32) * (2.0 / ROPE_DIM))
    ang = pos.astype(F32)[:, None] * inv[None, :]
    lane = np.arange(LANES)
    off = lane % HEAD_DIM
    cos = jnp.cos(ang)[:, lane % half]
    sin = jnp.sin(ang)[:, lane % half]
    c = jnp.where(off[None, :] < ROPE_DIM, cos, 1.0)
    s1 = jnp.where((off[None, :] >= half) & (off[None, :] < ROPE_DIM), sin, 0.0)
    s2 = jnp.where(off[None, :] < half, -sin, 0.0)
    return jnp.stack([c, s1, s2]).astype(F32)


def _q_heads_group_major(w, axis):
    if axis == 1:
        n = w.shape[0]
        return w.reshape(n, N_KV, GROUP, HEAD_DIM).transpose(0, 2, 1, 3).reshape(n, Q_WIDTH)
    n = w.shape[1]
    return w.reshape(N_KV, GROUP, HEAD_DIM, n).transpose(1, 0, 2, 3).reshape(Q_WIDTH, n)


def kernel(x_prompt, x_sample, cache_win_k, cache_win_v, state_pool, cache_mem_k, cache_mem_v, mem_prompt, w_in, sinks, w_pool_mix, pool_scale, w_mem_k, w_mem_v, w_branch_a, w_branch_b, w_branch_c, w_out, ln1_g, ln1_b, w_group, b_group, w_router, b_router, w_gate, w_up, w_down, ln2_g, ln2_b):
    assert w_in.shape[0] == DEPTH == 1
    b, l, _ = x_prompt.shape
    db, ds, _ = x_sample.shape
    assert ds == 1 and l % TM == 0 and db % SB == 0
    assert cache_win_k.shape[2] == WINDOW
    t = b * l

    win = w_in[0]
    wq = (_q_heads_group_major(win[:, Q0:Q0 + Q_WIDTH], 1) * ATT_SCALE).astype(BF16)
    wrest = win[:, K0:].astype(BF16)
    wa = _q_heads_group_major(w_branch_a[0], 0).astype(BF16)
    wb = w_branch_b[0].astype(BF16)
    wc = w_branch_c[0].astype(BF16)
    wo = w_out[0].astype(BF16)
    wmix = w_pool_mix[0].astype(BF16)
    pscale = pool_scale[0].reshape(1, POOL_WIDTH)
    g1 = ln1_g[0].reshape(1, D_MODEL)
    b1 = ln1_b[0].reshape(1, D_MODEL)
    g2 = ln2_g[0].reshape(1, D_MODEL)
    b2 = ln2_b[0].reshape(1, D_MODEL)
    wr = jnp.concatenate([w_group[0], w_router[0].reshape(D_MODEL, N_EXPERTS)], axis=1)
    wr = jnp.pad(wr, ((0, 0), (0, LANES - wr.shape[1]))).astype(BF16)
    br = jnp.pad(jnp.concatenate([b_group[0], b_router[0].reshape(N_EXPERTS)]), (0, LANES - N_EXPERT_GROUPS - N_EXPERTS))
    br = br.reshape(1, LANES).astype(F32)
    wg = w_gate[0]
    wu = w_up[0]
    wd = w_down[0]
    sink = sinks[0].astype(F32)
    sink_gk = jnp.broadcast_to(sink.reshape(N_KV, GROUP).T.reshape(N_HEADS, 1), (N_HEADS, LANES))

    mk, mv = _mem_project(mem_prompt, w_mem_k[0].astype(BF16), w_mem_v[0].astype(BF16))
    rope_p = _rope_tables(jnp.arange(l, dtype=jnp.int32))
    h, xs, route, counts, nk_p, nv_p, npool_p = _front(
        x_prompt, rope_p, sink, wq, wrest, wa, wb, wc, wo, wmix, pscale, mk, mv, g1, b1, wr, br)
    piece_src, piece_dst, tile_expert, tile_active = _piece_tables(
        counts.reshape(-1, LANES, LANES)[:, :N_EXPERTS, 0].astype(jnp.int32))
    ys = _grouped_gemm(piece_src, piece_dst, tile_expert, tile_active, xs, wg, wu, wd)
    y_p = _combine(ys, h.reshape(t, D_MODEL), route.reshape(t, LANES), g2, b2).reshape(b, l, D_MODEL)

    rope_s = _rope_tables(jnp.full((1,), PAST_LEN, jnp.int32))
    xs = x_sample.reshape(db, D_MODEL)
    feat_major = lambda c: jnp.transpose(c[0], (0, 2, 3, 1)).reshape(db, KV_WIDTH, c.shape[2])
    from_feat_major = lambda a: jnp.transpose(a.reshape(db, N_KV, HEAD_DIM, a.shape[2]), (0, 3, 1, 2))[None]
    z_s, oa_s, oc_s, nk_s, nv_s = _sample_attn(xs, rope_s, wq, wrest, sink_gk, feat_major(cache_win_k),
                                               feat_major(cache_win_v), feat_major(cache_mem_k), feat_major(cache_mem_v))
    state = jnp.transpose(state_pool[0], (1, 0, 2))
    y_s, npool_s = _sample_tail(xs, z_s, oa_s, oc_s, state, wa, wb, wc, wo, wmix, pscale, g1, b1, wr, br,
                                wg, wu, wd, g2, b2)

    kv5 = lambda a, n, w: a.reshape(1, n, w, N_KV, HEAD_DIM)
    return (y_p, y_s.reshape(db, 1, D_MODEL),
            kv5(nk_p, b, QB), kv5(nv_p, b, QB),
            npool_p[:, 2 * SUBLANES - POOL_STATE:, :][None],
            kv5(mk, b, N_MEM), kv5(mv, b, N_MEM),
            from_feat_major(nk_s), from_feat_major(nv_s),
            jnp.transpose(npool_s, (1, 0, 2))[None])
```

```python
import jax
import jax.numpy as jnp
import numpy as np
from jax import lax
from jax.experimental import pallas as pl
from jax.experimental.pallas import tpu as pltpu

D_MODEL = 1024
N_HEADS = 16
HEAD_DIM = 64
N_KV = 4
GROUP = N_HEADS // N_KV
WINDOW = 128
ROPE_THETA = 500000.0
ROPE_DIM = HEAD_DIM // 4
Q_WIDTH = N_HEADS * HEAD_DIM
KV_WIDTH = N_KV * HEAD_DIM
POOL_WINDOWS = (2, 4, 8, 16)
POOL_WIDTH = D_MODEL // 2
POOL_GROUP_DIM = POOL_WIDTH // len(POOL_WINDOWS)
POOL_STATE = max(POOL_WINDOWS) - 1
N_MEM = 256
MEM_HEADS = 4
MEM_WIDTH = MEM_HEADS * HEAD_DIM
N_EXPERT_GROUPS = 4
EXPERTS_PER_GROUP = 4
N_EXPERTS = N_EXPERT_GROUPS * EXPERTS_PER_GROUP
D_EXPERT = 512
PAST_LEN = 16384
DEPTH = 1
ALPHA = (2.0 * DEPTH) ** 0.25
LN_EPS = 1e-5

Q0 = 0
K0 = Q0 + Q_WIDTH
V0 = K0 + KV_WIDTH
U0 = V0 + KV_WIDTH
CQ0 = U0 + POOL_WIDTH
GZ0 = CQ0 + MEM_WIDTH
IN_WIDTH = GZ0 + 3 * D_MODEL

LANES = 128
SUBLANES = 8
VMEM_LIMIT = 56 * 1024 * 1024

TM = 256
QB = WINDOW
PIECE = 16
PIECES_PER_TILE = 32
TG = PIECE * PIECES_PER_TILE
MAX_CHUNK_PIECES = 2 * TM // PIECE + N_EXPERTS - 1
N_SPARE = 2 * (PIECES_PER_TILE - 1) + 1
IN_SLOTS = 3


def _chunk_rows(n_chunks):
    spare = -(-N_SPARE // n_chunks)
    return -(-(MAX_CHUNK_PIECES + spare) * PIECE // LANES) * LANES
SB = 16
IN_CHUNK = 768
Q_CHUNK = 512
ATT_SCALE = HEAD_DIM ** -0.5
SOFTMAX_ROWS = 64
assert N_MEM == 2 * QB and MEM_HEADS == N_KV and MEM_WIDTH == KV_WIDTH and TM <= GROUP * QB

BF16 = jnp.bfloat16
F32 = jnp.float32
NEG_INF = float("-inf")


def _const_spec(shape):
    nd = len(shape)
    return pl.BlockSpec(shape, lambda *_: (0,) * nd, pipeline_mode=pl.Buffered(1))


def _layer_norm(x, g, b):
    mu = jnp.mean(x, axis=-1, keepdims=True)
    xc = x - mu
    var = jnp.mean(xc * xc, axis=-1, keepdims=True)
    return xc * lax.rsqrt(var + LN_EPS) * g + b


def _dot(a, b):
    return jnp.dot(a, b, preferred_element_type=F32)


def _dot_nt(a, b):
    return lax.dot_general(a, b, (((1,), (1,)), ((), ())), preferred_element_type=F32)


def _lane_block_mask(shape, block, width=HEAD_DIM):
    lane = lax.broadcasted_iota(jnp.int32, shape, len(shape) - 1)
    return (lane >= block * width) & (lane < (block + 1) * width)


def _rope(x, c, s1, s2):
    half = ROPE_DIM // 2
    return x * c + pltpu.roll(x, half, 1) * s1 + pltpu.roll(x, LANES - half, 1) * s2


def _route(logits):
    rows = logits.shape[0]
    lane = lax.broadcasted_iota(jnp.int32, (rows, LANES), 1)
    lanef = lane.astype(F32)
    big = float(LANES)
    is_g = lane < N_EXPERT_GROUPS
    glog = jnp.where(is_g, logits, NEG_INF)
    gmax = jnp.max(glog, axis=1, keepdims=True)
    gsum = jnp.sum(jnp.where(is_g, jnp.exp(glog - gmax), 0.0), axis=1, keepdims=True)
    gp = 1.0 / gsum
    gidx = jnp.min(jnp.where(glog == gmax, lanef, big), axis=1, keepdims=True).astype(jnp.int32)
    lo = N_EXPERT_GROUPS + gidx * EXPERTS_PER_GROUP
    in_grp = (lane >= lo) & (lane < lo + EXPERTS_PER_GROUP)
    el = jnp.where(in_grp, logits, NEG_INF)
    v1 = jnp.max(el, axis=1, keepdims=True)
    i1 = jnp.min(jnp.where(el == v1, lanef, big), axis=1, keepdims=True).astype(jnp.int32)
    el2 = jnp.where(lane == i1, NEG_INF, el)
    v2 = jnp.max(el2, axis=1, keepdims=True)
    i2 = jnp.min(jnp.where(el2 == v2, lanef, big), axis=1, keepdims=True).astype(jnp.int32)
    e21 = jnp.exp(v2 - v1)
    inv = 1.0 / (1.0 + e21)
    w1 = inv * gp
    w2 = e21 * inv * gp
    e1 = i1 - N_EXPERT_GROUPS
    e2 = i2 - N_EXPERT_GROUPS
    return lane == e1, lane == e2, w1, w2


def _route_and_sort(logits, hb, cap):
    rows = hb.shape[0]
    lt = logits.T
    row = lax.broadcasted_iota(jnp.int32, (LANES, rows), 0)
    rowf = row.astype(F32)
    big = float(LANES)
    is_g = row < N_EXPERT_GROUPS
    glog = jnp.where(is_g, lt, NEG_INF)
    gmax = jnp.max(glog, axis=0, keepdims=True)
    gp = 1.0 / jnp.sum(jnp.where(is_g, jnp.exp(glog - gmax), 0.0), axis=0, keepdims=True)
    gidx = jnp.min(jnp.where(glog == gmax, rowf, big), axis=0, keepdims=True).astype(jnp.int32)
    lo = N_EXPERT_GROUPS + gidx * EXPERTS_PER_GROUP
    el = jnp.where((row >= lo) & (row < lo + EXPERTS_PER_GROUP), lt, NEG_INF)
    v1 = jnp.max(el, axis=0, keepdims=True)
    i1 = jnp.min(jnp.where(el == v1, rowf, big), axis=0, keepdims=True).astype(jnp.int32)
    el2 = jnp.where(row == i1, NEG_INF, el)
    v2 = jnp.max(el2, axis=0, keepdims=True)
    i2 = jnp.min(jnp.where(el2 == v2, rowf, big), axis=0, keepdims=True).astype(jnp.int32)
    e21 = jnp.exp(v2 - v1)
    inv = 1.0 / (1.0 + e21)
    w1 = inv * gp
    w2 = e21 * inv * gp
    hot1 = row == i1 - N_EXPERT_GROUPS
    hot2 = row == i2 - N_EXPERT_GROUPS
    onehot = jnp.where(hot1 | hot2, 1.0, 0.0)
    counts = jnp.broadcast_to(jnp.sum(onehot, axis=1, keepdims=True), (LANES, LANES))
    earlier = (lax.broadcasted_iota(jnp.int32, (rows, rows), 0)
               < lax.broadcasted_iota(jnp.int32, (rows, rows), 1)).astype(BF16)
    rank = _dot(onehot.astype(BF16), earlier)
    run = (((counts.astype(jnp.int32) + (PIECE - 1)) // PIECE) * PIECE).astype(BF16)
    below = (lax.broadcasted_iota(jnp.int32, (LANES, LANES), 1)
             < lax.broadcasted_iota(jnp.int32, (LANES, LANES), 0)).astype(BF16)
    start = _dot(below, run)
    slot = jnp.concatenate([start] * (rows // LANES), axis=1) + rank
    s1 = jnp.sum(jnp.where(hot1, slot, 0.0), axis=0, keepdims=True)
    s2 = jnp.sum(jnp.where(hot2, slot, 0.0), axis=0, keepdims=True)
    srow = lax.broadcasted_iota(jnp.int32, (cap, rows), 0).astype(F32)
    perm = jnp.where((srow == s1) | (srow == s2), 1.0, 0.0).astype(BF16)
    route_t = jnp.where(row == 0, w1, jnp.where(row == 1, w2, jnp.where(row == 2, s1, jnp.where(row == 3, s2, 0.0))))
    return _dot(perm, hb).astype(BF16), route_t.T, counts


def _sigmoid(x):
    return 0.5 * jnp.tanh(0.5 * x) + 0.5


def _merge_ln1(x, oa, ob, oc, gz, wa_ref, wb_ref, wc_ref, wo_ref, g1, b1):
    ya = _dot(oa, wa_ref[...])
    yb = _dot(ob.astype(BF16), wb_ref[...])
    yc = _dot(oc.astype(BF16), wc_ref[...])
    m = (_sigmoid(gz[:, 0:D_MODEL]) * ya
         + _sigmoid(gz[:, D_MODEL:2 * D_MODEL]) * yb
         + _sigmoid(gz[:, 2 * D_MODEL:3 * D_MODEL]) * yc)
    hpre = ALPHA * x + _dot(m.astype(BF16), wo_ref[...])
    return _layer_norm(hpre, g1, b1)


def _mem_kernel(mem_ref, wk_ref, wv_ref, mk_ref, mv_ref):
    m = mem_ref[...].astype(BF16)
    mk_ref[...] = _dot(m, wk_ref[...])
    mv_ref[...] = _dot(m, wv_ref[...])


def _mem_project(mem, wk, wv):
    b = mem.shape[0]
    out = jax.ShapeDtypeStruct((b, N_MEM, MEM_WIDTH), F32)
    return pl.pallas_call(
        _mem_kernel,
        grid=(b,),
        in_specs=[pl.BlockSpec((None, N_MEM, D_MODEL), lambda i: (i, 0, 0)),
                  _const_spec((D_MODEL, MEM_WIDTH)), _const_spec((D_MODEL, MEM_WIDTH))],
        out_specs=[pl.BlockSpec((None, N_MEM, MEM_WIDTH), lambda i: (i, 0, 0))] * 2,
        out_shape=[out, out],
        name="mem_project",
    )(mem, wk, wv)


def _project_q(xb, wq_ref, zq_ref):
    for c0 in range(0, Q_WIDTH, Q_CHUNK):
        zq_ref[:, c0:c0 + Q_CHUNK] = _dot(xb, wq_ref[:, c0:c0 + Q_CHUNK])


def _project_rest(xb, wrest_ref, z_ref):
    rest = IN_WIDTH - K0
    for c0 in range(0, rest, IN_CHUNK):
        c1 = min(c0 + IN_CHUNK, rest)
        z_ref[:, K0 + c0:K0 + c1] = _dot(xb, wrest_ref[:, c0:c1])


def _front_kernel(sinks_ref, x_ref, xn_ref, rope_ref, wq_ref, wrest_ref, wa_ref, wb_ref, wc_ref, wo_ref, wmix_ref,
                  pscale_ref, mk_ref, mv_ref, g1_ref, b1_ref, wr_ref, br_ref,
                  h_ref, xs_ref, route_ref, counts_ref, nk_ref, nv_ref, npool_ref,
                  z_ref, qb_ref, kext_ref, vext_ref, uext_ref, oa_ref, ob_ref, bias_ref, s_ref, p_ref, vblk_ref, zq_ref):
    i = pl.program_id(1)
    x = x_ref[...]
    xb = x.astype(BF16)
    hist = 2 * SUBLANES

    @pl.when((pl.program_id(0) == 0) & (i == 0))
    def _():
        _project_q(xb, wq_ref, zq_ref)

    @pl.when(i == 0)
    def _():
        kext_ref[0:QB, :] = jnp.zeros((QB, KV_WIDTH), BF16)
        vext_ref[0:QB, :] = jnp.zeros((QB, KV_WIDTH), BF16)
        uext_ref[0:hist, :] = jnp.zeros((hist, POOL_WIDTH), F32)

    @pl.when(i > 0)
    def _():
        kext_ref[0:QB, :] = kext_ref[TM:TM + QB, :]
        vext_ref[0:QB, :] = vext_ref[TM:TM + QB, :]
        uext_ref[0:hist, :] = uext_ref[TM:TM + hist, :]

    _project_rest(xb, wrest_ref, z_ref)

    c = rope_ref[0]
    s1 = rope_ref[1]
    s2 = rope_ref[2]
    for j in range(Q_WIDTH // LANES):
        sl = slice(j * LANES, (j + 1) * LANES)
        qb_ref[:, sl] = _rope(zq_ref[:, sl], c, s1, s2).astype(BF16)
    for j in range(KV_WIDTH // LANES):
        sl = slice(K0 + j * LANES, K0 + (j + 1) * LANES)
        kr = _rope(z_ref[:, sl], c, s1, s2)
        z_ref[:, sl] = kr
        kext_ref[QB:QB + TM, j * LANES:(j + 1) * LANES] = kr.astype(BF16)
    vext_ref[QB:QB + TM, :] = z_ref[:, V0:V0 + KV_WIDTH].astype(BF16)
    uext_ref[hist:hist + TM, :] = z_ref[:, U0:U0 + POOL_WIDTH]
    nk_ref[...] = z_ref[TM - QB:TM, K0:K0 + KV_WIDTH]
    nv_ref[...] = z_ref[TM - QB:TM, V0:V0 + KV_WIDTH]

    rowq = lax.broadcasted_iota(jnp.int32, (QB, 2 * QB), 0)
    colk = lax.broadcasted_iota(jnp.int32, (QB, 2 * QB), 1)
    band = (colk >= rowq) & (colk <= rowq + WINDOW)
    bias_ref[1] = jnp.where(band, 0.0, NEG_INF)
    bias_ref[0] = jnp.where(band & ((colk >= QB) | (i > 0)), 0.0, NEG_INF)
    for sb in range(TM // QB):
        k2 = kext_ref[sb * QB:(sb + 2) * QB, :]
        v2 = vext_ref[sb * QB:(sb + 2) * QB, :]
        qs = jnp.concatenate(
            [qb_ref[sb * QB:(sb + 1) * QB, g * KV_WIDTH:(g + 1) * KV_WIDTH] for g in range(GROUP)], axis=0)
        for kv in range(N_KV):
            kmask = _lane_block_mask((2 * QB, KV_WIDTH), kv)
            s_ref[...] = _dot_nt(qs, jnp.where(kmask, k2, jnp.zeros_like(k2)))
            vblk_ref[kv * 2 * QB:(kv + 1) * 2 * QB, :] = jnp.where(kmask, v2, jnp.zeros_like(v2))
            for c0 in range(0, GROUP * QB, SOFTMAX_ROWS):
                rq = c0 % QB
                sink = sinks_ref[kv * GROUP + c0 // QB]
                s = s_ref[c0:c0 + SOFTMAX_ROWS, :] + bias_ref[min(sb, 1), rq:rq + SOFTMAX_ROWS, :]
                m = jnp.maximum(jnp.max(s, axis=1, keepdims=True), sink)
                p = jnp.exp(s - m)
                den = jnp.sum(p, axis=1, keepdims=True) + jnp.exp(sink - m)
                p_ref[c0:c0 + SOFTMAX_ROWS, kv * 2 * QB:(kv + 1) * 2 * QB] = (p * (1.0 / den)).astype(BF16)
        o = _dot(p_ref[...], vblk_ref[...])
        for g in range(GROUP):
            oa_ref[sb * QB:(sb + 1) * QB, g * KV_WIDTH:(g + 1) * KV_WIDTH] = o[g * QB:(g + 1) * QB].astype(BF16)

    npool_ref[...] = uext_ref[TM:TM + hist, :]
    pos = i * TM + lax.broadcasted_iota(jnp.int32, (TM, 1), 0)
    for g, w in enumerate(POOL_WINDOWS):
        sl = slice(g * POOL_GROUP_DIM, (g + 1) * POOL_GROUP_DIM)
        cur = uext_ref[hist:hist + TM, sl]
        ws = cur
        for j in range(1, w):
            ws = ws + uext_ref[hist - j:hist - j + TM, sl]
        cnt = jnp.minimum(pos + 1, w).astype(F32)
        pooled = ws / cnt - cur
        ob_ref[:, sl] = _dot(pooled.astype(BF16), wmix_ref[g]) * pscale_ref[:, sl]

    cq = (z_ref[:, CQ0:CQ0 + MEM_WIDTH] * ATT_SCALE).astype(BF16)
    mk = mk_ref[...].astype(BF16)
    mv = mv_ref[...].astype(BF16)
    for hh in range(MEM_HEADS):
        hmask = _lane_block_mask((N_MEM, MEM_WIDTH), hh)
        s_ref[0:TM, :] = _dot_nt(cq, jnp.where(hmask, mk, jnp.zeros_like(mk)))
        vblk_ref[hh * N_MEM:(hh + 1) * N_MEM, :] = jnp.where(hmask, mv, jnp.zeros_like(mv))
        for c0 in range(0, TM, SOFTMAX_ROWS):
            s = s_ref[c0:c0 + SOFTMAX_ROWS, :]
            p = jnp.exp(s - jnp.max(s, axis=1, keepdims=True))
            den = jnp.sum(p, axis=1, keepdims=True)
            p_ref[c0:c0 + SOFTMAX_ROWS, hh * N_MEM:(hh + 1) * N_MEM] = (p * (1.0 / den)).astype(BF16)
    oc = _dot(p_ref[0:TM, :], vblk_ref[...])

    h = _merge_ln1(x, oa_ref[...], ob_ref[...], oc, z_ref[:, GZ0:GZ0 + 3 * D_MODEL],
                   wa_ref, wb_ref, wc_ref, wo_ref, g1_ref[...], b1_ref[...])
    h_ref[...] = h
    hb = h.astype(BF16)
    logits = _dot(hb, wr_ref[...]) + br_ref[...]
    _project_q(xn_ref[...].astype(BF16), wq_ref, zq_ref)
    xs_ref[...], route_ref[...], counts_ref[...] = _route_and_sort(logits, hb, xs_ref.shape[0])


def _front(x, rope, sinks, wq, wrest, wa, wb, wc, wo, wmix, pscale, mk, mv, g1, b1, wr, br):
    b, l, _ = x.shape
    nt = l // TM
    cap = _chunk_rows(b * nt)
    hist = 2 * SUBLANES
    tile = lambda w: pl.BlockSpec((None, TM, w), lambda bi, ti: (bi, ti, 0))
    per_b = lambda r, w: pl.BlockSpec((None, r, w), lambda bi, ti: (bi, 0, 0))
    nxt = lambda bi, ti: jnp.minimum(bi * nt + ti + 1, b * nt - 1)
    return pl.pallas_call(
        _front_kernel,
        grid=(b, nt),
        in_specs=[
            pl.BlockSpec(memory_space=pltpu.SMEM),
            tile(D_MODEL),
            pl.BlockSpec((None, TM, D_MODEL), lambda bi, ti: (nxt(bi, ti) // nt, nxt(bi, ti) % nt, 0)),
            pl.BlockSpec((3, TM, LANES), lambda bi, ti: (0, ti, 0)),
            _const_spec((D_MODEL, Q_WIDTH)), _const_spec((D_MODEL, IN_WIDTH - K0)),
            _const_spec((Q_WIDTH, D_MODEL)), _const_spec((POOL_WIDTH, D_MODEL)),
            _const_spec((MEM_WIDTH, D_MODEL)), _const_spec((D_MODEL, D_MODEL)),
            _const_spec((len(POOL_WINDOWS), POOL_GROUP_DIM, POOL_GROUP_DIM)),
            _const_spec((1, POOL_WIDTH)),
            per_b(N_MEM, MEM_WIDTH), per_b(N_MEM, MEM_WIDTH),
            _const_spec((1, D_MODEL)), _const_spec((1, D_MODEL)),
            _const_spec((D_MODEL, LANES)), _const_spec((1, LANES)),
        ],
        out_specs=[
            tile(D_MODEL),
            pl.BlockSpec((cap, D_MODEL), lambda bi, ti: (bi * nt + ti, 0)),
            tile(LANES),
            pl.BlockSpec((None, None, LANES, LANES), lambda bi, ti: (bi, ti, 0, 0)),
            per_b(QB, KV_WIDTH), per_b(QB, KV_WIDTH), per_b(hist, POOL_WIDTH),
        ],
        out_shape=[
            jax.ShapeDtypeStruct((b, l, D_MODEL), F32),
            jax.ShapeDtypeStruct((b * nt * cap, D_MODEL), BF16),
            jax.ShapeDtypeStruct((b, l, LANES), F32),
            jax.ShapeDtypeStruct((b, nt, LANES, LANES), F32),
            jax.ShapeDtypeStruct((b, QB, KV_WIDTH), F32),
            jax.ShapeDtypeStruct((b, QB, KV_WIDTH), F32),
            jax.ShapeDtypeStruct((b, hist, POOL_WIDTH), F32),
        ],
        scratch_shapes=[
            pltpu.VMEM((TM, IN_WIDTH), F32),
            pltpu.VMEM((TM, Q_WIDTH), BF16),
            pltpu.VMEM((QB + TM, KV_WIDTH), BF16),
            pltpu.VMEM((QB + TM, KV_WIDTH), BF16),
            pltpu.VMEM((hist + TM, POOL_WIDTH), F32),
            pltpu.VMEM((TM, Q_WIDTH), BF16),
            pltpu.VMEM((TM, POOL_WIDTH), F32),
            pltpu.VMEM((2, QB, 2 * QB), F32),
            pltpu.VMEM((GROUP * QB, 2 * QB), F32),
            pltpu.VMEM((GROUP * QB, N_KV * 2 * QB), BF16),
            pltpu.VMEM((N_KV * 2 * QB, KV_WIDTH), BF16),
            pltpu.VMEM((TM, Q_WIDTH), F32),
        ],
        compiler_params=pltpu.CompilerParams(
            dimension_semantics=("arbitrary", "arbitrary"), vmem_limit_bytes=VMEM_LIMIT),
        name="front_prompt",
    )(sinks, x, x, rope, wq, wrest, wa, wb, wc, wo, wmix, pscale, mk, mv, g1, b1, wr, br)


def _expert_mlp(xb, wg, wu, wd):
    a = _dot(xb, wg)
    hid = (a * jax.nn.sigmoid(a)) * _dot(xb, wu)
    return _dot(hid.astype(BF16), wd)


def _gemm_kernel(src_ref, dst_ref, te_ref, act_ref, xs_ref, wg_ref, wu_ref, wd_ref, ys_ref,
                 xbuf, obuf, wgb, wub, wdb, prime, sem_in, sem_out):
    i = pl.program_id(0)
    n = pl.num_programs(0)
    slot = i % 2
    in_slot = i % IN_SLOTS

    def start_in(tile, slot):
        for j in range(PIECES_PER_TILE):
            row0 = pl.multiple_of(src_ref[tile * PIECES_PER_TILE + j], PIECE)
            pltpu.make_async_copy(xs_ref.at[pl.ds(row0, PIECE)], xbuf.at[slot, pl.ds(j * PIECE, PIECE)],
                                  sem_in.at[slot]).start()

    def start_out(tile, slot):
        for j in range(PIECES_PER_TILE):
            row0 = pl.multiple_of(dst_ref[tile * PIECES_PER_TILE + j], PIECE)
            pltpu.make_async_copy(obuf.at[slot, pl.ds(j * PIECE, PIECE)], ys_ref.at[pl.ds(row0, PIECE)],
                                  sem_out.at[slot]).start()

    def wait_in(slot):
        for j in range(PIECES_PER_TILE):
            pltpu.make_async_copy(xs_ref.at[pl.ds(0, PIECE)], xbuf.at[slot, pl.ds(j * PIECE, PIECE)],
                                  sem_in.at[slot]).wait()

    def wait_out(slot):
        for j in range(PIECES_PER_TILE):
            pltpu.make_async_copy(obuf.at[slot, pl.ds(j * PIECE, PIECE)], ys_ref.at[pl.ds(0, PIECE)],
                                  sem_out.at[slot]).wait()

    active = act_ref[i] > 0
    ahead = IN_SLOTS - 1
    prefetched = (i < ahead) | (act_ref[jnp.maximum(i - ahead, 0)] > 0)
    out_pending = (i < 2) | (act_ref[jnp.maximum(i - 2, 0)] > 0)

    @pl.when(i == 0)
    def _():
        prime[0] = jnp.zeros((PIECE, D_MODEL), BF16)
        for s in range(2):
            for j in range(PIECES_PER_TILE):
                pltpu.make_async_copy(prime.at[0], prime.at[1 + s * PIECES_PER_TILE + j], sem_out.at[s]).start()
        for t in range(ahead):
            start_in(t, t)

    prev_active = (i >= 1) & (act_ref[jnp.maximum(i - 1, 0)] > 0)

    def active_step(after_first):
        @pl.when((i == 0) | (te_ref[i] != te_ref[jnp.maximum(i - 1, 0)]))
        def _():
            wgb[...] = wg_ref[...].astype(BF16)
            wub[...] = wu_ref[...].astype(BF16)
            wdb[...] = wd_ref[...].astype(BF16)

        wait_in(in_slot)
        wait_out(slot)
        out = _expert_mlp(xbuf[in_slot], wgb[...], wub[...], wdb[...])
        start_in(i + ahead, (i + ahead) % IN_SLOTS)
        if after_first:
            start_out(i - 1, 1 - slot)
        obuf[slot] = out.astype(BF16)

    @pl.when(active & (i == 0))
    def _():
        active_step(False)

    @pl.when(active & (i > 0))
    def _():
        active_step(True)

    @pl.when(jnp.logical_not(active))
    def _():
        @pl.when(prev_active)
        def _():
            start_out(i - 1, 1 - slot)

        @pl.when(prefetched)
        def _():
            wait_in(in_slot)

        @pl.when(out_pending)
        def _():
            wait_out(slot)

    @pl.when(i == n - 1)
    def _():
        for t in range(ahead):
            @pl.when(act_ref[jnp.maximum(i - t, 0)] > 0)
            def _(t=t):
                wait_in((i - t + ahead) % IN_SLOTS)

        @pl.when(active)
        def _():
            start_out(i, slot)
            wait_out(slot)

        @pl.when(prev_active)
        def _():
            wait_out(1 - slot)


def _grouped_gemm(piece_src, piece_dst, tile_expert, tile_active, xs, wg, wu, wd):
    n_tiles = tile_expert.shape[0]
    assert n_tiles >= 2
    wspec = lambda r, c: pl.BlockSpec((None, r, c), lambda i, src, dst, te, act: (te[i], 0, 0))
    return pl.pallas_call(
        _gemm_kernel,
        grid_spec=pltpu.PrefetchScalarGridSpec(
            num_scalar_prefetch=4,
            grid=(n_tiles,),
            in_specs=[pl.BlockSpec(memory_space=pl.ANY),
                      wspec(D_MODEL, D_EXPERT), wspec(D_MODEL, D_EXPERT), wspec(D_EXPERT, D_MODEL)],
            out_specs=pl.BlockSpec(memory_space=pl.ANY),
            scratch_shapes=[pltpu.VMEM((IN_SLOTS, TG, D_MODEL), BF16), pltpu.VMEM((2, TG, D_MODEL), BF16),
                            pltpu.VMEM((D_MODEL, D_EXPERT), BF16), pltpu.VMEM((D_MODEL, D_EXPERT), BF16),
                            pltpu.VMEM((D_EXPERT, D_MODEL), BF16),
                            pltpu.VMEM((1 + 2 * PIECES_PER_TILE, PIECE, D_MODEL), BF16),
                            pltpu.SemaphoreType.DMA((IN_SLOTS,)), pltpu.SemaphoreType.DMA((2,))],
        ),
        out_shape=jax.ShapeDtypeStruct(xs.shape, xs.dtype),
        input_output_aliases={4: 0},
        compiler_params=pltpu.CompilerParams(dimension_semantics=("arbitrary",), vmem_limit_bytes=VMEM_LIMIT),
        name="moe_grouped_gemm",
    )(piece_src, piece_dst, tile_expert, tile_active, xs, wg, wu, wd)


def _combine_kernel(ys_ref, h_ref, route_ref, g2_ref, b2_ref, y_ref):
    chunks = h_ref.shape[0] // TM
    cap = ys_ref.shape[0] // chunks
    slot = lax.broadcasted_iota(jnp.int32, (TM, cap), 1).astype(F32)
    for c in range(chunks):
        rows = slice(c * TM, (c + 1) * TM)
        route = route_ref[rows, :]
        sel = jnp.concatenate([jnp.where(slot == route[:, 2:3], 1.0, 0.0).astype(BF16),
                               jnp.where(slot == route[:, 3:4], 1.0, 0.0).astype(BF16)], axis=0)
        picked = _dot(sel, ys_ref[c * cap:(c + 1) * cap, :])
        f = route[:, 0:1] * picked[0:TM] + route[:, 1:2] * picked[TM:2 * TM]
        y_ref[rows, :] = _layer_norm(ALPHA * h_ref[rows, :] + f, g2_ref[...], b2_ref[...])


def _combine(ys, h, route, g2, b2):
    t = h.shape[0]
    cap = ys.shape[0] // (t // TM)
    per_step = next(k for k in (4, 2, 1) if (t // TM) % k == 0)
    rows = per_step * TM
    return pl.pallas_call(
        _combine_kernel,
        grid=(t // rows,),
        in_specs=[
            pl.BlockSpec((per_step * cap, D_MODEL), lambda i: (i, 0)),
            pl.BlockSpec((rows, D_MODEL), lambda i: (i, 0)),
            pl.BlockSpec((rows, LANES), lambda i: (i, 0)),
            pl.BlockSpec((1, D_MODEL), lambda i: (0, 0)),
            pl.BlockSpec((1, D_MODEL), lambda i: (0, 0)),
        ],
        out_specs=pl.BlockSpec((rows, D_MODEL), lambda i: (i, 0)),
        out_shape=jax.ShapeDtypeStruct((t, D_MODEL), F32),
        compiler_params=pltpu.CompilerParams(dimension_semantics=("arbitrary",)),
        name="moe_combine",
    )(ys, h, route, g2, b2)


def _select(table, idx):
    hot = idx[:, None] == jnp.arange(table.shape[0], dtype=jnp.int32)[None, :]
    return jnp.sum(jnp.where(hot[:, :, None], table[None, :, :], 0), axis=1)


def _piece_tables(counts):
    n_chunks = counts.shape[0]
    n_tiles = -(-(n_chunks * MAX_CHUNK_PIECES + N_EXPERTS * (PIECES_PER_TILE - 1)) // PIECES_PER_TILE)
    npc = (counts + (PIECE - 1)) // PIECE
    first = (jnp.cumsum(npc, axis=1) - npc).T
    npc_t = npc.T
    cum = jnp.cumsum(npc_t, axis=1)
    per_expert = cum[:, -1]
    tiles_e = (per_expert + (PIECES_PER_TILE - 1)) // PIECES_PER_TILE
    tile_end = jnp.cumsum(tiles_e)
    tile_idx = jnp.arange(n_tiles, dtype=jnp.int32)
    expert_of = lambda i: jnp.minimum(jnp.sum(i[:, None] >= tile_end[None, :], axis=1), N_EXPERTS - 1).astype(jnp.int32)
    active = tile_idx < tile_end[-1]
    tile_expert = jnp.where(active, expert_of(tile_idx), expert_of(tile_end[-1:] - 1))
    meta = jnp.stack([tile_end - tiles_e, per_expert], axis=1)
    meta_t = _select(meta, tile_expert)
    k = (tile_idx - meta_t[:, 0])[:, None] * PIECES_PER_TILE + jnp.arange(PIECES_PER_TILE, dtype=jnp.int32)[None, :]
    valid = active[:, None] & (k < meta_t[:, 1:2])
    cum_t = _select(cum, tile_expert)
    chunk = jnp.minimum(jnp.sum(k[:, :, None] >= cum_t[:, None, :], axis=2), n_chunks - 1).astype(jnp.int32)
    at_chunk = chunk[:, :, None] == jnp.arange(n_chunks, dtype=jnp.int32)[None, None, :]
    pick = lambda tab: jnp.sum(jnp.where(at_chunk, _select(tab, tile_expert)[:, None, :], 0), axis=2)
    piece = pick(first) + k - pick(cum - npc_t)
    cap = _chunk_rows(n_chunks)
    rows = chunk * cap + piece * PIECE
    d = ((tile_idx % 2)[:, None] * (PIECES_PER_TILE - 1)
         + jnp.maximum(jnp.arange(PIECES_PER_TILE, dtype=jnp.int32)[None, :] - 1, 0))
    spare_row = lambda d: (d % n_chunks) * cap + (MAX_CHUNK_PIECES + d // n_chunks) * PIECE
    spare = spare_row(d)
    zero_piece = spare_row(N_SPARE - 1)
    extra = jnp.full(((IN_SLOTS - 1) * PIECES_PER_TILE,), zero_piece, jnp.int32)
    src = jnp.concatenate([jnp.where(valid, rows, zero_piece).astype(jnp.int32).reshape(-1), extra])
    dst = jnp.concatenate([jnp.where(valid, rows, spare).astype(jnp.int32).reshape(-1), extra])
    return src, dst, tile_expert.astype(jnp.int32), active.astype(jnp.int32)


def _sample_attn_kernel(x_ref, rope_ref, wq_ref, wrest_ref, sink_ref, ck_ref, cv_ref, cmk_ref, cmv_ref,
                        z_ref, oa_ref, oc_ref, nk_ref, nv_ref, knew_t, vnew_t):
    j = pl.program_id(0)
    db = x_ref.shape[0]

    @pl.when(j == 0)
    def _():
        xb = x_ref[...].astype(BF16)
        _project_q(xb, wq_ref, z_ref)
        _project_rest(xb, wrest_ref, z_ref)
        c = rope_ref[0]
        s1 = rope_ref[1]
        s2 = rope_ref[2]
        for jj in range((Q_WIDTH + KV_WIDTH) // LANES):
            sl = slice(jj * LANES, (jj + 1) * LANES)
            z_ref[:, sl] = _rope(z_ref[:, sl], c, s1, s2)
        knew_t[...] = z_ref[:, K0:K0 + KV_WIDTH].T
        vnew_t[...] = z_ref[:, V0:V0 + KV_WIDTH].T

    r0 = pl.multiple_of(j * SB, SB)
    zq = z_ref[pl.ds(r0, SB), Q0:Q0 + Q_WIDTH]
    zk = z_ref[pl.ds(r0, SB), K0:K0 + KV_WIDTH]
    zv = z_ref[pl.ds(r0, SB), V0:V0 + KV_WIDTH]
    zc = z_ref[pl.ds(r0, SB), CQ0:CQ0 + MEM_WIDTH] * ATT_SCALE
    sink = sink_ref[:, 0:1]
    row_kv = lax.broadcasted_iota(jnp.int32, (N_HEADS, KV_WIDTH), 0) & (N_KV - 1)
    lane_kv = lax.broadcasted_iota(jnp.int32, (N_HEADS, KV_WIDTH), 1) // HEAD_DIM
    own = row_kv == lane_kv
    row_c = lax.broadcasted_iota(jnp.int32, (N_HEADS, MEM_WIDTH), 0)
    lane_c = lax.broadcasted_iota(jnp.int32, (N_HEADS, MEM_WIDTH), 1) // HEAD_DIM
    own_c = row_c == lane_c
    last_pos = lax.broadcasted_iota(jnp.int32, (KV_WIDTH, WINDOW), 1) == WINDOW - 1
    seq_lane = lax.broadcasted_iota(jnp.int32, (KV_WIDTH, db), 1)

    seqs = range(SB)
    qblk, cblk, s, sc = [], [], [], []
    for b in seqs:
        q4 = jnp.concatenate(
            [jnp.broadcast_to(zq[b:b + 1, g * KV_WIDTH:(g + 1) * KV_WIDTH], (N_KV, KV_WIDTH)) for g in range(GROUP)],
            axis=0)
        qblk.append(jnp.where(own, q4, 0.0).astype(BF16))
        cblk.append(jnp.where(own_c, jnp.broadcast_to(zc[b:b + 1, :], (N_HEADS, MEM_WIDTH)), 0.0).astype(BF16))
        s.append(_dot(qblk[b], ck_ref[b].astype(BF16)))
        sc.append(_dot(cblk[b], cmk_ref[b].astype(BF16)))

    pn, p_new, pc = [], [], []
    for b in seqs:
        s_new = jnp.sum(qblk[b].astype(F32) * zk[b:b + 1, :].astype(BF16).astype(F32), axis=1, keepdims=True)
        m = jnp.maximum(jnp.maximum(jnp.max(s[b], axis=1, keepdims=True), s_new), sink)
        p = jnp.exp(s[b] - m)
        e_new = jnp.exp(s_new - m)
        inv = 1.0 / (jnp.sum(p, axis=1, keepdims=True) + e_new + jnp.exp(sink - m))
        pn.append((p * inv).astype(BF16))
        p_new.append((e_new * inv).astype(BF16).astype(F32))
        e = jnp.exp(sc[b] - jnp.max(sc[b], axis=1, keepdims=True))
        pc.append((e * (1.0 / jnp.sum(e, axis=1, keepdims=True))).astype(BF16))

    oa_rows = [[] for _ in range(GROUP)]
    oc_rows = []
    for b in seqs:
        vc = cv_ref[b]
        o = _dot_nt(pn[b], vc.astype(BF16)) + p_new[b] * zv[b:b + 1, :].astype(BF16).astype(F32)
        o = jnp.where(own, o, 0.0)
        for g in range(GROUP):
            oa_rows[g].append(jnp.sum(o[g * N_KV:(g + 1) * N_KV], axis=0, keepdims=True))
        ocb = jnp.where(own_c, _dot_nt(pc[b], cmv_ref[b].astype(BF16)), 0.0)
        oc_rows.append(jnp.sum(ocb, axis=0, keepdims=True))
        here = seq_lane == j * SB + b
        knew_col = jnp.sum(jnp.where(here, knew_t[...], 0.0), axis=1, keepdims=True)
        vnew_col = jnp.sum(jnp.where(here, vnew_t[...], 0.0), axis=1, keepdims=True)
        nk_ref[b] = jnp.where(last_pos, knew_col, pltpu.roll(ck_ref[b], WINDOW - 1, 1))
        nv_ref[b] = jnp.where(last_pos, vnew_col, pltpu.roll(vc, WINDOW - 1, 1))
    for g in range(GROUP):
        oa_ref[pl.ds(r0, SB), g * KV_WIDTH:(g + 1) * KV_WIDTH] = jnp.concatenate(oa_rows[g], axis=0)
    oc_ref[pl.ds(r0, SB), :] = jnp.concatenate(oc_rows, axis=0)


def _sample_attn(x, rope, wq, wrest, sink_gk, ck, cv, cmk, cmv):
    db = x.shape[0]
    blk = lambda r: pl.BlockSpec((SB, KV_WIDTH, r), lambda j: (j, 0, 0))
    full = lambda w: pl.BlockSpec((db, w), lambda j: (0, 0))
    return pl.pallas_call(
        _sample_attn_kernel,
        grid=(db // SB,),
        in_specs=[
            full(D_MODEL), _const_spec((3, 1, LANES)),
            _const_spec((D_MODEL, Q_WIDTH)), _const_spec((D_MODEL, IN_WIDTH - K0)),
            _const_spec((N_HEADS, LANES)),
            blk(WINDOW), blk(WINDOW), blk(N_MEM), blk(N_MEM),
        ],
        out_specs=[full(IN_WIDTH), full(Q_WIDTH), full(MEM_WIDTH), blk(WINDOW), blk(WINDOW)],
        out_shape=[
            jax.ShapeDtypeStruct((db, IN_WIDTH), F32),
            jax.ShapeDtypeStruct((db, Q_WIDTH), F32),
            jax.ShapeDtypeStruct((db, MEM_WIDTH), F32),
            jax.ShapeDtypeStruct((db, KV_WIDTH, WINDOW), F32),
            jax.ShapeDtypeStruct((db, KV_WIDTH, WINDOW), F32),
        ],
        scratch_shapes=[pltpu.VMEM((KV_WIDTH, db), F32), pltpu.VMEM((KV_WIDTH, db), F32)],
        compiler_params=pltpu.CompilerParams(dimension_semantics=("arbitrary",), vmem_limit_bytes=VMEM_LIMIT),
        name="sample_attn",
    )(x, rope, wq, wrest, sink_gk, ck, cv, cmk, cmv)


def _sample_tail_kernel(x_ref, z_ref, oa_ref, oc_ref, st_ref, wa_ref, wb_ref, wc_ref, wo_ref, wmix_ref,
                        pscale_ref, g1_ref, b1_ref, wr_ref, br_ref, wg_ref, wu_ref, wd_ref, g2_ref, b2_ref,
                        y_ref, npool_ref, h_sc, comb_sc, acc_sc):
    e = pl.program_id(0)

    @pl.when(e == 0)
    def _():
        u = z_ref[:, U0:U0 + POOL_WIDTH]
        npool_ref[0:POOL_STATE - 1] = st_ref[1:POOL_STATE]
        npool_ref[POOL_STATE - 1] = u
        obs = []
        for g, w in enumerate(POOL_WINDOWS):
            sl = slice(g * POOL_GROUP_DIM, (g + 1) * POOL_GROUP_DIM)
            cur = u[:, sl]
            ws = cur
            for jj in range(1, w):
                ws = ws + st_ref[POOL_STATE - jj, :, sl]
            cnt = float(min(PAST_LEN + 1, w))
            pooled = ws / cnt - cur
            obs.append(_dot(pooled.astype(BF16), wmix_ref[g]) * pscale_ref[:, sl])
        ob = jnp.concatenate(obs, axis=1)
        h = _merge_ln1(x_ref[...], oa_ref[...].astype(BF16), ob, oc_ref[...], z_ref[:, GZ0:GZ0 + 3 * D_MODEL],
                       wa_ref, wb_ref, wc_ref, wo_ref, g1_ref[...], b1_ref[...])
        h_sc[...] = h
        logits = _dot(h.astype(BF16), wr_ref[...]) + br_ref[...]
        hot1, hot2, w1, w2 = _route(logits)
        comb_sc[...] = jnp.where(hot1, w1, 0.0) + jnp.where(hot2, w2, 0.0)
        acc_sc[...] = jnp.zeros_like(acc_sc)

    out = _expert_mlp(h_sc[...].astype(BF16), wg_ref[...].astype(BF16), wu_ref[...].astype(BF16),
                      wd_ref[...].astype(BF16))
    lane = lax.broadcasted_iota(jnp.int32, comb_sc.shape, 1)
    ce = jnp.sum(jnp.where(lane == e, comb_sc[...], 0.0), axis=1, keepdims=True)
    acc_sc[...] += ce * out

    @pl.when(e == pl.num_programs(0) - 1)
    def _():
        y_ref[...] = _layer_norm(ALPHA * h_sc[...] + acc_sc[...], g2_ref[...], b2_ref[...])


def _sample_tail(x, z, oa, oc, state, wa, wb, wc, wo, wmix, pscale, g1, b1, wr, br, wg, wu, wd, g2, b2):
    db = x.shape[0]
    full = lambda w: pl.BlockSpec((db, w), lambda e: (0, 0))
    vec = lambda w: pl.BlockSpec((1, w), lambda e: (0, 0))
    hist = pl.BlockSpec((POOL_STATE, db, POOL_WIDTH), lambda e: (0, 0, 0))
    return pl.pallas_call(
        _sample_tail_kernel,
        grid=(N_EXPERTS,),
        in_specs=[
            full(D_MODEL), full(IN_WIDTH), full(Q_WIDTH), full(MEM_WIDTH), hist,
            _const_spec((Q_WIDTH, D_MODEL)), _const_spec((POOL_WIDTH, D_MODEL)),
            _const_spec((MEM_WIDTH, D_MODEL)), _const_spec((D_MODEL, D_MODEL)),
            _const_spec((len(POOL_WINDOWS), POOL_GROUP_DIM, POOL_GROUP_DIM)),
            vec(POOL_WIDTH), vec(D_MODEL), vec(D_MODEL),
            _const_spec((D_MODEL, LANES)), vec(LANES),
            pl.BlockSpec((None, D_MODEL, D_EXPERT), lambda e: (e, 0, 0)),
            pl.BlockSpec((None, D_MODEL, D_EXPERT), lambda e: (e, 0, 0)),
            pl.BlockSpec((None, D_EXPERT, D_MODEL), lambda e: (e, 0, 0)),
            vec(D_MODEL), vec(D_MODEL),
        ],
        out_specs=[full(D_MODEL), hist],
        out_shape=[jax.ShapeDtypeStruct((db, D_MODEL), F32),
                   jax.ShapeDtypeStruct((POOL_STATE, db, POOL_WIDTH), F32)],
        scratch_shapes=[pltpu.VMEM((db, D_MODEL), F32), pltpu.VMEM((db, LANES), F32),
                        pltpu.VMEM((db, D_MODEL), F32)],
        compiler_params=pltpu.CompilerParams(dimension_semantics=("arbitrary",), vmem_limit_bytes=VMEM_LIMIT),
        name="sample_tail",
    )(x, z, oa, oc, state, wa, wb, wc, wo, wmix, pscale, g1, b1, wr, br, wg, wu, wd, g2, b2)


def _rope_tables(pos):
    half = ROPE_DIM // 2
    inv = jnp.power(ROPE_THETA, -jnp.arange(half, dtype=F32) * (2.0 / ROPE_DIM))
    ang = pos.astype(F32)[:, None] * inv[None, :]
    lane = np.arange(LANES)
    off = lane % HEAD_DIM
    cos = jnp.cos(ang)[:, lane % half]
    sin = jnp.sin(ang)[:, lane % half]
    c = jnp.where(off[None, :] < ROPE_DIM, cos, 1.0)
    s1 = jnp.where((off[None, :] >= half) & (off[None, :] < ROPE_DIM), sin, 0.0)
    s2 = jnp.where(off[None, :] < half, -sin, 0.0)
    return jnp.stack([c, s1, s2]).astype(F32)


def _q_heads_group_major(w, axis):
    if axis == 1:
        n = w.shape[0]
        return w.reshape(n, N_KV, GROUP, HEAD_DIM).transpose(0, 2, 1, 3).reshape(n, Q_WIDTH)
    n = w.shape[1]
    return w.reshape(N_KV, GROUP, HEAD_DIM, n).transpose(1, 0, 2, 3).reshape(Q_WIDTH, n)


def kernel(x_prompt, x_sample, cache_win_k, cache_win_v, state_pool, cache_mem_k, cache_mem_v, mem_prompt, w_in, sinks, w_pool_mix, pool_scale, w_mem_k, w_mem_v, w_branch_a, w_branch_b, w_branch_c, w_out, ln1_g, ln1_b, w_group, b_group, w_router, b_router, w_gate, w_up, w_down, ln2_g, ln2_b):
    assert w_in.shape[0] == DEPTH == 1
    b, l, _ = x_prompt.shape
    db, ds, _ = x_sample.shape
    assert ds == 1 and l % TM == 0 and db % SB == 0
    assert cache_win_k.shape[2] == WINDOW
    t = b * l

    win = w_in[0]
    wq = (_q_heads_group_major(win[:, Q0:Q0 + Q_WIDTH], 1) * ATT_SCALE).astype(BF16)
    wrest = win[:, K0:].astype(BF16)
    wa = _q_heads_group_major(w_branch_a[0], 0).astype(BF16)
    wb = w_branch_b[0].astype(BF16)
    wc = w_branch_c[0].astype(BF16)
    wo = w_out[0].astype(BF16)
    wmix = w_pool_mix[0].astype(BF16)
    pscale = pool_scale[0].reshape(1, POOL_WIDTH)
    g1 = ln1_g[0].reshape(1, D_MODEL)
    b1 = ln1_b[0].reshape(1, D_MODEL)
    g2 = ln2_g[0].reshape(1, D_MODEL)
    b2 = ln2_b[0].reshape(1, D_MODEL)
    wr = jnp.concatenate([w_group[0], w_router[0].reshape(D_MODEL, N_EXPERTS)], axis=1)
    wr = jnp.pad(wr, ((0, 0), (0, LANES - wr.shape[1]))).astype(BF16)
    br = jnp.pad(jnp.concatenate([b_group[0], b_router[0].reshape(N_EXPERTS)]), (0, LANES - N_EXPERT_GROUPS - N_EXPERTS))
    br = br.reshape(1, LANES).astype(F32)
    wg = w_gate[0]
    wu = w_up[0]
    wd = w_down[0]
    sink = sinks[0].astype(F32)
    sink_gk = jnp.broadcast_to(sink.reshape(N_KV, GROUP).T.reshape(N_HEADS, 1), (N_HEADS, LANES))

    mk, mv = _mem_project(mem_prompt, w_mem_k[0].astype(BF16), w_mem_v[0].astype(BF16))
    rope_p = _rope_tables(jnp.arange(l, dtype=jnp.int32))
    h, xs, route, counts, nk_p, nv_p, npool_p = _front(
        x_prompt, rope_p, sink, wq, wrest, wa, wb, wc, wo, wmix, pscale, mk, mv, g1, b1, wr, br)
    piece_src, piece_dst, tile_expert, tile_active = _piece_tables(
        counts.reshape(-1, LANES, LANES)[:, :N_EXPERTS, 0].astype(jnp.int32))
    ys = _grouped_gemm(piece_src, piece_dst, tile_expert, tile_active, xs, wg, wu, wd)
    y_p = _combine(ys, h.reshape(t, D_MODEL), route.reshape(t, LANES), g2, b2).reshape(b, l, D_MODEL)

    rope_s = _rope_tables(jnp.full((1,), PAST_LEN, jnp.int32))
    xs = x_sample.reshape(db, D_MODEL)
    feat_major = lambda c: jnp.transpose(c[0], (0, 2, 3, 1)).reshape(db, KV_WIDTH, c.shape[2])
    from_feat_major = lambda a: jnp.transpose(a.reshape(db, N_KV, HEAD_DIM, a.shape[2]), (0, 3, 1, 2))[None]
    z_s, oa_s, oc_s, nk_s, nv_s = _sample_attn(xs, rope_s, wq, wrest, sink_gk, feat_major(cache_win_k),
                                               feat_major(cache_win_v), feat_major(cache_mem_k), feat_major(cache_mem_v))
    state = jnp.transpose(state_pool[0], (1, 0, 2))
    y_s, npool_s = _sample_tail(xs, z_s, oa_s, oc_s, state, wa, wb, wc, wo, wmix, pscale, g1, b1, wr, br,
                                wg, wu, wd, g2, b2)

    kv5 = lambda a, n, w: a.reshape(1, n, w, N_KV, HEAD_DIM)
    return (y_p, y_s.reshape(db, 1, D_MODEL),
            kv5(nk_p, b, QB), kv5(nv_p, b, QB),
            npool_p[:, 2 * SUBLANES - POOL_STATE:, :][None],
            kv5(mk, b, N_MEM), kv5(mv, b, N_MEM),
            from_feat_major(nk_s), from_feat_major(nv_s),
            jnp.transpose(npool_s, (1, 0, 2))[None])
```

```python
import jax
import jax.numpy as jnp
import numpy as np
from jax import lax
from jax.experimental import pallas as pl
from jax.experimental.pallas import tpu as pltpu

D_MODEL = 1024
N_HEADS = 16
HEAD_DIM = 64
N_KV = 4
GROUP = N_HEADS // N_KV
WINDOW = 128
ROPE_THETA = 500000.0
ROPE_DIM = HEAD_DIM // 4
Q_WIDTH = N_HEADS * HEAD_DIM
KV_WIDTH = N_KV * HEAD_DIM
POOL_WINDOWS = (2, 4, 8, 16)
POOL_WIDTH = D_MODEL // 2
POOL_GROUP_DIM = POOL_WIDTH // len(POOL_WINDOWS)
POOL_STATE = max(POOL_WINDOWS) - 1
N_MEM = 256
MEM_HEADS = 4
MEM_WIDTH = MEM_HEADS * HEAD_DIM
N_EXPERT_GROUPS = 4
EXPERTS_PER_GROUP = 4
N_EXPERTS = N_EXPERT_GROUPS * EXPERTS_PER_GROUP
D_EXPERT = 512
PAST_LEN = 16384
DEPTH = 1
ALPHA = (2.0 * DEPTH) ** 0.25
LN_EPS = 1e-5

Q0 = 0
K0 = Q0 + Q_WIDTH
V0 = K0 + KV_WIDTH
U0 = V0 + KV_WIDTH
CQ0 = U0 + POOL_WIDTH
GZ0 = CQ0 + MEM_WIDTH
IN_WIDTH = GZ0 + 3 * D_MODEL

LANES = 128
SUBLANES = 8
VMEM_LIMIT = 56 * 1024 * 1024

TM = 256
QB = WINDOW
PIECE = 16
PIECES_PER_TILE = 32
TG = PIECE * PIECES_PER_TILE
MAX_CHUNK_PIECES = 2 * TM // PIECE + N_EXPERTS - 1
N_SPARE = 2 * (PIECES_PER_TILE - 1) + 1
IN_SLOTS = 3


def _chunk_rows(n_chunks):
    spare = -(-N_SPARE // n_chunks)
    return -(-(MAX_CHUNK_PIECES + spare) * PIECE // LANES) * LANES
SB = 8
IN_CHUNK = 768
Q_CHUNK = 512
ATT_SCALE = HEAD_DIM ** -0.5
SOFTMAX_ROWS = 64
assert N_MEM == 2 * QB and MEM_HEADS == N_KV and MEM_WIDTH == KV_WIDTH and TM <= GROUP * QB

BF16 = jnp.bfloat16
F32 = jnp.float32
NEG_INF = float("-inf")


def _const_spec(shape):
    nd = len(shape)
    return pl.BlockSpec(shape, lambda *_: (0,) * nd, pipeline_mode=pl.Buffered(1))


def _layer_norm(x, g, b):
    mu = jnp.mean(x, axis=-1, keepdims=True)
    xc = x - mu
    var = jnp.mean(xc * xc, axis=-1, keepdims=True)
    return xc * lax.rsqrt(var + LN_EPS) * g + b


def _dot(a, b):
    return jnp.dot(a, b, preferred_element_type=F32)


def _dot_nt(a, b):
    return lax.dot_general(a, b, (((1,), (1,)), ((), ())), preferred_element_type=F32)


def _lane_block_mask(shape, block, width=HEAD_DIM):
    lane = lax.broadcasted_iota(jnp.int32, shape, len(shape) - 1)
    return (lane >= block * width) & (lane < (block + 1) * width)


def _rope(x, c, s1, s2):
    half = ROPE_DIM // 2
    return x * c + pltpu.roll(x, half, 1) * s1 + pltpu.roll(x, LANES - half, 1) * s2


def _route(logits):
    rows = logits.shape[0]
    lane = lax.broadcasted_iota(jnp.int32, (rows, LANES), 1)
    lanef = lane.astype(F32)
    big = float(LANES)
    is_g = lane < N_EXPERT_GROUPS
    glog = jnp.where(is_g, logits, NEG_INF)
    gmax = jnp.max(glog, axis=1, keepdims=True)
    gsum = jnp.sum(jnp.where(is_g, jnp.exp(glog - gmax), 0.0), axis=1, keepdims=True)
    gp = 1.0 / gsum
    gidx = jnp.min(jnp.where(glog == gmax, lanef, big), axis=1, keepdims=True).astype(jnp.int32)
    lo = N_EXPERT_GROUPS + gidx * EXPERTS_PER_GROUP
    in_grp = (lane >= lo) & (lane < lo + EXPERTS_PER_GROUP)
    el = jnp.where(in_grp, logits, NEG_INF)
    v1 = jnp.max(el, axis=1, keepdims=True)
    i1 = jnp.min(jnp.where(el == v1, lanef, big), axis=1, keepdims=True).astype(jnp.int32)
    el2 = jnp.where(lane == i1, NEG_INF, el)
    v2 = jnp.max(el2, axis=1, keepdims=True)
    i2 = jnp.min(jnp.where(el2 == v2, lanef, big), axis=1, keepdims=True).astype(jnp.int32)
    e21 = jnp.exp(v2 - v1)
    inv = 1.0 / (1.0 + e21)
    w1 = inv * gp
    w2 = e21 * inv * gp
    e1 = i1 - N_EXPERT_GROUPS
    e2 = i2 - N_EXPERT_GROUPS
    return lane == e1, lane == e2, w1, w2


def _route_and_sort(logits, hb, cap):
    rows = hb.shape[0]
    lt = logits.T
    row = lax.broadcasted_iota(jnp.int32, (LANES, rows), 0)
    rowf = row.astype(F32)
    big = float(LANES)
    is_g = row < N_EXPERT_GROUPS
    glog = jnp.where(is_g, lt, NEG_INF)
    gmax = jnp.max(glog, axis=0, keepdims=True)
    gp = 1.0 / jnp.sum(jnp.where(is_g, jnp.exp(glog - gmax), 0.0), axis=0, keepdims=True)
    gidx = jnp.min(jnp.where(glog == gmax, rowf, big), axis=0, keepdims=True).astype(jnp.int32)
    lo = N_EXPERT_GROUPS + gidx * EXPERTS_PER_GROUP
    el = jnp.where((row >= lo) & (row < lo + EXPERTS_PER_GROUP), lt, NEG_INF)
    v1 = jnp.max(el, axis=0, keepdims=True)
    i1 = jnp.min(jnp.where(el == v1, rowf, big), axis=0, keepdims=True).astype(jnp.int32)
    el2 = jnp.where(row == i1, NEG_INF, el)
    v2 = jnp.max(el2, axis=0, keepdims=True)
    i2 = jnp.min(jnp.where(el2 == v2, rowf, big), axis=0, keepdims=True).astype(jnp.int32)
    e21 = jnp.exp(v2 - v1)
    inv = 1.0 / (1.0 + e21)
    w1 = inv * gp
    w2 = e21 * inv * gp
    hot1 = row == i1 - N_EXPERT_GROUPS
    hot2 = row == i2 - N_EXPERT_GROUPS
    onehot = jnp.where(hot1 | hot2, 1.0, 0.0)
    counts = jnp.broadcast_to(jnp.sum(onehot, axis=1, keepdims=True), (LANES, LANES))
    earlier = (lax.broadcasted_iota(jnp.int32, (rows, rows), 0)
               < lax.broadcasted_iota(jnp.int32, (rows, rows), 1)).astype(BF16)
    rank = _dot(onehot.astype(BF16), earlier)
    run = (((counts.astype(jnp.int32) + (PIECE - 1)) // PIECE) * PIECE).astype(BF16)
    below = (lax.broadcasted_iota(jnp.int32, (LANES, LANES), 1)
             < lax.broadcasted_iota(jnp.int32, (LANES, LANES), 0)).astype(BF16)
    start = _dot(below, run)
    slot = jnp.concatenate([start] * (rows // LANES), axis=1) + rank
    s1 = jnp.sum(jnp.where(hot1, slot, 0.0), axis=0, keepdims=True)
    s2 = jnp.sum(jnp.where(hot2, slot, 0.0), axis=0, keepdims=True)
    srow = lax.broadcasted_iota(jnp.int32, (cap, rows), 0).astype(F32)
    perm = jnp.where((srow == s1) | (srow == s2), 1.0, 0.0).astype(BF16)
    route_t = jnp.where(row == 0, w1, jnp.where(row == 1, w2, jnp.where(row == 2, s1, jnp.where(row == 3, s2, 0.0))))
    return _dot(perm, hb).astype(BF16), route_t.T, counts


def _sigmoid(x):
    return 0.5 * jnp.tanh(0.5 * x) + 0.5


def _merge_ln1(x, oa, ob, oc, gz, wa_ref, wb_ref, wc_ref, wo_ref, g1, b1):
    ya = _dot(oa, wa_ref[...])
    yb = _dot(ob.astype(BF16), wb_ref[...])
    yc = _dot(oc.astype(BF16), wc_ref[...])
    m = (_sigmoid(gz[:, 0:D_MODEL]) * ya
         + _sigmoid(gz[:, D_MODEL:2 * D_MODEL]) * yb
         + _sigmoid(gz[:, 2 * D_MODEL:3 * D_MODEL]) * yc)
    hpre = ALPHA * x + _dot(m.astype(BF16), wo_ref[...])
    return _layer_norm(hpre, g1, b1)


def _mem_kernel(mem_ref, wk_ref, wv_ref, mk_ref, mv_ref):
    m = mem_ref[...].astype(BF16)
    mk_ref[...] = _dot(m, wk_ref[...])
    mv_ref[...] = _dot(m, wv_ref[...])


def _mem_project(mem, wk, wv):
    b = mem.shape[0]
    out = jax.ShapeDtypeStruct((b, N_MEM, MEM_WIDTH), F32)
    return pl.pallas_call(
        _mem_kernel,
        grid=(b,),
        in_specs=[pl.BlockSpec((None, N_MEM, D_MODEL), lambda i: (i, 0, 0)),
                  _const_spec((D_MODEL, MEM_WIDTH)), _const_spec((D_MODEL, MEM_WIDTH))],
        out_specs=[pl.BlockSpec((None, N_MEM, MEM_WIDTH), lambda i: (i, 0, 0))] * 2,
        out_shape=[out, out],
        name="mem_project",
    )(mem, wk, wv)


def _project_q(xb, wq_ref, zq_ref):
    for c0 in range(0, Q_WIDTH, Q_CHUNK):
        zq_ref[:, c0:c0 + Q_CHUNK] = _dot(xb, wq_ref[:, c0:c0 + Q_CHUNK])


def _project_rest(xb, wrest_ref, z_ref):
    rest = IN_WIDTH - K0
    for c0 in range(0, rest, IN_CHUNK):
        c1 = min(c0 + IN_CHUNK, rest)
        z_ref[:, K0 + c0:K0 + c1] = _dot(xb, wrest_ref[:, c0:c1])


def _front_kernel(sinks_ref, x_ref, xn_ref, rope_ref, wq_ref, wrest_ref, wa_ref, wb_ref, wc_ref, wo_ref, wmix_ref,
                  pscale_ref, mk_ref, mv_ref, g1_ref, b1_ref, wr_ref, br_ref,
                  h_ref, xs_ref, route_ref, counts_ref, nk_ref, nv_ref, npool_ref,
                  z_ref, qb_ref, kext_ref, vext_ref, uext_ref, oa_ref, ob_ref, bias_ref, s_ref, p_ref, vblk_ref, zq_ref):
    i = pl.program_id(1)
    x = x_ref[...]
    xb = x.astype(BF16)
    hist = 2 * SUBLANES

    @pl.when((pl.program_id(0) == 0) & (i == 0))
    def _():
        _project_q(xb, wq_ref, zq_ref)

    @pl.when(i == 0)
    def _():
        kext_ref[0:QB, :] = jnp.zeros((QB, KV_WIDTH), BF16)
        vext_ref[0:QB, :] = jnp.zeros((QB, KV_WIDTH), BF16)
        uext_ref[0:hist, :] = jnp.zeros((hist, POOL_WIDTH), F32)

    @pl.when(i > 0)
    def _():
        kext_ref[0:QB, :] = kext_ref[TM:TM + QB, :]
        vext_ref[0:QB, :] = vext_ref[TM:TM + QB, :]
        uext_ref[0:hist, :] = uext_ref[TM:TM + hist, :]

    _project_rest(xb, wrest_ref, z_ref)

    c = rope_ref[0]
    s1 = rope_ref[1]
    s2 = rope_ref[2]
    for j in range(Q_WIDTH // LANES):
        sl = slice(j * LANES, (j + 1) * LANES)
        qb_ref[:, sl] = _rope(zq_ref[:, sl], c, s1, s2).astype(BF16)
    for j in range(KV_WIDTH // LANES):
        sl = slice(K0 + j * LANES, K0 + (j + 1) * LANES)
        kr = _rope(z_ref[:, sl], c, s1, s2)
        z_ref[:, sl] = kr
        kext_ref[QB:QB + TM, j * LANES:(j + 1) * LANES] = kr.astype(BF16)
    vext_ref[QB:QB + TM, :] = z_ref[:, V0:V0 + KV_WIDTH].astype(BF16)
    uext_ref[hist:hist + TM, :] = z_ref[:, U0:U0 + POOL_WIDTH]
    nk_ref[...] = z_ref[TM - QB:TM, K0:K0 + KV_WIDTH]
    nv_ref[...] = z_ref[TM - QB:TM, V0:V0 + KV_WIDTH]

    rowq = lax.broadcasted_iota(jnp.int32, (QB, 2 * QB), 0)
    colk = lax.broadcasted_iota(jnp.int32, (QB, 2 * QB), 1)
    band = (colk >= rowq) & (colk <= rowq + WINDOW)
    bias_ref[1] = jnp.where(band, 0.0, NEG_INF)
    bias_ref[0] = jnp.where(band & ((colk >= QB) | (i > 0)), 0.0, NEG_INF)
    for sb in range(TM // QB):
        k2 = kext_ref[sb * QB:(sb + 2) * QB, :]
        v2 = vext_ref[sb * QB:(sb + 2) * QB, :]
        qs = jnp.concatenate(
            [qb_ref[sb * QB:(sb + 1) * QB, g * KV_WIDTH:(g + 1) * KV_WIDTH] for g in range(GROUP)], axis=0)
        for kv in range(N_KV):
            kmask = _lane_block_mask((2 * QB, KV_WIDTH), kv)
            s_ref[...] = _dot_nt(qs, jnp.where(kmask, k2, jnp.zeros_like(k2)))
            vblk_ref[kv * 2 * QB:(kv + 1) * 2 * QB, :] = jnp.where(kmask, v2, jnp.zeros_like(v2))
            for c0 in range(0, GROUP * QB, SOFTMAX_ROWS):
                rq = c0 % QB
                sink = sinks_ref[kv * GROUP + c0 // QB]
                s = s_ref[c0:c0 + SOFTMAX_ROWS, :] + bias_ref[min(sb, 1), rq:rq + SOFTMAX_ROWS, :]
                m = jnp.maximum(jnp.max(s, axis=1, keepdims=True), sink)
                p = jnp.exp(s - m)
                den = jnp.sum(p, axis=1, keepdims=True) + jnp.exp(sink - m)
                p_ref[c0:c0 + SOFTMAX_ROWS, kv * 2 * QB:(kv + 1) * 2 * QB] = (p * (1.0 / den)).astype(BF16)
        o = _dot(p_ref[...], vblk_ref[...])
        for g in range(GROUP):
            oa_ref[sb * QB:(sb + 1) * QB, g * KV_WIDTH:(g + 1) * KV_WIDTH] = o[g * QB:(g + 1) * QB].astype(BF16)

    npool_ref[...] = uext_ref[TM:TM + hist, :]
    pos = i * TM + lax.broadcasted_iota(jnp.int32, (TM, 1), 0)
    for g, w in enumerate(POOL_WINDOWS):
        sl = slice(g * POOL_GROUP_DIM, (g + 1) * POOL_GROUP_DIM)
        cur = uext_ref[hist:hist + TM, sl]
        ws = cur
        for j in range(1, w):
            ws = ws + uext_ref[hist - j:hist - j + TM, sl]
        cnt = jnp.minimum(pos + 1, w).astype(F32)
        pooled = ws / cnt - cur
        ob_ref[:, sl] = _dot(pooled.astype(BF16), wmix_ref[g]) * pscale_ref[:, sl]

    cq = (z_ref[:, CQ0:CQ0 + MEM_WIDTH] * ATT_SCALE).astype(BF16)
    mk = mk_ref[...].astype(BF16)
    mv = mv_ref[...].astype(BF16)
    for hh in range(MEM_HEADS):
        hmask = _lane_block_mask((N_MEM, MEM_WIDTH), hh)
        s_ref[0:TM, :] = _dot_nt(cq, jnp.where(hmask, mk, jnp.zeros_like(mk)))
        vblk_ref[hh * N_MEM:(hh + 1) * N_MEM, :] = jnp.where(hmask, mv, jnp.zeros_like(mv))
        for c0 in range(0, TM, SOFTMAX_ROWS):
            s = s_ref[c0:c0 + SOFTMAX_ROWS, :]
            p = jnp.exp(s - jnp.max(s, axis=1, keepdims=True))
            den = jnp.sum(p, axis=1, keepdims=True)
            p_ref[c0:c0 + SOFTMAX_ROWS, hh * N_MEM:(hh + 1) * N_MEM] = (p * (1.0 / den)).astype(BF16)
    oc = _dot(p_ref[0:TM, :], vblk_ref[...])

    h = _merge_ln1(x, oa_ref[...], ob_ref[...], oc, z_ref[:, GZ0:GZ0 + 3 * D_MODEL],
                   wa_ref, wb_ref, wc_ref, wo_ref, g1_ref[...], b1_ref[...])
    h_ref[...] = h
    hb = h.astype(BF16)
    logits = _dot(hb, wr_ref[...]) + br_ref[...]
    _project_q(xn_ref[...].astype(BF16), wq_ref, zq_ref)
    xs_ref[...], route_ref[...], counts_ref[...] = _route_and_sort(logits, hb, xs_ref.shape[0])


def _front(x, rope, sinks, wq, wrest, wa, wb, wc, wo, wmix, pscale, mk, mv, g1, b1, wr, br):
    b, l, _ = x.shape
    nt = l // TM
    cap = _chunk_rows(b * nt)
    hist = 2 * SUBLANES
    tile = lambda w: pl.BlockSpec((None, TM, w), lambda bi, ti: (bi, ti, 0))
    per_b = lambda r, w: pl.BlockSpec((None, r, w), lambda bi, ti: (bi, 0, 0))
    nxt = lambda bi, ti: jnp.minimum(bi * nt + ti + 1, b * nt - 1)
    return pl.pallas_call(
        _front_kernel,
        grid=(b, nt),
        in_specs=[
            pl.BlockSpec(memory_space=pltpu.SMEM),
            tile(D_MODEL),
            pl.BlockSpec((None, TM, D_MODEL), lambda bi, ti: (nxt(bi, ti) // nt, nxt(bi, ti) % nt, 0)),
            pl.BlockSpec((3, TM, LANES), lambda bi, ti: (0, ti, 0)),
            _const_spec((D_MODEL, Q_WIDTH)), _const_spec((D_MODEL, IN_WIDTH - K0)),
            _const_spec((Q_WIDTH, D_MODEL)), _const_spec((POOL_WIDTH, D_MODEL)),
            _const_spec((MEM_WIDTH, D_MODEL)), _const_spec((D_MODEL, D_MODEL)),
            _const_spec((len(POOL_WINDOWS), POOL_GROUP_DIM, POOL_GROUP_DIM)),
            _const_spec((1, POOL_WIDTH)),
            per_b(N_MEM, MEM_WIDTH), per_b(N_MEM, MEM_WIDTH),
            _const_spec((1, D_MODEL)), _const_spec((1, D_MODEL)),
            _const_spec((D_MODEL, LANES)), _const_spec((1, LANES)),
        ],
        out_specs=[
            tile(D_MODEL),
            pl.BlockSpec((cap, D_MODEL), lambda bi, ti: (bi * nt + ti, 0)),
            tile(LANES),
            pl.BlockSpec((None, None, LANES, LANES), lambda bi, ti: (bi, ti, 0, 0)),
            per_b(QB, KV_WIDTH), per_b(QB, KV_WIDTH), per_b(hist, POOL_WIDTH),
        ],
        out_shape=[
            jax.ShapeDtypeStruct((b, l, D_MODEL), F32),
            jax.ShapeDtypeStruct((b * nt * cap, D_MODEL), BF16),
            jax.ShapeDtypeStruct((b, l, LANES), F32),
            jax.ShapeDtypeStruct((b, nt, LANES, LANES), F32),
            jax.ShapeDtypeStruct((b, QB, KV_WIDTH), F32),
            jax.ShapeDtypeStruct((b, QB, KV_WIDTH), F32),
            jax.ShapeDtypeStruct((b, hist, POOL_WIDTH), F32),
        ],
        scratch_shapes=[
            pltpu.VMEM((TM, IN_WIDTH), F32),
            pltpu.VMEM((TM, Q_WIDTH), BF16),
            pltpu.VMEM((QB + TM, KV_WIDTH), BF16),
            pltpu.VMEM((QB + TM, KV_WIDTH), BF16),
            pltpu.VMEM((hist + TM, POOL_WIDTH), F32),
            pltpu.VMEM((TM, Q_WIDTH), BF16),
            pltpu.VMEM((TM, POOL_WIDTH), F32),
            pltpu.VMEM((2, QB, 2 * QB), F32),
            pltpu.VMEM((GROUP * QB, 2 * QB), F32),
            pltpu.VMEM((GROUP * QB, N_KV * 2 * QB), BF16),
            pltpu.VMEM((N_KV * 2 * QB, KV_WIDTH), BF16),
            pltpu.VMEM((TM, Q_WIDTH), F32),
        ],
        compiler_params=pltpu.CompilerParams(
            dimension_semantics=("arbitrary", "arbitrary"), vmem_limit_bytes=VMEM_LIMIT),
        name="front_prompt",
    )(sinks, x, x, rope, wq, wrest, wa, wb, wc, wo, wmix, pscale, mk, mv, g1, b1, wr, br)


def _expert_mlp(xb, wg, wu, wd):
    a = _dot(xb, wg)
    hid = (a * jax.nn.sigmoid(a)) * _dot(xb, wu)
    return _dot(hid.astype(BF16), wd)


def _gemm_kernel(src_ref, dst_ref, te_ref, act_ref, xs_ref, wg_ref, wu_ref, wd_ref, ys_ref,
                 xbuf, obuf, wgb, wub, wdb, prime, sem_in, sem_out):
    i = pl.program_id(0)
    n = pl.num_programs(0)
    slot = i % 2
    in_slot = i % IN_SLOTS

    def start_in(tile, slot):
        for j in range(PIECES_PER_TILE):
            row0 = pl.multiple_of(src_ref[tile * PIECES_PER_TILE + j], PIECE)
            pltpu.make_async_copy(xs_ref.at[pl.ds(row0, PIECE)], xbuf.at[slot, pl.ds(j * PIECE, PIECE)],
                                  sem_in.at[slot]).start()

    def start_out(tile, slot):
        for j in range(PIECES_PER_TILE):
            row0 = pl.multiple_of(dst_ref[tile * PIECES_PER_TILE + j], PIECE)
            pltpu.make_async_copy(obuf.at[slot, pl.ds(j * PIECE, PIECE)], ys_ref.at[pl.ds(row0, PIECE)],
                                  sem_out.at[slot]).start()

    def wait_in(slot):
        for j in range(PIECES_PER_TILE):
            pltpu.make_async_copy(xs_ref.at[pl.ds(0, PIECE)], xbuf.at[slot, pl.ds(j * PIECE, PIECE)],
                                  sem_in.at[slot]).wait()

    def wait_out(slot):
        for j in range(PIECES_PER_TILE):
            pltpu.make_async_copy(obuf.at[slot, pl.ds(j * PIECE, PIECE)], ys_ref.at[pl.ds(0, PIECE)],
                                  sem_out.at[slot]).wait()

    active = act_ref[i] > 0
    ahead = IN_SLOTS - 1
    prefetched = (i < ahead) | (act_ref[jnp.maximum(i - ahead, 0)] > 0)
    out_pending = (i < 2) | (act_ref[jnp.maximum(i - 2, 0)] > 0)

    @pl.when(i == 0)
    def _():
        prime[0] = jnp.zeros((PIECE, D_MODEL), BF16)
        for s in range(2):
            for j in range(PIECES_PER_TILE):
                pltpu.make_async_copy(prime.at[0], prime.at[1 + s * PIECES_PER_TILE + j], sem_out.at[s]).start()
        for t in range(ahead):
            start_in(t, t)

    prev_active = (i >= 1) & (act_ref[jnp.maximum(i - 1, 0)] > 0)

    def active_step(after_first):
        @pl.when((i == 0) | (te_ref[i] != te_ref[jnp.maximum(i - 1, 0)]))
        def _():
            wgb[...] = wg_ref[...].astype(BF16)
            wub[...] = wu_ref[...].astype(BF16)
            wdb[...] = wd_ref[...].astype(BF16)

        wait_in(in_slot)
        wait_out(slot)
        out = _expert_mlp(xbuf[in_slot], wgb[...], wub[...], wdb[...])
        start_in(i + ahead, (i + ahead) % IN_SLOTS)
        if after_first:
            start_out(i - 1, 1 - slot)
        obuf[slot] = out.astype(BF16)

    @pl.when(active & (i == 0))
    def _():
        active_step(False)

    @pl.when(active & (i > 0))
    def _():
        active_step(True)

    @pl.when(jnp.logical_not(active))
    def _():
        @pl.when(prev_active)
        def _():
            start_out(i - 1, 1 - slot)

        @pl.when(prefetched)
        def _():
            wait_in(in_slot)

        @pl.when(out_pending)
        def _():
            wait_out(slot)

    @pl.when(i == n - 1)
    def _():
        for t in range(ahead):
            @pl.when(act_ref[jnp.maximum(i - t, 0)] > 0)
            def _(t=t):
                wait_in((i - t + ahead) % IN_SLOTS)

        @pl.when(active)
        def _():
            start_out(i, slot)
            wait_out(slot)

        @pl.when(prev_active)
        def _():
            wait_out(1 - slot)


def _grouped_gemm(piece_src, piece_dst, tile_expert, tile_active, xs, wg, wu, wd):
    n_tiles = tile_expert.shape[0]
    assert n_tiles >= 2
    wspec = lambda r, c: pl.BlockSpec((None, r, c), lambda i, src, dst, te, act: (te[i], 0, 0))
    return pl.pallas_call(
        _gemm_kernel,
        grid_spec=pltpu.PrefetchScalarGridSpec(
            num_scalar_prefetch=4,
            grid=(n_tiles,),
            in_specs=[pl.BlockSpec(memory_space=pl.ANY),
                      wspec(D_MODEL, D_EXPERT), wspec(D_MODEL, D_EXPERT), wspec(D_EXPERT, D_MODEL)],
            out_specs=pl.BlockSpec(memory_space=pl.ANY),
            scratch_shapes=[pltpu.VMEM((IN_SLOTS, TG, D_MODEL), BF16), pltpu.VMEM((2, TG, D_MODEL), BF16),
                            pltpu.VMEM((D_MODEL, D_EXPERT), BF16), pltpu.VMEM((D_MODEL, D_EXPERT), BF16),
                            pltpu.VMEM((D_EXPERT, D_MODEL), BF16),
                            pltpu.VMEM((1 + 2 * PIECES_PER_TILE, PIECE, D_MODEL), BF16),
                            pltpu.SemaphoreType.DMA((IN_SLOTS,)), pltpu.SemaphoreType.DMA((2,))],
        ),
        out_shape=jax.ShapeDtypeStruct(xs.shape, xs.dtype),
        input_output_aliases={4: 0},
        compiler_params=pltpu.CompilerParams(dimension_semantics=("arbitrary",), vmem_limit_bytes=VMEM_LIMIT),
        name="moe_grouped_gemm",
    )(piece_src, piece_dst, tile_expert, tile_active, xs, wg, wu, wd)


def _combine_kernel(ys_ref, h_ref, route_ref, g2_ref, b2_ref, y_ref):
    chunks = h_ref.shape[0] // TM
    cap = ys_ref.shape[0] // chunks
    slot = lax.broadcasted_iota(jnp.int32, (TM, cap), 1).astype(F32)
    for c in range(chunks):
        rows = slice(c * TM, (c + 1) * TM)
        route = route_ref[rows, :]
        sel = jnp.concatenate([jnp.where(slot == route[:, 2:3], 1.0, 0.0).astype(BF16),
                               jnp.where(slot == route[:, 3:4], 1.0, 0.0).astype(BF16)], axis=0)
        picked = _dot(sel, ys_ref[c * cap:(c + 1) * cap, :])
        f = route[:, 0:1] * picked[0:TM] + route[:, 1:2] * picked[TM:2 * TM]
        y_ref[rows, :] = _layer_norm(ALPHA * h_ref[rows, :] + f, g2_ref[...], b2_ref[...])


def _combine(ys, h, route, g2, b2):
    t = h.shape[0]
    cap = ys.shape[0] // (t // TM)
    per_step = next(k for k in (4, 2, 1) if (t // TM) % k == 0)
    rows = per_step * TM
    return pl.pallas_call(
        _combine_kernel,
        grid=(t // rows,),
        in_specs=[
            pl.BlockSpec((per_step * cap, D_MODEL), lambda i: (i, 0)),
            pl.BlockSpec((rows, D_MODEL), lambda i: (i, 0)),
            pl.BlockSpec((rows, LANES), lambda i: (i, 0)),
            pl.BlockSpec((1, D_MODEL), lambda i: (0, 0)),
            pl.BlockSpec((1, D_MODEL), lambda i: (0, 0)),
        ],
        out_specs=pl.BlockSpec((rows, D_MODEL), lambda i: (i, 0)),
        out_shape=jax.ShapeDtypeStruct((t, D_MODEL), F32),
        compiler_params=pltpu.CompilerParams(dimension_semantics=("arbitrary",)),
        name="moe_combine",
    )(ys, h, route, g2, b2)


def _select(table, idx):
    hot = idx[:, None] == jnp.arange(table.shape[0], dtype=jnp.int32)[None, :]
    return jnp.sum(jnp.where(hot[:, :, None], table[None, :, :], 0), axis=1)


def _piece_tables(counts):
    n_chunks = counts.shape[0]
    n_tiles = -(-(n_chunks * MAX_CHUNK_PIECES + N_EXPERTS * (PIECES_PER_TILE - 1)) // PIECES_PER_TILE)
    npc = (counts + (PIECE - 1)) // PIECE
    first = (jnp.cumsum(npc, axis=1) - npc).T
    npc_t = npc.T
    cum = jnp.cumsum(npc_t, axis=1)
    per_expert = cum[:, -1]
    tiles_e = (per_expert + (PIECES_PER_TILE - 1)) // PIECES_PER_TILE
    tile_end = jnp.cumsum(tiles_e)
    tile_idx = jnp.arange(n_tiles, dtype=jnp.int32)
    expert_of = lambda i: jnp.minimum(jnp.sum(i[:, None] >= tile_end[None, :], axis=1), N_EXPERTS - 1).astype(jnp.int32)
    active = tile_idx < tile_end[-1]
    tile_expert = jnp.where(active, expert_of(tile_idx), expert_of(tile_end[-1:] - 1))
    meta = jnp.stack([tile_end - tiles_e, per_expert], axis=1)
    meta_t = _select(meta, tile_expert)
    k = (tile_idx - meta_t[:, 0])[:, None] * PIECES_PER_TILE + jnp.arange(PIECES_PER_TILE, dtype=jnp.int32)[None, :]
    valid = active[:, None] & (k < meta_t[:, 1:2])
    cum_t = _select(cum, tile_expert)
    chunk = jnp.minimum(jnp.sum(k[:, :, None] >= cum_t[:, None, :], axis=2), n_chunks - 1).astype(jnp.int32)
    at_chunk = chunk[:, :, None] == jnp.arange(n_chunks, dtype=jnp.int32)[None, None, :]
    pick = lambda tab: jnp.sum(jnp.where(at_chunk, _select(tab, tile_expert)[:, None, :], 0), axis=2)
    piece = pick(first) + k - pick(cum - npc_t)
    cap = _chunk_rows(n_chunks)
    rows = chunk * cap + piece * PIECE
    d = ((tile_idx % 2)[:, None] * (PIECES_PER_TILE - 1)
         + jnp.maximum(jnp.arange(PIECES_PER_TILE, dtype=jnp.int32)[None, :] - 1, 0))
    spare_row = lambda d: (d % n_chunks) * cap + (MAX_CHUNK_PIECES + d // n_chunks) * PIECE
    spare = spare_row(d)
    zero_piece = spare_row(N_SPARE - 1)
    extra = jnp.full(((IN_SLOTS - 1) * PIECES_PER_TILE,), zero_piece, jnp.int32)
    src = jnp.concatenate([jnp.where(valid, rows, zero_piece).astype(jnp.int32).reshape(-1), extra])
    dst = jnp.concatenate([jnp.where(valid, rows, spare).astype(jnp.int32).reshape(-1), extra])
    return src, dst, tile_expert.astype(jnp.int32), active.astype(jnp.int32)


def _sample_attn_kernel(x_ref, rope_ref, wq_ref, wrest_ref, sink_ref, ck_ref, cv_ref, cmk_ref, cmv_ref,
                        z_ref, oa_ref, oc_ref, nk_ref, nv_ref, knew_t, vnew_t):
    j = pl.program_id(0)
    db = x_ref.shape[0]

    @pl.when(j == 0)
    def _():
        xb = x_ref[...].astype(BF16)
        _project_q(xb, wq_ref, z_ref)
        _project_rest(xb, wrest_ref, z_ref)
        c = rope_ref[0]
        s1 = rope_ref[1]
        s2 = rope_ref[2]
        for jj in range((Q_WIDTH + KV_WIDTH) // LANES):
            sl = slice(jj * LANES, (jj + 1) * LANES)
            z_ref[:, sl] = _rope(z_ref[:, sl], c, s1, s2)
        knew_t[...] = z_ref[:, K0:K0 + KV_WIDTH].T
        vnew_t[...] = z_ref[:, V0:V0 + KV_WIDTH].T

    r0 = pl.multiple_of(j * SB, SB)
    zq = z_ref[pl.ds(r0, SB), Q0:Q0 + Q_WIDTH]
    zk = z_ref[pl.ds(r0, SB), K0:K0 + KV_WIDTH]
    zv = z_ref[pl.ds(r0, SB), V0:V0 + KV_WIDTH]
    zc = z_ref[pl.ds(r0, SB), CQ0:CQ0 + MEM_WIDTH] * ATT_SCALE
    sink = sink_ref[:, 0:1]
    row_kv = lax.broadcasted_iota(jnp.int32, (N_HEADS, KV_WIDTH), 0) & (N_KV - 1)
    lane_kv = lax.broadcasted_iota(jnp.int32, (N_HEADS, KV_WIDTH), 1) // HEAD_DIM
    own = row_kv == lane_kv
    row_c = lax.broadcasted_iota(jnp.int32, (N_HEADS, MEM_WIDTH), 0)
    lane_c = lax.broadcasted_iota(jnp.int32, (N_HEADS, MEM_WIDTH), 1) // HEAD_DIM
    own_c = row_c == lane_c
    last_pos = lax.broadcasted_iota(jnp.int32, (KV_WIDTH, WINDOW), 1) == WINDOW - 1
    seq_lane = lax.broadcasted_iota(jnp.int32, (KV_WIDTH, db), 1)

    seqs = range(SB)
    qblk, cblk, s, sc = [], [], [], []
    for b in seqs:
        q4 = jnp.concatenate(
            [jnp.broadcast_to(zq[b:b + 1, g * KV_WIDTH:(g + 1) * KV_WIDTH], (N_KV, KV_WIDTH)) for g in range(GROUP)],
            axis=0)
        qblk.append(jnp.where(own, q4, 0.0).astype(BF16))
        cblk.append(jnp.where(own_c, jnp.broadcast_to(zc[b:b + 1, :], (N_HEADS, MEM_WIDTH)), 0.0).astype(BF16))
        s.append(_dot(qblk[b], ck_ref[b].astype(BF16)))
        sc.append(_dot(cblk[b], cmk_ref[b].astype(BF16)))

    pn, p_new, pc = [], [], []
    for b in seqs:
        s_new = jnp.sum(qblk[b].astype(F32) * zk[b:b + 1, :].astype(BF16).astype(F32), axis=1, keepdims=True)
        m = jnp.maximum(jnp.maximum(jnp.max(s[b], axis=1, keepdims=True), s_new), sink)
        p = jnp.exp(s[b] - m)
        e_new = jnp.exp(s_new - m)
        inv = 1.0 / (jnp.sum(p, axis=1, keepdims=True) + e_new + jnp.exp(sink - m))
        pn.append((p * inv).astype(BF16))
        p_new.append((e_new * inv).astype(BF16).astype(F32))
        e = jnp.exp(sc[b] - jnp.max(sc[b], axis=1, keepdims=True))
        pc.append((e * (1.0 / jnp.sum(e, axis=1, keepdims=True))).astype(BF16))

    oa_rows = [[] for _ in range(GROUP)]
    oc_rows = []
    for b in seqs:
        vc = cv_ref[b]
        o = _dot_nt(pn[b], vc.astype(BF16)) + p_new[b] * zv[b:b + 1, :].astype(BF16).astype(F32)
        o = jnp.where(own, o, 0.0)
        for g in range(GROUP):
            oa_rows[g].append(jnp.sum(o[g * N_KV:(g + 1) * N_KV], axis=0, keepdims=True))
        ocb = jnp.where(own_c, _dot_nt(pc[b], cmv_ref[b].astype(BF16)), 0.0)
        oc_rows.append(jnp.sum(ocb, axis=0, keepdims=True))
        here = seq_lane == j * SB + b
        knew_col = jnp.sum(jnp.where(here, knew_t[...], 0.0), axis=1, keepdims=True)
        vnew_col = jnp.sum(jnp.where(here, vnew_t[...], 0.0), axis=1, keepdims=True)
        nk_ref[b] = jnp.where(last_pos, knew_col, pltpu.roll(ck_ref[b], WINDOW - 1, 1))
        nv_ref[b] = jnp.where(last_pos, vnew_col, pltpu.roll(vc, WINDOW - 1, 1))
    for g in range(GROUP):
        oa_ref[pl.ds(r0, SB), g * KV_WIDTH:(g + 1) * KV_WIDTH] = jnp.concatenate(oa_rows[g], axis=0)
    oc_ref[pl.ds(r0, SB), :] = jnp.concatenate(oc_rows, axis=0)


def _sample_attn(x, rope, wq, wrest, sink_gk, ck, cv, cmk, cmv):
    db = x.shape[0]
    blk = lambda r: pl.BlockSpec((SB, KV_WIDTH, r), lambda j: (j, 0, 0))
    full = lambda w: pl.BlockSpec((db, w), lambda j: (0, 0))
    return pl.pallas_call(
        _sample_attn_kernel,
        grid=(db // SB,),
        in_specs=[
            full(D_MODEL), _const_spec((3, 1, LANES)),
            _const_spec((D_MODEL, Q_WIDTH)), _const_spec((D_MODEL, IN_WIDTH - K0)),
            _const_spec((N_HEADS, LANES)),
            blk(WINDOW), blk(WINDOW), blk(N_MEM), blk(N_MEM),
        ],
        out_specs=[full(IN_WIDTH), full(Q_WIDTH), full(MEM_WIDTH), blk(WINDOW), blk(WINDOW)],
        out_shape=[
            jax.ShapeDtypeStruct((db, IN_WIDTH), F32),
            jax.ShapeDtypeStruct((db, Q_WIDTH), F32),
            jax.ShapeDtypeStruct((db, MEM_WIDTH), F32),
            jax.ShapeDtypeStruct((db, KV_WIDTH, WINDOW), F32),
            jax.ShapeDtypeStruct((db, KV_WIDTH, WINDOW), F32),
        ],
        scratch_shapes=[pltpu.VMEM((KV_WIDTH, db), F32), pltpu.VMEM((KV_WIDTH, db), F32)],
        compiler_params=pltpu.CompilerParams(dimension_semantics=("arbitrary",), vmem_limit_bytes=VMEM_LIMIT),
        name="sample_attn",
    )(x, rope, wq, wrest, sink_gk, ck, cv, cmk, cmv)


def _sample_tail_kernel(x_ref, z_ref, oa_ref, oc_ref, st_ref, wa_ref, wb_ref, wc_ref, wo_ref, wmix_ref,
                        pscale_ref, g1_ref, b1_ref, wr_ref, br_ref, wg_ref, wu_ref, wd_ref, g2_ref, b2_ref,
                        y_ref, npool_ref, h_sc, comb_sc, acc_sc):
    e = pl.program_id(0)

    @pl.when(e == 0)
    def _():
        u = z_ref[:, U0:U0 + POOL_WIDTH]
        npool_ref[0:POOL_STATE - 1] = st_ref[1:POOL_STATE]
        npool_ref[POOL_STATE - 1] = u
        obs = []
        for g, w in enumerate(POOL_WINDOWS):
            sl = slice(g * POOL_GROUP_DIM, (g + 1) * POOL_GROUP_DIM)
            cur = u[:, sl]
            ws = cur
            for jj in range(1, w):
                ws = ws + st_ref[POOL_STATE - jj, :, sl]
            cnt = float(min(PAST_LEN + 1, w))
            pooled = ws / cnt - cur
            obs.append(_dot(pooled.astype(BF16), wmix_ref[g]) * pscale_ref[:, sl])
        ob = jnp.concatenate(obs, axis=1)
        h = _merge_ln1(x_ref[...], oa_ref[...].astype(BF16), ob, oc_ref[...], z_ref[:, GZ0:GZ0 + 3 * D_MODEL],
                       wa_ref, wb_ref, wc_ref, wo_ref, g1_ref[...], b1_ref[...])
        h_sc[...] = h
        logits = _dot(h.astype(BF16), wr_ref[...]) + br_ref[...]
        hot1, hot2, w1, w2 = _route(logits)
        comb_sc[...] = jnp.where(hot1, w1, 0.0) + jnp.where(hot2, w2, 0.0)
        acc_sc[...] = jnp.zeros_like(acc_sc)

    out = _expert_mlp(h_sc[...].astype(BF16), wg_ref[...].astype(BF16), wu_ref[...].astype(BF16),
                      wd_ref[...].astype(BF16))
    lane = lax.broadcasted_iota(jnp.int32, comb_sc.shape, 1)
    ce = jnp.sum(jnp.where(lane == e, comb_sc[...], 0.0), axis=1, keepdims=True)
    acc_sc[...] += ce * out

    @pl.when(e == pl.num_programs(0) - 1)
    def _():
        y_ref[...] = _layer_norm(ALPHA * h_sc[...] + acc_sc[...], g2_ref[...], b2_ref[...])


def _sample_tail(x, z, oa, oc, state, wa, wb, wc, wo, wmix, pscale, g1, b1, wr, br, wg, wu, wd, g2, b2):
    db = x.shape[0]
    full = lambda w: pl.BlockSpec((db, w), lambda e: (0, 0))
    vec = lambda w: pl.BlockSpec((1, w), lambda e: (0, 0))
    hist = pl.BlockSpec((POOL_STATE, db, POOL_WIDTH), lambda e: (0, 0, 0))
    return pl.pallas_call(
        _sample_tail_kernel,
        grid=(N_EXPERTS,),
        in_specs=[
            full(D_MODEL), full(IN_WIDTH), full(Q_WIDTH), full(MEM_WIDTH), hist,
            _const_spec((Q_WIDTH, D_MODEL)), _const_spec((POOL_WIDTH, D_MODEL)),
            _const_spec((MEM_WIDTH, D_MODEL)), _const_spec((D_MODEL, D_MODEL)),
            _const_spec((len(POOL_WINDOWS), POOL_GROUP_DIM, POOL_GROUP_DIM)),
            vec(POOL_WIDTH), vec(D_MODEL), vec(D_MODEL),
            _const_spec((D_MODEL, LANES)), vec(LANES),
            pl.BlockSpec((None, D_MODEL, D_EXPERT), lambda e: (e, 0, 0)),
            pl.BlockSpec((None, D_MODEL, D_EXPERT), lambda e: (e, 0, 0)),
            pl.BlockSpec((None, D_EXPERT, D_MODEL), lambda e: (e, 0, 0)),
            vec(D_MODEL), vec(D_MODEL),
        ],
        out_specs=[full(D_MODEL), hist],
        out_shape=[jax.ShapeDtypeStruct((db, D_MODEL), F32),
                   jax.ShapeDtypeStruct((POOL_STATE, db, POOL_WIDTH), F32)],
        scratch_shapes=[pltpu.VMEM((db, D_MODEL), F32), pltpu.VMEM((db, LANES), F32),
                        pltpu.VMEM((db, D_MODEL), F32)],
        compiler_params=pltpu.CompilerParams(dimension_semantics=("arbitrary",), vmem_limit_bytes=VMEM_LIMIT),
        name="sample_tail",
    )(x, z, oa, oc, state, wa, wb, wc, wo, wmix, pscale, g1, b1, wr, br, wg, wu, wd, g2, b2)


def _rope_tables(pos):
    half = ROPE_DIM // 2
    inv = jnp.power(ROPE_THETA, -jnp.arange(half, dtype=F32) * (2.0 / ROPE_DIM))
    ang = pos.astype(F32)[:, None] * inv[None, :]
    lane = np.arange(LANES)
    off = lane % HEAD_DIM
    cos = jnp.cos(ang)[:, lane % half]
    sin = jnp.sin(ang)[:, lane % half]
    c = jnp.where(off[None, :] < ROPE_DIM, cos, 1.0)
    s1 = jnp.where((off[None, :] >= half) & (off[None, :] < ROPE_DIM), sin, 0.0)
    s2 = jnp.where(off[None, :] < half, -sin, 0.0)
    return jnp.stack([c, s1, s2]).astype(F32)


def _q_heads_group_major(w, axis):
    if axis == 1:
        n = w.shape[0]
        return w.reshape(n, N_KV, GROUP, HEAD_DIM).transpose(0, 2, 1, 3).reshape(n, Q_WIDTH)
    n = w.shape[1]
    return w.reshape(N_KV, GROUP, HEAD_DIM, n).transpose(1, 0, 2, 3).reshape(Q_WIDTH, n)


def kernel(x_prompt, x_sample, cache_win_k, cache_win_v, state_pool, cache_mem_k, cache_mem_v, mem_prompt, w_in, sinks, w_pool_mix, pool_scale, w_mem_k, w_mem_v, w_branch_a, w_branch_b, w_branch_c, w_out, ln1_g, ln1_b, w_group, b_group, w_router, b_router, w_gate, w_up, w_down, ln2_g, ln2_b):
    assert w_in.shape[0] == DEPTH == 1
    b, l, _ = x_prompt.shape
    db, ds, _ = x_sample.shape
    assert ds == 1 and l % TM == 0 and db % SB == 0
    assert cache_win_k.shape[2] == WINDOW
    t = b * l

    win = w_in[0]
    wq = (_q_heads_group_major(win[:, Q0:Q0 + Q_WIDTH], 1) * ATT_SCALE).astype(BF16)
    wrest = win[:, K0:].astype(BF16)
    wa = _q_heads_group_major(w_branch_a[0], 0).astype(BF16)
    wb = w_branch_b[0].astype(BF16)
    wc = w_branch_c[0].astype(BF16)
    wo = w_out[0].astype(BF16)
    wmix = w_pool_mix[0].astype(BF16)
    pscale = pool_scale[0].reshape(1, POOL_WIDTH)
    g1 = ln1_g[0].reshape(1, D_MODEL)
    b1 = ln1_b[0].reshape(1, D_MODEL)
    g2 = ln2_g[0].reshape(1, D_MODEL)
    b2 = ln2_b[0].reshape(1, D_MODEL)
    wr = jnp.concatenate([w_group[0], w_router[0].reshape(D_MODEL, N_EXPERTS)], axis=1)
    wr = jnp.pad(wr, ((0, 0), (0, LANES - wr.shape[1]))).astype(BF16)
    br = jnp.pad(jnp.concatenate([b_group[0], b_router[0].reshape(N_EXPERTS)]), (0, LANES - N_EXPERT_GROUPS - N_EXPERTS))
    br = br.reshape(1, LANES).astype(F32)
    wg = w_gate[0]
    wu = w_up[0]
    wd = w_down[0]
    sink = sinks[0].astype(F32)
    sink_gk = jnp.broadcast_to(sink.reshape(N_KV, GROUP).T.reshape(N_HEADS, 1), (N_HEADS, LANES))

    mk, mv = _mem_project(mem_prompt, w_mem_k[0].astype(BF16), w_mem_v[0].astype(BF16))
    rope_p = _rope_tables(jnp.arange(l, dtype=jnp.int32))
    h, xs, route, counts, nk_p, nv_p, npool_p = _front(
        x_prompt, rope_p, sink, wq, wrest, wa, wb, wc, wo, wmix, pscale, mk, mv, g1, b1, wr, br)
    piece_src, piece_dst, tile_expert, tile_active = _piece_tables(
        counts.reshape(-1, LANES, LANES)[:, :N_EXPERTS, 0].astype(jnp.int32))
    ys = _grouped_gemm(piece_src, piece_dst, tile_expert, tile_active, xs, wg, wu, wd)
    y_p = _combine(ys, h.reshape(t, D_MODEL), route.reshape(t, LANES), g2, b2).reshape(b, l, D_MODEL)

    rope_s = _rope_tables(jnp.full((1,), PAST_LEN, jnp.int32))
    xs = x_sample.reshape(db, D_MODEL)
    feat_major = lambda c: jnp.transpose(c[0], (0, 2, 3, 1)).reshape(db, KV_WIDTH, c.shape[2])
    from_feat_major = lambda a: jnp.transpose(a.reshape(db, N_KV, HEAD_DIM, a.shape[2]), (0, 3, 1, 2))[None]
    z_s, oa_s, oc_s, nk_s, nv_s = _sample_attn(xs, rope_s, wq, wrest, sink_gk, feat_major(cache_win_k),
                                               feat_major(cache_win_v), feat_major(cache_mem_k), feat_major(cache_mem_v))
    state = jnp.transpose(state_pool[0], (1, 0, 2))
    y_s, npool_s = _sample_tail(xs, z_s, oa_s, oc_s, state, wa, wb, wc, wo, wmix, pscale, g1, b1, wr, br,
                                wg, wu, wd, g2, b2)

    kv5 = lambda a, n, w: a.reshape(1, n, w, N_KV, HEAD_DIM)
    return (y_p, y_s.reshape(db, 1, D_MODEL),
            kv5(nk_p, b, QB), kv5(nv_p, b, QB),
            npool_p[:, 2 * SUBLANES - POOL_STATE:, :][None],
            kv5(mk, b, N_MEM), kv5(mv, b, N_MEM),
            from_feat_major(nk_s), from_feat_major(nv_s),
            jnp.transpose(npool_s, (1, 0, 2))[None])
```

```python
import jax
import jax.numpy as jnp
from jax import lax
from jax.experimental import pallas as pl
from jax.experimental.pallas import tpu as pltpu

D_MODEL = 1024
N_HEADS = 16
HEAD_DIM = 64
N_KV = 4
GROUP = N_HEADS // N_KV
WINDOW = 128
ROPE_THETA = 500000.0
ROPE_DIM = HEAD_DIM // 4
Q_WIDTH = N_HEADS * HEAD_DIM
KV_WIDTH = N_KV * HEAD_DIM
POOL_WINDOWS = (2, 4, 8, 16)
POOL_WIDTH = D_MODEL // 2
POOL_GROUP_DIM = POOL_WIDTH // len(POOL_WINDOWS)
POOL_STATE = max(POOL_WINDOWS) - 1
N_MEM = 256
MEM_HEADS = 4
MEM_WIDTH = MEM_HEADS * HEAD_DIM
N_EXPERT_GROUPS = 4
EXPERTS_PER_GROUP = 4
N_EXPERTS = N_EXPERT_GROUPS * EXPERTS_PER_GROUP
D_EXPERT = 512
PAST_LEN = 16384
DEPTH = 1
ALPHA = (2.0 * DEPTH) ** 0.25
LN_EPS = 1e-5

Q0 = 0
K0 = Q0 + Q_WIDTH
V0 = K0 + KV_WIDTH
U0 = V0 + KV_WIDTH
CQ0 = U0 + POOL_WIDTH
GZ0 = CQ0 + MEM_WIDTH
IN_WIDTH = GZ0 + 3 * D_MODEL

LANES = 128
SUBLANES = 8
VMEM_LIMIT = 56 * 1024 * 1024

TM = 256
QB = WINDOW
PIECE = 16
PIECES_PER_TILE = 32
TG = PIECE * PIECES_PER_TILE
MAX_CHUNK_PIECES = 2 * TM // PIECE + N_EXPERTS - 1
N_SPARE = 2 * (PIECES_PER_TILE - 1) + 1
IN_SLOTS = 3


def _chunk_rows(n_chunks):
    spare = -(-N_SPARE // n_chunks)
    return -(-(MAX_CHUNK_PIECES + spare) * PIECE // LANES) * LANES
SB = 8
IN_CHUNK = 768
Q_CHUNK = 512
ATT_SCALE = HEAD_DIM ** -0.5
SOFTMAX_ROWS = 64
assert N_MEM == 2 * QB and MEM_HEADS == N_KV and MEM_WIDTH == KV_WIDTH and TM <= GROUP * QB

BF16 = jnp.bfloat16
F32 = jnp.float32
NEG_INF = float("-inf")


def _const_spec(shape):
    nd = len(shape)
    return pl.BlockSpec(shape, lambda *_: (0,) * nd, pipeline_mode=pl.Buffered(1))


def _layer_norm(x, g, b):
    mu = jnp.mean(x, axis=-1, keepdims=True)
    xc = x - mu
    var = jnp.mean(xc * xc, axis=-1, keepdims=True)
    return xc * lax.rsqrt(var + LN_EPS) * g + b


def _dot(a, b):
    return jnp.dot(a, b, preferred_element_type=F32)


def _dot_nt(a, b):
    return lax.dot_general(a, b, (((1,), (1,)), ((), ())), preferred_element_type=F32)


def _lane_block_mask(shape, block, width=HEAD_DIM):
    lane = lax.broadcasted_iota(jnp.int32, shape, len(shape) - 1)
    return (lane >= block * width) & (lane < (block + 1) * width)


def _rope(x, c, s1, s2):
    half = ROPE_DIM // 2
    return x * c + pltpu.roll(x, half, 1) * s1 + pltpu.roll(x, LANES - half, 1) * s2


def _route(logits):
    rows = logits.shape[0]
    lane = lax.broadcasted_iota(jnp.int32, (rows, LANES), 1)
    lanef = lane.astype(F32)
    big = float(LANES)
    is_g = lane < N_EXPERT_GROUPS
    glog = jnp.where(is_g, logits, NEG_INF)
    gmax = jnp.max(glog, axis=1, keepdims=True)
    gsum = jnp.sum(jnp.where(is_g, jnp.exp(glog - gmax), 0.0), axis=1, keepdims=True)
    gp = 1.0 / gsum
    gidx = jnp.min(jnp.where(glog == gmax, lanef, big), axis=1, keepdims=True).astype(jnp.int32)
    lo = N_EXPERT_GROUPS + gidx * EXPERTS_PER_GROUP
    in_grp = (lane >= lo) & (lane < lo + EXPERTS_PER_GROUP)
    el = jnp.where(in_grp, logits, NEG_INF)
    v1 = jnp.max(el, axis=1, keepdims=True)
    i1 = jnp.min(jnp.where(el == v1, lanef, big), axis=1, keepdims=True).astype(jnp.int32)
    el2 = jnp.where(lane == i1, NEG_INF, el)
    v2 = jnp.max(el2, axis=1, keepdims=True)
    i2 = jnp.min(jnp.where(el2 == v2, lanef, big), axis=1, keepdims=True).astype(jnp.int32)
    e21 = jnp.exp(v2 - v1)
    inv = 1.0 / (1.0 + e21)
    w1 = inv * gp
    w2 = e21 * inv * gp
    e1 = i1 - N_EXPERT_GROUPS
    e2 = i2 - N_EXPERT_GROUPS
    return lane == e1, lane == e2, w1, w2


def _route_and_sort(logits, hb, cap):
    rows = hb.shape[0]
    lt = logits.T
    row = lax.broadcasted_iota(jnp.int32, (LANES, rows), 0)
    rowf = row.astype(F32)
    big = float(LANES)
    is_g = row < N_EXPERT_GROUPS
    glog = jnp.where(is_g, lt, NEG_INF)
    gmax = jnp.max(glog, axis=0, keepdims=True)
    gp = 1.0 / jnp.sum(jnp.where(is_g, jnp.exp(glog - gmax), 0.0), axis=0, keepdims=True)
    gidx = jnp.min(jnp.where(glog == gmax, rowf, big), axis=0, keepdims=True).astype(jnp.int32)
    lo = N_EXPERT_GROUPS + gidx * EXPERTS_PER_GROUP
    el = jnp.where((row >= lo) & (row < lo + EXPERTS_PER_GROUP), lt, NEG_INF)
    v1 = jnp.max(el, axis=0, keepdims=True)
    i1 = jnp.min(jnp.where(el == v1, rowf, big), axis=0, keepdims=True).astype(jnp.int32)
    el2 = jnp.where(row == i1, NEG_INF, el)
    v2 = jnp.max(el2, axis=0, keepdims=True)
    i2 = jnp.min(jnp.where(el2 == v2, rowf, big), axis=0, keepdims=True).astype(jnp.int32)
    e21 = jnp.exp(v2 - v1)
    inv = 1.0 / (1.0 + e21)
    w1 = inv * gp
    w2 = e21 * inv * gp
    hot1 = row == i1 - N_EXPERT_GROUPS
    hot2 = row == i2 - N_EXPERT_GROUPS
    onehot = jnp.where(hot1 | hot2, 1.0, 0.0)
    counts = jnp.broadcast_to(jnp.sum(onehot, axis=1, keepdims=True), (LANES, LANES))
    earlier = (lax.broadcasted_iota(jnp.int32, (rows, rows), 0)
               < lax.broadcasted_iota(jnp.int32, (rows, rows), 1)).astype(BF16)
    rank = _dot(onehot.astype(BF16), earlier)
    run = (((counts.astype(jnp.int32) + (PIECE - 1)) // PIECE) * PIECE).astype(BF16)
    below = (lax.broadcasted_iota(jnp.int32, (LANES, LANES), 1)
             < lax.broadcasted_iota(jnp.int32, (LANES, LANES), 0)).astype(BF16)
    start = _dot(below, run)
    slot = jnp.concatenate([start] * (rows // LANES), axis=1) + rank
    s1 = jnp.sum(jnp.where(hot1, slot, 0.0), axis=0, keepdims=True)
    s2 = jnp.sum(jnp.where(hot2, slot, 0.0), axis=0, keepdims=True)
    srow = lax.broadcasted_iota(jnp.int32, (cap, rows), 0).astype(F32)
    perm = jnp.where((srow == s1) | (srow == s2), 1.0, 0.0).astype(BF16)
    route_t = jnp.where(row == 0, w1, jnp.where(row == 1, w2, jnp.where(row == 2, s1, jnp.where(row == 3, s2, 0.0))))
    return _dot(perm, hb).astype(BF16), route_t.T, counts


def _sigmoid(x):
    return 0.5 * jnp.tanh(0.5 * x) + 0.5


def _merge_ln1(x, oa, ob, oc, gz, wa_ref, wb_ref, wc_ref, wo_ref, g1, b1):
    ya = _dot(oa, wa_ref[...])
    yb = _dot(ob.astype(BF16), wb_ref[...])
    yc = _dot(oc.astype(BF16), wc_ref[...])
    m = (_sigmoid(gz[:, 0:D_MODEL]) * ya
         + _sigmoid(gz[:, D_MODEL:2 * D_MODEL]) * yb
         + _sigmoid(gz[:, 2 * D_MODEL:3 * D_MODEL]) * yc)
    hpre = ALPHA * x + _dot(m.astype(BF16), wo_ref[...])
    return _layer_norm(hpre, g1, b1)


def _mem_kernel(mem_ref, wk_ref, wv_ref, mk_ref, mv_ref):
    m = mem_ref[...].astype(BF16)
    mk_ref[...] = _dot(m, wk_ref[...])
    mv_ref[...] = _dot(m, wv_ref[...])


def _mem_project(mem, wk, wv):
    b = mem.shape[0]
    out = jax.ShapeDtypeStruct((b, N_MEM, MEM_WIDTH), F32)
    return pl.pallas_call(
        _mem_kernel,
        grid=(b,),
        in_specs=[pl.BlockSpec((None, N_MEM, D_MODEL), lambda i: (i, 0, 0)),
                  _const_spec((D_MODEL, MEM_WIDTH)), _const_spec((D_MODEL, MEM_WIDTH))],
        out_specs=[pl.BlockSpec((None, N_MEM, MEM_WIDTH), lambda i: (i, 0, 0))] * 2,
        out_shape=[out, out],
        name="mem_project",
    )(mem, wk, wv)


def _project_q(xb, wq_ref, zq_ref):
    for c0 in range(0, Q_WIDTH, Q_CHUNK):
        zq_ref[:, c0:c0 + Q_CHUNK] = _dot(xb, wq_ref[:, c0:c0 + Q_CHUNK])


def _project_rest(xb, wrest_ref, z_ref):
    rest = IN_WIDTH - K0
    for c0 in range(0, rest, IN_CHUNK):
        c1 = min(c0 + IN_CHUNK, rest)
        z_ref[:, K0 + c0:K0 + c1] = _dot(xb, wrest_ref[:, c0:c1])


def _front_kernel(sinks_ref, x_ref, xn_ref, rope_ref, wq_ref, wrest_ref, wa_ref, wb_ref, wc_ref, wo_ref, wmix_ref,
                  pscale_ref, mk_ref, mv_ref, g1_ref, b1_ref, wr_ref, br_ref,
                  h_ref, xs_ref, route_ref, counts_ref, nk_ref, nv_ref, npool_ref,
                  z_ref, qb_ref, kext_ref, vext_ref, uext_ref, oa_ref, ob_ref, bias_ref, s_ref, p_ref, vblk_ref, zq_ref):
    i = pl.program_id(1)
    x = x_ref[...]
    xb = x.astype(BF16)
    hist = 2 * SUBLANES

    @pl.when((pl.program_id(0) == 0) & (i == 0))
    def _():
        _project_q(xb, wq_ref, zq_ref)

    @pl.when(i == 0)
    def _():
        kext_ref[0:QB, :] = jnp.zeros((QB, KV_WIDTH), BF16)
        vext_ref[0:QB, :] = jnp.zeros((QB, KV_WIDTH), BF16)
        uext_ref[0:hist, :] = jnp.zeros((hist, POOL_WIDTH), F32)

    @pl.when(i > 0)
    def _():
        kext_ref[0:QB, :] = kext_ref[TM:TM + QB, :]
        vext_ref[0:QB, :] = vext_ref[TM:TM + QB, :]
        uext_ref[0:hist, :] = uext_ref[TM:TM + hist, :]

    _project_rest(xb, wrest_ref, z_ref)

    c = rope_ref[0]
    s1 = rope_ref[1]
    s2 = rope_ref[2]
    for j in range(Q_WIDTH // LANES):
        sl = slice(j * LANES, (j + 1) * LANES)
        qb_ref[:, sl] = _rope(zq_ref[:, sl], c, s1, s2).astype(BF16)
    for j in range(KV_WIDTH // LANES):
        sl = slice(K0 + j * LANES, K0 + (j + 1) * LANES)
        kr = _rope(z_ref[:, sl], c, s1, s2)
        z_ref[:, sl] = kr
        kext_ref[QB:QB + TM, j * LANES:(j + 1) * LANES] = kr.astype(BF16)
    vext_ref[QB:QB + TM, :] = z_ref[:, V0:V0 + KV_WIDTH].astype(BF16)
    uext_ref[hist:hist + TM, :] = z_ref[:, U0:U0 + POOL_WIDTH]
    nk_ref[...] = z_ref[TM - QB:TM, K0:K0 + KV_WIDTH]
    nv_ref[...] = z_ref[TM - QB:TM, V0:V0 + KV_WIDTH]

    rowq = lax.broadcasted_iota(jnp.int32, (QB, 2 * QB), 0)
    colk = lax.broadcasted_iota(jnp.int32, (QB, 2 * QB), 1)
    band = (colk >= rowq) & (colk <= rowq + WINDOW)
    bias_ref[1] = jnp.where(band, 0.0, NEG_INF)
    bias_ref[0] = jnp.where(band & ((colk >= QB) | (i > 0)), 0.0, NEG_INF)
    for sb in range(TM // QB):
        k2 = kext_ref[sb * QB:(sb + 2) * QB, :]
        v2 = vext_ref[sb * QB:(sb + 2) * QB, :]
        qs = jnp.concatenate(
            [qb_ref[sb * QB:(sb + 1) * QB, g * KV_WIDTH:(g + 1) * KV_WIDTH] for g in range(GROUP)], axis=0)
        for kv in range(N_KV):
            kmask = _lane_block_mask((2 * QB, KV_WIDTH), kv)
            s_ref[...] = _dot_nt(qs, jnp.where(kmask, k2, jnp.zeros_like(k2)))
            vblk_ref[kv * 2 * QB:(kv + 1) * 2 * QB, :] = jnp.where(kmask, v2, jnp.zeros_like(v2))
            for c0 in range(0, GROUP * QB, SOFTMAX_ROWS):
                rq = c0 % QB
                sink = sinks_ref[kv * GROUP + c0 // QB]
                s = s_ref[c0:c0 + SOFTMAX_ROWS, :] + bias_ref[min(sb, 1), rq:rq + SOFTMAX_ROWS, :]
                m = jnp.maximum(jnp.max(s, axis=1, keepdims=True), sink)
                p = jnp.exp(s - m)
                den = jnp.sum(p, axis=1, keepdims=True) + jnp.exp(sink - m)
                p_ref[c0:c0 + SOFTMAX_ROWS, kv * 2 * QB:(kv + 1) * 2 * QB] = (p * (1.0 / den)).astype(BF16)
        o = _dot(p_ref[...], vblk_ref[...])
        for g in range(GROUP):
            oa_ref[sb * QB:(sb + 1) * QB, g * KV_WIDTH:(g + 1) * KV_WIDTH] = o[g * QB:(g + 1) * QB].astype(BF16)

    npool_ref[...] = uext_ref[TM:TM + hist, :]
    pos = i * TM + lax.broadcasted_iota(jnp.int32, (TM, 1), 0)
    for g, w in enumerate(POOL_WINDOWS):
        sl = slice(g * POOL_GROUP_DIM, (g + 1) * POOL_GROUP_DIM)
        cur = uext_ref[hist:hist + TM, sl]
        ws = cur
        for j in range(1, w):
            ws = ws + uext_ref[hist - j:hist - j + TM, sl]
        cnt = jnp.minimum(pos + 1, w).astype(F32)
        pooled = ws / cnt - cur
        ob_ref[:, sl] = _dot(pooled.astype(BF16), wmix_ref[g]) * pscale_ref[:, sl]

    cq = (z_ref[:, CQ0:CQ0 + MEM_WIDTH] * ATT_SCALE).astype(BF16)
    mk = mk_ref[...].astype(BF16)
    mv = mv_ref[...].astype(BF16)
    for hh in range(MEM_HEADS):
        hmask = _lane_block_mask((N_MEM, MEM_WIDTH), hh)
        s_ref[0:TM, :] = _dot_nt(cq, jnp.where(hmask, mk, jnp.zeros_like(mk)))
        vblk_ref[hh * N_MEM:(hh + 1) * N_MEM, :] = jnp.where(hmask, mv, jnp.zeros_like(mv))
        for c0 in range(0, TM, SOFTMAX_ROWS):
            s = s_ref[c0:c0 + SOFTMAX_ROWS, :]
            p = jnp.exp(s - jnp.max(s, axis=1, keepdims=True))
            den = jnp.sum(p, axis=1, keepdims=True)
            p_ref[c0:c0 + SOFTMAX_ROWS, hh * N_MEM:(hh + 1) * N_MEM] = (p * (1.0 / den)).astype(BF16)
    oc = _dot(p_ref[0:TM, :], vblk_ref[...])

    h = _merge_ln1(x, oa_ref[...], ob_ref[...], oc, z_ref[:, GZ0:GZ0 + 3 * D_MODEL],
                   wa_ref, wb_ref, wc_ref, wo_ref, g1_ref[...], b1_ref[...])
    h_ref[...] = h
    hb = h.astype(BF16)
    logits = _dot(hb, wr_ref[...]) + br_ref[...]
    _project_q(xn_ref[...].astype(BF16), wq_ref, zq_ref)
    xs_ref[...], route_ref[...], counts_ref[...] = _route_and_sort(logits, hb, xs_ref.shape[0])


def _front(x, rope, sinks, wq, wrest, wa, wb, wc, wo, wmix, pscale, mk, mv, g1, b1, wr, br):
    b, l, _ = x.shape
    nt = l // TM
    cap = _chunk_rows(b * nt)
    hist = 2 * SUBLANES
    tile = lambda w: pl.BlockSpec((None, TM, w), lambda bi, ti: (bi, ti, 0))
    per_b = lambda r, w: pl.BlockSpec((None, r, w), lambda bi, ti: (bi, 0, 0))
    nxt = lambda bi, ti: jnp.minimum(bi * nt + ti + 1, b * nt - 1)
    return pl.pallas_call(
        _front_kernel,
        grid=(b, nt),
        in_specs=[
            pl.BlockSpec(memory_space=pltpu.SMEM),
            tile(D_MODEL),
            pl.BlockSpec((None, TM, D_MODEL), lambda bi, ti: (nxt(bi, ti) // nt, nxt(bi, ti) % nt, 0)),
            pl.BlockSpec((3, TM, LANES), lambda bi, ti: (0, ti, 0)),
            _const_spec((D_MODEL, Q_WIDTH)), _const_spec((D_MODEL, IN_WIDTH - K0)),
            _const_spec((Q_WIDTH, D_MODEL)), _const_spec((POOL_WIDTH, D_MODEL)),
            _const_spec((MEM_WIDTH, D_MODEL)), _const_spec((D_MODEL, D_MODEL)),
            _const_spec((len(POOL_WINDOWS), POOL_GROUP_DIM, POOL_GROUP_DIM)),
            _const_spec((1, POOL_WIDTH)),
            per_b(N_MEM, MEM_WIDTH), per_b(N_MEM, MEM_WIDTH),
            _const_spec((1, D_MODEL)), _const_spec((1, D_MODEL)),
            _const_spec((D_MODEL, LANES)), _const_spec((1, LANES)),
        ],
        out_specs=[
            tile(D_MODEL),
            pl.BlockSpec((cap, D_MODEL), lambda bi, ti: (bi * nt + ti, 0)),
            tile(LANES),
            pl.BlockSpec((None, None, LANES, LANES), lambda bi, ti: (bi, ti, 0, 0)),
            per_b(QB, KV_WIDTH), per_b(QB, KV_WIDTH), per_b(hist, POOL_WIDTH),
        ],
        out_shape=[
            jax.ShapeDtypeStruct((b, l, D_MODEL), F32),
            jax.ShapeDtypeStruct((b * nt * cap, D_MODEL), BF16),
            jax.ShapeDtypeStruct((b, l, LANES), F32),
            jax.ShapeDtypeStruct((b, nt, LANES, LANES), F32),
            jax.ShapeDtypeStruct((b, QB, KV_WIDTH), F32),
            jax.ShapeDtypeStruct((b, QB, KV_WIDTH), F32),
            jax.ShapeDtypeStruct((b, hist, POOL_WIDTH), F32),
        ],
        scratch_shapes=[
            pltpu.VMEM((TM, IN_WIDTH), F32),
            pltpu.VMEM((TM, Q_WIDTH), BF16),
            pltpu.VMEM((QB + TM, KV_WIDTH), BF16),
            pltpu.VMEM((QB + TM, KV_WIDTH), BF16),
            pltpu.VMEM((hist + TM, POOL_WIDTH), F32),
            pltpu.VMEM((TM, Q_WIDTH), BF16),
            pltpu.VMEM((TM, POOL_WIDTH), F32),
            pltpu.VMEM((2, QB, 2 * QB), F32),
            pltpu.VMEM((GROUP * QB, 2 * QB), F32),
            pltpu.VMEM((GROUP * QB, N_KV * 2 * QB), BF16),
            pltpu.VMEM((N_KV * 2 * QB, KV_WIDTH), BF16),
            pltpu.VMEM((TM, Q_WIDTH), F32),
        ],
        compiler_params=pltpu.CompilerParams(
            dimension_semantics=("arbitrary", "arbitrary"), vmem_limit_bytes=VMEM_LIMIT),
        name="front_prompt",
    )(sinks, x, x, rope, wq, wrest, wa, wb, wc, wo, wmix, pscale, mk, mv, g1, b1, wr, br)


def _expert_mlp(xb, wg, wu, wd):
    a = _dot(xb, wg)
    hid = (a * jax.nn.sigmoid(a)) * _dot(xb, wu)
    return _dot(hid.astype(BF16), wd)


def _gemm_kernel(src_ref, dst_ref, te_ref, act_ref, xs_ref, wg_ref, wu_ref, wd_ref, ys_ref,
                 xbuf, obuf, wgb, wub, wdb, prime, sem_in, sem_out):
    i = pl.program_id(0)
    n = pl.num_programs(0)
    slot = i % 2
    in_slot = i % IN_SLOTS

    def start_in(tile, slot):
        for j in range(PIECES_PER_TILE):
            row0 = pl.multiple_of(src_ref[tile * PIECES_PER_TILE + j], PIECE)
            pltpu.make_async_copy(xs_ref.at[pl.ds(row0, PIECE)], xbuf.at[slot, pl.ds(j * PIECE, PIECE)],
                                  sem_in.at[slot]).start()

    def start_out(tile, slot):
        for j in range(PIECES_PER_TILE):
            row0 = pl.multiple_of(dst_ref[tile * PIECES_PER_TILE + j], PIECE)
            pltpu.make_async_copy(obuf.at[slot, pl.ds(j * PIECE, PIECE)], ys_ref.at[pl.ds(row0, PIECE)],
                                  sem_out.at[slot]).start()

    def wait_in(slot):
        for j in range(PIECES_PER_TILE):
            pltpu.make_async_copy(xs_ref.at[pl.ds(0, PIECE)], xbuf.at[slot, pl.ds(j * PIECE, PIECE)],
                                  sem_in.at[slot]).wait()

    def wait_out(slot):
        for j in range(PIECES_PER_TILE):
            pltpu.make_async_copy(obuf.at[slot, pl.ds(j * PIECE, PIECE)], ys_ref.at[pl.ds(0, PIECE)],
                                  sem_out.at[slot]).wait()

    active = act_ref[i] > 0
    ahead = IN_SLOTS - 1
    prefetched = (i < ahead) | (act_ref[jnp.maximum(i - ahead, 0)] > 0)
    out_pending = (i < 2) | (act_ref[jnp.maximum(i - 2, 0)] > 0)

    @pl.when(i == 0)
    def _():
        prime[0] = jnp.zeros((PIECE, D_MODEL), BF16)
        for s in range(2):
            for j in range(PIECES_PER_TILE):
                pltpu.make_async_copy(prime.at[0], prime.at[1 + s * PIECES_PER_TILE + j], sem_out.at[s]).start()
        for t in range(ahead):
            start_in(t, t)

    prev_active = (i >= 1) & (act_ref[jnp.maximum(i - 1, 0)] > 0)

    def active_step(after_first):
        @pl.when((i == 0) | (te_ref[i] != te_ref[jnp.maximum(i - 1, 0)]))
        def _():
            wgb[...] = wg_ref[...].astype(BF16)
            wub[...] = wu_ref[...].astype(BF16)
            wdb[...] = wd_ref[...].astype(BF16)

        wait_in(in_slot)
        wait_out(slot)
        out = _expert_mlp(xbuf[in_slot], wgb[...], wub[...], wdb[...])
        start_in(i + ahead, (i + ahead) % IN_SLOTS)
        if after_first:
            start_out(i - 1, 1 - slot)
        obuf[slot] = out.astype(BF16)

    @pl.when(active & (i == 0))
    def _():
        active_step(False)

    @pl.when(active & (i > 0))
    def _():
        active_step(True)

    @pl.when(jnp.logical_not(active))
    def _():
        @pl.when(prev_active)
        def _():
            start_out(i - 1, 1 - slot)

        @pl.when(prefetched)
        def _():
            wait_in(in_slot)

        @pl.when(out_pending)
        def _():
            wait_out(slot)

    @pl.when(i == n - 1)
    def _():
        for t in range(ahead):
            @pl.when(act_ref[jnp.maximum(i - t, 0)] > 0)
            def _(t=t):
                wait_in((i - t + ahead) % IN_SLOTS)

        @pl.when(active)
        def _():
            start_out(i, slot)
            wait_out(slot)

        @pl.when(prev_active)
        def _():
            wait_out(1 - slot)


def _grouped_gemm(piece_src, piece_dst, tile_expert, tile_active, xs, wg, wu, wd):
    n_tiles = tile_expert.shape[0]
    assert n_tiles >= 2
    wspec = lambda r, c: pl.BlockSpec((None, r, c), lambda i, src, dst, te, act: (te[i], 0, 0))
    return pl.pallas_call(
        _gemm_kernel,
        grid_spec=pltpu.PrefetchScalarGridSpec(
            num_scalar_prefetch=4,
            grid=(n_tiles,),
            in_specs=[pl.BlockSpec(memory_space=pl.ANY),
                      wspec(D_MODEL, D_EXPERT), wspec(D_MODEL, D_EXPERT), wspec(D_EXPERT, D_MODEL)],
            out_specs=pl.BlockSpec(memory_space=pl.ANY),
            scratch_shapes=[pltpu.VMEM((IN_SLOTS, TG, D_MODEL), BF16), pltpu.VMEM((2, TG, D_MODEL), BF16),
                            pltpu.VMEM((D_MODEL, D_EXPERT), BF16), pltpu.VMEM((D_MODEL, D_EXPERT), BF16),
                            pltpu.VMEM((D_EXPERT, D_MODEL), BF16),
                            pltpu.VMEM((1 + 2 * PIECES_PER_TILE, PIECE, D_MODEL), BF16),
                            pltpu.SemaphoreType.DMA((IN_SLOTS,)), pltpu.SemaphoreType.DMA((2,))],
        ),
        out_shape=jax.ShapeDtypeStruct(xs.shape, xs.dtype),
        input_output_aliases={4: 0},
        compiler_params=pltpu.CompilerParams(dimension_semantics=("arbitrary",), vmem_limit_bytes=VMEM_LIMIT),
        name="moe_grouped_gemm",
    )(piece_src, piece_dst, tile_expert, tile_active, xs, wg, wu, wd)


def _combine_kernel(ys_ref, h_ref, route_ref, g2_ref, b2_ref, y_ref):
    chunks = h_ref.shape[0] // TM
    cap = ys_ref.shape[0] // chunks
    slot = lax.broadcasted_iota(jnp.int32, (TM, cap), 1).astype(F32)
    for c in range(chunks):
        rows = slice(c * TM, (c + 1) * TM)
        route = route_ref[rows, :]
        sel = jnp.concatenate([jnp.where(slot == route[:, 2:3], 1.0, 0.0).astype(BF16),
                               jnp.where(slot == route[:, 3:4], 1.0, 0.0).astype(BF16)], axis=0)
        picked = _dot(sel, ys_ref[c * cap:(c + 1) * cap, :])
        f = route[:, 0:1] * picked[0:TM] + route[:, 1:2] * picked[TM:2 * TM]
        y_ref[rows, :] = _layer_norm(ALPHA * h_ref[rows, :] + f, g2_ref[...], b2_ref[...])


def _combine(ys, h, route, g2, b2):
    t = h.shape[0]
    cap = ys.shape[0] // (t // TM)
    per_step = next(k for k in (4, 2, 1) if (t // TM) % k == 0)
    rows = per_step * TM
    return pl.pallas_call(
        _combine_kernel,
        grid=(t // rows,),
        in_specs=[
            pl.BlockSpec((per_step * cap, D_MODEL), lambda i: (i, 0)),
            pl.BlockSpec((rows, D_MODEL), lambda i: (i, 0)),
            pl.BlockSpec((rows, LANES), lambda i: (i, 0)),
            pl.BlockSpec((1, D_MODEL), lambda i: (0, 0)),
            pl.BlockSpec((1, D_MODEL), lambda i: (0, 0)),
        ],
        out_specs=pl.BlockSpec((rows, D_MODEL), lambda i: (i, 0)),
        out_shape=jax.ShapeDtypeStruct((t, D_MODEL), F32),
        compiler_params=pltpu.CompilerParams(dimension_semantics=("arbitrary",)),
        name="moe_combine",
    )(ys, h, route, g2, b2)


def _select(table, idx):
    hot = idx[:, None] == jnp.arange(table.shape[0], dtype=jnp.int32)[None, :]
    return jnp.sum(jnp.where(hot[:, :, None], table[None, :, :], 0), axis=1)


def _piece_tables(counts):
    n_chunks = counts.shape[0]
    n_tiles = -(-(n_chunks * MAX_CHUNK_PIECES + N_EXPERTS * (PIECES_PER_TILE - 1)) // PIECES_PER_TILE)
    npc = (counts + (PIECE - 1)) // PIECE
    first = (jnp.cumsum(npc, axis=1) - npc).T
    npc_t = npc.T
    cum = jnp.cumsum(npc_t, axis=1)
    per_expert = cum[:, -1]
    tiles_e = (per_expert + (PIECES_PER_TILE - 1)) // PIECES_PER_TILE
    tile_end = jnp.cumsum(tiles_e)
    tile_idx = jnp.arange(n_tiles, dtype=jnp.int32)
    expert_of = lambda i: jnp.minimum(jnp.sum(i[:, None] >= tile_end[None, :], axis=1), N_EXPERTS - 1).astype(jnp.int32)
    active = tile_idx < tile_end[-1]
    tile_expert = jnp.where(active, expert_of(tile_idx), expert_of(tile_end[-1:] - 1))
    meta = jnp.stack([tile_end - tiles_e, per_expert], axis=1)
    meta_t = _select(meta, tile_expert)
    k = (tile_idx - meta_t[:, 0])[:, None] * PIECES_PER_TILE + jnp.arange(PIECES_PER_TILE, dtype=jnp.int32)[None, :]
    valid = active[:, None] & (k < meta_t[:, 1:2])
    cum_t = _select(cum, tile_expert)
    chunk = jnp.minimum(jnp.sum(k[:, :, None] >= cum_t[:, None, :], axis=2), n_chunks - 1).astype(jnp.int32)
    at_chunk = chunk[:, :, None] == jnp.arange(n_chunks, dtype=jnp.int32)[None, None, :]
    pick = lambda tab: jnp.sum(jnp.where(at_chunk, _select(tab, tile_expert)[:, None, :], 0), axis=2)
    piece = pick(first) + k - pick(cum - npc_t)
    cap = _chunk_rows(n_chunks)
    rows = chunk * cap + piece * PIECE
    d = ((tile_idx % 2)[:, None] * (PIECES_PER_TILE - 1)
         + jnp.maximum(jnp.arange(PIECES_PER_TILE, dtype=jnp.int32)[None, :] - 1, 0))
    spare_row = lambda d: (d % n_chunks) * cap + (MAX_CHUNK_PIECES + d // n_chunks) * PIECE
    spare = spare_row(d)
    zero_piece = spare_row(N_SPARE - 1)
    extra = jnp.full(((IN_SLOTS - 1) * PIECES_PER_TILE,), zero_piece, jnp.int32)
    src = jnp.concatenate([jnp.where(valid, rows, zero_piece).astype(jnp.int32).reshape(-1), extra])
    dst = jnp.concatenate([jnp.where(valid, rows, spare).astype(jnp.int32).reshape(-1), extra])
    return src, dst, tile_expert.astype(jnp.int32), active.astype(jnp.int32)


def _sample_attn_kernel(x_ref, rope_ref, wq_ref, wrest_ref, sink_ref, ck_ref, cv_ref, cmk_ref, cmv_ref,
                        z_ref, oa_ref, oc_ref, nk_ref, nv_ref, knew_t, vnew_t):
    j = pl.program_id(0)
    db = x_ref.shape[0]

    @pl.when(j == 0)
    def _():
        xb = x_ref[...].astype(BF16)
        _project_q(xb, wq_ref, z_ref)
        _project_rest(xb, wrest_ref, z_ref)
        c = rope_ref[0]
        s1 = rope_ref[1]
        s2 = rope_ref[2]
        for jj in range((Q_WIDTH + KV_WIDTH) // LANES):
            sl = slice(jj * LANES, (jj + 1) * LANES)
            z_ref[:, sl] = _rope(z_ref[:, sl], c, s1, s2)
        knew_t[...] = z_ref[:, K0:K0 + KV_WIDTH].T
        vnew_t[...] = z_ref[:, V0:V0 + KV_WIDTH].T

    r0 = pl.multiple_of(j * SB, SB)
    zq = z_ref[pl.ds(r0, SB), Q0:Q0 + Q_WIDTH]
    zk = z_ref[pl.ds(r0, SB), K0:K0 + KV_WIDTH]
    zv = z_ref[pl.ds(r0, SB), V0:V0 + KV_WIDTH]
    zc = z_ref[pl.ds(r0, SB), CQ0:CQ0 + MEM_WIDTH] * ATT_SCALE
    sink = sink_ref[:, 0:1]
    row_kv = lax.broadcasted_iota(jnp.int32, (N_HEADS, KV_WIDTH), 0) & (N_KV - 1)
    lane_kv = lax.broadcasted_iota(jnp.int32, (N_HEADS, KV_WIDTH), 1) // HEAD_DIM
    own = row_kv == lane_kv
    row_c = lax.broadcasted_iota(jnp.int32, (N_HEADS, MEM_WIDTH), 0)
    lane_c = lax.broadcasted_iota(jnp.int32, (N_HEADS, MEM_WIDTH), 1) // HEAD_DIM
    own_c = row_c == lane_c
    last_pos = lax.broadcasted_iota(jnp.int32, (KV_WIDTH, WINDOW), 1) == WINDOW - 1
    seq_lane = lax.broadcasted_iota(jnp.int32, (KV_WIDTH, db), 1)

    seqs = range(SB)
    qblk, cblk, s, sc = [], [], [], []
    for b in seqs:
        q4 = jnp.concatenate(
            [jnp.broadcast_to(zq[b:b + 1, g * KV_WIDTH:(g + 1) * KV_WIDTH], (N_KV, KV_WIDTH)) for g in range(GROUP)],
            axis=0)
        qblk.append(jnp.where(own, q4, 0.0).astype(BF16))
        cblk.append(jnp.where(own_c, jnp.broadcast_to(zc[b:b + 1, :], (N_HEADS, MEM_WIDTH)), 0.0).astype(BF16))
        s.append(_dot(qblk[b], ck_ref[b].astype(BF16)))
        sc.append(_dot(cblk[b], cmk_ref[b].astype(BF16)))

    pn, p_new, pc = [], [], []
    for b in seqs:
        s_new = jnp.sum(qblk[b].astype(F32) * zk[b:b + 1, :].astype(BF16).astype(F32), axis=1, keepdims=True)
        m = jnp.maximum(jnp.maximum(jnp.max(s[b], axis=1, keepdims=True), s_new), sink)
        p = jnp.exp(s[b] - m)
        e_new = jnp.exp(s_new - m)
        inv = 1.0 / (jnp.sum(p, axis=1, keepdims=True) + e_new + jnp.exp(sink - m))
        pn.append((p * inv).astype(BF16))
        p_new.append((e_new * inv).astype(BF16).astype(F32))
        e = jnp.exp(sc[b] - jnp.max(sc[b], axis=1, keepdims=True))
        pc.append((e * (1.0 / jnp.sum(e, axis=1, keepdims=True))).astype(BF16))

    oa_rows = [[] for _ in range(GROUP)]
    oc_rows = []
    for b in seqs:
        vc = cv_ref[b]
        o = _dot_nt(pn[b], vc.astype(BF16)) + p_new[b] * zv[b:b + 1, :].astype(BF16).astype(F32)
        o = jnp.where(own, o, 0.0)
        for g in range(GROUP):
            oa_rows[g].append(jnp.sum(o[g * N_KV:(g + 1) * N_KV], axis=0, keepdims=True))
        ocb = jnp.where(own_c, _dot_nt(pc[b], cmv_ref[b].astype(BF16)), 0.0)
        oc_rows.append(jnp.sum(ocb, axis=0, keepdims=True))
        here = seq_lane == j * SB + b
        knew_col = jnp.sum(jnp.where(here, knew_t[...], 0.0), axis=1, keepdims=True)
        vnew_col = jnp.sum(jnp.where(here, vnew_t[...], 0.0), axis=1, keepdims=True)
        nk_ref[b] = jnp.where(last_pos, knew_col, pltpu.roll(ck_ref[b], WINDOW - 1, 1))
        nv_ref[b] = jnp.where(last_pos, vnew_col, pltpu.roll(vc, WINDOW - 1, 1))
    for g in range(GROUP):
        oa_ref[pl.ds(r0, SB), g * KV_WIDTH:(g + 1) * KV_WIDTH] = jnp.concatenate(oa_rows[g], axis=0)
    oc_ref[pl.ds(r0, SB), :] = jnp.concatenate(oc_rows, axis=0)


def _sample_attn(x, rope, wq, wrest, sink_gk, ck, cv, cmk, cmv):
    db = x.shape[0]
    blk = lambda r: pl.BlockSpec((SB, KV_WIDTH, r), lambda j: (j, 0, 0))
    full = lambda w: pl.BlockSpec((db, w), lambda j: (0, 0))
    return pl.pallas_call(
        _sample_attn_kernel,
        grid=(db // SB,),
        in_specs=[
            full(D_MODEL), _const_spec((3, 1, LANES)),
            _const_spec((D_MODEL, Q_WIDTH)), _const_spec((D_MODEL, IN_WIDTH - K0)),
            _const_spec((N_HEADS, LANES)),
            blk(WINDOW), blk(WINDOW), blk(N_MEM), blk(N_MEM),
        ],
        out_specs=[full(IN_WIDTH), full(Q_WIDTH), full(MEM_WIDTH), blk(WINDOW), blk(WINDOW)],
        out_shape=[
            jax.ShapeDtypeStruct((db, IN_WIDTH), F32),
            jax.ShapeDtypeStruct((db, Q_WIDTH), F32),
            jax.ShapeDtypeStruct((db, MEM_WIDTH), F32),
            jax.ShapeDtypeStruct((db, KV_WIDTH, WINDOW), F32),
            jax.ShapeDtypeStruct((db, KV_WIDTH, WINDOW), F32),
        ],
        scratch_shapes=[pltpu.VMEM((KV_WIDTH, db), F32), pltpu.VMEM((KV_WIDTH, db), F32)],
        compiler_params=pltpu.CompilerParams(dimension_semantics=("arbitrary",), vmem_limit_bytes=VMEM_LIMIT),
        name="sample_attn",
    )(x, rope, wq, wrest, sink_gk, ck, cv, cmk, cmv)


def _sample_tail_kernel(x_ref, z_ref, oa_ref, oc_ref, st_ref, wa_ref, wb_ref, wc_ref, wo_ref, wmix_ref,
                        pscale_ref, g1_ref, b1_ref, wr_ref, br_ref, wg_ref, wu_ref, wd_ref, g2_ref, b2_ref,
                        y_ref, npool_ref, h_sc, comb_sc, acc_sc):
    e = pl.program_id(0)

    @pl.when(e == 0)
    def _():
        u = z_ref[:, U0:U0 + POOL_WIDTH]
        npool_ref[0:POOL_STATE - 1] = st_ref[1:POOL_STATE]
        npool_ref[POOL_STATE - 1] = u
        obs = []
        for g, w in enumerate(POOL_WINDOWS):
            sl = slice(g * POOL_GROUP_DIM, (g + 1) * POOL_GROUP_DIM)
            cur = u[:, sl]
            ws = cur
            for jj in range(1, w):
                ws = ws + st_ref[POOL_STATE - jj, :, sl]
            cnt = float(min(PAST_LEN + 1, w))
            pooled = ws / cnt - cur
            obs.append(_dot(pooled.astype(BF16), wmix_ref[g]) * pscale_ref[:, sl])
        ob = jnp.concatenate(obs, axis=1)
        h = _merge_ln1(x_ref[...], oa_ref[...].astype(BF16), ob, oc_ref[...], z_ref[:, GZ0:GZ0 + 3 * D_MODEL],
                       wa_ref, wb_ref, wc_ref, wo_ref, g1_ref[...], b1_ref[...])
        h_sc[...] = h
        logits = _dot(h.astype(BF16), wr_ref[...]) + br_ref[...]
        hot1, hot2, w1, w2 = _route(logits)
        comb_sc[...] = jnp.where(hot1, w1, 0.0) + jnp.where(hot2, w2, 0.0)
        acc_sc[...] = jnp.zeros_like(acc_sc)

    out = _expert_mlp(h_sc[...].astype(BF16), wg_ref[...].astype(BF16), wu_ref[...].astype(BF16),
                      wd_ref[...].astype(BF16))
    lane = lax.broadcasted_iota(jnp.int32, comb_sc.shape, 1)
    ce = jnp.sum(jnp.where(lane == e, comb_sc[...], 0.0), axis=1, keepdims=True)
    acc_sc[...] += ce * out

    @pl.when(e == pl.num_programs(0) - 1)
    def _():
        y_ref[...] = _layer_norm(ALPHA * h_sc[...] + acc_sc[...], g2_ref[...], b2_ref[...])


def _sample_tail(x, z, oa, oc, state, wa, wb, wc, wo, wmix, pscale, g1, b1, wr, br, wg, wu, wd, g2, b2):
    db = x.shape[0]
    full = lambda w: pl.BlockSpec((db, w), lambda e: (0, 0))
    vec = lambda w: pl.BlockSpec((1, w), lambda e: (0, 0))
    hist = pl.BlockSpec((POOL_STATE, db, POOL_WIDTH), lambda e: (0, 0, 0))
    return pl.pallas_call(
        _sample_tail_kernel,
        grid=(N_EXPERTS,),
        in_specs=[
            full(D_MODEL), full(IN_WIDTH), full(Q_WIDTH), full(MEM_WIDTH), hist,
            _const_spec((Q_WIDTH, D_MODEL)), _const_spec((POOL_WIDTH, D_MODEL)),
            _const_spec((MEM_WIDTH, D_MODEL)), _const_spec((D_MODEL, D_MODEL)),
            _const_spec((len(POOL_WINDOWS), POOL_GROUP_DIM, POOL_GROUP_DIM)),
            vec(POOL_WIDTH), vec(D_MODEL), vec(D_MODEL),
            _const_spec((D_MODEL, LANES)), vec(LANES),
            pl.BlockSpec((None, D_MODEL, D_EXPERT), lambda e: (e, 0, 0)),
            pl.BlockSpec((None, D_MODEL, D_EXPERT), lambda e: (e, 0, 0)),
            pl.BlockSpec((None, D_EXPERT, D_MODEL), lambda e: (e, 0, 0)),
            vec(D_MODEL), vec(D_MODEL),
        ],
        out_specs=[full(D_MODEL), hist],
        out_shape=[jax.ShapeDtypeStruct((db, D_MODEL), F32),
                   jax.ShapeDtypeStruct((POOL_STATE, db, POOL_WIDTH), F32)],
        scratch_shapes=[pltpu.VMEM((db, D_MODEL), F32), pltpu.VMEM((db, LANES), F32),
                        pltpu.VMEM((db, D_MODEL), F32)],
        compiler_params=pltpu.CompilerParams(dimension_semantics=("arbitrary",), vmem_limit_bytes=VMEM_LIMIT),
        name="sample_tail",
    )(x, z, oa, oc, state, wa, wb, wc, wo, wmix, pscale, g1, b1, wr, br, wg, wu, wd, g2, b2)


def _rope_tables(pos):
    half = ROPE_DIM // 2
    inv = jnp.power(ROPE_THETA, -jnp.arange(half, dtype=F32) * (2.0 / ROPE_DIM))
    ang = pos.astype(F32)[:, None] * jnp.tile(inv, LANES // half)[None, :]
    off = lax.broadcasted_iota(jnp.int32, (1, LANES), 1) % HEAD_DIM
    cos = jnp.cos(ang)
    sin = jnp.sin(ang)
    c = jnp.where(off < ROPE_DIM, cos, 1.0)
    s1 = jnp.where((off >= half) & (off < ROPE_DIM), sin, 0.0)
    s2 = jnp.where(off < half, -sin, 0.0)
    return jnp.stack([c, s1, s2])


def _q_heads_group_major(w, axis):
    if axis == 1:
        n = w.shape[0]
        return w.reshape(n, N_KV, GROUP, HEAD_DIM).transpose(0, 2, 1, 3).reshape(n, Q_WIDTH)
    n = w.shape[1]
    return w.reshape(N_KV, GROUP, HEAD_DIM, n).transpose(1, 0, 2, 3).reshape(Q_WIDTH, n)


def kernel(x_prompt, x_sample, cache_win_k, cache_win_v, state_pool, cache_mem_k, cache_mem_v, mem_prompt, w_in, sinks, w_pool_mix, pool_scale, w_mem_k, w_mem_v, w_branch_a, w_branch_b, w_branch_c, w_out, ln1_g, ln1_b, w_group, b_group, w_router, b_router, w_gate, w_up, w_down, ln2_g, ln2_b):
    assert w_in.shape[0] == DEPTH == 1
    b, l, _ = x_prompt.shape
    db, ds, _ = x_sample.shape
    assert ds == 1 and l % TM == 0 and db % SB == 0
    assert cache_win_k.shape[2] == WINDOW
    t = b * l

    win = w_in[0]
    wq = (_q_heads_group_major(win[:, Q0:Q0 + Q_WIDTH], 1) * ATT_SCALE).astype(BF16)
    wrest = win[:, K0:].astype(BF16)
    wa = _q_heads_group_major(w_branch_a[0], 0).astype(BF16)
    wb = w_branch_b[0].astype(BF16)
    wc = w_branch_c[0].astype(BF16)
    wo = w_out[0].astype(BF16)
    wmix = w_pool_mix[0].astype(BF16)
    pscale = pool_scale[0].reshape(1, POOL_WIDTH)
    g1 = ln1_g[0].reshape(1, D_MODEL)
    b1 = ln1_b[0].reshape(1, D_MODEL)
    g2 = ln2_g[0].reshape(1, D_MODEL)
    b2 = ln2_b[0].reshape(1, D_MODEL)
    wr = jnp.concatenate([w_group[0], w_router[0].reshape(D_MODEL, N_EXPERTS)], axis=1)
    wr = jnp.pad(wr, ((0, 0), (0, LANES - wr.shape[1]))).astype(BF16)
    br = jnp.pad(jnp.concatenate([b_group[0], b_router[0].reshape(N_EXPERTS)]), (0, LANES - N_EXPERT_GROUPS - N_EXPERTS))
    br = br.reshape(1, LANES).astype(F32)
    wg = w_gate[0]
    wu = w_up[0]
    wd = w_down[0]
    sink = sinks[0].astype(F32)
    sink_gk = jnp.broadcast_to(sink.reshape(N_KV, GROUP).T.reshape(N_HEADS, 1), (N_HEADS, LANES))

    mk, mv = _mem_project(mem_prompt, w_mem_k[0].astype(BF16), w_mem_v[0].astype(BF16))
    rope_p = _rope_tables(jnp.arange(l, dtype=jnp.int32))
    h, xs, route, counts, nk_p, nv_p, npool_p = _front(
        x_prompt, rope_p, sink, wq, wrest, wa, wb, wc, wo, wmix, pscale, mk, mv, g1, b1, wr, br)
    piece_src, piece_dst, tile_expert, tile_active = _piece_tables(
        counts.reshape(-1, LANES, LANES)[:, :N_EXPERTS, 0].astype(jnp.int32))
    ys = _grouped_gemm(piece_src, piece_dst, tile_expert, tile_active, xs, wg, wu, wd)
    y_p = _combine(ys, h.reshape(t, D_MODEL), route.reshape(t, LANES), g2, b2).reshape(b, l, D_MODEL)

    rope_s = _rope_tables(jnp.full((1,), PAST_LEN, jnp.int32))
    xs = x_sample.reshape(db, D_MODEL)
    feat_major = lambda c: jnp.transpose(c[0], (0, 2, 3, 1)).reshape(db, KV_WIDTH, c.shape[2])
    from_feat_major = lambda a: jnp.transpose(a.reshape(db, N_KV, HEAD_DIM, a.shape[2]), (0, 3, 1, 2))[None]
    z_s, oa_s, oc_s, nk_s, nv_s = _sample_attn(xs, rope_s, wq, wrest, sink_gk, feat_major(cache_win_k),
                                               feat_major(cache_win_v), feat_major(cache_mem_k), feat_major(cache_mem_v))
    state = jnp.transpose(state_pool[0], (1, 0, 2))
    y_s, npool_s = _sample_tail(xs, z_s, oa_s, oc_s, state, wa, wb, wc, wo, wmix, pscale, g1, b1, wr, br,
                                wg, wu, wd, g2, b2)

    kv5 = lambda a, n, w: a.reshape(1, n, w, N_KV, HEAD_DIM)
    return (y_p, y_s.reshape(db, 1, D_MODEL),
            kv5(nk_p, b, QB), kv5(nv_p, b, QB),
            npool_p[:, 2 * SUBLANES - POOL_STATE:, :][None],
            kv5(mk, b, N_MEM), kv5(mv, b, N_MEM),
            from_feat_major(nk_s), from_feat_major(nv_s),
            jnp.transpose(npool_s, (1, 0, 2))[None])
```

```python
import jax
import jax.numpy as jnp
from jax import lax
from jax.experimental import pallas as pl
from jax.experimental.pallas import tpu as pltpu

D_MODEL = 1024
N_HEADS = 16
HEAD_DIM = 64
N_KV = 4
GROUP = N_HEADS // N_KV
WINDOW = 128
ROPE_THETA = 500000.0
ROPE_DIM = HEAD_DIM // 4
Q_WIDTH = N_HEADS * HEAD_DIM
KV_WIDTH = N_KV * HEAD_DIM
POOL_WINDOWS = (2, 4, 8, 16)
POOL_WIDTH = D_MODEL // 2
POOL_GROUP_DIM = POOL_WIDTH // len(POOL_WINDOWS)
POOL_STATE = max(POOL_WINDOWS) - 1
N_MEM = 256
MEM_HEADS = 4
MEM_WIDTH = MEM_HEADS * HEAD_DIM
N_EXPERT_GROUPS = 4
EXPERTS_PER_GROUP = 4
N_EXPERTS = N_EXPERT_GROUPS * EXPERTS_PER_GROUP
D_EXPERT = 512
PAST_LEN = 16384
DEPTH = 1
ALPHA = (2.0 * DEPTH) ** 0.25
LN_EPS = 1e-5

Q0 = 0
K0 = Q0 + Q_WIDTH
V0 = K0 + KV_WIDTH
U0 = V0 + KV_WIDTH
CQ0 = U0 + POOL_WIDTH
GZ0 = CQ0 + MEM_WIDTH
IN_WIDTH = GZ0 + 3 * D_MODEL

LANES = 128
SUBLANES = 8
VMEM_LIMIT = 56 * 1024 * 1024

TM = 256
QB = WINDOW
PIECE = 16
PIECES_PER_TILE = 32
TG = PIECE * PIECES_PER_TILE
MAX_CHUNK_PIECES = 2 * TM // PIECE + N_EXPERTS - 1
N_SPARE = 2 * (PIECES_PER_TILE - 1) + 1
IN_SLOTS = 3


def _chunk_rows(n_chunks):
    spare = -(-N_SPARE // n_chunks)
    return -(-(MAX_CHUNK_PIECES + spare) * PIECE // LANES) * LANES
SB = 8
IN_CHUNK = 768
Q_CHUNK = 512
ATT_SCALE = HEAD_DIM ** -0.5
SOFTMAX_ROWS = 64
assert N_MEM == 2 * QB and MEM_HEADS == N_KV and MEM_WIDTH == KV_WIDTH and TM <= GROUP * QB

BF16 = jnp.bfloat16
F32 = jnp.float32
NEG_INF = float("-inf")


def _const_spec(shape):
    nd = len(shape)
    return pl.BlockSpec(shape, lambda *_: (0,) * nd, pipeline_mode=pl.Buffered(1))


def _layer_norm(x, g, b):
    mu = jnp.mean(x, axis=-1, keepdims=True)
    xc = x - mu
    var = jnp.mean(xc * xc, axis=-1, keepdims=True)
    return xc * lax.rsqrt(var + LN_EPS) * g + b


def _dot(a, b):
    return jnp.dot(a, b, preferred_element_type=F32)


def _dot_nt(a, b):
    return lax.dot_general(a, b, (((1,), (1,)), ((), ())), preferred_element_type=F32)


def _lane_block_mask(shape, block, width=HEAD_DIM):
    lane = lax.broadcasted_iota(jnp.int32, shape, len(shape) - 1)
    return (lane >= block * width) & (lane < (block + 1) * width)


def _rope(x, c, s1, s2):
    half = ROPE_DIM // 2
    return x * c + pltpu.roll(x, half, 1) * s1 + pltpu.roll(x, LANES - half, 1) * s2


def _route(logits):
    rows = logits.shape[0]
    lane = lax.broadcasted_iota(jnp.int32, (rows, LANES), 1)
    lanef = lane.astype(F32)
    big = float(LANES)
    is_g = lane < N_EXPERT_GROUPS
    glog = jnp.where(is_g, logits, NEG_INF)
    gmax = jnp.max(glog, axis=1, keepdims=True)
    gsum = jnp.sum(jnp.where(is_g, jnp.exp(glog - gmax), 0.0), axis=1, keepdims=True)
    gp = 1.0 / gsum
    gidx = jnp.min(jnp.where(glog == gmax, lanef, big), axis=1, keepdims=True).astype(jnp.int32)
    lo = N_EXPERT_GROUPS + gidx * EXPERTS_PER_GROUP
    in_grp = (lane >= lo) & (lane < lo + EXPERTS_PER_GROUP)
    el = jnp.where(in_grp, logits, NEG_INF)
    v1 = jnp.max(el, axis=1, keepdims=True)
    i1 = jnp.min(jnp.where(el == v1, lanef, big), axis=1, keepdims=True).astype(jnp.int32)
    el2 = jnp.where(lane == i1, NEG_INF, el)
    v2 = jnp.max(el2, axis=1, keepdims=True)
    i2 = jnp.min(jnp.where(el2 == v2, lanef, big), axis=1, keepdims=True).astype(jnp.int32)
    e21 = jnp.exp(v2 - v1)
    inv = 1.0 / (1.0 + e21)
    w1 = inv * gp
    w2 = e21 * inv * gp
    e1 = i1 - N_EXPERT_GROUPS
    e2 = i2 - N_EXPERT_GROUPS
    return lane == e1, lane == e2, w1, w2


def _route_and_sort(logits, hb, cap):
    rows = hb.shape[0]
    lt = logits.T
    row = lax.broadcasted_iota(jnp.int32, (LANES, rows), 0)
    rowf = row.astype(F32)
    big = float(LANES)
    is_g = row < N_EXPERT_GROUPS
    glog = jnp.where(is_g, lt, NEG_INF)
    gmax = jnp.max(glog, axis=0, keepdims=True)
    gp = 1.0 / jnp.sum(jnp.where(is_g, jnp.exp(glog - gmax), 0.0), axis=0, keepdims=True)
    gidx = jnp.min(jnp.where(glog == gmax, rowf, big), axis=0, keepdims=True).astype(jnp.int32)
    lo = N_EXPERT_GROUPS + gidx * EXPERTS_PER_GROUP
    el = jnp.where((row >= lo) & (row < lo + EXPERTS_PER_GROUP), lt, NEG_INF)
    v1 = jnp.max(el, axis=0, keepdims=True)
    i1 = jnp.min(jnp.where(el == v1, rowf, big), axis=0, keepdims=True).astype(jnp.int32)
    el2 = jnp.where(row == i1, NEG_INF, el)
    v2 = jnp.max(el2, axis=0, keepdims=True)
    i2 = jnp.min(jnp.where(el2 == v2, rowf, big), axis=0, keepdims=True).astype(jnp.int32)
    e21 = jnp.exp(v2 - v1)
    inv = 1.0 / (1.0 + e21)
    w1 = inv * gp
    w2 = e21 * inv * gp
    hot1 = row == i1 - N_EXPERT_GROUPS
    hot2 = row == i2 - N_EXPERT_GROUPS
    onehot = jnp.where(hot1 | hot2, 1.0, 0.0)
    counts = jnp.broadcast_to(jnp.sum(onehot, axis=1, keepdims=True), (LANES, LANES))
    earlier = (lax.broadcasted_iota(jnp.int32, (rows, rows), 0)
               < lax.broadcasted_iota(jnp.int32, (rows, rows), 1)).astype(BF16)
    rank = _dot(onehot.astype(BF16), earlier)
    run = (((counts.astype(jnp.int32) + (PIECE - 1)) // PIECE) * PIECE).astype(BF16)
    below = (lax.broadcasted_iota(jnp.int32, (LANES, LANES), 1)
             < lax.broadcasted_iota(jnp.int32, (LANES, LANES), 0)).astype(BF16)
    start = _dot(below, run)
    slot = jnp.concatenate([start] * (rows // LANES), axis=1) + rank
    s1 = jnp.sum(jnp.where(hot1, slot, 0.0), axis=0, keepdims=True)
    s2 = jnp.sum(jnp.where(hot2, slot, 0.0), axis=0, keepdims=True)
    srow = lax.broadcasted_iota(jnp.int32, (cap, rows), 0).astype(F32)
    perm = jnp.where((srow == s1) | (srow == s2), 1.0, 0.0).astype(BF16)
    route_t = jnp.where(row == 0, w1, jnp.where(row == 1, w2, jnp.where(row == 2, s1, jnp.where(row == 3, s2, 0.0))))
    return _dot(perm, hb).astype(BF16), route_t.T, counts


def _sigmoid(x):
    return 0.5 * jnp.tanh(0.5 * x) + 0.5


def _merge_ln1(x, oa, ob, oc, gz, wa_ref, wb_ref, wc_ref, wo_ref, g1, b1):
    ya = _dot(oa, wa_ref[...])
    yb = _dot(ob.astype(BF16), wb_ref[...])
    yc = _dot(oc.astype(BF16), wc_ref[...])
    m = (_sigmoid(gz[:, 0:D_MODEL]) * ya
         + _sigmoid(gz[:, D_MODEL:2 * D_MODEL]) * yb
         + _sigmoid(gz[:, 2 * D_MODEL:3 * D_MODEL]) * yc)
    hpre = ALPHA * x + _dot(m.astype(BF16), wo_ref[...])
    return _layer_norm(hpre, g1, b1)


def _mem_kernel(mem_ref, wk_ref, wv_ref, mk_ref, mv_ref):
    m = mem_ref[...].astype(BF16)
    mk_ref[...] = _dot(m, wk_ref[...])
    mv_ref[...] = _dot(m, wv_ref[...])


def _mem_project(mem, wk, wv):
    b = mem.shape[0]
    out = jax.ShapeDtypeStruct((b, N_MEM, MEM_WIDTH), F32)
    return pl.pallas_call(
        _mem_kernel,
        grid=(b,),
        in_specs=[pl.BlockSpec((None, N_MEM, D_MODEL), lambda i: (i, 0, 0)),
                  _const_spec((D_MODEL, MEM_WIDTH)), _const_spec((D_MODEL, MEM_WIDTH))],
        out_specs=[pl.BlockSpec((None, N_MEM, MEM_WIDTH), lambda i: (i, 0, 0))] * 2,
        out_shape=[out, out],
        name="mem_project",
    )(mem, wk, wv)


def _project_q(xb, wq_ref, zq_ref):
    for c0 in range(0, Q_WIDTH, Q_CHUNK):
        zq_ref[:, c0:c0 + Q_CHUNK] = _dot(xb, wq_ref[:, c0:c0 + Q_CHUNK])


def _project_rest(xb, wrest_ref, z_ref):
    rest = IN_WIDTH - K0
    for c0 in range(0, rest, IN_CHUNK):
        c1 = min(c0 + IN_CHUNK, rest)
        z_ref[:, K0 + c0:K0 + c1] = _dot(xb, wrest_ref[:, c0:c1])


def _front_kernel(sinks_ref, x_ref, xn_ref, rope_ref, wq_ref, wrest_ref, wa_ref, wb_ref, wc_ref, wo_ref, wmix_ref,
                  pscale_ref, mk_ref, mv_ref, g1_ref, b1_ref, wr_ref, br_ref,
                  h_ref, xs_ref, route_ref, counts_ref, nk_ref, nv_ref, npool_ref,
                  z_ref, qb_ref, kext_ref, vext_ref, uext_ref, oa_ref, ob_ref, bias_ref, s_ref, p_ref, vblk_ref, zq_ref):
    i = pl.program_id(1)
    x = x_ref[...]
    xb = x.astype(BF16)
    hist = 2 * SUBLANES

    @pl.when((pl.program_id(0) == 0) & (i == 0))
    def _():
        _project_q(xb, wq_ref, zq_ref)

    @pl.when(i == 0)
    def _():
        kext_ref[0:QB, :] = jnp.zeros((QB, KV_WIDTH), BF16)
        vext_ref[0:QB, :] = jnp.zeros((QB, KV_WIDTH), BF16)
        uext_ref[0:hist, :] = jnp.zeros((hist, POOL_WIDTH), F32)

    @pl.when(i > 0)
    def _():
        kext_ref[0:QB, :] = kext_ref[TM:TM + QB, :]
        vext_ref[0:QB, :] = vext_ref[TM:TM + QB, :]
        uext_ref[0:hist, :] = uext_ref[TM:TM + hist, :]

    _project_rest(xb, wrest_ref, z_ref)

    c = rope_ref[0]
    s1 = rope_ref[1]
    s2 = rope_ref[2]
    for j in range(Q_WIDTH // LANES):
        sl = slice(j * LANES, (j + 1) * LANES)
        qb_ref[:, sl] = _rope(zq_ref[:, sl], c, s1, s2).astype(BF16)
    for j in range(KV_WIDTH // LANES):
        sl = slice(K0 + j * LANES, K0 + (j + 1) * LANES)
        kr = _rope(z_ref[:, sl], c, s1, s2)
        z_ref[:, sl] = kr
        kext_ref[QB:QB + TM, j * LANES:(j + 1) * LANES] = kr.astype(BF16)
    vext_ref[QB:QB + TM, :] = z_ref[:, V0:V0 + KV_WIDTH].astype(BF16)
    uext_ref[hist:hist + TM, :] = z_ref[:, U0:U0 + POOL_WIDTH]
    nk_ref[...] = z_ref[TM - QB:TM, K0:K0 + KV_WIDTH]
    nv_ref[...] = z_ref[TM - QB:TM, V0:V0 + KV_WIDTH]

    rowq = lax.broadcasted_iota(jnp.int32, (QB, 2 * QB), 0)
    colk = lax.broadcasted_iota(jnp.int32, (QB, 2 * QB), 1)
    band = (colk >= rowq) & (colk <= rowq + WINDOW)
    bias_ref[1] = jnp.where(band, 0.0, NEG_INF)
    bias_ref[0] = jnp.where(band & ((colk >= QB) | (i > 0)), 0.0, NEG_INF)
    for sb in range(TM // QB):
        k2 = kext_ref[sb * QB:(sb + 2) * QB, :]
        v2 = vext_ref[sb * QB:(sb + 2) * QB, :]
        qs = jnp.concatenate(
            [qb_ref[sb * QB:(sb + 1) * QB, g * KV_WIDTH:(g + 1) * KV_WIDTH] for g in range(GROUP)], axis=0)
        for kv in range(N_KV):
            kmask = _lane_block_mask((2 * QB, KV_WIDTH), kv)
            s_ref[...] = _dot_nt(qs, jnp.where(kmask, k2, jnp.zeros_like(k2)))
            vblk_ref[kv * 2 * QB:(kv + 1) * 2 * QB, :] = jnp.where(kmask, v2, jnp.zeros_like(v2))
            for c0 in range(0, GROUP * QB, SOFTMAX_ROWS):
                rq = c0 % QB
                sink = sinks_ref[kv * GROUP + c0 // QB]
                s = s_ref[c0:c0 + SOFTMAX_ROWS, :] + bias_ref[min(sb, 1), rq:rq + SOFTMAX_ROWS, :]
                m = jnp.maximum(jnp.max(s, axis=1, keepdims=True), sink)
                p = jnp.exp(s - m)
                den = jnp.sum(p, axis=1, keepdims=True) + jnp.exp(sink - m)
                p_ref[c0:c0 + SOFTMAX_ROWS, kv * 2 * QB:(kv + 1) * 2 * QB] = (p * (1.0 / den)).astype(BF16)
        o = _dot(p_ref[...], vblk_ref[...])
        for g in range(GROUP):
            oa_ref[sb * QB:(sb + 1) * QB, g * KV_WIDTH:(g + 1) * KV_WIDTH] = o[g * QB:(g + 1) * QB].astype(BF16)

    npool_ref[...] = uext_ref[TM:TM + hist, :]
    pos = i * TM + lax.broadcasted_iota(jnp.int32, (TM, 1), 0)
    for g, w in enumerate(POOL_WINDOWS):
        sl = slice(g * POOL_GROUP_DIM, (g + 1) * POOL_GROUP_DIM)
        cur = uext_ref[hist:hist + TM, sl]
        ws = cur
        for j in range(1, w):
            ws = ws + uext_ref[hist - j:hist - j + TM, sl]
        cnt = jnp.minimum(pos + 1, w).astype(F32)
        pooled = ws / cnt - cur
        ob_ref[:, sl] = _dot(pooled.astype(BF16), wmix_ref[g]) * pscale_ref[:, sl]

    cq = (z_ref[:, CQ0:CQ0 + MEM_WIDTH] * ATT_SCALE).astype(BF16)
    mk = mk_ref[...].astype(BF16)
    mv = mv_ref[...].astype(BF16)
    for hh in range(MEM_HEADS):
        hmask = _lane_block_mask((N_MEM, MEM_WIDTH), hh)
        s_ref[0:TM, :] = _dot_nt(cq, jnp.where(hmask, mk, jnp.zeros_like(mk)))
        vblk_ref[hh * N_MEM:(hh + 1) * N_MEM, :] = jnp.where(hmask, mv, jnp.zeros_like(mv))
        for c0 in range(0, TM, SOFTMAX_ROWS):
            s = s_ref[c0:c0 + SOFTMAX_ROWS, :]
            p = jnp.exp(s - jnp.max(s, axis=1, keepdims=True))
            den = jnp.sum(p, axis=1, keepdims=True)
            p_ref[c0:c0 + SOFTMAX_ROWS, hh * N_MEM:(hh + 1) * N_MEM] = (p * (1.0 / den)).astype(BF16)
    oc = _dot(p_ref[0:TM, :], vblk_ref[...])

    h = _merge_ln1(x, oa_ref[...], ob_ref[...], oc, z_ref[:, GZ0:GZ0 + 3 * D_MODEL],
                   wa_ref, wb_ref, wc_ref, wo_ref, g1_ref[...], b1_ref[...])
    h_ref[...] = h
    hb = h.astype(BF16)
    logits = _dot(hb, wr_ref[...]) + br_ref[...]
    _project_q(xn_ref[...].astype(BF16), wq_ref, zq_ref)
    xs_ref[...], route_ref[...], counts_ref[...] = _route_and_sort(logits, hb, xs_ref.shape[0])


def _front(x, rope, sinks, wq, wrest, wa, wb, wc, wo, wmix, pscale, mk, mv, g1, b1, wr, br):
    b, l, _ = x.shape
    nt = l // TM
    cap = _chunk_rows(b * nt)
    hist = 2 * SUBLANES
    tile = lambda w: pl.BlockSpec((None, TM, w), lambda bi, ti: (bi, ti, 0))
    per_b = lambda r, w: pl.BlockSpec((None, r, w), lambda bi, ti: (bi, 0, 0))
    nxt = lambda bi, ti: jnp.minimum(bi * nt + ti + 1, b * nt - 1)
    return pl.pallas_call(
        _front_kernel,
        grid=(b, nt),
        in_specs=[
            pl.BlockSpec(memory_space=pltpu.SMEM),
            tile(D_MODEL),
            pl.BlockSpec((None, TM, D_MODEL), lambda bi, ti: (nxt(bi, ti) // nt, nxt(bi, ti) % nt, 0)),
            pl.BlockSpec((3, TM, LANES), lambda bi, ti: (0, ti, 0)),
            _const_spec((D_MODEL, Q_WIDTH)), _const_spec((D_MODEL, IN_WIDTH - K0)),
            _const_spec((Q_WIDTH, D_MODEL)), _const_spec((POOL_WIDTH, D_MODEL)),
            _const_spec((MEM_WIDTH, D_MODEL)), _const_spec((D_MODEL, D_MODEL)),
            _const_spec((len(POOL_WINDOWS), POOL_GROUP_DIM, POOL_GROUP_DIM)),
            _const_spec((1, POOL_WIDTH)),
            per_b(N_MEM, MEM_WIDTH), per_b(N_MEM, MEM_WIDTH),
            _const_spec((1, D_MODEL)), _const_spec((1, D_MODEL)),
            _const_spec((D_MODEL, LANES)), _const_spec((1, LANES)),
        ],
        out_specs=[
            tile(D_MODEL),
            pl.BlockSpec((cap, D_MODEL), lambda bi, ti: (bi * nt + ti, 0)),
            tile(LANES),
            pl.BlockSpec((None, None, LANES, LANES), lambda bi, ti: (bi, ti, 0, 0)),
            per_b(QB, KV_WIDTH), per_b(QB, KV_WIDTH), per_b(hist, POOL_WIDTH),
        ],
        out_shape=[
            jax.ShapeDtypeStruct((b, l, D_MODEL), F32),
            jax.ShapeDtypeStruct((b * nt * cap, D_MODEL), BF16),
            jax.ShapeDtypeStruct((b, l, LANES), F32),
            jax.ShapeDtypeStruct((b, nt, LANES, LANES), F32),
            jax.ShapeDtypeStruct((b, QB, KV_WIDTH), F32),
            jax.ShapeDtypeStruct((b, QB, KV_WIDTH), F32),
            jax.ShapeDtypeStruct((b, hist, POOL_WIDTH), F32),
        ],
        scratch_shapes=[
            pltpu.VMEM((TM, IN_WIDTH), F32),
            pltpu.VMEM((TM, Q_WIDTH), BF16),
            pltpu.VMEM((QB + TM, KV_WIDTH), BF16),
            pltpu.VMEM((QB + TM, KV_WIDTH), BF16),
            pltpu.VMEM((hist + TM, POOL_WIDTH), F32),
            pltpu.VMEM((TM, Q_WIDTH), BF16),
            pltpu.VMEM((TM, POOL_WIDTH), F32),
            pltpu.VMEM((2, QB, 2 * QB), F32),
            pltpu.VMEM((GROUP * QB, 2 * QB), F32),
            pltpu.VMEM((GROUP * QB, N_KV * 2 * QB), BF16),
            pltpu.VMEM((N_KV * 2 * QB, KV_WIDTH), BF16),
            pltpu.VMEM((TM, Q_WIDTH), F32),
        ],
        compiler_params=pltpu.CompilerParams(
            dimension_semantics=("arbitrary", "arbitrary"), vmem_limit_bytes=VMEM_LIMIT),
        name="front_prompt",
    )(sinks, x, x, rope, wq, wrest, wa, wb, wc, wo, wmix, pscale, mk, mv, g1, b1, wr, br)


def _expert_mlp(xb, wg, wu, wd):
    a = _dot(xb, wg)
    hid = (a * jax.nn.sigmoid(a)) * _dot(xb, wu)
    return _dot(hid.astype(BF16), wd)


def _gemm_kernel(src_ref, dst_ref, te_ref, act_ref, xs_ref, wg_ref, wu_ref, wd_ref, ys_ref,
                 xbuf, obuf, wgb, wub, wdb, prime, sem_in, sem_out):
    i = pl.program_id(0)
    n = pl.num_programs(0)
    slot = i % 2
    in_slot = i % IN_SLOTS

    def start_in(tile, slot):
        for j in range(PIECES_PER_TILE):
            row0 = pl.multiple_of(src_ref[tile * PIECES_PER_TILE + j], PIECE)
            pltpu.make_async_copy(xs_ref.at[pl.ds(row0, PIECE)], xbuf.at[slot, pl.ds(j * PIECE, PIECE)],
                                  sem_in.at[slot]).start()

    def start_out(tile, slot):
        for j in range(PIECES_PER_TILE):
            row0 = pl.multiple_of(dst_ref[tile * PIECES_PER_TILE + j], PIECE)
            pltpu.make_async_copy(obuf.at[slot, pl.ds(j * PIECE, PIECE)], ys_ref.at[pl.ds(row0, PIECE)],
                                  sem_out.at[slot]).start()

    def wait_in(slot):
        for j in range(PIECES_PER_TILE):
            pltpu.make_async_copy(xs_ref.at[pl.ds(0, PIECE)], xbuf.at[slot, pl.ds(j * PIECE, PIECE)],
                                  sem_in.at[slot]).wait()

    def wait_out(slot):
        for j in range(PIECES_PER_TILE):
            pltpu.make_async_copy(obuf.at[slot, pl.ds(j * PIECE, PIECE)], ys_ref.at[pl.ds(0, PIECE)],
                                  sem_out.at[slot]).wait()

    active = act_ref[i] > 0
    ahead = IN_SLOTS - 1
    prefetched = (i < ahead) | (act_ref[jnp.maximum(i - ahead, 0)] > 0)
    out_pending = (i < 2) | (act_ref[jnp.maximum(i - 2, 0)] > 0)

    @pl.when(i == 0)
    def _():
        prime[0] = jnp.zeros((PIECE, D_MODEL), BF16)
        for s in range(2):
            for j in range(PIECES_PER_TILE):
                pltpu.make_async_copy(prime.at[0], prime.at[1 + s * PIECES_PER_TILE + j], sem_out.at[s]).start()
        for t in range(ahead):
            start_in(t, t)

    prev_active = (i >= 1) & (act_ref[jnp.maximum(i - 1, 0)] > 0)

    def active_step(after_first):
        @pl.when((i == 0) | (te_ref[i] != te_ref[jnp.maximum(i - 1, 0)]))
        def _():
            wgb[...] = wg_ref[...].astype(BF16)
            wub[...] = wu_ref[...].astype(BF16)
            wdb[...] = wd_ref[...].astype(BF16)

        wait_in(in_slot)
        wait_out(slot)
        out = _expert_mlp(xbuf[in_slot], wgb[...], wub[...], wdb[...])
        start_in(i + ahead, (i + ahead) % IN_SLOTS)
        if after_first:
            start_out(i - 1, 1 - slot)
        obuf[slot] = out.astype(BF16)

    @pl.when(active & (i == 0))
    def _():
        active_step(False)

    @pl.when(active & (i > 0))
    def _():
        active_step(True)

    @pl.when(jnp.logical_not(active))
    def _():
        @pl.when(prev_active)
        def _():
            start_out(i - 1, 1 - slot)

        @pl.when(prefetched)
        def _():
            wait_in(in_slot)

        @pl.when(out_pending)
        def _():
            wait_out(slot)

    @pl.when(i == n - 1)
    def _():
        for t in range(ahead):
            @pl.when(act_ref[jnp.maximum(i - t, 0)] > 0)
            def _(t=t):
                wait_in((i - t + ahead) % IN_SLOTS)

        @pl.when(active)
        def _():
            start_out(i, slot)
            wait_out(slot)

        @pl.when(prev_active)
        def _():
            wait_out(1 - slot)


def _grouped_gemm(piece_src, piece_dst, tile_expert, tile_active, xs, wg, wu, wd):
    n_tiles = tile_expert.shape[0]
    assert n_tiles >= 2
    wspec = lambda r, c: pl.BlockSpec((None, r, c), lambda i, src, dst, te, act: (te[i], 0, 0))
    return pl.pallas_call(
        _gemm_kernel,
        grid_spec=pltpu.PrefetchScalarGridSpec(
            num_scalar_prefetch=4,
            grid=(n_tiles,),
            in_specs=[pl.BlockSpec(memory_space=pl.ANY),
                      wspec(D_MODEL, D_EXPERT), wspec(D_MODEL, D_EXPERT), wspec(D_EXPERT, D_MODEL)],
            out_specs=pl.BlockSpec(memory_space=pl.ANY),
            scratch_shapes=[pltpu.VMEM((IN_SLOTS, TG, D_MODEL), BF16), pltpu.VMEM((2, TG, D_MODEL), BF16),
                            pltpu.VMEM((D_MODEL, D_EXPERT), BF16), pltpu.VMEM((D_MODEL, D_EXPERT), BF16),
                            pltpu.VMEM((D_EXPERT, D_MODEL), BF16),
                            pltpu.VMEM((1 + 2 * PIECES_PER_TILE, PIECE, D_MODEL), BF16),
                            pltpu.SemaphoreType.DMA((IN_SLOTS,)), pltpu.SemaphoreType.DMA((2,))],
        ),
        out_shape=jax.ShapeDtypeStruct(xs.shape, xs.dtype),
        input_output_aliases={4: 0},
        compiler_params=pltpu.CompilerParams(dimension_semantics=("arbitrary",), vmem_limit_bytes=VMEM_LIMIT),
        name="moe_grouped_gemm",
    )(piece_src, piece_dst, tile_expert, tile_active, xs, wg, wu, wd)


def _combine_kernel(ys_ref, h_ref, route_ref, g2_ref, b2_ref, y_ref):
    chunks = h_ref.shape[0] // TM
    cap = ys_ref.shape[0] // chunks
    slot = lax.broadcasted_iota(jnp.int32, (TM, cap), 1).astype(F32)
    for c in range(chunks):
        rows = slice(c * TM, (c + 1) * TM)
        route = route_ref[rows, :]
        sel = jnp.where(slot == route[:, 2:3], route[:, 0:1], jnp.where(slot == route[:, 3:4], route[:, 1:2], 0.0))
        f = _dot(sel.astype(BF16), ys_ref[c * cap:(c + 1) * cap, :])
        y_ref[rows, :] = _layer_norm(ALPHA * h_ref[rows, :] + f, g2_ref[...], b2_ref[...])


def _combine(ys, h, route, g2, b2):
    t = h.shape[0]
    cap = ys.shape[0] // (t // TM)
    per_step = next(k for k in (4, 2, 1) if (t // TM) % k == 0)
    rows = per_step * TM
    return pl.pallas_call(
        _combine_kernel,
        grid=(t // rows,),
        in_specs=[
            pl.BlockSpec((per_step * cap, D_MODEL), lambda i: (i, 0)),
            pl.BlockSpec((rows, D_MODEL), lambda i: (i, 0)),
            pl.BlockSpec((rows, LANES), lambda i: (i, 0)),
            pl.BlockSpec((1, D_MODEL), lambda i: (0, 0)),
            pl.BlockSpec((1, D_MODEL), lambda i: (0, 0)),
        ],
        out_specs=pl.BlockSpec((rows, D_MODEL), lambda i: (i, 0)),
        out_shape=jax.ShapeDtypeStruct((t, D_MODEL), F32),
        compiler_params=pltpu.CompilerParams(dimension_semantics=("arbitrary",)),
        name="moe_combine",
    )(ys, h, route, g2, b2)


def _select(table, idx):
    hot = idx[:, None] == jnp.arange(table.shape[0], dtype=jnp.int32)[None, :]
    return jnp.sum(jnp.where(hot[:, :, None], table[None, :, :], 0), axis=1)


def _piece_tables(counts):
    n_chunks = counts.shape[0]
    n_tiles = -(-(n_chunks * MAX_CHUNK_PIECES + N_EXPERTS * (PIECES_PER_TILE - 1)) // PIECES_PER_TILE)
    npc = (counts + (PIECE - 1)) // PIECE
    first = (jnp.cumsum(npc, axis=1) - npc).T
    npc_t = npc.T
    cum = jnp.cumsum(npc_t, axis=1)
    per_expert = cum[:, -1]
    tiles_e = (per_expert + (PIECES_PER_TILE - 1)) // PIECES_PER_TILE
    tile_end = jnp.cumsum(tiles_e)
    tile_idx = jnp.arange(n_tiles, dtype=jnp.int32)
    expert_of = lambda i: jnp.minimum(jnp.sum(i[:, None] >= tile_end[None, :], axis=1), N_EXPERTS - 1).astype(jnp.int32)
    active = tile_idx < tile_end[-1]
    tile_expert = jnp.where(active, expert_of(tile_idx), expert_of(tile_end[-1:] - 1))
    meta = jnp.stack([tile_end - tiles_e, per_expert], axis=1)
    meta_t = _select(meta, tile_expert)
    k = (tile_idx - meta_t[:, 0])[:, None] * PIECES_PER_TILE + jnp.arange(PIECES_PER_TILE, dtype=jnp.int32)[None, :]
    valid = active[:, None] & (k < meta_t[:, 1:2])
    cum_t = _select(cum, tile_expert)
    chunk = jnp.minimum(jnp.sum(k[:, :, None] >= cum_t[:, None, :], axis=2), n_chunks - 1).astype(jnp.int32)
    at_chunk = chunk[:, :, None] == jnp.arange(n_chunks, dtype=jnp.int32)[None, None, :]
    pick = lambda tab: jnp.sum(jnp.where(at_chunk, _select(tab, tile_expert)[:, None, :], 0), axis=2)
    piece = pick(first) + k - pick(cum - npc_t)
    cap = _chunk_rows(n_chunks)
    rows = chunk * cap + piece * PIECE
    d = ((tile_idx % 2)[:, None] * (PIECES_PER_TILE - 1)
         + jnp.maximum(jnp.arange(PIECES_PER_TILE, dtype=jnp.int32)[None, :] - 1, 0))
    spare_row = lambda d: (d % n_chunks) * cap + (MAX_CHUNK_PIECES + d // n_chunks) * PIECE
    spare = spare_row(d)
    zero_piece = spare_row(N_SPARE - 1)
    extra = jnp.full(((IN_SLOTS - 1) * PIECES_PER_TILE,), zero_piece, jnp.int32)
    src = jnp.concatenate([jnp.where(valid, rows, zero_piece).astype(jnp.int32).reshape(-1), extra])
    dst = jnp.concatenate([jnp.where(valid, rows, spare).astype(jnp.int32).reshape(-1), extra])
    return src, dst, tile_expert.astype(jnp.int32), active.astype(jnp.int32)


def _sample_attn_kernel(x_ref, rope_ref, wq_ref, wrest_ref, sink_ref, ck_ref, cv_ref, cmk_ref, cmv_ref,
                        z_ref, oa_ref, oc_ref, nk_ref, nv_ref, knew_t, vnew_t):
    j = pl.program_id(0)
    db = x_ref.shape[0]

    @pl.when(j == 0)
    def _():
        xb = x_ref[...].astype(BF16)
        _project_q(xb, wq_ref, z_ref)
        _project_rest(xb, wrest_ref, z_ref)
        c = rope_ref[0]
        s1 = rope_ref[1]
        s2 = rope_ref[2]
        for jj in range((Q_WIDTH + KV_WIDTH) // LANES):
            sl = slice(jj * LANES, (jj + 1) * LANES)
            z_ref[:, sl] = _rope(z_ref[:, sl], c, s1, s2)
        knew_t[...] = z_ref[:, K0:K0 + KV_WIDTH].T
        vnew_t[...] = z_ref[:, V0:V0 + KV_WIDTH].T

    r0 = pl.multiple_of(j * SB, SB)
    zq = z_ref[pl.ds(r0, SB), Q0:Q0 + Q_WIDTH]
    zk = z_ref[pl.ds(r0, SB), K0:K0 + KV_WIDTH]
    zv = z_ref[pl.ds(r0, SB), V0:V0 + KV_WIDTH]
    zc = z_ref[pl.ds(r0, SB), CQ0:CQ0 + MEM_WIDTH] * ATT_SCALE
    sink = sink_ref[:, 0:1]
    row_kv = lax.broadcasted_iota(jnp.int32, (N_HEADS, KV_WIDTH), 0) & (N_KV - 1)
    lane_kv = lax.broadcasted_iota(jnp.int32, (N_HEADS, KV_WIDTH), 1) // HEAD_DIM
    own = row_kv == lane_kv
    row_c = lax.broadcasted_iota(jnp.int32, (N_HEADS, MEM_WIDTH), 0)
    lane_c = lax.broadcasted_iota(jnp.int32, (N_HEADS, MEM_WIDTH), 1) // HEAD_DIM
    own_c = row_c == lane_c
    last_pos = lax.broadcasted_iota(jnp.int32, (KV_WIDTH, WINDOW), 1) == WINDOW - 1
    seq_lane = lax.broadcasted_iota(jnp.int32, (KV_WIDTH, db), 1)

    seqs = range(SB)
    qblk, cblk, s, sc = [], [], [], []
    for b in seqs:
        q4 = jnp.concatenate(
            [jnp.broadcast_to(zq[b:b + 1, g * KV_WIDTH:(g + 1) * KV_WIDTH], (N_KV, KV_WIDTH)) for g in range(GROUP)],
            axis=0)
        qblk.append(jnp.where(own, q4, 0.0).astype(BF16))
        cblk.append(jnp.where(own_c, jnp.broadcast_to(zc[b:b + 1, :], (N_HEADS, MEM_WIDTH)), 0.0).astype(BF16))
        s.append(_dot(qblk[b], ck_ref[b].astype(BF16)))
        sc.append(_dot(cblk[b], cmk_ref[b].astype(BF16)))

    pn, p_new, pc = [], [], []
    for b in seqs:
        s_new = jnp.sum(qblk[b].astype(F32) * zk[b:b + 1, :].astype(BF16).astype(F32), axis=1, keepdims=True)
        m = jnp.maximum(jnp.maximum(jnp.max(s[b], axis=1, keepdims=True), s_new), sink)
        p = jnp.exp(s[b] - m)
        e_new = jnp.exp(s_new - m)
        inv = 1.0 / (jnp.sum(p, axis=1, keepdims=True) + e_new + jnp.exp(sink - m))
        pn.append((p * inv).astype(BF16))
        p_new.append((e_new * inv).astype(BF16).astype(F32))
        e = jnp.exp(sc[b] - jnp.max(sc[b], axis=1, keepdims=True))
        pc.append((e * (1.0 / jnp.sum(e, axis=1, keepdims=True))).astype(BF16))

    oa_rows = [[] for _ in range(GROUP)]
    oc_rows = []
    for b in seqs:
        vc = cv_ref[b]
        o = _dot_nt(pn[b], vc.astype(BF16)) + p_new[b] * zv[b:b + 1, :].astype(BF16).astype(F32)
        o = jnp.where(own, o, 0.0)
        for g in range(GROUP):
            oa_rows[g].append(jnp.sum(o[g * N_KV:(g + 1) * N_KV], axis=0, keepdims=True))
        ocb = jnp.where(own_c, _dot_nt(pc[b], cmv_ref[b].astype(BF16)), 0.0)
        oc_rows.append(jnp.sum(ocb, axis=0, keepdims=True))
        here = seq_lane == j * SB + b
        knew_col = jnp.sum(jnp.where(here, knew_t[...], 0.0), axis=1, keepdims=True)
        vnew_col = jnp.sum(jnp.where(here, vnew_t[...], 0.0), axis=1, keepdims=True)
        nk_ref[b] = jnp.where(last_pos, knew_col, pltpu.roll(ck_ref[b], WINDOW - 1, 1))
        nv_ref[b] = jnp.where(last_pos, vnew_col, pltpu.roll(vc, WINDOW - 1, 1))
    for g in range(GROUP):
        oa_ref[pl.ds(r0, SB), g * KV_WIDTH:(g + 1) * KV_WIDTH] = jnp.concatenate(oa_rows[g], axis=0)
    oc_ref[pl.ds(r0, SB), :] = jnp.concatenate(oc_rows, axis=0)


def _sample_attn(x, rope, wq, wrest, sink_gk, ck, cv, cmk, cmv):
    db = x.shape[0]
    blk = lambda r: pl.BlockSpec((SB, KV_WIDTH, r), lambda j: (j, 0, 0))
    full = lambda w: pl.BlockSpec((db, w), lambda j: (0, 0))
    return pl.pallas_call(
        _sample_attn_kernel,
        grid=(db // SB,),
        in_specs=[
            full(D_MODEL), _const_spec((3, 1, LANES)),
            _const_spec((D_MODEL, Q_WIDTH)), _const_spec((D_MODEL, IN_WIDTH - K0)),
            _const_spec((N_HEADS, LANES)),
            blk(WINDOW), blk(WINDOW), blk(N_MEM), blk(N_MEM),
        ],
        out_specs=[full(IN_WIDTH), full(Q_WIDTH), full(MEM_WIDTH), blk(WINDOW), blk(WINDOW)],
        out_shape=[
            jax.ShapeDtypeStruct((db, IN_WIDTH), F32),
            jax.ShapeDtypeStruct((db, Q_WIDTH), F32),
            jax.ShapeDtypeStruct((db, MEM_WIDTH), F32),
            jax.ShapeDtypeStruct((db, KV_WIDTH, WINDOW), F32),
            jax.ShapeDtypeStruct((db, KV_WIDTH, WINDOW), F32),
        ],
        scratch_shapes=[pltpu.VMEM((KV_WIDTH, db), F32), pltpu.VMEM((KV_WIDTH, db), F32)],
        compiler_params=pltpu.CompilerParams(dimension_semantics=("arbitrary",), vmem_limit_bytes=VMEM_LIMIT),
        name="sample_attn",
    )(x, rope, wq, wrest, sink_gk, ck, cv, cmk, cmv)


def _sample_tail_kernel(x_ref, z_ref, oa_ref, oc_ref, st_ref, wa_ref, wb_ref, wc_ref, wo_ref, wmix_ref,
                        pscale_ref, g1_ref, b1_ref, wr_ref, br_ref, wg_ref, wu_ref, wd_ref, g2_ref, b2_ref,
                        y_ref, npool_ref, h_sc, comb_sc, acc_sc):
    e = pl.program_id(0)

    @pl.when(e == 0)
    def _():
        u = z_ref[:, U0:U0 + POOL_WIDTH]
        npool_ref[0:POOL_STATE - 1] = st_ref[1:POOL_STATE]
        npool_ref[POOL_STATE - 1] = u
        obs = []
        for g, w in enumerate(POOL_WINDOWS):
            sl = slice(g * POOL_GROUP_DIM, (g + 1) * POOL_GROUP_DIM)
            cur = u[:, sl]
            ws = cur
            for jj in range(1, w):
                ws = ws + st_ref[POOL_STATE - jj, :, sl]
            cnt = float(min(PAST_LEN + 1, w))
            pooled = ws / cnt - cur
            obs.append(_dot(pooled.astype(BF16), wmix_ref[g]) * pscale_ref[:, sl])
        ob = jnp.concatenate(obs, axis=1)
        h = _merge_ln1(x_ref[...], oa_ref[...].astype(BF16), ob, oc_ref[...], z_ref[:, GZ0:GZ0 + 3 * D_MODEL],
                       wa_ref, wb_ref, wc_ref, wo_ref, g1_ref[...], b1_ref[...])
        h_sc[...] = h
        logits = _dot(h.astype(BF16), wr_ref[...]) + br_ref[...]
        hot1, hot2, w1, w2 = _route(logits)
        comb_sc[...] = jnp.where(hot1, w1, 0.0) + jnp.where(hot2, w2, 0.0)
        acc_sc[...] = jnp.zeros_like(acc_sc)

    out = _expert_mlp(h_sc[...].astype(BF16), wg_ref[...].astype(BF16), wu_ref[...].astype(BF16),
                      wd_ref[...].astype(BF16))
    lane = lax.broadcasted_iota(jnp.int32, comb_sc.shape, 1)
    ce = jnp.sum(jnp.where(lane == e, comb_sc[...], 0.0), axis=1, keepdims=True)
    acc_sc[...] += ce * out

    @pl.when(e == pl.num_programs(0) - 1)
    def _():
        y_ref[...] = _layer_norm(ALPHA * h_sc[...] + acc_sc[...], g2_ref[...], b2_ref[...])


def _sample_tail(x, z, oa, oc, state, wa, wb, wc, wo, wmix, pscale, g1, b1, wr, br, wg, wu, wd, g2, b2):
    db = x.shape[0]
    full = lambda w: pl.BlockSpec((db, w), lambda e: (0, 0))
    vec = lambda w: pl.BlockSpec((1, w), lambda e: (0, 0))
    hist = pl.BlockSpec((POOL_STATE, db, POOL_WIDTH), lambda e: (0, 0, 0))
    return pl.pallas_call(
        _sample_tail_kernel,
        grid=(N_EXPERTS,),
        in_specs=[
            full(D_MODEL), full(IN_WIDTH), full(Q_WIDTH), full(MEM_WIDTH), hist,
            _const_spec((Q_WIDTH, D_MODEL)), _const_spec((POOL_WIDTH, D_MODEL)),
            _const_spec((MEM_WIDTH, D_MODEL)), _const_spec((D_MODEL, D_MODEL)),
            _const_spec((len(POOL_WINDOWS), POOL_GROUP_DIM, POOL_GROUP_DIM)),
            vec(POOL_WIDTH), vec(D_MODEL), vec(D_MODEL),
            _const_spec((D_MODEL, LANES)), vec(LANES),
            pl.BlockSpec((None, D_MODEL, D_EXPERT), lambda e: (e, 0, 0)),
            pl.BlockSpec((None, D_MODEL, D_EXPERT), lambda e: (e, 0, 0)),
            pl.BlockSpec((None, D_EXPERT, D_MODEL), lambda e: (e, 0, 0)),
            vec(D_MODEL), vec(D_MODEL),
        ],
        out_specs=[full(D_MODEL), hist],
        out_shape=[jax.ShapeDtypeStruct((db, D_MODEL), F32),
                   jax.ShapeDtypeStruct((POOL_STATE, db, POOL_WIDTH), F32)],
        scratch_shapes=[pltpu.VMEM((db, D_MODEL), F32), pltpu.VMEM((db, LANES), F32),
                        pltpu.VMEM((db, D_MODEL), F32)],
        compiler_params=pltpu.CompilerParams(dimension_semantics=("arbitrary",), vmem_limit_bytes=VMEM_LIMIT),
        name="sample_tail",
    )(x, z, oa, oc, state, wa, wb, wc, wo, wmix, pscale, g1, b1, wr, br, wg, wu, wd, g2, b2)


def _rope_tables(pos):
    half = ROPE_DIM // 2
    inv = jnp.power(ROPE_THETA, -jnp.arange(half, dtype=F32) * (2.0 / ROPE_DIM))
    ang = pos.astype(F32)[:, None] * jnp.tile(inv, LANES // half)[None, :]
    off = lax.broadcasted_iota(jnp.int32, (1, LANES), 1) % HEAD_DIM
    cos = jnp.cos(ang)
    sin = jnp.sin(ang)
    c = jnp.where(off < ROPE_DIM, cos, 1.0)
    s1 = jnp.where((off >= half) & (off < ROPE_DIM), sin, 0.0)
    s2 = jnp.where(off < half, -sin, 0.0)
    return jnp.stack([c, s1, s2])


def _q_heads_group_major(w, axis):
    if axis == 1:
        n = w.shape[0]
        return w.reshape(n, N_KV, GROUP, HEAD_DIM).transpose(0, 2, 1, 3).reshape(n, Q_WIDTH)
    n = w.shape[1]
    return w.reshape(N_KV, GROUP, HEAD_DIM, n).transpose(1, 0, 2, 3).reshape(Q_WIDTH, n)


def kernel(x_prompt, x_sample, cache_win_k, cache_win_v, state_pool, cache_mem_k, cache_mem_v, mem_prompt, w_in, sinks, w_pool_mix, pool_scale, w_mem_k, w_mem_v, w_branch_a, w_branch_b, w_branch_c, w_out, ln1_g, ln1_b, w_group, b_group, w_router, b_router, w_gate, w_up, w_down, ln2_g, ln2_b):
    assert w_in.shape[0] == DEPTH == 1
    b, l, _ = x_prompt.shape
    db, ds, _ = x_sample.shape
    assert ds == 1 and l % TM == 0 and db % SB == 0
    assert cache_win_k.shape[2] == WINDOW
    t = b * l

    win = w_in[0]
    wq = (_q_heads_group_major(win[:, Q0:Q0 + Q_WIDTH], 1) * ATT_SCALE).astype(BF16)
    wrest = win[:, K0:].astype(BF16)
    wa = _q_heads_group_major(w_branch_a[0], 0).astype(BF16)
    wb = w_branch_b[0].astype(BF16)
    wc = w_branch_c[0].astype(BF16)
    wo = w_out[0].astype(BF16)
    wmix = w_pool_mix[0].astype(BF16)
    pscale = pool_scale[0].reshape(1, POOL_WIDTH)
    g1 = ln1_g[0].reshape(1, D_MODEL)
    b1 = ln1_b[0].reshape(1, D_MODEL)
    g2 = ln2_g[0].reshape(1, D_MODEL)
    b2 = ln2_b[0].reshape(1, D_MODEL)
    wr = jnp.concatenate([w_group[0], w_router[0].reshape(D_MODEL, N_EXPERTS)], axis=1)
    wr = jnp.pad(wr, ((0, 0), (0, LANES - wr.shape[1]))).astype(BF16)
    br = jnp.pad(jnp.concatenate([b_group[0], b_router[0].reshape(N_EXPERTS)]), (0, LANES - N_EXPERT_GROUPS - N_EXPERTS))
    br = br.reshape(1, LANES).astype(F32)
    wg = w_gate[0]
    wu = w_up[0]
    wd = w_down[0]
    sink = sinks[0].astype(F32)
    sink_gk = jnp.broadcast_to(sink.reshape(N_KV, GROUP).T.reshape(N_HEADS, 1), (N_HEADS, LANES))

    mk, mv = _mem_project(mem_prompt, w_mem_k[0].astype(BF16), w_mem_v[0].astype(BF16))
    rope_p = _rope_tables(jnp.arange(l, dtype=jnp.int32))
    h, xs, route, counts, nk_p, nv_p, npool_p = _front(
        x_prompt, rope_p, sink, wq, wrest, wa, wb, wc, wo, wmix, pscale, mk, mv, g1, b1, wr, br)
    piece_src, piece_dst, tile_expert, tile_active = _piece_tables(
        counts.reshape(-1, LANES, LANES)[:, :N_EXPERTS, 0].astype(jnp.int32))
    ys = _grouped_gemm(piece_src, piece_dst, tile_expert, tile_active, xs, wg, wu, wd)
    y_p = _combine(ys, h.reshape(t, D_MODEL), route.reshape(t, LANES), g2, b2).reshape(b, l, D_MODEL)

    rope_s = _rope_tables(jnp.full((1,), PAST_LEN, jnp.int32))
    xs = x_sample.reshape(db, D_MODEL)
    feat_major = lambda c: jnp.transpose(c[0], (0, 2, 3, 1)).reshape(db, KV_WIDTH, c.shape[2])
    from_feat_major = lambda a: jnp.transpose(a.reshape(db, N_KV, HEAD_DIM, a.shape[2]), (0, 3, 1, 2))[None]
    z_s, oa_s, oc_s, nk_s, nv_s = _sample_attn(xs, rope_s, wq, wrest, sink_gk, feat_major(cache_win_k),
                                               feat_major(cache_win_v), feat_major(cache_mem_k), feat_major(cache_mem_v))
    state = jnp.transpose(state_pool[0], (1, 0, 2))
    y_s, npool_s = _sample_tail(xs, z_s, oa_s, oc_s, state, wa, wb, wc, wo, wmix, pscale, g1, b1, wr, br,
                                wg, wu, wd, g2, b2)

    kv5 = lambda a, n, w: a.reshape(1, n, w, N_KV, HEAD_DIM)
    return (y_p, y_s.reshape(db, 1, D_MODEL),
            kv5(nk_p, b, QB), kv5(nv_p, b, QB),
            npool_p[:, 2 * SUBLANES - POOL_STATE:, :][None],
            kv5(mk, b, N_MEM), kv5(mv, b, N_MEM),
            from_feat_major(nk_s), from_feat_major(nv_s),
            jnp.transpose(npool_s, (1, 0, 2))[None])
```

```python
import jax
import jax.numpy as jnp
from jax import lax
from jax.experimental import pallas as pl
from jax.experimental.pallas import tpu as pltpu

D_MODEL = 1024
N_HEADS = 16
HEAD_DIM = 64
N_KV = 4
GROUP = N_HEADS // N_KV
WINDOW = 128
ROPE_THETA = 500000.0
ROPE_DIM = HEAD_DIM // 4
Q_WIDTH = N_HEADS * HEAD_DIM
KV_WIDTH = N_KV * HEAD_DIM
POOL_WINDOWS = (2, 4, 8, 16)
POOL_WIDTH = D_MODEL // 2
POOL_GROUP_DIM = POOL_WIDTH // len(POOL_WINDOWS)
POOL_STATE = max(POOL_WINDOWS) - 1
N_MEM = 256
MEM_HEADS = 4
MEM_WIDTH = MEM_HEADS * HEAD_DIM
N_EXPERT_GROUPS = 4
EXPERTS_PER_GROUP = 4
N_EXPERTS = N_EXPERT_GROUPS * EXPERTS_PER_GROUP
D_EXPERT = 512
PAST_LEN = 16384
DEPTH = 1
ALPHA = (2.0 * DEPTH) ** 0.25
LN_EPS = 1e-5

Q0 = 0
K0 = Q0 + Q_WIDTH
V0 = K0 + KV_WIDTH
U0 = V0 + KV_WIDTH
CQ0 = U0 + POOL_WIDTH
GZ0 = CQ0 + MEM_WIDTH
IN_WIDTH = GZ0 + 3 * D_MODEL

LANES = 128
SUBLANES = 8
VMEM_LIMIT = 56 * 1024 * 1024

TM = 256
QB = WINDOW
PIECE = 16
PIECES_PER_TILE = 32
TG = PIECE * PIECES_PER_TILE
MAX_CHUNK_PIECES = 2 * TM // PIECE + N_EXPERTS - 1
N_SPARE = 2 * (PIECES_PER_TILE - 1) + 1
IN_SLOTS = 3


def _chunk_rows(n_chunks):
    spare = -(-N_SPARE // n_chunks)
    return -(-(MAX_CHUNK_PIECES + spare) * PIECE // LANES) * LANES
SB = 8
IN_CHUNK = 768
Q_CHUNK = 512
ATT_SCALE = HEAD_DIM ** -0.5
SOFTMAX_ROWS = 64
assert N_MEM == 2 * QB and MEM_HEADS == N_KV and MEM_WIDTH == KV_WIDTH and TM <= GROUP * QB

BF16 = jnp.bfloat16
F32 = jnp.float32
NEG_INF = float("-inf")


def _const_spec(shape):
    nd = len(shape)
    return pl.BlockSpec(shape, lambda *_: (0,) * nd, pipeline_mode=pl.Buffered(1))


def _layer_norm(x, g, b):
    mu = jnp.mean(x, axis=-1, keepdims=True)
    xc = x - mu
    var = jnp.mean(xc * xc, axis=-1, keepdims=True)
    return xc * lax.rsqrt(var + LN_EPS) * g + b


def _dot(a, b):
    return jnp.dot(a, b, preferred_element_type=F32)


def _dot_nt(a, b):
    return lax.dot_general(a, b, (((1,), (1,)), ((), ())), preferred_element_type=F32)


def _lane_block_mask(shape, block, width=HEAD_DIM):
    lane = lax.broadcasted_iota(jnp.int32, shape, len(shape) - 1)
    return (lane >= block * width) & (lane < (block + 1) * width)


def _rope(x, c, s1, s2):
    half = ROPE_DIM // 2
    return x * c + pltpu.roll(x, half, 1) * s1 + pltpu.roll(x, LANES - half, 1) * s2


def _route(logits):
    rows = logits.shape[0]
    lane = lax.broadcasted_iota(jnp.int32, (rows, LANES), 1)
    lanef = lane.astype(F32)
    big = float(LANES)
    is_g = lane < N_EXPERT_GROUPS
    glog = jnp.where(is_g, logits, NEG_INF)
    gmax = jnp.max(glog, axis=1, keepdims=True)
    gsum = jnp.sum(jnp.where(is_g, jnp.exp(glog - gmax), 0.0), axis=1, keepdims=True)
    gp = 1.0 / gsum
    gidx = jnp.min(jnp.where(glog == gmax, lanef, big), axis=1, keepdims=True).astype(jnp.int32)
    lo = N_EXPERT_GROUPS + gidx * EXPERTS_PER_GROUP
    in_grp = (lane >= lo) & (lane < lo + EXPERTS_PER_GROUP)
    el = jnp.where(in_grp, logits, NEG_INF)
    v1 = jnp.max(el, axis=1, keepdims=True)
    i1 = jnp.min(jnp.where(el == v1, lanef, big), axis=1, keepdims=True).astype(jnp.int32)
    el2 = jnp.where(lane == i1, NEG_INF, el)
    v2 = jnp.max(el2, axis=1, keepdims=True)
    i2 = jnp.min(jnp.where(el2 == v2, lanef, big), axis=1, keepdims=True).astype(jnp.int32)
    e21 = jnp.exp(v2 - v1)
    inv = 1.0 / (1.0 + e21)
    w1 = inv * gp
    w2 = e21 * inv * gp
    e1 = i1 - N_EXPERT_GROUPS
    e2 = i2 - N_EXPERT_GROUPS
    return lane == e1, lane == e2, w1, w2


def _route_and_sort(logits, hb, cap):
    rows = hb.shape[0]
    lt = logits.T
    row = lax.broadcasted_iota(jnp.int32, (LANES, rows), 0)
    rowf = row.astype(F32)
    big = float(LANES)
    is_g = row < N_EXPERT_GROUPS
    glog = jnp.where(is_g, lt, NEG_INF)
    gmax = jnp.max(glog, axis=0, keepdims=True)
    gp = 1.0 / jnp.sum(jnp.where(is_g, jnp.exp(glog - gmax), 0.0), axis=0, keepdims=True)
    gidx = jnp.min(jnp.where(glog == gmax, rowf, big), axis=0, keepdims=True).astype(jnp.int32)
    lo = N_EXPERT_GROUPS + gidx * EXPERTS_PER_GROUP
    el = jnp.where((row >= lo) & (row < lo + EXPERTS_PER_GROUP), lt, NEG_INF)
    v1 = jnp.max(el, axis=0, keepdims=True)
    i1 = jnp.min(jnp.where(el == v1, rowf, big), axis=0, keepdims=True).astype(jnp.int32)
    el2 = jnp.where(row == i1, NEG_INF, el)
    v2 = jnp.max(el2, axis=0, keepdims=True)
    i2 = jnp.min(jnp.where(el2 == v2, rowf, big), axis=0, keepdims=True).astype(jnp.int32)
    e21 = jnp.exp(v2 - v1)
    inv = 1.0 / (1.0 + e21)
    w1 = inv * gp
    w2 = e21 * inv * gp
    hot1 = row == i1 - N_EXPERT_GROUPS
    hot2 = row == i2 - N_EXPERT_GROUPS
    onehot = jnp.where(hot1 | hot2, 1.0, 0.0)
    counts = jnp.broadcast_to(jnp.sum(onehot, axis=1, keepdims=True), (LANES, LANES))
    earlier = (lax.broadcasted_iota(jnp.int32, (rows, rows), 0)
               < lax.broadcasted_iota(jnp.int32, (rows, rows), 1)).astype(BF16)
    rank = _dot(onehot.astype(BF16), earlier)
    run = (((counts.astype(jnp.int32) + (PIECE - 1)) // PIECE) * PIECE).astype(BF16)
    below = (lax.broadcasted_iota(jnp.int32, (LANES, LANES), 1)
             < lax.broadcasted_iota(jnp.int32, (LANES, LANES), 0)).astype(BF16)
    start = _dot(below, run)
    slot = jnp.concatenate([start] * (rows // LANES), axis=1) + rank
    s1 = jnp.sum(jnp.where(hot1, slot, 0.0), axis=0, keepdims=True)
    s2 = jnp.sum(jnp.where(hot2, slot, 0.0), axis=0, keepdims=True)
    srow = lax.broadcasted_iota(jnp.int32, (cap, rows), 0).astype(F32)
    perm = jnp.where((srow == s1) | (srow == s2), 1.0, 0.0).astype(BF16)
    route_t = jnp.where(row == 0, w1, jnp.where(row == 1, w2, jnp.where(row == 2, s1, jnp.where(row == 3, s2, 0.0))))
    return _dot(perm, hb).astype(BF16), route_t.T, counts


def _sigmoid(x):
    return 0.5 * jnp.tanh(0.5 * x) + 0.5


def _merge_ln1(x, oa, ob, oc, gz, wa_ref, wb_ref, wc_ref, wo_ref, g1, b1):
    ya = _dot(oa, wa_ref[...])
    yb = _dot(ob.astype(BF16), wb_ref[...])
    yc = _dot(oc.astype(BF16), wc_ref[...])
    m = (_sigmoid(gz[:, 0:D_MODEL]) * ya
         + _sigmoid(gz[:, D_MODEL:2 * D_MODEL]) * yb
         + _sigmoid(gz[:, 2 * D_MODEL:3 * D_MODEL]) * yc)
    hpre = ALPHA * x + _dot(m.astype(BF16), wo_ref[...])
    return _layer_norm(hpre, g1, b1)


def _mem_kernel(mem_ref, wk_ref, wv_ref, mk_ref, mv_ref):
    m = mem_ref[...].astype(BF16)
    mk_ref[...] = _dot(m, wk_ref[...])
    mv_ref[...] = _dot(m, wv_ref[...])


def _mem_project(mem, wk, wv):
    b = mem.shape[0]
    out = jax.ShapeDtypeStruct((b, N_MEM, MEM_WIDTH), F32)
    return pl.pallas_call(
        _mem_kernel,
        grid=(b,),
        in_specs=[pl.BlockSpec((None, N_MEM, D_MODEL), lambda i: (i, 0, 0)),
                  _const_spec((D_MODEL, MEM_WIDTH)), _const_spec((D_MODEL, MEM_WIDTH))],
        out_specs=[pl.BlockSpec((None, N_MEM, MEM_WIDTH), lambda i: (i, 0, 0))] * 2,
        out_shape=[out, out],
        name="mem_project",
    )(mem, wk, wv)


def _project_q(xb, wq_ref, zq_ref):
    for c0 in range(0, Q_WIDTH, Q_CHUNK):
        zq_ref[:, c0:c0 + Q_CHUNK] = _dot(xb, wq_ref[:, c0:c0 + Q_CHUNK])


def _project_rest(xb, wrest_ref, z_ref):
    rest = IN_WIDTH - K0
    for c0 in range(0, rest, IN_CHUNK):
        c1 = min(c0 + IN_CHUNK, rest)
        z_ref[:, K0 + c0:K0 + c1] = _dot(xb, wrest_ref[:, c0:c1])


def _front_kernel(sinks_ref, x_ref, xn_ref, rope_ref, wq_ref, wrest_ref, wa_ref, wb_ref, wc_ref, wo_ref, wmix_ref,
                  pscale_ref, mk_ref, mv_ref, g1_ref, b1_ref, wr_ref, br_ref,
                  h_ref, xs_ref, route_ref, counts_ref, nk_ref, nv_ref, npool_ref,
                  z_ref, qb_ref, kext_ref, vext_ref, uext_ref, oa_ref, ob_ref, bias_ref, s_ref, p_ref, vblk_ref, zq_ref):
    i = pl.program_id(1)
    x = x_ref[...]
    xb = x.astype(BF16)
    hist = 2 * SUBLANES

    @pl.when((pl.program_id(0) == 0) & (i == 0))
    def _():
        _project_q(xb, wq_ref, zq_ref)

    @pl.when(i == 0)
    def _():
        kext_ref[0:QB, :] = jnp.zeros((QB, KV_WIDTH), BF16)
        vext_ref[0:QB, :] = jnp.zeros((QB, KV_WIDTH), BF16)
        uext_ref[0:hist, :] = jnp.zeros((hist, POOL_WIDTH), F32)

    @pl.when(i > 0)
    def _():
        kext_ref[0:QB, :] = kext_ref[TM:TM + QB, :]
        vext_ref[0:QB, :] = vext_ref[TM:TM + QB, :]
        uext_ref[0:hist, :] = uext_ref[TM:TM + hist, :]

    _project_rest(xb, wrest_ref, z_ref)

    c = rope_ref[0]
    s1 = rope_ref[1]
    s2 = rope_ref[2]
    for j in range(Q_WIDTH // LANES):
        sl = slice(j * LANES, (j + 1) * LANES)
        qb_ref[:, sl] = _rope(zq_ref[:, sl], c, s1, s2).astype(BF16)
    for j in range(KV_WIDTH // LANES):
        sl = slice(K0 + j * LANES, K0 + (j + 1) * LANES)
        kr = _rope(z_ref[:, sl], c, s1, s2)
        z_ref[:, sl] = kr
        kext_ref[QB:QB + TM, j * LANES:(j + 1) * LANES] = kr.astype(BF16)
    vext_ref[QB:QB + TM, :] = z_ref[:, V0:V0 + KV_WIDTH].astype(BF16)
    uext_ref[hist:hist + TM, :] = z_ref[:, U0:U0 + POOL_WIDTH]
    nk_ref[...] = z_ref[TM - QB:TM, K0:K0 + KV_WIDTH]
    nv_ref[...] = z_ref[TM - QB:TM, V0:V0 + KV_WIDTH]

    rowq = lax.broadcasted_iota(jnp.int32, (QB, 2 * QB), 0)
    colk = lax.broadcasted_iota(jnp.int32, (QB, 2 * QB), 1)
    band = (colk >= rowq) & (colk <= rowq + WINDOW)
    bias_ref[1] = jnp.where(band, 0.0, NEG_INF)
    bias_ref[0] = jnp.where(band & ((colk >= QB) | (i > 0)), 0.0, NEG_INF)
    for sb in range(TM // QB):
        k2 = kext_ref[sb * QB:(sb + 2) * QB, :]
        v2 = vext_ref[sb * QB:(sb + 2) * QB, :]
        qs = jnp.concatenate(
            [qb_ref[sb * QB:(sb + 1) * QB, g * KV_WIDTH:(g + 1) * KV_WIDTH] for g in range(GROUP)], axis=0)
        for kv in range(N_KV):
            kmask = _lane_block_mask((2 * QB, KV_WIDTH), kv)
            s_ref[...] = _dot_nt(qs, jnp.where(kmask, k2, jnp.zeros_like(k2)))
            vblk_ref[kv * 2 * QB:(kv + 1) * 2 * QB, :] = jnp.where(kmask, v2, jnp.zeros_like(v2))
            for c0 in range(0, GROUP * QB, SOFTMAX_ROWS):
                rq = c0 % QB
                sink = sinks_ref[kv * GROUP + c0 // QB]
                s = s_ref[c0:c0 + SOFTMAX_ROWS, :] + bias_ref[min(sb, 1), rq:rq + SOFTMAX_ROWS, :]
                m = jnp.maximum(jnp.max(s, axis=1, keepdims=True), sink)
                p = jnp.exp(s - m)
                den = jnp.sum(p, axis=1, keepdims=True) + jnp.exp(sink - m)
                p_ref[c0:c0 + SOFTMAX_ROWS, kv * 2 * QB:(kv + 1) * 2 * QB] = (p * (1.0 / den)).astype(BF16)
        o = _dot(p_ref[...], vblk_ref[...])
        for g in range(GROUP):
            oa_ref[sb * QB:(sb + 1) * QB, g * KV_WIDTH:(g + 1) * KV_WIDTH] = o[g * QB:(g + 1) * QB].astype(BF16)

    npool_ref[...] = uext_ref[TM:TM + hist, :]
    pos = i * TM + lax.broadcasted_iota(jnp.int32, (TM, 1), 0)
    for g, w in enumerate(POOL_WINDOWS):
        sl = slice(g * POOL_GROUP_DIM, (g + 1) * POOL_GROUP_DIM)
        cur = uext_ref[hist:hist + TM, sl]
        ws = cur
        for j in range(1, w):
            ws = ws + uext_ref[hist - j:hist - j + TM, sl]
        cnt = jnp.minimum(pos + 1, w).astype(F32)
        pooled = ws / cnt - cur
        ob_ref[:, sl] = _dot(pooled.astype(BF16), wmix_ref[g]) * pscale_ref[:, sl]

    cq = (z_ref[:, CQ0:CQ0 + MEM_WIDTH] * ATT_SCALE).astype(BF16)
    mk = mk_ref[...].astype(BF16)
    mv = mv_ref[...].astype(BF16)
    for hh in range(MEM_HEADS):
        hmask = _lane_block_mask((N_MEM, MEM_WIDTH), hh)
        s_ref[0:TM, :] = _dot_nt(cq, jnp.where(hmask, mk, jnp.zeros_like(mk)))
        vblk_ref[hh * N_MEM:(hh + 1) * N_MEM, :] = jnp.where(hmask, mv, jnp.zeros_like(mv))
        for c0 in range(0, TM, SOFTMAX_ROWS):
            s = s_ref[c0:c0 + SOFTMAX_ROWS, :]
            p = jnp.exp(s - jnp.max(s, axis=1, keepdims=True))
            den = jnp.sum(p, axis=1, keepdims=True)
            p_ref[c0:c0 + SOFTMAX_ROWS, hh * N_MEM:(hh + 1) * N_MEM] = (p * (1.0 / den)).astype(BF16)
    oc = _dot(p_ref[0:TM, :], vblk_ref[...])

    h = _merge_ln1(x, oa_ref[...], ob_ref[...], oc, z_ref[:, GZ0:GZ0 + 3 * D_MODEL],
                   wa_ref, wb_ref, wc_ref, wo_ref, g1_ref[...], b1_ref[...])
    h_ref[...] = h
    hb = h.astype(BF16)
    logits = _dot(hb, wr_ref[...]) + br_ref[...]
    _project_q(xn_ref[...].astype(BF16), wq_ref, zq_ref)
    xs_ref[...], route_ref[...], counts_ref[...] = _route_and_sort(logits, hb, xs_ref.shape[0])


def _front(x, rope, sinks, wq, wrest, wa, wb, wc, wo, wmix, pscale, mk, mv, g1, b1, wr, br):
    b, l, _ = x.shape
    nt = l // TM
    cap = _chunk_rows(b * nt)
    hist = 2 * SUBLANES
    tile = lambda w: pl.BlockSpec((None, TM, w), lambda bi, ti: (bi, ti, 0))
    per_b = lambda r, w: pl.BlockSpec((None, r, w), lambda bi, ti: (bi, 0, 0))
    nxt = lambda bi, ti: jnp.minimum(bi * nt + ti + 1, b * nt - 1)
    return pl.pallas_call(
        _front_kernel,
        grid=(b, nt),
        in_specs=[
            pl.BlockSpec(memory_space=pltpu.SMEM),
            tile(D_MODEL),
            pl.BlockSpec((None, TM, D_MODEL), lambda bi, ti: (nxt(bi, ti) // nt, nxt(bi, ti) % nt, 0)),
            pl.BlockSpec((3, TM, LANES), lambda bi, ti: (0, ti, 0)),
            _const_spec((D_MODEL, Q_WIDTH)), _const_spec((D_MODEL, IN_WIDTH - K0)),
            _const_spec((Q_WIDTH, D_MODEL)), _const_spec((POOL_WIDTH, D_MODEL)),
            _const_spec((MEM_WIDTH, D_MODEL)), _const_spec((D_MODEL, D_MODEL)),
            _const_spec((len(POOL_WINDOWS), POOL_GROUP_DIM, POOL_GROUP_DIM)),
            _const_spec((1, POOL_WIDTH)),
            per_b(N_MEM, MEM_WIDTH), per_b(N_MEM, MEM_WIDTH),
            _const_spec((1, D_MODEL)), _const_spec((1, D_MODEL)),
            _const_spec((D_MODEL, LANES)), _const_spec((1, LANES)),
        ],
        out_specs=[
            tile(D_MODEL),
            pl.BlockSpec((cap, D_MODEL), lambda bi, ti: (bi * nt + ti, 0)),
            tile(LANES),
            pl.BlockSpec((None, None, LANES, LANES), lambda bi, ti: (bi, ti, 0, 0)),
            per_b(QB, KV_WIDTH), per_b(QB, KV_WIDTH), per_b(hist, POOL_WIDTH),
        ],
        out_shape=[
            jax.ShapeDtypeStruct((b, l, D_MODEL), F32),
            jax.ShapeDtypeStruct((b * nt * cap, D_MODEL), BF16),
            jax.ShapeDtypeStruct((b, l, LANES), F32),
            jax.ShapeDtypeStruct((b, nt, LANES, LANES), F32),
            jax.ShapeDtypeStruct((b, QB, KV_WIDTH), F32),
            jax.ShapeDtypeStruct((b, QB, KV_WIDTH), F32),
            jax.ShapeDtypeStruct((b, hist, POOL_WIDTH), F32),
        ],
        scratch_shapes=[
            pltpu.VMEM((TM, IN_WIDTH), F32),
            pltpu.VMEM((TM, Q_WIDTH), BF16),
            pltpu.VMEM((QB + TM, KV_WIDTH), BF16),
            pltpu.VMEM((QB + TM, KV_WIDTH), BF16),
            pltpu.VMEM((hist + TM, POOL_WIDTH), F32),
            pltpu.VMEM((TM, Q_WIDTH), BF16),
            pltpu.VMEM((TM, POOL_WIDTH), F32),
            pltpu.VMEM((2, QB, 2 * QB), F32),
            pltpu.VMEM((GROUP * QB, 2 * QB), F32),
            pltpu.VMEM((GROUP * QB, N_KV * 2 * QB), BF16),
            pltpu.VMEM((N_KV * 2 * QB, KV_WIDTH), BF16),
            pltpu.VMEM((TM, Q_WIDTH), F32),
        ],
        compiler_params=pltpu.CompilerParams(
            dimension_semantics=("arbitrary", "arbitrary"), vmem_limit_bytes=VMEM_LIMIT),
        name="front_prompt",
    )(sinks, x, x, rope, wq, wrest, wa, wb, wc, wo, wmix, pscale, mk, mv, g1, b1, wr, br)


def _expert_mlp(xb, wg, wu, wd):
    a = _dot(xb, wg)
    hid = (a * jax.nn.sigmoid(a)) * _dot(xb, wu)
    return _dot(hid.astype(BF16), wd)


def _gemm_kernel(src_ref, dst_ref, te_ref, act_ref, xs_ref, wg_ref, wu_ref, wd_ref, ys_ref,
                 xbuf, obuf, wgb, wub, wdb, prime, sem_in, sem_out):
    i = pl.program_id(0)
    n = pl.num_programs(0)
    slot = i % 2
    in_slot = i % IN_SLOTS

    def start_in(tile, slot):
        for j in range(PIECES_PER_TILE):
            row0 = pl.multiple_of(src_ref[tile * PIECES_PER_TILE + j], PIECE)
            pltpu.make_async_copy(xs_ref.at[pl.ds(row0, PIECE)], xbuf.at[slot, pl.ds(j * PIECE, PIECE)],
                                  sem_in.at[slot]).start(priority=j % 2)

    def start_out(tile, slot):
        for j in range(PIECES_PER_TILE):
            row0 = pl.multiple_of(dst_ref[tile * PIECES_PER_TILE + j], PIECE)
            pltpu.make_async_copy(obuf.at[slot, pl.ds(j * PIECE, PIECE)], ys_ref.at[pl.ds(row0, PIECE)],
                                  sem_out.at[slot]).start(priority=j % 2)

    def wait_in(slot):
        for j in range(PIECES_PER_TILE):
            pltpu.make_async_copy(xs_ref.at[pl.ds(0, PIECE)], xbuf.at[slot, pl.ds(j * PIECE, PIECE)],
                                  sem_in.at[slot]).wait()

    def wait_out(slot):
        for j in range(PIECES_PER_TILE):
            pltpu.make_async_copy(obuf.at[slot, pl.ds(j * PIECE, PIECE)], ys_ref.at[pl.ds(0, PIECE)],
                                  sem_out.at[slot]).wait()

    active = act_ref[i] > 0
    ahead = IN_SLOTS - 1
    prefetched = (i < ahead) | (act_ref[jnp.maximum(i - ahead, 0)] > 0)
    out_pending = (i < 2) | (act_ref[jnp.maximum(i - 2, 0)] > 0)

    @pl.when(i == 0)
    def _():
        prime[0] = jnp.zeros((PIECE, D_MODEL), BF16)
        for s in range(2):
            for j in range(PIECES_PER_TILE):
                pltpu.make_async_copy(prime.at[0], prime.at[1 + s * PIECES_PER_TILE + j], sem_out.at[s]).start()
        for t in range(ahead):
            start_in(t, t)

    prev_active = (i >= 1) & (act_ref[jnp.maximum(i - 1, 0)] > 0)

    def active_step(after_first):
        @pl.when((i == 0) | (te_ref[i] != te_ref[jnp.maximum(i - 1, 0)]))
        def _():
            wgb[...] = wg_ref[...].astype(BF16)
            wub[...] = wu_ref[...].astype(BF16)
            wdb[...] = wd_ref[...].astype(BF16)

        wait_in(in_slot)
        wait_out(slot)
        out = _expert_mlp(xbuf[in_slot], wgb[...], wub[...], wdb[...])
        start_in(i + ahead, (i + ahead) % IN_SLOTS)
        if after_first:
            start_out(i - 1, 1 - slot)
        obuf[slot] = out.astype(BF16)

    @pl.when(active & (i == 0))
    def _():
        active_step(False)

    @pl.when(active & (i > 0))
    def _():
        active_step(True)

    @pl.when(jnp.logical_not(active))
    def _():
        @pl.when(prev_active)
        def _():
            start_out(i - 1, 1 - slot)

        @pl.when(prefetched)
        def _():
            wait_in(in_slot)

        @pl.when(out_pending)
        def _():
            wait_out(slot)

    @pl.when(i == n - 1)
    def _():
        for t in range(ahead):
            @pl.when(act_ref[jnp.maximum(i - t, 0)] > 0)
            def _(t=t):
                wait_in((i - t + ahead) % IN_SLOTS)

        @pl.when(active)
        def _():
            start_out(i, slot)
            wait_out(slot)

        @pl.when(prev_active)
        def _():
            wait_out(1 - slot)


def _grouped_gemm(piece_src, piece_dst, tile_expert, tile_active, xs, wg, wu, wd):
    n_tiles = tile_expert.shape[0]
    assert n_tiles >= 2
    wspec = lambda r, c: pl.BlockSpec((None, r, c), lambda i, src, dst, te, act: (te[i], 0, 0))
    return pl.pallas_call(
        _gemm_kernel,
        grid_spec=pltpu.PrefetchScalarGridSpec(
            num_scalar_prefetch=4,
            grid=(n_tiles,),
            in_specs=[pl.BlockSpec(memory_space=pl.ANY),
                      wspec(D_MODEL, D_EXPERT), wspec(D_MODEL, D_EXPERT), wspec(D_EXPERT, D_MODEL)],
            out_specs=pl.BlockSpec(memory_space=pl.ANY),
            scratch_shapes=[pltpu.VMEM((IN_SLOTS, TG, D_MODEL), BF16), pltpu.VMEM((2, TG, D_MODEL), BF16),
                            pltpu.VMEM((D_MODEL, D_EXPERT), BF16), pltpu.VMEM((D_MODEL, D_EXPERT), BF16),
                            pltpu.VMEM((D_EXPERT, D_MODEL), BF16),
                            pltpu.VMEM((1 + 2 * PIECES_PER_TILE, PIECE, D_MODEL), BF16),
                            pltpu.SemaphoreType.DMA((IN_SLOTS,)), pltpu.SemaphoreType.DMA((2,))],
        ),
        out_shape=jax.ShapeDtypeStruct(xs.shape, xs.dtype),
        input_output_aliases={4: 0},
        compiler_params=pltpu.CompilerParams(dimension_semantics=("arbitrary",), vmem_limit_bytes=VMEM_LIMIT),
        name="moe_grouped_gemm",
    )(piece_src, piece_dst, tile_expert, tile_active, xs, wg, wu, wd)


def _combine_kernel(ys_ref, h_ref, route_ref, g2_ref, b2_ref, y_ref):
    chunks = h_ref.shape[0] // TM
    cap = ys_ref.shape[0] // chunks
    slot = lax.broadcasted_iota(jnp.int32, (TM, cap), 1).astype(F32)
    for c in range(chunks):
        rows = slice(c * TM, (c + 1) * TM)
        route = route_ref[rows, :]
        sel = jnp.where(slot == route[:, 2:3], route[:, 0:1], jnp.where(slot == route[:, 3:4], route[:, 1:2], 0.0))
        f = _dot(sel.astype(BF16), ys_ref[c * cap:(c + 1) * cap, :])
        y_ref[rows, :] = _layer_norm(ALPHA * h_ref[rows, :] + f, g2_ref[...], b2_ref[...])


def _combine(ys, h, route, g2, b2):
    t = h.shape[0]
    cap = ys.shape[0] // (t // TM)
    per_step = next(k for k in (4, 2, 1) if (t // TM) % k == 0)
    rows = per_step * TM
    return pl.pallas_call(
        _combine_kernel,
        grid=(t // rows,),
        in_specs=[
            pl.BlockSpec((per_step * cap, D_MODEL), lambda i: (i, 0)),
            pl.BlockSpec((rows, D_MODEL), lambda i: (i, 0)),
            pl.BlockSpec((rows, LANES), lambda i: (i, 0)),
            pl.BlockSpec((1, D_MODEL), lambda i: (0, 0)),
            pl.BlockSpec((1, D_MODEL), lambda i: (0, 0)),
        ],
        out_specs=pl.BlockSpec((rows, D_MODEL), lambda i: (i, 0)),
        out_shape=jax.ShapeDtypeStruct((t, D_MODEL), F32),
        compiler_params=pltpu.CompilerParams(dimension_semantics=("arbitrary",)),
        name="moe_combine",
    )(ys, h, route, g2, b2)


def _select(table, idx):
    hot = idx[:, None] == jnp.arange(table.shape[0], dtype=jnp.int32)[None, :]
    return jnp.sum(jnp.where(hot[:, :, None], table[None, :, :], 0), axis=1)


def _piece_tables(counts):
    n_chunks = counts.shape[0]
    n_tiles = -(-(n_chunks * MAX_CHUNK_PIECES + N_EXPERTS * (PIECES_PER_TILE - 1)) // PIECES_PER_TILE)
    npc = (counts + (PIECE - 1)) // PIECE
    first = (jnp.cumsum(npc, axis=1) - npc).T
    npc_t = npc.T
    cum = jnp.cumsum(npc_t, axis=1)
    per_expert = cum[:, -1]
    tiles_e = (per_expert + (PIECES_PER_TILE - 1)) // PIECES_PER_TILE
    tile_end = jnp.cumsum(tiles_e)
    tile_idx = jnp.arange(n_tiles, dtype=jnp.int32)
    expert_of = lambda i: jnp.minimum(jnp.sum(i[:, None] >= tile_end[None, :], axis=1), N_EXPERTS - 1).astype(jnp.int32)
    active = tile_idx < tile_end[-1]
    tile_expert = jnp.where(active, expert_of(tile_idx), expert_of(tile_end[-1:] - 1))
    meta = jnp.stack([tile_end - tiles_e, per_expert], axis=1)
    meta_t = _select(meta, tile_expert)
    k = (tile_idx - meta_t[:, 0])[:, None] * PIECES_PER_TILE + jnp.arange(PIECES_PER_TILE, dtype=jnp.int32)[None, :]
    valid = active[:, None] & (k < meta_t[:, 1:2])
    cum_t = _select(cum, tile_expert)
    chunk = jnp.minimum(jnp.sum(k[:, :, None] >= cum_t[:, None, :], axis=2), n_chunks - 1).astype(jnp.int32)
    at_chunk = chunk[:, :, None] == jnp.arange(n_chunks, dtype=jnp.int32)[None, None, :]
    pick = lambda tab: jnp.sum(jnp.where(at_chunk, _select(tab, tile_expert)[:, None, :], 0), axis=2)
    piece = pick(first) + k - pick(cum - npc_t)
    cap = _chunk_rows(n_chunks)
    rows = chunk * cap + piece * PIECE
    d = ((tile_idx % 2)[:, None] * (PIECES_PER_TILE - 1)
         + jnp.maximum(jnp.arange(PIECES_PER_TILE, dtype=jnp.int32)[None, :] - 1, 0))
    spare_row = lambda d: (d % n_chunks) * cap + (MAX_CHUNK_PIECES + d // n_chunks) * PIECE
    spare = spare_row(d)
    zero_piece = spare_row(N_SPARE - 1)
    extra = jnp.full(((IN_SLOTS - 1) * PIECES_PER_TILE,), zero_piece, jnp.int32)
    src = jnp.concatenate([jnp.where(valid, rows, zero_piece).astype(jnp.int32).reshape(-1), extra])
    dst = jnp.concatenate([jnp.where(valid, rows, spare).astype(jnp.int32).reshape(-1), extra])
    return src, dst, tile_expert.astype(jnp.int32), active.astype(jnp.int32)


def _sample_attn_kernel(x_ref, rope_ref, wq_ref, wrest_ref, sink_ref, ck_ref, cv_ref, cmk_ref, cmv_ref,
                        z_ref, oa_ref, oc_ref, nk_ref, nv_ref, knew_t, vnew_t):
    j = pl.program_id(0)
    db = x_ref.shape[0]

    @pl.when(j == 0)
    def _():
        xb = x_ref[...].astype(BF16)
        _project_q(xb, wq_ref, z_ref)
        _project_rest(xb, wrest_ref, z_ref)
        c = rope_ref[0]
        s1 = rope_ref[1]
        s2 = rope_ref[2]
        for jj in range((Q_WIDTH + KV_WIDTH) // LANES):
            sl = slice(jj * LANES, (jj + 1) * LANES)
            z_ref[:, sl] = _rope(z_ref[:, sl], c, s1, s2)
        knew_t[...] = z_ref[:, K0:K0 + KV_WIDTH].T
        vnew_t[...] = z_ref[:, V0:V0 + KV_WIDTH].T

    r0 = pl.multiple_of(j * SB, SB)
    zq = z_ref[pl.ds(r0, SB), Q0:Q0 + Q_WIDTH]
    zk = z_ref[pl.ds(r0, SB), K0:K0 + KV_WIDTH]
    zv = z_ref[pl.ds(r0, SB), V0:V0 + KV_WIDTH]
    zc = z_ref[pl.ds(r0, SB), CQ0:CQ0 + MEM_WIDTH] * ATT_SCALE
    sink = sink_ref[:, 0:1]
    row_kv = lax.broadcasted_iota(jnp.int32, (N_HEADS, KV_WIDTH), 0) & (N_KV - 1)
    lane_kv = lax.broadcasted_iota(jnp.int32, (N_HEADS, KV_WIDTH), 1) // HEAD_DIM
    own = row_kv == lane_kv
    row_c = lax.broadcasted_iota(jnp.int32, (N_HEADS, MEM_WIDTH), 0)
    lane_c = lax.broadcasted_iota(jnp.int32, (N_HEADS, MEM_WIDTH), 1) // HEAD_DIM
    own_c = row_c == lane_c
    last_pos = lax.broadcasted_iota(jnp.int32, (KV_WIDTH, WINDOW), 1) == WINDOW - 1
    seq_lane = lax.broadcasted_iota(jnp.int32, (KV_WIDTH, db), 1)

    seqs = range(SB)
    qblk, cblk, s, sc = [], [], [], []
    for b in seqs:
        q4 = jnp.concatenate(
            [jnp.broadcast_to(zq[b:b + 1, g * KV_WIDTH:(g + 1) * KV_WIDTH], (N_KV, KV_WIDTH)) for g in range(GROUP)],
            axis=0)
        qblk.append(jnp.where(own, q4, 0.0).astype(BF16))
        cblk.append(jnp.where(own_c, jnp.broadcast_to(zc[b:b + 1, :], (N_HEADS, MEM_WIDTH)), 0.0).astype(BF16))
        s.append(_dot(qblk[b], ck_ref[b].astype(BF16)))
        sc.append(_dot(cblk[b], cmk_ref[b].astype(BF16)))

    pn, p_new, pc = [], [], []
    for b in seqs:
        s_new = jnp.sum(qblk[b].astype(F32) * zk[b:b + 1, :].astype(BF16).astype(F32), axis=1, keepdims=True)
        m = jnp.maximum(jnp.maximum(jnp.max(s[b], axis=1, keepdims=True), s_new), sink)
        p = jnp.exp(s[b] - m)
        e_new = jnp.exp(s_new - m)
        inv = 1.0 / (jnp.sum(p, axis=1, keepdims=True) + e_new + jnp.exp(sink - m))
        pn.append((p * inv).astype(BF16))
        p_new.append((e_new * inv).astype(BF16).astype(F32))
        e = jnp.exp(sc[b] - jnp.max(sc[b], axis=1, keepdims=True))
        pc.append((e * (1.0 / jnp.sum(e, axis=1, keepdims=True))).astype(BF16))

    oa_rows = [[] for _ in range(GROUP)]
    oc_rows = []
    for b in seqs:
        vc = cv_ref[b]
        o = _dot_nt(pn[b], vc.astype(BF16)) + p_new[b] * zv[b:b + 1, :].astype(BF16).astype(F32)
        o = jnp.where(own, o, 0.0)
        for g in range(GROUP):
            oa_rows[g].append(jnp.sum(o[g * N_KV:(g + 1) * N_KV], axis=0, keepdims=True))
        ocb = jnp.where(own_c, _dot_nt(pc[b], cmv_ref[b].astype(BF16)), 0.0)
        oc_rows.append(jnp.sum(ocb, axis=0, keepdims=True))
        here = seq_lane == j * SB + b
        knew_col = jnp.sum(jnp.where(here, knew_t[...], 0.0), axis=1, keepdims=True)
        vnew_col = jnp.sum(jnp.where(here, vnew_t[...], 0.0), axis=1, keepdims=True)
        nk_ref[b] = jnp.where(last_pos, knew_col, pltpu.roll(ck_ref[b], WINDOW - 1, 1))
        nv_ref[b] = jnp.where(last_pos, vnew_col, pltpu.roll(vc, WINDOW - 1, 1))
    for g in range(GROUP):
        oa_ref[pl.ds(r0, SB), g * KV_WIDTH:(g + 1) * KV_WIDTH] = jnp.concatenate(oa_rows[g], axis=0)
    oc_ref[pl.ds(r0, SB), :] = jnp.concatenate(oc_rows, axis=0)


def _sample_attn(x, rope, wq, wrest, sink_gk, ck, cv, cmk, cmv):
    db = x.shape[0]
    blk = lambda r: pl.BlockSpec((SB, KV_WIDTH, r), lambda j: (j, 0, 0))
    full = lambda w: pl.BlockSpec((db, w), lambda j: (0, 0))
    return pl.pallas_call(
        _sample_attn_kernel,
        grid=(db // SB,),
        in_specs=[
            full(D_MODEL), _const_spec((3, 1, LANES)),
            _const_spec((D_MODEL, Q_WIDTH)), _const_spec((D_MODEL, IN_WIDTH - K0)),
            _const_spec((N_HEADS, LANES)),
            blk(WINDOW), blk(WINDOW), blk(N_MEM), blk(N_MEM),
        ],
        out_specs=[full(IN_WIDTH), full(Q_WIDTH), full(MEM_WIDTH), blk(WINDOW), blk(WINDOW)],
        out_shape=[
            jax.ShapeDtypeStruct((db, IN_WIDTH), F32),
            jax.ShapeDtypeStruct((db, Q_WIDTH), F32),
            jax.ShapeDtypeStruct((db, MEM_WIDTH), F32),
            jax.ShapeDtypeStruct((db, KV_WIDTH, WINDOW), F32),
            jax.ShapeDtypeStruct((db, KV_WIDTH, WINDOW), F32),
        ],
        scratch_shapes=[pltpu.VMEM((KV_WIDTH, db), F32), pltpu.VMEM((KV_WIDTH, db), F32)],
        compiler_params=pltpu.CompilerParams(dimension_semantics=("arbitrary",), vmem_limit_bytes=VMEM_LIMIT),
        name="sample_attn",
    )(x, rope, wq, wrest, sink_gk, ck, cv, cmk, cmv)


def _sample_tail_kernel(x_ref, z_ref, oa_ref, oc_ref, st_ref, wa_ref, wb_ref, wc_ref, wo_ref, wmix_ref,
                        pscale_ref, g1_ref, b1_ref, wr_ref, br_ref, wg_ref, wu_ref, wd_ref, g2_ref, b2_ref,
                        y_ref, npool_ref, h_sc, comb_sc, acc_sc):
    e = pl.program_id(0)

    @pl.when(e == 0)
    def _():
        u = z_ref[:, U0:U0 + POOL_WIDTH]
        npool_ref[0:POOL_STATE - 1] = st_ref[1:POOL_STATE]
        npool_ref[POOL_STATE - 1] = u
        obs = []
        for g, w in enumerate(POOL_WINDOWS):
            sl = slice(g * POOL_GROUP_DIM, (g + 1) * POOL_GROUP_DIM)
            cur = u[:, sl]
            ws = cur
            for jj in range(1, w):
                ws = ws + st_ref[POOL_STATE - jj, :, sl]
            cnt = float(min(PAST_LEN + 1, w))
            pooled = ws / cnt - cur
            obs.append(_dot(pooled.astype(BF16), wmix_ref[g]) * pscale_ref[:, sl])
        ob = jnp.concatenate(obs, axis=1)
        h = _merge_ln1(x_ref[...], oa_ref[...].astype(BF16), ob, oc_ref[...], z_ref[:, GZ0:GZ0 + 3 * D_MODEL],
                       wa_ref, wb_ref, wc_ref, wo_ref, g1_ref[...], b1_ref[...])
        h_sc[...] = h
        logits = _dot(h.astype(BF16), wr_ref[...]) + br_ref[...]
        hot1, hot2, w1, w2 = _route(logits)
        comb_sc[...] = jnp.where(hot1, w1, 0.0) + jnp.where(hot2, w2, 0.0)
        acc_sc[...] = jnp.zeros_like(acc_sc)

    out = _expert_mlp(h_sc[...].astype(BF16), wg_ref[...].astype(BF16), wu_ref[...].astype(BF16),
                      wd_ref[...].astype(BF16))
    lane = lax.broadcasted_iota(jnp.int32, comb_sc.shape, 1)
    ce = jnp.sum(jnp.where(lane == e, comb_sc[...], 0.0), axis=1, keepdims=True)
    acc_sc[...] += ce * out

    @pl.when(e == pl.num_programs(0) - 1)
    def _():
        y_ref[...] = _layer_norm(ALPHA * h_sc[...] + acc_sc[...], g2_ref[...], b2_ref[...])


def _sample_tail(x, z, oa, oc, state, wa, wb, wc, wo, wmix, pscale, g1, b1, wr, br, wg, wu, wd, g2, b2):
    db = x.shape[0]
    full = lambda w: pl.BlockSpec((db, w), lambda e: (0, 0))
    vec = lambda w: pl.BlockSpec((1, w), lambda e: (0, 0))
    hist = pl.BlockSpec((POOL_STATE, db, POOL_WIDTH), lambda e: (0, 0, 0))
    return pl.pallas_call(
        _sample_tail_kernel,
        grid=(N_EXPERTS,),
        in_specs=[
            full(D_MODEL), full(IN_WIDTH), full(Q_WIDTH), full(MEM_WIDTH), hist,
            _const_spec((Q_WIDTH, D_MODEL)), _const_spec((POOL_WIDTH, D_MODEL)),
            _const_spec((MEM_WIDTH, D_MODEL)), _const_spec((D_MODEL, D_MODEL)),
            _const_spec((len(POOL_WINDOWS), POOL_GROUP_DIM, POOL_GROUP_DIM)),
            vec(POOL_WIDTH), vec(D_MODEL), vec(D_MODEL),
            _const_spec((D_MODEL, LANES)), vec(LANES),
            pl.BlockSpec((None, D_MODEL, D_EXPERT), lambda e: (e, 0, 0)),
            pl.BlockSpec((None, D_MODEL, D_EXPERT), lambda e: (e, 0, 0)),
            pl.BlockSpec((None, D_EXPERT, D_MODEL), lambda e: (e, 0, 0)),
            vec(D_MODEL), vec(D_MODEL),
        ],
        out_specs=[full(D_MODEL), hist],
        out_shape=[jax.ShapeDtypeStruct((db, D_MODEL), F32),
                   jax.ShapeDtypeStruct((POOL_STATE, db, POOL_WIDTH), F32)],
        scratch_shapes=[pltpu.VMEM((db, D_MODEL), F32), pltpu.VMEM((db, LANES), F32),
                        pltpu.VMEM((db, D_MODEL), F32)],
        compiler_params=pltpu.CompilerParams(dimension_semantics=("arbitrary",), vmem_limit_bytes=VMEM_LIMIT),
        name="sample_tail",
    )(x, z, oa, oc, state, wa, wb, wc, wo, wmix, pscale, g1, b1, wr, br, wg, wu, wd, g2, b2)


def _rope_tables(pos):
    half = ROPE_DIM // 2
    inv = jnp.power(ROPE_THETA, -jnp.arange(half, dtype=F32) * (2.0 / ROPE_DIM))
    ang = pos.astype(F32)[:, None] * jnp.tile(inv, LANES // half)[None, :]
    off = lax.broadcasted_iota(jnp.int32, (1, LANES), 1) % HEAD_DIM
    cos = jnp.cos(ang)
    sin = jnp.sin(ang)
    c = jnp.where(off < ROPE_DIM, cos, 1.0)
    s1 = jnp.where((off >= half) & (off < ROPE_DIM), sin, 0.0)
    s2 = jnp.where(off < half, -sin, 0.0)
    return jnp.stack([c, s1, s2])


def _q_heads_group_major(w, axis):
    if axis == 1:
        n = w.shape[0]
        return w.reshape(n, N_KV, GROUP, HEAD_DIM).transpose(0, 2, 1, 3).reshape(n, Q_WIDTH)
    n = w.shape[1]
    return w.reshape(N_KV, GROUP, HEAD_DIM, n).transpose(1, 0, 2, 3).reshape(Q_WIDTH, n)


def kernel(x_prompt, x_sample, cache_win_k, cache_win_v, state_pool, cache_mem_k, cache_mem_v, mem_prompt, w_in, sinks, w_pool_mix, pool_scale, w_mem_k, w_mem_v, w_branch_a, w_branch_b, w_branch_c, w_out, ln1_g, ln1_b, w_group, b_group, w_router, b_router, w_gate, w_up, w_down, ln2_g, ln2_b):
    assert w_in.shape[0] == DEPTH == 1
    b, l, _ = x_prompt.shape
    db, ds, _ = x_sample.shape
    assert ds == 1 and l % TM == 0 and db % SB == 0
    assert cache_win_k.shape[2] == WINDOW
    t = b * l

    win = w_in[0]
    wq = (_q_heads_group_major(win[:, Q0:Q0 + Q_WIDTH], 1) * ATT_SCALE).astype(BF16)
    wrest = win[:, K0:].astype(BF16)
    wa = _q_heads_group_major(w_branch_a[0], 0).astype(BF16)
    wb = w_branch_b[0].astype(BF16)
    wc = w_branch_c[0].astype(BF16)
    wo = w_out[0].astype(BF16)
    wmix = w_pool_mix[0].astype(BF16)
    pscale = pool_scale[0].reshape(1, POOL_WIDTH)
    g1 = ln1_g[0].reshape(1, D_MODEL)
    b1 = ln1_b[0].reshape(1, D_MODEL)
    g2 = ln2_g[0].reshape(1, D_MODEL)
    b2 = ln2_b[0].reshape(1, D_MODEL)
    wr = jnp.concatenate([w_group[0], w_router[0].reshape(D_MODEL, N_EXPERTS)], axis=1)
    wr = jnp.pad(wr, ((0, 0), (0, LANES - wr.shape[1]))).astype(BF16)
    br = jnp.pad(jnp.concatenate([b_group[0], b_router[0].reshape(N_EXPERTS)]), (0, LANES - N_EXPERT_GROUPS - N_EXPERTS))
    br = br.reshape(1, LANES).astype(F32)
    wg = w_gate[0]
    wu = w_up[0]
    wd = w_down[0]
    sink = sinks[0].astype(F32)
    sink_gk = jnp.broadcast_to(sink.reshape(N_KV, GROUP).T.reshape(N_HEADS, 1), (N_HEADS, LANES))

    mk, mv = _mem_project(mem_prompt, w_mem_k[0].astype(BF16), w_mem_v[0].astype(BF16))
    rope_p = _rope_tables(jnp.arange(l, dtype=jnp.int32))
    h, xs, route, counts, nk_p, nv_p, npool_p = _front(
        x_prompt, rope_p, sink, wq, wrest, wa, wb, wc, wo, wmix, pscale, mk, mv, g1, b1, wr, br)
    piece_src, piece_dst, tile_expert, tile_active = _piece_tables(
        counts.reshape(-1, LANES, LANES)[:, :N_EXPERTS, 0].astype(jnp.int32))
    ys = _grouped_gemm(piece_src, piece_dst, tile_expert, tile_active, xs, wg, wu, wd)
    y_p = _combine(ys, h.reshape(t, D_MODEL), route.reshape(t, LANES), g2, b2).reshape(b, l, D_MODEL)

    rope_s = _rope_tables(jnp.full((1,), PAST_LEN, jnp.int32))
    xs = x_sample.reshape(db, D_MODEL)
    feat_major = lambda c: jnp.transpose(c[0], (0, 2, 3, 1)).reshape(db, KV_WIDTH, c.shape[2])
    from_feat_major = lambda a: jnp.transpose(a.reshape(db, N_KV, HEAD_DIM, a.shape[2]), (0, 3, 1, 2))[None]
    z_s, oa_s, oc_s, nk_s, nv_s = _sample_attn(xs, rope_s, wq, wrest, sink_gk, feat_major(cache_win_k),
                                               feat_major(cache_win_v), feat_major(cache_mem_k), feat_major(cache_mem_v))
    state = jnp.transpose(state_pool[0], (1, 0, 2))
    y_s, npool_s = _sample_tail(xs, z_s, oa_s, oc_s, state, wa, wb, wc, wo, wmix, pscale, g1, b1, wr, br,
                                wg, wu, wd, g2, b2)

    kv5 = lambda a, n, w: a.reshape(1, n, w, N_KV, HEAD_DIM)
    return (y_p, y_s.reshape(db, 1, D_MODEL),
            kv5(nk_p, b, QB), kv5(nv_p, b, QB),
            npool_p[:, 2 * SUBLANES - POOL_STATE:, :][None],
            kv5(mk, b, N_MEM), kv5(mv, b, N_MEM),
            from_feat_major(nk_s), from_feat_major(nv_s),
            jnp.transpose(npool_s, (1, 0, 2))[None])
```

```python
import jax
import jax.numpy as jnp
from jax import lax
from jax.experimental import pallas as pl
from jax.experimental.pallas import tpu as pltpu

D_MODEL = 1024
N_HEADS = 16
HEAD_DIM = 64
N_KV = 4
GROUP = N_HEADS // N_KV
WINDOW = 128
ROPE_THETA = 500000.0
ROPE_DIM = HEAD_DIM // 4
Q_WIDTH = N_HEADS * HEAD_DIM
KV_WIDTH = N_KV * HEAD_DIM
POOL_WINDOWS = (2, 4, 8, 16)
POOL_WIDTH = D_MODEL // 2
POOL_GROUP_DIM = POOL_WIDTH // len(POOL_WINDOWS)
POOL_STATE = max(POOL_WINDOWS) - 1
N_MEM = 256
MEM_HEADS = 4
MEM_WIDTH = MEM_HEADS * HEAD_DIM
N_EXPERT_GROUPS = 4
EXPERTS_PER_GROUP = 4
N_EXPERTS = N_EXPERT_GROUPS * EXPERTS_PER_GROUP
D_EXPERT = 512
PAST_LEN = 16384
DEPTH = 1
ALPHA = (2.0 * DEPTH) ** 0.25
LN_EPS = 1e-5

Q0 = 0
K0 = Q0 + Q_WIDTH
V0 = K0 + KV_WIDTH
U0 = V0 + KV_WIDTH
CQ0 = U0 + POOL_WIDTH
GZ0 = CQ0 + MEM_WIDTH
IN_WIDTH = GZ0 + 3 * D_MODEL

LANES = 128
SUBLANES = 8
VMEM_LIMIT = 56 * 1024 * 1024

TM = 256
QB = WINDOW
PIECE = 16
PIECES_PER_TILE = 32
TG = PIECE * PIECES_PER_TILE
MAX_CHUNK_PIECES = 2 * TM // PIECE + N_EXPERTS - 1
N_SPARE = 2 * (PIECES_PER_TILE - 1) + 1
IN_SLOTS = 3
COMBINE_SLOTS = 3


def _chunk_rows(n_chunks):
    spare = -(-N_SPARE // n_chunks)
    return -(-(MAX_CHUNK_PIECES + spare) * PIECE // LANES) * LANES
SB = 8
IN_CHUNK = 768
Q_CHUNK = 512
ATT_SCALE = HEAD_DIM ** -0.5
SOFTMAX_ROWS = 64
assert N_MEM == 2 * QB and MEM_HEADS == N_KV and MEM_WIDTH == KV_WIDTH and TM <= GROUP * QB

BF16 = jnp.bfloat16
F32 = jnp.float32
NEG_INF = float("-inf")


def _const_spec(shape):
    nd = len(shape)
    return pl.BlockSpec(shape, lambda *_: (0,) * nd, pipeline_mode=pl.Buffered(1))


def _layer_norm(x, g, b):
    mu = jnp.mean(x, axis=-1, keepdims=True)
    xc = x - mu
    var = jnp.mean(xc * xc, axis=-1, keepdims=True)
    return xc * lax.rsqrt(var + LN_EPS) * g + b


def _dot(a, b):
    return jnp.dot(a, b, preferred_element_type=F32)


def _dot_nt(a, b):
    return lax.dot_general(a, b, (((1,), (1,)), ((), ())), preferred_element_type=F32)


def _lane_block_mask(shape, block, width=HEAD_DIM):
    lane = lax.broadcasted_iota(jnp.int32, shape, len(shape) - 1)
    return (lane >= block * width) & (lane < (block + 1) * width)


def _rope(x, c, s1, s2):
    half = ROPE_DIM // 2
    return x * c + pltpu.roll(x, half, 1) * s1 + pltpu.roll(x, LANES - half, 1) * s2


def _route(logits):
    rows = logits.shape[0]
    lane = lax.broadcasted_iota(jnp.int32, (rows, LANES), 1)
    lanef = lane.astype(F32)
    big = float(LANES)
    is_g = lane < N_EXPERT_GROUPS
    glog = jnp.where(is_g, logits, NEG_INF)
    gmax = jnp.max(glog, axis=1, keepdims=True)
    gsum = jnp.sum(jnp.where(is_g, jnp.exp(glog - gmax), 0.0), axis=1, keepdims=True)
    gp = 1.0 / gsum
    gidx = jnp.min(jnp.where(glog == gmax, lanef, big), axis=1, keepdims=True).astype(jnp.int32)
    lo = N_EXPERT_GROUPS + gidx * EXPERTS_PER_GROUP
    in_grp = (lane >= lo) & (lane < lo + EXPERTS_PER_GROUP)
    el = jnp.where(in_grp, logits, NEG_INF)
    v1 = jnp.max(el, axis=1, keepdims=True)
    i1 = jnp.min(jnp.where(el == v1, lanef, big), axis=1, keepdims=True).astype(jnp.int32)
    el2 = jnp.where(lane == i1, NEG_INF, el)
    v2 = jnp.max(el2, axis=1, keepdims=True)
    i2 = jnp.min(jnp.where(el2 == v2, lanef, big), axis=1, keepdims=True).astype(jnp.int32)
    e21 = jnp.exp(v2 - v1)
    inv = 1.0 / (1.0 + e21)
    w1 = inv * gp
    w2 = e21 * inv * gp
    e1 = i1 - N_EXPERT_GROUPS
    e2 = i2 - N_EXPERT_GROUPS
    return lane == e1, lane == e2, w1, w2


def _route_and_sort(logits, hb, cap):
    rows = hb.shape[0]
    lt = logits.T
    row = lax.broadcasted_iota(jnp.int32, (LANES, rows), 0)
    rowf = row.astype(F32)
    big = float(LANES)
    is_g = row < N_EXPERT_GROUPS
    glog = jnp.where(is_g, lt, NEG_INF)
    gmax = jnp.max(glog, axis=0, keepdims=True)
    gp = 1.0 / jnp.sum(jnp.where(is_g, jnp.exp(glog - gmax), 0.0), axis=0, keepdims=True)
    gidx = jnp.min(jnp.where(glog == gmax, rowf, big), axis=0, keepdims=True).astype(jnp.int32)
    lo = N_EXPERT_GROUPS + gidx * EXPERTS_PER_GROUP
    el = jnp.where((row >= lo) & (row < lo + EXPERTS_PER_GROUP), lt, NEG_INF)
    v1 = jnp.max(el, axis=0, keepdims=True)
    i1 = jnp.min(jnp.where(el == v1, rowf, big), axis=0, keepdims=True).astype(jnp.int32)
    el2 = jnp.where(row == i1, NEG_INF, el)
    v2 = jnp.max(el2, axis=0, keepdims=True)
    i2 = jnp.min(jnp.where(el2 == v2, rowf, big), axis=0, keepdims=True).astype(jnp.int32)
    e21 = jnp.exp(v2 - v1)
    inv = 1.0 / (1.0 + e21)
    w1 = inv * gp
    w2 = e21 * inv * gp
    hot1 = row == i1 - N_EXPERT_GROUPS
    hot2 = row == i2 - N_EXPERT_GROUPS
    onehot = jnp.where(hot1 | hot2, 1.0, 0.0)
    counts = jnp.broadcast_to(jnp.sum(onehot, axis=1, keepdims=True), (LANES, LANES))
    earlier = (lax.broadcasted_iota(jnp.int32, (rows, rows), 0)
               < lax.broadcasted_iota(jnp.int32, (rows, rows), 1)).astype(BF16)
    rank = _dot(onehot.astype(BF16), earlier)
    run = (((counts.astype(jnp.int32) + (PIECE - 1)) // PIECE) * PIECE).astype(BF16)
    below = (lax.broadcasted_iota(jnp.int32, (LANES, LANES), 1)
             < lax.broadcasted_iota(jnp.int32, (LANES, LANES), 0)).astype(BF16)
    start = _dot(below, run)
    slot = jnp.concatenate([start] * (rows // LANES), axis=1) + rank
    s1 = jnp.sum(jnp.where(hot1, slot, 0.0), axis=0, keepdims=True)
    s2 = jnp.sum(jnp.where(hot2, slot, 0.0), axis=0, keepdims=True)
    srow = lax.broadcasted_iota(jnp.int32, (cap, rows), 0).astype(F32)
    perm = jnp.where((srow == s1) | (srow == s2), 1.0, 0.0).astype(BF16)
    route_t = jnp.where(row == 0, w1, jnp.where(row == 1, w2, jnp.where(row == 2, s1, jnp.where(row == 3, s2, 0.0))))
    return _dot(perm, hb).astype(BF16), route_t.T, counts


def _sigmoid(x):
    return 0.5 * jnp.tanh(0.5 * x) + 0.5


def _merge_ln1(x, oa, ob, oc, gz, wa_ref, wb_ref, wc_ref, wo_ref, g1, b1):
    ya = _dot(oa, wa_ref[...])
    yb = _dot(ob.astype(BF16), wb_ref[...])
    yc = _dot(oc.astype(BF16), wc_ref[...])
    m = (_sigmoid(gz[:, 0:D_MODEL]) * ya
         + _sigmoid(gz[:, D_MODEL:2 * D_MODEL]) * yb
         + _sigmoid(gz[:, 2 * D_MODEL:3 * D_MODEL]) * yc)
    hpre = ALPHA * x + _dot(m.astype(BF16), wo_ref[...])
    return _layer_norm(hpre, g1, b1)


def _mem_kernel(mem_ref, wk_ref, wv_ref, mk_ref, mv_ref):
    m = mem_ref[...].astype(BF16)
    mk_ref[...] = _dot(m, wk_ref[...])
    mv_ref[...] = _dot(m, wv_ref[...])


def _mem_project(mem, wk, wv):
    b = mem.shape[0]
    out = jax.ShapeDtypeStruct((b, N_MEM, MEM_WIDTH), F32)
    return pl.pallas_call(
        _mem_kernel,
        grid=(b,),
        in_specs=[pl.BlockSpec((None, N_MEM, D_MODEL), lambda i: (i, 0, 0)),
                  _const_spec((D_MODEL, MEM_WIDTH)), _const_spec((D_MODEL, MEM_WIDTH))],
        out_specs=[pl.BlockSpec((None, N_MEM, MEM_WIDTH), lambda i: (i, 0, 0))] * 2,
        out_shape=[out, out],
        name="mem_project",
    )(mem, wk, wv)


def _project_q(xb, wq_ref, zq_ref):
    for c0 in range(0, Q_WIDTH, Q_CHUNK):
        zq_ref[:, c0:c0 + Q_CHUNK] = _dot(xb, wq_ref[:, c0:c0 + Q_CHUNK])


def _project_rest(xb, wrest_ref, z_ref):
    rest = IN_WIDTH - K0
    for c0 in range(0, rest, IN_CHUNK):
        c1 = min(c0 + IN_CHUNK, rest)
        z_ref[:, K0 + c0:K0 + c1] = _dot(xb, wrest_ref[:, c0:c1])


def _front_kernel(sinks_ref, x_ref, xn_ref, rope_ref, wq_ref, wrest_ref, wa_ref, wb_ref, wc_ref, wo_ref, wmix_ref,
                  pscale_ref, mk_ref, mv_ref, g1_ref, b1_ref, wr_ref, br_ref,
                  h_ref, xs_ref, route_ref, counts_ref, nk_ref, nv_ref, npool_ref,
                  z_ref, qb_ref, kext_ref, vext_ref, uext_ref, oa_ref, ob_ref, bias_ref, s_ref, p_ref, vblk_ref, zq_ref):
    i = pl.program_id(1)
    x = x_ref[...]
    xb = x.astype(BF16)
    hist = 2 * SUBLANES

    @pl.when((pl.program_id(0) == 0) & (i == 0))
    def _():
        _project_q(xb, wq_ref, zq_ref)

    @pl.when(i == 0)
    def _():
        kext_ref[0:QB, :] = jnp.zeros((QB, KV_WIDTH), BF16)
        vext_ref[0:QB, :] = jnp.zeros((QB, KV_WIDTH), BF16)
        uext_ref[0:hist, :] = jnp.zeros((hist, POOL_WIDTH), F32)

    @pl.when(i > 0)
    def _():
        kext_ref[0:QB, :] = kext_ref[TM:TM + QB, :]
        vext_ref[0:QB, :] = vext_ref[TM:TM + QB, :]
        uext_ref[0:hist, :] = uext_ref[TM:TM + hist, :]

    _project_rest(xb, wrest_ref, z_ref)

    c = rope_ref[0]
    s1 = rope_ref[1]
    s2 = rope_ref[2]
    for j in range(Q_WIDTH // LANES):
        sl = slice(j * LANES, (j + 1) * LANES)
        qb_ref[:, sl] = _rope(zq_ref[:, sl], c, s1, s2).astype(BF16)
    for j in range(KV_WIDTH // LANES):
        sl = slice(K0 + j * LANES, K0 + (j + 1) * LANES)
        kr = _rope(z_ref[:, sl], c, s1, s2)
        z_ref[:, sl] = kr
        kext_ref[QB:QB + TM, j * LANES:(j + 1) * LANES] = kr.astype(BF16)
    vext_ref[QB:QB + TM, :] = z_ref[:, V0:V0 + KV_WIDTH].astype(BF16)
    uext_ref[hist:hist + TM, :] = z_ref[:, U0:U0 + POOL_WIDTH]
    nk_ref[...] = z_ref[TM - QB:TM, K0:K0 + KV_WIDTH]
    nv_ref[...] = z_ref[TM - QB:TM, V0:V0 + KV_WIDTH]

    rowq = lax.broadcasted_iota(jnp.int32, (QB, 2 * QB), 0)
    colk = lax.broadcasted_iota(jnp.int32, (QB, 2 * QB), 1)
    band = (colk >= rowq) & (colk <= rowq + WINDOW)
    bias_ref[1] = jnp.where(band, 0.0, NEG_INF)
    bias_ref[0] = jnp.where(band & ((colk >= QB) | (i > 0)), 0.0, NEG_INF)
    for sb in range(TM // QB):
        k2 = kext_ref[sb * QB:(sb + 2) * QB, :]
        v2 = vext_ref[sb * QB:(sb + 2) * QB, :]
        qs = jnp.concatenate(
            [qb_ref[sb * QB:(sb + 1) * QB, g * KV_WIDTH:(g + 1) * KV_WIDTH] for g in range(GROUP)], axis=0)
        for kv in range(N_KV):
            kmask = _lane_block_mask((2 * QB, KV_WIDTH), kv)
            s_ref[...] = _dot_nt(qs, jnp.where(kmask, k2, jnp.zeros_like(k2)))
            vblk_ref[kv * 2 * QB:(kv + 1) * 2 * QB, :] = jnp.where(kmask, v2, jnp.zeros_like(v2))
            for c0 in range(0, GROUP * QB, SOFTMAX_ROWS):
                rq = c0 % QB
                sink = sinks_ref[kv * GROUP + c0 // QB]
                s = s_ref[c0:c0 + SOFTMAX_ROWS, :] + bias_ref[min(sb, 1), rq:rq + SOFTMAX_ROWS, :]
                m = jnp.maximum(jnp.max(s, axis=1, keepdims=True), sink)
                p = jnp.exp(s - m)
                den = jnp.sum(p, axis=1, keepdims=True) + jnp.exp(sink - m)
                p_ref[c0:c0 + SOFTMAX_ROWS, kv * 2 * QB:(kv + 1) * 2 * QB] = (p * (1.0 / den)).astype(BF16)
        o = _dot(p_ref[...], vblk_ref[...])
        for g in range(GROUP):
            oa_ref[sb * QB:(sb + 1) * QB, g * KV_WIDTH:(g + 1) * KV_WIDTH] = o[g * QB:(g + 1) * QB].astype(BF16)

    npool_ref[...] = uext_ref[TM:TM + hist, :]
    pos = i * TM + lax.broadcasted_iota(jnp.int32, (TM, 1), 0)
    for g, w in enumerate(POOL_WINDOWS):
        sl = slice(g * POOL_GROUP_DIM, (g + 1) * POOL_GROUP_DIM)
        cur = uext_ref[hist:hist + TM, sl]
        ws = cur
        for j in range(1, w):
            ws = ws + uext_ref[hist - j:hist - j + TM, sl]
        cnt = jnp.minimum(pos + 1, w).astype(F32)
        pooled = ws / cnt - cur
        ob_ref[:, sl] = _dot(pooled.astype(BF16), wmix_ref[g]) * pscale_ref[:, sl]

    cq = (z_ref[:, CQ0:CQ0 + MEM_WIDTH] * ATT_SCALE).astype(BF16)
    mk = mk_ref[...].astype(BF16)
    mv = mv_ref[...].astype(BF16)
    for hh in range(MEM_HEADS):
        hmask = _lane_block_mask((N_MEM, MEM_WIDTH), hh)
        s_ref[0:TM, :] = _dot_nt(cq, jnp.where(hmask, mk, jnp.zeros_like(mk)))
        vblk_ref[hh * N_MEM:(hh + 1) * N_MEM, :] = jnp.where(hmask, mv, jnp.zeros_like(mv))
        for c0 in range(0, TM, SOFTMAX_ROWS):
            s = s_ref[c0:c0 + SOFTMAX_ROWS, :]
            p = jnp.exp(s - jnp.max(s, axis=1, keepdims=True))
            den = jnp.sum(p, axis=1, keepdims=True)
            p_ref[c0:c0 + SOFTMAX_ROWS, hh * N_MEM:(hh + 1) * N_MEM] = (p * (1.0 / den)).astype(BF16)
    oc = _dot(p_ref[0:TM, :], vblk_ref[...])

    h = _merge_ln1(x, oa_ref[...], ob_ref[...], oc, z_ref[:, GZ0:GZ0 + 3 * D_MODEL],
                   wa_ref, wb_ref, wc_ref, wo_ref, g1_ref[...], b1_ref[...])
    h_ref[...] = h
    hb = h.astype(BF16)
    logits = _dot(hb, wr_ref[...]) + br_ref[...]
    _project_q(xn_ref[...].astype(BF16), wq_ref, zq_ref)
    xs_ref[...], route_ref[...], counts_ref[...] = _route_and_sort(logits, hb, xs_ref.shape[0])


def _front(x, rope, sinks, wq, wrest, wa, wb, wc, wo, wmix, pscale, mk, mv, g1, b1, wr, br):
    b, l, _ = x.shape
    nt = l // TM
    cap = _chunk_rows(b * nt)
    hist = 2 * SUBLANES
    tile = lambda w: pl.BlockSpec((None, TM, w), lambda bi, ti: (bi, ti, 0))
    per_b = lambda r, w: pl.BlockSpec((None, r, w), lambda bi, ti: (bi, 0, 0))
    nxt = lambda bi, ti: jnp.minimum(bi * nt + ti + 1, b * nt - 1)
    return pl.pallas_call(
        _front_kernel,
        grid=(b, nt),
        in_specs=[
            pl.BlockSpec(memory_space=pltpu.SMEM),
            tile(D_MODEL),
            pl.BlockSpec((None, TM, D_MODEL), lambda bi, ti: (nxt(bi, ti) // nt, nxt(bi, ti) % nt, 0)),
            pl.BlockSpec((3, TM, LANES), lambda bi, ti: (0, ti, 0)),
            _const_spec((D_MODEL, Q_WIDTH)), _const_spec((D_MODEL, IN_WIDTH - K0)),
            _const_spec((Q_WIDTH, D_MODEL)), _const_spec((POOL_WIDTH, D_MODEL)),
            _const_spec((MEM_WIDTH, D_MODEL)), _const_spec((D_MODEL, D_MODEL)),
            _const_spec((len(POOL_WINDOWS), POOL_GROUP_DIM, POOL_GROUP_DIM)),
            _const_spec((1, POOL_WIDTH)),
            per_b(N_MEM, MEM_WIDTH), per_b(N_MEM, MEM_WIDTH),
            _const_spec((1, D_MODEL)), _const_spec((1, D_MODEL)),
            _const_spec((D_MODEL, LANES)), _const_spec((1, LANES)),
        ],
        out_specs=[
            tile(D_MODEL),
            pl.BlockSpec((cap, D_MODEL), lambda bi, ti: (bi * nt + ti, 0)),
            tile(LANES),
            pl.BlockSpec((None, None, LANES, LANES), lambda bi, ti: (bi, ti, 0, 0)),
            per_b(QB, KV_WIDTH), per_b(QB, KV_WIDTH), per_b(hist, POOL_WIDTH),
        ],
        out_shape=[
            jax.ShapeDtypeStruct((b, l, D_MODEL), F32),
            jax.ShapeDtypeStruct((b * nt * cap, D_MODEL), BF16),
            jax.ShapeDtypeStruct((b, l, LANES), F32),
            jax.ShapeDtypeStruct((b, nt, LANES, LANES), F32),
            jax.ShapeDtypeStruct((b, QB, KV_WIDTH), F32),
            jax.ShapeDtypeStruct((b, QB, KV_WIDTH), F32),
            jax.ShapeDtypeStruct((b, hist, POOL_WIDTH), F32),
        ],
        scratch_shapes=[
            pltpu.VMEM((TM, IN_WIDTH), F32),
            pltpu.VMEM((TM, Q_WIDTH), BF16),
            pltpu.VMEM((QB + TM, KV_WIDTH), BF16),
            pltpu.VMEM((QB + TM, KV_WIDTH), BF16),
            pltpu.VMEM((hist + TM, POOL_WIDTH), F32),
            pltpu.VMEM((TM, Q_WIDTH), BF16),
            pltpu.VMEM((TM, POOL_WIDTH), F32),
            pltpu.VMEM((2, QB, 2 * QB), F32),
            pltpu.VMEM((GROUP * QB, 2 * QB), F32),
            pltpu.VMEM((GROUP * QB, N_KV * 2 * QB), BF16),
            pltpu.VMEM((N_KV * 2 * QB, KV_WIDTH), BF16),
            pltpu.VMEM((TM, Q_WIDTH), F32),
        ],
        compiler_params=pltpu.CompilerParams(
            dimension_semantics=("arbitrary", "arbitrary"), vmem_limit_bytes=VMEM_LIMIT),
        name="front_prompt",
    )(sinks, x, x, rope, wq, wrest, wa, wb, wc, wo, wmix, pscale, mk, mv, g1, b1, wr, br)


def _expert_mlp(xb, wg, wu, wd):
    a = _dot(xb, wg)
    hid = (a * jax.nn.sigmoid(a)) * _dot(xb, wu)
    return _dot(hid.astype(BF16), wd)


def _gemm_kernel(src_ref, dst_ref, te_ref, act_ref, xs_ref, wg_ref, wu_ref, wd_ref, ys_ref,
                 xbuf, obuf, wgb, wub, wdb, prime, sem_in, sem_out):
    i = pl.program_id(0)
    n = pl.num_programs(0)
    slot = i % 2
    in_slot = i % IN_SLOTS

    def start_in(tile, slot):
        for j in range(PIECES_PER_TILE):
            row0 = pl.multiple_of(src_ref[tile * PIECES_PER_TILE + j], PIECE)
            pltpu.make_async_copy(xs_ref.at[pl.ds(row0, PIECE)], xbuf.at[slot, pl.ds(j * PIECE, PIECE)],
                                  sem_in.at[slot]).start()

    def start_out(tile, slot):
        for j in range(PIECES_PER_TILE):
            row0 = pl.multiple_of(dst_ref[tile * PIECES_PER_TILE + j], PIECE)
            pltpu.make_async_copy(obuf.at[slot, pl.ds(j * PIECE, PIECE)], ys_ref.at[pl.ds(row0, PIECE)],
                                  sem_out.at[slot]).start()

    def wait_in(slot):
        for j in range(PIECES_PER_TILE):
            pltpu.make_async_copy(xs_ref.at[pl.ds(0, PIECE)], xbuf.at[slot, pl.ds(j * PIECE, PIECE)],
                                  sem_in.at[slot]).wait()

    def wait_out(slot):
        for j in range(PIECES_PER_TILE):
            pltpu.make_async_copy(obuf.at[slot, pl.ds(j * PIECE, PIECE)], ys_ref.at[pl.ds(0, PIECE)],
                                  sem_out.at[slot]).wait()

    active = act_ref[i] > 0
    ahead = IN_SLOTS - 1
    prefetched = (i < ahead) | (act_ref[jnp.maximum(i - ahead, 0)] > 0)
    out_pending = (i < 2) | (act_ref[jnp.maximum(i - 2, 0)] > 0)

    @pl.when(i == 0)
    def _():
        prime[0] = jnp.zeros((PIECE, D_MODEL), BF16)
        for s in range(2):
            for j in range(PIECES_PER_TILE):
                pltpu.make_async_copy(prime.at[0], prime.at[1 + s * PIECES_PER_TILE + j], sem_out.at[s]).start()
        for t in range(ahead):
            start_in(t, t)

    prev_active = (i >= 1) & (act_ref[jnp.maximum(i - 1, 0)] > 0)

    def active_step(after_first):
        @pl.when((i == 0) | (te_ref[i] != te_ref[jnp.maximum(i - 1, 0)]))
        def _():
            wgb[...] = wg_ref[...].astype(BF16)
            wub[...] = wu_ref[...].astype(BF16)
            wdb[...] = wd_ref[...].astype(BF16)

        wait_in(in_slot)
        wait_out(slot)
        out = _expert_mlp(xbuf[in_slot], wgb[...], wub[...], wdb[...])
        start_in(i + ahead, (i + ahead) % IN_SLOTS)
        if after_first:
            start_out(i - 1, 1 - slot)
        obuf[slot] = out.astype(BF16)

    @pl.when(active & (i == 0))
    def _():
        active_step(False)

    @pl.when(active & (i > 0))
    def _():
        active_step(True)

    @pl.when(jnp.logical_not(active))
    def _():
        @pl.when(prev_active)
        def _():
            start_out(i - 1, 1 - slot)

        @pl.when(prefetched)
        def _():
            wait_in(in_slot)

        @pl.when(out_pending)
        def _():
            wait_out(slot)

    @pl.when(i == n - 1)
    def _():
        for t in range(ahead):
            @pl.when(act_ref[jnp.maximum(i - t, 0)] > 0)
            def _(t=t):
                wait_in((i - t + ahead) % IN_SLOTS)

        @pl.when(active)
        def _():
            start_out(i, slot)
            wait_out(slot)

        @pl.when(prev_active)
        def _():
            wait_out(1 - slot)


def _grouped_gemm(piece_src, piece_dst, tile_expert, tile_active, xs, wg, wu, wd):
    n_tiles = tile_expert.shape[0]
    assert n_tiles >= 2
    wspec = lambda r, c: pl.BlockSpec((None, r, c), lambda i, src, dst, te, act: (te[i], 0, 0))
    return pl.pallas_call(
        _gemm_kernel,
        grid_spec=pltpu.PrefetchScalarGridSpec(
            num_scalar_prefetch=4,
            grid=(n_tiles,),
            in_specs=[pl.BlockSpec(memory_space=pl.ANY),
                      wspec(D_MODEL, D_EXPERT), wspec(D_MODEL, D_EXPERT), wspec(D_EXPERT, D_MODEL)],
            out_specs=pl.BlockSpec(memory_space=pl.ANY),
            scratch_shapes=[pltpu.VMEM((IN_SLOTS, TG, D_MODEL), BF16), pltpu.VMEM((2, TG, D_MODEL), BF16),
                            pltpu.VMEM((D_MODEL, D_EXPERT), BF16), pltpu.VMEM((D_MODEL, D_EXPERT), BF16),
                            pltpu.VMEM((D_EXPERT, D_MODEL), BF16),
                            pltpu.VMEM((1 + 2 * PIECES_PER_TILE, PIECE, D_MODEL), BF16),
                            pltpu.SemaphoreType.DMA((IN_SLOTS,)), pltpu.SemaphoreType.DMA((2,))],
        ),
        out_shape=jax.ShapeDtypeStruct(xs.shape, xs.dtype),
        input_output_aliases={4: 0},
        compiler_params=pltpu.CompilerParams(dimension_semantics=("arbitrary",), vmem_limit_bytes=VMEM_LIMIT),
        name="moe_grouped_gemm",
    )(piece_src, piece_dst, tile_expert, tile_active, xs, wg, wu, wd)


def _combine_kernel(ys_hbm, h_hbm, route_ref, g2_ref, b2_ref, y_ref, ybuf, hbuf, sem):
    s = pl.program_id(0)
    yrows, hrows = ybuf.shape[1], hbuf.shape[1]
    n = h_hbm.shape[0] // hrows
    ahead = COMBINE_SLOTS - 1

    def copies(step):
        ring = step % COMBINE_SLOTS
        return (pltpu.make_async_copy(ys_hbm.at[pl.ds(pl.multiple_of(step * yrows, PIECE), yrows)], ybuf.at[ring],
                                      sem.at[0, ring]),
                pltpu.make_async_copy(h_hbm.at[pl.ds(pl.multiple_of(step * hrows, TM), hrows)], hbuf.at[ring],
                                      sem.at[1, ring]))

    @pl.when(s == 0)
    def _():
        for t in range(min(ahead, n)):
            for cp in copies(t):
                cp.start()

    @pl.when(s + ahead < n)
    def _():
        for cp in copies(s + ahead):
            cp.start()

    for cp in copies(s):
        cp.wait()

    ring = s % COMBINE_SLOTS
    chunks = hrows // TM
    cap = yrows // chunks
    slot = lax.broadcasted_iota(jnp.int32, (TM, cap), 1).astype(F32)
    for c in range(chunks):
        rows = slice(c * TM, (c + 1) * TM)
        route = route_ref[rows, :]
        sel = jnp.where(slot == route[:, 2:3], route[:, 0:1], jnp.where(slot == route[:, 3:4], route[:, 1:2], 0.0))
        f = _dot(sel.astype(BF16), ybuf[ring, c * cap:(c + 1) * cap, :])
        y_ref[rows, :] = _layer_norm(ALPHA * hbuf[ring, rows, :] + f, g2_ref[...], b2_ref[...])


def _combine(ys, h, route, g2, b2):
    t = h.shape[0]
    cap = ys.shape[0] // (t // TM)
    per_step = next(k for k in (4, 2, 1) if (t // TM) % k == 0)
    rows = per_step * TM
    return pl.pallas_call(
        _combine_kernel,
        grid=(t // rows,),
        in_specs=[
            pl.BlockSpec(memory_space=pl.ANY),
            pl.BlockSpec(memory_space=pl.ANY),
            pl.BlockSpec((rows, LANES), lambda i: (i, 0)),
            pl.BlockSpec((1, D_MODEL), lambda i: (0, 0)),
            pl.BlockSpec((1, D_MODEL), lambda i: (0, 0)),
        ],
        out_specs=pl.BlockSpec((rows, D_MODEL), lambda i: (i, 0)),
        out_shape=jax.ShapeDtypeStruct((t, D_MODEL), F32),
        scratch_shapes=[pltpu.VMEM((COMBINE_SLOTS, per_step * cap, D_MODEL), BF16),
                        pltpu.VMEM((COMBINE_SLOTS, rows, D_MODEL), F32),
                        pltpu.SemaphoreType.DMA((2, COMBINE_SLOTS))],
        compiler_params=pltpu.CompilerParams(dimension_semantics=("arbitrary",), vmem_limit_bytes=VMEM_LIMIT),
        name="moe_combine",
    )(ys, h, route, g2, b2)


def _select(table, idx):
    hot = idx[:, None] == jnp.arange(table.shape[0], dtype=jnp.int32)[None, :]
    return jnp.sum(jnp.where(hot[:, :, None], table[None, :, :], 0), axis=1)


def _piece_tables(counts):
    n_chunks = counts.shape[0]
    n_tiles = -(-(n_chunks * MAX_CHUNK_PIECES + N_EXPERTS * (PIECES_PER_TILE - 1)) // PIECES_PER_TILE)
    npc = (counts + (PIECE - 1)) // PIECE
    first = (jnp.cumsum(npc, axis=1) - npc).T
    npc_t = npc.T
    cum = jnp.cumsum(npc_t, axis=1)
    per_expert = cum[:, -1]
    tiles_e = (per_expert + (PIECES_PER_TILE - 1)) // PIECES_PER_TILE
    tile_end = jnp.cumsum(tiles_e)
    tile_idx = jnp.arange(n_tiles, dtype=jnp.int32)
    expert_of = lambda i: jnp.minimum(jnp.sum(i[:, None] >= tile_end[None, :], axis=1), N_EXPERTS - 1).astype(jnp.int32)
    active = tile_idx < tile_end[-1]
    tile_expert = jnp.where(active, expert_of(tile_idx), expert_of(tile_end[-1:] - 1))
    meta = jnp.stack([tile_end - tiles_e, per_expert], axis=1)
    meta_t = _select(meta, tile_expert)
    k = (tile_idx - meta_t[:, 0])[:, None] * PIECES_PER_TILE + jnp.arange(PIECES_PER_TILE, dtype=jnp.int32)[None, :]
    valid = active[:, None] & (k < meta_t[:, 1:2])
    cum_t = _select(cum, tile_expert)
    chunk = jnp.minimum(jnp.sum(k[:, :, None] >= cum_t[:, None, :], axis=2), n_chunks - 1).astype(jnp.int32)
    at_chunk = chunk[:, :, None] == jnp.arange(n_chunks, dtype=jnp.int32)[None, None, :]
    pick = lambda tab: jnp.sum(jnp.where(at_chunk, _select(tab, tile_expert)[:, None, :], 0), axis=2)
    piece = pick(first) + k - pick(cum - npc_t)
    cap = _chunk_rows(n_chunks)
    rows = chunk * cap + piece * PIECE
    d = ((tile_idx % 2)[:, None] * (PIECES_PER_TILE - 1)
         + jnp.maximum(jnp.arange(PIECES_PER_TILE, dtype=jnp.int32)[None, :] - 1, 0))
    spare_row = lambda d: (d % n_chunks) * cap + (MAX_CHUNK_PIECES + d // n_chunks) * PIECE
    spare = spare_row(d)
    zero_piece = spare_row(N_SPARE - 1)
    extra = jnp.full(((IN_SLOTS - 1) * PIECES_PER_TILE,), zero_piece, jnp.int32)
    src = jnp.concatenate([jnp.where(valid, rows, zero_piece).astype(jnp.int32).reshape(-1), extra])
    dst = jnp.concatenate([jnp.where(valid, rows, spare).astype(jnp.int32).reshape(-1), extra])
    return src, dst, tile_expert.astype(jnp.int32), active.astype(jnp.int32)


def _sample_attn_kernel(x_ref, rope_ref, wq_ref, wrest_ref, sink_ref, ck_ref, cv_ref, cmk_ref, cmv_ref,
                        z_ref, oa_ref, oc_ref, nk_ref, nv_ref, knew_t, vnew_t):
    j = pl.program_id(0)
    db = x_ref.shape[0]

    @pl.when(j == 0)
    def _():
        xb = x_ref[...].astype(BF16)
        _project_q(xb, wq_ref, z_ref)
        _project_rest(xb, wrest_ref, z_ref)
        c = rope_ref[0]
        s1 = rope_ref[1]
        s2 = rope_ref[2]
        for jj in range((Q_WIDTH + KV_WIDTH) // LANES):
            sl = slice(jj * LANES, (jj + 1) * LANES)
            z_ref[:, sl] = _rope(z_ref[:, sl], c, s1, s2)
        knew_t[...] = z_ref[:, K0:K0 + KV_WIDTH].T
        vnew_t[...] = z_ref[:, V0:V0 + KV_WIDTH].T

    r0 = pl.multiple_of(j * SB, SB)
    zq = z_ref[pl.ds(r0, SB), Q0:Q0 + Q_WIDTH]
    zk = z_ref[pl.ds(r0, SB), K0:K0 + KV_WIDTH]
    zv = z_ref[pl.ds(r0, SB), V0:V0 + KV_WIDTH]
    zc = z_ref[pl.ds(r0, SB), CQ0:CQ0 + MEM_WIDTH] * ATT_SCALE
    sink = sink_ref[:, 0:1]
    row_kv = lax.broadcasted_iota(jnp.int32, (N_HEADS, KV_WIDTH), 0) & (N_KV - 1)
    lane_kv = lax.broadcasted_iota(jnp.int32, (N_HEADS, KV_WIDTH), 1) // HEAD_DIM
    own = row_kv == lane_kv
    row_c = lax.broadcasted_iota(jnp.int32, (N_HEADS, MEM_WIDTH), 0)
    lane_c = lax.broadcasted_iota(jnp.int32, (N_HEADS, MEM_WIDTH), 1) // HEAD_DIM
    own_c = row_c == lane_c
    last_pos = lax.broadcasted_iota(jnp.int32, (KV_WIDTH, WINDOW), 1) == WINDOW - 1
    seq_lane = lax.broadcasted_iota(jnp.int32, (KV_WIDTH, db), 1)

    seqs = range(SB)
    qblk, cblk, s, sc = [], [], [], []
    for b in seqs:
        q4 = jnp.concatenate(
            [jnp.broadcast_to(zq[b:b + 1, g * KV_WIDTH:(g + 1) * KV_WIDTH], (N_KV, KV_WIDTH)) for g in range(GROUP)],
            axis=0)
        qblk.append(jnp.where(own, q4, 0.0).astype(BF16))
        cblk.append(jnp.where(own_c, jnp.broadcast_to(zc[b:b + 1, :], (N_HEADS, MEM_WIDTH)), 0.0).astype(BF16))
        s.append(_dot(qblk[b], ck_ref[b].astype(BF16)))
        sc.append(_dot(cblk[b], cmk_ref[b].astype(BF16)))

    pn, p_new, pc = [], [], []
    for b in seqs:
        s_new = jnp.sum(qblk[b].astype(F32) * zk[b:b + 1, :].astype(BF16).astype(F32), axis=1, keepdims=True)
        m = jnp.maximum(jnp.maximum(jnp.max(s[b], axis=1, keepdims=True), s_new), sink)
        p = jnp.exp(s[b] - m)
        e_new = jnp.exp(s_new - m)
        inv = 1.0 / (jnp.sum(p, axis=1, keepdims=True) + e_new + jnp.exp(sink - m))
        pn.append((p * inv).astype(BF16))
        p_new.append((e_new * inv).astype(BF16).astype(F32))
        e = jnp.exp(sc[b] - jnp.max(sc[b], axis=1, keepdims=True))
        pc.append((e * (1.0 / jnp.sum(e, axis=1, keepdims=True))).astype(BF16))

    oa_rows = [[] for _ in range(GROUP)]
    oc_rows = []
    for b in seqs:
        vc = cv_ref[b]
        o = _dot_nt(pn[b], vc.astype(BF16)) + p_new[b] * zv[b:b + 1, :].astype(BF16).astype(F32)
        o = jnp.where(own, o, 0.0)
        for g in range(GROUP):
            oa_rows[g].append(jnp.sum(o[g * N_KV:(g + 1) * N_KV], axis=0, keepdims=True))
        ocb = jnp.where(own_c, _dot_nt(pc[b], cmv_ref[b].astype(BF16)), 0.0)
        oc_rows.append(jnp.sum(ocb, axis=0, keepdims=True))
        here = seq_lane == j * SB + b
        knew_col = jnp.sum(jnp.where(here, knew_t[...], 0.0), axis=1, keepdims=True)
        vnew_col = jnp.sum(jnp.where(here, vnew_t[...], 0.0), axis=1, keepdims=True)
        nk_ref[b] = jnp.where(last_pos, knew_col, pltpu.roll(ck_ref[b], WINDOW - 1, 1))
        nv_ref[b] = jnp.where(last_pos, vnew_col, pltpu.roll(vc, WINDOW - 1, 1))
    for g in range(GROUP):
        oa_ref[pl.ds(r0, SB), g * KV_WIDTH:(g + 1) * KV_WIDTH] = jnp.concatenate(oa_rows[g], axis=0)
    oc_ref[pl.ds(r0, SB), :] = jnp.concatenate(oc_rows, axis=0)


def _sample_attn(x, rope, wq, wrest, sink_gk, ck, cv, cmk, cmv):
    db = x.shape[0]
    blk = lambda r: pl.BlockSpec((SB, KV_WIDTH, r), lambda j: (j, 0, 0))
    full = lambda w: pl.BlockSpec((db, w), lambda j: (0, 0))
    return pl.pallas_call(
        _sample_attn_kernel,
        grid=(db // SB,),
        in_specs=[
            full(D_MODEL), _const_spec((3, 1, LANES)),
            _const_spec((D_MODEL, Q_WIDTH)), _const_spec((D_MODEL, IN_WIDTH - K0)),
            _const_spec((N_HEADS, LANES)),
            blk(WINDOW), blk(WINDOW), blk(N_MEM), blk(N_MEM),
        ],
        out_specs=[full(IN_WIDTH), full(Q_WIDTH), full(MEM_WIDTH), blk(WINDOW), blk(WINDOW)],
        out_shape=[
            jax.ShapeDtypeStruct((db, IN_WIDTH), F32),
            jax.ShapeDtypeStruct((db, Q_WIDTH), F32),
            jax.ShapeDtypeStruct((db, MEM_WIDTH), F32),
            jax.ShapeDtypeStruct((db, KV_WIDTH, WINDOW), F32),
            jax.ShapeDtypeStruct((db, KV_WIDTH, WINDOW), F32),
        ],
        scratch_shapes=[pltpu.VMEM((KV_WIDTH, db), F32), pltpu.VMEM((KV_WIDTH, db), F32)],
        compiler_params=pltpu.CompilerParams(dimension_semantics=("arbitrary",), vmem_limit_bytes=VMEM_LIMIT),
        name="sample_attn",
    )(x, rope, wq, wrest, sink_gk, ck, cv, cmk, cmv)


def _sample_tail_kernel(x_ref, z_ref, oa_ref, oc_ref, st_ref, wa_ref, wb_ref, wc_ref, wo_ref, wmix_ref,
                        pscale_ref, g1_ref, b1_ref, wr_ref, br_ref, wg_ref, wu_ref, wd_ref, g2_ref, b2_ref,
                        y_ref, npool_ref, h_sc, comb_sc, acc_sc):
    e = pl.program_id(0)

    @pl.when(e == 0)
    def _():
        u = z_ref[:, U0:U0 + POOL_WIDTH]
        npool_ref[0:POOL_STATE - 1] = st_ref[1:POOL_STATE]
        npool_ref[POOL_STATE - 1] = u
        obs = []
        for g, w in enumerate(POOL_WINDOWS):
            sl = slice(g * POOL_GROUP_DIM, (g + 1) * POOL_GROUP_DIM)
            cur = u[:, sl]
            ws = cur
            for jj in range(1, w):
                ws = ws + st_ref[POOL_STATE - jj, :, sl]
            cnt = float(min(PAST_LEN + 1, w))
            pooled = ws / cnt - cur
            obs.append(_dot(pooled.astype(BF16), wmix_ref[g]) * pscale_ref[:, sl])
        ob = jnp.concatenate(obs, axis=1)
        h = _merge_ln1(x_ref[...], oa_ref[...].astype(BF16), ob, oc_ref[...], z_ref[:, GZ0:GZ0 + 3 * D_MODEL],
                       wa_ref, wb_ref, wc_ref, wo_ref, g1_ref[...], b1_ref[...])
        h_sc[...] = h
        logits = _dot(h.astype(BF16), wr_ref[...]) + br_ref[...]
        hot1, hot2, w1, w2 = _route(logits)
        comb_sc[...] = jnp.where(hot1, w1, 0.0) + jnp.where(hot2, w2, 0.0)
        acc_sc[...] = jnp.zeros_like(acc_sc)

    out = _expert_mlp(h_sc[...].astype(BF16), wg_ref[...].astype(BF16), wu_ref[...].astype(BF16),
                      wd_ref[...].astype(BF16))
    lane = lax.broadcasted_iota(jnp.int32, comb_sc.shape, 1)
    ce = jnp.sum(jnp.where(lane == e, comb_sc[...], 0.0), axis=1, keepdims=True)
    acc_sc[...] += ce * out

    @pl.when(e == pl.num_programs(0) - 1)
    def _():
        y_ref[...] = _layer_norm(ALPHA * h_sc[...] + acc_sc[...], g2_ref[...], b2_ref[...])


def _sample_tail(x, z, oa, oc, state, wa, wb, wc, wo, wmix, pscale, g1, b1, wr, br, wg, wu, wd, g2, b2):
    db = x.shape[0]
    full = lambda w: pl.BlockSpec((db, w), lambda e: (0, 0))
    vec = lambda w: pl.BlockSpec((1, w), lambda e: (0, 0))
    hist = pl.BlockSpec((POOL_STATE, db, POOL_WIDTH), lambda e: (0, 0, 0))
    return pl.pallas_call(
        _sample_tail_kernel,
        grid=(N_EXPERTS,),
        in_specs=[
            full(D_MODEL), full(IN_WIDTH), full(Q_WIDTH), full(MEM_WIDTH), hist,
            _const_spec((Q_WIDTH, D_MODEL)), _const_spec((POOL_WIDTH, D_MODEL)),
            _const_spec((MEM_WIDTH, D_MODEL)), _const_spec((D_MODEL, D_MODEL)),
            _const_spec((len(POOL_WINDOWS), POOL_GROUP_DIM, POOL_GROUP_DIM)),
            vec(POOL_WIDTH), vec(D_MODEL), vec(D_MODEL),
            _const_spec((D_MODEL, LANES)), vec(LANES),
            pl.BlockSpec((None, D_MODEL, D_EXPERT), lambda e: (e, 0, 0)),
            pl.BlockSpec((None, D_MODEL, D_EXPERT), lambda e: (e, 0, 0)),
            pl.BlockSpec((None, D_EXPERT, D_MODEL), lambda e: (e, 0, 0)),
            vec(D_MODEL), vec(D_MODEL),
        ],
        out_specs=[full(D_MODEL), hist],
        out_shape=[jax.ShapeDtypeStruct((db, D_MODEL), F32),
                   jax.ShapeDtypeStruct((POOL_STATE, db, POOL_WIDTH), F32)],
        scratch_shapes=[pltpu.VMEM((db, D_MODEL), F32), pltpu.VMEM((db, LANES), F32),
                        pltpu.VMEM((db, D_MODEL), F32)],
        compiler_params=pltpu.CompilerParams(dimension_semantics=("arbitrary",), vmem_limit_bytes=VMEM_LIMIT),
        name="sample_tail",
    )(x, z, oa, oc, state, wa, wb, wc, wo, wmix, pscale, g1, b1, wr, br, wg, wu, wd, g2, b2)


def _rope_tables(pos):
    half = ROPE_DIM // 2
    inv = jnp.power(ROPE_THETA, -jnp.arange(half, dtype=F32) * (2.0 / ROPE_DIM))
    ang = pos.astype(F32)[:, None] * jnp.tile(inv, LANES // half)[None, :]
    off = lax.broadcasted_iota(jnp.int32, (1, LANES), 1) % HEAD_DIM
    cos = jnp.cos(ang)
    sin = jnp.sin(ang)
    c = jnp.where(off < ROPE_DIM, cos, 1.0)
    s1 = jnp.where((off >= half) & (off < ROPE_DIM), sin, 0.0)
    s2 = jnp.where(off < half, -sin, 0.0)
    return jnp.stack([c, s1, s2])


def _q_heads_group_major(w, axis):
    if axis == 1:
        n = w.shape[0]
        return w.reshape(n, N_KV, GROUP, HEAD_DIM).transpose(0, 2, 1, 3).reshape(n, Q_WIDTH)
    n = w.shape[1]
    return w.reshape(N_KV, GROUP, HEAD_DIM, n).transpose(1, 0, 2, 3).reshape(Q_WIDTH, n)


def kernel(x_prompt, x_sample, cache_win_k, cache_win_v, state_pool, cache_mem_k, cache_mem_v, mem_prompt, w_in, sinks, w_pool_mix, pool_scale, w_mem_k, w_mem_v, w_branch_a, w_branch_b, w_branch_c, w_out, ln1_g, ln1_b, w_group, b_group, w_router, b_router, w_gate, w_up, w_down, ln2_g, ln2_b):
    assert w_in.shape[0] == DEPTH == 1
    b, l, _ = x_prompt.shape
    db, ds, _ = x_sample.shape
    assert ds == 1 and l % TM == 0 and db % SB == 0
    assert cache_win_k.shape[2] == WINDOW
    t = b * l

    win = w_in[0]
    wq = (_q_heads_group_major(win[:, Q0:Q0 + Q_WIDTH], 1) * ATT_SCALE).astype(BF16)
    wrest = win[:, K0:].astype(BF16)
    wa = _q_heads_group_major(w_branch_a[0], 0).astype(BF16)
    wb = w_branch_b[0].astype(BF16)
    wc = w_branch_c[0].astype(BF16)
    wo = w_out[0].astype(BF16)
    wmix = w_pool_mix[0].astype(BF16)
    pscale = pool_scale[0].reshape(1, POOL_WIDTH)
    g1 = ln1_g[0].reshape(1, D_MODEL)
    b1 = ln1_b[0].reshape(1, D_MODEL)
    g2 = ln2_g[0].reshape(1, D_MODEL)
    b2 = ln2_b[0].reshape(1, D_MODEL)
    wr = jnp.concatenate([w_group[0], w_router[0].reshape(D_MODEL, N_EXPERTS)], axis=1)
    wr = jnp.pad(wr, ((0, 0), (0, LANES - wr.shape[1]))).astype(BF16)
    br = jnp.pad(jnp.concatenate([b_group[0], b_router[0].reshape(N_EXPERTS)]), (0, LANES - N_EXPERT_GROUPS - N_EXPERTS))
    br = br.reshape(1, LANES).astype(F32)
    wg = w_gate[0]
    wu = w_up[0]
    wd = w_down[0]
    sink = sinks[0].astype(F32)
    sink_gk = jnp.broadcast_to(sink.reshape(N_KV, GROUP).T.reshape(N_HEADS, 1), (N_HEADS, LANES))

    mk, mv = _mem_project(mem_prompt, w_mem_k[0].astype(BF16), w_mem_v[0].astype(BF16))
    rope_p = _rope_tables(jnp.arange(l, dtype=jnp.int32))
    h, xs, route, counts, nk_p, nv_p, npool_p = _front(
        x_prompt, rope_p, sink, wq, wrest, wa, wb, wc, wo, wmix, pscale, mk, mv, g1, b1, wr, br)
    piece_src, piece_dst, tile_expert, tile_active = _piece_tables(
        counts.reshape(-1, LANES, LANES)[:, :N_EXPERTS, 0].astype(jnp.int32))
    ys = _grouped_gemm(piece_src, piece_dst, tile_expert, tile_active, xs, wg, wu, wd)
    y_p = _combine(ys, h.reshape(t, D_MODEL), route.reshape(t, LANES), g2, b2).reshape(b, l, D_MODEL)

    rope_s = _rope_tables(jnp.full((1,), PAST_LEN, jnp.int32))
    xs = x_sample.reshape(db, D_MODEL)
    feat_major = lambda c: jnp.transpose(c[0], (0, 2, 3, 1)).reshape(db, KV_WIDTH, c.shape[2])
    from_feat_major = lambda a: jnp.transpose(a.reshape(db, N_KV, HEAD_DIM, a.shape[2]), (0, 3, 1, 2))[None]
    z_s, oa_s, oc_s, nk_s, nv_s = _sample_attn(xs, rope_s, wq, wrest, sink_gk, feat_major(cache_win_k),
                                               feat_major(cache_win_v), feat_major(cache_mem_k), feat_major(cache_mem_v))
    state = jnp.transpose(state_pool[0], (1, 0, 2))
    y_s, npool_s = _sample_tail(xs, z_s, oa_s, oc_s, state, wa, wb, wc, wo, wmix, pscale, g1, b1, wr, br,
                                wg, wu, wd, g2, b2)

    kv5 = lambda a, n, w: a.reshape(1, n, w, N_KV, HEAD_DIM)
    return (y_p, y_s.reshape(db, 1, D_MODEL),
            kv5(nk_p, b, QB), kv5(nv_p, b, QB),
            npool_p[:, 2 * SUBLANES - POOL_STATE:, :][None],
            kv5(mk, b, N_MEM), kv5(mv, b, N_MEM),
            from_feat_major(nk_s), from_feat_major(nv_s),
            jnp.transpose(npool_s, (1, 0, 2))[None])
```

```python
import jax
import jax.numpy as jnp
from jax import lax
from jax.experimental import pallas as pl
from jax.experimental.pallas import tpu as pltpu

D_MODEL = 1024
N_HEADS = 16
HEAD_DIM = 64
N_KV = 4
GROUP = N_HEADS // N_KV
WINDOW = 128
ROPE_THETA = 500000.0
ROPE_DIM = HEAD_DIM // 4
Q_WIDTH = N_HEADS * HEAD_DIM
KV_WIDTH = N_KV * HEAD_DIM
POOL_WINDOWS = (2, 4, 8, 16)
POOL_WIDTH = D_MODEL // 2
POOL_GROUP_DIM = POOL_WIDTH // len(POOL_WINDOWS)
POOL_STATE = max(POOL_WINDOWS) - 1
N_MEM = 256
MEM_HEADS = 4
MEM_WIDTH = MEM_HEADS * HEAD_DIM
N_EXPERT_GROUPS = 4
EXPERTS_PER_GROUP = 4
N_EXPERTS = N_EXPERT_GROUPS * EXPERTS_PER_GROUP
D_EXPERT = 512
PAST_LEN = 16384
DEPTH = 1
ALPHA = (2.0 * DEPTH) ** 0.25
LN_EPS = 1e-5

Q0 = 0
K0 = Q0 + Q_WIDTH
V0 = K0 + KV_WIDTH
U0 = V0 + KV_WIDTH
CQ0 = U0 + POOL_WIDTH
GZ0 = CQ0 + MEM_WIDTH
IN_WIDTH = GZ0 + 3 * D_MODEL

LANES = 128
SUBLANES = 8
VMEM_LIMIT = 56 * 1024 * 1024

TM = 256
QB = WINDOW
PIECE = 16
PIECES_PER_TILE = 32
TG = PIECE * PIECES_PER_TILE
MAX_CHUNK_PIECES = 2 * TM // PIECE + N_EXPERTS - 1
N_SPARE = 2 * (PIECES_PER_TILE - 1) + 1
IN_SLOTS = 3
COMBINE_SLOTS = 3


def _chunk_rows(n_chunks):
    spare = -(-N_SPARE // n_chunks)
    return -(-(MAX_CHUNK_PIECES + spare) * PIECE // LANES) * LANES
SB = 8
IN_CHUNK = 768
Q_CHUNK = 512
ATT_SCALE = HEAD_DIM ** -0.5
SOFTMAX_ROWS = 64
assert N_MEM == 2 * QB and MEM_HEADS == N_KV and MEM_WIDTH == KV_WIDTH and TM <= GROUP * QB

BF16 = jnp.bfloat16
F32 = jnp.float32
NEG_INF = float("-inf")


def _const_spec(shape):
    nd = len(shape)
    return pl.BlockSpec(shape, lambda *_: (0,) * nd, pipeline_mode=pl.Buffered(1))


def _layer_norm(x, g, b):
    mu = jnp.mean(x, axis=-1, keepdims=True)
    xc = x - mu
    var = jnp.mean(xc * xc, axis=-1, keepdims=True)
    return xc * lax.rsqrt(var + LN_EPS) * g + b


def _dot(a, b):
    return jnp.dot(a, b, preferred_element_type=F32)


def _dot_nt(a, b):
    return lax.dot_general(a, b, (((1,), (1,)), ((), ())), preferred_element_type=F32)


def _lane_block_mask(shape, block, width=HEAD_DIM):
    lane = lax.broadcasted_iota(jnp.int32, shape, len(shape) - 1)
    return (lane >= block * width) & (lane < (block + 1) * width)


def _rope(x, c, s1, s2):
    half = ROPE_DIM // 2
    return x * c + pltpu.roll(x, half, 1) * s1 + pltpu.roll(x, LANES - half, 1) * s2


def _route(logits):
    rows = logits.shape[0]
    lane = lax.broadcasted_iota(jnp.int32, (rows, LANES), 1)
    lanef = lane.astype(F32)
    big = float(LANES)
    is_g = lane < N_EXPERT_GROUPS
    glog = jnp.where(is_g, logits, NEG_INF)
    gmax = jnp.max(glog, axis=1, keepdims=True)
    gsum = jnp.sum(jnp.where(is_g, jnp.exp(glog - gmax), 0.0), axis=1, keepdims=True)
    gp = 1.0 / gsum
    gidx = jnp.min(jnp.where(glog == gmax, lanef, big), axis=1, keepdims=True).astype(jnp.int32)
    lo = N_EXPERT_GROUPS + gidx * EXPERTS_PER_GROUP
    in_grp = (lane >= lo) & (lane < lo + EXPERTS_PER_GROUP)
    el = jnp.where(in_grp, logits, NEG_INF)
    v1 = jnp.max(el, axis=1, keepdims=True)
    i1 = jnp.min(jnp.where(el == v1, lanef, big), axis=1, keepdims=True).astype(jnp.int32)
    el2 = jnp.where(lane == i1, NEG_INF, el)
    v2 = jnp.max(el2, axis=1, keepdims=True)
    i2 = jnp.min(jnp.where(el2 == v2, lanef, big), axis=1, keepdims=True).astype(jnp.int32)
    e21 = jnp.exp(v2 - v1)
    inv = 1.0 / (1.0 + e21)
    w1 = inv * gp
    w2 = e21 * inv * gp
    e1 = i1 - N_EXPERT_GROUPS
    e2 = i2 - N_EXPERT_GROUPS
    return lane == e1, lane == e2, w1, w2


def _route_and_sort(logits, hb, cap):
    rows = hb.shape[0]
    lt = logits.T
    row = lax.broadcasted_iota(jnp.int32, (LANES, rows), 0)
    rowf = row.astype(F32)
    big = float(LANES)
    is_g = row < N_EXPERT_GROUPS
    glog = jnp.where(is_g, lt, NEG_INF)
    gmax = jnp.max(glog, axis=0, keepdims=True)
    gp = 1.0 / jnp.sum(jnp.where(is_g, jnp.exp(glog - gmax), 0.0), axis=0, keepdims=True)
    gidx = jnp.min(jnp.where(glog == gmax, rowf, big), axis=0, keepdims=True).astype(jnp.int32)
    lo = N_EXPERT_GROUPS + gidx * EXPERTS_PER_GROUP
    el = jnp.where((row >= lo) & (row < lo + EXPERTS_PER_GROUP), lt, NEG_INF)
    v1 = jnp.max(el, axis=0, keepdims=True)
    i1 = jnp.min(jnp.where(el == v1, rowf, big), axis=0, keepdims=True).astype(jnp.int32)
    el2 = jnp.where(row == i1, NEG_INF, el)
    v2 = jnp.max(el2, axis=0, keepdims=True)
    i2 = jnp.min(jnp.where(el2 == v2, rowf, big), axis=0, keepdims=True).astype(jnp.int32)
    e21 = jnp.exp(v2 - v1)
    inv = 1.0 / (1.0 + e21)
    w1 = inv * gp
    w2 = e21 * inv * gp
    hot1 = row == i1 - N_EXPERT_GROUPS
    hot2 = row == i2 - N_EXPERT_GROUPS
    onehot = jnp.where(hot1 | hot2, 1.0, 0.0)
    counts = jnp.broadcast_to(jnp.sum(onehot, axis=1, keepdims=True), (LANES, LANES))
    earlier = (lax.broadcasted_iota(jnp.int32, (rows, rows), 0)
               < lax.broadcasted_iota(jnp.int32, (rows, rows), 1)).astype(BF16)
    rank = _dot(onehot.astype(BF16), earlier)
    run = (((counts.astype(jnp.int32) + (PIECE - 1)) // PIECE) * PIECE).astype(BF16)
    below = (lax.broadcasted_iota(jnp.int32, (LANES, LANES), 1)
             < lax.broadcasted_iota(jnp.int32, (LANES, LANES), 0)).astype(BF16)
    start = _dot(below, run)
    slot = jnp.concatenate([start] * (rows // LANES), axis=1) + rank
    s1 = jnp.sum(jnp.where(hot1, slot, 0.0), axis=0, keepdims=True)
    s2 = jnp.sum(jnp.where(hot2, slot, 0.0), axis=0, keepdims=True)
    srow = lax.broadcasted_iota(jnp.int32, (cap, rows), 0).astype(F32)
    perm = jnp.where((srow == s1) | (srow == s2), 1.0, 0.0).astype(BF16)
    route_t = jnp.where(row == 0, w1, jnp.where(row == 1, w2, jnp.where(row == 2, s1, jnp.where(row == 3, s2, 0.0))))
    return _dot(perm, hb).astype(BF16), route_t.T, counts


def _sigmoid(x):
    return 0.5 * jnp.tanh(0.5 * x) + 0.5


def _merge_ln1(x, oa, ob, oc, gz, wa_ref, wb_ref, wc_ref, wo_ref, g1, b1):
    ya = _dot(oa, wa_ref[...])
    yb = _dot(ob.astype(BF16), wb_ref[...])
    yc = _dot(oc.astype(BF16), wc_ref[...])
    m = (_sigmoid(gz[:, 0:D_MODEL]) * ya
         + _sigmoid(gz[:, D_MODEL:2 * D_MODEL]) * yb
         + _sigmoid(gz[:, 2 * D_MODEL:3 * D_MODEL]) * yc)
    hpre = ALPHA * x + _dot(m.astype(BF16), wo_ref[...])
    return _layer_norm(hpre, g1, b1)


def _mem_kernel(mem_ref, wk_ref, wv_ref, mk_ref, mv_ref):
    m = mem_ref[...].astype(BF16)
    mk_ref[...] = _dot(m, wk_ref[...])
    mv_ref[...] = _dot(m, wv_ref[...])


def _mem_project(mem, wk, wv):
    b = mem.shape[0]
    out = jax.ShapeDtypeStruct((b, N_MEM, MEM_WIDTH), F32)
    return pl.pallas_call(
        _mem_kernel,
        grid=(b,),
        in_specs=[pl.BlockSpec((None, N_MEM, D_MODEL), lambda i: (i, 0, 0)),
                  _const_spec((D_MODEL, MEM_WIDTH)), _const_spec((D_MODEL, MEM_WIDTH))],
        out_specs=[pl.BlockSpec((None, N_MEM, MEM_WIDTH), lambda i: (i, 0, 0))] * 2,
        out_shape=[out, out],
        name="mem_project",
    )(mem, wk, wv)


def _project_q(xb, wq_ref, zq_ref):
    for c0 in range(0, Q_WIDTH, Q_CHUNK):
        zq_ref[:, c0:c0 + Q_CHUNK] = _dot(xb, wq_ref[:, c0:c0 + Q_CHUNK])


def _project_rest(xb, wrest_ref, z_ref):
    rest = IN_WIDTH - K0
    for c0 in range(0, rest, IN_CHUNK):
        c1 = min(c0 + IN_CHUNK, rest)
        z_ref[:, K0 + c0:K0 + c1] = _dot(xb, wrest_ref[:, c0:c1])


def _front_kernel(sinks_ref, x_ref, xn_ref, rope_ref, wq_ref, wrest_ref, wa_ref, wb_ref, wc_ref, wo_ref, wmix_ref,
                  pscale_ref, mk_ref, mv_ref, g1_ref, b1_ref, wr_ref, br_ref,
                  h_ref, xs_ref, route_ref, counts_ref, nk_ref, nv_ref, npool_ref,
                  z_ref, qb_ref, kext_ref, vext_ref, uext_ref, oa_ref, ob_ref, bias_ref, s_ref, p_ref, vblk_ref, zq_ref):
    i = pl.program_id(1)
    x = x_ref[...]
    xb = x.astype(BF16)
    hist = 2 * SUBLANES

    @pl.when((pl.program_id(0) == 0) & (i == 0))
    def _():
        _project_q(xb, wq_ref, zq_ref)

    @pl.when(i == 0)
    def _():
        kext_ref[0:QB, :] = jnp.zeros((QB, KV_WIDTH), BF16)
        vext_ref[0:QB, :] = jnp.zeros((QB, KV_WIDTH), BF16)
        uext_ref[0:hist, :] = jnp.zeros((hist, POOL_WIDTH), F32)

    @pl.when(i > 0)
    def _():
        kext_ref[0:QB, :] = kext_ref[TM:TM + QB, :]
        vext_ref[0:QB, :] = vext_ref[TM:TM + QB, :]
        uext_ref[0:hist, :] = uext_ref[TM:TM + hist, :]

    _project_rest(xb, wrest_ref, z_ref)

    c = rope_ref[0]
    s1 = rope_ref[1]
    s2 = rope_ref[2]
    for j in range(Q_WIDTH // LANES):
        sl = slice(j * LANES, (j + 1) * LANES)
        qb_ref[:, sl] = _rope(zq_ref[:, sl], c, s1, s2).astype(BF16)
    for j in range(KV_WIDTH // LANES):
        sl = slice(K0 + j * LANES, K0 + (j + 1) * LANES)
        kr = _rope(z_ref[:, sl], c, s1, s2)
        z_ref[:, sl] = kr
        kext_ref[QB:QB + TM, j * LANES:(j + 1) * LANES] = kr.astype(BF16)
    vext_ref[QB:QB + TM, :] = z_ref[:, V0:V0 + KV_WIDTH].astype(BF16)
    uext_ref[hist:hist + TM, :] = z_ref[:, U0:U0 + POOL_WIDTH]
    nk_ref[...] = z_ref[TM - QB:TM, K0:K0 + KV_WIDTH]
    nv_ref[...] = z_ref[TM - QB:TM, V0:V0 + KV_WIDTH]

    rowq = lax.broadcasted_iota(jnp.int32, (QB, 2 * QB), 0)
    colk = lax.broadcasted_iota(jnp.int32, (QB, 2 * QB), 1)
    band = (colk >= rowq) & (colk <= rowq + WINDOW)
    bias_ref[1] = jnp.where(band, 0.0, NEG_INF)
    bias_ref[0] = jnp.where(band & ((colk >= QB) | (i > 0)), 0.0, NEG_INF)
    for sb in range(TM // QB):
        k2 = kext_ref[sb * QB:(sb + 2) * QB, :]
        v2 = vext_ref[sb * QB:(sb + 2) * QB, :]
        qs = jnp.concatenate(
            [qb_ref[sb * QB:(sb + 1) * QB, g * KV_WIDTH:(g + 1) * KV_WIDTH] for g in range(GROUP)], axis=0)
        for kv in range(N_KV):
            kmask = _lane_block_mask((2 * QB, KV_WIDTH), kv)
            s_ref[...] = _dot_nt(qs, jnp.where(kmask, k2, jnp.zeros_like(k2)))
            vblk_ref[kv * 2 * QB:(kv + 1) * 2 * QB, :] = jnp.where(kmask, v2, jnp.zeros_like(v2))
            for c0 in range(0, GROUP * QB, SOFTMAX_ROWS):
                rq = c0 % QB
                sink = sinks_ref[kv * GROUP + c0 // QB]
                s = s_ref[c0:c0 + SOFTMAX_ROWS, :] + bias_ref[min(sb, 1), rq:rq + SOFTMAX_ROWS, :]
                m = jnp.maximum(jnp.max(s, axis=1, keepdims=True), sink)
                p = jnp.exp(s - m)
                den = jnp.sum(p, axis=1, keepdims=True) + jnp.exp(sink - m)
                p_ref[c0:c0 + SOFTMAX_ROWS, kv * 2 * QB:(kv + 1) * 2 * QB] = (p * (1.0 / den)).astype(BF16)
        o = _dot(p_ref[...], vblk_ref[...])
        for g in range(GROUP):
            oa_ref[sb * QB:(sb + 1) * QB, g * KV_WIDTH:(g + 1) * KV_WIDTH] = o[g * QB:(g + 1) * QB].astype(BF16)

    npool_ref[...] = uext_ref[TM:TM + hist, :]
    pos = i * TM + lax.broadcasted_iota(jnp.int32, (TM, 1), 0)
    for g, w in enumerate(POOL_WINDOWS):
        sl = slice(g * POOL_GROUP_DIM, (g + 1) * POOL_GROUP_DIM)
        cur = uext_ref[hist:hist + TM, sl]
        ws = cur
        for j in range(1, w):
            ws = ws + uext_ref[hist - j:hist - j + TM, sl]
        cnt = jnp.minimum(pos + 1, w).astype(F32)
        pooled = ws / cnt - cur
        ob_ref[:, sl] = _dot(pooled.astype(BF16), wmix_ref[g]) * pscale_ref[:, sl]

    cq = (z_ref[:, CQ0:CQ0 + MEM_WIDTH] * ATT_SCALE).astype(BF16)
    mk = mk_ref[...].astype(BF16)
    mv = mv_ref[...].astype(BF16)
    for hh in range(MEM_HEADS):
        hmask = _lane_block_mask((N_MEM, MEM_WIDTH), hh)
        s_ref[0:TM, :] = _dot_nt(cq, jnp.where(hmask, mk, jnp.zeros_like(mk)))
        vblk_ref[hh * N_MEM:(hh + 1) * N_MEM, :] = jnp.where(hmask, mv, jnp.zeros_like(mv))
        for c0 in range(0, TM, SOFTMAX_ROWS):
            s = s_ref[c0:c0 + SOFTMAX_ROWS, :]
            p = jnp.exp(s - jnp.max(s, axis=1, keepdims=True))
            den = jnp.sum(p, axis=1, keepdims=True)
            p_ref[c0:c0 + SOFTMAX_ROWS, hh * N_MEM:(hh + 1) * N_MEM] = (p * (1.0 / den)).astype(BF16)
    oc = _dot(p_ref[0:TM, :], vblk_ref[...])

    h = _merge_ln1(x, oa_ref[...], ob_ref[...], oc, z_ref[:, GZ0:GZ0 + 3 * D_MODEL],
                   wa_ref, wb_ref, wc_ref, wo_ref, g1_ref[...], b1_ref[...])
    h_ref[...] = h
    hb = h.astype(BF16)
    logits = _dot(hb, wr_ref[...]) + br_ref[...]
    _project_q(xn_ref[...].astype(BF16), wq_ref, zq_ref)
    xs_ref[...], route_ref[...], counts_ref[...] = _route_and_sort(logits, hb, xs_ref.shape[0])


def _front(x, rope, sinks, wq, wrest, wa, wb, wc, wo, wmix, pscale, mk, mv, g1, b1, wr, br):
    b, l, _ = x.shape
    nt = l // TM
    cap = _chunk_rows(b * nt)
    hist = 2 * SUBLANES
    tile = lambda w: pl.BlockSpec((None, TM, w), lambda bi, ti: (bi, ti, 0))
    per_b = lambda r, w: pl.BlockSpec((None, r, w), lambda bi, ti: (bi, 0, 0))
    nxt = lambda bi, ti: jnp.minimum(bi * nt + ti + 1, b * nt - 1)
    return pl.pallas_call(
        _front_kernel,
        grid=(b, nt),
        in_specs=[
            pl.BlockSpec(memory_space=pltpu.SMEM),
            tile(D_MODEL),
            pl.BlockSpec((None, TM, D_MODEL), lambda bi, ti: (nxt(bi, ti) // nt, nxt(bi, ti) % nt, 0)),
            pl.BlockSpec((3, TM, LANES), lambda bi, ti: (0, ti, 0)),
            _const_spec((D_MODEL, Q_WIDTH)), _const_spec((D_MODEL, IN_WIDTH - K0)),
            _const_spec((Q_WIDTH, D_MODEL)), _const_spec((POOL_WIDTH, D_MODEL)),
            _const_spec((MEM_WIDTH, D_MODEL)), _const_spec((D_MODEL, D_MODEL)),
            _const_spec((len(POOL_WINDOWS), POOL_GROUP_DIM, POOL_GROUP_DIM)),
            _const_spec((1, POOL_WIDTH)),
            per_b(N_MEM, MEM_WIDTH), per_b(N_MEM, MEM_WIDTH),
            _const_spec((1, D_MODEL)), _const_spec((1, D_MODEL)),
            _const_spec((D_MODEL, LANES)), _const_spec((1, LANES)),
        ],
        out_specs=[
            tile(D_MODEL),
            pl.BlockSpec((cap, D_MODEL), lambda bi, ti: (bi * nt + ti, 0)),
            tile(LANES),
            pl.BlockSpec((None, None, LANES, LANES), lambda bi, ti: (bi, ti, 0, 0)),
            per_b(QB, KV_WIDTH), per_b(QB, KV_WIDTH), per_b(hist, POOL_WIDTH),
        ],
        out_shape=[
            jax.ShapeDtypeStruct((b, l, D_MODEL), F32),
            jax.ShapeDtypeStruct((b * nt * cap, D_MODEL), BF16),
            jax.ShapeDtypeStruct((b, l, LANES), F32),
            jax.ShapeDtypeStruct((b, nt, LANES, LANES), F32),
            jax.ShapeDtypeStruct((b, QB, KV_WIDTH), F32),
            jax.ShapeDtypeStruct((b, QB, KV_WIDTH), F32),
            jax.ShapeDtypeStruct((b, hist, POOL_WIDTH), F32),
        ],
        scratch_shapes=[
            pltpu.VMEM((TM, IN_WIDTH), F32),
            pltpu.VMEM((TM, Q_WIDTH), BF16),
            pltpu.VMEM((QB + TM, KV_WIDTH), BF16),
            pltpu.VMEM((QB + TM, KV_WIDTH), BF16),
            pltpu.VMEM((hist + TM, POOL_WIDTH), F32),
            pltpu.VMEM((TM, Q_WIDTH), BF16),
            pltpu.VMEM((TM, POOL_WIDTH), F32),
            pltpu.VMEM((2, QB, 2 * QB), F32),
            pltpu.VMEM((GROUP * QB, 2 * QB), F32),
            pltpu.VMEM((GROUP * QB, N_KV * 2 * QB), BF16),
            pltpu.VMEM((N_KV * 2 * QB, KV_WIDTH), BF16),
            pltpu.VMEM((TM, Q_WIDTH), F32),
        ],
        compiler_params=pltpu.CompilerParams(
            dimension_semantics=("arbitrary", "arbitrary"), vmem_limit_bytes=VMEM_LIMIT),
        name="front_prompt",
    )(sinks, x, x, rope, wq, wrest, wa, wb, wc, wo, wmix, pscale, mk, mv, g1, b1, wr, br)


def _expert_mlp(xb, wg, wu, wd):
    a = _dot(xb, wg)
    hid = (a * jax.nn.sigmoid(a)) * _dot(xb, wu)
    return _dot(hid.astype(BF16), wd)


def _gemm_kernel(src_ref, dst_ref, te_ref, act_ref, xs_ref, wg_ref, wu_ref, wd_ref, ys_ref,
                 xbuf, obuf, wgb, wub, wdb, prime, sem_in, sem_out):
    i = pl.program_id(0)
    n = pl.num_programs(0)
    slot = i % 2
    in_slot = i % IN_SLOTS

    def start_in(tile, slot):
        for j in range(PIECES_PER_TILE):
            row0 = pl.multiple_of(src_ref[tile * PIECES_PER_TILE + j], PIECE)
            pltpu.make_async_copy(xs_ref.at[pl.ds(row0, PIECE)], xbuf.at[slot, pl.ds(j * PIECE, PIECE)],
                                  sem_in.at[slot]).start()

    def start_out(tile, slot):
        for j in range(PIECES_PER_TILE):
            row0 = pl.multiple_of(dst_ref[tile * PIECES_PER_TILE + j], PIECE)
            pltpu.make_async_copy(obuf.at[slot, pl.ds(j * PIECE, PIECE)], ys_ref.at[pl.ds(row0, PIECE)],
                                  sem_out.at[slot]).start()

    def wait_in(slot):
        for j in range(PIECES_PER_TILE):
            pltpu.make_async_copy(xs_ref.at[pl.ds(0, PIECE)], xbuf.at[slot, pl.ds(j * PIECE, PIECE)],
                                  sem_in.at[slot]).wait()

    def wait_out(slot):
        for j in range(PIECES_PER_TILE):
            pltpu.make_async_copy(obuf.at[slot, pl.ds(j * PIECE, PIECE)], ys_ref.at[pl.ds(0, PIECE)],
                                  sem_out.at[slot]).wait()

    active = act_ref[i] > 0
    ahead = IN_SLOTS - 1
    prefetched = (i < ahead) | (act_ref[jnp.maximum(i - ahead, 0)] > 0)
    out_pending = (i < 2) | (act_ref[jnp.maximum(i - 2, 0)] > 0)

    @pl.when(i == 0)
    def _():
        prime[0] = jnp.zeros((PIECE, D_MODEL), BF16)
        for s in range(2):
            for j in range(PIECES_PER_TILE):
                pltpu.make_async_copy(prime.at[0], prime.at[1 + s * PIECES_PER_TILE + j], sem_out.at[s]).start()
        for t in range(ahead):
            start_in(t, t)

    prev_active = (i >= 1) & (act_ref[jnp.maximum(i - 1, 0)] > 0)

    def active_step(after_first):
        @pl.when((i == 0) | (te_ref[i] != te_ref[jnp.maximum(i - 1, 0)]))
        def _():
            wgb[...] = wg_ref[...].astype(BF16)
            wub[...] = wu_ref[...].astype(BF16)
            wdb[...] = wd_ref[...].astype(BF16)

        wait_in(in_slot)
        wait_out(slot)
        out = _expert_mlp(xbuf[in_slot], wgb[...], wub[...], wdb[...])
        start_in(i + ahead, (i + ahead) % IN_SLOTS)
        if after_first:
            start_out(i - 1, 1 - slot)
        obuf[slot] = out.astype(BF16)

    @pl.when(active & (i == 0))
    def _():
        active_step(False)

    @pl.when(active & (i > 0))
    def _():
        active_step(True)

    @pl.when(jnp.logical_not(active))
    def _():
        @pl.when(prev_active)
        def _():
            start_out(i - 1, 1 - slot)

        @pl.when(prefetched)
        def _():
            wait_in(in_slot)

        @pl.when(out_pending)
        def _():
            wait_out(slot)

    @pl.when(i == n - 1)
    def _():
        for t in range(ahead):
            @pl.when(act_ref[jnp.maximum(i - t, 0)] > 0)
            def _(t=t):
                wait_in((i - t + ahead) % IN_SLOTS)

        @pl.when(active)
        def _():
            start_out(i, slot)
            wait_out(slot)

        @pl.when(prev_active)
        def _():
            wait_out(1 - slot)


def _grouped_gemm(piece_src, piece_dst, tile_expert, tile_active, xs, wg, wu, wd):
    n_tiles = tile_expert.shape[0]
    assert n_tiles >= 2
    wspec = lambda r, c: pl.BlockSpec((None, r, c), lambda i, src, dst, te, act: (te[i], 0, 0))
    return pl.pallas_call(
        _gemm_kernel,
        grid_spec=pltpu.PrefetchScalarGridSpec(
            num_scalar_prefetch=4,
            grid=(n_tiles,),
            in_specs=[pl.BlockSpec(memory_space=pl.ANY),
                      wspec(D_MODEL, D_EXPERT), wspec(D_MODEL, D_EXPERT), wspec(D_EXPERT, D_MODEL)],
            out_specs=pl.BlockSpec(memory_space=pl.ANY),
            scratch_shapes=[pltpu.VMEM((IN_SLOTS, TG, D_MODEL), BF16), pltpu.VMEM((2, TG, D_MODEL), BF16),
                            pltpu.VMEM((D_MODEL, D_EXPERT), BF16), pltpu.VMEM((D_MODEL, D_EXPERT), BF16),
                            pltpu.VMEM((D_EXPERT, D_MODEL), BF16),
                            pltpu.VMEM((1 + 2 * PIECES_PER_TILE, PIECE, D_MODEL), BF16),
                            pltpu.SemaphoreType.DMA((IN_SLOTS,)), pltpu.SemaphoreType.DMA((2,))],
        ),
        out_shape=jax.ShapeDtypeStruct(xs.shape, xs.dtype),
        input_output_aliases={4: 0},
        compiler_params=pltpu.CompilerParams(dimension_semantics=("arbitrary",), vmem_limit_bytes=VMEM_LIMIT),
        name="moe_grouped_gemm",
    )(piece_src, piece_dst, tile_expert, tile_active, xs, wg, wu, wd)


def _combine_kernel(ys_hbm, h_hbm, route_ref, g2_ref, b2_ref, y_ref, ybuf, hbuf, sem):
    s = pl.program_id(0)
    yrows, hrows = ybuf.shape[1], hbuf.shape[1]
    n = h_hbm.shape[0] // hrows
    ahead = COMBINE_SLOTS - 1

    def copies(step):
        ring = step % COMBINE_SLOTS
        return (pltpu.make_async_copy(ys_hbm.at[pl.ds(pl.multiple_of(step * yrows, PIECE), yrows)], ybuf.at[ring],
                                      sem.at[0, ring]),
                pltpu.make_async_copy(h_hbm.at[pl.ds(pl.multiple_of(step * hrows, TM), hrows)], hbuf.at[ring],
                                      sem.at[1, ring]))

    @pl.when(s == 0)
    def _():
        for t in range(min(ahead, n)):
            for cp in copies(t):
                cp.start()

    @pl.when(s + ahead < n)
    def _():
        for cp in copies(s + ahead):
            cp.start()

    for cp in copies(s):
        cp.wait()

    ring = s % COMBINE_SLOTS
    chunks = hrows // TM
    cap = yrows // chunks
    slot = lax.broadcasted_iota(jnp.int32, (TM, cap), 1).astype(F32)
    for c in range(chunks):
        rows = slice(c * TM, (c + 1) * TM)
        route = route_ref[rows, :]
        sel = jnp.where(slot == route[:, 2:3], route[:, 0:1], jnp.where(slot == route[:, 3:4], route[:, 1:2], 0.0))
        f = _dot(sel.astype(BF16), ybuf[ring, c * cap:(c + 1) * cap, :])
        y_ref[rows, :] = _layer_norm(ALPHA * hbuf[ring, rows, :] + f, g2_ref[...], b2_ref[...])


def _combine(ys, h, route, g2, b2):
    t = h.shape[0]
    cap = ys.shape[0] // (t // TM)
    per_step = next(k for k in (4, 2, 1) if (t // TM) % k == 0)
    rows = per_step * TM
    return pl.pallas_call(
        _combine_kernel,
        grid=(t // rows,),
        in_specs=[
            pl.BlockSpec(memory_space=pl.ANY),
            pl.BlockSpec(memory_space=pl.ANY),
            pl.BlockSpec((rows, LANES), lambda i: (i, 0)),
            pl.BlockSpec((1, D_MODEL), lambda i: (0, 0)),
            pl.BlockSpec((1, D_MODEL), lambda i: (0, 0)),
        ],
        out_specs=pl.BlockSpec((rows, D_MODEL), lambda i: (i, 0)),
        out_shape=jax.ShapeDtypeStruct((t, D_MODEL), F32),
        scratch_shapes=[pltpu.VMEM((COMBINE_SLOTS, per_step * cap, D_MODEL), BF16),
                        pltpu.VMEM((COMBINE_SLOTS, rows, D_MODEL), F32),
                        pltpu.SemaphoreType.DMA((2, COMBINE_SLOTS))],
        compiler_params=pltpu.CompilerParams(dimension_semantics=("arbitrary",), vmem_limit_bytes=VMEM_LIMIT),
        name="moe_combine",
    )(ys, h, route, g2, b2)


def _select(table, idx):
    hot = idx[:, None] == jnp.arange(table.shape[0], dtype=jnp.int32)[None, :]
    return jnp.sum(jnp.where(hot[:, :, None], table[None, :, :], 0), axis=1)


def _piece_tables(counts):
    n_chunks = counts.shape[0]
    n_tiles = -(-(n_chunks * MAX_CHUNK_PIECES + N_EXPERTS * (PIECES_PER_TILE - 1)) // PIECES_PER_TILE)
    npc = (counts + (PIECE - 1)) // PIECE
    first = (jnp.cumsum(npc, axis=1) - npc).T
    npc_t = npc.T
    cum = jnp.cumsum(npc_t, axis=1)
    per_expert = cum[:, -1]
    tiles_e = (per_expert + (PIECES_PER_TILE - 1)) // PIECES_PER_TILE
    tile_end = jnp.cumsum(tiles_e)
    tile_idx = jnp.arange(n_tiles, dtype=jnp.int32)
    expert_of = lambda i: jnp.minimum(jnp.sum(i[:, None] >= tile_end[None, :], axis=1), N_EXPERTS - 1).astype(jnp.int32)
    active = tile_idx < tile_end[-1]
    tile_expert = jnp.where(active, expert_of(tile_idx), expert_of(tile_end[-1:] - 1))
    meta = jnp.stack([tile_end - tiles_e, per_expert], axis=1)
    meta_t = _select(meta, tile_expert)
    k = (tile_idx - meta_t[:, 0])[:, None] * PIECES_PER_TILE + jnp.arange(PIECES_PER_TILE, dtype=jnp.int32)[None, :]
    valid = active[:, None] & (k < meta_t[:, 1:2])
    cum_t = _select(cum, tile_expert)
    chunk = jnp.minimum(jnp.sum(k[:, :, None] >= cum_t[:, None, :], axis=2), n_chunks - 1).astype(jnp.int32)
    at_chunk = chunk[:, :, None] == jnp.arange(n_chunks, dtype=jnp.int32)[None, None, :]
    pick = lambda tab: jnp.sum(jnp.where(at_chunk, _select(tab, tile_expert)[:, None, :], 0), axis=2)
    piece = pick(first) + k - pick(cum - npc_t)
    cap = _chunk_rows(n_chunks)
    rows = chunk * cap + piece * PIECE
    d = ((tile_idx % 2)[:, None] * (PIECES_PER_TILE - 1)
         + jnp.maximum(jnp.arange(PIECES_PER_TILE, dtype=jnp.int32)[None, :] - 1, 0))
    spare_row = lambda d: (d % n_chunks) * cap + (MAX_CHUNK_PIECES + d // n_chunks) * PIECE
    spare = spare_row(d)
    zero_piece = spare_row(N_SPARE - 1)
    extra = jnp.full(((IN_SLOTS - 1) * PIECES_PER_TILE,), zero_piece, jnp.int32)
    src = jnp.concatenate([jnp.where(valid, rows, zero_piece).astype(jnp.int32).reshape(-1), extra])
    dst = jnp.concatenate([jnp.where(valid, rows, spare).astype(jnp.int32).reshape(-1), extra])
    return src, dst, tile_expert.astype(jnp.int32), active.astype(jnp.int32)


def _sample_attn_kernel(x_ref, rope_ref, wq_ref, wrest_ref, sink_ref, ck_ref, cv_ref, cmk_ref, cmv_ref,
                        z_ref, oa_ref, oc_ref, nk_ref, nv_ref, knew_t, vnew_t):
    j = pl.program_id(0)
    db = x_ref.shape[0]

    @pl.when(j == 0)
    def _():
        xb = x_ref[...].astype(BF16)
        _project_q(xb, wq_ref, z_ref)
        _project_rest(xb, wrest_ref, z_ref)
        c = rope_ref[0]
        s1 = rope_ref[1]
        s2 = rope_ref[2]
        for jj in range((Q_WIDTH + KV_WIDTH) // LANES):
            sl = slice(jj * LANES, (jj + 1) * LANES)
            z_ref[:, sl] = _rope(z_ref[:, sl], c, s1, s2)
        knew_t[...] = z_ref[:, K0:K0 + KV_WIDTH].T
        vnew_t[...] = z_ref[:, V0:V0 + KV_WIDTH].T

    r0 = pl.multiple_of(j * SB, SB)
    zq = z_ref[pl.ds(r0, SB), Q0:Q0 + Q_WIDTH]
    zk = z_ref[pl.ds(r0, SB), K0:K0 + KV_WIDTH]
    zv = z_ref[pl.ds(r0, SB), V0:V0 + KV_WIDTH]
    zc = z_ref[pl.ds(r0, SB), CQ0:CQ0 + MEM_WIDTH] * ATT_SCALE
    sink = sink_ref[:, 0:1]
    row_kv = lax.broadcasted_iota(jnp.int32, (N_HEADS, KV_WIDTH), 0) & (N_KV - 1)
    lane_kv = lax.broadcasted_iota(jnp.int32, (N_HEADS, KV_WIDTH), 1) // HEAD_DIM
    own = row_kv == lane_kv
    row_c = lax.broadcasted_iota(jnp.int32, (N_HEADS, MEM_WIDTH), 0)
    lane_c = lax.broadcasted_iota(jnp.int32, (N_HEADS, MEM_WIDTH), 1) // HEAD_DIM
    own_c = row_c == lane_c
    last_pos = lax.broadcasted_iota(jnp.int32, (KV_WIDTH, WINDOW), 1) == WINDOW - 1
    seq_lane = lax.broadcasted_iota(jnp.int32, (KV_WIDTH, db), 1)

    seqs = range(SB)
    qblk, cblk, s, sc = [], [], [], []
    for b in seqs:
        q4 = jnp.concatenate(
            [jnp.broadcast_to(zq[b:b + 1, g * KV_WIDTH:(g + 1) * KV_WIDTH], (N_KV, KV_WIDTH)) for g in range(GROUP)],
            axis=0)
        qblk.append(jnp.where(own, q4, 0.0).astype(BF16))
        cblk.append(jnp.where(own_c, jnp.broadcast_to(zc[b:b + 1, :], (N_HEADS, MEM_WIDTH)), 0.0).astype(BF16))
        s.append(_dot(qblk[b], ck_ref[b].astype(BF16)))
        sc.append(_dot(cblk[b], cmk_ref[b].astype(BF16)))

    pn, p_new, pc = [], [], []
    for b in seqs:
        s_new = jnp.sum(qblk[b].astype(F32) * zk[b:b + 1, :].astype(BF16).astype(F32), axis=1, keepdims=True)
        m = jnp.maximum(jnp.maximum(jnp.max(s[b], axis=1, keepdims=True), s_new), sink)
        p = jnp.exp(s[b] - m)
        e_new = jnp.exp(s_new - m)
        inv = 1.0 / (jnp.sum(p, axis=1, keepdims=True) + e_new + jnp.exp(sink - m))
        pn.append((p * inv).astype(BF16))
        p_new.append((e_new * inv).astype(BF16).astype(F32))
        e = jnp.exp(sc[b] - jnp.max(sc[b], axis=1, keepdims=True))
        pc.append((e * (1.0 / jnp.sum(e, axis=1, keepdims=True))).astype(BF16))

    oa_rows = [[] for _ in range(GROUP)]
    oc_rows = []
    for b in seqs:
        vc = cv_ref[b]
        o = _dot_nt(pn[b], vc.astype(BF16)) + p_new[b] * zv[b:b + 1, :].astype(BF16).astype(F32)
        o = jnp.where(own, o, 0.0)
        for g in range(GROUP):
            oa_rows[g].append(jnp.sum(o[g * N_KV:(g + 1) * N_KV], axis=0, keepdims=True))
        ocb = jnp.where(own_c, _dot_nt(pc[b], cmv_ref[b].astype(BF16)), 0.0)
        oc_rows.append(jnp.sum(ocb, axis=0, keepdims=True))
        here = seq_lane == j * SB + b
        knew_col = jnp.sum(jnp.where(here, knew_t[...], 0.0), axis=1, keepdims=True)
        vnew_col = jnp.sum(jnp.where(here, vnew_t[...], 0.0), axis=1, keepdims=True)
        nk_ref[b] = jnp.where(last_pos, knew_col, pltpu.roll(ck_ref[b], WINDOW - 1, 1))
        nv_ref[b] = jnp.where(last_pos, vnew_col, pltpu.roll(vc, WINDOW - 1, 1))
    for g in range(GROUP):
        oa_ref[pl.ds(r0, SB), g * KV_WIDTH:(g + 1) * KV_WIDTH] = jnp.concatenate(oa_rows[g], axis=0)
    oc_ref[pl.ds(r0, SB), :] = jnp.concatenate(oc_rows, axis=0)


def _sample_attn(x, rope, wq, wrest, sink_gk, ck, cv, cmk, cmv):
    db = x.shape[0]
    blk = lambda r: pl.BlockSpec((SB, KV_WIDTH, r), lambda j: (j, 0, 0))
    full = lambda w: pl.BlockSpec((db, w), lambda j: (0, 0))
    return pl.pallas_call(
        _sample_attn_kernel,
        grid=(db // SB,),
        in_specs=[
            full(D_MODEL), _const_spec((3, 1, LANES)),
            _const_spec((D_MODEL, Q_WIDTH)), _const_spec((D_MODEL, IN_WIDTH - K0)),
            _const_spec((N_HEADS, LANES)),
            blk(WINDOW), blk(WINDOW), blk(N_MEM), blk(N_MEM),
        ],
        out_specs=[full(IN_WIDTH), full(Q_WIDTH), full(MEM_WIDTH), blk(WINDOW), blk(WINDOW)],
        out_shape=[
            jax.ShapeDtypeStruct((db, IN_WIDTH), F32),
            jax.ShapeDtypeStruct((db, Q_WIDTH), F32),
            jax.ShapeDtypeStruct((db, MEM_WIDTH), F32),
            jax.ShapeDtypeStruct((db, KV_WIDTH, WINDOW), F32),
            jax.ShapeDtypeStruct((db, KV_WIDTH, WINDOW), F32),
        ],
        scratch_shapes=[pltpu.VMEM((KV_WIDTH, db), F32), pltpu.VMEM((KV_WIDTH, db), F32)],
        compiler_params=pltpu.CompilerParams(dimension_semantics=("arbitrary",), vmem_limit_bytes=VMEM_LIMIT),
        name="sample_attn",
    )(x, rope, wq, wrest, sink_gk, ck, cv, cmk, cmv)


def _sample_tail_kernel(x_ref, z_ref, oa_ref, oc_ref, st_ref, wa_ref, wb_ref, wc_ref, wo_ref, wmix_ref,
                        pscale_ref, g1_ref, b1_ref, wr_ref, br_ref, wg_hbm, wu_hbm, wd_hbm, g2_ref, b2_ref,
                        y_ref, npool_ref, h_sc, comb_sc, acc_sc, wgbuf, wubuf, wdbuf, sem):
    e = pl.program_id(0)
    n = pl.num_programs(0)
    ahead = COMBINE_SLOTS - 1

    def copies(step):
        ring = step % COMBINE_SLOTS
        return [pltpu.make_async_copy(w.at[step], buf.at[ring], sem.at[k, ring])
                for k, (w, buf) in enumerate(((wg_hbm, wgbuf), (wu_hbm, wubuf), (wd_hbm, wdbuf)))]

    @pl.when(e == 0)
    def _():
        for t in range(ahead):
            for cp in copies(t):
                cp.start()

    @pl.when(e + ahead < n)
    def _():
        for cp in copies(e + ahead):
            cp.start()

    @pl.when(e == 0)
    def _():
        u = z_ref[:, U0:U0 + POOL_WIDTH]
        npool_ref[0:POOL_STATE - 1] = st_ref[1:POOL_STATE]
        npool_ref[POOL_STATE - 1] = u
        obs = []
        for g, w in enumerate(POOL_WINDOWS):
            sl = slice(g * POOL_GROUP_DIM, (g + 1) * POOL_GROUP_DIM)
            cur = u[:, sl]
            ws = cur
            for jj in range(1, w):
                ws = ws + st_ref[POOL_STATE - jj, :, sl]
            cnt = float(min(PAST_LEN + 1, w))
            pooled = ws / cnt - cur
            obs.append(_dot(pooled.astype(BF16), wmix_ref[g]) * pscale_ref[:, sl])
        ob = jnp.concatenate(obs, axis=1)
        h = _merge_ln1(x_ref[...], oa_ref[...].astype(BF16), ob, oc_ref[...], z_ref[:, GZ0:GZ0 + 3 * D_MODEL],
                       wa_ref, wb_ref, wc_ref, wo_ref, g1_ref[...], b1_ref[...])
        h_sc[...] = h
        logits = _dot(h.astype(BF16), wr_ref[...]) + br_ref[...]
        hot1, hot2, w1, w2 = _route(logits)
        comb_sc[...] = jnp.where(hot1, w1, 0.0) + jnp.where(hot2, w2, 0.0)
        acc_sc[...] = jnp.zeros_like(acc_sc)

    for cp in copies(e):
        cp.wait()
    ring = e % COMBINE_SLOTS
    out = _expert_mlp(h_sc[...].astype(BF16), wgbuf[ring].astype(BF16), wubuf[ring].astype(BF16),
                      wdbuf[ring].astype(BF16))
    lane = lax.broadcasted_iota(jnp.int32, comb_sc.shape, 1)
    ce = jnp.sum(jnp.where(lane == e, comb_sc[...], 0.0), axis=1, keepdims=True)
    acc_sc[...] += ce * out

    @pl.when(e == pl.num_programs(0) - 1)
    def _():
        y_ref[...] = _layer_norm(ALPHA * h_sc[...] + acc_sc[...], g2_ref[...], b2_ref[...])


def _sample_tail(x, z, oa, oc, state, wa, wb, wc, wo, wmix, pscale, g1, b1, wr, br, wg, wu, wd, g2, b2):
    db = x.shape[0]
    full = lambda w: pl.BlockSpec((db, w), lambda e: (0, 0))
    vec = lambda w: pl.BlockSpec((1, w), lambda e: (0, 0))
    hist = pl.BlockSpec((POOL_STATE, db, POOL_WIDTH), lambda e: (0, 0, 0))
    return pl.pallas_call(
        _sample_tail_kernel,
        grid=(N_EXPERTS,),
        in_specs=[
            full(D_MODEL), full(IN_WIDTH), full(Q_WIDTH), full(MEM_WIDTH), hist,
            _const_spec((Q_WIDTH, D_MODEL)), _const_spec((POOL_WIDTH, D_MODEL)),
            _const_spec((MEM_WIDTH, D_MODEL)), _const_spec((D_MODEL, D_MODEL)),
            _const_spec((len(POOL_WINDOWS), POOL_GROUP_DIM, POOL_GROUP_DIM)),
            vec(POOL_WIDTH), vec(D_MODEL), vec(D_MODEL),
            _const_spec((D_MODEL, LANES)), vec(LANES),
            pl.BlockSpec(memory_space=pl.ANY),
            pl.BlockSpec(memory_space=pl.ANY),
            pl.BlockSpec(memory_space=pl.ANY),
            vec(D_MODEL), vec(D_MODEL),
        ],
        out_specs=[full(D_MODEL), hist],
        out_shape=[jax.ShapeDtypeStruct((db, D_MODEL), F32),
                   jax.ShapeDtypeStruct((POOL_STATE, db, POOL_WIDTH), F32)],
        scratch_shapes=[pltpu.VMEM((db, D_MODEL), F32), pltpu.VMEM((db, LANES), F32),
                        pltpu.VMEM((db, D_MODEL), F32),
                        pltpu.VMEM((COMBINE_SLOTS, D_MODEL, D_EXPERT), F32),
                        pltpu.VMEM((COMBINE_SLOTS, D_MODEL, D_EXPERT), F32),
                        pltpu.VMEM((COMBINE_SLOTS, D_EXPERT, D_MODEL), F32),
                        pltpu.SemaphoreType.DMA((3, COMBINE_SLOTS))],
        compiler_params=pltpu.CompilerParams(dimension_semantics=("arbitrary",), vmem_limit_bytes=VMEM_LIMIT),
        name="sample_tail",
    )(x, z, oa, oc, state, wa, wb, wc, wo, wmix, pscale, g1, b1, wr, br, wg, wu, wd, g2, b2)


def _rope_tables(pos):
    half = ROPE_DIM // 2
    inv = jnp.power(ROPE_THETA, -jnp.arange(half, dtype=F32) * (2.0 / ROPE_DIM))
    ang = pos.astype(F32)[:, None] * jnp.tile(inv, LANES // half)[None, :]
    off = lax.broadcasted_iota(jnp.int32, (1, LANES), 1) % HEAD_DIM
    cos = jnp.cos(ang)
    sin = jnp.sin(ang)
    c = jnp.where(off < ROPE_DIM, cos, 1.0)
    s1 = jnp.where((off >= half) & (off < ROPE_DIM), sin, 0.0)
    s2 = jnp.where(off < half, -sin, 0.0)
    return jnp.stack([c, s1, s2])


def _q_heads_group_major(w, axis):
    if axis == 1:
        n = w.shape[0]
        return w.reshape(n, N_KV, GROUP, HEAD_DIM).transpose(0, 2, 1, 3).reshape(n, Q_WIDTH)
    n = w.shape[1]
    return w.reshape(N_KV, GROUP, HEAD_DIM, n).transpose(1, 0, 2, 3).reshape(Q_WIDTH, n)


def kernel(x_prompt, x_sample, cache_win_k, cache_win_v, state_pool, cache_mem_k, cache_mem_v, mem_prompt, w_in, sinks, w_pool_mix, pool_scale, w_mem_k, w_mem_v, w_branch_a, w_branch_b, w_branch_c, w_out, ln1_g, ln1_b, w_group, b_group, w_router, b_router, w_gate, w_up, w_down, ln2_g, ln2_b):
    assert w_in.shape[0] == DEPTH == 1
    b, l, _ = x_prompt.shape
    db, ds, _ = x_sample.shape
    assert ds == 1 and l % TM == 0 and db % SB == 0
    assert cache_win_k.shape[2] == WINDOW
    t = b * l

    win = w_in[0]
    wq = (_q_heads_group_major(win[:, Q0:Q0 + Q_WIDTH], 1) * ATT_SCALE).astype(BF16)
    wrest = win[:, K0:].astype(BF16)
    wa = _q_heads_group_major(w_branch_a[0], 0).astype(BF16)
    wb = w_branch_b[0].astype(BF16)
    wc = w_branch_c[0].astype(BF16)
    wo = w_out[0].astype(BF16)
    wmix = w_pool_mix[0].astype(BF16)
    pscale = pool_scale[0].reshape(1, POOL_WIDTH)
    g1 = ln1_g[0].reshape(1, D_MODEL)
    b1 = ln1_b[0].reshape(1, D_MODEL)
    g2 = ln2_g[0].reshape(1, D_MODEL)
    b2 = ln2_b[0].reshape(1, D_MODEL)
    wr = jnp.concatenate([w_group[0], w_router[0].reshape(D_MODEL, N_EXPERTS)], axis=1)
    wr = jnp.pad(wr, ((0, 0), (0, LANES - wr.shape[1]))).astype(BF16)
    br = jnp.pad(jnp.concatenate([b_group[0], b_router[0].reshape(N_EXPERTS)]), (0, LANES - N_EXPERT_GROUPS - N_EXPERTS))
    br = br.reshape(1, LANES).astype(F32)
    wg = w_gate[0]
    wu = w_up[0]
    wd = w_down[0]
    sink = sinks[0].astype(F32)
    sink_gk = jnp.broadcast_to(sink.reshape(N_KV, GROUP).T.reshape(N_HEADS, 1), (N_HEADS, LANES))

    mk, mv = _mem_project(mem_prompt, w_mem_k[0].astype(BF16), w_mem_v[0].astype(BF16))
    rope_p = _rope_tables(jnp.arange(l, dtype=jnp.int32))
    h, xs, route, counts, nk_p, nv_p, npool_p = _front(
        x_prompt, rope_p, sink, wq, wrest, wa, wb, wc, wo, wmix, pscale, mk, mv, g1, b1, wr, br)
    piece_src, piece_dst, tile_expert, tile_active = _piece_tables(
        counts.reshape(-1, LANES, LANES)[:, :N_EXPERTS, 0].astype(jnp.int32))
    ys = _grouped_gemm(piece_src, piece_dst, tile_expert, tile_active, xs, wg, wu, wd)
    y_p = _combine(ys, h.reshape(t, D_MODEL), route.reshape(t, LANES), g2, b2).reshape(b, l, D_MODEL)

    rope_s = _rope_tables(jnp.full((1,), PAST_LEN, jnp.int32))
    xs = x_sample.reshape(db, D_MODEL)
    feat_major = lambda c: jnp.transpose(c[0], (0, 2, 3, 1)).reshape(db, KV_WIDTH, c.shape[2])
    from_feat_major = lambda a: jnp.transpose(a.reshape(db, N_KV, HEAD_DIM, a.shape[2]), (0, 3, 1, 2))[None]
    z_s, oa_s, oc_s, nk_s, nv_s = _sample_attn(xs, rope_s, wq, wrest, sink_gk, feat_major(cache_win_k),
                                               feat_major(cache_win_v), feat_major(cache_mem_k), feat_major(cache_mem_v))
    state = jnp.transpose(state_pool[0], (1, 0, 2))
    y_s, npool_s = _sample_tail(xs, z_s, oa_s, oc_s, state, wa, wb, wc, wo, wmix, pscale, g1, b1, wr, br,
                                wg, wu, wd, g2, b2)

    kv5 = lambda a, n, w: a.reshape(1, n, w, N_KV, HEAD_DIM)
    return (y_p, y_s.reshape(db, 1, D_MODEL),
            kv5(nk_p, b, QB), kv5(nv_p, b, QB),
            npool_p[:, 2 * SUBLANES - POOL_STATE:, :][None],
            kv5(mk, b, N_MEM), kv5(mv, b, N_MEM),
            from_feat_major(nk_s), from_feat_major(nv_s),
            jnp.transpose(npool_s, (1, 0, 2))[None])
```
